```python
import jax, jax.numpy as jnp
from jax import lax
import numpy as np

D_MODEL = 1024
BATCH = 8
SEQ = 2048
DEPTH = 2

BRANCH_W = D_MODEL // 2
LRU_W = BRANCH_W
LRU_BLOCKS = 8
LRU_BLOCK_W = LRU_W // LRU_BLOCKS
LRU_CONV = 4
LRU_C = 8.0
SC_W = BRANCH_W
SC_CONV = 3
HEAD_DIM = 64
N_Q_HEADS = BRANCH_W // HEAD_DIM
N_KV_HEADS = 2
GQA_GROUP = N_Q_HEADS // N_KV_HEADS
WINDOW = 128
BLOCK = 128
CF_W = BRANCH_W
CF_CONV = 31
N_BRANCHES = 4
D_FF = -(-8 * D_MODEL // (3 * 256)) * 256
EPS = 1e-6
NEG_INF = -1e30

SPLIT_SIZES = (
    LRU_W, LRU_W,
    SC_W, SC_W, SC_W,
    N_Q_HEADS * HEAD_DIM,
    N_KV_HEADS * HEAD_DIM,
    N_KV_HEADS * HEAD_DIM,
    2 * CF_W,
    N_BRANCHES * D_MODEL,
)
IN_W = int(sum(SPLIT_SIZES))
SPLIT_POINTS = tuple(int(v) for v in np.cumsum(SPLIT_SIZES)[:-1])

kernel_name = "hybrid_rglru_shortconv_swa_conformer"


def rms_norm(x, g):
    xf = x.astype(jnp.float32)
    y = xf * lax.rsqrt(jnp.mean(xf * xf, axis=-1, keepdims=True) + EPS)
    return (y * g.astype(jnp.float32)).astype(x.dtype)


def layer_norm(x, g, b):
    xf = x.astype(jnp.float32)
    mu = jnp.mean(xf, axis=-1, keepdims=True)
    var = jnp.mean(jnp.square(xf - mu), axis=-1, keepdims=True)
    y = (xf - mu) * lax.rsqrt(var + EPS)
    return (y * g.astype(jnp.float32) + b.astype(jnp.float32)).astype(x.dtype)


def causal_dwconv(x, w, b=None):
    k, c = w.shape
    y = lax.conv_general_dilated(
        x, w[:, None, :].astype(x.dtype), window_strides=(1,), padding=[(k - 1, 0)],
        dimension_numbers=("NWC", "WIO", "NWC"), feature_group_count=c)
    if b is not None:
        y = y + b
    return y


def rg_lru(x, wx, bx, wa, ba, lam):
    b, s, w = x.shape
    xb = x.reshape(b, s, LRU_BLOCKS, LRU_BLOCK_W)
    gate_i = jax.nn.sigmoid(jnp.einsum("bshi,hij->bshj", xb, wx).reshape(b, s, w) + bx)
    gate_r = jax.nn.sigmoid(jnp.einsum("bshi,hij->bshj", xb, wa).reshape(b, s, w) + ba)
    log_a = -LRU_C * gate_r.astype(jnp.float32) * jax.nn.softplus(-lam.astype(jnp.float32))
    a = jnp.exp(log_a)
    mult = jnp.sqrt(-jnp.expm1(2.0 * log_a))
    u = (x * gate_i).astype(jnp.float32) * mult

    def combine(left, right):
        a_l, b_l = left
        a_r, b_r = right
        return a_l * a_r, a_r * b_l + b_r

    _, h = lax.associative_scan(combine, (a, u), axis=1)
    return h.astype(x.dtype)


def alibi_slopes(n):
    return jnp.asarray([2.0 ** (-8.0 * (i + 1) / n) for i in range(n)], dtype=jnp.float32)


def sliding_window_attention(q, k, v, sinks):
    b, s, _ = q.shape
    nb = s // BLOCK
    q = q.reshape(b, nb, BLOCK, N_KV_HEADS, GQA_GROUP, HEAD_DIM)
    k = k.reshape(b, nb, BLOCK, N_KV_HEADS, HEAD_DIM)
    v = v.reshape(b, nb, BLOCK, N_KV_HEADS, HEAD_DIM)
    zero = jnp.zeros_like(k[:, :1])
    k2 = jnp.concatenate([jnp.concatenate([zero, k[:, :-1]], axis=1), k], axis=2)
    v2 = jnp.concatenate([jnp.concatenate([zero, v[:, :-1]], axis=1), v], axis=2)
    scores = jnp.einsum("bnqhgd,bnkhd->bnhgqk", q, k2).astype(jnp.float32) * (HEAD_DIM ** -0.5)
    qi = jnp.arange(BLOCK)[:, None]
    ki = jnp.arange(2 * BLOCK)[None, :]
    dist = qi + BLOCK - ki
    key_pos = (jnp.arange(nb)[:, None, None] - 1) * BLOCK + ki[None]
    valid = (dist >= 0)[None] & (dist < WINDOW)[None] & (key_pos >= 0)
    slopes = alibi_slopes(N_Q_HEADS).reshape(N_KV_HEADS, GQA_GROUP)
    scores = scores - slopes[:, :, None, None] * dist.astype(jnp.float32)
    scores = jnp.where(valid[None, :, None, None], scores, NEG_INF)
    sink = jnp.broadcast_to(
        sinks.astype(jnp.float32).reshape(1, 1, N_KV_HEADS, GQA_GROUP, 1, 1),
        scores.shape[:-1] + (1,))
    probs = jax.nn.softmax(jnp.concatenate([scores, sink], axis=-1), axis=-1)[..., :-1]
    out = jnp.einsum("bnhgqk,bnkhd->bnqhgd", probs.astype(v2.dtype), v2)
    return out.reshape(b, s, N_Q_HEADS * HEAD_DIM)


def hybrid_mixer(xn, w_in, conv_a_w, conv_a_b, lru_wx, lru_bx, lru_wa, lru_ba, lru_lambda,
                 w_a_out, conv_b_w, w_b_out, sinks, w_c_out, conv_d_w, conv_d_b,
                 ln_d_g, ln_d_b, w_d_out, w_o):
    b, s, _ = xn.shape
    proj = xn @ w_in
    a_x, a_gate, b_v, b_c, b_b, q, k, v, d_in, gate_logits = jnp.split(proj, SPLIT_POINTS, axis=-1)
    a_h = rg_lru(causal_dwconv(a_x, conv_a_w, conv_a_b), lru_wx, lru_bx, lru_wa, lru_ba, lru_lambda)
    y_a = (a_h * jax.nn.gelu(a_gate)) @ w_a_out
    y_b = (b_b * causal_dwconv(b_c * b_v, conv_b_w)) @ w_b_out
    y_c = sliding_window_attention(q, k, v, sinks) @ w_c_out
    d = d_in[..., :CF_W] * jax.nn.sigmoid(d_in[..., CF_W:])
    d = jax.nn.silu(layer_norm(causal_dwconv(d, conv_d_w, conv_d_b), ln_d_g, ln_d_b))
    y_d = d @ w_d_out
    g = jax.nn.sigmoid(gate_logits).reshape(b, s, N_BRANCHES, D_MODEL)
    merged = g[:, :, 0] * y_a + g[:, :, 1] * y_b + g[:, :, 2] * y_c + g[:, :, 3] * y_d
    return merged @ w_o


def swiglu(x, w_gate, w_up, w_down):
    return (jax.nn.silu(x @ w_gate) * (x @ w_up)) @ w_down


def _fwd_setup_inputs(seed: int = 0) -> dict:
    key = jax.random.key(seed)
    ks = jax.random.split(key, 32)
    f32 = jnp.float32

    def nrm(k, shape, scale):
        return jax.random.normal(k, shape, f32) * scale

    L = DEPTH
    target = jax.random.uniform(ks[9], (L, LRU_W), f32, 0.9, 0.999)
    sig = target ** (1.0 / LRU_C)
    lru_lambda = jnp.log(sig) - jnp.log1p(-sig)
    return {
        "x": nrm(ks[0], (BATCH, SEQ, D_MODEL), 1.0),
        "norm1_g": 1.0 + nrm(ks[1], (L, D_MODEL), 0.02),
        "w_in": nrm(ks[2], (L, D_MODEL, IN_W), D_MODEL ** -0.5),
        "conv_a_w": nrm(ks[3], (L, LRU_CONV, LRU_W), LRU_CONV ** -0.5),
        "conv_a_b": nrm(ks[4], (L, LRU_W), 0.02),
        "lru_wx": nrm(ks[5], (L, LRU_BLOCKS, LRU_BLOCK_W, LRU_BLOCK_W), LRU_BLOCK_W ** -0.5),
        "lru_bx": nrm(ks[6], (L, LRU_W), 0.02),
        "lru_wa": nrm(ks[7], (L, LRU_BLOCKS, LRU_BLOCK_W, LRU_BLOCK_W), LRU_BLOCK_W ** -0.5),
        "lru_ba": nrm(ks[8], (L, LRU_W), 0.02),
        "lru_lambda": lru_lambda,
        "w_a_out": nrm(ks[10], (L, LRU_W, D_MODEL), LRU_W ** -0.5),
        "conv_b_w": nrm(ks[11], (L, SC_CONV, SC_W), SC_CONV ** -0.5),
        "w_b_out": nrm(ks[12], (L, SC_W, D_MODEL), SC_W ** -0.5),
        "sinks": nrm(ks[13], (L, N_Q_HEADS), 0.5),
        "w_c_out": nrm(ks[14], (L, N_Q_HEADS * HEAD_DIM, D_MODEL), (N_Q_HEADS * HEAD_DIM) ** -0.5),
        "conv_d_w": nrm(ks[15], (L, CF_CONV, CF_W), CF_CONV ** -0.5),
        "conv_d_b": nrm(ks[16], (L, CF_W), 0.02),
        "ln_d_g": 1.0 + nrm(ks[17], (L, CF_W), 0.02),
        "ln_d_b": nrm(ks[18], (L, CF_W), 0.02),
        "w_d_out": nrm(ks[19], (L, CF_W, D_MODEL), CF_W ** -0.5),
        "w_o": nrm(ks[20], (L, D_MODEL, D_MODEL), D_MODEL ** -0.5),
        "norm2_g": 1.0 + nrm(ks[21], (L, D_MODEL), 0.02),
        "w_ffn_gate": nrm(ks[22], (L, D_MODEL, D_FF), D_MODEL ** -0.5),
        "w_ffn_up": nrm(ks[23], (L, D_MODEL, D_FF), D_MODEL ** -0.5),
        "w_ffn_down": nrm(ks[24], (L, D_FF, D_MODEL), D_FF ** -0.5),
        "final_g": 1.0 + nrm(ks[25], (D_MODEL,), 0.02),
    }


def _fwd_reference(x, norm1_g, w_in, conv_a_w, conv_a_b, lru_wx, lru_bx, lru_wa, lru_ba, lru_lambda,
              w_a_out, conv_b_w, w_b_out, sinks, w_c_out, conv_d_w, conv_d_b, ln_d_g, ln_d_b,
              w_d_out, w_o, norm2_g, w_ffn_gate, w_ffn_up, w_ffn_down, final_g):
    for l in range(DEPTH):
        xn = rms_norm(x, norm1_g[l])
        x = x + hybrid_mixer(xn, w_in[l], conv_a_w[l], conv_a_b[l], lru_wx[l], lru_bx[l],
                             lru_wa[l], lru_ba[l], lru_lambda[l], w_a_out[l], conv_b_w[l],
                             w_b_out[l], sinks[l], w_c_out[l], conv_d_w[l], conv_d_b[l],
                             ln_d_g[l], ln_d_b[l], w_d_out[l], w_o[l])
        x = x + swiglu(rms_norm(x, norm2_g[l]), w_ffn_gate[l], w_ffn_up[l], w_ffn_down[l])
    return rms_norm(x, final_g)


import jax as _jax
import jax.numpy as _jnp

TWIN_FORMAT = 'train_step'
FWD_PARAMS = ['x', 'norm1_g', 'w_in', 'conv_a_w', 'conv_a_b', 'lru_wx', 'lru_bx', 'lru_wa', 'lru_ba', 'lru_lambda', 'w_a_out', 'conv_b_w', 'w_b_out', 'sinks', 'w_c_out', 'conv_d_w', 'conv_d_b', 'ln_d_g', 'ln_d_b', 'w_d_out', 'w_o', 'norm2_g', 'w_ffn_gate', 'w_ffn_up', 'w_ffn_down', 'final_g']
TWIN_WEIGHTS = ['norm1_g', 'w_in', 'conv_a_w', 'conv_a_b', 'lru_wx', 'lru_bx', 'lru_wa', 'lru_ba', 'lru_lambda', 'w_a_out', 'conv_b_w', 'w_b_out', 'sinks', 'w_c_out', 'conv_d_w', 'conv_d_b', 'ln_d_g', 'ln_d_b', 'w_d_out', 'w_o', 'norm2_g', 'w_ffn_gate', 'w_ffn_up', 'w_ffn_down', 'final_g']
TWIN_DIFF_INPUT = 'x'
TWIN_INPUTS = ['x', 'norm1_g', 'w_in', 'conv_a_w', 'conv_a_b', 'lru_wx', 'lru_bx', 'lru_wa', 'lru_ba', 'lru_lambda', 'w_a_out', 'conv_b_w', 'w_b_out', 'sinks', 'w_c_out', 'conv_d_w', 'conv_d_b', 'ln_d_g', 'ln_d_b', 'w_d_out', 'w_o', 'norm2_g', 'w_ffn_gate', 'w_ffn_up', 'w_ffn_down', 'final_g', 'loss_target', 'm_norm1_g', 'm_w_in', 'm_conv_a_w', 'm_conv_a_b', 'm_lru_wx', 'm_lru_bx', 'm_lru_wa', 'm_lru_ba', 'm_lru_lambda', 'm_w_a_out', 'm_conv_b_w', 'm_w_b_out', 'm_sinks', 'm_w_c_out', 'm_conv_d_w', 'm_conv_d_b', 'm_ln_d_g', 'm_ln_d_b', 'm_w_d_out', 'm_w_o', 'm_norm2_g', 'm_w_ffn_gate', 'm_w_ffn_up', 'm_w_ffn_down', 'm_final_g', 'v_norm1_g', 'v_w_in', 'v_conv_a_w', 'v_conv_a_b', 'v_lru_wx', 'v_lru_bx', 'v_lru_wa', 'v_lru_ba', 'v_lru_lambda', 'v_w_a_out', 'v_conv_b_w', 'v_w_b_out', 'v_sinks', 'v_w_c_out', 'v_conv_d_w', 'v_conv_d_b', 'v_ln_d_g', 'v_ln_d_b', 'v_w_d_out', 'v_w_o', 'v_norm2_g', 'v_w_ffn_gate', 'v_w_ffn_up', 'v_w_ffn_down', 'v_final_g']
TWIN_OUTPUTS = ['loss', 'grad_x', 'grad_norm1_g', 'grad_w_in', 'grad_conv_a_w', 'grad_conv_a_b', 'grad_lru_wx', 'grad_lru_bx', 'grad_lru_wa', 'grad_lru_ba', 'grad_lru_lambda', 'grad_w_a_out', 'grad_conv_b_w', 'grad_w_b_out', 'grad_sinks', 'grad_w_c_out', 'grad_conv_d_w', 'grad_conv_d_b', 'grad_ln_d_g', 'grad_ln_d_b', 'grad_w_d_out', 'grad_w_o', 'grad_norm2_g', 'grad_w_ffn_gate', 'grad_w_ffn_up', 'grad_w_ffn_down', 'grad_final_g', 'delta_norm1_g', 'delta_w_in', 'delta_conv_a_w', 'delta_conv_a_b', 'delta_lru_wx', 'delta_lru_bx', 'delta_lru_wa', 'delta_lru_ba', 'delta_lru_lambda', 'delta_w_a_out', 'delta_conv_b_w', 'delta_w_b_out', 'delta_sinks', 'delta_w_c_out', 'delta_conv_d_w', 'delta_conv_d_b', 'delta_ln_d_g', 'delta_ln_d_b', 'delta_w_d_out', 'delta_w_o', 'delta_norm2_g', 'delta_w_ffn_gate', 'delta_w_ffn_up', 'delta_w_ffn_down', 'delta_final_g', 'new_m_norm1_g', 'new_m_w_in', 'new_m_conv_a_w', 'new_m_conv_a_b', 'new_m_lru_wx', 'new_m_lru_bx', 'new_m_lru_wa', 'new_m_lru_ba', 'new_m_lru_lambda', 'new_m_w_a_out', 'new_m_conv_b_w', 'new_m_w_b_out', 'new_m_sinks', 'new_m_w_c_out', 'new_m_conv_d_w', 'new_m_conv_d_b', 'new_m_ln_d_g', 'new_m_ln_d_b', 'new_m_w_d_out', 'new_m_w_o', 'new_m_norm2_g', 'new_m_w_ffn_gate', 'new_m_w_ffn_up', 'new_m_w_ffn_down', 'new_m_final_g', 'new_v_norm1_g', 'new_v_w_in', 'new_v_conv_a_w', 'new_v_conv_a_b', 'new_v_lru_wx', 'new_v_lru_bx', 'new_v_lru_wa', 'new_v_lru_ba', 'new_v_lru_lambda', 'new_v_w_a_out', 'new_v_conv_b_w', 'new_v_w_b_out', 'new_v_sinks', 'new_v_w_c_out', 'new_v_conv_d_w', 'new_v_conv_d_b', 'new_v_ln_d_g', 'new_v_ln_d_b', 'new_v_w_d_out', 'new_v_w_o', 'new_v_norm2_g', 'new_v_w_ffn_gate', 'new_v_w_ffn_up', 'new_v_w_ffn_down', 'new_v_final_g']
TWIN_LEAF_KINDS = {'loss': 'loss', 'grad_x': 'grad_x', 'grad_norm1_g': 'grad_w', 'grad_w_in': 'grad_w', 'grad_conv_a_w': 'grad_w', 'grad_conv_a_b': 'grad_w', 'grad_lru_wx': 'grad_w', 'grad_lru_bx': 'grad_w', 'grad_lru_wa': 'grad_w', 'grad_lru_ba': 'grad_w', 'grad_lru_lambda': 'grad_w', 'grad_w_a_out': 'grad_w', 'grad_conv_b_w': 'grad_w', 'grad_w_b_out': 'grad_w', 'grad_sinks': 'grad_w', 'grad_w_c_out': 'grad_w', 'grad_conv_d_w': 'grad_w', 'grad_conv_d_b': 'grad_w', 'grad_ln_d_g': 'grad_w', 'grad_ln_d_b': 'grad_w', 'grad_w_d_out': 'grad_w', 'grad_w_o': 'grad_w', 'grad_norm2_g': 'grad_w', 'grad_w_ffn_gate': 'grad_w', 'grad_w_ffn_up': 'grad_w', 'grad_w_ffn_down': 'grad_w', 'grad_final_g': 'grad_w', 'delta_norm1_g': 'delta_w', 'delta_w_in': 'delta_w', 'delta_conv_a_w': 'delta_w', 'delta_conv_a_b': 'delta_w', 'delta_lru_wx': 'delta_w', 'delta_lru_bx': 'delta_w', 'delta_lru_wa': 'delta_w', 'delta_lru_ba': 'delta_w', 'delta_lru_lambda': 'delta_w', 'delta_w_a_out': 'delta_w', 'delta_conv_b_w': 'delta_w', 'delta_w_b_out': 'delta_w', 'delta_sinks': 'delta_w', 'delta_w_c_out': 'delta_w', 'delta_conv_d_w': 'delta_w', 'delta_conv_d_b': 'delta_w', 'delta_ln_d_g': 'delta_w', 'delta_ln_d_b': 'delta_w', 'delta_w_d_out': 'delta_w', 'delta_w_o': 'delta_w', 'delta_norm2_g': 'delta_w', 'delta_w_ffn_gate': 'delta_w', 'delta_w_ffn_up': 'delta_w', 'delta_w_ffn_down': 'delta_w', 'delta_final_g': 'delta_w', 'new_m_norm1_g': 'new_m', 'new_m_w_in': 'new_m', 'new_m_conv_a_w': 'new_m', 'new_m_conv_a_b': 'new_m', 'new_m_lru_wx': 'new_m', 'new_m_lru_bx': 'new_m', 'new_m_lru_wa': 'new_m', 'new_m_lru_ba': 'new_m', 'new_m_lru_lambda': 'new_m', 'new_m_w_a_out': 'new_m', 'new_m_conv_b_w': 'new_m', 'new_m_w_b_out': 'new_m', 'new_m_sinks': 'new_m', 'new_m_w_c_out': 'new_m', 'new_m_conv_d_w': 'new_m', 'new_m_conv_d_b': 'new_m', 'new_m_ln_d_g': 'new_m', 'new_m_ln_d_b': 'new_m', 'new_m_w_d_out': 'new_m', 'new_m_w_o': 'new_m', 'new_m_norm2_g': 'new_m', 'new_m_w_ffn_gate': 'new_m', 'new_m_w_ffn_up': 'new_m', 'new_m_w_ffn_down': 'new_m', 'new_m_final_g': 'new_m', 'new_v_norm1_g': 'new_v', 'new_v_w_in': 'new_v', 'new_v_conv_a_w': 'new_v', 'new_v_conv_a_b': 'new_v', 'new_v_lru_wx': 'new_v', 'new_v_lru_bx': 'new_v', 'new_v_lru_wa': 'new_v', 'new_v_lru_ba': 'new_v', 'new_v_lru_lambda': 'new_v', 'new_v_w_a_out': 'new_v', 'new_v_conv_b_w': 'new_v', 'new_v_w_b_out': 'new_v', 'new_v_sinks': 'new_v', 'new_v_w_c_out': 'new_v', 'new_v_conv_d_w': 'new_v', 'new_v_conv_d_b': 'new_v', 'new_v_ln_d_g': 'new_v', 'new_v_ln_d_b': 'new_v', 'new_v_w_d_out': 'new_v', 'new_v_w_o': 'new_v', 'new_v_norm2_g': 'new_v', 'new_v_w_ffn_gate': 'new_v', 'new_v_w_ffn_up': 'new_v', 'new_v_w_ffn_down': 'new_v', 'new_v_final_g': 'new_v'}


def _forward(args):
    return _fwd_reference(*[args[k] for k in FWD_PARAMS])


def _output_shape():
    out = _jax.eval_shape(lambda: _forward(_fwd_setup_inputs(0)))
    return out.shape, out.dtype

N_MICROBATCH = 1
ADAM_LR = 0.001
ADAM_B1 = 0.9
ADAM_B2 = 0.999
ADAM_EPS = 1e-08
ADAM_WD = 0.01
ADAM_STEP = 10
PER_EXAMPLE_BATCH_AXIS = {'x': 0, 'loss_target': 0}
SHARED_INPUTS = []
_WEIGHT_DTYPES = {'norm1_g': _jnp.float32, 'w_in': _jnp.float32, 'conv_a_w': _jnp.float32, 'conv_a_b': _jnp.float32, 'lru_wx': _jnp.float32, 'lru_bx': _jnp.float32, 'lru_wa': _jnp.float32, 'lru_ba': _jnp.float32, 'lru_lambda': _jnp.float32, 'w_a_out': _jnp.float32, 'conv_b_w': _jnp.float32, 'w_b_out': _jnp.float32, 'sinks': _jnp.float32, 'w_c_out': _jnp.float32, 'conv_d_w': _jnp.float32, 'conv_d_b': _jnp.float32, 'ln_d_g': _jnp.float32, 'ln_d_b': _jnp.float32, 'w_d_out': _jnp.float32, 'w_o': _jnp.float32, 'norm2_g': _jnp.float32, 'w_ffn_gate': _jnp.float32, 'w_ffn_up': _jnp.float32, 'w_ffn_down': _jnp.float32, 'final_g': _jnp.float32}
MOMENT_SCALE = {'norm1_g': 1.280805e-01, 'w_in': 4.505123e-02, 'conv_a_w': 4.850397e-02, 'conv_a_b': 4.177127e-01, 'lru_wx': 2.713871e-02, 'lru_bx': 1.852618e-02, 'lru_wa': 1.489667e-02, 'lru_ba': 1.130781e-02, 'lru_lambda': 2.187110e-02, 'w_a_out': 3.444001e-02, 'conv_b_w': 8.965964e-02, 'w_b_out': 6.194192e-02, 'sinks': 8.657523e-02, 'w_c_out': 1.959941e-02, 'conv_d_w': 5.591187e-02, 'conv_d_b': 1.353786e-01, 'ln_d_g': 7.700831e-02, 'ln_d_b': 6.967461e-02, 'w_d_out': 3.987356e-02, 'w_o': 8.297118e-02, 'norm2_g': 8.212677e-02, 'w_ffn_gate': 3.633464e-02, 'w_ffn_up': 3.519706e-02, 'w_ffn_down': 5.848797e-02, 'final_g': 1.601824e+01}


def _to_microbatches(a, axis):
    t = _jnp.moveaxis(a, axis, 0)
    t = t.reshape((N_MICROBATCH, t.shape[0] // N_MICROBATCH) + t.shape[1:])
    return _jnp.moveaxis(t, 1, axis + 1)


def setup_inputs(seed: int = 0) -> dict:
    inp = _fwd_setup_inputs(seed)
    key = _jax.random.fold_in(_jax.random.key(seed), 7919)
    shape, _ = _output_shape()
    out = dict(inp)
    out["loss_target"] = _jax.random.normal(_jax.random.fold_in(key, 0), shape, _jnp.float32)
    for i, name in enumerate(TWIN_WEIGHTS):
        w = inp[name].astype(_jnp.float32)
        if MOMENT_SCALE is None:
            s = _jnp.sqrt(_jnp.mean(_jnp.square(w)) + 1e-30)
        else:
            s = MOMENT_SCALE[name]
        km, kv = _jax.random.split(_jax.random.fold_in(key, i + 1))
        out[name] = w
        out["m_" + name] = s * _jax.random.normal(km, w.shape, _jnp.float32)
        out["v_" + name] = (s * s) * _jax.random.uniform(kv, w.shape, _jnp.float32, 0.5, 1.5)
    if N_MICROBATCH > 1:
        for name, axis in PER_EXAMPLE_BATCH_AXIS.items():
            out[name] = _to_microbatches(out[name], axis)
    return {'x': out['x'], 'norm1_g': out['norm1_g'], 'w_in': out['w_in'], 'conv_a_w': out['conv_a_w'], 'conv_a_b': out['conv_a_b'], 'lru_wx': out['lru_wx'], 'lru_bx': out['lru_bx'], 'lru_wa': out['lru_wa'], 'lru_ba': out['lru_ba'], 'lru_lambda': out['lru_lambda'], 'w_a_out': out['w_a_out'], 'conv_b_w': out['conv_b_w'], 'w_b_out': out['w_b_out'], 'sinks': out['sinks'], 'w_c_out': out['w_c_out'], 'conv_d_w': out['conv_d_w'], 'conv_d_b': out['conv_d_b'], 'ln_d_g': out['ln_d_g'], 'ln_d_b': out['ln_d_b'], 'w_d_out': out['w_d_out'], 'w_o': out['w_o'], 'norm2_g': out['norm2_g'], 'w_ffn_gate': out['w_ffn_gate'], 'w_ffn_up': out['w_ffn_up'], 'w_ffn_down': out['w_ffn_down'], 'final_g': out['final_g'], 'loss_target': out['loss_target'], 'm_norm1_g': out['m_norm1_g'], 'm_w_in': out['m_w_in'], 'm_conv_a_w': out['m_conv_a_w'], 'm_conv_a_b': out['m_conv_a_b'], 'm_lru_wx': out['m_lru_wx'], 'm_lru_bx': out['m_lru_bx'], 'm_lru_wa': out['m_lru_wa'], 'm_lru_ba': out['m_lru_ba'], 'm_lru_lambda': out['m_lru_lambda'], 'm_w_a_out': out['m_w_a_out'], 'm_conv_b_w': out['m_conv_b_w'], 'm_w_b_out': out['m_w_b_out'], 'm_sinks': out['m_sinks'], 'm_w_c_out': out['m_w_c_out'], 'm_conv_d_w': out['m_conv_d_w'], 'm_conv_d_b': out['m_conv_d_b'], 'm_ln_d_g': out['m_ln_d_g'], 'm_ln_d_b': out['m_ln_d_b'], 'm_w_d_out': out['m_w_d_out'], 'm_w_o': out['m_w_o'], 'm_norm2_g': out['m_norm2_g'], 'm_w_ffn_gate': out['m_w_ffn_gate'], 'm_w_ffn_up': out['m_w_ffn_up'], 'm_w_ffn_down': out['m_w_ffn_down'], 'm_final_g': out['m_final_g'], 'v_norm1_g': out['v_norm1_g'], 'v_w_in': out['v_w_in'], 'v_conv_a_w': out['v_conv_a_w'], 'v_conv_a_b': out['v_conv_a_b'], 'v_lru_wx': out['v_lru_wx'], 'v_lru_bx': out['v_lru_bx'], 'v_lru_wa': out['v_lru_wa'], 'v_lru_ba': out['v_lru_ba'], 'v_lru_lambda': out['v_lru_lambda'], 'v_w_a_out': out['v_w_a_out'], 'v_conv_b_w': out['v_conv_b_w'], 'v_w_b_out': out['v_w_b_out'], 'v_sinks': out['v_sinks'], 'v_w_c_out': out['v_w_c_out'], 'v_conv_d_w': out['v_conv_d_w'], 'v_conv_d_b': out['v_conv_d_b'], 'v_ln_d_g': out['v_ln_d_g'], 'v_ln_d_b': out['v_ln_d_b'], 'v_w_d_out': out['v_w_d_out'], 'v_w_o': out['v_w_o'], 'v_norm2_g': out['v_norm2_g'], 'v_w_ffn_gate': out['v_w_ffn_gate'], 'v_w_ffn_up': out['v_w_ffn_up'], 'v_w_ffn_down': out['v_w_ffn_down'], 'v_final_g': out['v_final_g']}


def _loss(weights, diff, rest, loss_target):
    with _jax.named_scope("forward"):
        args = {**rest, TWIN_DIFF_INPUT: diff, **{k: w.astype(_WEIGHT_DTYPES[k]) for k, w in weights.items()}}
        y = _forward(args)
    with _jax.named_scope("loss_head"):
        err = _jnp.square(y.astype(_jnp.float32) - loss_target)
        return 0.5 * _jnp.sum(_jnp.mean(err, axis=-1)) if err.ndim else 0.5 * err


def _adamw(w, g, m, v):
    m = ADAM_B1 * m + (1.0 - ADAM_B1) * g
    v = ADAM_B2 * v + (1.0 - ADAM_B2) * _jnp.square(g)
    m_hat = m / (1.0 - ADAM_B1 ** ADAM_STEP)
    v_hat = v / (1.0 - ADAM_B2 ** ADAM_STEP)
    delta = -ADAM_LR * (m_hat / (_jnp.sqrt(v_hat) + ADAM_EPS) + ADAM_WD * w)
    return delta, m, v


def reference(x, norm1_g, w_in, conv_a_w, conv_a_b, lru_wx, lru_bx, lru_wa, lru_ba, lru_lambda, w_a_out, conv_b_w, w_b_out, sinks, w_c_out, conv_d_w, conv_d_b, ln_d_g, ln_d_b, w_d_out, w_o, norm2_g, w_ffn_gate, w_ffn_up, w_ffn_down, final_g, loss_target, m_norm1_g, m_w_in, m_conv_a_w, m_conv_a_b, m_lru_wx, m_lru_bx, m_lru_wa, m_lru_ba, m_lru_lambda, m_w_a_out, m_conv_b_w, m_w_b_out, m_sinks, m_w_c_out, m_conv_d_w, m_conv_d_b, m_ln_d_g, m_ln_d_b, m_w_d_out, m_w_o, m_norm2_g, m_w_ffn_gate, m_w_ffn_up, m_w_ffn_down, m_final_g, v_norm1_g, v_w_in, v_conv_a_w, v_conv_a_b, v_lru_wx, v_lru_bx, v_lru_wa, v_lru_ba, v_lru_lambda, v_w_a_out, v_conv_b_w, v_w_b_out, v_sinks, v_w_c_out, v_conv_d_w, v_conv_d_b, v_ln_d_g, v_ln_d_b, v_w_d_out, v_w_o, v_norm2_g, v_w_ffn_gate, v_w_ffn_up, v_w_ffn_down, v_final_g):
    given = dict(x=x, norm1_g=norm1_g, w_in=w_in, conv_a_w=conv_a_w, conv_a_b=conv_a_b, lru_wx=lru_wx, lru_bx=lru_bx, lru_wa=lru_wa, lru_ba=lru_ba, lru_lambda=lru_lambda, w_a_out=w_a_out, conv_b_w=conv_b_w, w_b_out=w_b_out, sinks=sinks, w_c_out=w_c_out, conv_d_w=conv_d_w, conv_d_b=conv_d_b, ln_d_g=ln_d_g, ln_d_b=ln_d_b, w_d_out=w_d_out, w_o=w_o, norm2_g=norm2_g, w_ffn_gate=w_ffn_gate, w_ffn_up=w_ffn_up, w_ffn_down=w_ffn_down, final_g=final_g, loss_target=loss_target, m_norm1_g=m_norm1_g, m_w_in=m_w_in, m_conv_a_w=m_conv_a_w, m_conv_a_b=m_conv_a_b, m_lru_wx=m_lru_wx, m_lru_bx=m_lru_bx, m_lru_wa=m_lru_wa, m_lru_ba=m_lru_ba, m_lru_lambda=m_lru_lambda, m_w_a_out=m_w_a_out, m_conv_b_w=m_conv_b_w, m_w_b_out=m_w_b_out, m_sinks=m_sinks, m_w_c_out=m_w_c_out, m_conv_d_w=m_conv_d_w, m_conv_d_b=m_conv_d_b, m_ln_d_g=m_ln_d_g, m_ln_d_b=m_ln_d_b, m_w_d_out=m_w_d_out, m_w_o=m_w_o, m_norm2_g=m_norm2_g, m_w_ffn_gate=m_w_ffn_gate, m_w_ffn_up=m_w_ffn_up, m_w_ffn_down=m_w_ffn_down, m_final_g=m_final_g, v_norm1_g=v_norm1_g, v_w_in=v_w_in, v_conv_a_w=v_conv_a_w, v_conv_a_b=v_conv_a_b, v_lru_wx=v_lru_wx, v_lru_bx=v_lru_bx, v_lru_wa=v_lru_wa, v_lru_ba=v_lru_ba, v_lru_lambda=v_lru_lambda, v_w_a_out=v_w_a_out, v_conv_b_w=v_conv_b_w, v_w_b_out=v_w_b_out, v_sinks=v_sinks, v_w_c_out=v_w_c_out, v_conv_d_w=v_conv_d_w, v_conv_d_b=v_conv_d_b, v_ln_d_g=v_ln_d_g, v_ln_d_b=v_ln_d_b, v_w_d_out=v_w_d_out, v_w_o=v_w_o, v_norm2_g=v_norm2_g, v_w_ffn_gate=v_w_ffn_gate, v_w_ffn_up=v_w_ffn_up, v_w_ffn_down=v_w_ffn_down, v_final_g=v_final_g)
    weights = {n: given[n] for n in TWIN_WEIGHTS}
    shared = {n: given[n] for n in SHARED_INPUTS}
    per_example = {n: given[n] for n in ['x']}
    grad_fn = _jax.value_and_grad(_loss, argnums=(0, 1))

    def one_microbatch(ex, loss_target):
        ex = dict(ex)
        diff = ex.pop(TWIN_DIFF_INPUT)
        return grad_fn(weights, diff, {**shared, **ex}, loss_target)

    if N_MICROBATCH == 1:
        loss, (grad_w, grad_x) = one_microbatch(per_example, given["loss_target"])
    else:
        def body(carry, xs):
            loss_sum, grad_sum = carry
            l_k, (gw_k, gx_k) = one_microbatch(xs[0], xs[1])
            with _jax.named_scope("update"):
                return (loss_sum + l_k, _jax.tree.map(_jnp.add, grad_sum, gw_k)), gx_k

        init = (_jnp.zeros((), _jnp.float32), _jax.tree.map(_jnp.zeros_like, weights))
        (loss, grad_w), grad_x = _jax.lax.scan(body, init, (per_example, given["loss_target"]))
    with _jax.named_scope("update"):
        delta_w, new_m, new_v = {}, {}, {}
        for n in TWIN_WEIGHTS:
            delta_w[n], new_m[n], new_v[n] = _adamw(weights[n], grad_w[n], given["m_" + n], given["v_" + n])
    return (loss, grad_x, *[grad_w[n] for n in TWIN_WEIGHTS], *[delta_w[n] for n in TWIN_WEIGHTS],
            *[new_m[n] for n in TWIN_WEIGHTS], *[new_v[n] for n in TWIN_WEIGHTS])
```

```python
import functools
import math

import numpy as np
import jax
import jax.numpy as jnp
from jax import lax
from jax.experimental import pallas as pl
from jax.experimental.pallas import tpu as pltpu

F32 = jnp.float32
BF16 = jnp.bfloat16
MESH = pl.DeviceIdType.MESH

D_MODEL = 1024
DEPTH = 2
BW = 512
HEAD_DIM = 64
N_Q = 8
N_KV = 2
BLK = 128
D_FF = 2816
IN_W = 8448
EPS = 1e-6
NEG_INF = -1e30
LRU_C = 8.0
CONV_A, CONV_B, CONV_D = 4, 3, 31
LANE = 128
ROW_TILE = 256
VMEM_LIMIT = 56 * 1024 * 1024

C_AX, C_AG, C_BV, C_BC, C_BB = 0, 4, 8, 12, 16
OFF_Q, OFF_K, OFF_V = 2560, 3072, 3200
C_D1, C_D2 = 26, 30
OFF_GL = 4352

ADAM_LR, ADAM_B1, ADAM_B2, ADAM_EPS, ADAM_WD, ADAM_STEP = 0.001, 0.9, 0.999, 1e-08, 0.01, 10

PACK_W = 1024
PACK_R = 10240
PACK_M = PACK_R // 2

BIG = ("w_in", "w_a_out", "w_b_out", "w_c_out", "w_d_out", "w_o", "w_ffn_gate", "w_ffn_up", "w_ffn_down")
BIG_ROWS = dict(w_in=4224, w_a_out=256, w_b_out=256, w_c_out=256, w_d_out=256, w_o=512,
                w_ffn_gate=1408, w_ffn_up=1408, w_ffn_down=1408)
CONV_NAMES = ("conv_a_w", "conv_b_w", "conv_d_w")
SMALL = ("norm1_g", "conv_a_w", "conv_a_b", "lru_wx", "lru_bx", "lru_wa", "lru_ba", "lru_lambda", "conv_b_w",
         "sinks", "conv_d_w", "conv_d_b", "ln_d_g", "ln_d_b", "norm2_g", "final_g")
NAMES = ['norm1_g', 'w_in', 'conv_a_w', 'conv_a_b', 'lru_wx', 'lru_bx', 'lru_wa', 'lru_ba', 'lru_lambda', 'w_a_out',
         'conv_b_w', 'w_b_out', 'sinks', 'w_c_out', 'conv_d_w', 'conv_d_b', 'ln_d_g', 'ln_d_b', 'w_d_out', 'w_o',
         'norm2_g', 'w_ffn_gate', 'w_ffn_up', 'w_ffn_down', 'final_g']


def _pick(n, cands):
    for c in cands:
        if n % c == 0:
            return c
    return n


def _params(sem=None):
    return pltpu.CompilerParams(dimension_semantics=sem, vmem_limit_bytes=VMEM_LIMIT)


def _sig(z):
    return 1.0 / (1.0 + jnp.exp(-z))


def _dot(a, b, dims):
    return lax.dot_general(a.astype(BF16), b.astype(BF16), (dims, ((), ())), preferred_element_type=F32)


NN = ((1,), (0,))
NT = ((1,), (1,))
TN = ((0,), (0,))


def _mm(a, b, mode, name, out_dtype=F32, add=None):
    if mode == "nn":
        (m, k), n = a.shape, b.shape[1]
    elif mode == "nt":
        (m, k), n = a.shape, b.shape[0]
    else:
        (k, m), n = a.shape, b.shape[1]
    tm = _pick(m, (512, 256, 128))
    tn = _pick(n, (768, 512, 256, 128))
    tk = _pick(k, (2048, 1408, 1024, 768, 512))
    nk = k // tk
    dims = {"nn": NN, "nt": NT, "tn": TN}[mode]

    def body(*refs):
        if add is None:
            a_ref, b_ref, o_ref, acc = refs
        else:
            a_ref, b_ref, c_ref, o_ref, acc = refs
        kk = pl.program_id(2)

        @pl.when(kk == 0)
        def _():
            acc[...] = jnp.zeros_like(acc)

        acc[...] += _dot(a_ref[...], b_ref[...], dims)

        @pl.when(kk == nk - 1)
        def _():
            r = acc[...]
            if add is not None:
                r = r + c_ref[...]
            o_ref[...] = r.astype(out_dtype)

    if mode == "nn":
        a_spec = pl.BlockSpec((tm, tk), lambda i, j, q: (i, q))
        b_spec = pl.BlockSpec((tk, tn), lambda i, j, q: (q, j))
    elif mode == "nt":
        a_spec = pl.BlockSpec((tm, tk), lambda i, j, q: (i, q))
        b_spec = pl.BlockSpec((tn, tk), lambda i, j, q: (j, q))
    else:
        a_spec = pl.BlockSpec((tk, tm), lambda i, j, q: (q, i))
        b_spec = pl.BlockSpec((tk, tn), lambda i, j, q: (q, j))
    o_spec = pl.BlockSpec((tm, tn), lambda i, j, q: (i, j))
    in_specs = [a_spec, b_spec] + ([o_spec] if add is not None else [])
    args = (a, b) + ((add,) if add is not None else ())
    return pl.pallas_call(
        body, name=name, out_shape=jax.ShapeDtypeStruct((m, n), out_dtype),
        grid=(m // tm, n // tn, nk), in_specs=in_specs, out_specs=o_spec,
        scratch_shapes=[pltpu.VMEM((tm, tn), F32)],
        compiler_params=_params(("parallel", "parallel", "arbitrary")),
    )(*args)


def _row_spec(cols, tr=ROW_TILE):
    return pl.BlockSpec((tr, cols), lambda i: (i, 0))


def _vec_spec(cols):
    return pl.BlockSpec((1, cols), lambda i: (0, 0))


def _rms_fwd(x, g, name):
    t, d = x.shape

    def body(x_ref, g_ref, o_ref):
        xv = x_ref[...]
        r = lax.rsqrt(jnp.mean(xv * xv, axis=1, keepdims=True) + EPS)
        o_ref[...] = (xv * r * g_ref[...]).astype(BF16)

    return pl.pallas_call(
        body, name=name, out_shape=jax.ShapeDtypeStruct((t, d), BF16), grid=(t // ROW_TILE,),
        in_specs=[_row_spec(d), _vec_spec(d)], out_specs=_row_spec(d), compiler_params=_params(("parallel",)),
    )(x, g)


def _rms_bwd(x, g, dxn, dres, name):
    t, d = x.shape

    def body(x_ref, g_ref, dy_ref, dr_ref, dx_ref, dg_ref):
        @pl.when(pl.program_id(0) == 0)
        def _():
            dg_ref[...] = jnp.zeros_like(dg_ref)

        xv = x_ref[...]
        dy = dy_ref[...]
        r = lax.rsqrt(jnp.mean(xv * xv, axis=1, keepdims=True) + EPS)
        w = dy * g_ref[...]
        dx_ref[...] = dr_ref[...] + r * w - xv * (r * r * r) * jnp.mean(w * xv, axis=1, keepdims=True)
        dg_ref[...] += jnp.sum(dy * xv * r, axis=0, keepdims=True)

    return pl.pallas_call(
        body, name=name,
        out_shape=(jax.ShapeDtypeStruct((t, d), F32), jax.ShapeDtypeStruct((1, d), F32)), grid=(t // ROW_TILE,),
        in_specs=[_row_spec(d), _vec_spec(d), _row_spec(d), _row_spec(d)], out_specs=(_row_spec(d), _vec_spec(d)),
        compiler_params=_params(("arbitrary",)),
    )(x, g, dxn, dres)


def _loss_head(x, g, tgt, name):
    t, d = x.shape

    def body(x_ref, g_ref, t_ref, loss_ref, dx_ref, dg_ref):
        @pl.when(pl.program_id(0) == 0)
        def _():
            dg_ref[...] = jnp.zeros_like(dg_ref)
            loss_ref[...] = jnp.zeros_like(loss_ref)

        xv = x_ref[...]
        gv = g_ref[...]
        r = lax.rsqrt(jnp.mean(xv * xv, axis=1, keepdims=True) + EPS)
        e = xv * r * gv - t_ref[...]
        loss_ref[...] += jnp.full(loss_ref.shape, (0.5 / d) * jnp.sum(e * e), F32)
        dy = e * (1.0 / d)
        w = dy * gv
        dx_ref[...] = r * w - xv * (r * r * r) * jnp.mean(w * xv, axis=1, keepdims=True)
        dg_ref[...] += jnp.sum(dy * xv * r, axis=0, keepdims=True)

    return pl.pallas_call(
        body, name=name,
        out_shape=(jax.ShapeDtypeStruct((1, LANE), F32), jax.ShapeDtypeStruct((t, d), F32),
                   jax.ShapeDtypeStruct((1, d), F32)),
        grid=(t // ROW_TILE,), in_specs=[_row_spec(d), _vec_spec(d), _row_spec(d)],
        out_specs=(_vec_spec(LANE), _row_spec(d), _vec_spec(d)), compiler_params=_params(("arbitrary",)),
    )(x, g, tgt)


def _swiglu_fwd(gg, uu, name):
    t, f = gg.shape

    def body(g_ref, u_ref, o_ref):
        gv = g_ref[...]
        o_ref[...] = (gv * _sig(gv) * u_ref[...]).astype(BF16)

    return pl.pallas_call(
        body, name=name, out_shape=jax.ShapeDtypeStruct((t, f), BF16), grid=(t // ROW_TILE,),
        in_specs=[_row_spec(f), _row_spec(f)], out_specs=_row_spec(f), compiler_params=_params(("parallel",)),
    )(gg, uu)


def _swiglu_bwd(gg, uu, dact, name):
    t, f = gg.shape

    def body(g_ref, u_ref, d_ref, dg_ref, du_ref):
        gv = g_ref[...]
        dv = d_ref[...]
        s = _sig(gv)
        dg_ref[...] = (dv * u_ref[...] * s * (1.0 + gv * (1.0 - s))).astype(BF16)
        du_ref[...] = (dv * gv * s).astype(BF16)

    return pl.pallas_call(
        body, name=name,
        out_shape=(jax.ShapeDtypeStruct((t, f), BF16), jax.ShapeDtypeStruct((t, f), BF16)), grid=(t // ROW_TILE,),
        in_specs=[_row_spec(f)] * 3, out_specs=(_row_spec(f), _row_spec(f)), compiler_params=_params(("parallel",)),
    )(gg, uu, dact)


MERGE_COLS = 256


def _gate_specs():
    nb = D_MODEL // MERGE_COLS
    base = OFF_GL // MERGE_COLS
    return [pl.BlockSpec((ROW_TILE, MERGE_COLS), functools.partial(lambda i, j, kk: (i, base + nb * kk + j), kk=kk))
            for kk in range(4)]


def _merge_fwd(proj, ys, name):
    t = proj.shape[0]
    yspec = pl.BlockSpec((ROW_TILE, MERGE_COLS), lambda i, j: (i, j))

    def body(g0, g1, g2, g3, y0, y1, y2, y3, o_ref):
        acc = _sig(g0[...]) * y0[...]
        acc += _sig(g1[...]) * y1[...]
        acc += _sig(g2[...]) * y2[...]
        acc += _sig(g3[...]) * y3[...]
        o_ref[...] = acc.astype(BF16)

    return pl.pallas_call(
        body, name=name, out_shape=jax.ShapeDtypeStruct((t, D_MODEL), BF16),
        grid=(t // ROW_TILE, D_MODEL // MERGE_COLS), in_specs=_gate_specs() + [yspec] * 4, out_specs=yspec,
        compiler_params=_params(("parallel", "parallel")),
    )(proj, proj, proj, proj, *ys)


def _merge_bwd(proj, ys, dmerged, name):
    t = proj.shape[0]
    yspec = pl.BlockSpec((ROW_TILE, MERGE_COLS), lambda i, j: (i, j))

    def body(g0, g1, g2, g3, y0, y1, y2, y3, dm_ref, *outs):
        dm = dm_ref[...]
        for gr, yr, dy_ref, dg_ref in zip((g0, g1, g2, g3), (y0, y1, y2, y3), outs[:4], outs[4:]):
            s = _sig(gr[...])
            dy_ref[...] = (dm * s).astype(BF16)
            dg_ref[...] = (dm * yr[...] * s * (1.0 - s)).astype(BF16)

    shp = jax.ShapeDtypeStruct((t, D_MODEL), BF16)
    outs = pl.pallas_call(
        body, name=name, out_shape=(shp,) * 8, grid=(t // ROW_TILE, D_MODEL // MERGE_COLS),
        in_specs=_gate_specs() + [yspec] * 5, out_specs=(yspec,) * 8, compiler_params=_params(("parallel", "parallel")),
    )(proj, proj, proj, proj, *ys, dmerged)
    return outs[:4], outs[4:]


def _ln_silu_fwd(cd, g, b, name):
    t, c = cd.shape

    def body(x_ref, g_ref, b_ref, o_ref):
        xv = x_ref[...]
        mu = jnp.mean(xv, axis=1, keepdims=True)
        xc = xv - mu
        rs = lax.rsqrt(jnp.mean(xc * xc, axis=1, keepdims=True) + EPS)
        z = xc * rs * g_ref[...] + b_ref[...]
        o_ref[...] = (z * _sig(z)).astype(BF16)

    return pl.pallas_call(
        body, name=name, out_shape=jax.ShapeDtypeStruct((t, c), BF16), grid=(t // ROW_TILE,),
        in_specs=[_row_spec(c), _vec_spec(c), _vec_spec(c)], out_specs=_row_spec(c),
        compiler_params=_params(("parallel",)),
    )(cd, g, b)


def _ln_silu_bwd(cd, g, b, dy, name):
    t, c = cd.shape

    def body(x_ref, g_ref, b_ref, dy_ref, dx_ref, dg_ref, db_ref):
        @pl.when(pl.program_id(0) == 0)
        def _():
            dg_ref[...] = jnp.zeros_like(dg_ref)
            db_ref[...] = jnp.zeros_like(db_ref)

        xv = x_ref[...]
        gv = g_ref[...]
        mu = jnp.mean(xv, axis=1, keepdims=True)
        xc = xv - mu
        rs = lax.rsqrt(jnp.mean(xc * xc, axis=1, keepdims=True) + EPS)
        xh = xc * rs
        z = xh * gv + b_ref[...]
        s = _sig(z)
        dz = dy_ref[...] * s * (1.0 + z * (1.0 - s))
        dg_ref[...] += jnp.sum(dz * xh, axis=0, keepdims=True)
        db_ref[...] += jnp.sum(dz, axis=0, keepdims=True)
        dxh = dz * gv
        dx_ref[...] = rs * (dxh - jnp.mean(dxh, axis=1, keepdims=True) - xh * jnp.mean(dxh * xh, axis=1, keepdims=True))

    return pl.pallas_call(
        body, name=name,
        out_shape=(jax.ShapeDtypeStruct((t, c), F32), jax.ShapeDtypeStruct((1, c), F32),
                   jax.ShapeDtypeStruct((1, c), F32)),
        grid=(t // ROW_TILE,), in_specs=[_row_spec(c), _vec_spec(c), _vec_spec(c), _row_spec(c)],
        out_specs=(_row_spec(c), _vec_spec(c), _vec_spec(c)), compiler_params=_params(("arbitrary",)),
    )(cd, g, b, dy)


def _shift_dn(x, k):
    if k == 0:
        return x
    row = lax.broadcasted_iota(jnp.int32, x.shape, 0)
    return jnp.where(row >= k, pltpu.roll(x, k, 0), 0.0)


def _shift_up(x, k):
    if k == 0:
        return x
    t = x.shape[0]
    row = lax.broadcasted_iota(jnp.int32, x.shape, 0)
    return jnp.where(row < t - k, pltpu.roll(x, t - k, 0), 0.0)


def _conv_fwd(x, w_ref, taps):
    acc = w_ref[pl.ds(taps - 1, 1), :] * x
    for k in range(taps - 1):
        acc += w_ref[pl.ds(k, 1), :] * _shift_dn(x, taps - 1 - k)
    return acc


def _conv_bwd(x, dy, w_ref, dw_ref, taps):
    dx = w_ref[pl.ds(taps - 1, 1), :] * dy
    dw_ref[pl.ds(taps - 1, 1), :] = jnp.sum(dy * x, axis=0, keepdims=True)
    for k in range(taps - 1):
        s = taps - 1 - k
        dx += w_ref[pl.ds(k, 1), :] * _shift_up(dy, s)
        dw_ref[pl.ds(k, 1), :] = jnp.sum(dy * _shift_dn(x, s), axis=0, keepdims=True)
    return dx


def _scan_fwd(a, u):
    t = a.shape[0]
    k = 1
    while k < t:
        u = u + a * _shift_dn(u, k)
        if 2 * k < t:
            a = a * _shift_dn(a, k)
        k *= 2
    return u


def _scan_rev(a, u):
    t = a.shape[0]
    k = 1
    while k < t:
        u = u + a * _shift_up(u, k)
        if 2 * k < t:
            a = a * _shift_up(a, k)
        k *= 2
    return u


def _one_minus_exp(y):
    return jnp.where(y > -1e-3, -(y + 0.5 * y * y + (1.0 / 6.0) * y * y * y), 1.0 - jnp.exp(y))


GELU_C = math.sqrt(2.0 / math.pi)


def _gelu(x):
    th = jnp.tanh(GELU_C * (x + 0.044715 * x * x * x))
    return 0.5 * x * (1.0 + th), th


def _softplus(x):
    return jnp.maximum(x, 0.0) + jnp.log(1.0 + jnp.exp(-jnp.abs(x)))


def _chunk_spec(t, blk0):
    return pl.BlockSpec((t, LANE), functools.partial(lambda c, b: (0, b + c), b=blk0))


def _tap_spec(taps):
    return pl.BlockSpec((taps, LANE), lambda c: (0, c))


def _cvec_spec():
    return pl.BlockSpec((1, LANE), lambda c: (0, c))


def _cmat_spec():
    return pl.BlockSpec((1, LANE, LANE), lambda c: (c, 0, 0))


def _lru_forward(ax, wA_ref, bA_ref, wx_ref, bx_ref, wa_ref, ba_ref, lam_ref):
    ca = _conv_fwd(ax, wA_ref, CONV_A) + bA_ref[...]
    gi = _sig(_dot(ca, wx_ref[0], NN) + bx_ref[...])
    gr = _sig(_dot(ca, wa_ref[0], NN) + ba_ref[...])
    sp = _softplus(-lam_ref[...])
    la = -LRU_C * gr * sp
    a = jnp.exp(la)
    mult = jnp.sqrt(_one_minus_exp(2.0 * la))
    h = _scan_fwd(a, ca * gi * mult)
    return ca, gi, gr, sp, a, mult, h


def _a_fwd(proj, wA, bA, wx, bx, wa, ba, lam, name):
    t = proj.shape[0]

    def body(ax_ref, ag_ref, wA_ref, bA_ref, wx_ref, bx_ref, wa_ref, ba_ref, lam_ref, o_ref):
        h = _lru_forward(ax_ref[...], wA_ref, bA_ref, wx_ref, bx_ref, wa_ref, ba_ref, lam_ref)[-1]
        o_ref[...] = (h * _gelu(ag_ref[...])[0]).astype(BF16)

    return pl.pallas_call(
        body, name=name, out_shape=jax.ShapeDtypeStruct((t, BW), BF16), grid=(BW // LANE,),
        in_specs=[_chunk_spec(t, C_AX), _chunk_spec(t, C_AG), _tap_spec(CONV_A), _cvec_spec(), _cmat_spec(),
                  _cvec_spec(), _cmat_spec(), _cvec_spec(), _cvec_spec()],
        out_specs=_chunk_spec(t, 0), compiler_params=_params(("parallel",)),
    )(proj, proj, wA, bA, wx, bx, wa, ba, lam)


def _a_bwd(proj, dya, wA, bA, wx, bx, wa, ba, lam, name):
    t = proj.shape[0]

    def body(ax_ref, ag_ref, dy_ref, wA_ref, bA_ref, wx_ref, bx_ref, wa_ref, ba_ref, lam_ref,
             dax_ref, dag_ref, dwA_ref, dbA_ref, dwx_ref, dbx_ref, dwa_ref, dba_ref, dlam_ref):
        ax = ax_ref[...]
        ag = ag_ref[...]
        dy = dy_ref[...]
        ca, gi, gr, sp, a, mult, h = _lru_forward(ax, wA_ref, bA_ref, wx_ref, bx_ref, wa_ref, ba_ref, lam_ref)
        gel, th = _gelu(ag)
        dgel = 0.5 * (1.0 + th) + 0.5 * ag * (1.0 - th * th) * GELU_C * (1.0 + 3.0 * 0.044715 * ag * ag)
        dag_ref[...] = (dy * h * dgel).astype(BF16)
        s = _scan_rev(_shift_up(a, 1), dy * gel)
        da = s * _shift_dn(h, 1)
        dca = s * gi * mult
        dgi = s * ca * mult
        dmult = s * ca * gi
        dla = da * a - dmult * a * a / mult
        dgr = dla * (-LRU_C * sp)
        dsp = jnp.sum(dla * (-LRU_C * gr), axis=0, keepdims=True)
        dlam_ref[...] = -_sig(-lam_ref[...]) * dsp
        dzi = dgi * gi * (1.0 - gi)
        dzr = dgr * gr * (1.0 - gr)
        dbx_ref[...] = jnp.sum(dzi, axis=0, keepdims=True)
        dba_ref[...] = jnp.sum(dzr, axis=0, keepdims=True)
        dwx_ref[0] = _dot(ca, dzi, TN)
        dwa_ref[0] = _dot(ca, dzr, TN)
        dca += _dot(dzi, wx_ref[0], NT) + _dot(dzr, wa_ref[0], NT)
        dbA_ref[...] = jnp.sum(dca, axis=0, keepdims=True)
        dax_ref[...] = _conv_bwd(ax, dca, wA_ref, dwA_ref, CONV_A).astype(BF16)

    big = jax.ShapeDtypeStruct((t, BW), BF16)
    vec = jax.ShapeDtypeStruct((1, BW), F32)
    mat = jax.ShapeDtypeStruct((BW // LANE, LANE, LANE), F32)
    return pl.pallas_call(
        body, name=name,
        out_shape=(big, big, jax.ShapeDtypeStruct((CONV_A, BW), F32), vec, mat, vec, mat, vec, vec),
        grid=(BW // LANE,),
        in_specs=[_chunk_spec(t, C_AX), _chunk_spec(t, C_AG), _chunk_spec(t, 0), _tap_spec(CONV_A), _cvec_spec(),
                  _cmat_spec(), _cvec_spec(), _cmat_spec(), _cvec_spec(), _cvec_spec()],
        out_specs=(_chunk_spec(t, 0), _chunk_spec(t, 0), _tap_spec(CONV_A), _cvec_spec(), _cmat_spec(), _cvec_spec(),
                   _cmat_spec(), _cvec_spec(), _cvec_spec()),
        compiler_params=_params(("parallel",)),
    )(proj, proj, dya, wA, bA, wx, bx, wa, ba, lam)


def _b_fwd(proj, wB, name):
    t = proj.shape[0]

    def body(bv_ref, bc_ref, bb_ref, w_ref, o_ref):
        o_ref[...] = (bb_ref[...] * _conv_fwd(bc_ref[...] * bv_ref[...], w_ref, CONV_B)).astype(BF16)

    return pl.pallas_call(
        body, name=name, out_shape=jax.ShapeDtypeStruct((t, BW), BF16), grid=(BW // LANE,),
        in_specs=[_chunk_spec(t, C_BV), _chunk_spec(t, C_BC), _chunk_spec(t, C_BB), _tap_spec(CONV_B)],
        out_specs=_chunk_spec(t, 0), compiler_params=_params(("parallel",)),
    )(proj, proj, proj, wB)


def _b_bwd(proj, dyb, wB, name):
    t = proj.shape[0]

    def body(bv_ref, bc_ref, bb_ref, dy_ref, w_ref, dbv_ref, dbc_ref, dbb_ref, dw_ref):
        bv = bv_ref[...]
        bc = bc_ref[...]
        dy = dy_ref[...]
        p = bc * bv
        dbb_ref[...] = (dy * _conv_fwd(p, w_ref, CONV_B)).astype(BF16)
        dp = _conv_bwd(p, dy * bb_ref[...], w_ref, dw_ref, CONV_B)
        dbc_ref[...] = (dp * bv).astype(BF16)
        dbv_ref[...] = (dp * bc).astype(BF16)

    big = jax.ShapeDtypeStruct((t, BW), BF16)
    return pl.pallas_call(
        body, name=name, out_shape=(big, big, big, jax.ShapeDtypeStruct((CONV_B, BW), F32)), grid=(BW // LANE,),
        in_specs=[_chunk_spec(t, C_BV), _chunk_spec(t, C_BC), _chunk_spec(t, C_BB), _chunk_spec(t, 0),
                  _tap_spec(CONV_B)],
        out_specs=(_chunk_spec(t, 0),) * 3 + (_tap_spec(CONV_B),), compiler_params=_params(("parallel",)),
    )(proj, proj, proj, dyb, wB)


def _d_conv_fwd(proj, wD, bD, name):
    t = proj.shape[0]

    def body(d1_ref, d2_ref, w_ref, b_ref, o_ref):
        o_ref[...] = _conv_fwd(d1_ref[...] * _sig(d2_ref[...]), w_ref, CONV_D) + b_ref[...]

    return pl.pallas_call(
        body, name=name, out_shape=jax.ShapeDtypeStruct((t, BW), F32), grid=(BW // LANE,),
        in_specs=[_chunk_spec(t, C_D1), _chunk_spec(t, C_D2), _tap_spec(CONV_D), _cvec_spec()],
        out_specs=_chunk_spec(t, 0), compiler_params=_params(("parallel",)),
    )(proj, proj, wD, bD)


def _d_conv_bwd(proj, dcd, wD, name):
    t = proj.shape[0]

    def body(d1_ref, d2_ref, dy_ref, w_ref, dd1_ref, dd2_ref, dw_ref, db_ref):
        d1 = d1_ref[...]
        s = _sig(d2_ref[...])
        dy = dy_ref[...]
        db_ref[...] = jnp.sum(dy, axis=0, keepdims=True)
        dd = _conv_bwd(d1 * s, dy, w_ref, dw_ref, CONV_D)
        dd1_ref[...] = (dd * s).astype(BF16)
        dd2_ref[...] = (dd * d1 * s * (1.0 - s)).astype(BF16)

    big = jax.ShapeDtypeStruct((t, BW), BF16)
    return pl.pallas_call(
        body, name=name,
        out_shape=(big, big, jax.ShapeDtypeStruct((CONV_D, BW), F32), jax.ShapeDtypeStruct((1, BW), F32)),
        grid=(BW // LANE,),
        in_specs=[_chunk_spec(t, C_D1), _chunk_spec(t, C_D2), _chunk_spec(t, 0), _tap_spec(CONV_D)],
        out_specs=(_chunk_spec(t, 0), _chunk_spec(t, 0), _tap_spec(CONV_D), _cvec_spec()),
        compiler_params=_params(("parallel",)),
    )(proj, proj, dcd, wD)


SCALE = HEAD_DIM ** -0.5
GROUP = N_Q // N_KV


def _attn_probs(q_ref, k_ref, ss_ref, h, n):
    qi = lax.broadcasted_iota(jnp.int32, (BLK, BLK), 0)
    ki = lax.broadcasted_iota(jnp.int32, (BLK, BLK), 1)
    dist = (qi - ki).astype(F32)
    sink = ss_ref[0, h]
    slope = ss_ref[1, h]
    s0 = pl.multiple_of(n * BLK, BLK)
    sp = pl.multiple_of(jnp.maximum(n - 1, 0) * BLK, BLK)
    q = q_ref[0, pl.ds(s0, BLK), :]
    kc = k_ref[0, pl.ds(s0, BLK), :]
    kp = k_ref[0, pl.ds(sp, BLK), :]
    sc = jnp.where(ki <= qi, _dot(q, kc, NT) * SCALE - slope * dist, NEG_INF)
    first = jnp.where(n >= 1, 0, BLK)
    sv = jnp.where(ki > qi + first, _dot(q, kp, NT) * SCALE - slope * (dist + BLK), NEG_INF)
    m = jnp.maximum(jnp.maximum(jnp.max(sc, axis=1, keepdims=True), jnp.max(sv, axis=1, keepdims=True)), sink)
    pc = jnp.exp(sc - m)
    pp = jnp.exp(sv - m)
    ps = jnp.exp(sink - m)
    z = jnp.sum(pc, axis=1, keepdims=True) + jnp.sum(pp, axis=1, keepdims=True) + ps
    return s0, sp, q, kc, kp, pc, pp, ps, z


def _attn_specs(t):
    qs = pl.BlockSpec((1, t, HEAD_DIM), lambda h: (h, 0, 0))
    ks = pl.BlockSpec((1, t, HEAD_DIM), lambda h: (h // GROUP, 0, 0))
    ss = pl.BlockSpec(memory_space=pltpu.SMEM)
    return qs, ks, ss


def _attn_fwd(q, k, v, ss, name):
    t = q.shape[1]
    qs, ks, sspec = _attn_specs(t)

    def body(q_ref, k_ref, v_ref, ss_ref, o_ref):
        h = pl.program_id(0)

        def blk(n, carry):
            s0, sp, _, _, _, pc, pp, _, z = _attn_probs(q_ref, k_ref, ss_ref, h, n)
            o = _dot(pc, v_ref[0, pl.ds(s0, BLK), :], NN) + _dot(pp, v_ref[0, pl.ds(sp, BLK), :], NN)
            o_ref[0, pl.ds(s0, BLK), :] = (o / z).astype(BF16)
            return carry

        lax.fori_loop(0, t // BLK, blk, 0)

    return pl.pallas_call(
        body, name=name, out_shape=jax.ShapeDtypeStruct((N_Q, t, HEAD_DIM), BF16), grid=(N_Q,),
        in_specs=[qs, ks, ks, sspec], out_specs=qs, compiler_params=_params(("parallel",)),
    )(q, k, v, ss)


def _attn_bwd(q, k, v, do, ss, name):
    t = q.shape[1]
    qs, ks, sspec = _attn_specs(t)

    def body(q_ref, k_ref, v_ref, do_ref, ss_ref, dq_ref, dk_ref, dv_ref, ds_ref):
        h = pl.program_id(0)

        @pl.when(h % GROUP == 0)
        def _():
            dk_ref[...] = jnp.zeros_like(dk_ref)
            dv_ref[...] = jnp.zeros_like(dv_ref)

        def blk(n, dsink):
            s0, sp, q, kc, kp, pc, pp, ps, z = _attn_probs(q_ref, k_ref, ss_ref, h, n)
            rz = 1.0 / z
            pc = pc * rz
            pp = pp * rz
            do_b = do_ref[0, pl.ds(s0, BLK), :]
            dpc = _dot(do_b, v_ref[0, pl.ds(s0, BLK), :], NT)
            dpp = _dot(do_b, v_ref[0, pl.ds(sp, BLK), :], NT)
            delta = jnp.sum(pc * dpc, axis=1, keepdims=True) + jnp.sum(pp * dpp, axis=1, keepdims=True)
            dsc = pc * (dpc - delta)
            dsp = pp * (dpp - delta)
            dq_ref[0, pl.ds(s0, BLK), :] = ((_dot(dsc, kc, NN) + _dot(dsp, kp, NN)) * SCALE).astype(BF16)
            dk_ref[0, pl.ds(s0, BLK), :] += _dot(dsc, q, TN) * SCALE
            dk_ref[0, pl.ds(sp, BLK), :] += _dot(dsp, q, TN) * SCALE
            dv_ref[0, pl.ds(s0, BLK), :] += _dot(pc, do_b, TN)
            dv_ref[0, pl.ds(sp, BLK), :] += _dot(pp, do_b, TN)
            return dsink - ps * rz * delta

        dsink = lax.fori_loop(0, t // BLK, blk, jnp.zeros((BLK, 1), F32))
        ds_ref[...] = jnp.full(ds_ref.shape, jnp.sum(dsink), F32)

    kv = jax.ShapeDtypeStruct((N_KV, t, HEAD_DIM), F32)
    return pl.pallas_call(
        body, name=name,
        out_shape=(jax.ShapeDtypeStruct((N_Q, t, HEAD_DIM), BF16), kv, kv, jax.ShapeDtypeStruct((N_Q, 8, LANE), F32)),
        grid=(N_Q,), in_specs=[qs, ks, ks, qs, sspec],
        out_specs=(qs, ks, ks, pl.BlockSpec((1, 8, LANE), lambda h: (h, 0, 0))),
        compiler_params=_params(("arbitrary",)),
    )(q, k, v, do, ss)


def _heads(x2d, n):
    t = x2d.shape[0]
    return x2d.reshape(t, n, HEAD_DIM).transpose(1, 0, 2)


def _unheads(x3d):
    n, t, _ = x3d.shape
    return x3d.transpose(1, 0, 2).reshape(t, n * HEAD_DIM)


def _adamw(w, g, m, v, name):
    r, c = w.shape
    tr = _pick(r, (256, 128, 64, 32, 16, 8))
    spec = pl.BlockSpec((tr, c), lambda i: (i, 0))

    def body(w_ref, g_ref, m_ref, v_ref, d_ref, nm_ref, nv_ref):
        gv = g_ref[...]
        nm = ADAM_B1 * m_ref[...] + (1.0 - ADAM_B1) * gv
        nv = ADAM_B2 * v_ref[...] + (1.0 - ADAM_B2) * (gv * gv)
        m_hat = nm / (1.0 - ADAM_B1 ** ADAM_STEP)
        v_hat = nv / (1.0 - ADAM_B2 ** ADAM_STEP)
        d_ref[...] = -ADAM_LR * (m_hat / (jnp.sqrt(v_hat) + ADAM_EPS) + ADAM_WD * w_ref[...])
        nm_ref[...] = nm
        nv_ref[...] = nv

    shp = jax.ShapeDtypeStruct((r, c), F32)
    return pl.pallas_call(
        body, name=name, out_shape=(shp, shp, shp), grid=(r // tr,), in_specs=[spec] * 4, out_specs=(spec,) * 3,
        compiler_params=_params(("parallel",)),
    )(w, g, m, v)


def _sum_leading(x, name):
    n, r, c = x.shape
    tr = _pick(r, (256, 128, 64, 32, 16, 8))

    def body(x_ref, o_ref):
        acc = x_ref[0]
        for i in range(1, n):
            acc = acc + x_ref[i]
        o_ref[...] = acc

    return pl.pallas_call(
        body, name=name, out_shape=jax.ShapeDtypeStruct((r, c), F32), grid=(r // tr,),
        in_specs=[pl.BlockSpec((n, tr, c), lambda i: (0, i, 0))], out_specs=pl.BlockSpec((tr, c), lambda i: (i, 0)),
        compiler_params=_params(("parallel",)),
    )(x)


def _sum_own_plus(p, sel, recv, name, out_dtype):
    _, r, c = p.shape
    n = recv.shape[0]
    tr = _pick(r, (256, 128, 16))

    def body(sel_ref, p_ref, r_ref, o_ref):
        acc = p_ref[0].astype(F32)
        for i in range(n):
            acc = acc + r_ref[i].astype(F32)
        o_ref[...] = acc.astype(out_dtype)

    grid_spec = pltpu.PrefetchScalarGridSpec(
        num_scalar_prefetch=1, grid=(r // tr,),
        in_specs=[pl.BlockSpec((1, tr, c), lambda i, s: (s[0], i, 0)), pl.BlockSpec((n, tr, c), lambda i, s: (0, i, 0))],
        out_specs=pl.BlockSpec((tr, c), lambda i, s: (i, 0)))
    return pl.pallas_call(
        body, name=name, out_shape=jax.ShapeDtypeStruct((r, c), out_dtype), grid_spec=grid_spec,
        compiler_params=_params(("parallel",)),
    )(sel, p, recv)


def _coords():
    return lax.axis_index("x"), lax.axis_index("y"), lax.axis_index("c")


def _allgather8(x2, name, space):
    _, m, n = x2.shape

    def body(x_ref, out_ref, send_sems, recv_sems, local_sem):
        x, y, c = _coords()
        me, sibling = (x, y, c), (x, y, 1 - c)
        chips = [(1 - x, y), (x, 1 - y), (1 - x, 1 - y)]
        mine_src = x_ref.at[c]

        def rows(px, py, pc):
            return out_ref.at[4 * px + 2 * py + pc]

        def copy(k, block, to, src=None):
            return pltpu.make_async_remote_copy(
                src_ref=rows(*block) if src is None else src, dst_ref=rows(*block),
                send_sem=send_sems.at[k], recv_sem=recv_sems.at[k], device_id=to, device_id_type=MESH)

        mine = pltpu.make_async_copy(mine_src, rows(*me), local_sem)
        mine.start()
        first = [copy(0, me, sibling, src=mine_src)]
        first += [copy(1 + j, me, (*chip, c), src=mine_src) for j, chip in enumerate(chips)]
        for cp in first:
            cp.start()
        passed = [copy(4 + j, (*chip, c), sibling) for j, chip in enumerate(chips)]
        for j, chip in enumerate(chips):
            copy(1 + j, (*chip, c), me).wait_recv()
            passed[j].start()
        copy(0, sibling, me).wait_recv()
        for j, chip in enumerate(chips):
            copy(4 + j, (*chip, 1 - c), me).wait_recv()
        for cp in first + passed:
            cp.wait_send()
        mine.wait()

    return pl.pallas_call(
        body, name=name, out_shape=jax.ShapeDtypeStruct((8, m, n), x2.dtype),
        in_specs=[pl.BlockSpec(memory_space=space)], out_specs=pl.BlockSpec(memory_space=space),
        scratch_shapes=[pltpu.SemaphoreType.DMA((7,)), pltpu.SemaphoreType.DMA((7,)), pltpu.SemaphoreType.DMA],
        compiler_params=pltpu.CompilerParams(vmem_limit_bytes=VMEM_LIMIT),
    )(x2)


def _swap_sibling_half(p, name):
    def body(p_ref, out_ref, send_sem, recv_sem):
        x, y, c = _coords()
        cp = pltpu.make_async_remote_copy(
            src_ref=p_ref.at[1 - c], dst_ref=out_ref, send_sem=send_sem, recv_sem=recv_sem,
            device_id=(x, y, 1 - c), device_id_type=MESH)
        cp.start()
        cp.wait()

    return pl.pallas_call(
        body, name=name, out_shape=jax.ShapeDtypeStruct(p.shape[1:], p.dtype),
        in_specs=[pl.BlockSpec(memory_space=pl.ANY)], out_specs=pl.BlockSpec(memory_space=pl.ANY),
        scratch_shapes=[pltpu.SemaphoreType.DMA, pltpu.SemaphoreType.DMA],
    )(p)


def _scatter_chips(s, name):
    _, m, n = s.shape

    def body(s_ref, out_ref, send_sems, recv_sems):
        x, y, c = _coords()
        chips = [(1 - x, y), (x, 1 - y), (1 - x, 1 - y)]
        cps = [pltpu.make_async_remote_copy(
            src_ref=s_ref.at[2 * cx + cy], dst_ref=out_ref.at[k], send_sem=send_sems.at[k],
            recv_sem=recv_sems.at[k], device_id=(cx, cy, c), device_id_type=MESH)
            for k, (cx, cy) in enumerate(chips)]
        for cp in cps:
            cp.start()
        for cp in cps:
            cp.wait()

    return pl.pallas_call(
        body, name=name, out_shape=jax.ShapeDtypeStruct((3, m, n), s.dtype),
        in_specs=[pl.BlockSpec(memory_space=pl.ANY)], out_specs=pl.BlockSpec(memory_space=pl.ANY),
        scratch_shapes=[pltpu.SemaphoreType.DMA((3,)), pltpu.SemaphoreType.DMA((3,))],
    )(s)


def _join_sibling(r, name):
    m, n = r.shape

    def body(r_ref, out_ref, send_sem, recv_sem, local_sem):
        x, y, c = _coords()
        local = pltpu.make_async_copy(r_ref, out_ref.at[c], local_sem)
        local.start()
        cp = pltpu.make_async_remote_copy(
            src_ref=r_ref, dst_ref=out_ref.at[c], send_sem=send_sem, recv_sem=recv_sem,
            device_id=(x, y, 1 - c), device_id_type=MESH)
        cp.start()
        cp.wait()
        local.wait()

    return pl.pallas_call(
        body, name=name, out_shape=jax.ShapeDtypeStruct((2, m, n), r.dtype),
        in_specs=[pl.BlockSpec(memory_space=pl.ANY)], out_specs=pl.BlockSpec(memory_space=pl.ANY),
        scratch_shapes=[pltpu.SemaphoreType.DMA, pltpu.SemaphoreType.DMA, pltpu.SemaphoreType.DMA],
    )(r)


def _pack_weight_shards(w):
    parts = [w[n].astype(BF16).reshape(-1, PACK_W) for n in BIG]
    taps = jnp.concatenate([w[n].reshape(-1) for n in CONV_NAMES])
    bits = lax.bitcast_convert_type(taps, BF16).reshape(-1)
    used = sum(BIG_ROWS.values()) * PACK_W + bits.shape[0]
    pad = jnp.zeros((PACK_R * PACK_W - used,), BF16)
    parts.append(jnp.concatenate([bits, pad]).reshape(-1, PACK_W))
    return jnp.concatenate(parts, axis=0).reshape(2, PACK_M, PACK_W)


def _unpack_full_weights(g):
    sec = g.reshape(4, PACK_R, PACK_W)
    out, off = {}, 0
    for n in BIG:
        rows = BIG_ROWS[n]
        blk = sec[:, off:off + rows]
        off += rows
        if n == "w_in":
            out[n] = blk.reshape(4, DEPTH, D_MODEL, IN_W // 4).transpose(1, 2, 0, 3).reshape(DEPTH, D_MODEL, IN_W)
        elif n.endswith("_out"):
            out[n] = blk.reshape(4, DEPTH, BW, D_MODEL // 4).transpose(1, 2, 0, 3).reshape(DEPTH, BW, D_MODEL)
        elif n == "w_o":
            out[n] = blk.reshape(4, DEPTH, D_MODEL // 4, D_MODEL).transpose(1, 0, 2, 3).reshape(DEPTH, D_MODEL, D_MODEL)
        elif n == "w_ffn_down":
            out[n] = blk.reshape(4, DEPTH, D_FF // 4, D_MODEL).transpose(1, 0, 2, 3).reshape(DEPTH, D_FF, D_MODEL)
        else:
            out[n] = blk.reshape(4, DEPTH, D_MODEL, D_FF // 4).transpose(1, 2, 0, 3).reshape(DEPTH, D_MODEL, D_FF)
    n_taps = DEPTH * (CONV_A + CONV_B + CONV_D) * LANE
    bits = sec[:, off:].reshape(4, -1)[:, :2 * n_taps].reshape(4, n_taps, 2)
    taps = lax.bitcast_convert_type(bits, F32)
    o2 = 0
    for n, k in zip(CONV_NAMES, (CONV_A, CONV_B, CONV_D)):
        cnt = DEPTH * k * LANE
        out[n] = taps[:, o2:o2 + cnt].reshape(4, DEPTH, k, LANE).transpose(1, 2, 0, 3).reshape(DEPTH, k, BW)
        o2 += cnt
    return out


def _pack_full_grads(g):
    parts = []
    for n in BIG:
        x = g[n]
        if n == "w_in":
            x = x.reshape(DEPTH, D_MODEL, 4, IN_W // 4).transpose(2, 0, 1, 3)
        elif n.endswith("_out"):
            x = x.reshape(DEPTH, BW, 4, D_MODEL // 4).transpose(2, 0, 1, 3)
        elif n == "w_o":
            x = x.reshape(DEPTH, 4, D_MODEL // 4, D_MODEL).transpose(1, 0, 2, 3)
        elif n == "w_ffn_down":
            x = x.reshape(DEPTH, 4, D_FF // 4, D_MODEL).transpose(1, 0, 2, 3)
        else:
            x = x.reshape(DEPTH, D_MODEL, 4, D_FF // 4).transpose(2, 0, 1, 3)
        parts.append(x.reshape(4, BIG_ROWS[n], PACK_W))
    parts.append(jnp.zeros((4, PACK_R - sum(BIG_ROWS.values()), PACK_W), BF16))
    sec = jnp.concatenate(parts, axis=1)
    return sec.reshape(4, 2, PACK_M, PACK_W).transpose(1, 0, 2, 3)


def _unpack_shard_grads(red, like):
    out, off = {}, 0
    for n in BIG:
        rows = BIG_ROWS[n]
        out[n] = red[off:off + rows].reshape(like[n].shape)
        off += rows
    return out


def _flat_pack(arrs):
    flat = jnp.concatenate([a.reshape(-1).astype(F32) for a in arrs])
    rows = -(-flat.shape[0] // (8 * LANE)) * 8
    return jnp.concatenate([flat, jnp.zeros((rows * LANE - flat.shape[0],), F32)]).reshape(rows, LANE)


def _flat_unpack(packed, shapes):
    flat, out, off = packed.reshape(-1), [], 0
    for s in shapes:
        cnt = int(np.prod(s))
        out.append(flat[off:off + cnt].reshape(s))
        off += cnt
    return out


def _blockdiag_chunks(w):
    w4 = w.reshape(4, 2, 64, 64)
    z = jnp.zeros((4, 2, 64, 2, 64), F32)
    z = z.at[:, 0, :, 0, :].set(w4[:, 0]).at[:, 1, :, 1, :].set(w4[:, 1])
    return z.reshape(4, LANE, LANE)


def _blockdiag_extract(d):
    d5 = d.reshape(4, 2, 64, 2, 64)
    return jnp.stack([d5[:, 0, :, 0, :], d5[:, 1, :, 1, :]], axis=1).reshape(8, 64, 64)


SLOPES = np.asarray([2.0 ** (-8.0 * (i + 1) / N_Q) for i in range(N_Q)], np.float32)


def _layer_consts(p, fw, l):
    row = lambda a: a[l].reshape(1, -1)
    return dict(
        g1=row(p["norm1_g"]), g2=row(p["norm2_g"]), wA=fw["conv_a_w"][l], bA=row(p["conv_a_b"]),
        wx=_blockdiag_chunks(p["lru_wx"][l]), bx=row(p["lru_bx"]), wa=_blockdiag_chunks(p["lru_wa"][l]),
        ba=row(p["lru_ba"]), lam=row(p["lru_lambda"]), wB=fw["conv_b_w"][l],
        ss=jnp.stack([p["sinks"][l], jnp.asarray(SLOPES)]), wD=fw["conv_d_w"][l], bD=row(p["conv_d_b"]),
        lg=row(p["ln_d_g"]), lb=row(p["ln_d_b"]))


def _layer_fwd(x, c, fw, l):
    t = f"l{l}_"
    xn = _rms_fwd(x, c["g1"], t + "rms1")
    proj = _mm(xn, fw["w_in"][l], "nn", t + "proj")
    ya = _a_fwd(proj, c["wA"], c["bA"], c["wx"], c["bx"], c["wa"], c["ba"], c["lam"], t + "a_fwd")
    yb = _b_fwd(proj, c["wB"], t + "b_fwd")
    q3 = _heads(proj[:, OFF_Q:OFF_K], N_Q)
    k3 = _heads(proj[:, OFF_K:OFF_V], N_KV)
    v3 = _heads(proj[:, OFF_V:OFF_V + N_KV * HEAD_DIM], N_KV)
    yc = _unheads(_attn_fwd(q3, k3, v3, c["ss"], t + "attn_fwd"))
    cd = _d_conv_fwd(proj, c["wD"], c["bD"], t + "d_conv_fwd")
    yd = _ln_silu_fwd(cd, c["lg"], c["lb"], t + "d_ln_fwd")
    ys = (ya, yb, yc, yd)
    wouts = [fw[n][l] for n in ("w_a_out", "w_b_out", "w_c_out", "w_d_out")]
    big_y = tuple(_mm(y, w, "nn", t + f"out{i}") for i, (y, w) in enumerate(zip(ys, wouts)))
    merged = _merge_fwd(proj, big_y, t + "merge_fwd")
    hres = _mm(merged, fw["w_o"][l], "nn", t + "wo", add=x)
    hn = _rms_fwd(hres, c["g2"], t + "rms2")
    gg = _mm(hn, fw["w_ffn_gate"][l], "nn", t + "ffn_gate")
    uu = _mm(hn, fw["w_ffn_up"][l], "nn", t + "ffn_up")
    act = _swiglu_fwd(gg, uu, t + "swiglu_fwd")
    xout = _mm(act, fw["w_ffn_down"][l], "nn", t + "ffn_down", add=hres)
    saved = dict(x=x, xn=xn, proj=proj, ys=ys, q3=q3, k3=k3, v3=v3, cd=cd, big_y=big_y, merged=merged, hres=hres,
                 hn=hn, gg=gg, uu=uu, act=act)
    return xout, saved


def _layer_bwd(dxout, s, c, fw, l):
    t = f"l{l}_"
    gw, gs = {}, {}
    dact = _mm(dxout, fw["w_ffn_down"][l], "nt", t + "d_act")
    gw["w_ffn_down"] = _mm(s["act"], dxout, "tn", t + "dw_down", BF16)
    dgg, duu = _swiglu_bwd(s["gg"], s["uu"], dact, t + "swiglu_bwd")
    gw["w_ffn_gate"] = _mm(s["hn"], dgg, "tn", t + "dw_gate", BF16)
    gw["w_ffn_up"] = _mm(s["hn"], duu, "tn", t + "dw_up", BF16)
    dhn = _mm(dgg, fw["w_ffn_gate"][l], "nt", t + "d_hn_g")
    dhn = _mm(duu, fw["w_ffn_up"][l], "nt", t + "d_hn_u", add=dhn)
    dhres, gs["norm2_g"] = _rms_bwd(s["hres"], c["g2"], dhn, dxout, t + "rms2_bwd")
    dmerged = _mm(dhres, fw["w_o"][l], "nt", t + "d_merged")
    gw["w_o"] = _mm(s["merged"], dhres, "tn", t + "dw_o", BF16)
    dbig_y, dgl = _merge_bwd(s["proj"], s["big_y"], dmerged, t + "merge_bwd")
    dys = []
    for i, n in enumerate(("w_a_out", "w_b_out", "w_c_out", "w_d_out")):
        gw[n] = _mm(s["ys"][i], dbig_y[i], "tn", t + f"dw_out{i}", BF16)
        dys.append(_mm(dbig_y[i], fw[n][l], "nt", t + f"d_y{i}"))
    proj = s["proj"]
    (dax, dag, gs["conv_a_w"], gs["conv_a_b"], dwx, gs["lru_bx"], dwa, gs["lru_ba"], gs["lru_lambda"]) = _a_bwd(
        proj, dys[0], c["wA"], c["bA"], c["wx"], c["bx"], c["wa"], c["ba"], c["lam"], t + "a_bwd")
    gs["lru_wx"] = _blockdiag_extract(dwx)
    gs["lru_wa"] = _blockdiag_extract(dwa)
    dbv, dbc, dbb, gs["conv_b_w"] = _b_bwd(proj, dys[1], c["wB"], t + "b_bwd")
    dq3, dk3, dv3, dsink = _attn_bwd(s["q3"], s["k3"], s["v3"], _heads(dys[2], N_Q), c["ss"], t + "attn_bwd")
    gs["sinks"] = dsink[:, 0, 0]
    dcd, gs["ln_d_g"], gs["ln_d_b"] = _ln_silu_bwd(s["cd"], c["lg"], c["lb"], dys[3], t + "d_ln_bwd")
    dd1, dd2, gs["conv_d_w"], gs["conv_d_b"] = _d_conv_bwd(proj, dcd, c["wD"], t + "d_conv_bwd")
    dproj = jnp.concatenate(
        [dax, dag, dbv, dbc, dbb, _unheads(dq3), _unheads(dk3).astype(BF16), _unheads(dv3).astype(BF16), dd1, dd2,
         *dgl], axis=1)
    gw["w_in"] = _mm(s["xn"], dproj, "tn", t + "dw_in", BF16)
    dxn = _mm(dproj, fw["w_in"][l], "nt", t + "d_xn")
    dx, gs["norm1_g"] = _rms_bwd(s["x"], c["g1"], dxn, dhres, t + "rms1_bwd")
    return dx, gw, gs


def kernel(x, norm1_g, w_in, conv_a_w, conv_a_b, lru_wx, lru_bx, lru_wa, lru_ba, lru_lambda, w_a_out, conv_b_w, w_b_out, sinks, w_c_out, conv_d_w, conv_d_b, ln_d_g, ln_d_b, w_d_out, w_o, norm2_g, w_ffn_gate, w_ffn_up, w_ffn_down, final_g, loss_target, m_norm1_g, m_w_in, m_conv_a_w, m_conv_a_b, m_lru_wx, m_lru_bx, m_lru_wa, m_lru_ba, m_lru_lambda, m_w_a_out, m_conv_b_w, m_w_b_out, m_sinks, m_w_c_out, m_conv_d_w, m_conv_d_b, m_ln_d_g, m_ln_d_b, m_w_d_out, m_w_o, m_norm2_g, m_w_ffn_gate, m_w_ffn_up, m_w_ffn_down, m_final_g, v_norm1_g, v_w_in, v_conv_a_w, v_conv_a_b, v_lru_wx, v_lru_bx, v_lru_wa, v_lru_ba, v_lru_lambda, v_w_a_out, v_conv_b_w, v_w_b_out, v_sinks, v_w_c_out, v_conv_d_w, v_conv_d_b, v_ln_d_g, v_ln_d_b, v_w_d_out, v_w_o, v_norm2_g, v_w_ffn_gate, v_w_ffn_up, v_w_ffn_down, v_final_g):
    given = dict(locals())
    p = {n: given[n] for n in NAMES}
    mom = {n: given["m_" + n] for n in NAMES}
    var = {n: given["v_" + n] for n in NAMES}
    cx, cy, cc = _coords()
    chip = 2 * cx + cy

    fw = _unpack_full_weights(_allgather8(_pack_weight_shards(p), "gather_weights", pl.ANY))

    h = x[0]
    consts, saved = [], []
    for l in range(DEPTH):
        consts.append(_layer_consts(p, fw, l))
        h, s = _layer_fwd(h, consts[l], fw, l)
        saved.append(s)
    loss_vec, dh, g_final = _loss_head(h, final_g.reshape(1, -1), loss_target[0], "loss_head")
    loss = lax.psum(loss_vec[0, 0], ("x", "y", "c"))

    gws, gss = [None] * DEPTH, [None] * DEPTH
    for l in reversed(range(DEPTH)):
        dh, gws[l], gss[l] = _layer_bwd(dh, saved[l], consts[l], fw, l)
    grad_x = dh[None]

    packed = _pack_full_grads({n: jnp.stack([gws[l][n] for l in range(DEPTH)]) for n in BIG})
    from_sibling = _swap_sibling_half(packed, "grads_swap_sibling")
    chip_sum = _sum_own_plus(packed.reshape(2, 4 * PACK_M, PACK_W), cc.reshape(1).astype(jnp.int32),
                             from_sibling.reshape(1, 4 * PACK_M, PACK_W), "grads_sum_chip", BF16)
    chip_sum = chip_sum.reshape(4, PACK_M, PACK_W)
    from_chips = _scatter_chips(chip_sum, "grads_scatter_chips")
    half = _sum_own_plus(chip_sum, chip.reshape(1).astype(jnp.int32), from_chips, "grads_sum_all", F32)
    reduced = _join_sibling(half, "grads_join_sibling").reshape(PACK_R, PACK_W)
    g = _unpack_shard_grads(reduced, p)

    small_full = {n: (g_final.reshape(-1) if n == "final_g" else
                      jnp.stack([gss[l][n].reshape(gss[l][n].shape[-2:] if n.startswith("conv") and n.endswith("_w")
                                                   else p[n].shape[1:]) for l in range(DEPTH)]))
                  for n in SMALL}
    part = _flat_pack([small_full[n] for n in SMALL])
    rows = part.shape[0]
    gathered = _allgather8(jnp.stack([part, part]), "gather_small_grads", pltpu.VMEM)
    small_sum = _flat_unpack(_sum_leading(gathered, "small_grads_sum"), [small_full[n].shape for n in SMALL])
    for n, a in zip(SMALL, small_sum):
        g[n] = lax.dynamic_slice_in_dim(a, chip * LANE, LANE, axis=2) if n in CONV_NAMES else a

    delta, new_m, new_v = {}, {}, {}
    for n in BIG:
        shp = p[n].shape
        two_d = lambda a: a.reshape(-1, shp[-1])
        d, nm, nv = _adamw(two_d(p[n]), two_d(g[n]), two_d(mom[n]), two_d(var[n]), "adamw_" + n)
        delta[n], new_m[n], new_v[n] = d.reshape(shp), nm.reshape(shp), nv.reshape(shp)
    shapes = [p[n].shape for n in SMALL]
    d, nm, nv = _adamw(_flat_pack([p[n] for n in SMALL]), _flat_pack([g[n] for n in SMALL]),
                       _flat_pack([mom[n] for n in SMALL]), _flat_pack([var[n] for n in SMALL]), "adamw_small")
    for n, a, b, cval in zip(SMALL, _flat_unpack(d, shapes), _flat_unpack(nm, shapes), _flat_unpack(nv, shapes)):
        delta[n], new_m[n], new_v[n] = a, b, cval

    return (loss, grad_x, *[g[n] for n in NAMES], *[delta[n] for n in NAMES], *[new_m[n] for n in NAMES],
            *[new_v[n] for n in NAMES])
```

```python
import functools
import math

import numpy as np
import jax
import jax.numpy as jnp
from jax import lax
from jax.experimental import pallas as pl
from jax.experimental.pallas import tpu as pltpu

F32 = jnp.float32
BF16 = jnp.bfloat16
MESH = pl.DeviceIdType.MESH

D_MODEL = 1024
DEPTH = 2
BW = 512
HEAD_DIM = 64
N_Q = 8
N_KV = 2
BLK = 128
D_FF = 2816
IN_W = 8448
EPS = 1e-6
NEG_INF = -1e30
LRU_C = 8.0
CONV_A, CONV_B, CONV_D = 4, 3, 31
LANE = 128
ROW_TILE = 256
VMEM_LIMIT = 56 * 1024 * 1024

C_AX, C_AG, C_BV, C_BC, C_BB = 0, 4, 8, 12, 16
OFF_Q, OFF_K, OFF_V = 2560, 3072, 3200
C_D1, C_D2 = 26, 30
OFF_GL = 4352

ADAM_LR, ADAM_B1, ADAM_B2, ADAM_EPS, ADAM_WD, ADAM_STEP = 0.001, 0.9, 0.999, 1e-08, 0.01, 10

ARENA_W = 1024
R_DOWN, R_GATE, R_UP, R_IN, R_O, R_OUT = 0, 2816, 5632, 8448, 16896, 17920
ARENA_ROWS = 19968
ROW_REGIONS = ((R_DOWN, 704), (R_GATE, 704), (R_UP, 704), (R_IN, 2112), (R_O, 256))
PIECE_OFF = (0, 704, 1408, 2112, 4224)
PIECE_ROWS = 4480
OUT_ROWS, OUT_COLS = 4 * BW, D_MODEL // 4

BIG = ("w_in", "w_a_out", "w_b_out", "w_c_out", "w_d_out", "w_o", "w_ffn_gate", "w_ffn_up", "w_ffn_down")
CONV_NAMES = ("conv_a_w", "conv_b_w", "conv_d_w")
N_TAPS = CONV_A + CONV_B + CONV_D
SMALL = ("norm1_g", "conv_a_w", "conv_a_b", "lru_wx", "lru_bx", "lru_wa", "lru_ba", "lru_lambda", "conv_b_w",
         "sinks", "conv_d_w", "conv_d_b", "ln_d_g", "ln_d_b", "norm2_g", "final_g")
NAMES = ['norm1_g', 'w_in', 'conv_a_w', 'conv_a_b', 'lru_wx', 'lru_bx', 'lru_wa', 'lru_ba', 'lru_lambda', 'w_a_out',
         'conv_b_w', 'w_b_out', 'sinks', 'w_c_out', 'conv_d_w', 'conv_d_b', 'ln_d_g', 'ln_d_b', 'w_d_out', 'w_o',
         'norm2_g', 'w_ffn_gate', 'w_ffn_up', 'w_ffn_down', 'final_g']


def _pick(n, cands, off=0):
    for c in cands:
        if n % c == 0 and off % c == 0:
            return c
    assert off == 0, (n, off)
    return n


class Win:
    def __init__(self, arena, l, off, rows):
        self.arena, self.l, self.off, self.rows = arena, l, off, rows
        self.shape = (rows, arena.shape[2])


def _params(sem=None):
    return pltpu.CompilerParams(dimension_semantics=sem, vmem_limit_bytes=VMEM_LIMIT)


def _sig(z):
    return 1.0 / (1.0 + jnp.exp(-z))


def _dot(a, b, dims):
    return lax.dot_general(a.astype(BF16), b.astype(BF16), (dims, ((), ())), preferred_element_type=F32)


NN = ((1,), (0,))
NT = ((1,), (1,))
TN = ((0,), (0,))


def _mm(a, b, mode, name, out_dtype=F32, add=None, out=None):
    if mode == "nn":
        (m, k), n = a.shape, b.shape[1]
    elif mode == "nt":
        (m, k), n = a.shape, b.shape[0]
    else:
        (k, m), n = a.shape, b.shape[1]
    b_win = isinstance(b, Win)
    b_off = b.off if b_win else 0
    o_off = out.off if out is not None else 0
    tm = _pick(m, (768, 512, 256, 128), o_off)
    tn = _pick(n, (768, 512, 256, 128), b_off if mode == "nt" else 0)
    tk = _pick(k, (2816, 2048, 1408, 1024, 768, 512, 256), b_off if mode != "nt" else 0)
    nk = k // tk
    dims = {"nn": NN, "nt": NT, "tn": TN}[mode]
    if out is not None:
        out_dtype = out.arena.dtype

    def body(*refs):
        a_ref, b_ref = refs[:2]
        c_ref = refs[2] if add is not None else None
        o_ref, acc = refs[-2:]
        kk = pl.program_id(2)

        @pl.when(kk == 0)
        def _():
            acc[...] = jnp.zeros_like(acc)

        acc[...] += _dot(a_ref[...], b_ref[...], dims)

        @pl.when(kk == nk - 1)
        def _():
            r = acc[...]
            if add is not None:
                r = r + c_ref[...]
            o_ref[...] = r.astype(out_dtype)

    if mode == "tn":
        a_spec = pl.BlockSpec((tk, tm), lambda i, j, q: (q, i))
    else:
        a_spec = pl.BlockSpec((tm, tk), lambda i, j, q: (i, q))
    if mode == "nt":
        b_blk, b_idx = (tn, tk), (lambda i, j, q: (b_off // tn + j, q))
    else:
        b_blk, b_idx = (tk, tn), (lambda i, j, q: (b_off // tk + q, j))
    if b_win:
        bl = b.l
        b_spec = pl.BlockSpec((None,) + b_blk, lambda i, j, q: (bl,) + b_idx(i, j, q))
    else:
        b_spec = pl.BlockSpec(b_blk, b_idx)
    plain_o = pl.BlockSpec((tm, tn), lambda i, j, q: (i, j))
    in_specs = [a_spec, b_spec] + ([plain_o] if add is not None else [])
    args = (a, b.arena if b_win else b) + ((add,) if add is not None else ())
    aliases = {}
    if out is None:
        o_spec, o_shape = plain_o, jax.ShapeDtypeStruct((m, n), out_dtype)
    else:
        ol = out.l
        o_spec = pl.BlockSpec((None, tm, tn), lambda i, j, q: (ol, o_off // tm + i, j))
        o_shape = jax.ShapeDtypeStruct(out.arena.shape, out_dtype)
        aliases = {len(args): 0}
        in_specs.append(pl.BlockSpec(memory_space=pl.ANY))
        args = args + (out.arena,)
    return pl.pallas_call(
        body, name=name, out_shape=o_shape,
        grid=(m // tm, n // tn, nk), in_specs=in_specs, out_specs=o_spec,
        scratch_shapes=[pltpu.VMEM((tm, tn), F32)], input_output_aliases=aliases,
        compiler_params=_params(("parallel", "parallel", "arbitrary")),
    )(*args)


def _row_spec(cols, tr=ROW_TILE):
    return pl.BlockSpec((tr, cols), lambda i: (i, 0))


def _vec_spec(cols):
    return pl.BlockSpec((1, cols), lambda i: (0, 0))


def _rms_fwd(x, g, name):
    t, d = x.shape

    def body(x_ref, g_ref, o_ref):
        xv = x_ref[...]
        r = lax.rsqrt(jnp.mean(xv * xv, axis=1, keepdims=True) + EPS)
        o_ref[...] = (xv * r * g_ref[...]).astype(BF16)

    return pl.pallas_call(
        body, name=name, out_shape=jax.ShapeDtypeStruct((t, d), BF16), grid=(t // ROW_TILE,),
        in_specs=[_row_spec(d), _vec_spec(d)], out_specs=_row_spec(d), compiler_params=_params(("parallel",)),
    )(x, g)


def _rms_bwd(x, g, dxn, dres, name):
    t, d = x.shape

    def body(x_ref, g_ref, dy_ref, dr_ref, dx_ref, dg_ref):
        @pl.when(pl.program_id(0) == 0)
        def _():
            dg_ref[...] = jnp.zeros_like(dg_ref)

        xv = x_ref[...]
        dy = dy_ref[...]
        r = lax.rsqrt(jnp.mean(xv * xv, axis=1, keepdims=True) + EPS)
        w = dy * g_ref[...]
        dx_ref[...] = dr_ref[...] + r * w - xv * (r * r * r) * jnp.mean(w * xv, axis=1, keepdims=True)
        dg_ref[...] += jnp.sum(dy * xv * r, axis=0, keepdims=True)

    return pl.pallas_call(
        body, name=name,
        out_shape=(jax.ShapeDtypeStruct((t, d), F32), jax.ShapeDtypeStruct((1, d), F32)), grid=(t // ROW_TILE,),
        in_specs=[_row_spec(d), _vec_spec(d), _row_spec(d), _row_spec(d)], out_specs=(_row_spec(d), _vec_spec(d)),
        compiler_params=_params(("arbitrary",)),
    )(x, g, dxn, dres)


def _loss_head(x, g, tgt, name):
    t, d = x.shape

    def body(x_ref, g_ref, t_ref, loss_ref, dx_ref, dg_ref):
        @pl.when(pl.program_id(0) == 0)
        def _():
            dg_ref[...] = jnp.zeros_like(dg_ref)
            loss_ref[...] = jnp.zeros_like(loss_ref)

        xv = x_ref[...]
        gv = g_ref[...]
        r = lax.rsqrt(jnp.mean(xv * xv, axis=1, keepdims=True) + EPS)
        e = xv * r * gv - t_ref[...]
        loss_ref[...] += jnp.full(loss_ref.shape, (0.5 / d) * jnp.sum(e * e), F32)
        dy = e * (1.0 / d)
        w = dy * gv
        dx_ref[...] = r * w - xv * (r * r * r) * jnp.mean(w * xv, axis=1, keepdims=True)
        dg_ref[...] += jnp.sum(dy * xv * r, axis=0, keepdims=True)

    return pl.pallas_call(
        body, name=name,
        out_shape=(jax.ShapeDtypeStruct((1, LANE), F32), jax.ShapeDtypeStruct((t, d), F32),
                   jax.ShapeDtypeStruct((1, d), F32)),
        grid=(t // ROW_TILE,), in_specs=[_row_spec(d), _vec_spec(d), _row_spec(d)],
        out_specs=(_vec_spec(LANE), _row_spec(d), _vec_spec(d)), compiler_params=_params(("arbitrary",)),
    )(x, g, tgt)


def _swiglu_fwd(gg, uu, name):
    t, f = gg.shape

    def body(g_ref, u_ref, o_ref):
        gv = g_ref[...]
        o_ref[...] = (gv * _sig(gv) * u_ref[...]).astype(BF16)

    return pl.pallas_call(
        body, name=name, out_shape=jax.ShapeDtypeStruct((t, f), BF16), grid=(t // ROW_TILE,),
        in_specs=[_row_spec(f), _row_spec(f)], out_specs=_row_spec(f), compiler_params=_params(("parallel",)),
    )(gg, uu)


def _swiglu_bwd(gg, uu, dact, name):
    t, f = gg.shape

    def body(g_ref, u_ref, d_ref, dg_ref, du_ref):
        gv = g_ref[...]
        dv = d_ref[...]
        s = _sig(gv)
        dg_ref[...] = (dv * u_ref[...] * s * (1.0 + gv * (1.0 - s))).astype(BF16)
        du_ref[...] = (dv * gv * s).astype(BF16)

    return pl.pallas_call(
        body, name=name,
        out_shape=(jax.ShapeDtypeStruct((t, f), BF16), jax.ShapeDtypeStruct((t, f), BF16)), grid=(t // ROW_TILE,),
        in_specs=[_row_spec(f)] * 3, out_specs=(_row_spec(f), _row_spec(f)), compiler_params=_params(("parallel",)),
    )(gg, uu, dact)


MERGE_COLS = 256


def _gate_specs():
    nb = D_MODEL // MERGE_COLS
    base = OFF_GL // MERGE_COLS
    return [pl.BlockSpec((ROW_TILE, MERGE_COLS), functools.partial(lambda i, j, kk: (i, base + nb * kk + j), kk=kk))
            for kk in range(4)]


def _merge_fwd(proj, ys, name):
    t = proj.shape[0]
    yspec = pl.BlockSpec((ROW_TILE, MERGE_COLS), lambda i, j: (i, j))

    def body(g0, g1, g2, g3, y0, y1, y2, y3, o_ref):
        acc = _sig(g0[...]) * y0[...]
        acc += _sig(g1[...]) * y1[...]
        acc += _sig(g2[...]) * y2[...]
        acc += _sig(g3[...]) * y3[...]
        o_ref[...] = acc.astype(BF16)

    return pl.pallas_call(
        body, name=name, out_shape=jax.ShapeDtypeStruct((t, D_MODEL), BF16),
        grid=(t // ROW_TILE, D_MODEL // MERGE_COLS), in_specs=_gate_specs() + [yspec] * 4, out_specs=yspec,
        compiler_params=_params(("parallel", "parallel")),
    )(proj, proj, proj, proj, *ys)


def _merge_bwd(proj, ys, dmerged, name):
    t = proj.shape[0]
    yspec = pl.BlockSpec((ROW_TILE, MERGE_COLS), lambda i, j: (i, j))

    def body(g0, g1, g2, g3, y0, y1, y2, y3, dm_ref, *outs):
        dm = dm_ref[...]
        for gr, yr, dy_ref, dg_ref in zip((g0, g1, g2, g3), (y0, y1, y2, y3), outs[:4], outs[4:]):
            s = _sig(gr[...])
            dy_ref[...] = (dm * s).astype(BF16)
            dg_ref[...] = (dm * yr[...] * s * (1.0 - s)).astype(BF16)

    shp = jax.ShapeDtypeStruct((t, D_MODEL), BF16)
    outs = pl.pallas_call(
        body, name=name, out_shape=(shp,) * 8, grid=(t // ROW_TILE, D_MODEL // MERGE_COLS),
        in_specs=_gate_specs() + [yspec] * 5, out_specs=(yspec,) * 8, compiler_params=_params(("parallel", "parallel")),
    )(proj, proj, proj, proj, *ys, dmerged)
    return outs[:4], outs[4:]


def _ln_silu_fwd(cd, g, b, name):
    t, c = cd.shape

    def body(x_ref, g_ref, b_ref, o_ref):
        xv = x_ref[...]
        mu = jnp.mean(xv, axis=1, keepdims=True)
        xc = xv - mu
        rs = lax.rsqrt(jnp.mean(xc * xc, axis=1, keepdims=True) + EPS)
        z = xc * rs * g_ref[...] + b_ref[...]
        o_ref[...] = (z * _sig(z)).astype(BF16)

    return pl.pallas_call(
        body, name=name, out_shape=jax.ShapeDtypeStruct((t, c), BF16), grid=(t // ROW_TILE,),
        in_specs=[_row_spec(c), _vec_spec(c), _vec_spec(c)], out_specs=_row_spec(c),
        compiler_params=_params(("parallel",)),
    )(cd, g, b)


def _ln_silu_bwd(cd, g, b, dy, name):
    t, c = cd.shape

    def body(x_ref, g_ref, b_ref, dy_ref, dx_ref, dg_ref, db_ref):
        @pl.when(pl.program_id(0) == 0)
        def _():
            dg_ref[...] = jnp.zeros_like(dg_ref)
            db_ref[...] = jnp.zeros_like(db_ref)

        xv = x_ref[...]
        gv = g_ref[...]
        mu = jnp.mean(xv, axis=1, keepdims=True)
        xc = xv - mu
        rs = lax.rsqrt(jnp.mean(xc * xc, axis=1, keepdims=True) + EPS)
        xh = xc * rs
        z = xh * gv + b_ref[...]
        s = _sig(z)
        dz = dy_ref[...] * s * (1.0 + z * (1.0 - s))
        dg_ref[...] += jnp.sum(dz * xh, axis=0, keepdims=True)
        db_ref[...] += jnp.sum(dz, axis=0, keepdims=True)
        dxh = dz * gv
        dx_ref[...] = rs * (dxh - jnp.mean(dxh, axis=1, keepdims=True) - xh * jnp.mean(dxh * xh, axis=1, keepdims=True))

    return pl.pallas_call(
        body, name=name,
        out_shape=(jax.ShapeDtypeStruct((t, c), F32), jax.ShapeDtypeStruct((1, c), F32),
                   jax.ShapeDtypeStruct((1, c), F32)),
        grid=(t // ROW_TILE,), in_specs=[_row_spec(c), _vec_spec(c), _vec_spec(c), _row_spec(c)],
        out_specs=(_row_spec(c), _vec_spec(c), _vec_spec(c)), compiler_params=_params(("arbitrary",)),
    )(cd, g, b, dy)


def _shift_dn(x, k):
    if k == 0:
        return x
    row = lax.broadcasted_iota(jnp.int32, x.shape, 0)
    return jnp.where(row >= k, pltpu.roll(x, k, 0), 0.0)


def _shift_up(x, k):
    if k == 0:
        return x
    t = x.shape[0]
    row = lax.broadcasted_iota(jnp.int32, x.shape, 0)
    return jnp.where(row < t - k, pltpu.roll(x, t - k, 0), 0.0)


def _conv_fwd(x, w_ref, taps):
    acc = w_ref[pl.ds(taps - 1, 1), :] * x
    for k in range(taps - 1):
        acc += w_ref[pl.ds(k, 1), :] * _shift_dn(x, taps - 1 - k)
    return acc


def _conv_bwd(x, dy, w_ref, dw_ref, taps):
    dx = w_ref[pl.ds(taps - 1, 1), :] * dy
    dw_ref[pl.ds(taps - 1, 1), :] = jnp.sum(dy * x, axis=0, keepdims=True)
    for k in range(taps - 1):
        s = taps - 1 - k
        dx += w_ref[pl.ds(k, 1), :] * _shift_up(dy, s)
        dw_ref[pl.ds(k, 1), :] = jnp.sum(dy * _shift_dn(x, s), axis=0, keepdims=True)
    return dx


def _scan_fwd(a, u):
    t = a.shape[0]
    k = 1
    while k < t:
        u = u + a * _shift_dn(u, k)
        if 2 * k < t:
            a = a * _shift_dn(a, k)
        k *= 2
    return u


def _scan_rev(a, u):
    t = a.shape[0]
    k = 1
    while k < t:
        u = u + a * _shift_up(u, k)
        if 2 * k < t:
            a = a * _shift_up(a, k)
        k *= 2
    return u


def _one_minus_exp(y):
    return jnp.where(y > -1e-3, -(y + 0.5 * y * y + (1.0 / 6.0) * y * y * y), 1.0 - jnp.exp(y))


GELU_C = math.sqrt(2.0 / math.pi)


def _gelu(x):
    th = jnp.tanh(GELU_C * (x + 0.044715 * x * x * x))
    return 0.5 * x * (1.0 + th), th


def _softplus(x):
    return jnp.maximum(x, 0.0) + jnp.log(1.0 + jnp.exp(-jnp.abs(x)))


def _chunk_spec(t, blk0):
    return pl.BlockSpec((t, LANE), functools.partial(lambda c, b: (0, b + c), b=blk0))


def _tap_spec(taps):
    return pl.BlockSpec((taps, LANE), lambda c: (0, c))


def _cvec_spec():
    return pl.BlockSpec((1, LANE), lambda c: (0, c))


def _cmat_spec():
    return pl.BlockSpec((1, LANE, LANE), lambda c: (c, 0, 0))


def _lru_forward(ax, wA_ref, bA_ref, wx_ref, bx_ref, wa_ref, ba_ref, lam_ref):
    ca = _conv_fwd(ax, wA_ref, CONV_A) + bA_ref[...]
    gi = _sig(_dot(ca, wx_ref[0], NN) + bx_ref[...])
    gr = _sig(_dot(ca, wa_ref[0], NN) + ba_ref[...])
    sp = _softplus(-lam_ref[...])
    la = -LRU_C * gr * sp
    a = jnp.exp(la)
    mult = jnp.sqrt(_one_minus_exp(2.0 * la))
    h = _scan_fwd(a, ca * gi * mult)
    return ca, gi, gr, sp, a, mult, h


def _a_fwd(proj, wA, bA, wx, bx, wa, ba, lam, name):
    t = proj.shape[0]

    def body(ax_ref, ag_ref, wA_ref, bA_ref, wx_ref, bx_ref, wa_ref, ba_ref, lam_ref, o_ref):
        h = _lru_forward(ax_ref[...], wA_ref, bA_ref, wx_ref, bx_ref, wa_ref, ba_ref, lam_ref)[-1]
        o_ref[...] = (h * _gelu(ag_ref[...])[0]).astype(BF16)

    return pl.pallas_call(
        body, name=name, out_shape=jax.ShapeDtypeStruct((t, BW), BF16), grid=(BW // LANE,),
        in_specs=[_chunk_spec(t, C_AX), _chunk_spec(t, C_AG), _tap_spec(CONV_A), _cvec_spec(), _cmat_spec(),
                  _cvec_spec(), _cmat_spec(), _cvec_spec(), _cvec_spec()],
        out_specs=_chunk_spec(t, 0), compiler_params=_params(("parallel",)),
    )(proj, proj, wA, bA, wx, bx, wa, ba, lam)


def _a_bwd(proj, dya, wA, bA, wx, bx, wa, ba, lam, name):
    t = proj.shape[0]

    def body(ax_ref, ag_ref, dy_ref, wA_ref, bA_ref, wx_ref, bx_ref, wa_ref, ba_ref, lam_ref,
             dax_ref, dag_ref, dwA_ref, dbA_ref, dwx_ref, dbx_ref, dwa_ref, dba_ref, dlam_ref):
        ax = ax_ref[...]
        ag = ag_ref[...]
        dy = dy_ref[...]
        ca, gi, gr, sp, a, mult, h = _lru_forward(ax, wA_ref, bA_ref, wx_ref, bx_ref, wa_ref, ba_ref, lam_ref)
        gel, th = _gelu(ag)
        dgel = 0.5 * (1.0 + th) + 0.5 * ag * (1.0 - th * th) * GELU_C * (1.0 + 3.0 * 0.044715 * ag * ag)
        dag_ref[...] = (dy * h * dgel).astype(BF16)
        s = _scan_rev(_shift_up(a, 1), dy * gel)
        da = s * _shift_dn(h, 1)
        dca = s * gi * mult
        dgi = s * ca * mult
        dmult = s * ca * gi
        dla = da * a - dmult * a * a / mult
        dgr = dla * (-LRU_C * sp)
        dsp = jnp.sum(dla * (-LRU_C * gr), axis=0, keepdims=True)
        dlam_ref[...] = -_sig(-lam_ref[...]) * dsp
        dzi = dgi * gi * (1.0 - gi)
        dzr = dgr * gr * (1.0 - gr)
        dbx_ref[...] = jnp.sum(dzi, axis=0, keepdims=True)
        dba_ref[...] = jnp.sum(dzr, axis=0, keepdims=True)
        dwx_ref[0] = _dot(ca, dzi, TN)
        dwa_ref[0] = _dot(ca, dzr, TN)
        dca += _dot(dzi, wx_ref[0], NT) + _dot(dzr, wa_ref[0], NT)
        dbA_ref[...] = jnp.sum(dca, axis=0, keepdims=True)
        dax_ref[...] = _conv_bwd(ax, dca, wA_ref, dwA_ref, CONV_A).astype(BF16)

    big = jax.ShapeDtypeStruct((t, BW), BF16)
    vec = jax.ShapeDtypeStruct((1, BW), F32)
    mat = jax.ShapeDtypeStruct((BW // LANE, LANE, LANE), F32)
    return pl.pallas_call(
        body, name=name,
        out_shape=(big, big, jax.ShapeDtypeStruct((CONV_A, BW), F32), vec, mat, vec, mat, vec, vec),
        grid=(BW // LANE,),
        in_specs=[_chunk_spec(t, C_AX), _chunk_spec(t, C_AG), _chunk_spec(t, 0), _tap_spec(CONV_A), _cvec_spec(),
                  _cmat_spec(), _cvec_spec(), _cmat_spec(), _cvec_spec(), _cvec_spec()],
        out_specs=(_chunk_spec(t, 0), _chunk_spec(t, 0), _tap_spec(CONV_A), _cvec_spec(), _cmat_spec(), _cvec_spec(),
                   _cmat_spec(), _cvec_spec(), _cvec_spec()),
        compiler_params=_params(("parallel",)),
    )(proj, proj, dya, wA, bA, wx, bx, wa, ba, lam)


def _b_fwd(proj, wB, name):
    t = proj.shape[0]

    def body(bv_ref, bc_ref, bb_ref, w_ref, o_ref):
        o_ref[...] = (bb_ref[...] * _conv_fwd(bc_ref[...] * bv_ref[...], w_ref, CONV_B)).astype(BF16)

    return pl.pallas_call(
        body, name=name, out_shape=jax.ShapeDtypeStruct((t, BW), BF16), grid=(BW // LANE,),
        in_specs=[_chunk_spec(t, C_BV), _chunk_spec(t, C_BC), _chunk_spec(t, C_BB), _tap_spec(CONV_B)],
        out_specs=_chunk_spec(t, 0), compiler_params=_params(("parallel",)),
    )(proj, proj, proj, wB)


def _b_bwd(proj, dyb, wB, name):
    t = proj.shape[0]

    def body(bv_ref, bc_ref, bb_ref, dy_ref, w_ref, dbv_ref, dbc_ref, dbb_ref, dw_ref):
        bv = bv_ref[...]
        bc = bc_ref[...]
        dy = dy_ref[...]
        p = bc * bv
        dbb_ref[...] = (dy * _conv_fwd(p, w_ref, CONV_B)).astype(BF16)
        dp = _conv_bwd(p, dy * bb_ref[...], w_ref, dw_ref, CONV_B)
        dbc_ref[...] = (dp * bv).astype(BF16)
        dbv_ref[...] = (dp * bc).astype(BF16)

    big = jax.ShapeDtypeStruct((t, BW), BF16)
    return pl.pallas_call(
        body, name=name, out_shape=(big, big, big, jax.ShapeDtypeStruct((CONV_B, BW), F32)), grid=(BW // LANE,),
        in_specs=[_chunk_spec(t, C_BV), _chunk_spec(t, C_BC), _chunk_spec(t, C_BB), _chunk_spec(t, 0),
                  _tap_spec(CONV_B)],
        out_specs=(_chunk_spec(t, 0),) * 3 + (_tap_spec(CONV_B),), compiler_params=_params(("parallel",)),
    )(proj, proj, proj, dyb, wB)


def _d_conv_fwd(proj, wD, bD, name):
    t = proj.shape[0]

    def body(d1_ref, d2_ref, w_ref, b_ref, o_ref):
        o_ref[...] = _conv_fwd(d1_ref[...] * _sig(d2_ref[...]), w_ref, CONV_D) + b_ref[...]

    return pl.pallas_call(
        body, name=name, out_shape=jax.ShapeDtypeStruct((t, BW), F32), grid=(BW // LANE,),
        in_specs=[_chunk_spec(t, C_D1), _chunk_spec(t, C_D2), _tap_spec(CONV_D), _cvec_spec()],
        out_specs=_chunk_spec(t, 0), compiler_params=_params(("parallel",)),
    )(proj, proj, wD, bD)


def _d_conv_bwd(proj, dcd, wD, name):
    t = proj.shape[0]

    def body(d1_ref, d2_ref, dy_ref, w_ref, dd1_ref, dd2_ref, dw_ref, db_ref):
        d1 = d1_ref[...]
        s = _sig(d2_ref[...])
        dy = dy_ref[...]
        db_ref[...] = jnp.sum(dy, axis=0, keepdims=True)
        dd = _conv_bwd(d1 * s, dy, w_ref, dw_ref, CONV_D)
        dd1_ref[...] = (dd * s).astype(BF16)
        dd2_ref[...] = (dd * d1 * s * (1.0 - s)).astype(BF16)

    big = jax.ShapeDtypeStruct((t, BW), BF16)
    return pl.pallas_call(
        body, name=name,
        out_shape=(big, big, jax.ShapeDtypeStruct((CONV_D, BW), F32), jax.ShapeDtypeStruct((1, BW), F32)),
        grid=(BW // LANE,),
        in_specs=[_chunk_spec(t, C_D1), _chunk_spec(t, C_D2), _chunk_spec(t, 0), _tap_spec(CONV_D)],
        out_specs=(_chunk_spec(t, 0), _chunk_spec(t, 0), _tap_spec(CONV_D), _cvec_spec()),
        compiler_params=_params(("parallel",)),
    )(proj, proj, dcd, wD)


SCALE = HEAD_DIM ** -0.5
GROUP = N_Q // N_KV


def _attn_probs(q_ref, k_ref, ss_ref, h, n):
    qi = lax.broadcasted_iota(jnp.int32, (BLK, BLK), 0)
    ki = lax.broadcasted_iota(jnp.int32, (BLK, BLK), 1)
    dist = (qi - ki).astype(F32)
    sink = ss_ref[0, h]
    slope = ss_ref[1, h]
    s0 = pl.multiple_of(n * BLK, BLK)
    sp = pl.multiple_of(jnp.maximum(n - 1, 0) * BLK, BLK)
    q = q_ref[0, pl.ds(s0, BLK), :]
    kc = k_ref[0, pl.ds(s0, BLK), :]
    kp = k_ref[0, pl.ds(sp, BLK), :]
    sc = jnp.where(ki <= qi, _dot(q, kc, NT) * SCALE - slope * dist, NEG_INF)
    first = jnp.where(n >= 1, 0, BLK)
    sv = jnp.where(ki > qi + first, _dot(q, kp, NT) * SCALE - slope * (dist + BLK), NEG_INF)
    m = jnp.maximum(jnp.maximum(jnp.max(sc, axis=1, keepdims=True), jnp.max(sv, axis=1, keepdims=True)), sink)
    pc = jnp.exp(sc - m)
    pp = jnp.exp(sv - m)
    ps = jnp.exp(sink - m)
    z = jnp.sum(pc, axis=1, keepdims=True) + jnp.sum(pp, axis=1, keepdims=True) + ps
    return s0, sp, q, kc, kp, pc, pp, ps, z


def _attn_specs(t):
    qs = pl.BlockSpec((1, t, HEAD_DIM), lambda h: (h, 0, 0))
    ks = pl.BlockSpec((1, t, HEAD_DIM), lambda h: (h // GROUP, 0, 0))
    ss = pl.BlockSpec(memory_space=pltpu.SMEM)
    return qs, ks, ss


def _attn_fwd(q, k, v, ss, name):
    t = q.shape[1]
    qs, ks, sspec = _attn_specs(t)

    def body(q_ref, k_ref, v_ref, ss_ref, o_ref):
        h = pl.program_id(0)

        def blk(n, carry):
            s0, sp, _, _, _, pc, pp, _, z = _attn_probs(q_ref, k_ref, ss_ref, h, n)
            o = _dot(pc, v_ref[0, pl.ds(s0, BLK), :], NN) + _dot(pp, v_ref[0, pl.ds(sp, BLK), :], NN)
            o_ref[0, pl.ds(s0, BLK), :] = (o / z).astype(BF16)
            return carry

        lax.fori_loop(0, t // BLK, blk, 0)

    return pl.pallas_call(
        body, name=name, out_shape=jax.ShapeDtypeStruct((N_Q, t, HEAD_DIM), BF16), grid=(N_Q,),
        in_specs=[qs, ks, ks, sspec], out_specs=qs, compiler_params=_params(("parallel",)),
    )(q, k, v, ss)


def _attn_bwd(q, k, v, do, ss, name):
    t = q.shape[1]
    qs, ks, sspec = _attn_specs(t)

    def body(q_ref, k_ref, v_ref, do_ref, ss_ref, dq_ref, dk_ref, dv_ref, ds_ref):
        h = pl.program_id(0)

        @pl.when(h % GROUP == 0)
        def _():
            dk_ref[...] = jnp.zeros_like(dk_ref)
            dv_ref[...] = jnp.zeros_like(dv_ref)

        def blk(n, dsink):
            s0, sp, q, kc, kp, pc, pp, ps, z = _attn_probs(q_ref, k_ref, ss_ref, h, n)
            rz = 1.0 / z
            pc = pc * rz
            pp = pp * rz
            do_b = do_ref[0, pl.ds(s0, BLK), :]
            dpc = _dot(do_b, v_ref[0, pl.ds(s0, BLK), :], NT)
            dpp = _dot(do_b, v_ref[0, pl.ds(sp, BLK), :], NT)
            delta = jnp.sum(pc * dpc, axis=1, keepdims=True) + jnp.sum(pp * dpp, axis=1, keepdims=True)
            dsc = pc * (dpc - delta)
            dsp = pp * (dpp - delta)
            dq_ref[0, pl.ds(s0, BLK), :] = ((_dot(dsc, kc, NN) + _dot(dsp, kp, NN)) * SCALE).astype(BF16)
            dk_ref[0, pl.ds(s0, BLK), :] += _dot(dsc, q, TN) * SCALE
            dk_ref[0, pl.ds(sp, BLK), :] += _dot(dsp, q, TN) * SCALE
            dv_ref[0, pl.ds(s0, BLK), :] += _dot(pc, do_b, TN)
            dv_ref[0, pl.ds(sp, BLK), :] += _dot(pp, do_b, TN)
            return dsink - ps * rz * delta

        dsink = lax.fori_loop(0, t // BLK, blk, jnp.zeros((BLK, 1), F32))
        ds_ref[...] = jnp.full(ds_ref.shape, jnp.sum(dsink), F32)

    kv = jax.ShapeDtypeStruct((N_KV, t, HEAD_DIM), F32)
    return pl.pallas_call(
        body, name=name,
        out_shape=(jax.ShapeDtypeStruct((N_Q, t, HEAD_DIM), BF16), kv, kv, jax.ShapeDtypeStruct((N_Q, 8, LANE), F32)),
        grid=(N_Q,), in_specs=[qs, ks, ks, qs, sspec],
        out_specs=(qs, ks, ks, pl.BlockSpec((1, 8, LANE), lambda h: (h, 0, 0))),
        compiler_params=_params(("arbitrary",)),
    )(q, k, v, do, ss)


def _heads(x2d, n):
    t = x2d.shape[0]
    return x2d.reshape(t, n, HEAD_DIM).transpose(1, 0, 2)


def _unheads(x3d):
    n, t, _ = x3d.shape
    return x3d.transpose(1, 0, 2).reshape(t, n * HEAD_DIM)


def _adamw(w, g, m, v, name):
    r, c = w.shape
    tr = _pick(r, (256, 128, 64, 32, 16, 8))
    spec = pl.BlockSpec((tr, c), lambda i: (i, 0))

    def body(w_ref, g_ref, m_ref, v_ref, d_ref, nm_ref, nv_ref):
        gv = g_ref[...]
        nm = ADAM_B1 * m_ref[...] + (1.0 - ADAM_B1) * gv
        nv = ADAM_B2 * v_ref[...] + (1.0 - ADAM_B2) * (gv * gv)
        m_hat = nm / (1.0 - ADAM_B1 ** ADAM_STEP)
        v_hat = nv / (1.0 - ADAM_B2 ** ADAM_STEP)
        d_ref[...] = -ADAM_LR * (m_hat / (jnp.sqrt(v_hat) + ADAM_EPS) + ADAM_WD * w_ref[...])
        nm_ref[...] = nm
        nv_ref[...] = nv

    shp = jax.ShapeDtypeStruct((r, c), F32)
    return pl.pallas_call(
        body, name=name, out_shape=(shp, shp, shp), grid=(r // tr,), in_specs=[spec] * 4, out_specs=(spec,) * 3,
        compiler_params=_params(("parallel",)),
    )(w, g, m, v)


def _sum_leading(x, name):
    n, r, c = x.shape
    tr = _pick(r, (256, 128, 64, 32, 16, 8))

    def body(x_ref, o_ref):
        acc = x_ref[0]
        for i in range(1, n):
            acc = acc + x_ref[i]
        o_ref[...] = acc

    return pl.pallas_call(
        body, name=name, out_shape=jax.ShapeDtypeStruct((r, c), F32), grid=(r // tr,),
        in_specs=[pl.BlockSpec((n, tr, c), lambda i: (0, i, 0))], out_specs=pl.BlockSpec((tr, c), lambda i: (i, 0)),
        compiler_params=_params(("parallel",)),
    )(x)


def _sum_own_plus(p, sel, recv, name, out_dtype):
    _, r, c = p.shape
    n = recv.shape[0]
    tr = _pick(r, (256, 128, 16))

    def body(sel_ref, p_ref, r_ref, o_ref):
        acc = p_ref[0].astype(F32)
        for i in range(n):
            acc = acc + r_ref[i].astype(F32)
        o_ref[...] = acc.astype(out_dtype)

    grid_spec = pltpu.PrefetchScalarGridSpec(
        num_scalar_prefetch=1, grid=(r // tr,),
        in_specs=[pl.BlockSpec((1, tr, c), lambda i, s: (s[0], i, 0)), pl.BlockSpec((n, tr, c), lambda i, s: (0, i, 0))],
        out_specs=pl.BlockSpec((tr, c), lambda i, s: (i, 0)))
    return pl.pallas_call(
        body, name=name, out_shape=jax.ShapeDtypeStruct((r, c), out_dtype), grid_spec=grid_spec,
        compiler_params=_params(("parallel",)),
    )(sel, p, recv)


def _coords():
    return lax.axis_index("x"), lax.axis_index("y"), lax.axis_index("c")


def _allgather8(x2, name, space):
    _, m, n = x2.shape

    def body(x_ref, out_ref, send_sems, recv_sems, local_sem):
        x, y, c = _coords()
        me, sibling = (x, y, c), (x, y, 1 - c)
        chips = [(1 - x, y), (x, 1 - y), (1 - x, 1 - y)]
        mine_src = x_ref.at[c]

        def rows(px, py, pc):
            return out_ref.at[4 * px + 2 * py + pc]

        def copy(k, block, to, src=None):
            return pltpu.make_async_remote_copy(
                src_ref=rows(*block) if src is None else src, dst_ref=rows(*block),
                send_sem=send_sems.at[k], recv_sem=recv_sems.at[k], device_id=to, device_id_type=MESH)

        mine = pltpu.make_async_copy(mine_src, rows(*me), local_sem)
        mine.start()
        first = [copy(0, me, sibling, src=mine_src)]
        first += [copy(1 + j, me, (*chip, c), src=mine_src) for j, chip in enumerate(chips)]
        for cp in first:
            cp.start()
        passed = [copy(4 + j, (*chip, c), sibling) for j, chip in enumerate(chips)]
        for j, chip in enumerate(chips):
            copy(1 + j, (*chip, c), me).wait_recv()
            passed[j].start()
        copy(0, sibling, me).wait_recv()
        for j, chip in enumerate(chips):
            copy(4 + j, (*chip, 1 - c), me).wait_recv()
        for cp in first + passed:
            cp.wait_send()
        mine.wait()

    return pl.pallas_call(
        body, name=name, out_shape=jax.ShapeDtypeStruct((8, m, n), x2.dtype),
        in_specs=[pl.BlockSpec(memory_space=space)], out_specs=pl.BlockSpec(memory_space=space),
        scratch_shapes=[pltpu.SemaphoreType.DMA((7,)), pltpu.SemaphoreType.DMA((7,)), pltpu.SemaphoreType.DMA],
        compiler_params=pltpu.CompilerParams(vmem_limit_bytes=VMEM_LIMIT),
    )(x2)


def _swap_sibling_half(p, name):
    def body(p_ref, out_ref, send_sem, recv_sem):
        x, y, c = _coords()
        cp = pltpu.make_async_remote_copy(
            src_ref=p_ref.at[1 - c], dst_ref=out_ref, send_sem=send_sem, recv_sem=recv_sem,
            device_id=(x, y, 1 - c), device_id_type=MESH)
        cp.start()
        cp.wait()

    return pl.pallas_call(
        body, name=name, out_shape=jax.ShapeDtypeStruct(p.shape[1:], p.dtype),
        in_specs=[pl.BlockSpec(memory_space=pl.ANY)], out_specs=pl.BlockSpec(memory_space=pl.ANY),
        scratch_shapes=[pltpu.SemaphoreType.DMA, pltpu.SemaphoreType.DMA],
    )(p)


N_REG = len(ROW_REGIONS) + 1


def _chip_window(ref, lead, r, j):
    view = ref if lead is None else ref.at[lead]
    if r < len(ROW_REGIONS):
        off, rows = ROW_REGIONS[r]
        return view.at[pl.ds(pl.multiple_of(off + j * rows, 16), rows), :]
    return view.at[pl.ds(R_OUT, OUT_ROWS), pl.ds(pl.multiple_of(j * OUT_COLS, LANE), OUT_COLS)]


def _gather_arena(shards, name):
    def body(*refs):
        srcs, out_ref, send_sems, recv_sems = refs[:N_REG], refs[N_REG], refs[N_REG + 1], refs[N_REG + 2]
        x, y, c = _coords()
        chips = [(1 - x, y), (x, 1 - y), (1 - x, 1 - y)]
        sibling = (x, y, 1 - c)

        def copy(r, k, layer, j, to, src=None):
            win = _chip_window(out_ref, layer, r, j)
            return pltpu.make_async_remote_copy(
                src_ref=win if src is None else src, dst_ref=win, send_sem=send_sems.at[r, k],
                recv_sem=recv_sems.at[r, k], device_id=to, device_id_type=MESH)

        first = [copy(r, k, c, 2 * x + y, (cx, cy, c), src=srcs[r].at[c])
                 for k, (cx, cy) in enumerate(chips) for r in range(N_REG)]
        for cp in first:
            cp.start()
        passed = []
        for k, (cx, cy) in enumerate(chips):
            for r in range(N_REG):
                copy(r, k, c, 2 * cx + cy, sibling).wait_recv()
                fwd = copy(r, 3 + k, c, 2 * cx + cy, sibling)
                fwd.start()
                passed.append(fwd)
        for k, (cx, cy) in enumerate(chips):
            for r in range(N_REG):
                copy(r, 3 + k, 1 - c, 2 * cx + cy, sibling).wait_recv()
        for cp in first + passed:
            cp.wait_send()

    return pl.pallas_call(
        body, name=name, out_shape=jax.ShapeDtypeStruct((DEPTH, ARENA_ROWS, ARENA_W), BF16),
        in_specs=[pl.BlockSpec(memory_space=pl.ANY)] * N_REG, out_specs=pl.BlockSpec(memory_space=pl.ANY),
        scratch_shapes=[pltpu.SemaphoreType.DMA((N_REG, 6)), pltpu.SemaphoreType.DMA((N_REG, 6))],
    )(*shards)


def _place_own(arena, shards, chip):
    for (off, rows), s in zip(ROW_REGIONS, shards[:-1]):
        arena = lax.dynamic_update_slice(arena, s, (0, off + chip * rows, 0))
    return lax.dynamic_update_slice(arena, shards[-1], (0, R_OUT, chip * OUT_COLS))


def _scatter_pieces(s, name):
    def body(s_ref, main_ref, outp_ref, send_sems, recv_sems):
        x, y, c = _coords()
        chips = [(1 - x, y), (x, 1 - y), (1 - x, 1 - y)]
        cps = []
        for k, (cx, cy) in enumerate(chips):
            for r in range(N_REG):
                if r < len(ROW_REGIONS):
                    dst = main_ref.at[k, pl.ds(PIECE_OFF[r], ROW_REGIONS[r][1]), :]
                else:
                    dst = outp_ref.at[k]
                cps.append(pltpu.make_async_remote_copy(
                    src_ref=_chip_window(s_ref, None, r, 2 * cx + cy), dst_ref=dst, send_sem=send_sems.at[r, k],
                    recv_sem=recv_sems.at[r, k], device_id=(cx, cy, c), device_id_type=MESH))
        for cp in cps:
            cp.start()
        for cp in cps:
            cp.wait()

    return pl.pallas_call(
        body, name=name,
        out_shape=(jax.ShapeDtypeStruct((3, PIECE_ROWS, ARENA_W), s.dtype),
                   jax.ShapeDtypeStruct((3, OUT_ROWS, OUT_COLS), s.dtype)),
        in_specs=[pl.BlockSpec(memory_space=pl.ANY)], out_specs=(pl.BlockSpec(memory_space=pl.ANY),) * 2,
        scratch_shapes=[pltpu.SemaphoreType.DMA((N_REG, 3)), pltpu.SemaphoreType.DMA((N_REG, 3))],
    )(s)


def _own_piece(s, chip):
    main = jnp.concatenate([lax.dynamic_slice(s, (off + chip * rows, 0), (rows, ARENA_W))
                            for off, rows in ROW_REGIONS])
    return main, lax.dynamic_slice(s, (R_OUT, chip * OUT_COLS), (OUT_ROWS, OUT_COLS))


def _swap_pair(a, b, name):
    def body(a_ref, b_ref, ra_ref, rb_ref, send_sems, recv_sems):
        x, y, c = _coords()
        cps = [pltpu.make_async_remote_copy(
            src_ref=s, dst_ref=d, send_sem=send_sems.at[i], recv_sem=recv_sems.at[i], device_id=(x, y, 1 - c),
            device_id_type=MESH) for i, (s, d) in enumerate(((a_ref, ra_ref), (b_ref, rb_ref)))]
        for cp in cps:
            cp.start()
        for cp in cps:
            cp.wait()

    return pl.pallas_call(
        body, name=name,
        out_shape=(jax.ShapeDtypeStruct(a.shape, a.dtype), jax.ShapeDtypeStruct(b.shape, b.dtype)),
        in_specs=[pl.BlockSpec(memory_space=pl.ANY)] * 2, out_specs=(pl.BlockSpec(memory_space=pl.ANY),) * 2,
        scratch_shapes=[pltpu.SemaphoreType.DMA((2,)), pltpu.SemaphoreType.DMA((2,))],
    )(a, b)


OUT_NAMES = ("w_a_out", "w_b_out", "w_c_out", "w_d_out")


def _arena_shards(w):
    t = lambda a: a.astype(BF16).transpose(0, 2, 1)
    return (w["w_ffn_down"].astype(BF16), t(w["w_ffn_gate"]), t(w["w_ffn_up"]), t(w["w_in"]), w["w_o"].astype(BF16),
            jnp.concatenate([w[n].astype(BF16) for n in OUT_NAMES], axis=1))


def _shard_grads(main, outp):
    t = lambda r: main[:, PIECE_OFF[r]:PIECE_OFF[r] + ROW_REGIONS[r][1]]
    g = dict(w_ffn_down=t(0), w_ffn_gate=t(1).transpose(0, 2, 1), w_ffn_up=t(2).transpose(0, 2, 1),
             w_in=t(3).transpose(0, 2, 1), w_o=t(4))
    for i, n in enumerate(OUT_NAMES):
        g[n] = outp[:, i * BW:(i + 1) * BW]
    return g


def _gather_taps(p, name):
    mine = jnp.concatenate([p[n] for n in CONV_NAMES], axis=1).reshape(DEPTH * N_TAPS, LANE)
    rows = -(-mine.shape[0] // 8) * 8
    mine = jnp.concatenate([mine, jnp.zeros((rows - mine.shape[0], LANE), F32)])
    g = _allgather8(jnp.stack([mine, mine]), name, pltpu.VMEM)[0::2, :DEPTH * N_TAPS]
    full = g.reshape(4, DEPTH, N_TAPS, LANE).transpose(1, 2, 0, 3).reshape(DEPTH, N_TAPS, BW)
    return dict(conv_a_w=full[:, :CONV_A], conv_b_w=full[:, CONV_A:CONV_A + CONV_B], conv_d_w=full[:, CONV_A + CONV_B:])


def _flat_pack(arrs):
    flat = jnp.concatenate([a.reshape(-1).astype(F32) for a in arrs])
    rows = -(-flat.shape[0] // (8 * LANE)) * 8
    return jnp.concatenate([flat, jnp.zeros((rows * LANE - flat.shape[0],), F32)]).reshape(rows, LANE)


def _flat_unpack(packed, shapes):
    flat, out, off = packed.reshape(-1), [], 0
    for s in shapes:
        cnt = int(np.prod(s))
        out.append(flat[off:off + cnt].reshape(s))
        off += cnt
    return out


def _blockdiag_chunks(w):
    w4 = w.reshape(4, 2, 64, 64)
    z = jnp.zeros((4, 2, 64, 2, 64), F32)
    z = z.at[:, 0, :, 0, :].set(w4[:, 0]).at[:, 1, :, 1, :].set(w4[:, 1])
    return z.reshape(4, LANE, LANE)


def _blockdiag_extract(d):
    d5 = d.reshape(4, 2, 64, 2, 64)
    return jnp.stack([d5[:, 0, :, 0, :], d5[:, 1, :, 1, :]], axis=1).reshape(8, 64, 64)


SLOPES = np.asarray([2.0 ** (-8.0 * (i + 1) / N_Q) for i in range(N_Q)], np.float32)


def _layer_consts(p, fw, l):
    row = lambda a: a[l].reshape(1, -1)
    return dict(
        g1=row(p["norm1_g"]), g2=row(p["norm2_g"]), wA=fw["conv_a_w"][l], bA=row(p["conv_a_b"]),
        wx=_blockdiag_chunks(p["lru_wx"][l]), bx=row(p["lru_bx"]), wa=_blockdiag_chunks(p["lru_wa"][l]),
        ba=row(p["lru_ba"]), lam=row(p["lru_lambda"]), wB=fw["conv_b_w"][l],
        ss=jnp.stack([p["sinks"][l], jnp.asarray(SLOPES)]), wD=fw["conv_d_w"][l], bD=row(p["conv_d_b"]),
        lg=row(p["ln_d_g"]), lb=row(p["ln_d_b"]))


def _layer_fwd(x, c, fw, l):
    t = f"l{l}_"
    xn = _rms_fwd(x, c["g1"], t + "rms1")
    wt = lambda off, rows: Win(fw["arena"], l, off, rows)
    proj = _mm(xn, wt(R_IN, IN_W), "nt", t + "proj")
    ya = _a_fwd(proj, c["wA"], c["bA"], c["wx"], c["bx"], c["wa"], c["ba"], c["lam"], t + "a_fwd")
    yb = _b_fwd(proj, c["wB"], t + "b_fwd")
    q3 = _heads(proj[:, OFF_Q:OFF_K], N_Q)
    k3 = _heads(proj[:, OFF_K:OFF_V], N_KV)
    v3 = _heads(proj[:, OFF_V:OFF_V + N_KV * HEAD_DIM], N_KV)
    yc = _unheads(_attn_fwd(q3, k3, v3, c["ss"], t + "attn_fwd"))
    cd = _d_conv_fwd(proj, c["wD"], c["bD"], t + "d_conv_fwd")
    yd = _ln_silu_fwd(cd, c["lg"], c["lb"], t + "d_ln_fwd")
    ys = (ya, yb, yc, yd)
    big_y = tuple(_mm(y, wt(R_OUT + i * BW, BW), "nn", t + f"out{i}") for i, y in enumerate(ys))
    merged = _merge_fwd(proj, big_y, t + "merge_fwd")
    hres = _mm(merged, wt(R_O, D_MODEL), "nn", t + "wo", add=x)
    hn = _rms_fwd(hres, c["g2"], t + "rms2")
    gg = _mm(hn, wt(R_GATE, D_FF), "nt", t + "ffn_gate")
    uu = _mm(hn, wt(R_UP, D_FF), "nt", t + "ffn_up")
    act = _swiglu_fwd(gg, uu, t + "swiglu_fwd")
    xout = _mm(act, wt(R_DOWN, D_FF), "nn", t + "ffn_down", add=hres)
    saved = dict(x=x, xn=xn, proj=proj, ys=ys, q3=q3, k3=k3, v3=v3, cd=cd, big_y=big_y, merged=merged, hres=hres,
                 hn=hn, gg=gg, uu=uu, act=act)
    return xout, saved


def _layer_bwd(dxout, s, c, fw, l, ga):
    t = f"l{l}_"
    gs = {}
    wt = lambda off, rows: Win(fw["arena"], l, off, rows)
    gt = lambda off, rows: Win(ga, l, off, rows)
    dact = _mm(dxout, wt(R_DOWN, D_FF), "nt", t + "d_act")
    ga = _mm(s["act"], dxout, "tn", t + "dw_down", out=gt(R_DOWN, D_FF))
    dgg, duu = _swiglu_bwd(s["gg"], s["uu"], dact, t + "swiglu_bwd")
    ga = _mm(dgg, s["hn"], "tn", t + "dw_gate", out=gt(R_GATE, D_FF))
    ga = _mm(duu, s["hn"], "tn", t + "dw_up", out=gt(R_UP, D_FF))
    dhn = _mm(dgg, wt(R_GATE, D_FF), "nn", t + "d_hn_g")
    dhn = _mm(duu, wt(R_UP, D_FF), "nn", t + "d_hn_u", add=dhn)
    dhres, gs["norm2_g"] = _rms_bwd(s["hres"], c["g2"], dhn, dxout, t + "rms2_bwd")
    dmerged = _mm(dhres, wt(R_O, D_MODEL), "nt", t + "d_merged")
    ga = _mm(s["merged"], dhres, "tn", t + "dw_o", out=gt(R_O, D_MODEL))
    dbig_y, dgl = _merge_bwd(s["proj"], s["big_y"], dmerged, t + "merge_bwd")
    dys = []
    for i in range(4):
        ga = _mm(s["ys"][i], dbig_y[i], "tn", t + f"dw_out{i}", out=gt(R_OUT + i * BW, BW))
        dys.append(_mm(dbig_y[i], wt(R_OUT + i * BW, BW), "nt", t + f"d_y{i}"))
    proj = s["proj"]
    (dax, dag, gs["conv_a_w"], gs["conv_a_b"], dwx, gs["lru_bx"], dwa, gs["lru_ba"], gs["lru_lambda"]) = _a_bwd(
        proj, dys[0], c["wA"], c["bA"], c["wx"], c["bx"], c["wa"], c["ba"], c["lam"], t + "a_bwd")
    gs["lru_wx"] = _blockdiag_extract(dwx)
    gs["lru_wa"] = _blockdiag_extract(dwa)
    dbv, dbc, dbb, gs["conv_b_w"] = _b_bwd(proj, dys[1], c["wB"], t + "b_bwd")
    dq3, dk3, dv3, dsink = _attn_bwd(s["q3"], s["k3"], s["v3"], _heads(dys[2], N_Q), c["ss"], t + "attn_bwd")
    gs["sinks"] = dsink[:, 0, 0]
    dcd, gs["ln_d_g"], gs["ln_d_b"] = _ln_silu_bwd(s["cd"], c["lg"], c["lb"], dys[3], t + "d_ln_bwd")
    dd1, dd2, gs["conv_d_w"], gs["conv_d_b"] = _d_conv_bwd(proj, dcd, c["wD"], t + "d_conv_bwd")
    dproj = jnp.concatenate(
        [dax, dag, dbv, dbc, dbb, _unheads(dq3), _unheads(dk3).astype(BF16), _unheads(dv3).astype(BF16), dd1, dd2,
         *dgl], axis=1)
    ga = _mm(dproj, s["xn"], "tn", t + "dw_in", out=gt(R_IN, IN_W))
    dxn = _mm(dproj, wt(R_IN, IN_W), "nn", t + "d_xn")
    dx, gs["norm1_g"] = _rms_bwd(s["x"], c["g1"], dxn, dhres, t + "rms1_bwd")
    return dx, ga, gs


def kernel(x, norm1_g, w_in, conv_a_w, conv_a_b, lru_wx, lru_bx, lru_wa, lru_ba, lru_lambda, w_a_out, conv_b_w, w_b_out, sinks, w_c_out, conv_d_w, conv_d_b, ln_d_g, ln_d_b, w_d_out, w_o, norm2_g, w_ffn_gate, w_ffn_up, w_ffn_down, final_g, loss_target, m_norm1_g, m_w_in, m_conv_a_w, m_conv_a_b, m_lru_wx, m_lru_bx, m_lru_wa, m_lru_ba, m_lru_lambda, m_w_a_out, m_conv_b_w, m_w_b_out, m_sinks, m_w_c_out, m_conv_d_w, m_conv_d_b, m_ln_d_g, m_ln_d_b, m_w_d_out, m_w_o, m_norm2_g, m_w_ffn_gate, m_w_ffn_up, m_w_ffn_down, m_final_g, v_norm1_g, v_w_in, v_conv_a_w, v_conv_a_b, v_lru_wx, v_lru_bx, v_lru_wa, v_lru_ba, v_lru_lambda, v_w_a_out, v_conv_b_w, v_w_b_out, v_sinks, v_w_c_out, v_conv_d_w, v_conv_d_b, v_ln_d_g, v_ln_d_b, v_w_d_out, v_w_o, v_norm2_g, v_w_ffn_gate, v_w_ffn_up, v_w_ffn_down, v_final_g):
    given = dict(locals())
    p = {n: given[n] for n in NAMES}
    mom = {n: given["m_" + n] for n in NAMES}
    var = {n: given["v_" + n] for n in NAMES}
    cx, cy, cc = _coords()
    chip = 2 * cx + cy

    shards = _arena_shards(p)
    fw = _gather_taps(p, "gather_taps")
    fw["arena"] = _place_own(_gather_arena(shards, "gather_weights"), shards, chip)

    h = x[0]
    consts, saved = [], []
    for l in range(DEPTH):
        consts.append(_layer_consts(p, fw, l))
        h, s = _layer_fwd(h, consts[l], fw, l)
        saved.append(s)
    loss_vec, dh, g_final = _loss_head(h, final_g.reshape(1, -1), loss_target[0], "loss_head")
    loss = lax.psum(loss_vec[0, 0], ("x", "y", "c"))

    gss = [None] * DEPTH
    ga = jnp.zeros((DEPTH, ARENA_ROWS, ARENA_W), BF16)
    for l in reversed(range(DEPTH)):
        dh, ga, gss[l] = _layer_bwd(dh, saved[l], consts[l], fw, l, ga)
    grad_x = dh[None]

    zero = jnp.zeros((1,), jnp.int32)
    from_sibling = _swap_sibling_half(ga[:, None], "grads_swap_sibling")
    chip_sum = _sum_own_plus(ga, cc.reshape(1).astype(jnp.int32), from_sibling, "grads_sum_chip", BF16)
    recv_main, recv_out = _scatter_pieces(chip_sum, "grads_scatter_chips")
    own_main, own_out = _own_piece(chip_sum, chip)
    red_main = _sum_own_plus(own_main[None], zero, recv_main, "grads_sum_all", F32)
    red_out = _sum_own_plus(own_out[None], zero, recv_out, "grads_sum_all_out", F32)
    sib_main, sib_out = _swap_pair(red_main, red_out, "grads_swap_reduced")
    by_layer = lambda mine, theirs: jnp.stack([jnp.where(cc == l, mine, theirs) for l in range(DEPTH)])
    g = _shard_grads(by_layer(red_main, sib_main), by_layer(red_out, sib_out))

    small_full = {n: (g_final.reshape(-1) if n == "final_g" else
                      jnp.stack([gss[l][n].reshape(gss[l][n].shape[-2:] if n.startswith("conv") and n.endswith("_w")
                                                   else p[n].shape[1:]) for l in range(DEPTH)]))
                  for n in SMALL}
    part = _flat_pack([small_full[n] for n in SMALL])
    rows = part.shape[0]
    gathered = _allgather8(jnp.stack([part, part]), "gather_small_grads", pltpu.VMEM)
    small_sum = _flat_unpack(_sum_leading(gathered, "small_grads_sum"), [small_full[n].shape for n in SMALL])
    for n, a in zip(SMALL, small_sum):
        g[n] = lax.dynamic_slice_in_dim(a, chip * LANE, LANE, axis=2) if n in CONV_NAMES else a

    delta, new_m, new_v = {}, {}, {}
    for n in BIG:
        shp = p[n].shape
        two_d = lambda a: a.reshape(-1, shp[-1])
        d, nm, nv = _adamw(two_d(p[n]), two_d(g[n]), two_d(mom[n]), two_d(var[n]), "adamw_" + n)
        delta[n], new_m[n], new_v[n] = d.reshape(shp), nm.reshape(shp), nv.reshape(shp)
    shapes = [p[n].shape for n in SMALL]
    d, nm, nv = _adamw(_flat_pack([p[n] for n in SMALL]), _flat_pack([g[n] for n in SMALL]),
                       _flat_pack([mom[n] for n in SMALL]), _flat_pack([var[n] for n in SMALL]), "adamw_small")
    for n, a, b, cval in zip(SMALL, _flat_unpack(d, shapes), _flat_unpack(nm, shapes), _flat_unpack(nv, shapes)):
        delta[n], new_m[n], new_v[n] = a, b, cval

    return (loss, grad_x, *[g[n] for n in NAMES], *[delta[n] for n in NAMES], *[new_m[n] for n in NAMES],
            *[new_v[n] for n in NAMES])
```

```python
import functools
import math

import numpy as np
import jax
import jax.numpy as jnp
from jax import lax
from jax.experimental import pallas as pl
from jax.experimental.pallas import tpu as pltpu

F32 = jnp.float32
BF16 = jnp.bfloat16
MESH = pl.DeviceIdType.MESH

D_MODEL = 1024
DEPTH = 2
BW = 512
HEAD_DIM = 64
N_Q = 8
N_KV = 2
BLK = 128
D_FF = 2816
IN_W = 8448
EPS = 1e-6
NEG_INF = -1e30
LRU_C = 8.0
CONV_A, CONV_B, CONV_D = 4, 3, 31
LANE = 128
ROW_TILE = 256
VMEM_LIMIT = 56 * 1024 * 1024
MM_VMEM_BUDGET = 36 * 1024 * 1024

C_AX, C_AG, C_BV, C_BC, C_BB = 0, 4, 8, 12, 16
OFF_Q, OFF_K, OFF_V = 2560, 3072, 3200
C_D1, C_D2 = 26, 30
OFF_GL = 4352

ADAM_LR, ADAM_B1, ADAM_B2, ADAM_EPS, ADAM_WD, ADAM_STEP = 0.001, 0.9, 0.999, 1e-08, 0.01, 10

ARENA_W = 1024
R_DOWN, R_GATE, R_UP, R_IN, R_O, R_OUT = 0, 2816, 5632, 8448, 16896, 17920
ARENA_ROWS = 19968
ROW_REGIONS = ((R_DOWN, 704), (R_GATE, 704), (R_UP, 704), (R_IN, 2112), (R_O, 256))
PIECE_OFF = (0, 704, 1408, 2112, 4224)
PIECE_ROWS = 4480
OUT_ROWS, OUT_COLS = 4 * BW, D_MODEL // 4

BIG = ("w_in", "w_a_out", "w_b_out", "w_c_out", "w_d_out", "w_o", "w_ffn_gate", "w_ffn_up", "w_ffn_down")
CONV_NAMES = ("conv_a_w", "conv_b_w", "conv_d_w")
N_TAPS = CONV_A + CONV_B + CONV_D
SMALL = ("norm1_g", "conv_a_w", "conv_a_b", "lru_wx", "lru_bx", "lru_wa", "lru_ba", "lru_lambda", "conv_b_w",
         "sinks", "conv_d_w", "conv_d_b", "ln_d_g", "ln_d_b", "norm2_g", "final_g")
NAMES = ['norm1_g', 'w_in', 'conv_a_w', 'conv_a_b', 'lru_wx', 'lru_bx', 'lru_wa', 'lru_ba', 'lru_lambda', 'w_a_out',
         'conv_b_w', 'w_b_out', 'sinks', 'w_c_out', 'conv_d_w', 'conv_d_b', 'ln_d_g', 'ln_d_b', 'w_d_out', 'w_o',
         'norm2_g', 'w_ffn_gate', 'w_ffn_up', 'w_ffn_down', 'final_g']


def _pick(n, cands, off=0):
    for c in cands:
        if n % c == 0 and off % c == 0:
            return c
    assert off == 0, (n, off)
    return n


class Win:
    def __init__(self, arena, l, off, rows):
        self.arena, self.l, self.off, self.rows = arena, l, off, rows
        self.shape = (rows, arena.shape[2])


def _params(sem=None):
    return pltpu.CompilerParams(dimension_semantics=sem, vmem_limit_bytes=VMEM_LIMIT)


def _sig(z):
    return 1.0 / (1.0 + jnp.exp(-z))


def _dot(a, b, dims):
    return lax.dot_general(a.astype(BF16), b.astype(BF16), (dims, ((), ())), preferred_element_type=F32)


NN = ((1,), (0,))
NT = ((1,), (1,))
TN = ((0,), (0,))


def _mm(a, b, mode, name, out_dtype=F32, add=None, out=None):
    if mode == "nn":
        (m, k), n = a.shape, b.shape[1]
    elif mode == "nt":
        (m, k), n = a.shape, b.shape[0]
    else:
        (k, m), n = a.shape, b.shape[1]
    b_win = isinstance(b, Win)
    b_off = b.off if b_win else 0
    o_off = out.off if out is not None else 0
    if out is not None:
        out_dtype = out.arena.dtype
    tk = _pick(k, (2816, 2048, 1408, 1024, 768, 512, 256), b_off if mode != "nt" else 0)
    nk = k // tk
    n_off = b_off if mode == "nt" else 0
    a_bytes, b_bytes, o_bytes = a.dtype.itemsize, 2, jnp.dtype(out_dtype).itemsize

    def vmem_bytes(tm_, tn_):
        tile = tm_ * tn_
        return (2 * tk * (tm_ * a_bytes + tn_ * b_bytes) + 2 * tile * o_bytes + (tile * 4 if nk > 1 else 0)
                + (2 * tile * 4 if add is not None else 0) + tile * 4)

    pairs = [(tm_, tn_) for tm_ in (2048, 1024, 768, 512, 256, 128) for tn_ in (1024, 768, 512, 256, 128)
             if m % tm_ == 0 and o_off % tm_ == 0 and n % tn_ == 0 and n_off % tn_ == 0
             and vmem_bytes(tm_, tn_) <= MM_VMEM_BUDGET]
    tm, tn = max(pairs, key=lambda p: (p[0] * p[1], p[0]))
    dims = {"nn": NN, "nt": NT, "tn": TN}[mode]

    def body(*refs):
        a_ref, b_ref = refs[:2]
        c_ref = refs[2] if add is not None else None
        if nk == 1:
            r = _dot(a_ref[...], b_ref[...], dims)
            if add is not None:
                r = r + c_ref[...]
            refs[-1][...] = r.astype(out_dtype)
            return
        o_ref, acc = refs[-2:]
        kk = pl.program_id(2)

        @pl.when(kk == 0)
        def _():
            acc[...] = jnp.zeros_like(acc)

        acc[...] += _dot(a_ref[...], b_ref[...], dims)

        @pl.when(kk == nk - 1)
        def _():
            r = acc[...]
            if add is not None:
                r = r + c_ref[...]
            o_ref[...] = r.astype(out_dtype)

    if mode == "tn":
        a_spec = pl.BlockSpec((tk, tm), lambda i, j, q: (q, i))
    else:
        a_spec = pl.BlockSpec((tm, tk), lambda i, j, q: (i, q))
    if mode == "nt":
        b_blk, b_idx = (tn, tk), (lambda i, j, q: (b_off // tn + j, q))
    else:
        b_blk, b_idx = (tk, tn), (lambda i, j, q: (b_off // tk + q, j))
    if b_win:
        bl = b.l
        b_spec = pl.BlockSpec((None,) + b_blk, lambda i, j, q: (bl,) + b_idx(i, j, q))
    else:
        b_spec = pl.BlockSpec(b_blk, b_idx)
    plain_o = pl.BlockSpec((tm, tn), lambda i, j, q: (i, j))
    in_specs = [a_spec, b_spec] + ([plain_o] if add is not None else [])
    args = (a, b.arena if b_win else b) + ((add,) if add is not None else ())
    aliases = {}
    if out is None:
        o_spec, o_shape = plain_o, jax.ShapeDtypeStruct((m, n), out_dtype)
    else:
        ol = out.l
        o_spec = pl.BlockSpec((None, tm, tn), lambda i, j, q: (ol, o_off // tm + i, j))
        o_shape = jax.ShapeDtypeStruct(out.arena.shape, out_dtype)
        aliases = {len(args): 0}
        in_specs.append(pl.BlockSpec(memory_space=pl.ANY))
        args = args + (out.arena,)
    return pl.pallas_call(
        body, name=name, out_shape=o_shape,
        grid=(m // tm, n // tn, nk), in_specs=in_specs, out_specs=o_spec,
        scratch_shapes=[pltpu.VMEM((tm, tn), F32)] if nk > 1 else [], input_output_aliases=aliases,
        compiler_params=_params(("parallel", "parallel", "arbitrary")),
    )(*args)


def _row_spec(cols, tr=ROW_TILE):
    return pl.BlockSpec((tr, cols), lambda i: (i, 0))


def _vec_spec(cols):
    return pl.BlockSpec((1, cols), lambda i: (0, 0))


def _rms_fwd(x, g, name):
    t, d = x.shape

    def body(x_ref, g_ref, o_ref):
        xv = x_ref[...]
        r = lax.rsqrt(jnp.mean(xv * xv, axis=1, keepdims=True) + EPS)
        o_ref[...] = (xv * r * g_ref[...]).astype(BF16)

    return pl.pallas_call(
        body, name=name, out_shape=jax.ShapeDtypeStruct((t, d), BF16), grid=(t // ROW_TILE,),
        in_specs=[_row_spec(d), _vec_spec(d)], out_specs=_row_spec(d), compiler_params=_params(("parallel",)),
    )(x, g)


def _rms_bwd(x, g, dxn, dres, name):
    t, d = x.shape

    def body(x_ref, g_ref, dy_ref, dr_ref, dx_ref, dg_ref):
        @pl.when(pl.program_id(0) == 0)
        def _():
            dg_ref[...] = jnp.zeros_like(dg_ref)

        xv = x_ref[...]
        dy = dy_ref[...]
        r = lax.rsqrt(jnp.mean(xv * xv, axis=1, keepdims=True) + EPS)
        w = dy * g_ref[...]
        dx_ref[...] = dr_ref[...] + r * w - xv * (r * r * r) * jnp.mean(w * xv, axis=1, keepdims=True)
        dg_ref[...] += jnp.sum(dy * xv * r, axis=0, keepdims=True)

    return pl.pallas_call(
        body, name=name,
        out_shape=(jax.ShapeDtypeStruct((t, d), F32), jax.ShapeDtypeStruct((1, d), F32)), grid=(t // ROW_TILE,),
        in_specs=[_row_spec(d), _vec_spec(d), _row_spec(d), _row_spec(d)], out_specs=(_row_spec(d), _vec_spec(d)),
        compiler_params=_params(("arbitrary",)),
    )(x, g, dxn, dres)


def _loss_head(x, g, tgt, name):
    t, d = x.shape

    def body(x_ref, g_ref, t_ref, loss_ref, dx_ref, dg_ref):
        @pl.when(pl.program_id(0) == 0)
        def _():
            dg_ref[...] = jnp.zeros_like(dg_ref)
            loss_ref[...] = jnp.zeros_like(loss_ref)

        xv = x_ref[...]
        gv = g_ref[...]
        r = lax.rsqrt(jnp.mean(xv * xv, axis=1, keepdims=True) + EPS)
        e = xv * r * gv - t_ref[...]
        loss_ref[...] += jnp.full(loss_ref.shape, (0.5 / d) * jnp.sum(e * e), F32)
        dy = e * (1.0 / d)
        w = dy * gv
        dx_ref[...] = r * w - xv * (r * r * r) * jnp.mean(w * xv, axis=1, keepdims=True)
        dg_ref[...] += jnp.sum(dy * xv * r, axis=0, keepdims=True)

    return pl.pallas_call(
        body, name=name,
        out_shape=(jax.ShapeDtypeStruct((1, LANE), F32), jax.ShapeDtypeStruct((t, d), F32),
                   jax.ShapeDtypeStruct((1, d), F32)),
        grid=(t // ROW_TILE,), in_specs=[_row_spec(d), _vec_spec(d), _row_spec(d)],
        out_specs=(_vec_spec(LANE), _row_spec(d), _vec_spec(d)), compiler_params=_params(("arbitrary",)),
    )(x, g, tgt)


def _swiglu_fwd(gg, uu, name):
    t, f = gg.shape

    def body(g_ref, u_ref, o_ref):
        gv = g_ref[...]
        o_ref[...] = (gv * _sig(gv) * u_ref[...]).astype(BF16)

    return pl.pallas_call(
        body, name=name, out_shape=jax.ShapeDtypeStruct((t, f), BF16), grid=(t // ROW_TILE,),
        in_specs=[_row_spec(f), _row_spec(f)], out_specs=_row_spec(f), compiler_params=_params(("parallel",)),
    )(gg, uu)


def _swiglu_bwd(gg, uu, dact, name):
    t, f = gg.shape

    def body(g_ref, u_ref, d_ref, dg_ref, du_ref):
        gv = g_ref[...]
        dv = d_ref[...]
        s = _sig(gv)
        dg_ref[...] = (dv * u_ref[...] * s * (1.0 + gv * (1.0 - s))).astype(BF16)
        du_ref[...] = (dv * gv * s).astype(BF16)

    return pl.pallas_call(
        body, name=name,
        out_shape=(jax.ShapeDtypeStruct((t, f), BF16), jax.ShapeDtypeStruct((t, f), BF16)), grid=(t // ROW_TILE,),
        in_specs=[_row_spec(f)] * 3, out_specs=(_row_spec(f), _row_spec(f)), compiler_params=_params(("parallel",)),
    )(gg, uu, dact)


MERGE_COLS = 256


def _gate_specs():
    nb = D_MODEL // MERGE_COLS
    base = OFF_GL // MERGE_COLS
    return [pl.BlockSpec((ROW_TILE, MERGE_COLS), functools.partial(lambda i, j, kk: (i, base + nb * kk + j), kk=kk))
            for kk in range(4)]


def _merge_fwd(proj, ys, name):
    t = proj.shape[0]
    yspec = pl.BlockSpec((ROW_TILE, MERGE_COLS), lambda i, j: (i, j))

    def body(g0, g1, g2, g3, y0, y1, y2, y3, o_ref):
        acc = _sig(g0[...]) * y0[...]
        acc += _sig(g1[...]) * y1[...]
        acc += _sig(g2[...]) * y2[...]
        acc += _sig(g3[...]) * y3[...]
        o_ref[...] = acc.astype(BF16)

    return pl.pallas_call(
        body, name=name, out_shape=jax.ShapeDtypeStruct((t, D_MODEL), BF16),
        grid=(t // ROW_TILE, D_MODEL // MERGE_COLS), in_specs=_gate_specs() + [yspec] * 4, out_specs=yspec,
        compiler_params=_params(("parallel", "parallel")),
    )(proj, proj, proj, proj, *ys)


def _merge_bwd(proj, ys, dmerged, name):
    t = proj.shape[0]
    yspec = pl.BlockSpec((ROW_TILE, MERGE_COLS), lambda i, j: (i, j))

    def body(g0, g1, g2, g3, y0, y1, y2, y3, dm_ref, *outs):
        dm = dm_ref[...]
        for gr, yr, dy_ref, dg_ref in zip((g0, g1, g2, g3), (y0, y1, y2, y3), outs[:4], outs[4:]):
            s = _sig(gr[...])
            dy_ref[...] = (dm * s).astype(BF16)
            dg_ref[...] = (dm * yr[...] * s * (1.0 - s)).astype(BF16)

    shp = jax.ShapeDtypeStruct((t, D_MODEL), BF16)
    outs = pl.pallas_call(
        body, name=name, out_shape=(shp,) * 8, grid=(t // ROW_TILE, D_MODEL // MERGE_COLS),
        in_specs=_gate_specs() + [yspec] * 5, out_specs=(yspec,) * 8, compiler_params=_params(("parallel", "parallel")),
    )(proj, proj, proj, proj, *ys, dmerged)
    return outs[:4], outs[4:]


def _ln_silu_fwd(cd, g, b, name):
    t, c = cd.shape

    def body(x_ref, g_ref, b_ref, o_ref):
        xv = x_ref[...]
        mu = jnp.mean(xv, axis=1, keepdims=True)
        xc = xv - mu
        rs = lax.rsqrt(jnp.mean(xc * xc, axis=1, keepdims=True) + EPS)
        z = xc * rs * g_ref[...] + b_ref[...]
        o_ref[...] = (z * _sig(z)).astype(BF16)

    return pl.pallas_call(
        body, name=name, out_shape=jax.ShapeDtypeStruct((t, c), BF16), grid=(t // ROW_TILE,),
        in_specs=[_row_spec(c), _vec_spec(c), _vec_spec(c)], out_specs=_row_spec(c),
        compiler_params=_params(("parallel",)),
    )(cd, g, b)


def _ln_silu_bwd(cd, g, b, dy, name):
    t, c = cd.shape

    def body(x_ref, g_ref, b_ref, dy_ref, dx_ref, dg_ref, db_ref):
        @pl.when(pl.program_id(0) == 0)
        def _():
            dg_ref[...] = jnp.zeros_like(dg_ref)
            db_ref[...] = jnp.zeros_like(db_ref)

        xv = x_ref[...]
        gv = g_ref[...]
        mu = jnp.mean(xv, axis=1, keepdims=True)
        xc = xv - mu
        rs = lax.rsqrt(jnp.mean(xc * xc, axis=1, keepdims=True) + EPS)
        xh = xc * rs
        z = xh * gv + b_ref[...]
        s = _sig(z)
        dz = dy_ref[...] * s * (1.0 + z * (1.0 - s))
        dg_ref[...] += jnp.sum(dz * xh, axis=0, keepdims=True)
        db_ref[...] += jnp.sum(dz, axis=0, keepdims=True)
        dxh = dz * gv
        dx_ref[...] = rs * (dxh - jnp.mean(dxh, axis=1, keepdims=True) - xh * jnp.mean(dxh * xh, axis=1, keepdims=True))

    return pl.pallas_call(
        body, name=name,
        out_shape=(jax.ShapeDtypeStruct((t, c), F32), jax.ShapeDtypeStruct((1, c), F32),
                   jax.ShapeDtypeStruct((1, c), F32)),
        grid=(t // ROW_TILE,), in_specs=[_row_spec(c), _vec_spec(c), _vec_spec(c), _row_spec(c)],
        out_specs=(_row_spec(c), _vec_spec(c), _vec_spec(c)), compiler_params=_params(("arbitrary",)),
    )(cd, g, b, dy)


def _shift_dn(x, k):
    if k == 0:
        return x
    row = lax.broadcasted_iota(jnp.int32, x.shape, 0)
    return jnp.where(row >= k, pltpu.roll(x, k, 0), 0.0)


def _shift_up(x, k):
    if k == 0:
        return x
    t = x.shape[0]
    row = lax.broadcasted_iota(jnp.int32, x.shape, 0)
    return jnp.where(row < t - k, pltpu.roll(x, t - k, 0), 0.0)


def _conv_fwd(x, w_ref, taps):
    acc = w_ref[pl.ds(taps - 1, 1), :] * x
    for k in range(taps - 1):
        acc += w_ref[pl.ds(k, 1), :] * _shift_dn(x, taps - 1 - k)
    return acc


def _conv_bwd(x, dy, w_ref, dw_ref, taps):
    dx = w_ref[pl.ds(taps - 1, 1), :] * dy
    dw_ref[pl.ds(taps - 1, 1), :] = jnp.sum(dy * x, axis=0, keepdims=True)
    for k in range(taps - 1):
        s = taps - 1 - k
        dx += w_ref[pl.ds(k, 1), :] * _shift_up(dy, s)
        dw_ref[pl.ds(k, 1), :] = jnp.sum(dy * _shift_dn(x, s), axis=0, keepdims=True)
    return dx


def _scan_fwd(a, u):
    t = a.shape[0]
    k = 1
    while k < t:
        u = u + a * _shift_dn(u, k)
        if 2 * k < t:
            a = a * _shift_dn(a, k)
        k *= 2
    return u


def _scan_rev(a, u):
    t = a.shape[0]
    k = 1
    while k < t:
        u = u + a * _shift_up(u, k)
        if 2 * k < t:
            a = a * _shift_up(a, k)
        k *= 2
    return u


def _one_minus_exp(y):
    return jnp.where(y > -1e-3, -(y + 0.5 * y * y + (1.0 / 6.0) * y * y * y), 1.0 - jnp.exp(y))


GELU_C = math.sqrt(2.0 / math.pi)


def _gelu(x):
    th = jnp.tanh(GELU_C * (x + 0.044715 * x * x * x))
    return 0.5 * x * (1.0 + th), th


def _softplus(x):
    return jnp.maximum(x, 0.0) + jnp.log(1.0 + jnp.exp(-jnp.abs(x)))


def _chunk_spec(t, blk0):
    return pl.BlockSpec((t, LANE), functools.partial(lambda c, b: (0, b + c), b=blk0))


def _tap_spec(taps):
    return pl.BlockSpec((taps, LANE), lambda c: (0, c))


def _cvec_spec():
    return pl.BlockSpec((1, LANE), lambda c: (0, c))


def _cmat_spec():
    return pl.BlockSpec((1, LANE, LANE), lambda c: (c, 0, 0))


def _lru_forward(ax, wA_ref, bA_ref, wx_ref, bx_ref, wa_ref, ba_ref, lam_ref):
    ca = _conv_fwd(ax, wA_ref, CONV_A) + bA_ref[...]
    gi = _sig(_dot(ca, wx_ref[0], NN) + bx_ref[...])
    gr = _sig(_dot(ca, wa_ref[0], NN) + ba_ref[...])
    sp = _softplus(-lam_ref[...])
    la = -LRU_C * gr * sp
    a = jnp.exp(la)
    mult = jnp.sqrt(_one_minus_exp(2.0 * la))
    h = _scan_fwd(a, ca * gi * mult)
    return ca, gi, gr, sp, a, mult, h


def _a_fwd(proj, wA, bA, wx, bx, wa, ba, lam, name):
    t = proj.shape[0]

    def body(ax_ref, ag_ref, wA_ref, bA_ref, wx_ref, bx_ref, wa_ref, ba_ref, lam_ref, o_ref):
        h = _lru_forward(ax_ref[...], wA_ref, bA_ref, wx_ref, bx_ref, wa_ref, ba_ref, lam_ref)[-1]
        o_ref[...] = (h * _gelu(ag_ref[...])[0]).astype(BF16)

    return pl.pallas_call(
        body, name=name, out_shape=jax.ShapeDtypeStruct((t, BW), BF16), grid=(BW // LANE,),
        in_specs=[_chunk_spec(t, C_AX), _chunk_spec(t, C_AG), _tap_spec(CONV_A), _cvec_spec(), _cmat_spec(),
                  _cvec_spec(), _cmat_spec(), _cvec_spec(), _cvec_spec()],
        out_specs=_chunk_spec(t, 0), compiler_params=_params(("parallel",)),
    )(proj, proj, wA, bA, wx, bx, wa, ba, lam)


def _a_bwd(proj, dya, wA, bA, wx, bx, wa, ba, lam, name):
    t = proj.shape[0]

    def body(ax_ref, ag_ref, dy_ref, wA_ref, bA_ref, wx_ref, bx_ref, wa_ref, ba_ref, lam_ref,
             dax_ref, dag_ref, dwA_ref, dbA_ref, dwx_ref, dbx_ref, dwa_ref, dba_ref, dlam_ref):
        ax = ax_ref[...]
        ag = ag_ref[...]
        dy = dy_ref[...]
        ca, gi, gr, sp, a, mult, h = _lru_forward(ax, wA_ref, bA_ref, wx_ref, bx_ref, wa_ref, ba_ref, lam_ref)
        gel, th = _gelu(ag)
        dgel = 0.5 * (1.0 + th) + 0.5 * ag * (1.0 - th * th) * GELU_C * (1.0 + 3.0 * 0.044715 * ag * ag)
        dag_ref[...] = (dy * h * dgel).astype(BF16)
        s = _scan_rev(_shift_up(a, 1), dy * gel)
        da = s * _shift_dn(h, 1)
        dca = s * gi * mult
        dgi = s * ca * mult
        dmult = s * ca * gi
        dla = da * a - dmult * a * a / mult
        dgr = dla * (-LRU_C * sp)
        dsp = jnp.sum(dla * (-LRU_C * gr), axis=0, keepdims=True)
        dlam_ref[...] = -_sig(-lam_ref[...]) * dsp
        dzi = dgi * gi * (1.0 - gi)
        dzr = dgr * gr * (1.0 - gr)
        dbx_ref[...] = jnp.sum(dzi, axis=0, keepdims=True)
        dba_ref[...] = jnp.sum(dzr, axis=0, keepdims=True)
        dwx_ref[0] = _dot(ca, dzi, TN)
        dwa_ref[0] = _dot(ca, dzr, TN)
        dca += _dot(dzi, wx_ref[0], NT) + _dot(dzr, wa_ref[0], NT)
        dbA_ref[...] = jnp.sum(dca, axis=0, keepdims=True)
        dax_ref[...] = _conv_bwd(ax, dca, wA_ref, dwA_ref, CONV_A).astype(BF16)

    big = jax.ShapeDtypeStruct((t, BW), BF16)
    vec = jax.ShapeDtypeStruct((1, BW), F32)
    mat = jax.ShapeDtypeStruct((BW // LANE, LANE, LANE), F32)
    return pl.pallas_call(
        body, name=name,
        out_shape=(big, big, jax.ShapeDtypeStruct((CONV_A, BW), F32), vec, mat, vec, mat, vec, vec),
        grid=(BW // LANE,),
        in_specs=[_chunk_spec(t, C_AX), _chunk_spec(t, C_AG), _chunk_spec(t, 0), _tap_spec(CONV_A), _cvec_spec(),
                  _cmat_spec(), _cvec_spec(), _cmat_spec(), _cvec_spec(), _cvec_spec()],
        out_specs=(_chunk_spec(t, 0), _chunk_spec(t, 0), _tap_spec(CONV_A), _cvec_spec(), _cmat_spec(), _cvec_spec(),
                   _cmat_spec(), _cvec_spec(), _cvec_spec()),
        compiler_params=_params(("parallel",)),
    )(proj, proj, dya, wA, bA, wx, bx, wa, ba, lam)


def _b_fwd(proj, wB, name):
    t = proj.shape[0]

    def body(bv_ref, bc_ref, bb_ref, w_ref, o_ref):
        o_ref[...] = (bb_ref[...] * _conv_fwd(bc_ref[...] * bv_ref[...], w_ref, CONV_B)).astype(BF16)

    return pl.pallas_call(
        body, name=name, out_shape=jax.ShapeDtypeStruct((t, BW), BF16), grid=(BW // LANE,),
        in_specs=[_chunk_spec(t, C_BV), _chunk_spec(t, C_BC), _chunk_spec(t, C_BB), _tap_spec(CONV_B)],
        out_specs=_chunk_spec(t, 0), compiler_params=_params(("parallel",)),
    )(proj, proj, proj, wB)


def _b_bwd(proj, dyb, wB, name):
    t = proj.shape[0]

    def body(bv_ref, bc_ref, bb_ref, dy_ref, w_ref, dbv_ref, dbc_ref, dbb_ref, dw_ref):
        bv = bv_ref[...]
        bc = bc_ref[...]
        dy = dy_ref[...]
        p = bc * bv
        dbb_ref[...] = (dy * _conv_fwd(p, w_ref, CONV_B)).astype(BF16)
        dp = _conv_bwd(p, dy * bb_ref[...], w_ref, dw_ref, CONV_B)
        dbc_ref[...] = (dp * bv).astype(BF16)
        dbv_ref[...] = (dp * bc).astype(BF16)

    big = jax.ShapeDtypeStruct((t, BW), BF16)
    return pl.pallas_call(
        body, name=name, out_shape=(big, big, big, jax.ShapeDtypeStruct((CONV_B, BW), F32)), grid=(BW // LANE,),
        in_specs=[_chunk_spec(t, C_BV), _chunk_spec(t, C_BC), _chunk_spec(t, C_BB), _chunk_spec(t, 0),
                  _tap_spec(CONV_B)],
        out_specs=(_chunk_spec(t, 0),) * 3 + (_tap_spec(CONV_B),), compiler_params=_params(("parallel",)),
    )(proj, proj, proj, dyb, wB)


def _d_conv_fwd(proj, wD, bD, name):
    t = proj.shape[0]

    def body(d1_ref, d2_ref, w_ref, b_ref, o_ref):
        o_ref[...] = _conv_fwd(d1_ref[...] * _sig(d2_ref[...]), w_ref, CONV_D) + b_ref[...]

    return pl.pallas_call(
        body, name=name, out_shape=jax.ShapeDtypeStruct((t, BW), F32), grid=(BW // LANE,),
        in_specs=[_chunk_spec(t, C_D1), _chunk_spec(t, C_D2), _tap_spec(CONV_D), _cvec_spec()],
        out_specs=_chunk_spec(t, 0), compiler_params=_params(("parallel",)),
    )(proj, proj, wD, bD)


def _d_conv_bwd(proj, dcd, wD, name):
    t = proj.shape[0]

    def body(d1_ref, d2_ref, dy_ref, w_ref, dd1_ref, dd2_ref, dw_ref, db_ref):
        d1 = d1_ref[...]
        s = _sig(d2_ref[...])
        dy = dy_ref[...]
        db_ref[...] = jnp.sum(dy, axis=0, keepdims=True)
        dd = _conv_bwd(d1 * s, dy, w_ref, dw_ref, CONV_D)
        dd1_ref[...] = (dd * s).astype(BF16)
        dd2_ref[...] = (dd * d1 * s * (1.0 - s)).astype(BF16)

    big = jax.ShapeDtypeStruct((t, BW), BF16)
    return pl.pallas_call(
        body, name=name,
        out_shape=(big, big, jax.ShapeDtypeStruct((CONV_D, BW), F32), jax.ShapeDtypeStruct((1, BW), F32)),
        grid=(BW // LANE,),
        in_specs=[_chunk_spec(t, C_D1), _chunk_spec(t, C_D2), _chunk_spec(t, 0), _tap_spec(CONV_D)],
        out_specs=(_chunk_spec(t, 0), _chunk_spec(t, 0), _tap_spec(CONV_D), _cvec_spec()),
        compiler_params=_params(("parallel",)),
    )(proj, proj, dcd, wD)


SCALE = HEAD_DIM ** -0.5
GROUP = N_Q // N_KV


def _attn_probs(q_ref, k_ref, ss_ref, h, n):
    qi = lax.broadcasted_iota(jnp.int32, (BLK, BLK), 0)
    ki = lax.broadcasted_iota(jnp.int32, (BLK, BLK), 1)
    dist = (qi - ki).astype(F32)
    sink = ss_ref[0, h]
    slope = ss_ref[1, h]
    s0 = pl.multiple_of(n * BLK, BLK)
    sp = pl.multiple_of(jnp.maximum(n - 1, 0) * BLK, BLK)
    q = q_ref[0, pl.ds(s0, BLK), :]
    kc = k_ref[0, pl.ds(s0, BLK), :]
    kp = k_ref[0, pl.ds(sp, BLK), :]
    sc = jnp.where(ki <= qi, _dot(q, kc, NT) * SCALE - slope * dist, NEG_INF)
    first = jnp.where(n >= 1, 0, BLK)
    sv = jnp.where(ki > qi + first, _dot(q, kp, NT) * SCALE - slope * (dist + BLK), NEG_INF)
    m = jnp.maximum(jnp.maximum(jnp.max(sc, axis=1, keepdims=True), jnp.max(sv, axis=1, keepdims=True)), sink)
    pc = jnp.exp(sc - m)
    pp = jnp.exp(sv - m)
    ps = jnp.exp(sink - m)
    z = jnp.sum(pc, axis=1, keepdims=True) + jnp.sum(pp, axis=1, keepdims=True) + ps
    return s0, sp, q, kc, kp, pc, pp, ps, z


def _attn_specs(t):
    qs = pl.BlockSpec((1, t, HEAD_DIM), lambda h: (h, 0, 0))
    ks = pl.BlockSpec((1, t, HEAD_DIM), lambda h: (h // GROUP, 0, 0))
    ss = pl.BlockSpec(memory_space=pltpu.SMEM)
    return qs, ks, ss


def _attn_fwd(q, k, v, ss, name):
    t = q.shape[1]
    qs, ks, sspec = _attn_specs(t)

    def body(q_ref, k_ref, v_ref, ss_ref, o_ref):
        h = pl.program_id(0)

        def blk(n, carry):
            s0, sp, _, _, _, pc, pp, _, z = _attn_probs(q_ref, k_ref, ss_ref, h, n)
            o = _dot(pc, v_ref[0, pl.ds(s0, BLK), :], NN) + _dot(pp, v_ref[0, pl.ds(sp, BLK), :], NN)
            o_ref[0, pl.ds(s0, BLK), :] = (o / z).astype(BF16)
            return carry

        lax.fori_loop(0, t // BLK, blk, 0)

    return pl.pallas_call(
        body, name=name, out_shape=jax.ShapeDtypeStruct((N_Q, t, HEAD_DIM), BF16), grid=(N_Q,),
        in_specs=[qs, ks, ks, sspec], out_specs=qs, compiler_params=_params(("parallel",)),
    )(q, k, v, ss)


def _attn_bwd(q, k, v, do, ss, name):
    t = q.shape[1]
    qs, ks, sspec = _attn_specs(t)

    def body(q_ref, k_ref, v_ref, do_ref, ss_ref, dq_ref, dk_ref, dv_ref, ds_ref):
        h = pl.program_id(0)

        @pl.when(h % GROUP == 0)
        def _():
            dk_ref[...] = jnp.zeros_like(dk_ref)
            dv_ref[...] = jnp.zeros_like(dv_ref)

        def blk(n, dsink):
            s0, sp, q, kc, kp, pc, pp, ps, z = _attn_probs(q_ref, k_ref, ss_ref, h, n)
            rz = 1.0 / z
            pc = pc * rz
            pp = pp * rz
            do_b = do_ref[0, pl.ds(s0, BLK), :]
            dpc = _dot(do_b, v_ref[0, pl.ds(s0, BLK), :], NT)
            dpp = _dot(do_b, v_ref[0, pl.ds(sp, BLK), :], NT)
            delta = jnp.sum(pc * dpc, axis=1, keepdims=True) + jnp.sum(pp * dpp, axis=1, keepdims=True)
            dsc = pc * (dpc - delta)
            dsp = pp * (dpp - delta)
            dq_ref[0, pl.ds(s0, BLK), :] = ((_dot(dsc, kc, NN) + _dot(dsp, kp, NN)) * SCALE).astype(BF16)
            dk_ref[0, pl.ds(s0, BLK), :] += _dot(dsc, q, TN) * SCALE
            dk_ref[0, pl.ds(sp, BLK), :] += _dot(dsp, q, TN) * SCALE
            dv_ref[0, pl.ds(s0, BLK), :] += _dot(pc, do_b, TN)
            dv_ref[0, pl.ds(sp, BLK), :] += _dot(pp, do_b, TN)
            return dsink - ps * rz * delta

        dsink = lax.fori_loop(0, t // BLK, blk, jnp.zeros((BLK, 1), F32))
        ds_ref[...] = jnp.full(ds_ref.shape, jnp.sum(dsink), F32)

    kv = jax.ShapeDtypeStruct((N_KV, t, HEAD_DIM), F32)
    return pl.pallas_call(
        body, name=name,
        out_shape=(jax.ShapeDtypeStruct((N_Q, t, HEAD_DIM), BF16), kv, kv, jax.ShapeDtypeStruct((N_Q, 8, LANE), F32)),
        grid=(N_Q,), in_specs=[qs, ks, ks, qs, sspec],
        out_specs=(qs, ks, ks, pl.BlockSpec((1, 8, LANE), lambda h: (h, 0, 0))),
        compiler_params=_params(("arbitrary",)),
    )(q, k, v, do, ss)


def _heads(x2d, n):
    t = x2d.shape[0]
    return x2d.reshape(t, n, HEAD_DIM).transpose(1, 0, 2)


def _unheads(x3d):
    n, t, _ = x3d.shape
    return x3d.transpose(1, 0, 2).reshape(t, n * HEAD_DIM)


SMALL_ELEMS = 256 * 1024
TILE_ELEMS = 640 * 1024


def _row_tile(r, c):
    if r * c <= SMALL_ELEMS:
        return r
    return _pick(r, [t for t in (512, 256, 128, 64, 32, 16, 8) if t * c <= TILE_ELEMS])


def _adamw(w, g, m, v, name):
    r, c = w.shape
    tr = _row_tile(r, c)
    spec = pl.BlockSpec((tr, c), lambda i: (i, 0))

    def body(w_ref, g_ref, m_ref, v_ref, d_ref, nm_ref, nv_ref):
        gv = g_ref[...]
        nm = ADAM_B1 * m_ref[...] + (1.0 - ADAM_B1) * gv
        nv = ADAM_B2 * v_ref[...] + (1.0 - ADAM_B2) * (gv * gv)
        m_hat = nm / (1.0 - ADAM_B1 ** ADAM_STEP)
        v_hat = nv / (1.0 - ADAM_B2 ** ADAM_STEP)
        d_ref[...] = -ADAM_LR * (m_hat / (jnp.sqrt(v_hat) + ADAM_EPS) + ADAM_WD * w_ref[...])
        nm_ref[...] = nm
        nv_ref[...] = nv

    shp = jax.ShapeDtypeStruct((r, c), F32)
    return pl.pallas_call(
        body, name=name, out_shape=(shp, shp, shp), grid=(r // tr,), in_specs=[spec] * 4, out_specs=(spec,) * 3,
        compiler_params=_params(("parallel",)),
    )(w, g, m, v)


def _sum_leading(x, name):
    n, r, c = x.shape
    tr = _row_tile(r, c)

    def body(x_ref, o_ref):
        acc = x_ref[0]
        for i in range(1, n):
            acc = acc + x_ref[i]
        o_ref[...] = acc

    return pl.pallas_call(
        body, name=name, out_shape=jax.ShapeDtypeStruct((r, c), F32), grid=(r // tr,),
        in_specs=[pl.BlockSpec((n, tr, c), lambda i: (0, i, 0))], out_specs=pl.BlockSpec((tr, c), lambda i: (i, 0)),
        compiler_params=_params(("parallel",)),
    )(x)


def _sum_own_plus(p, sel, recv, name, out_dtype):
    _, r, c = p.shape
    n = recv.shape[0]
    tr = _pick(r, (256, 128, 16))

    def body(sel_ref, p_ref, r_ref, o_ref):
        acc = p_ref[0].astype(F32)
        for i in range(n):
            acc = acc + r_ref[i].astype(F32)
        o_ref[...] = acc.astype(out_dtype)

    grid_spec = pltpu.PrefetchScalarGridSpec(
        num_scalar_prefetch=1, grid=(r // tr,),
        in_specs=[pl.BlockSpec((1, tr, c), lambda i, s: (s[0], i, 0)), pl.BlockSpec((n, tr, c), lambda i, s: (0, i, 0))],
        out_specs=pl.BlockSpec((tr, c), lambda i, s: (i, 0)))
    return pl.pallas_call(
        body, name=name, out_shape=jax.ShapeDtypeStruct((r, c), out_dtype), grid_spec=grid_spec,
        compiler_params=_params(("parallel",)),
    )(sel, p, recv)


def _coords():
    return lax.axis_index("x"), lax.axis_index("y"), lax.axis_index("c")


def _allgather8(x2, name, space):
    _, m, n = x2.shape

    def body(x_ref, out_ref, send_sems, recv_sems, local_sem):
        x, y, c = _coords()
        me, sibling = (x, y, c), (x, y, 1 - c)
        chips = [(1 - x, y), (x, 1 - y), (1 - x, 1 - y)]
        mine_src = x_ref.at[c]

        def rows(px, py, pc):
            return out_ref.at[4 * px + 2 * py + pc]

        def copy(k, block, to, src=None):
            return pltpu.make_async_remote_copy(
                src_ref=rows(*block) if src is None else src, dst_ref=rows(*block),
                send_sem=send_sems.at[k], recv_sem=recv_sems.at[k], device_id=to, device_id_type=MESH)

        mine = pltpu.make_async_copy(mine_src, rows(*me), local_sem)
        mine.start()
        first = [copy(0, me, sibling, src=mine_src)]
        first += [copy(1 + j, me, (*chip, c), src=mine_src) for j, chip in enumerate(chips)]
        for cp in first:
            cp.start()
        passed = [copy(4 + j, (*chip, c), sibling) for j, chip in enumerate(chips)]
        for j, chip in enumerate(chips):
            copy(1 + j, (*chip, c), me).wait_recv()
            passed[j].start()
        copy(0, sibling, me).wait_recv()
        for j, chip in enumerate(chips):
            copy(4 + j, (*chip, 1 - c), me).wait_recv()
        for cp in first + passed:
            cp.wait_send()
        mine.wait()

    return pl.pallas_call(
        body, name=name, out_shape=jax.ShapeDtypeStruct((8, m, n), x2.dtype),
        in_specs=[pl.BlockSpec(memory_space=space)], out_specs=pl.BlockSpec(memory_space=space),
        scratch_shapes=[pltpu.SemaphoreType.DMA((7,)), pltpu.SemaphoreType.DMA((7,)), pltpu.SemaphoreType.DMA],
        compiler_params=pltpu.CompilerParams(vmem_limit_bytes=VMEM_LIMIT),
    )(x2)


def _swap_sibling_half(p, name):
    def body(p_ref, out_ref, send_sem, recv_sem):
        x, y, c = _coords()
        cp = pltpu.make_async_remote_copy(
            src_ref=p_ref.at[1 - c], dst_ref=out_ref, send_sem=send_sem, recv_sem=recv_sem,
            device_id=(x, y, 1 - c), device_id_type=MESH)
        cp.start()
        cp.wait()

    return pl.pallas_call(
        body, name=name, out_shape=jax.ShapeDtypeStruct(p.shape[1:], p.dtype),
        in_specs=[pl.BlockSpec(memory_space=pl.ANY)], out_specs=pl.BlockSpec(memory_space=pl.ANY),
        scratch_shapes=[pltpu.SemaphoreType.DMA, pltpu.SemaphoreType.DMA],
    )(p)


N_REG = len(ROW_REGIONS) + 1


def _chip_window(ref, lead, r, j):
    view = ref if lead is None else ref.at[lead]
    if r < len(ROW_REGIONS):
        off, rows = ROW_REGIONS[r]
        return view.at[pl.ds(pl.multiple_of(off + j * rows, 16), rows), :]
    return view.at[pl.ds(R_OUT, OUT_ROWS), pl.ds(pl.multiple_of(j * OUT_COLS, LANE), OUT_COLS)]


def _gather_arena(shards, name):
    def body(*refs):
        srcs, out_ref, send_sems, recv_sems = refs[:N_REG], refs[N_REG], refs[N_REG + 1], refs[N_REG + 2]
        x, y, c = _coords()
        chips = [(1 - x, y), (x, 1 - y), (1 - x, 1 - y)]
        sibling = (x, y, 1 - c)

        def copy(r, k, layer, j, to, src=None):
            win = _chip_window(out_ref, layer, r, j)
            return pltpu.make_async_remote_copy(
                src_ref=win if src is None else src, dst_ref=win, send_sem=send_sems.at[r, k],
                recv_sem=recv_sems.at[r, k], device_id=to, device_id_type=MESH)

        first = [copy(r, k, c, 2 * x + y, (cx, cy, c), src=srcs[r].at[c])
                 for k, (cx, cy) in enumerate(chips) for r in range(N_REG)]
        for cp in first:
            cp.start()
        passed = []
        for k, (cx, cy) in enumerate(chips):
            for r in range(N_REG):
                copy(r, k, c, 2 * cx + cy, sibling).wait_recv()
                fwd = copy(r, 3 + k, c, 2 * cx + cy, sibling)
                fwd.start()
                passed.append(fwd)
        for k, (cx, cy) in enumerate(chips):
            for r in range(N_REG):
                copy(r, 3 + k, 1 - c, 2 * cx + cy, sibling).wait_recv()
        for cp in first + passed:
            cp.wait_send()

    return pl.pallas_call(
        body, name=name, out_shape=jax.ShapeDtypeStruct((DEPTH, ARENA_ROWS, ARENA_W), BF16),
        in_specs=[pl.BlockSpec(memory_space=pl.ANY)] * N_REG, out_specs=pl.BlockSpec(memory_space=pl.ANY),
        scratch_shapes=[pltpu.SemaphoreType.DMA((N_REG, 6)), pltpu.SemaphoreType.DMA((N_REG, 6))],
    )(*shards)


def _place_own(arena, shards, chip):
    for (off, rows), s in zip(ROW_REGIONS, shards[:-1]):
        arena = lax.dynamic_update_slice(arena, s, (0, off + chip * rows, 0))
    return lax.dynamic_update_slice(arena, shards[-1], (0, R_OUT, chip * OUT_COLS))


def _scatter_pieces(s, name):
    def body(s_ref, main_ref, outp_ref, send_sems, recv_sems):
        x, y, c = _coords()
        chips = [(1 - x, y), (x, 1 - y), (1 - x, 1 - y)]
        cps = []
        for k, (cx, cy) in enumerate(chips):
            for r in range(N_REG):
                if r < len(ROW_REGIONS):
                    dst = main_ref.at[k, pl.ds(PIECE_OFF[r], ROW_REGIONS[r][1]), :]
                else:
                    dst = outp_ref.at[k]
                cps.append(pltpu.make_async_remote_copy(
                    src_ref=_chip_window(s_ref, None, r, 2 * cx + cy), dst_ref=dst, send_sem=send_sems.at[r, k],
                    recv_sem=recv_sems.at[r, k], device_id=(cx, cy, c), device_id_type=MESH))
        for cp in cps:
            cp.start()
        for cp in cps:
            cp.wait()

    return pl.pallas_call(
        body, name=name,
        out_shape=(jax.ShapeDtypeStruct((3, PIECE_ROWS, ARENA_W), s.dtype),
                   jax.ShapeDtypeStruct((3, OUT_ROWS, OUT_COLS), s.dtype)),
        in_specs=[pl.BlockSpec(memory_space=pl.ANY)], out_specs=(pl.BlockSpec(memory_space=pl.ANY),) * 2,
        scratch_shapes=[pltpu.SemaphoreType.DMA((N_REG, 3)), pltpu.SemaphoreType.DMA((N_REG, 3))],
    )(s)


def _own_piece(s, chip):
    main = jnp.concatenate([lax.dynamic_slice(s, (off + chip * rows, 0), (rows, ARENA_W))
                            for off, rows in ROW_REGIONS])
    return main, lax.dynamic_slice(s, (R_OUT, chip * OUT_COLS), (OUT_ROWS, OUT_COLS))


def _swap_pair(a, b, name):
    def body(a_ref, b_ref, ra_ref, rb_ref, send_sems, recv_sems):
        x, y, c = _coords()
        cps = [pltpu.make_async_remote_copy(
            src_ref=s, dst_ref=d, send_sem=send_sems.at[i], recv_sem=recv_sems.at[i], device_id=(x, y, 1 - c),
            device_id_type=MESH) for i, (s, d) in enumerate(((a_ref, ra_ref), (b_ref, rb_ref)))]
        for cp in cps:
            cp.start()
        for cp in cps:
            cp.wait()

    return pl.pallas_call(
        body, name=name,
        out_shape=(jax.ShapeDtypeStruct(a.shape, a.dtype), jax.ShapeDtypeStruct(b.shape, b.dtype)),
        in_specs=[pl.BlockSpec(memory_space=pl.ANY)] * 2, out_specs=(pl.BlockSpec(memory_space=pl.ANY),) * 2,
        scratch_shapes=[pltpu.SemaphoreType.DMA((2,)), pltpu.SemaphoreType.DMA((2,))],
    )(a, b)


OUT_NAMES = ("w_a_out", "w_b_out", "w_c_out", "w_d_out")


def _arena_shards(w):
    t = lambda a: a.astype(BF16).transpose(0, 2, 1)
    return (w["w_ffn_down"].astype(BF16), t(w["w_ffn_gate"]), t(w["w_ffn_up"]), t(w["w_in"]), w["w_o"].astype(BF16),
            jnp.concatenate([w[n].astype(BF16) for n in OUT_NAMES], axis=1))


def _shard_grads(main, outp):
    t = lambda r: main[:, PIECE_OFF[r]:PIECE_OFF[r] + ROW_REGIONS[r][1]]
    g = dict(w_ffn_down=t(0), w_ffn_gate=t(1).transpose(0, 2, 1), w_ffn_up=t(2).transpose(0, 2, 1),
             w_in=t(3).transpose(0, 2, 1), w_o=t(4))
    for i, n in enumerate(OUT_NAMES):
        g[n] = outp[:, i * BW:(i + 1) * BW]
    return g


def _gather_taps(p, name):
    mine = jnp.concatenate([p[n] for n in CONV_NAMES], axis=1).reshape(DEPTH * N_TAPS, LANE)
    rows = -(-mine.shape[0] // 8) * 8
    mine = jnp.concatenate([mine, jnp.zeros((rows - mine.shape[0], LANE), F32)])
    g = _allgather8(jnp.stack([mine, mine]), name, pltpu.VMEM)[0::2, :DEPTH * N_TAPS]
    full = g.reshape(4, DEPTH, N_TAPS, LANE).transpose(1, 2, 0, 3).reshape(DEPTH, N_TAPS, BW)
    return dict(conv_a_w=full[:, :CONV_A], conv_b_w=full[:, CONV_A:CONV_A + CONV_B], conv_d_w=full[:, CONV_A + CONV_B:])


def _flat_pack(arrs):
    flat = jnp.concatenate([a.reshape(-1).astype(F32) for a in arrs])
    rows = -(-flat.shape[0] // (8 * LANE)) * 8
    return jnp.concatenate([flat, jnp.zeros((rows * LANE - flat.shape[0],), F32)]).reshape(rows, LANE)


def _flat_unpack(packed, shapes):
    flat, out, off = packed.reshape(-1), [], 0
    for s in shapes:
        cnt = int(np.prod(s))
        out.append(flat[off:off + cnt].reshape(s))
        off += cnt
    return out


def _blockdiag_chunks(w):
    w4 = w.reshape(4, 2, 64, 64)
    z = jnp.zeros((4, 2, 64, 2, 64), F32)
    z = z.at[:, 0, :, 0, :].set(w4[:, 0]).at[:, 1, :, 1, :].set(w4[:, 1])
    return z.reshape(4, LANE, LANE)


def _blockdiag_extract(d):
    d5 = d.reshape(4, 2, 64, 2, 64)
    return jnp.stack([d5[:, 0, :, 0, :], d5[:, 1, :, 1, :]], axis=1).reshape(8, 64, 64)


SLOPES = np.asarray([2.0 ** (-8.0 * (i + 1) / N_Q) for i in range(N_Q)], np.float32)


def _layer_consts(p, fw, l):
    row = lambda a: a[l].reshape(1, -1)
    return dict(
        g1=row(p["norm1_g"]), g2=row(p["norm2_g"]), wA=fw["conv_a_w"][l], bA=row(p["conv_a_b"]),
        wx=_blockdiag_chunks(p["lru_wx"][l]), bx=row(p["lru_bx"]), wa=_blockdiag_chunks(p["lru_wa"][l]),
        ba=row(p["lru_ba"]), lam=row(p["lru_lambda"]), wB=fw["conv_b_w"][l],
        ss=jnp.stack([p["sinks"][l], jnp.asarray(SLOPES)]), wD=fw["conv_d_w"][l], bD=row(p["conv_d_b"]),
        lg=row(p["ln_d_g"]), lb=row(p["ln_d_b"]))


def _layer_fwd(x, c, fw, l):
    t = f"l{l}_"
    xn = _rms_fwd(x, c["g1"], t + "rms1")
    wt = lambda off, rows: Win(fw["arena"], l, off, rows)
    proj = _mm(xn, wt(R_IN, IN_W), "nt", t + "proj")
    ya = _a_fwd(proj, c["wA"], c["bA"], c["wx"], c["bx"], c["wa"], c["ba"], c["lam"], t + "a_fwd")
    yb = _b_fwd(proj, c["wB"], t + "b_fwd")
    q3 = _heads(proj[:, OFF_Q:OFF_K], N_Q)
    k3 = _heads(proj[:, OFF_K:OFF_V], N_KV)
    v3 = _heads(proj[:, OFF_V:OFF_V + N_KV * HEAD_DIM], N_KV)
    yc = _unheads(_attn_fwd(q3, k3, v3, c["ss"], t + "attn_fwd"))
    cd = _d_conv_fwd(proj, c["wD"], c["bD"], t + "d_conv_fwd")
    yd = _ln_silu_fwd(cd, c["lg"], c["lb"], t + "d_ln_fwd")
    ys = (ya, yb, yc, yd)
    big_y = tuple(_mm(y, wt(R_OUT + i * BW, BW), "nn", t + f"out{i}") for i, y in enumerate(ys))
    merged = _merge_fwd(proj, big_y, t + "merge_fwd")
    hres = _mm(merged, wt(R_O, D_MODEL), "nn", t + "wo", add=x)
    hn = _rms_fwd(hres, c["g2"], t + "rms2")
    gg = _mm(hn, wt(R_GATE, D_FF), "nt", t + "ffn_gate")
    uu = _mm(hn, wt(R_UP, D_FF), "nt", t + "ffn_up")
    act = _swiglu_fwd(gg, uu, t + "swiglu_fwd")
    xout = _mm(act, wt(R_DOWN, D_FF), "nn", t + "ffn_down", add=hres)
    saved = dict(x=x, xn=xn, proj=proj, ys=ys, q3=q3, k3=k3, v3=v3, cd=cd, big_y=big_y, merged=merged, hres=hres,
                 hn=hn, gg=gg, uu=uu, act=act)
    return xout, saved


def _layer_bwd(dxout, s, c, fw, l, ga):
    t = f"l{l}_"
    gs = {}
    wt = lambda off, rows: Win(fw["arena"], l, off, rows)
    gt = lambda off, rows: Win(ga, l, off, rows)
    dact = _mm(dxout, wt(R_DOWN, D_FF), "nt", t + "d_act")
    ga = _mm(s["act"], dxout, "tn", t + "dw_down", out=gt(R_DOWN, D_FF))
    dgg, duu = _swiglu_bwd(s["gg"], s["uu"], dact, t + "swiglu_bwd")
    ga = _mm(dgg, s["hn"], "tn", t + "dw_gate", out=gt(R_GATE, D_FF))
    ga = _mm(duu, s["hn"], "tn", t + "dw_up", out=gt(R_UP, D_FF))
    dhn = _mm(dgg, wt(R_GATE, D_FF), "nn", t + "d_hn_g")
    dhn = _mm(duu, wt(R_UP, D_FF), "nn", t + "d_hn_u", add=dhn)
    dhres, gs["norm2_g"] = _rms_bwd(s["hres"], c["g2"], dhn, dxout, t + "rms2_bwd")
    dmerged = _mm(dhres, wt(R_O, D_MODEL), "nt", t + "d_merged")
    ga = _mm(s["merged"], dhres, "tn", t + "dw_o", out=gt(R_O, D_MODEL))
    dbig_y, dgl = _merge_bwd(s["proj"], s["big_y"], dmerged, t + "merge_bwd")
    dys = []
    for i in range(4):
        ga = _mm(s["ys"][i], dbig_y[i], "tn", t + f"dw_out{i}", out=gt(R_OUT + i * BW, BW))
        dys.append(_mm(dbig_y[i], wt(R_OUT + i * BW, BW), "nt", t + f"d_y{i}"))
    proj = s["proj"]
    (dax, dag, gs["conv_a_w"], gs["conv_a_b"], dwx, gs["lru_bx"], dwa, gs["lru_ba"], gs["lru_lambda"]) = _a_bwd(
        proj, dys[0], c["wA"], c["bA"], c["wx"], c["bx"], c["wa"], c["ba"], c["lam"], t + "a_bwd")
    gs["lru_wx"] = _blockdiag_extract(dwx)
    gs["lru_wa"] = _blockdiag_extract(dwa)
    dbv, dbc, dbb, gs["conv_b_w"] = _b_bwd(proj, dys[1], c["wB"], t + "b_bwd")
    dq3, dk3, dv3, dsink = _attn_bwd(s["q3"], s["k3"], s["v3"], _heads(dys[2], N_Q), c["ss"], t + "attn_bwd")
    gs["sinks"] = dsink[:, 0, 0]
    dcd, gs["ln_d_g"], gs["ln_d_b"] = _ln_silu_bwd(s["cd"], c["lg"], c["lb"], dys[3], t + "d_ln_bwd")
    dd1, dd2, gs["conv_d_w"], gs["conv_d_b"] = _d_conv_bwd(proj, dcd, c["wD"], t + "d_conv_bwd")
    dproj = jnp.concatenate(
        [dax, dag, dbv, dbc, dbb, _unheads(dq3), _unheads(dk3).astype(BF16), _unheads(dv3).astype(BF16), dd1, dd2,
         *dgl], axis=1)
    ga = _mm(dproj, s["xn"], "tn", t + "dw_in", out=gt(R_IN, IN_W))
    dxn = _mm(dproj, wt(R_IN, IN_W), "nn", t + "d_xn")
    dx, gs["norm1_g"] = _rms_bwd(s["x"], c["g1"], dxn, dhres, t + "rms1_bwd")
    return dx, ga, gs


def kernel(x, norm1_g, w_in, conv_a_w, conv_a_b, lru_wx, lru_bx, lru_wa, lru_ba, lru_lambda, w_a_out, conv_b_w, w_b_out, sinks, w_c_out, conv_d_w, conv_d_b, ln_d_g, ln_d_b, w_d_out, w_o, norm2_g, w_ffn_gate, w_ffn_up, w_ffn_down, final_g, loss_target, m_norm1_g, m_w_in, m_conv_a_w, m_conv_a_b, m_lru_wx, m_lru_bx, m_lru_wa, m_lru_ba, m_lru_lambda, m_w_a_out, m_conv_b_w, m_w_b_out, m_sinks, m_w_c_out, m_conv_d_w, m_conv_d_b, m_ln_d_g, m_ln_d_b, m_w_d_out, m_w_o, m_norm2_g, m_w_ffn_gate, m_w_ffn_up, m_w_ffn_down, m_final_g, v_norm1_g, v_w_in, v_conv_a_w, v_conv_a_b, v_lru_wx, v_lru_bx, v_lru_wa, v_lru_ba, v_lru_lambda, v_w_a_out, v_conv_b_w, v_w_b_out, v_sinks, v_w_c_out, v_conv_d_w, v_conv_d_b, v_ln_d_g, v_ln_d_b, v_w_d_out, v_w_o, v_norm2_g, v_w_ffn_gate, v_w_ffn_up, v_w_ffn_down, v_final_g):
    given = dict(locals())
    p = {n: given[n] for n in NAMES}
    mom = {n: given["m_" + n] for n in NAMES}
    var = {n: given["v_" + n] for n in NAMES}
    cx, cy, cc = _coords()
    chip = 2 * cx + cy

    shards = _arena_shards(p)
    fw = _gather_taps(p, "gather_taps")
    fw["arena"] = _place_own(_gather_arena(shards, "gather_weights"), shards, chip)

    h = x[0]
    consts, saved = [], []
    for l in range(DEPTH):
        consts.append(_layer_consts(p, fw, l))
        h, s = _layer_fwd(h, consts[l], fw, l)
        saved.append(s)
    loss_vec, dh, g_final = _loss_head(h, final_g.reshape(1, -1), loss_target[0], "loss_head")
    loss = lax.psum(loss_vec[0, 0], ("x", "y", "c"))

    gss = [None] * DEPTH
    ga = jnp.zeros((DEPTH, ARENA_ROWS, ARENA_W), BF16)
    for l in reversed(range(DEPTH)):
        dh, ga, gss[l] = _layer_bwd(dh, saved[l], consts[l], fw, l, ga)
    grad_x = dh[None]

    zero = jnp.zeros((1,), jnp.int32)
    from_sibling = _swap_sibling_half(ga[:, None], "grads_swap_sibling")
    chip_sum = _sum_own_plus(ga, cc.reshape(1).astype(jnp.int32), from_sibling, "grads_sum_chip", BF16)
    recv_main, recv_out = _scatter_pieces(chip_sum, "grads_scatter_chips")
    own_main, own_out = _own_piece(chip_sum, chip)
    red_main = _sum_own_plus(own_main[None], zero, recv_main, "grads_sum_all", F32)
    red_out = _sum_own_plus(own_out[None], zero, recv_out, "grads_sum_all_out", F32)
    sib_main, sib_out = _swap_pair(red_main, red_out, "grads_swap_reduced")
    by_layer = lambda mine, theirs: jnp.stack([jnp.where(cc == l, mine, theirs) for l in range(DEPTH)])
    g = _shard_grads(by_layer(red_main, sib_main), by_layer(red_out, sib_out))

    small_full = {n: (g_final.reshape(-1) if n == "final_g" else
                      jnp.stack([gss[l][n].reshape(gss[l][n].shape[-2:] if n.startswith("conv") and n.endswith("_w")
                                                   else p[n].shape[1:]) for l in range(DEPTH)]))
                  for n in SMALL}
    part = _flat_pack([small_full[n] for n in SMALL])
    rows = part.shape[0]
    gathered = _allgather8(jnp.stack([part, part]), "gather_small_grads", pltpu.VMEM)
    small_sum = _flat_unpack(_sum_leading(gathered, "small_grads_sum"), [small_full[n].shape for n in SMALL])
    for n, a in zip(SMALL, small_sum):
        g[n] = lax.dynamic_slice_in_dim(a, chip * LANE, LANE, axis=2) if n in CONV_NAMES else a

    delta, new_m, new_v = {}, {}, {}
    for n in BIG:
        shp = p[n].shape
        two_d = lambda a: a.reshape(-1, shp[-1])
        d, nm, nv = _adamw(two_d(p[n]), two_d(g[n]), two_d(mom[n]), two_d(var[n]), "adamw_" + n)
        delta[n], new_m[n], new_v[n] = d.reshape(shp), nm.reshape(shp), nv.reshape(shp)
    shapes = [p[n].shape for n in SMALL]
    d, nm, nv = _adamw(_flat_pack([p[n] for n in SMALL]), _flat_pack([g[n] for n in SMALL]),
                       _flat_pack([mom[n] for n in SMALL]), _flat_pack([var[n] for n in SMALL]), "adamw_small")
    for n, a, b, cval in zip(SMALL, _flat_unpack(d, shapes), _flat_unpack(nm, shapes), _flat_unpack(nv, shapes)):
        delta[n], new_m[n], new_v[n] = a, b, cval

    return (loss, grad_x, *[g[n] for n in NAMES], *[delta[n] for n in NAMES], *[new_m[n] for n in NAMES],
            *[new_v[n] for n in NAMES])
```

```python
import functools
import math

import numpy as np
import jax
import jax.numpy as jnp
from jax import lax
from jax.experimental import pallas as pl
from jax.experimental.pallas import tpu as pltpu

F32 = jnp.float32
BF16 = jnp.bfloat16
MESH = pl.DeviceIdType.MESH

D_MODEL = 1024
DEPTH = 2
BW = 512
HEAD_DIM = 64
N_Q = 8
N_KV = 2
BLK = 128
D_FF = 2816
IN_W = 8448
EPS = 1e-6
NEG_INF = -1e30
LRU_C = 8.0
CONV_A, CONV_B, CONV_D = 4, 3, 31
LANE = 128
ROW_TILE = 256
VMEM_LIMIT = 56 * 1024 * 1024
MM_VMEM_BUDGET = 36 * 1024 * 1024

C_AX, C_AG, C_BV, C_BC, C_BB = 0, 4, 8, 12, 16
OFF_Q, OFF_K, OFF_V = 2560, 3072, 3200
C_D1, C_D2 = 26, 30
OFF_GL = 4352

ADAM_LR, ADAM_B1, ADAM_B2, ADAM_EPS, ADAM_WD, ADAM_STEP = 0.001, 0.9, 0.999, 1e-08, 0.01, 10

ARENA_W = 1024
R_DOWN, R_GATE, R_UP, R_IN, R_O, R_OUT = 0, 2816, 5632, 8448, 16896, 17920
ARENA_ROWS = 19968
ROW_REGIONS = ((R_DOWN, 704), (R_GATE, 704), (R_UP, 704), (R_IN, 2112), (R_O, 256))
PIECE_OFF = (0, 704, 1408, 2112, 4224)
PIECE_ROWS = 4480
OUT_ROWS, OUT_COLS = 4 * BW, D_MODEL // 4

BIG = ("w_in", "w_a_out", "w_b_out", "w_c_out", "w_d_out", "w_o", "w_ffn_gate", "w_ffn_up", "w_ffn_down")
CONV_NAMES = ("conv_a_w", "conv_b_w", "conv_d_w")
N_TAPS = CONV_A + CONV_B + CONV_D
SMALL = ("norm1_g", "conv_a_w", "conv_a_b", "lru_wx", "lru_bx", "lru_wa", "lru_ba", "lru_lambda", "conv_b_w",
         "sinks", "conv_d_w", "conv_d_b", "ln_d_g", "ln_d_b", "norm2_g", "final_g")
NAMES = ['norm1_g', 'w_in', 'conv_a_w', 'conv_a_b', 'lru_wx', 'lru_bx', 'lru_wa', 'lru_ba', 'lru_lambda', 'w_a_out',
         'conv_b_w', 'w_b_out', 'sinks', 'w_c_out', 'conv_d_w', 'conv_d_b', 'ln_d_g', 'ln_d_b', 'w_d_out', 'w_o',
         'norm2_g', 'w_ffn_gate', 'w_ffn_up', 'w_ffn_down', 'final_g']


def _pick(n, cands, off=0):
    for c in cands:
        if n % c == 0 and off % c == 0:
            return c
    assert off == 0, (n, off)
    return n


class Win:
    def __init__(self, arena, l, off, rows):
        self.arena, self.l, self.off, self.rows = arena, l, off, rows
        self.shape = (rows, arena.shape[-1])


def _params(sem=None):
    return pltpu.CompilerParams(dimension_semantics=sem, vmem_limit_bytes=VMEM_LIMIT)


def _sig(z):
    return 1.0 / (1.0 + jnp.exp(-z))


def _dot(a, b, dims):
    return lax.dot_general(a.astype(BF16), b.astype(BF16), (dims, ((), ())), preferred_element_type=F32)


NN = ((1,), (0,))
NT = ((1,), (1,))
TN = ((0,), (0,))


def _mm(a, b, mode, name, out_dtype=F32, add=None, out=None):
    if mode == "nn":
        (m, k), n = a.shape, b.shape[1]
    elif mode == "nt":
        (m, k), n = a.shape, b.shape[0]
    else:
        (k, m), n = a.shape, b.shape[1]
    b_win = isinstance(b, Win)
    b_off = b.off if b_win else 0
    o_off = out.off if out is not None else 0
    if out is not None:
        out_dtype = out.arena.dtype
    tk = _pick(k, (2816, 2048, 1408, 1024, 768, 512, 256), b_off if mode != "nt" else 0)
    nk = k // tk
    n_off = b_off if mode == "nt" else 0
    a_bytes, b_bytes, o_bytes = a.dtype.itemsize, 2, jnp.dtype(out_dtype).itemsize

    def vmem_bytes(tm_, tn_):
        tile = tm_ * tn_
        return (2 * tk * (tm_ * a_bytes + tn_ * b_bytes) + 2 * tile * o_bytes + (tile * 4 if nk > 1 else 0)
                + (2 * tile * 4 if add is not None else 0) + tile * 4)

    pairs = [(tm_, tn_) for tm_ in (2048, 1024, 768, 512, 256, 128) for tn_ in (1024, 768, 512, 256, 128)
             if m % tm_ == 0 and o_off % tm_ == 0 and n % tn_ == 0 and n_off % tn_ == 0
             and vmem_bytes(tm_, tn_) <= MM_VMEM_BUDGET]
    tm, tn = max(pairs, key=lambda p: (p[0] * p[1], p[0]))
    dims = {"nn": NN, "nt": NT, "tn": TN}[mode]

    def body(*refs):
        a_ref, b_ref = refs[:2]
        c_ref = refs[2] if add is not None else None
        if nk == 1:
            r = _dot(a_ref[...], b_ref[...], dims)
            if add is not None:
                r = r + c_ref[...]
            refs[-1][...] = r.astype(out_dtype)
            return
        o_ref, acc = refs[-2:]
        kk = pl.program_id(2)

        @pl.when(kk == 0)
        def _():
            acc[...] = jnp.zeros_like(acc)

        acc[...] += _dot(a_ref[...], b_ref[...], dims)

        @pl.when(kk == nk - 1)
        def _():
            r = acc[...]
            if add is not None:
                r = r + c_ref[...]
            o_ref[...] = r.astype(out_dtype)

    if mode == "tn":
        a_spec = pl.BlockSpec((tk, tm), lambda i, j, q: (q, i))
    else:
        a_spec = pl.BlockSpec((tm, tk), lambda i, j, q: (i, q))
    if mode == "nt":
        b_blk, b_idx = (tn, tk), (lambda i, j, q: (b_off // tn + j, q))
    else:
        b_blk, b_idx = (tk, tn), (lambda i, j, q: (b_off // tk + q, j))
    if b_win and b.arena.ndim == 3:
        bl = b.l
        b_spec = pl.BlockSpec((None,) + b_blk, lambda i, j, q: (bl,) + b_idx(i, j, q))
    else:
        b_spec = pl.BlockSpec(b_blk, b_idx)
    plain_o = pl.BlockSpec((tm, tn), lambda i, j, q: (i, j))
    in_specs = [a_spec, b_spec] + ([plain_o] if add is not None else [])
    args = (a, b.arena if b_win else b) + ((add,) if add is not None else ())
    aliases = {}
    if out is None:
        o_spec, o_shape = plain_o, jax.ShapeDtypeStruct((m, n), out_dtype)
    else:
        ol = out.l
        o_spec = pl.BlockSpec((None, tm, tn), lambda i, j, q: (ol, o_off // tm + i, j))
        o_shape = jax.ShapeDtypeStruct(out.arena.shape, out_dtype)
        aliases = {len(args): 0}
        in_specs.append(pl.BlockSpec(memory_space=pl.ANY))
        args = args + (out.arena,)
    return pl.pallas_call(
        body, name=name, out_shape=o_shape,
        grid=(m // tm, n // tn, nk), in_specs=in_specs, out_specs=o_spec,
        scratch_shapes=[pltpu.VMEM((tm, tn), F32)] if nk > 1 else [], input_output_aliases=aliases,
        compiler_params=_params(("parallel", "parallel", "arbitrary")),
    )(*args)


def _row_spec(cols, tr=ROW_TILE):
    return pl.BlockSpec((tr, cols), lambda i: (i, 0))


def _vec_spec(cols):
    return pl.BlockSpec((1, cols), lambda i: (0, 0))


def _rms_fwd(x, g, name):
    t, d = x.shape

    def body(x_ref, g_ref, o_ref):
        xv = x_ref[...]
        r = lax.rsqrt(jnp.mean(xv * xv, axis=1, keepdims=True) + EPS)
        o_ref[...] = (xv * r * g_ref[...]).astype(BF16)

    return pl.pallas_call(
        body, name=name, out_shape=jax.ShapeDtypeStruct((t, d), BF16), grid=(t // ROW_TILE,),
        in_specs=[_row_spec(d), _vec_spec(d)], out_specs=_row_spec(d), compiler_params=_params(("parallel",)),
    )(x, g)


def _rms_bwd(x, g, dxn, dres, name):
    t, d = x.shape

    def body(x_ref, g_ref, dy_ref, dr_ref, dx_ref, dg_ref):
        @pl.when(pl.program_id(0) == 0)
        def _():
            dg_ref[...] = jnp.zeros_like(dg_ref)

        xv = x_ref[...]
        dy = dy_ref[...]
        r = lax.rsqrt(jnp.mean(xv * xv, axis=1, keepdims=True) + EPS)
        w = dy * g_ref[...]
        dx_ref[...] = dr_ref[...] + r * w - xv * (r * r * r) * jnp.mean(w * xv, axis=1, keepdims=True)
        dg_ref[...] += jnp.sum(dy * xv * r, axis=0, keepdims=True)

    return pl.pallas_call(
        body, name=name,
        out_shape=(jax.ShapeDtypeStruct((t, d), F32), jax.ShapeDtypeStruct((1, d), F32)), grid=(t // ROW_TILE,),
        in_specs=[_row_spec(d), _vec_spec(d), _row_spec(d), _row_spec(d)], out_specs=(_row_spec(d), _vec_spec(d)),
        compiler_params=_params(("arbitrary",)),
    )(x, g, dxn, dres)


def _loss_head(x, g, tgt, name):
    t, d = x.shape

    def body(x_ref, g_ref, t_ref, loss_ref, dx_ref, dg_ref):
        @pl.when(pl.program_id(0) == 0)
        def _():
            dg_ref[...] = jnp.zeros_like(dg_ref)
            loss_ref[...] = jnp.zeros_like(loss_ref)

        xv = x_ref[...]
        gv = g_ref[...]
        r = lax.rsqrt(jnp.mean(xv * xv, axis=1, keepdims=True) + EPS)
        e = xv * r * gv - t_ref[...]
        loss_ref[...] += jnp.full(loss_ref.shape, (0.5 / d) * jnp.sum(e * e), F32)
        dy = e * (1.0 / d)
        w = dy * gv
        dx_ref[...] = r * w - xv * (r * r * r) * jnp.mean(w * xv, axis=1, keepdims=True)
        dg_ref[...] += jnp.sum(dy * xv * r, axis=0, keepdims=True)

    return pl.pallas_call(
        body, name=name,
        out_shape=(jax.ShapeDtypeStruct((1, LANE), F32), jax.ShapeDtypeStruct((t, d), F32),
                   jax.ShapeDtypeStruct((1, d), F32)),
        grid=(t // ROW_TILE,), in_specs=[_row_spec(d), _vec_spec(d), _row_spec(d)],
        out_specs=(_vec_spec(LANE), _row_spec(d), _vec_spec(d)), compiler_params=_params(("arbitrary",)),
    )(x, g, tgt)


def _swiglu_fwd(gg, uu, name):
    t, f = gg.shape

    def body(g_ref, u_ref, o_ref):
        gv = g_ref[...]
        o_ref[...] = (gv * _sig(gv) * u_ref[...]).astype(BF16)

    return pl.pallas_call(
        body, name=name, out_shape=jax.ShapeDtypeStruct((t, f), BF16), grid=(t // ROW_TILE,),
        in_specs=[_row_spec(f), _row_spec(f)], out_specs=_row_spec(f), compiler_params=_params(("parallel",)),
    )(gg, uu)


def _swiglu_bwd(gg, uu, dact, name):
    t, f = gg.shape

    def body(g_ref, u_ref, d_ref, dg_ref, du_ref):
        gv = g_ref[...]
        dv = d_ref[...]
        s = _sig(gv)
        dg_ref[...] = (dv * u_ref[...] * s * (1.0 + gv * (1.0 - s))).astype(BF16)
        du_ref[...] = (dv * gv * s).astype(BF16)

    return pl.pallas_call(
        body, name=name,
        out_shape=(jax.ShapeDtypeStruct((t, f), BF16), jax.ShapeDtypeStruct((t, f), BF16)), grid=(t // ROW_TILE,),
        in_specs=[_row_spec(f)] * 3, out_specs=(_row_spec(f), _row_spec(f)), compiler_params=_params(("parallel",)),
    )(gg, uu, dact)


MERGE_COLS = 256


def _gate_specs():
    nb = D_MODEL // MERGE_COLS
    base = OFF_GL // MERGE_COLS
    return [pl.BlockSpec((ROW_TILE, MERGE_COLS), functools.partial(lambda i, j, kk: (i, base + nb * kk + j), kk=kk))
            for kk in range(4)]


def _merge_fwd(proj, ys, name):
    t = proj.shape[0]
    yspec = pl.BlockSpec((ROW_TILE, MERGE_COLS), lambda i, j: (i, j))

    def body(g0, g1, g2, g3, y0, y1, y2, y3, o_ref):
        acc = _sig(g0[...]) * y0[...]
        acc += _sig(g1[...]) * y1[...]
        acc += _sig(g2[...]) * y2[...]
        acc += _sig(g3[...]) * y3[...]
        o_ref[...] = acc.astype(BF16)

    return pl.pallas_call(
        body, name=name, out_shape=jax.ShapeDtypeStruct((t, D_MODEL), BF16),
        grid=(t // ROW_TILE, D_MODEL // MERGE_COLS), in_specs=_gate_specs() + [yspec] * 4, out_specs=yspec,
        compiler_params=_params(("parallel", "parallel")),
    )(proj, proj, proj, proj, *ys)


def _merge_bwd(proj, ys, dmerged, name):
    t = proj.shape[0]
    yspec = pl.BlockSpec((ROW_TILE, MERGE_COLS), lambda i, j: (i, j))

    def body(g0, g1, g2, g3, y0, y1, y2, y3, dm_ref, *outs):
        dm = dm_ref[...]
        for gr, yr, dy_ref, dg_ref in zip((g0, g1, g2, g3), (y0, y1, y2, y3), outs[:4], outs[4:]):
            s = _sig(gr[...])
            dy_ref[...] = (dm * s).astype(BF16)
            dg_ref[...] = (dm * yr[...] * s * (1.0 - s)).astype(BF16)

    shp = jax.ShapeDtypeStruct((t, D_MODEL), BF16)
    outs = pl.pallas_call(
        body, name=name, out_shape=(shp,) * 8, grid=(t // ROW_TILE, D_MODEL // MERGE_COLS),
        in_specs=_gate_specs() + [yspec] * 5, out_specs=(yspec,) * 8, compiler_params=_params(("parallel", "parallel")),
    )(proj, proj, proj, proj, *ys, dmerged)
    return outs[:4], outs[4:]


def _ln_silu_fwd(cd, g, b, name):
    t, c = cd.shape

    def body(x_ref, g_ref, b_ref, o_ref):
        xv = x_ref[...]
        mu = jnp.mean(xv, axis=1, keepdims=True)
        xc = xv - mu
        rs = lax.rsqrt(jnp.mean(xc * xc, axis=1, keepdims=True) + EPS)
        z = xc * rs * g_ref[...] + b_ref[...]
        o_ref[...] = (z * _sig(z)).astype(BF16)

    return pl.pallas_call(
        body, name=name, out_shape=jax.ShapeDtypeStruct((t, c), BF16), grid=(t // ROW_TILE,),
        in_specs=[_row_spec(c), _vec_spec(c), _vec_spec(c)], out_specs=_row_spec(c),
        compiler_params=_params(("parallel",)),
    )(cd, g, b)


def _ln_silu_bwd(cd, g, b, dy, name):
    t, c = cd.shape

    def body(x_ref, g_ref, b_ref, dy_ref, dx_ref, dg_ref, db_ref):
        @pl.when(pl.program_id(0) == 0)
        def _():
            dg_ref[...] = jnp.zeros_like(dg_ref)
            db_ref[...] = jnp.zeros_like(db_ref)

        xv = x_ref[...]
        gv = g_ref[...]
        mu = jnp.mean(xv, axis=1, keepdims=True)
        xc = xv - mu
        rs = lax.rsqrt(jnp.mean(xc * xc, axis=1, keepdims=True) + EPS)
        xh = xc * rs
        z = xh * gv + b_ref[...]
        s = _sig(z)
        dz = dy_ref[...] * s * (1.0 + z * (1.0 - s))
        dg_ref[...] += jnp.sum(dz * xh, axis=0, keepdims=True)
        db_ref[...] += jnp.sum(dz, axis=0, keepdims=True)
        dxh = dz * gv
        dx_ref[...] = rs * (dxh - jnp.mean(dxh, axis=1, keepdims=True) - xh * jnp.mean(dxh * xh, axis=1, keepdims=True))

    return pl.pallas_call(
        body, name=name,
        out_shape=(jax.ShapeDtypeStruct((t, c), F32), jax.ShapeDtypeStruct((1, c), F32),
                   jax.ShapeDtypeStruct((1, c), F32)),
        grid=(t // ROW_TILE,), in_specs=[_row_spec(c), _vec_spec(c), _vec_spec(c), _row_spec(c)],
        out_specs=(_row_spec(c), _vec_spec(c), _vec_spec(c)), compiler_params=_params(("arbitrary",)),
    )(cd, g, b, dy)


def _shift_dn(x, k):
    if k == 0:
        return x
    row = lax.broadcasted_iota(jnp.int32, x.shape, 0)
    return jnp.where(row >= k, pltpu.roll(x, k, 0), 0.0)


def _shift_up(x, k):
    if k == 0:
        return x
    t = x.shape[0]
    row = lax.broadcasted_iota(jnp.int32, x.shape, 0)
    return jnp.where(row < t - k, pltpu.roll(x, t - k, 0), 0.0)


def _conv_fwd(x, w_ref, taps):
    acc = w_ref[pl.ds(taps - 1, 1), :] * x
    for k in range(taps - 1):
        acc += w_ref[pl.ds(k, 1), :] * _shift_dn(x, taps - 1 - k)
    return acc


def _conv_bwd(x, dy, w_ref, dw_ref, taps):
    dx = w_ref[pl.ds(taps - 1, 1), :] * dy
    dw_ref[pl.ds(taps - 1, 1), :] = jnp.sum(dy * x, axis=0, keepdims=True)
    for k in range(taps - 1):
        s = taps - 1 - k
        dx += w_ref[pl.ds(k, 1), :] * _shift_up(dy, s)
        dw_ref[pl.ds(k, 1), :] = jnp.sum(dy * _shift_dn(x, s), axis=0, keepdims=True)
    return dx


def _scan_fwd(a, u):
    t = a.shape[0]
    k = 1
    while k < t:
        u = u + a * _shift_dn(u, k)
        if 2 * k < t:
            a = a * _shift_dn(a, k)
        k *= 2
    return u


def _scan_rev(a, u):
    t = a.shape[0]
    k = 1
    while k < t:
        u = u + a * _shift_up(u, k)
        if 2 * k < t:
            a = a * _shift_up(a, k)
        k *= 2
    return u


def _one_minus_exp(y):
    return jnp.where(y > -1e-3, -(y + 0.5 * y * y + (1.0 / 6.0) * y * y * y), 1.0 - jnp.exp(y))


GELU_C = math.sqrt(2.0 / math.pi)


def _gelu(x):
    th = jnp.tanh(GELU_C * (x + 0.044715 * x * x * x))
    return 0.5 * x * (1.0 + th), th


def _softplus(x):
    return jnp.maximum(x, 0.0) + jnp.log(1.0 + jnp.exp(-jnp.abs(x)))


def _chunk_spec(t, blk0):
    return pl.BlockSpec((t, LANE), functools.partial(lambda c, b: (0, b + c), b=blk0))


def _tap_spec(taps):
    return pl.BlockSpec((taps, LANE), lambda c: (0, c))


def _cvec_spec():
    return pl.BlockSpec((1, LANE), lambda c: (0, c))


def _cmat_spec():
    return pl.BlockSpec((1, LANE, LANE), lambda c: (c, 0, 0))


def _lru_forward(ax, wA_ref, bA_ref, wx_ref, bx_ref, wa_ref, ba_ref, lam_ref):
    ca = _conv_fwd(ax, wA_ref, CONV_A) + bA_ref[...]
    gi = _sig(_dot(ca, wx_ref[0], NN) + bx_ref[...])
    gr = _sig(_dot(ca, wa_ref[0], NN) + ba_ref[...])
    sp = _softplus(-lam_ref[...])
    la = -LRU_C * gr * sp
    a = jnp.exp(la)
    mult = jnp.sqrt(_one_minus_exp(2.0 * la))
    h = _scan_fwd(a, ca * gi * mult)
    return ca, gi, gr, sp, a, mult, h


def _a_fwd(proj, wA, bA, wx, bx, wa, ba, lam, name):
    t = proj.shape[0]

    def body(ax_ref, ag_ref, wA_ref, bA_ref, wx_ref, bx_ref, wa_ref, ba_ref, lam_ref, o_ref):
        h = _lru_forward(ax_ref[...], wA_ref, bA_ref, wx_ref, bx_ref, wa_ref, ba_ref, lam_ref)[-1]
        o_ref[...] = (h * _gelu(ag_ref[...])[0]).astype(BF16)

    return pl.pallas_call(
        body, name=name, out_shape=jax.ShapeDtypeStruct((t, BW), BF16), grid=(BW // LANE,),
        in_specs=[_chunk_spec(t, C_AX), _chunk_spec(t, C_AG), _tap_spec(CONV_A), _cvec_spec(), _cmat_spec(),
                  _cvec_spec(), _cmat_spec(), _cvec_spec(), _cvec_spec()],
        out_specs=_chunk_spec(t, 0), compiler_params=_params(("parallel",)),
    )(proj, proj, wA, bA, wx, bx, wa, ba, lam)


def _a_bwd(proj, dya, wA, bA, wx, bx, wa, ba, lam, name):
    t = proj.shape[0]

    def body(ax_ref, ag_ref, dy_ref, wA_ref, bA_ref, wx_ref, bx_ref, wa_ref, ba_ref, lam_ref,
             dax_ref, dag_ref, dwA_ref, dbA_ref, dwx_ref, dbx_ref, dwa_ref, dba_ref, dlam_ref):
        ax = ax_ref[...]
        ag = ag_ref[...]
        dy = dy_ref[...]
        ca, gi, gr, sp, a, mult, h = _lru_forward(ax, wA_ref, bA_ref, wx_ref, bx_ref, wa_ref, ba_ref, lam_ref)
        gel, th = _gelu(ag)
        dgel = 0.5 * (1.0 + th) + 0.5 * ag * (1.0 - th * th) * GELU_C * (1.0 + 3.0 * 0.044715 * ag * ag)
        dag_ref[...] = (dy * h * dgel).astype(BF16)
        s = _scan_rev(_shift_up(a, 1), dy * gel)
        da = s * _shift_dn(h, 1)
        dca = s * gi * mult
        dgi = s * ca * mult
        dmult = s * ca * gi
        dla = da * a - dmult * a * a / mult
        dgr = dla * (-LRU_C * sp)
        dsp = jnp.sum(dla * (-LRU_C * gr), axis=0, keepdims=True)
        dlam_ref[...] = -_sig(-lam_ref[...]) * dsp
        dzi = dgi * gi * (1.0 - gi)
        dzr = dgr * gr * (1.0 - gr)
        dbx_ref[...] = jnp.sum(dzi, axis=0, keepdims=True)
        dba_ref[...] = jnp.sum(dzr, axis=0, keepdims=True)
        dwx_ref[0] = _dot(ca, dzi, TN)
        dwa_ref[0] = _dot(ca, dzr, TN)
        dca += _dot(dzi, wx_ref[0], NT) + _dot(dzr, wa_ref[0], NT)
        dbA_ref[...] = jnp.sum(dca, axis=0, keepdims=True)
        dax_ref[...] = _conv_bwd(ax, dca, wA_ref, dwA_ref, CONV_A).astype(BF16)

    big = jax.ShapeDtypeStruct((t, BW), BF16)
    vec = jax.ShapeDtypeStruct((1, BW), F32)
    mat = jax.ShapeDtypeStruct((BW // LANE, LANE, LANE), F32)
    return pl.pallas_call(
        body, name=name,
        out_shape=(big, big, jax.ShapeDtypeStruct((CONV_A, BW), F32), vec, mat, vec, mat, vec, vec),
        grid=(BW // LANE,),
        in_specs=[_chunk_spec(t, C_AX), _chunk_spec(t, C_AG), _chunk_spec(t, 0), _tap_spec(CONV_A), _cvec_spec(),
                  _cmat_spec(), _cvec_spec(), _cmat_spec(), _cvec_spec(), _cvec_spec()],
        out_specs=(_chunk_spec(t, 0), _chunk_spec(t, 0), _tap_spec(CONV_A), _cvec_spec(), _cmat_spec(), _cvec_spec(),
                   _cmat_spec(), _cvec_spec(), _cvec_spec()),
        compiler_params=_params(("parallel",)),
    )(proj, proj, dya, wA, bA, wx, bx, wa, ba, lam)


def _b_fwd(proj, wB, name):
    t = proj.shape[0]

    def body(bv_ref, bc_ref, bb_ref, w_ref, o_ref):
        o_ref[...] = (bb_ref[...] * _conv_fwd(bc_ref[...] * bv_ref[...], w_ref, CONV_B)).astype(BF16)

    return pl.pallas_call(
        body, name=name, out_shape=jax.ShapeDtypeStruct((t, BW), BF16), grid=(BW // LANE,),
        in_specs=[_chunk_spec(t, C_BV), _chunk_spec(t, C_BC), _chunk_spec(t, C_BB), _tap_spec(CONV_B)],
        out_specs=_chunk_spec(t, 0), compiler_params=_params(("parallel",)),
    )(proj, proj, proj, wB)


def _b_bwd(proj, dyb, wB, name):
    t = proj.shape[0]

    def body(bv_ref, bc_ref, bb_ref, dy_ref, w_ref, dbv_ref, dbc_ref, dbb_ref, dw_ref):
        bv = bv_ref[...]
        bc = bc_ref[...]
        dy = dy_ref[...]
        p = bc * bv
        dbb_ref[...] = (dy * _conv_fwd(p, w_ref, CONV_B)).astype(BF16)
        dp = _conv_bwd(p, dy * bb_ref[...], w_ref, dw_ref, CONV_B)
        dbc_ref[...] = (dp * bv).astype(BF16)
        dbv_ref[...] = (dp * bc).astype(BF16)

    big = jax.ShapeDtypeStruct((t, BW), BF16)
    return pl.pallas_call(
        body, name=name, out_shape=(big, big, big, jax.ShapeDtypeStruct((CONV_B, BW), F32)), grid=(BW // LANE,),
        in_specs=[_chunk_spec(t, C_BV), _chunk_spec(t, C_BC), _chunk_spec(t, C_BB), _chunk_spec(t, 0),
                  _tap_spec(CONV_B)],
        out_specs=(_chunk_spec(t, 0),) * 3 + (_tap_spec(CONV_B),), compiler_params=_params(("parallel",)),
    )(proj, proj, proj, dyb, wB)


def _d_conv_fwd(proj, wD, bD, name):
    t = proj.shape[0]

    def body(d1_ref, d2_ref, w_ref, b_ref, o_ref):
        o_ref[...] = _conv_fwd(d1_ref[...] * _sig(d2_ref[...]), w_ref, CONV_D) + b_ref[...]

    return pl.pallas_call(
        body, name=name, out_shape=jax.ShapeDtypeStruct((t, BW), F32), grid=(BW // LANE,),
        in_specs=[_chunk_spec(t, C_D1), _chunk_spec(t, C_D2), _tap_spec(CONV_D), _cvec_spec()],
        out_specs=_chunk_spec(t, 0), compiler_params=_params(("parallel",)),
    )(proj, proj, wD, bD)


def _d_conv_bwd(proj, dcd, wD, name):
    t = proj.shape[0]

    def body(d1_ref, d2_ref, dy_ref, w_ref, dd1_ref, dd2_ref, dw_ref, db_ref):
        d1 = d1_ref[...]
        s = _sig(d2_ref[...])
        dy = dy_ref[...]
        db_ref[...] = jnp.sum(dy, axis=0, keepdims=True)
        dd = _conv_bwd(d1 * s, dy, w_ref, dw_ref, CONV_D)
        dd1_ref[...] = (dd * s).astype(BF16)
        dd2_ref[...] = (dd * d1 * s * (1.0 - s)).astype(BF16)

    big = jax.ShapeDtypeStruct((t, BW), BF16)
    return pl.pallas_call(
        body, name=name,
        out_shape=(big, big, jax.ShapeDtypeStruct((CONV_D, BW), F32), jax.ShapeDtypeStruct((1, BW), F32)),
        grid=(BW // LANE,),
        in_specs=[_chunk_spec(t, C_D1), _chunk_spec(t, C_D2), _chunk_spec(t, 0), _tap_spec(CONV_D)],
        out_specs=(_chunk_spec(t, 0), _chunk_spec(t, 0), _tap_spec(CONV_D), _cvec_spec()),
        compiler_params=_params(("parallel",)),
    )(proj, proj, dcd, wD)


SCALE = HEAD_DIM ** -0.5
GROUP = N_Q // N_KV


def _attn_probs(q_ref, k_ref, ss_ref, h, n):
    qi = lax.broadcasted_iota(jnp.int32, (BLK, BLK), 0)
    ki = lax.broadcasted_iota(jnp.int32, (BLK, BLK), 1)
    dist = (qi - ki).astype(F32)
    sink = ss_ref[0, h]
    slope = ss_ref[1, h]
    s0 = pl.multiple_of(n * BLK, BLK)
    sp = pl.multiple_of(jnp.maximum(n - 1, 0) * BLK, BLK)
    q = q_ref[0, pl.ds(s0, BLK), :]
    kc = k_ref[0, pl.ds(s0, BLK), :]
    kp = k_ref[0, pl.ds(sp, BLK), :]
    sc = jnp.where(ki <= qi, _dot(q, kc, NT) * SCALE - slope * dist, NEG_INF)
    first = jnp.where(n >= 1, 0, BLK)
    sv = jnp.where(ki > qi + first, _dot(q, kp, NT) * SCALE - slope * (dist + BLK), NEG_INF)
    m = jnp.maximum(jnp.maximum(jnp.max(sc, axis=1, keepdims=True), jnp.max(sv, axis=1, keepdims=True)), sink)
    pc = jnp.exp(sc - m)
    pp = jnp.exp(sv - m)
    ps = jnp.exp(sink - m)
    z = jnp.sum(pc, axis=1, keepdims=True) + jnp.sum(pp, axis=1, keepdims=True) + ps
    return s0, sp, q, kc, kp, pc, pp, ps, z


def _attn_specs(t):
    qs = pl.BlockSpec((1, t, HEAD_DIM), lambda h: (h, 0, 0))
    ks = pl.BlockSpec((1, t, HEAD_DIM), lambda h: (h // GROUP, 0, 0))
    ss = pl.BlockSpec(memory_space=pltpu.SMEM)
    return qs, ks, ss


def _attn_fwd(q, k, v, ss, name):
    t = q.shape[1]
    qs, ks, sspec = _attn_specs(t)

    def body(q_ref, k_ref, v_ref, ss_ref, o_ref):
        h = pl.program_id(0)

        def blk(n, carry):
            s0, sp, _, _, _, pc, pp, _, z = _attn_probs(q_ref, k_ref, ss_ref, h, n)
            o = _dot(pc, v_ref[0, pl.ds(s0, BLK), :], NN) + _dot(pp, v_ref[0, pl.ds(sp, BLK), :], NN)
            o_ref[0, pl.ds(s0, BLK), :] = (o / z).astype(BF16)
            return carry

        lax.fori_loop(0, t // BLK, blk, 0)

    return pl.pallas_call(
        body, name=name, out_shape=jax.ShapeDtypeStruct((N_Q, t, HEAD_DIM), BF16), grid=(N_Q,),
        in_specs=[qs, ks, ks, sspec], out_specs=qs, compiler_params=_params(("parallel",)),
    )(q, k, v, ss)


def _attn_bwd(q, k, v, do, ss, name):
    t = q.shape[1]
    qs, ks, sspec = _attn_specs(t)

    def body(q_ref, k_ref, v_ref, do_ref, ss_ref, dq_ref, dk_ref, dv_ref, ds_ref):
        h = pl.program_id(0)

        @pl.when(h % GROUP == 0)
        def _():
            dk_ref[...] = jnp.zeros_like(dk_ref)
            dv_ref[...] = jnp.zeros_like(dv_ref)

        def blk(n, dsink):
            s0, sp, q, kc, kp, pc, pp, ps, z = _attn_probs(q_ref, k_ref, ss_ref, h, n)
            rz = 1.0 / z
            pc = pc * rz
            pp = pp * rz
            do_b = do_ref[0, pl.ds(s0, BLK), :]
            dpc = _dot(do_b, v_ref[0, pl.ds(s0, BLK), :], NT)
            dpp = _dot(do_b, v_ref[0, pl.ds(sp, BLK), :], NT)
            delta = jnp.sum(pc * dpc, axis=1, keepdims=True) + jnp.sum(pp * dpp, axis=1, keepdims=True)
            dsc = pc * (dpc - delta)
            dsp = pp * (dpp - delta)
            dq_ref[0, pl.ds(s0, BLK), :] = ((_dot(dsc, kc, NN) + _dot(dsp, kp, NN)) * SCALE).astype(BF16)
            dk_ref[0, pl.ds(s0, BLK), :] += _dot(dsc, q, TN) * SCALE
            dk_ref[0, pl.ds(sp, BLK), :] += _dot(dsp, q, TN) * SCALE
            dv_ref[0, pl.ds(s0, BLK), :] += _dot(pc, do_b, TN)
            dv_ref[0, pl.ds(sp, BLK), :] += _dot(pp, do_b, TN)
            return dsink - ps * rz * delta

        dsink = lax.fori_loop(0, t // BLK, blk, jnp.zeros((BLK, 1), F32))
        ds_ref[...] = jnp.full(ds_ref.shape, jnp.sum(dsink), F32)

    kv = jax.ShapeDtypeStruct((N_KV, t, HEAD_DIM), F32)
    return pl.pallas_call(
        body, name=name,
        out_shape=(jax.ShapeDtypeStruct((N_Q, t, HEAD_DIM), BF16), kv, kv, jax.ShapeDtypeStruct((N_Q, 8, LANE), F32)),
        grid=(N_Q,), in_specs=[qs, ks, ks, qs, sspec],
        out_specs=(qs, ks, ks, pl.BlockSpec((1, 8, LANE), lambda h: (h, 0, 0))),
        compiler_params=_params(("arbitrary",)),
    )(q, k, v, do, ss)


def _heads(x2d, n):
    t = x2d.shape[0]
    return x2d.reshape(t, n, HEAD_DIM).transpose(1, 0, 2)


def _unheads(x3d):
    n, t, _ = x3d.shape
    return x3d.transpose(1, 0, 2).reshape(t, n * HEAD_DIM)


SMALL_ELEMS = 256 * 1024
TILE_ELEMS = 640 * 1024


def _row_tile(r, c):
    if r * c <= SMALL_ELEMS:
        return r
    return _pick(r, [t for t in (512, 256, 128, 64, 32, 16, 8) if t * c <= TILE_ELEMS])


def _adamw(w, g, m, v, name):
    r, c = w.shape
    tr = _row_tile(r, c)
    spec = pl.BlockSpec((tr, c), lambda i: (i, 0))

    def body(w_ref, g_ref, m_ref, v_ref, d_ref, nm_ref, nv_ref):
        gv = g_ref[...]
        nm = ADAM_B1 * m_ref[...] + (1.0 - ADAM_B1) * gv
        nv = ADAM_B2 * v_ref[...] + (1.0 - ADAM_B2) * (gv * gv)
        m_hat = nm / (1.0 - ADAM_B1 ** ADAM_STEP)
        v_hat = nv / (1.0 - ADAM_B2 ** ADAM_STEP)
        d_ref[...] = -ADAM_LR * (m_hat / (jnp.sqrt(v_hat) + ADAM_EPS) + ADAM_WD * w_ref[...])
        nm_ref[...] = nm
        nv_ref[...] = nv

    shp = jax.ShapeDtypeStruct((r, c), F32)
    return pl.pallas_call(
        body, name=name, out_shape=(shp, shp, shp), grid=(r // tr,), in_specs=[spec] * 4, out_specs=(spec,) * 3,
        compiler_params=_params(("parallel",)),
    )(w, g, m, v)


def _sum_leading(x, name):
    n, r, c = x.shape
    tr = _row_tile(r, c)

    def body(x_ref, o_ref):
        acc = x_ref[0]
        for i in range(1, n):
            acc = acc + x_ref[i]
        o_ref[...] = acc

    return pl.pallas_call(
        body, name=name, out_shape=jax.ShapeDtypeStruct((r, c), F32), grid=(r // tr,),
        in_specs=[pl.BlockSpec((n, tr, c), lambda i: (0, i, 0))], out_specs=pl.BlockSpec((tr, c), lambda i: (i, 0)),
        compiler_params=_params(("parallel",)),
    )(x)


def _sum_own_plus(p, sel, recv, name, out_dtype):
    _, r, c = p.shape
    n = recv.shape[0]
    tr = _pick(r, (256, 128, 16))

    def body(sel_ref, p_ref, r_ref, o_ref):
        acc = p_ref[0].astype(F32)
        for i in range(n):
            acc = acc + r_ref[i].astype(F32)
        o_ref[...] = acc.astype(out_dtype)

    grid_spec = pltpu.PrefetchScalarGridSpec(
        num_scalar_prefetch=1, grid=(r // tr,),
        in_specs=[pl.BlockSpec((1, tr, c), lambda i, s: (s[0], i, 0)), pl.BlockSpec((n, tr, c), lambda i, s: (0, i, 0))],
        out_specs=pl.BlockSpec((tr, c), lambda i, s: (i, 0)))
    return pl.pallas_call(
        body, name=name, out_shape=jax.ShapeDtypeStruct((r, c), out_dtype), grid_spec=grid_spec,
        compiler_params=_params(("parallel",)),
    )(sel, p, recv)


def _coords():
    return lax.axis_index("x"), lax.axis_index("y"), lax.axis_index("c")


def _allgather8(x2, name, space):
    _, m, n = x2.shape

    def body(x_ref, out_ref, send_sems, recv_sems, local_sem):
        x, y, c = _coords()
        me, sibling = (x, y, c), (x, y, 1 - c)
        chips = [(1 - x, y), (x, 1 - y), (1 - x, 1 - y)]
        mine_src = x_ref.at[c]

        def rows(px, py, pc):
            return out_ref.at[4 * px + 2 * py + pc]

        def copy(k, block, to, src=None):
            return pltpu.make_async_remote_copy(
                src_ref=rows(*block) if src is None else src, dst_ref=rows(*block),
                send_sem=send_sems.at[k], recv_sem=recv_sems.at[k], device_id=to, device_id_type=MESH)

        mine = pltpu.make_async_copy(mine_src, rows(*me), local_sem)
        mine.start()
        first = [copy(0, me, sibling, src=mine_src)]
        first += [copy(1 + j, me, (*chip, c), src=mine_src) for j, chip in enumerate(chips)]
        for cp in first:
            cp.start()
        passed = [copy(4 + j, (*chip, c), sibling) for j, chip in enumerate(chips)]
        for j, chip in enumerate(chips):
            copy(1 + j, (*chip, c), me).wait_recv()
            passed[j].start()
        copy(0, sibling, me).wait_recv()
        for j, chip in enumerate(chips):
            copy(4 + j, (*chip, 1 - c), me).wait_recv()
        for cp in first + passed:
            cp.wait_send()
        mine.wait()

    return pl.pallas_call(
        body, name=name, out_shape=jax.ShapeDtypeStruct((8, m, n), x2.dtype),
        in_specs=[pl.BlockSpec(memory_space=space)], out_specs=pl.BlockSpec(memory_space=space),
        scratch_shapes=[pltpu.SemaphoreType.DMA((7,)), pltpu.SemaphoreType.DMA((7,)), pltpu.SemaphoreType.DMA],
        compiler_params=pltpu.CompilerParams(vmem_limit_bytes=VMEM_LIMIT),
    )(x2)


def _swap_sibling_half(p, name):
    def body(p_ref, out_ref, send_sem, recv_sem):
        x, y, c = _coords()
        cp = pltpu.make_async_remote_copy(
            src_ref=p_ref.at[1 - c], dst_ref=out_ref, send_sem=send_sem, recv_sem=recv_sem,
            device_id=(x, y, 1 - c), device_id_type=MESH)
        cp.start()
        cp.wait()

    return pl.pallas_call(
        body, name=name, out_shape=jax.ShapeDtypeStruct(p.shape[1:], p.dtype),
        in_specs=[pl.BlockSpec(memory_space=pl.ANY)], out_specs=pl.BlockSpec(memory_space=pl.ANY),
        scratch_shapes=[pltpu.SemaphoreType.DMA, pltpu.SemaphoreType.DMA],
    )(p)


N_REG = len(ROW_REGIONS) + 1


def _chip_window(ref, lead, r, j):
    view = ref if lead is None else ref.at[lead]
    if r < len(ROW_REGIONS):
        off, rows = ROW_REGIONS[r]
        return view.at[pl.ds(pl.multiple_of(off + j * rows, 16), rows), :]
    return view.at[pl.ds(R_OUT, OUT_ROWS), pl.ds(pl.multiple_of(j * OUT_COLS, LANE), OUT_COLS)]


HBM_SPEC = pl.BlockSpec(memory_space=pltpu.HBM)
SEM_SPEC = pl.BlockSpec(memory_space=pltpu.SEMAPHORE)


def _half_window(ref, r, j, h):
    if r < len(ROW_REGIONS):
        off, rows = ROW_REGIONS[r]
        return ref.at[pl.ds(pl.multiple_of(off + j * rows + h * (rows // 2), 16), rows // 2), :]
    half = OUT_ROWS // 2
    return ref.at[pl.ds(pl.multiple_of(R_OUT + h * half, 16), half),
                  pl.ds(pl.multiple_of(j * OUT_COLS, LANE), OUT_COLS)]


def _other_chips():
    x, y, _ = _coords()
    return [(1 - x, y), (x, 1 - y), (1 - x, 1 - y)]


def _ici_copies(srcs, arena_ref, send_sems, recv_sems):
    x, y, c = _coords()
    sends, arrivals = [], []
    for k, (cx, cy) in enumerate(_other_chips()):
        for r in range(N_REG):
            def remote(src, j):
                return pltpu.make_async_remote_copy(
                    src_ref=src, dst_ref=_half_window(arena_ref, r, j, c), send_sem=send_sems.at[3 * r + k],
                    recv_sem=recv_sems.at[3 * r + k], device_id=(cx, cy, c), device_id_type=MESH)
            rows = srcs[r].shape[0] // 2
            sends.append(remote(srcs[r].at[pl.ds(pl.multiple_of(c * rows, 16), rows), :], 2 * x + y))
            arrivals.append(remote(_half_window(arena_ref, r, 2 * cx + cy, c), 2 * cx + cy))
    return sends, arrivals


def _sibling_copies(srcs, arena_ref, send_sems, recv_sems):
    x, y, c = _coords()
    sends, arrivals = [], []

    def remote(win, r, k, src=None):
        return pltpu.make_async_remote_copy(
            src_ref=win if src is None else src, dst_ref=win, send_sem=send_sems.at[r, k],
            recv_sem=recv_sems.at[r, k], device_id=(x, y, 1 - c), device_id_type=MESH)

    for k, (cx, cy) in enumerate(_other_chips()):
        for r in range(N_REG):
            sends.append(remote(_half_window(arena_ref, r, 2 * cx + cy, c), r, k))
            arrivals.append(remote(_half_window(arena_ref, r, 2 * cx + cy, 1 - c), r, k))
    for r in range(N_REG):
        own = _chip_window(arena_ref, None, r, 2 * x + y)
        sends.append(remote(own, r, 3, src=srcs[r]))
        arrivals.append(remote(own, r, 3))
    return sends, arrivals


ICI_SEMS = pltpu.SemaphoreType.DMA((3 * N_REG,))
SIBLING_SEMS = pltpu.SemaphoreType.DMA((N_REG, 4))
ARENA_SHAPE = (ARENA_ROWS, ARENA_W)


def _gather_layer(shards, name):
    def body(*refs):
        srcs, arena_ref = refs[:N_REG], refs[N_REG]
        ici_send, ici_recv, sib_send, sib_recv = refs[N_REG + 1:]
        sends, arrivals = _ici_copies(srcs, arena_ref, ici_send, ici_recv)
        passes, landings = _sibling_copies(srcs, arena_ref, sib_send, sib_recv)
        for cp in sends + passes[len(arrivals):]:
            cp.start()
        for arrival, onward in zip(arrivals, passes):
            arrival.wait_recv()
            onward.start()
        for cp in landings:
            cp.wait_recv()
        for cp in sends + passes:
            cp.wait_send()

    return pl.pallas_call(
        body, name=name, out_shape=jax.ShapeDtypeStruct(ARENA_SHAPE, BF16),
        in_specs=[pl.BlockSpec(memory_space=pl.ANY)] * N_REG, out_specs=pl.BlockSpec(memory_space=pl.ANY),
        scratch_shapes=[ICI_SEMS, ICI_SEMS, SIBLING_SEMS, SIBLING_SEMS],
    )(*shards)


def _gather_start(shards, name):
    def body(*refs):
        srcs, arena_ref = refs[:N_REG], refs[N_REG]
        send_sems, recv_sems = refs[N_REG + 1], refs[N_REG + 2]
        token = refs[-1]
        for cp in _ici_copies(srcs, arena_ref, send_sems, recv_sems)[0]:
            cp.start()
        token[...] = jnp.zeros_like(token)

    hbm = lambda a: pltpu.with_memory_space_constraint(a, pltpu.HBM)
    outs = pl.pallas_call(
        body, name=name,
        out_shape=(ICI_SEMS, ICI_SEMS, *[pltpu.HBM(s.shape, s.dtype) for s in shards],
                   pltpu.HBM(ARENA_SHAPE, BF16), jax.ShapeDtypeStruct((8, LANE), F32)),
        in_specs=[HBM_SPEC] * (N_REG + 1),
        out_specs=(SEM_SPEC, SEM_SPEC, *[HBM_SPEC] * (N_REG + 1), pl.BlockSpec(memory_space=pltpu.VMEM)),
        input_output_aliases={i: 2 + i for i in range(N_REG + 1)},
        compiler_params=pltpu.CompilerParams(has_side_effects=pltpu.SideEffectType.DATAFLOW_SIDE_EFFECTING),
    )(*[hbm(s) for s in shards], hbm(lax.empty(ARENA_SHAPE, BF16)))
    return outs[0], outs[1], outs[2:2 + N_REG], outs[2 + N_REG], outs[-1]


def _gather_wait(send_sems, recv_sems, shards, arena, after, name):
    def body(*refs):
        srcs, arena_ref = refs[:N_REG], refs[N_REG]
        sends, arrivals = _ici_copies(srcs, arena_ref, refs[N_REG + 1], refs[N_REG + 2])
        for cp in sends:
            cp.wait_send()
        for cp in arrivals:
            cp.wait_recv()

    outs = pl.pallas_call(
        body, name=name,
        out_shape=(*[pltpu.HBM(s.shape, s.dtype) for s in shards], pltpu.HBM(ARENA_SHAPE, BF16)),
        in_specs=[HBM_SPEC] * (N_REG + 1) + [SEM_SPEC, SEM_SPEC, pl.BlockSpec(memory_space=pl.ANY)],
        out_specs=(HBM_SPEC,) * (N_REG + 1), input_output_aliases={i: i for i in range(N_REG + 1)},
        compiler_params=pltpu.CompilerParams(has_side_effects=pltpu.SideEffectType.DATAFLOW_SIDE_EFFECTING),
    )(*shards, arena, send_sems, recv_sems, after)
    return outs[:N_REG], outs[N_REG]


def _gather_finish(shards, arena, name):
    def body(*refs):
        srcs, arena_ref = refs[:N_REG], refs[N_REG + 1]
        sends, arrivals = _sibling_copies(srcs, arena_ref, refs[N_REG + 2], refs[N_REG + 3])
        for cp in sends:
            cp.start()
        for cp in arrivals:
            cp.wait_recv()
        for cp in sends:
            cp.wait_send()

    return pl.pallas_call(
        body, name=name, out_shape=jax.ShapeDtypeStruct(ARENA_SHAPE, BF16),
        in_specs=[pl.BlockSpec(memory_space=pl.ANY)] * (N_REG + 1), out_specs=pl.BlockSpec(memory_space=pl.ANY),
        scratch_shapes=[SIBLING_SEMS, SIBLING_SEMS], input_output_aliases={N_REG: 0},
    )(*shards, arena)


def _scatter_pieces(s, name):
    def body(s_ref, main_ref, outp_ref, send_sems, recv_sems):
        x, y, c = _coords()
        chips = [(1 - x, y), (x, 1 - y), (1 - x, 1 - y)]
        cps = []
        for k, (cx, cy) in enumerate(chips):
            for r in range(N_REG):
                if r < len(ROW_REGIONS):
                    dst = main_ref.at[k, pl.ds(PIECE_OFF[r], ROW_REGIONS[r][1]), :]
                else:
                    dst = outp_ref.at[k]
                cps.append(pltpu.make_async_remote_copy(
                    src_ref=_chip_window(s_ref, None, r, 2 * cx + cy), dst_ref=dst, send_sem=send_sems.at[r, k],
                    recv_sem=recv_sems.at[r, k], device_id=(cx, cy, c), device_id_type=MESH))
        for cp in cps:
            cp.start()
        for cp in cps:
            cp.wait()

    return pl.pallas_call(
        body, name=name,
        out_shape=(jax.ShapeDtypeStruct((3, PIECE_ROWS, ARENA_W), s.dtype),
                   jax.ShapeDtypeStruct((3, OUT_ROWS, OUT_COLS), s.dtype)),
        in_specs=[pl.BlockSpec(memory_space=pl.ANY)], out_specs=(pl.BlockSpec(memory_space=pl.ANY),) * 2,
        scratch_shapes=[pltpu.SemaphoreType.DMA((N_REG, 3)), pltpu.SemaphoreType.DMA((N_REG, 3))],
    )(s)


def _own_piece(s, chip):
    main = jnp.concatenate([lax.dynamic_slice(s, (off + chip * rows, 0), (rows, ARENA_W))
                            for off, rows in ROW_REGIONS])
    return main, lax.dynamic_slice(s, (R_OUT, chip * OUT_COLS), (OUT_ROWS, OUT_COLS))


def _swap_pair(a, b, name):
    def body(a_ref, b_ref, ra_ref, rb_ref, send_sems, recv_sems):
        x, y, c = _coords()
        cps = [pltpu.make_async_remote_copy(
            src_ref=s, dst_ref=d, send_sem=send_sems.at[i], recv_sem=recv_sems.at[i], device_id=(x, y, 1 - c),
            device_id_type=MESH) for i, (s, d) in enumerate(((a_ref, ra_ref), (b_ref, rb_ref)))]
        for cp in cps:
            cp.start()
        for cp in cps:
            cp.wait()

    return pl.pallas_call(
        body, name=name,
        out_shape=(jax.ShapeDtypeStruct(a.shape, a.dtype), jax.ShapeDtypeStruct(b.shape, b.dtype)),
        in_specs=[pl.BlockSpec(memory_space=pl.ANY)] * 2, out_specs=(pl.BlockSpec(memory_space=pl.ANY),) * 2,
        scratch_shapes=[pltpu.SemaphoreType.DMA((2,)), pltpu.SemaphoreType.DMA((2,))],
    )(a, b)


OUT_NAMES = ("w_a_out", "w_b_out", "w_c_out", "w_d_out")


def _arena_shards(w):
    t = lambda a: a.astype(BF16).transpose(0, 2, 1)
    return (w["w_ffn_down"].astype(BF16), t(w["w_ffn_gate"]), t(w["w_ffn_up"]), t(w["w_in"]), w["w_o"].astype(BF16),
            jnp.concatenate([w[n].astype(BF16) for n in OUT_NAMES], axis=1))


def _shard_grads(main, outp):
    t = lambda r: main[:, PIECE_OFF[r]:PIECE_OFF[r] + ROW_REGIONS[r][1]]
    g = dict(w_ffn_down=t(0), w_ffn_gate=t(1).transpose(0, 2, 1), w_ffn_up=t(2).transpose(0, 2, 1),
             w_in=t(3).transpose(0, 2, 1), w_o=t(4))
    for i, n in enumerate(OUT_NAMES):
        g[n] = outp[:, i * BW:(i + 1) * BW]
    return g


def _gather_taps(p, name):
    mine = jnp.concatenate([p[n] for n in CONV_NAMES], axis=1).reshape(DEPTH * N_TAPS, LANE)
    rows = -(-mine.shape[0] // 8) * 8
    mine = jnp.concatenate([mine, jnp.zeros((rows - mine.shape[0], LANE), F32)])
    g = _allgather8(jnp.stack([mine, mine]), name, pltpu.VMEM)[0::2, :DEPTH * N_TAPS]
    full = g.reshape(4, DEPTH, N_TAPS, LANE).transpose(1, 2, 0, 3).reshape(DEPTH, N_TAPS, BW)
    return dict(conv_a_w=full[:, :CONV_A], conv_b_w=full[:, CONV_A:CONV_A + CONV_B], conv_d_w=full[:, CONV_A + CONV_B:])


def _flat_pack(arrs):
    flat = jnp.concatenate([a.reshape(-1).astype(F32) for a in arrs])
    rows = -(-flat.shape[0] // (8 * LANE)) * 8
    return jnp.concatenate([flat, jnp.zeros((rows * LANE - flat.shape[0],), F32)]).reshape(rows, LANE)


def _flat_unpack(packed, shapes):
    flat, out, off = packed.reshape(-1), [], 0
    for s in shapes:
        cnt = int(np.prod(s))
        out.append(flat[off:off + cnt].reshape(s))
        off += cnt
    return out


def _blockdiag_chunks(w):
    w4 = w.reshape(4, 2, 64, 64)
    z = jnp.zeros((4, 2, 64, 2, 64), F32)
    z = z.at[:, 0, :, 0, :].set(w4[:, 0]).at[:, 1, :, 1, :].set(w4[:, 1])
    return z.reshape(4, LANE, LANE)


def _blockdiag_extract(d):
    d5 = d.reshape(4, 2, 64, 2, 64)
    return jnp.stack([d5[:, 0, :, 0, :], d5[:, 1, :, 1, :]], axis=1).reshape(8, 64, 64)


SLOPES = np.asarray([2.0 ** (-8.0 * (i + 1) / N_Q) for i in range(N_Q)], np.float32)


def _layer_consts(p, fw, l):
    row = lambda a: a[l].reshape(1, -1)
    return dict(
        g1=row(p["norm1_g"]), g2=row(p["norm2_g"]), wA=fw["conv_a_w"][l], bA=row(p["conv_a_b"]),
        wx=_blockdiag_chunks(p["lru_wx"][l]), bx=row(p["lru_bx"]), wa=_blockdiag_chunks(p["lru_wa"][l]),
        ba=row(p["lru_ba"]), lam=row(p["lru_lambda"]), wB=fw["conv_b_w"][l],
        ss=jnp.stack([p["sinks"][l], jnp.asarray(SLOPES)]), wD=fw["conv_d_w"][l], bD=row(p["conv_d_b"]),
        lg=row(p["ln_d_g"]), lb=row(p["ln_d_b"]))


def _layer_fwd(x, c, fw, l):
    t = f"l{l}_"
    xn = _rms_fwd(x, c["g1"], t + "rms1")
    wt = lambda off, rows: Win(fw["arena"][l], None, off, rows)
    proj = _mm(xn, wt(R_IN, IN_W), "nt", t + "proj")
    ya = _a_fwd(proj, c["wA"], c["bA"], c["wx"], c["bx"], c["wa"], c["ba"], c["lam"], t + "a_fwd")
    yb = _b_fwd(proj, c["wB"], t + "b_fwd")
    q3 = _heads(proj[:, OFF_Q:OFF_K], N_Q)
    k3 = _heads(proj[:, OFF_K:OFF_V], N_KV)
    v3 = _heads(proj[:, OFF_V:OFF_V + N_KV * HEAD_DIM], N_KV)
    yc = _unheads(_attn_fwd(q3, k3, v3, c["ss"], t + "attn_fwd"))
    cd = _d_conv_fwd(proj, c["wD"], c["bD"], t + "d_conv_fwd")
    yd = _ln_silu_fwd(cd, c["lg"], c["lb"], t + "d_ln_fwd")
    ys = (ya, yb, yc, yd)
    big_y = tuple(_mm(y, wt(R_OUT + i * BW, BW), "nn", t + f"out{i}") for i, y in enumerate(ys))
    merged = _merge_fwd(proj, big_y, t + "merge_fwd")
    hres = _mm(merged, wt(R_O, D_MODEL), "nn", t + "wo", add=x)
    hn = _rms_fwd(hres, c["g2"], t + "rms2")
    gg = _mm(hn, wt(R_GATE, D_FF), "nt", t + "ffn_gate")
    uu = _mm(hn, wt(R_UP, D_FF), "nt", t + "ffn_up")
    act = _swiglu_fwd(gg, uu, t + "swiglu_fwd")
    xout = _mm(act, wt(R_DOWN, D_FF), "nn", t + "ffn_down", add=hres)
    saved = dict(x=x, xn=xn, proj=proj, ys=ys, q3=q3, k3=k3, v3=v3, cd=cd, big_y=big_y, merged=merged, hres=hres,
                 hn=hn, gg=gg, uu=uu, act=act)
    return xout, saved


def _layer_bwd(dxout, s, c, fw, l, ga):
    t = f"l{l}_"
    gs = {}
    wt = lambda off, rows: Win(fw["arena"][l], None, off, rows)
    gt = lambda off, rows: Win(ga, l, off, rows)
    dact = _mm(dxout, wt(R_DOWN, D_FF), "nt", t + "d_act")
    ga = _mm(s["act"], dxout, "tn", t + "dw_down", out=gt(R_DOWN, D_FF))
    dgg, duu = _swiglu_bwd(s["gg"], s["uu"], dact, t + "swiglu_bwd")
    ga = _mm(dgg, s["hn"], "tn", t + "dw_gate", out=gt(R_GATE, D_FF))
    ga = _mm(duu, s["hn"], "tn", t + "dw_up", out=gt(R_UP, D_FF))
    dhn = _mm(dgg, wt(R_GATE, D_FF), "nn", t + "d_hn_g")
    dhn = _mm(duu, wt(R_UP, D_FF), "nn", t + "d_hn_u", add=dhn)
    dhres, gs["norm2_g"] = _rms_bwd(s["hres"], c["g2"], dhn, dxout, t + "rms2_bwd")
    dmerged = _mm(dhres, wt(R_O, D_MODEL), "nt", t + "d_merged")
    ga = _mm(s["merged"], dhres, "tn", t + "dw_o", out=gt(R_O, D_MODEL))
    dbig_y, dgl = _merge_bwd(s["proj"], s["big_y"], dmerged, t + "merge_bwd")
    dys = []
    for i in range(4):
        ga = _mm(s["ys"][i], dbig_y[i], "tn", t + f"dw_out{i}", out=gt(R_OUT + i * BW, BW))
        dys.append(_mm(dbig_y[i], wt(R_OUT + i * BW, BW), "nt", t + f"d_y{i}"))
    proj = s["proj"]
    (dax, dag, gs["conv_a_w"], gs["conv_a_b"], dwx, gs["lru_bx"], dwa, gs["lru_ba"], gs["lru_lambda"]) = _a_bwd(
        proj, dys[0], c["wA"], c["bA"], c["wx"], c["bx"], c["wa"], c["ba"], c["lam"], t + "a_bwd")
    gs["lru_wx"] = _blockdiag_extract(dwx)
    gs["lru_wa"] = _blockdiag_extract(dwa)
    dbv, dbc, dbb, gs["conv_b_w"] = _b_bwd(proj, dys[1], c["wB"], t + "b_bwd")
    dq3, dk3, dv3, dsink = _attn_bwd(s["q3"], s["k3"], s["v3"], _heads(dys[2], N_Q), c["ss"], t + "attn_bwd")
    gs["sinks"] = dsink[:, 0, 0]
    dcd, gs["ln_d_g"], gs["ln_d_b"] = _ln_silu_bwd(s["cd"], c["lg"], c["lb"], dys[3], t + "d_ln_bwd")
    dd1, dd2, gs["conv_d_w"], gs["conv_d_b"] = _d_conv_bwd(proj, dcd, c["wD"], t + "d_conv_bwd")
    dproj = jnp.concatenate(
        [dax, dag, dbv, dbc, dbb, _unheads(dq3), _unheads(dk3).astype(BF16), _unheads(dv3).astype(BF16), dd1, dd2,
         *dgl], axis=1)
    ga = _mm(dproj, s["xn"], "tn", t + "dw_in", out=gt(R_IN, IN_W))
    dxn = _mm(dproj, wt(R_IN, IN_W), "nn", t + "d_xn")
    dx, gs["norm1_g"] = _rms_bwd(s["x"], c["g1"], dxn, dhres, t + "rms1_bwd")
    return dx, ga, gs


def kernel(x, norm1_g, w_in, conv_a_w, conv_a_b, lru_wx, lru_bx, lru_wa, lru_ba, lru_lambda, w_a_out, conv_b_w, w_b_out, sinks, w_c_out, conv_d_w, conv_d_b, ln_d_g, ln_d_b, w_d_out, w_o, norm2_g, w_ffn_gate, w_ffn_up, w_ffn_down, final_g, loss_target, m_norm1_g, m_w_in, m_conv_a_w, m_conv_a_b, m_lru_wx, m_lru_bx, m_lru_wa, m_lru_ba, m_lru_lambda, m_w_a_out, m_conv_b_w, m_w_b_out, m_sinks, m_w_c_out, m_conv_d_w, m_conv_d_b, m_ln_d_g, m_ln_d_b, m_w_d_out, m_w_o, m_norm2_g, m_w_ffn_gate, m_w_ffn_up, m_w_ffn_down, m_final_g, v_norm1_g, v_w_in, v_conv_a_w, v_conv_a_b, v_lru_wx, v_lru_bx, v_lru_wa, v_lru_ba, v_lru_lambda, v_w_a_out, v_conv_b_w, v_w_b_out, v_sinks, v_w_c_out, v_conv_d_w, v_conv_d_b, v_ln_d_g, v_ln_d_b, v_w_d_out, v_w_o, v_norm2_g, v_w_ffn_gate, v_w_ffn_up, v_w_ffn_down, v_final_g):
    given = dict(locals())
    p = {n: given[n] for n in NAMES}
    mom = {n: given["m_" + n] for n in NAMES}
    var = {n: given["v_" + n] for n in NAMES}
    cx, cy, cc = _coords()
    chip = 2 * cx + cy

    shards = _arena_shards(p)
    fw = _gather_taps(p, "gather_taps")
    fw["arena"] = [_gather_layer([s[0] for s in shards], "gather_weights_l0"), None]
    send_sems, recv_sems, shards1, landing, token = _gather_start([s[1] for s in shards], "gather_weights_l1_start")

    h = x[0]
    consts, saved = [], []
    for l in range(DEPTH):
        consts.append(_layer_consts(p, fw, l))
        if l == 0:
            consts[0]["g1"] = consts[0]["g1"] + token[0:1, 0:1]
        else:
            shards1, landing = _gather_wait(send_sems, recv_sems, shards1, landing, h, "gather_weights_l1_wait")
            fw["arena"][1] = _gather_finish(shards1, landing, "gather_weights_l1_finish")
        h, s = _layer_fwd(h, consts[l], fw, l)
        saved.append(s)
    loss_vec, dh, g_final = _loss_head(h, final_g.reshape(1, -1), loss_target[0], "loss_head")
    loss = lax.psum(loss_vec[0, 0], ("x", "y", "c"))

    gss = [None] * DEPTH
    ga = lax.empty((DEPTH, ARENA_ROWS, ARENA_W), BF16)
    for l in reversed(range(DEPTH)):
        dh, ga, gss[l] = _layer_bwd(dh, saved[l], consts[l], fw, l, ga)
    grad_x = dh[None]

    zero = jnp.zeros((1,), jnp.int32)
    from_sibling = _swap_sibling_half(ga[:, None], "grads_swap_sibling")
    chip_sum = _sum_own_plus(ga, cc.reshape(1).astype(jnp.int32), from_sibling, "grads_sum_chip", BF16)
    recv_main, recv_out = _scatter_pieces(chip_sum, "grads_scatter_chips")
    own_main, own_out = _own_piece(chip_sum, chip)
    red_main = _sum_own_plus(own_main[None], zero, recv_main, "grads_sum_all", F32)
    red_out = _sum_own_plus(own_out[None], zero, recv_out, "grads_sum_all_out", F32)
    sib_main, sib_out = _swap_pair(red_main, red_out, "grads_swap_reduced")
    by_layer = lambda mine, theirs: jnp.stack([jnp.where(cc == l, mine, theirs) for l in range(DEPTH)])
    g = _shard_grads(by_layer(red_main, sib_main), by_layer(red_out, sib_out))

    small_full = {n: (g_final.reshape(-1) if n == "final_g" else
                      jnp.stack([gss[l][n].reshape(gss[l][n].shape[-2:] if n.startswith("conv") and n.endswith("_w")
                                                   else p[n].shape[1:]) for l in range(DEPTH)]))
                  for n in SMALL}
    part = _flat_pack([small_full[n] for n in SMALL])
    rows = part.shape[0]
    gathered = _allgather8(jnp.stack([part, part]), "gather_small_grads", pltpu.VMEM)
    small_sum = _flat_unpack(_sum_leading(gathered, "small_grads_sum"), [small_full[n].shape for n in SMALL])
    for n, a in zip(SMALL, small_sum):
        g[n] = lax.dynamic_slice_in_dim(a, chip * LANE, LANE, axis=2) if n in CONV_NAMES else a

    delta, new_m, new_v = {}, {}, {}
    for n in BIG:
        shp = p[n].shape
        two_d = lambda a: a.reshape(-1, shp[-1])
        d, nm, nv = _adamw(two_d(p[n]), two_d(g[n]), two_d(mom[n]), two_d(var[n]), "adamw_" + n)
        delta[n], new_m[n], new_v[n] = d.reshape(shp), nm.reshape(shp), nv.reshape(shp)
    shapes = [p[n].shape for n in SMALL]
    d, nm, nv = _adamw(_flat_pack([p[n] for n in SMALL]), _flat_pack([g[n] for n in SMALL]),
                       _flat_pack([mom[n] for n in SMALL]), _flat_pack([var[n] for n in SMALL]), "adamw_small")
    for n, a, b, cval in zip(SMALL, _flat_unpack(d, shapes), _flat_unpack(nm, shapes), _flat_unpack(nv, shapes)):
        delta[n], new_m[n], new_v[n] = a, b, cval

    return (loss, grad_x, *[g[n] for n in NAMES], *[delta[n] for n in NAMES], *[new_m[n] for n in NAMES],
            *[new_v[n] for n in NAMES])
```

```python
import functools
import math

import numpy as np
import jax
import jax.numpy as jnp
from jax import lax
from jax.experimental import pallas as pl
from jax.experimental.pallas import tpu as pltpu

F32 = jnp.float32
BF16 = jnp.bfloat16
MESH = pl.DeviceIdType.MESH

D_MODEL = 1024
DEPTH = 2
BW = 512
HEAD_DIM = 64
N_Q = 8
N_KV = 2
BLK = 128
D_FF = 2816
IN_W = 8448
EPS = 1e-6
NEG_INF = -1e30
LRU_C = 8.0
CONV_A, CONV_B, CONV_D = 4, 3, 31
LANE = 128
ROW_TILE = 256
VMEM_LIMIT = 56 * 1024 * 1024
MM_VMEM_BUDGET = 36 * 1024 * 1024

C_AX, C_AG, C_BV, C_BC, C_BB = 0, 4, 8, 12, 16
OFF_Q, OFF_K, OFF_V = 2560, 3072, 3200
C_D1, C_D2 = 26, 30
OFF_GL = 4352

ADAM_LR, ADAM_B1, ADAM_B2, ADAM_EPS, ADAM_WD, ADAM_STEP = 0.001, 0.9, 0.999, 1e-08, 0.01, 10

ARENA_W = 1024
R_DOWN, R_GATE, R_UP, R_IN, R_O, R_OUT = 0, 2816, 5632, 8448, 16896, 17920
ARENA_ROWS = 19968
ROW_REGIONS = ((R_DOWN, 704), (R_GATE, 704), (R_UP, 704), (R_IN, 2112), (R_O, 256))
PIECE_OFF = (0, 704, 1408, 2112, 4224)
PIECE_ROWS = 4480
OUT_ROWS, OUT_COLS = 4 * BW, D_MODEL // 4

BIG = ("w_in", "w_a_out", "w_b_out", "w_c_out", "w_d_out", "w_o", "w_ffn_gate", "w_ffn_up", "w_ffn_down")
CONV_NAMES = ("conv_a_w", "conv_b_w", "conv_d_w")
N_TAPS = CONV_A + CONV_B + CONV_D
SMALL = ("norm1_g", "conv_a_w", "conv_a_b", "lru_wx", "lru_bx", "lru_wa", "lru_ba", "lru_lambda", "conv_b_w",
         "sinks", "conv_d_w", "conv_d_b", "ln_d_g", "ln_d_b", "norm2_g", "final_g")
NAMES = ['norm1_g', 'w_in', 'conv_a_w', 'conv_a_b', 'lru_wx', 'lru_bx', 'lru_wa', 'lru_ba', 'lru_lambda', 'w_a_out',
         'conv_b_w', 'w_b_out', 'sinks', 'w_c_out', 'conv_d_w', 'conv_d_b', 'ln_d_g', 'ln_d_b', 'w_d_out', 'w_o',
         'norm2_g', 'w_ffn_gate', 'w_ffn_up', 'w_ffn_down', 'final_g']


def _pick(n, cands, off=0):
    for c in cands:
        if n % c == 0 and off % c == 0:
            return c
    assert off == 0, (n, off)
    return n


class Win:
    def __init__(self, arena, l, off, rows):
        self.arena, self.l, self.off, self.rows = arena, l, off, rows
        self.shape = (rows, arena.shape[-1])


def _params(sem=None):
    return pltpu.CompilerParams(dimension_semantics=sem, vmem_limit_bytes=VMEM_LIMIT)


def _sig(z):
    return 1.0 / (1.0 + jnp.exp(-z))


def _dot(a, b, dims):
    return lax.dot_general(a.astype(BF16), b.astype(BF16), (dims, ((), ())), preferred_element_type=F32)


NN = ((1,), (0,))
NT = ((1,), (1,))
TN = ((0,), (0,))


def _mm(a, b, mode, name, out_dtype=F32, add=None, out=None):
    if mode == "nn":
        (m, k), n = a.shape, b.shape[1]
    elif mode == "nt":
        (m, k), n = a.shape, b.shape[0]
    else:
        (k, m), n = a.shape, b.shape[1]
    b_win = isinstance(b, Win)
    b_off = b.off if b_win else 0
    o_off = out.off if out is not None else 0
    if out is not None:
        out_dtype = out.arena.dtype
    tk = _pick(k, (2816, 2048, 1408, 1024, 768, 512, 256), b_off if mode != "nt" else 0)
    nk = k // tk
    n_off = b_off if mode == "nt" else 0
    a_bytes, b_bytes, o_bytes = a.dtype.itemsize, 2, jnp.dtype(out_dtype).itemsize

    def vmem_bytes(tm_, tn_):
        tile = tm_ * tn_
        return (2 * tk * (tm_ * a_bytes + tn_ * b_bytes) + 2 * tile * o_bytes + (tile * 4 if nk > 1 else 0)
                + (2 * tile * 4 if add is not None else 0) + tile * 4)

    pairs = [(tm_, tn_) for tm_ in (2048, 1024, 768, 512, 256, 128) for tn_ in (1024, 768, 512, 256, 128)
             if m % tm_ == 0 and o_off % tm_ == 0 and n % tn_ == 0 and n_off % tn_ == 0
             and vmem_bytes(tm_, tn_) <= MM_VMEM_BUDGET]
    tm, tn = max(pairs, key=lambda p: (p[0] * p[1], p[0]))
    dims = {"nn": NN, "nt": NT, "tn": TN}[mode]

    def body(*refs):
        a_ref, b_ref = refs[:2]
        c_ref = refs[2] if add is not None else None
        if nk == 1:
            r = _dot(a_ref[...], b_ref[...], dims)
            if add is not None:
                r = r + c_ref[...]
            refs[-1][...] = r.astype(out_dtype)
            return
        o_ref, acc = refs[-2:]
        kk = pl.program_id(2)

        @pl.when(kk == 0)
        def _():
            acc[...] = jnp.zeros_like(acc)

        acc[...] += _dot(a_ref[...], b_ref[...], dims)

        @pl.when(kk == nk - 1)
        def _():
            r = acc[...]
            if add is not None:
                r = r + c_ref[...]
            o_ref[...] = r.astype(out_dtype)

    if mode == "tn":
        a_spec = pl.BlockSpec((tk, tm), lambda i, j, q: (q, i))
    else:
        a_spec = pl.BlockSpec((tm, tk), lambda i, j, q: (i, q))
    if mode == "nt":
        b_blk, b_idx = (tn, tk), (lambda i, j, q: (b_off // tn + j, q))
    else:
        b_blk, b_idx = (tk, tn), (lambda i, j, q: (b_off // tk + q, j))
    if b_win and b.arena.ndim == 3:
        bl = b.l
        b_spec = pl.BlockSpec((None,) + b_blk, lambda i, j, q: (bl,) + b_idx(i, j, q))
    else:
        b_spec = pl.BlockSpec(b_blk, b_idx)
    plain_o = pl.BlockSpec((tm, tn), lambda i, j, q: (i, j))
    in_specs = [a_spec, b_spec] + ([plain_o] if add is not None else [])
    args = (a, b.arena if b_win else b) + ((add,) if add is not None else ())
    aliases = {}
    if out is None:
        o_spec, o_shape = plain_o, jax.ShapeDtypeStruct((m, n), out_dtype)
    else:
        ol = out.l
        o_spec = pl.BlockSpec((None, tm, tn), lambda i, j, q: (ol, o_off // tm + i, j))
        o_shape = jax.ShapeDtypeStruct(out.arena.shape, out_dtype)
        aliases = {len(args): 0}
        in_specs.append(pl.BlockSpec(memory_space=pl.ANY))
        args = args + (out.arena,)
    return pl.pallas_call(
        body, name=name, out_shape=o_shape,
        grid=(m // tm, n // tn, nk), in_specs=in_specs, out_specs=o_spec,
        scratch_shapes=[pltpu.VMEM((tm, tn), F32)] if nk > 1 else [], input_output_aliases=aliases,
        compiler_params=_params(("parallel", "parallel", "arbitrary")),
    )(*args)


def _row_spec(cols, tr=ROW_TILE):
    return pl.BlockSpec((tr, cols), lambda i: (i, 0))


def _vec_spec(cols):
    return pl.BlockSpec((1, cols), lambda i: (0, 0))


def _rms_fwd(x, g, name):
    t, d = x.shape

    def body(x_ref, g_ref, o_ref):
        xv = x_ref[...]
        r = lax.rsqrt(jnp.mean(xv * xv, axis=1, keepdims=True) + EPS)
        o_ref[...] = (xv * r * g_ref[...]).astype(BF16)

    return pl.pallas_call(
        body, name=name, out_shape=jax.ShapeDtypeStruct((t, d), BF16), grid=(t // ROW_TILE,),
        in_specs=[_row_spec(d), _vec_spec(d)], out_specs=_row_spec(d), compiler_params=_params(("parallel",)),
    )(x, g)


def _rms_bwd(x, g, dxn, dres, name):
    t, d = x.shape

    def body(x_ref, g_ref, dy_ref, dr_ref, dx_ref, dg_ref):
        @pl.when(pl.program_id(0) == 0)
        def _():
            dg_ref[...] = jnp.zeros_like(dg_ref)

        xv = x_ref[...]
        dy = dy_ref[...]
        r = lax.rsqrt(jnp.mean(xv * xv, axis=1, keepdims=True) + EPS)
        w = dy * g_ref[...]
        dx_ref[...] = dr_ref[...] + r * w - xv * (r * r * r) * jnp.mean(w * xv, axis=1, keepdims=True)
        dg_ref[...] += jnp.sum(dy * xv * r, axis=0, keepdims=True)

    return pl.pallas_call(
        body, name=name,
        out_shape=(jax.ShapeDtypeStruct((t, d), F32), jax.ShapeDtypeStruct((1, d), F32)), grid=(t // ROW_TILE,),
        in_specs=[_row_spec(d), _vec_spec(d), _row_spec(d), _row_spec(d)], out_specs=(_row_spec(d), _vec_spec(d)),
        compiler_params=_params(("arbitrary",)),
    )(x, g, dxn, dres)


def _loss_head(x, g, tgt, name):
    t, d = x.shape

    def body(x_ref, g_ref, t_ref, loss_ref, dx_ref, dg_ref):
        @pl.when(pl.program_id(0) == 0)
        def _():
            dg_ref[...] = jnp.zeros_like(dg_ref)
            loss_ref[...] = jnp.zeros_like(loss_ref)

        xv = x_ref[...]
        gv = g_ref[...]
        r = lax.rsqrt(jnp.mean(xv * xv, axis=1, keepdims=True) + EPS)
        e = xv * r * gv - t_ref[...]
        loss_ref[...] += jnp.full(loss_ref.shape, (0.5 / d) * jnp.sum(e * e), F32)
        dy = e * (1.0 / d)
        w = dy * gv
        dx_ref[...] = r * w - xv * (r * r * r) * jnp.mean(w * xv, axis=1, keepdims=True)
        dg_ref[...] += jnp.sum(dy * xv * r, axis=0, keepdims=True)

    return pl.pallas_call(
        body, name=name,
        out_shape=(jax.ShapeDtypeStruct((1, LANE), F32), jax.ShapeDtypeStruct((t, d), F32),
                   jax.ShapeDtypeStruct((1, d), F32)),
        grid=(t // ROW_TILE,), in_specs=[_row_spec(d), _vec_spec(d), _row_spec(d)],
        out_specs=(_vec_spec(LANE), _row_spec(d), _vec_spec(d)), compiler_params=_params(("arbitrary",)),
    )(x, g, tgt)


def _swiglu_fwd(gg, uu, name):
    t, f = gg.shape

    def body(g_ref, u_ref, o_ref):
        gv = g_ref[...]
        o_ref[...] = (gv * _sig(gv) * u_ref[...]).astype(BF16)

    return pl.pallas_call(
        body, name=name, out_shape=jax.ShapeDtypeStruct((t, f), BF16), grid=(t // ROW_TILE,),
        in_specs=[_row_spec(f), _row_spec(f)], out_specs=_row_spec(f), compiler_params=_params(("parallel",)),
    )(gg, uu)


def _swiglu_bwd(gg, uu, dact, name):
    t, f = gg.shape

    def body(g_ref, u_ref, d_ref, dg_ref, du_ref):
        gv = g_ref[...]
        dv = d_ref[...]
        s = _sig(gv)
        dg_ref[...] = (dv * u_ref[...] * s * (1.0 + gv * (1.0 - s))).astype(BF16)
        du_ref[...] = (dv * gv * s).astype(BF16)

    return pl.pallas_call(
        body, name=name,
        out_shape=(jax.ShapeDtypeStruct((t, f), BF16), jax.ShapeDtypeStruct((t, f), BF16)), grid=(t // ROW_TILE,),
        in_specs=[_row_spec(f)] * 3, out_specs=(_row_spec(f), _row_spec(f)), compiler_params=_params(("parallel",)),
    )(gg, uu, dact)


MERGE_COLS = 256


def _gate_specs():
    nb = D_MODEL // MERGE_COLS
    base = OFF_GL // MERGE_COLS
    return [pl.BlockSpec((ROW_TILE, MERGE_COLS), functools.partial(lambda i, j, kk: (i, base + nb * kk + j), kk=kk))
            for kk in range(4)]


def _merge_fwd(proj, ys, name):
    t = proj.shape[0]
    yspec = pl.BlockSpec((ROW_TILE, MERGE_COLS), lambda i, j: (i, j))

    def body(g0, g1, g2, g3, y0, y1, y2, y3, o_ref):
        acc = _sig(g0[...]) * y0[...]
        acc += _sig(g1[...]) * y1[...]
        acc += _sig(g2[...]) * y2[...]
        acc += _sig(g3[...]) * y3[...]
        o_ref[...] = acc.astype(BF16)

    return pl.pallas_call(
        body, name=name, out_shape=jax.ShapeDtypeStruct((t, D_MODEL), BF16),
        grid=(t // ROW_TILE, D_MODEL // MERGE_COLS), in_specs=_gate_specs() + [yspec] * 4, out_specs=yspec,
        compiler_params=_params(("parallel", "parallel")),
    )(proj, proj, proj, proj, *ys)


def _merge_bwd(proj, ys, dmerged, name):
    t = proj.shape[0]
    yspec = pl.BlockSpec((ROW_TILE, MERGE_COLS), lambda i, j: (i, j))

    def body(g0, g1, g2, g3, y0, y1, y2, y3, dm_ref, *outs):
        dm = dm_ref[...]
        for gr, yr, dy_ref, dg_ref in zip((g0, g1, g2, g3), (y0, y1, y2, y3), outs[:4], outs[4:]):
            s = _sig(gr[...])
            dy_ref[...] = (dm * s).astype(BF16)
            dg_ref[...] = (dm * yr[...] * s * (1.0 - s)).astype(BF16)

    shp = jax.ShapeDtypeStruct((t, D_MODEL), BF16)
    outs = pl.pallas_call(
        body, name=name, out_shape=(shp,) * 8, grid=(t // ROW_TILE, D_MODEL // MERGE_COLS),
        in_specs=_gate_specs() + [yspec] * 5, out_specs=(yspec,) * 8, compiler_params=_params(("parallel", "parallel")),
    )(proj, proj, proj, proj, *ys, dmerged)
    return outs[:4], outs[4:]


def _ln_silu_fwd(cd, g, b, name):
    t, c = cd.shape

    def body(x_ref, g_ref, b_ref, o_ref):
        xv = x_ref[...]
        mu = jnp.mean(xv, axis=1, keepdims=True)
        xc = xv - mu
        rs = lax.rsqrt(jnp.mean(xc * xc, axis=1, keepdims=True) + EPS)
        z = xc * rs * g_ref[...] + b_ref[...]
        o_ref[...] = (z * _sig(z)).astype(BF16)

    return pl.pallas_call(
        body, name=name, out_shape=jax.ShapeDtypeStruct((t, c), BF16), grid=(t // ROW_TILE,),
        in_specs=[_row_spec(c), _vec_spec(c), _vec_spec(c)], out_specs=_row_spec(c),
        compiler_params=_params(("parallel",)),
    )(cd, g, b)


def _ln_silu_bwd(cd, g, b, dy, name):
    t, c = cd.shape

    def body(x_ref, g_ref, b_ref, dy_ref, dx_ref, dg_ref, db_ref):
        @pl.when(pl.program_id(0) == 0)
        def _():
            dg_ref[...] = jnp.zeros_like(dg_ref)
            db_ref[...] = jnp.zeros_like(db_ref)

        xv = x_ref[...]
        gv = g_ref[...]
        mu = jnp.mean(xv, axis=1, keepdims=True)
        xc = xv - mu
        rs = lax.rsqrt(jnp.mean(xc * xc, axis=1, keepdims=True) + EPS)
        xh = xc * rs
        z = xh * gv + b_ref[...]
        s = _sig(z)
        dz = dy_ref[...] * s * (1.0 + z * (1.0 - s))
        dg_ref[...] += jnp.sum(dz * xh, axis=0, keepdims=True)
        db_ref[...] += jnp.sum(dz, axis=0, keepdims=True)
        dxh = dz * gv
        dx_ref[...] = rs * (dxh - jnp.mean(dxh, axis=1, keepdims=True) - xh * jnp.mean(dxh * xh, axis=1, keepdims=True))

    return pl.pallas_call(
        body, name=name,
        out_shape=(jax.ShapeDtypeStruct((t, c), F32), jax.ShapeDtypeStruct((1, c), F32),
                   jax.ShapeDtypeStruct((1, c), F32)),
        grid=(t // ROW_TILE,), in_specs=[_row_spec(c), _vec_spec(c), _vec_spec(c), _row_spec(c)],
        out_specs=(_row_spec(c), _vec_spec(c), _vec_spec(c)), compiler_params=_params(("arbitrary",)),
    )(cd, g, b, dy)


def _shift_dn(x, k):
    if k == 0:
        return x
    row = lax.broadcasted_iota(jnp.int32, x.shape, 0)
    return jnp.where(row >= k, pltpu.roll(x, k, 0), 0.0)


def _shift_up(x, k):
    if k == 0:
        return x
    t = x.shape[0]
    row = lax.broadcasted_iota(jnp.int32, x.shape, 0)
    return jnp.where(row < t - k, pltpu.roll(x, t - k, 0), 0.0)


def _conv_fwd(x, w_ref, taps):
    acc = w_ref[pl.ds(taps - 1, 1), :] * x
    for k in range(taps - 1):
        acc += w_ref[pl.ds(k, 1), :] * _shift_dn(x, taps - 1 - k)
    return acc


def _conv_bwd(x, dy, w_ref, dw_ref, taps):
    dx = w_ref[pl.ds(taps - 1, 1), :] * dy
    dw_ref[pl.ds(taps - 1, 1), :] = jnp.sum(dy * x, axis=0, keepdims=True)
    for k in range(taps - 1):
        s = taps - 1 - k
        dx += w_ref[pl.ds(k, 1), :] * _shift_up(dy, s)
        dw_ref[pl.ds(k, 1), :] = jnp.sum(dy * _shift_dn(x, s), axis=0, keepdims=True)
    return dx


def _scan_fwd(a, u):
    t = a.shape[0]
    k = 1
    while k < t:
        u = u + a * _shift_dn(u, k)
        if 2 * k < t:
            a = a * _shift_dn(a, k)
        k *= 2
    return u


def _scan_rev(a, u):
    t = a.shape[0]
    k = 1
    while k < t:
        u = u + a * _shift_up(u, k)
        if 2 * k < t:
            a = a * _shift_up(a, k)
        k *= 2
    return u


def _one_minus_exp(y):
    return jnp.where(y > -1e-3, -(y + 0.5 * y * y + (1.0 / 6.0) * y * y * y), 1.0 - jnp.exp(y))


GELU_C = math.sqrt(2.0 / math.pi)


def _gelu(x):
    th = jnp.tanh(GELU_C * (x + 0.044715 * x * x * x))
    return 0.5 * x * (1.0 + th), th


def _softplus(x):
    return jnp.maximum(x, 0.0) + jnp.log(1.0 + jnp.exp(-jnp.abs(x)))


def _chunk_spec(t, blk0):
    return pl.BlockSpec((t, LANE), functools.partial(lambda c, b: (0, b + c), b=blk0))


def _tap_spec(taps):
    return pl.BlockSpec((taps, LANE), lambda c: (0, c))


def _cvec_spec():
    return pl.BlockSpec((1, LANE), lambda c: (0, c))


def _cmat_spec():
    return pl.BlockSpec((1, LANE, LANE), lambda c: (c, 0, 0))


def _lru_forward(ax, wA_ref, bA_ref, wx_ref, bx_ref, wa_ref, ba_ref, lam_ref):
    ca = _conv_fwd(ax, wA_ref, CONV_A) + bA_ref[...]
    gi = _sig(_dot(ca, wx_ref[0], NN) + bx_ref[...])
    gr = _sig(_dot(ca, wa_ref[0], NN) + ba_ref[...])
    sp = _softplus(-lam_ref[...])
    la = -LRU_C * gr * sp
    a = jnp.exp(la)
    mult = jnp.sqrt(_one_minus_exp(2.0 * la))
    h = _scan_fwd(a, ca * gi * mult)
    return ca, gi, gr, sp, a, mult, h


def _a_fwd(proj, wA, bA, wx, bx, wa, ba, lam, name):
    t = proj.shape[0]

    def body(ax_ref, ag_ref, wA_ref, bA_ref, wx_ref, bx_ref, wa_ref, ba_ref, lam_ref, o_ref):
        h = _lru_forward(ax_ref[...], wA_ref, bA_ref, wx_ref, bx_ref, wa_ref, ba_ref, lam_ref)[-1]
        o_ref[...] = (h * _gelu(ag_ref[...])[0]).astype(BF16)

    return pl.pallas_call(
        body, name=name, out_shape=jax.ShapeDtypeStruct((t, BW), BF16), grid=(BW // LANE,),
        in_specs=[_chunk_spec(t, C_AX), _chunk_spec(t, C_AG), _tap_spec(CONV_A), _cvec_spec(), _cmat_spec(),
                  _cvec_spec(), _cmat_spec(), _cvec_spec(), _cvec_spec()],
        out_specs=_chunk_spec(t, 0), compiler_params=_params(("parallel",)),
    )(proj, proj, wA, bA, wx, bx, wa, ba, lam)


def _a_bwd(proj, dya, wA, bA, wx, bx, wa, ba, lam, name):
    t = proj.shape[0]

    def body(ax_ref, ag_ref, dy_ref, wA_ref, bA_ref, wx_ref, bx_ref, wa_ref, ba_ref, lam_ref,
             dax_ref, dag_ref, dwA_ref, dbA_ref, dwx_ref, dbx_ref, dwa_ref, dba_ref, dlam_ref):
        ax = ax_ref[...]
        ag = ag_ref[...]
        dy = dy_ref[...]
        ca, gi, gr, sp, a, mult, h = _lru_forward(ax, wA_ref, bA_ref, wx_ref, bx_ref, wa_ref, ba_ref, lam_ref)
        gel, th = _gelu(ag)
        dgel = 0.5 * (1.0 + th) + 0.5 * ag * (1.0 - th * th) * GELU_C * (1.0 + 3.0 * 0.044715 * ag * ag)
        dag_ref[...] = (dy * h * dgel).astype(BF16)
        s = _scan_rev(_shift_up(a, 1), dy * gel)
        da = s * _shift_dn(h, 1)
        dca = s * gi * mult
        dgi = s * ca * mult
        dmult = s * ca * gi
        dla = da * a - dmult * a * a / mult
        dgr = dla * (-LRU_C * sp)
        dsp = jnp.sum(dla * (-LRU_C * gr), axis=0, keepdims=True)
        dlam_ref[...] = -_sig(-lam_ref[...]) * dsp
        dzi = dgi * gi * (1.0 - gi)
        dzr = dgr * gr * (1.0 - gr)
        dbx_ref[...] = jnp.sum(dzi, axis=0, keepdims=True)
        dba_ref[...] = jnp.sum(dzr, axis=0, keepdims=True)
        dwx_ref[0] = _dot(ca, dzi, TN)
        dwa_ref[0] = _dot(ca, dzr, TN)
        dca += _dot(dzi, wx_ref[0], NT) + _dot(dzr, wa_ref[0], NT)
        dbA_ref[...] = jnp.sum(dca, axis=0, keepdims=True)
        dax_ref[...] = _conv_bwd(ax, dca, wA_ref, dwA_ref, CONV_A).astype(BF16)

    big = jax.ShapeDtypeStruct((t, BW), BF16)
    vec = jax.ShapeDtypeStruct((1, BW), F32)
    mat = jax.ShapeDtypeStruct((BW // LANE, LANE, LANE), F32)
    return pl.pallas_call(
        body, name=name,
        out_shape=(big, big, jax.ShapeDtypeStruct((CONV_A, BW), F32), vec, mat, vec, mat, vec, vec),
        grid=(BW // LANE,),
        in_specs=[_chunk_spec(t, C_AX), _chunk_spec(t, C_AG), _chunk_spec(t, 0), _tap_spec(CONV_A), _cvec_spec(),
                  _cmat_spec(), _cvec_spec(), _cmat_spec(), _cvec_spec(), _cvec_spec()],
        out_specs=(_chunk_spec(t, 0), _chunk_spec(t, 0), _tap_spec(CONV_A), _cvec_spec(), _cmat_spec(), _cvec_spec(),
                   _cmat_spec(), _cvec_spec(), _cvec_spec()),
        compiler_params=_params(("parallel",)),
    )(proj, proj, dya, wA, bA, wx, bx, wa, ba, lam)


def _b_fwd(proj, wB, name):
    t = proj.shape[0]

    def body(bv_ref, bc_ref, bb_ref, w_ref, o_ref):
        o_ref[...] = (bb_ref[...] * _conv_fwd(bc_ref[...] * bv_ref[...], w_ref, CONV_B)).astype(BF16)

    return pl.pallas_call(
        body, name=name, out_shape=jax.ShapeDtypeStruct((t, BW), BF16), grid=(BW // LANE,),
        in_specs=[_chunk_spec(t, C_BV), _chunk_spec(t, C_BC), _chunk_spec(t, C_BB), _tap_spec(CONV_B)],
        out_specs=_chunk_spec(t, 0), compiler_params=_params(("parallel",)),
    )(proj, proj, proj, wB)


def _b_bwd(proj, dyb, wB, name):
    t = proj.shape[0]

    def body(bv_ref, bc_ref, bb_ref, dy_ref, w_ref, dbv_ref, dbc_ref, dbb_ref, dw_ref):
        bv = bv_ref[...]
        bc = bc_ref[...]
        dy = dy_ref[...]
        p = bc * bv
        dbb_ref[...] = (dy * _conv_fwd(p, w_ref, CONV_B)).astype(BF16)
        dp = _conv_bwd(p, dy * bb_ref[...], w_ref, dw_ref, CONV_B)
        dbc_ref[...] = (dp * bv).astype(BF16)
        dbv_ref[...] = (dp * bc).astype(BF16)

    big = jax.ShapeDtypeStruct((t, BW), BF16)
    return pl.pallas_call(
        body, name=name, out_shape=(big, big, big, jax.ShapeDtypeStruct((CONV_B, BW), F32)), grid=(BW // LANE,),
        in_specs=[_chunk_spec(t, C_BV), _chunk_spec(t, C_BC), _chunk_spec(t, C_BB), _chunk_spec(t, 0),
                  _tap_spec(CONV_B)],
        out_specs=(_chunk_spec(t, 0),) * 3 + (_tap_spec(CONV_B),), compiler_params=_params(("parallel",)),
    )(proj, proj, proj, dyb, wB)


def _d_conv_fwd(proj, wD, bD, name):
    t = proj.shape[0]

    def body(d1_ref, d2_ref, w_ref, b_ref, o_ref):
        o_ref[...] = _conv_fwd(d1_ref[...] * _sig(d2_ref[...]), w_ref, CONV_D) + b_ref[...]

    return pl.pallas_call(
        body, name=name, out_shape=jax.ShapeDtypeStruct((t, BW), F32), grid=(BW // LANE,),
        in_specs=[_chunk_spec(t, C_D1), _chunk_spec(t, C_D2), _tap_spec(CONV_D), _cvec_spec()],
        out_specs=_chunk_spec(t, 0), compiler_params=_params(("parallel",)),
    )(proj, proj, wD, bD)


def _d_conv_bwd(proj, dcd, wD, name):
    t = proj.shape[0]

    def body(d1_ref, d2_ref, dy_ref, w_ref, dd1_ref, dd2_ref, dw_ref, db_ref):
        d1 = d1_ref[...]
        s = _sig(d2_ref[...])
        dy = dy_ref[...]
        db_ref[...] = jnp.sum(dy, axis=0, keepdims=True)
        dd = _conv_bwd(d1 * s, dy, w_ref, dw_ref, CONV_D)
        dd1_ref[...] = (dd * s).astype(BF16)
        dd2_ref[...] = (dd * d1 * s * (1.0 - s)).astype(BF16)

    big = jax.ShapeDtypeStruct((t, BW), BF16)
    return pl.pallas_call(
        body, name=name,
        out_shape=(big, big, jax.ShapeDtypeStruct((CONV_D, BW), F32), jax.ShapeDtypeStruct((1, BW), F32)),
        grid=(BW // LANE,),
        in_specs=[_chunk_spec(t, C_D1), _chunk_spec(t, C_D2), _chunk_spec(t, 0), _tap_spec(CONV_D)],
        out_specs=(_chunk_spec(t, 0), _chunk_spec(t, 0), _tap_spec(CONV_D), _cvec_spec()),
        compiler_params=_params(("parallel",)),
    )(proj, proj, dcd, wD)


SCALE = HEAD_DIM ** -0.5
GROUP = N_Q // N_KV


def _attn_probs(q_ref, k_ref, ss_ref, h, n):
    qi = lax.broadcasted_iota(jnp.int32, (BLK, BLK), 0)
    ki = lax.broadcasted_iota(jnp.int32, (BLK, BLK), 1)
    dist = (qi - ki).astype(F32)
    sink = ss_ref[0, h]
    slope = ss_ref[1, h]
    s0 = pl.multiple_of(n * BLK, BLK)
    sp = pl.multiple_of(jnp.maximum(n - 1, 0) * BLK, BLK)
    q = q_ref[0, pl.ds(s0, BLK), :]
    kc = k_ref[0, pl.ds(s0, BLK), :]
    kp = k_ref[0, pl.ds(sp, BLK), :]
    sc = jnp.where(ki <= qi, _dot(q, kc, NT) * SCALE - slope * dist, NEG_INF)
    first = jnp.where(n >= 1, 0, BLK)
    sv = jnp.where(ki > qi + first, _dot(q, kp, NT) * SCALE - slope * (dist + BLK), NEG_INF)
    m = jnp.maximum(jnp.maximum(jnp.max(sc, axis=1, keepdims=True), jnp.max(sv, axis=1, keepdims=True)), sink)
    pc = jnp.exp(sc - m)
    pp = jnp.exp(sv - m)
    ps = jnp.exp(sink - m)
    z = jnp.sum(pc, axis=1, keepdims=True) + jnp.sum(pp, axis=1, keepdims=True) + ps
    return s0, sp, q, kc, kp, pc, pp, ps, z


def _attn_specs(t):
    qs = pl.BlockSpec((1, t, HEAD_DIM), lambda h: (h, 0, 0))
    ks = pl.BlockSpec((1, t, HEAD_DIM), lambda h: (h // GROUP, 0, 0))
    ss = pl.BlockSpec(memory_space=pltpu.SMEM)
    return qs, ks, ss


def _attn_fwd(q, k, v, ss, name):
    t = q.shape[1]
    qs, ks, sspec = _attn_specs(t)

    def body(q_ref, k_ref, v_ref, ss_ref, o_ref):
        h = pl.program_id(0)

        def blk(n, carry):
            s0, sp, _, _, _, pc, pp, _, z = _attn_probs(q_ref, k_ref, ss_ref, h, n)
            o = _dot(pc, v_ref[0, pl.ds(s0, BLK), :], NN) + _dot(pp, v_ref[0, pl.ds(sp, BLK), :], NN)
            o_ref[0, pl.ds(s0, BLK), :] = (o / z).astype(BF16)
            return carry

        lax.fori_loop(0, t // BLK, blk, 0)

    return pl.pallas_call(
        body, name=name, out_shape=jax.ShapeDtypeStruct((N_Q, t, HEAD_DIM), BF16), grid=(N_Q,),
        in_specs=[qs, ks, ks, sspec], out_specs=qs, compiler_params=_params(("parallel",)),
    )(q, k, v, ss)


def _attn_bwd(q, k, v, do, ss, name):
    t = q.shape[1]
    qs, ks, sspec = _attn_specs(t)

    def body(q_ref, k_ref, v_ref, do_ref, ss_ref, dq_ref, dk_ref, dv_ref, ds_ref):
        h = pl.program_id(0)

        @pl.when(h % GROUP == 0)
        def _():
            dk_ref[...] = jnp.zeros_like(dk_ref)
            dv_ref[...] = jnp.zeros_like(dv_ref)

        def blk(n, dsink):
            s0, sp, q, kc, kp, pc, pp, ps, z = _attn_probs(q_ref, k_ref, ss_ref, h, n)
            rz = 1.0 / z
            pc = pc * rz
            pp = pp * rz
            do_b = do_ref[0, pl.ds(s0, BLK), :]
            dpc = _dot(do_b, v_ref[0, pl.ds(s0, BLK), :], NT)
            dpp = _dot(do_b, v_ref[0, pl.ds(sp, BLK), :], NT)
            delta = jnp.sum(pc * dpc, axis=1, keepdims=True) + jnp.sum(pp * dpp, axis=1, keepdims=True)
            dsc = pc * (dpc - delta)
            dsp = pp * (dpp - delta)
            dq_ref[0, pl.ds(s0, BLK), :] = ((_dot(dsc, kc, NN) + _dot(dsp, kp, NN)) * SCALE).astype(BF16)
            dk_ref[0, pl.ds(s0, BLK), :] += _dot(dsc, q, TN) * SCALE
            dk_ref[0, pl.ds(sp, BLK), :] += _dot(dsp, q, TN) * SCALE
            dv_ref[0, pl.ds(s0, BLK), :] += _dot(pc, do_b, TN)
            dv_ref[0, pl.ds(sp, BLK), :] += _dot(pp, do_b, TN)
            return dsink - ps * rz * delta

        dsink = lax.fori_loop(0, t // BLK, blk, jnp.zeros((BLK, 1), F32))
        ds_ref[...] = jnp.full(ds_ref.shape, jnp.sum(dsink), F32)

    kv = jax.ShapeDtypeStruct((N_KV, t, HEAD_DIM), F32)
    return pl.pallas_call(
        body, name=name,
        out_shape=(jax.ShapeDtypeStruct((N_Q, t, HEAD_DIM), BF16), kv, kv, jax.ShapeDtypeStruct((N_Q, 8, LANE), F32)),
        grid=(N_Q,), in_specs=[qs, ks, ks, qs, sspec],
        out_specs=(qs, ks, ks, pl.BlockSpec((1, 8, LANE), lambda h: (h, 0, 0))),
        compiler_params=_params(("arbitrary",)),
    )(q, k, v, do, ss)


def _heads(x2d, n):
    t = x2d.shape[0]
    return x2d.reshape(t, n, HEAD_DIM).transpose(1, 0, 2)


def _unheads(x3d):
    n, t, _ = x3d.shape
    return x3d.transpose(1, 0, 2).reshape(t, n * HEAD_DIM)


SMALL_ELEMS = 256 * 1024
TILE_ELEMS = 640 * 1024


def _row_tile(r, c):
    if r * c <= SMALL_ELEMS:
        return r
    return _pick(r, [t for t in (512, 256, 128, 64, 32, 16, 8) if t * c <= TILE_ELEMS])


def _adamw(w, g, m, v, name):
    r, c = w.shape
    tr = _row_tile(r, c)
    spec = pl.BlockSpec((tr, c), lambda i: (i, 0))

    def body(w_ref, g_ref, m_ref, v_ref, d_ref, nm_ref, nv_ref):
        gv = g_ref[...]
        nm = ADAM_B1 * m_ref[...] + (1.0 - ADAM_B1) * gv
        nv = ADAM_B2 * v_ref[...] + (1.0 - ADAM_B2) * (gv * gv)
        m_hat = nm / (1.0 - ADAM_B1 ** ADAM_STEP)
        v_hat = nv / (1.0 - ADAM_B2 ** ADAM_STEP)
        d_ref[...] = -ADAM_LR * (m_hat / (jnp.sqrt(v_hat) + ADAM_EPS) + ADAM_WD * w_ref[...])
        nm_ref[...] = nm
        nv_ref[...] = nv

    shp = jax.ShapeDtypeStruct((r, c), F32)
    return pl.pallas_call(
        body, name=name, out_shape=(shp, shp, shp), grid=(r // tr,), in_specs=[spec] * 4, out_specs=(spec,) * 3,
        compiler_params=_params(("parallel",)),
    )(w, g, m, v)


def _sum_leading(x, name):
    n, r, c = x.shape
    tr = _row_tile(r, c)

    def body(x_ref, o_ref):
        acc = x_ref[0]
        for i in range(1, n):
            acc = acc + x_ref[i]
        o_ref[...] = acc

    return pl.pallas_call(
        body, name=name, out_shape=jax.ShapeDtypeStruct((r, c), F32), grid=(r // tr,),
        in_specs=[pl.BlockSpec((n, tr, c), lambda i: (0, i, 0))], out_specs=pl.BlockSpec((tr, c), lambda i: (i, 0)),
        compiler_params=_params(("parallel",)),
    )(x)


def _sum_own_plus(p, sel, recv, name, out_dtype):
    _, r, c = p.shape
    n = recv.shape[0]
    tr = _pick(r, (256, 128, 16))

    def body(sel_ref, p_ref, r_ref, o_ref):
        acc = p_ref[0].astype(F32)
        for i in range(n):
            acc = acc + r_ref[i].astype(F32)
        o_ref[...] = acc.astype(out_dtype)

    grid_spec = pltpu.PrefetchScalarGridSpec(
        num_scalar_prefetch=1, grid=(r // tr,),
        in_specs=[pl.BlockSpec((1, tr, c), lambda i, s: (s[0], i, 0)), pl.BlockSpec((n, tr, c), lambda i, s: (0, i, 0))],
        out_specs=pl.BlockSpec((tr, c), lambda i, s: (i, 0)))
    return pl.pallas_call(
        body, name=name, out_shape=jax.ShapeDtypeStruct((r, c), out_dtype), grid_spec=grid_spec,
        compiler_params=_params(("parallel",)),
    )(sel, p, recv)


def _coords():
    return lax.axis_index("x"), lax.axis_index("y"), lax.axis_index("c")


def _allgather8(x2, name, space):
    _, m, n = x2.shape

    def body(x_ref, out_ref, send_sems, recv_sems, local_sem):
        x, y, c = _coords()
        me, sibling = (x, y, c), (x, y, 1 - c)
        chips = [(1 - x, y), (x, 1 - y), (1 - x, 1 - y)]
        mine_src = x_ref.at[c]

        def rows(px, py, pc):
            return out_ref.at[4 * px + 2 * py + pc]

        def copy(k, block, to, src=None):
            return pltpu.make_async_remote_copy(
                src_ref=rows(*block) if src is None else src, dst_ref=rows(*block),
                send_sem=send_sems.at[k], recv_sem=recv_sems.at[k], device_id=to, device_id_type=MESH)

        mine = pltpu.make_async_copy(mine_src, rows(*me), local_sem)
        mine.start()
        first = [copy(0, me, sibling, src=mine_src)]
        first += [copy(1 + j, me, (*chip, c), src=mine_src) for j, chip in enumerate(chips)]
        for cp in first:
            cp.start()
        passed = [copy(4 + j, (*chip, c), sibling) for j, chip in enumerate(chips)]
        for j, chip in enumerate(chips):
            copy(1 + j, (*chip, c), me).wait_recv()
            passed[j].start()
        copy(0, sibling, me).wait_recv()
        for j, chip in enumerate(chips):
            copy(4 + j, (*chip, 1 - c), me).wait_recv()
        for cp in first + passed:
            cp.wait_send()
        mine.wait()

    return pl.pallas_call(
        body, name=name, out_shape=jax.ShapeDtypeStruct((8, m, n), x2.dtype),
        in_specs=[pl.BlockSpec(memory_space=space)], out_specs=pl.BlockSpec(memory_space=space),
        scratch_shapes=[pltpu.SemaphoreType.DMA((7,)), pltpu.SemaphoreType.DMA((7,)), pltpu.SemaphoreType.DMA],
        compiler_params=pltpu.CompilerParams(vmem_limit_bytes=VMEM_LIMIT),
    )(x2)


def _swap_sibling_half(p, name):
    def body(p_ref, out_ref, send_sem, recv_sem):
        x, y, c = _coords()
        cp = pltpu.make_async_remote_copy(
            src_ref=p_ref.at[1 - c], dst_ref=out_ref, send_sem=send_sem, recv_sem=recv_sem,
            device_id=(x, y, 1 - c), device_id_type=MESH)
        cp.start()
        cp.wait()

    return pl.pallas_call(
        body, name=name, out_shape=jax.ShapeDtypeStruct(p.shape[1:], p.dtype),
        in_specs=[pl.BlockSpec(memory_space=pl.ANY)], out_specs=pl.BlockSpec(memory_space=pl.ANY),
        scratch_shapes=[pltpu.SemaphoreType.DMA, pltpu.SemaphoreType.DMA],
    )(p)


N_REG = len(ROW_REGIONS) + 1


def _chip_window(ref, lead, r, j):
    view = ref if lead is None else ref.at[lead]
    if r < len(ROW_REGIONS):
        off, rows = ROW_REGIONS[r]
        return view.at[pl.ds(pl.multiple_of(off + j * rows, 16), rows), :]
    return view.at[pl.ds(R_OUT, OUT_ROWS), pl.ds(pl.multiple_of(j * OUT_COLS, LANE), OUT_COLS)]


HBM_SPEC = pl.BlockSpec(memory_space=pltpu.HBM)
SEM_SPEC = pl.BlockSpec(memory_space=pltpu.SEMAPHORE)


def _half_window(ref, r, j, h):
    if r < len(ROW_REGIONS):
        off, rows = ROW_REGIONS[r]
        return ref.at[pl.ds(pl.multiple_of(off + j * rows + h * (rows // 2), 16), rows // 2), :]
    half = OUT_ROWS // 2
    return ref.at[pl.ds(pl.multiple_of(R_OUT + h * half, 16), half),
                  pl.ds(pl.multiple_of(j * OUT_COLS, LANE), OUT_COLS)]


def _other_chips():
    x, y, _ = _coords()
    return [(1 - x, y), (x, 1 - y), (1 - x, 1 - y)]


def _ici_copies(srcs, arena_ref, send_sems, recv_sems, regions):
    x, y, c = _coords()
    sends, arrivals = [], []
    for k, (cx, cy) in enumerate(_other_chips()):
        for r in regions:
            def remote(src, j):
                return pltpu.make_async_remote_copy(
                    src_ref=src, dst_ref=_half_window(arena_ref, r, j, c), send_sem=send_sems.at[3 * r + k],
                    recv_sem=recv_sems.at[3 * r + k], device_id=(cx, cy, c), device_id_type=MESH)
            rows = srcs[r].shape[0] // 2
            sends.append(remote(srcs[r].at[pl.ds(pl.multiple_of(c * rows, 16), rows), :], 2 * x + y))
            arrivals.append(remote(_half_window(arena_ref, r, 2 * cx + cy, c), 2 * cx + cy))
    return sends, arrivals


def _sibling_copies(srcs, arena_ref, send_sems, recv_sems, regions):
    x, y, c = _coords()
    sends, arrivals = [], []

    def remote(win, r, k, src=None):
        return pltpu.make_async_remote_copy(
            src_ref=win if src is None else src, dst_ref=win, send_sem=send_sems.at[r, k],
            recv_sem=recv_sems.at[r, k], device_id=(x, y, 1 - c), device_id_type=MESH)

    for k, (cx, cy) in enumerate(_other_chips()):
        for r in regions:
            sends.append(remote(_half_window(arena_ref, r, 2 * cx + cy, c), r, k))
            arrivals.append(remote(_half_window(arena_ref, r, 2 * cx + cy, 1 - c), r, k))
    for r in regions:
        own = _chip_window(arena_ref, None, r, 2 * x + y)
        sends.append(remote(own, r, 3, src=srcs[r]))
        arrivals.append(remote(own, r, 3))
    return sends, arrivals


ICI_SEMS = pltpu.SemaphoreType.DMA((3 * N_REG,))
SIBLING_SEMS = pltpu.SemaphoreType.DMA((N_REG, 4))
ARENA_SHAPE = (ARENA_ROWS, ARENA_W)
ALL_REGIONS = tuple(range(N_REG))
IN_REGION = (3,)
REST_REGIONS = (0, 1, 2, 4, 5)


def _gather_layer(shards, name, regions=ALL_REGIONS):
    def body(*refs):
        srcs, arena_ref = refs[:N_REG], refs[N_REG]
        ici_send, ici_recv, sib_send, sib_recv = refs[N_REG + 1:]
        sends, arrivals = _ici_copies(srcs, arena_ref, ici_send, ici_recv, regions)
        passes, landings = _sibling_copies(srcs, arena_ref, sib_send, sib_recv, regions)
        for cp in sends + passes[len(arrivals):]:
            cp.start()
        for arrival, onward in zip(arrivals, passes):
            arrival.wait_recv()
            onward.start()
        for cp in landings:
            cp.wait_recv()
        for cp in sends + passes:
            cp.wait_send()

    return pl.pallas_call(
        body, name=name, out_shape=jax.ShapeDtypeStruct(ARENA_SHAPE, BF16),
        in_specs=[pl.BlockSpec(memory_space=pl.ANY)] * N_REG, out_specs=pl.BlockSpec(memory_space=pl.ANY),
        scratch_shapes=[ICI_SEMS, ICI_SEMS, SIBLING_SEMS, SIBLING_SEMS],
    )(*shards)


def _gather_start(shards, after, name, regions=ALL_REGIONS):
    def body(*refs):
        srcs, arena_ref = refs[:N_REG], refs[N_REG]
        send_sems, recv_sems = refs[N_REG + 2], refs[N_REG + 3]
        token = refs[-1]
        for cp in _ici_copies(srcs, arena_ref, send_sems, recv_sems, regions)[0]:
            cp.start()
        token[...] = jnp.zeros_like(token)

    hbm = lambda a: pltpu.with_memory_space_constraint(a, pltpu.HBM)
    outs = pl.pallas_call(
        body, name=name,
        out_shape=(ICI_SEMS, ICI_SEMS, *[pltpu.HBM(s.shape, s.dtype) for s in shards],
                   pltpu.HBM(ARENA_SHAPE, BF16), jax.ShapeDtypeStruct((8, LANE), F32)),
        in_specs=[HBM_SPEC] * (N_REG + 1) + [pl.BlockSpec(memory_space=pl.ANY)],
        out_specs=(SEM_SPEC, SEM_SPEC, *[HBM_SPEC] * (N_REG + 1), pl.BlockSpec(memory_space=pltpu.VMEM)),
        input_output_aliases={i: 2 + i for i in range(N_REG + 1)},
        compiler_params=pltpu.CompilerParams(has_side_effects=pltpu.SideEffectType.DATAFLOW_SIDE_EFFECTING),
    )(*[hbm(s) for s in shards], hbm(lax.empty(ARENA_SHAPE, BF16)), after)
    return outs[0], outs[1], outs[2:2 + N_REG], outs[2 + N_REG], outs[-1]


def _gather_wait(send_sems, recv_sems, shards, arena, after, name, regions=ALL_REGIONS):
    def body(*refs):
        srcs, arena_ref = refs[:N_REG], refs[N_REG]
        sends, arrivals = _ici_copies(srcs, arena_ref, refs[N_REG + 1], refs[N_REG + 2], regions)
        for cp in sends:
            cp.wait_send()
        for cp in arrivals:
            cp.wait_recv()

    outs = pl.pallas_call(
        body, name=name,
        out_shape=(*[pltpu.HBM(s.shape, s.dtype) for s in shards], pltpu.HBM(ARENA_SHAPE, BF16)),
        in_specs=[HBM_SPEC] * (N_REG + 1) + [SEM_SPEC, SEM_SPEC, pl.BlockSpec(memory_space=pl.ANY)],
        out_specs=(HBM_SPEC,) * (N_REG + 1), input_output_aliases={i: i for i in range(N_REG + 1)},
        compiler_params=pltpu.CompilerParams(has_side_effects=pltpu.SideEffectType.DATAFLOW_SIDE_EFFECTING),
    )(*shards, arena, send_sems, recv_sems, after)
    return outs[:N_REG], outs[N_REG]


def _gather_finish(shards, arena, name, regions=ALL_REGIONS):
    def body(*refs):
        srcs, arena_ref = refs[:N_REG], refs[N_REG + 1]
        sends, arrivals = _sibling_copies(srcs, arena_ref, refs[N_REG + 2], refs[N_REG + 3], regions)
        for cp in sends:
            cp.start()
        for cp in arrivals:
            cp.wait_recv()
        for cp in sends:
            cp.wait_send()

    return pl.pallas_call(
        body, name=name, out_shape=jax.ShapeDtypeStruct(ARENA_SHAPE, BF16),
        in_specs=[pl.BlockSpec(memory_space=pl.ANY)] * (N_REG + 1), out_specs=pl.BlockSpec(memory_space=pl.ANY),
        scratch_shapes=[SIBLING_SEMS, SIBLING_SEMS], input_output_aliases={N_REG: 0},
    )(*shards, arena)


def _scatter_pieces(s, name):
    def body(s_ref, main_ref, outp_ref, send_sems, recv_sems):
        x, y, c = _coords()
        chips = [(1 - x, y), (x, 1 - y), (1 - x, 1 - y)]
        cps = []
        for k, (cx, cy) in enumerate(chips):
            for r in range(N_REG):
                if r < len(ROW_REGIONS):
                    dst = main_ref.at[k, pl.ds(PIECE_OFF[r], ROW_REGIONS[r][1]), :]
                else:
                    dst = outp_ref.at[k]
                cps.append(pltpu.make_async_remote_copy(
                    src_ref=_chip_window(s_ref, None, r, 2 * cx + cy), dst_ref=dst, send_sem=send_sems.at[r, k],
                    recv_sem=recv_sems.at[r, k], device_id=(cx, cy, c), device_id_type=MESH))
        for cp in cps:
            cp.start()
        for cp in cps:
            cp.wait()

    return pl.pallas_call(
        body, name=name,
        out_shape=(jax.ShapeDtypeStruct((3, PIECE_ROWS, ARENA_W), s.dtype),
                   jax.ShapeDtypeStruct((3, OUT_ROWS, OUT_COLS), s.dtype)),
        in_specs=[pl.BlockSpec(memory_space=pl.ANY)], out_specs=(pl.BlockSpec(memory_space=pl.ANY),) * 2,
        scratch_shapes=[pltpu.SemaphoreType.DMA((N_REG, 3)), pltpu.SemaphoreType.DMA((N_REG, 3))],
    )(s)


def _own_piece(s, chip):
    main = jnp.concatenate([lax.dynamic_slice(s, (off + chip * rows, 0), (rows, ARENA_W))
                            for off, rows in ROW_REGIONS])
    return main, lax.dynamic_slice(s, (R_OUT, chip * OUT_COLS), (OUT_ROWS, OUT_COLS))


def _swap_pair(a, b, name):
    def body(a_ref, b_ref, ra_ref, rb_ref, send_sems, recv_sems):
        x, y, c = _coords()
        cps = [pltpu.make_async_remote_copy(
            src_ref=s, dst_ref=d, send_sem=send_sems.at[i], recv_sem=recv_sems.at[i], device_id=(x, y, 1 - c),
            device_id_type=MESH) for i, (s, d) in enumerate(((a_ref, ra_ref), (b_ref, rb_ref)))]
        for cp in cps:
            cp.start()
        for cp in cps:
            cp.wait()

    return pl.pallas_call(
        body, name=name,
        out_shape=(jax.ShapeDtypeStruct(a.shape, a.dtype), jax.ShapeDtypeStruct(b.shape, b.dtype)),
        in_specs=[pl.BlockSpec(memory_space=pl.ANY)] * 2, out_specs=(pl.BlockSpec(memory_space=pl.ANY),) * 2,
        scratch_shapes=[pltpu.SemaphoreType.DMA((2,)), pltpu.SemaphoreType.DMA((2,))],
    )(a, b)


OUT_NAMES = ("w_a_out", "w_b_out", "w_c_out", "w_d_out")


def _arena_shards(w):
    t = lambda a: a.astype(BF16).transpose(0, 2, 1)
    return (w["w_ffn_down"].astype(BF16), t(w["w_ffn_gate"]), t(w["w_ffn_up"]), t(w["w_in"]), w["w_o"].astype(BF16),
            jnp.concatenate([w[n].astype(BF16) for n in OUT_NAMES], axis=1))


def _shard_grads(main, outp):
    t = lambda r: main[:, PIECE_OFF[r]:PIECE_OFF[r] + ROW_REGIONS[r][1]]
    g = dict(w_ffn_down=t(0), w_ffn_gate=t(1).transpose(0, 2, 1), w_ffn_up=t(2).transpose(0, 2, 1),
             w_in=t(3).transpose(0, 2, 1), w_o=t(4))
    for i, n in enumerate(OUT_NAMES):
        g[n] = outp[:, i * BW:(i + 1) * BW]
    return g


def _gather_taps(p, name):
    mine = jnp.concatenate([p[n] for n in CONV_NAMES], axis=1).reshape(DEPTH * N_TAPS, LANE)
    rows = -(-mine.shape[0] // 8) * 8
    mine = jnp.concatenate([mine, jnp.zeros((rows - mine.shape[0], LANE), F32)])
    g = _allgather8(jnp.stack([mine, mine]), name, pltpu.VMEM)[0::2, :DEPTH * N_TAPS]
    full = g.reshape(4, DEPTH, N_TAPS, LANE).transpose(1, 2, 0, 3).reshape(DEPTH, N_TAPS, BW)
    return dict(conv_a_w=full[:, :CONV_A], conv_b_w=full[:, CONV_A:CONV_A + CONV_B], conv_d_w=full[:, CONV_A + CONV_B:])


def _flat_pack(arrs):
    flat = jnp.concatenate([a.reshape(-1).astype(F32) for a in arrs])
    rows = -(-flat.shape[0] // (8 * LANE)) * 8
    return jnp.concatenate([flat, jnp.zeros((rows * LANE - flat.shape[0],), F32)]).reshape(rows, LANE)


def _flat_unpack(packed, shapes):
    flat, out, off = packed.reshape(-1), [], 0
    for s in shapes:
        cnt = int(np.prod(s))
        out.append(flat[off:off + cnt].reshape(s))
        off += cnt
    return out


def _blockdiag_chunks(w):
    w4 = w.reshape(4, 2, 64, 64)
    z = jnp.zeros((4, 2, 64, 2, 64), F32)
    z = z.at[:, 0, :, 0, :].set(w4[:, 0]).at[:, 1, :, 1, :].set(w4[:, 1])
    return z.reshape(4, LANE, LANE)


def _blockdiag_extract(d):
    d5 = d.reshape(4, 2, 64, 2, 64)
    return jnp.stack([d5[:, 0, :, 0, :], d5[:, 1, :, 1, :]], axis=1).reshape(8, 64, 64)


SLOPES = np.asarray([2.0 ** (-8.0 * (i + 1) / N_Q) for i in range(N_Q)], np.float32)


def _layer_consts(p, fw, l):
    row = lambda a: a[l].reshape(1, -1)
    return dict(
        g1=row(p["norm1_g"]), g2=row(p["norm2_g"]), wA=fw["conv_a_w"][l], bA=row(p["conv_a_b"]),
        wx=_blockdiag_chunks(p["lru_wx"][l]), bx=row(p["lru_bx"]), wa=_blockdiag_chunks(p["lru_wa"][l]),
        ba=row(p["lru_ba"]), lam=row(p["lru_lambda"]), wB=fw["conv_b_w"][l],
        ss=jnp.stack([p["sinks"][l], jnp.asarray(SLOPES)]), wD=fw["conv_d_w"][l], bD=row(p["conv_d_b"]),
        lg=row(p["ln_d_g"]), lb=row(p["ln_d_b"]))


def _layer_fwd(x, c, fw, l, rest_of_weights=None):
    t = f"l{l}_"
    xn = _rms_fwd(x, c["g1"], t + "rms1")
    wt = lambda off, rows: Win(fw["arena"][l], None, off, rows)
    proj = _mm(xn, Win(fw["arena_in"][l], None, R_IN, IN_W), "nt", t + "proj")
    ya = _a_fwd(proj, c["wA"], c["bA"], c["wx"], c["bx"], c["wa"], c["ba"], c["lam"], t + "a_fwd")
    yb = _b_fwd(proj, c["wB"], t + "b_fwd")
    q3 = _heads(proj[:, OFF_Q:OFF_K], N_Q)
    k3 = _heads(proj[:, OFF_K:OFF_V], N_KV)
    v3 = _heads(proj[:, OFF_V:OFF_V + N_KV * HEAD_DIM], N_KV)
    yc = _unheads(_attn_fwd(q3, k3, v3, c["ss"], t + "attn_fwd"))
    cd = _d_conv_fwd(proj, c["wD"], c["bD"], t + "d_conv_fwd")
    yd = _ln_silu_fwd(cd, c["lg"], c["lb"], t + "d_ln_fwd")
    ys = (ya, yb, yc, yd)
    if fw["arena"][l] is None:
        fw["arena"][l] = rest_of_weights(yd)
    big_y = tuple(_mm(y, wt(R_OUT + i * BW, BW), "nn", t + f"out{i}") for i, y in enumerate(ys))
    merged = _merge_fwd(proj, big_y, t + "merge_fwd")
    hres = _mm(merged, wt(R_O, D_MODEL), "nn", t + "wo", add=x)
    hn = _rms_fwd(hres, c["g2"], t + "rms2")
    gg = _mm(hn, wt(R_GATE, D_FF), "nt", t + "ffn_gate")
    uu = _mm(hn, wt(R_UP, D_FF), "nt", t + "ffn_up")
    act = _swiglu_fwd(gg, uu, t + "swiglu_fwd")
    xout = _mm(act, wt(R_DOWN, D_FF), "nn", t + "ffn_down", add=hres)
    saved = dict(x=x, xn=xn, proj=proj, ys=ys, q3=q3, k3=k3, v3=v3, cd=cd, big_y=big_y, merged=merged, hres=hres,
                 hn=hn, gg=gg, uu=uu, act=act)
    return xout, saved


def _layer_bwd(dxout, s, c, fw, l, ga):
    t = f"l{l}_"
    gs = {}
    wt = lambda off, rows: Win(fw["arena"][l], None, off, rows)
    gt = lambda off, rows: Win(ga, l, off, rows)
    dact = _mm(dxout, wt(R_DOWN, D_FF), "nt", t + "d_act")
    ga = _mm(s["act"], dxout, "tn", t + "dw_down", out=gt(R_DOWN, D_FF))
    dgg, duu = _swiglu_bwd(s["gg"], s["uu"], dact, t + "swiglu_bwd")
    ga = _mm(dgg, s["hn"], "tn", t + "dw_gate", out=gt(R_GATE, D_FF))
    ga = _mm(duu, s["hn"], "tn", t + "dw_up", out=gt(R_UP, D_FF))
    dhn = _mm(dgg, wt(R_GATE, D_FF), "nn", t + "d_hn_g")
    dhn = _mm(duu, wt(R_UP, D_FF), "nn", t + "d_hn_u", add=dhn)
    dhres, gs["norm2_g"] = _rms_bwd(s["hres"], c["g2"], dhn, dxout, t + "rms2_bwd")
    dmerged = _mm(dhres, wt(R_O, D_MODEL), "nt", t + "d_merged")
    ga = _mm(s["merged"], dhres, "tn", t + "dw_o", out=gt(R_O, D_MODEL))
    dbig_y, dgl = _merge_bwd(s["proj"], s["big_y"], dmerged, t + "merge_bwd")
    dys = []
    for i in range(4):
        ga = _mm(s["ys"][i], dbig_y[i], "tn", t + f"dw_out{i}", out=gt(R_OUT + i * BW, BW))
        dys.append(_mm(dbig_y[i], wt(R_OUT + i * BW, BW), "nt", t + f"d_y{i}"))
    proj = s["proj"]
    (dax, dag, gs["conv_a_w"], gs["conv_a_b"], dwx, gs["lru_bx"], dwa, gs["lru_ba"], gs["lru_lambda"]) = _a_bwd(
        proj, dys[0], c["wA"], c["bA"], c["wx"], c["bx"], c["wa"], c["ba"], c["lam"], t + "a_bwd")
    gs["lru_wx"] = _blockdiag_extract(dwx)
    gs["lru_wa"] = _blockdiag_extract(dwa)
    dbv, dbc, dbb, gs["conv_b_w"] = _b_bwd(proj, dys[1], c["wB"], t + "b_bwd")
    dq3, dk3, dv3, dsink = _attn_bwd(s["q3"], s["k3"], s["v3"], _heads(dys[2], N_Q), c["ss"], t + "attn_bwd")
    gs["sinks"] = dsink[:, 0, 0]
    dcd, gs["ln_d_g"], gs["ln_d_b"] = _ln_silu_bwd(s["cd"], c["lg"], c["lb"], dys[3], t + "d_ln_bwd")
    dd1, dd2, gs["conv_d_w"], gs["conv_d_b"] = _d_conv_bwd(proj, dcd, c["wD"], t + "d_conv_bwd")
    dproj = jnp.concatenate(
        [dax, dag, dbv, dbc, dbb, _unheads(dq3), _unheads(dk3).astype(BF16), _unheads(dv3).astype(BF16), dd1, dd2,
         *dgl], axis=1)
    ga = _mm(dproj, s["xn"], "tn", t + "dw_in", out=gt(R_IN, IN_W))
    dxn = _mm(dproj, Win(fw["arena_in"][l], None, R_IN, IN_W), "nn", t + "d_xn")
    dx, gs["norm1_g"] = _rms_bwd(s["x"], c["g1"], dxn, dhres, t + "rms1_bwd")
    return dx, ga, gs


def kernel(x, norm1_g, w_in, conv_a_w, conv_a_b, lru_wx, lru_bx, lru_wa, lru_ba, lru_lambda, w_a_out, conv_b_w, w_b_out, sinks, w_c_out, conv_d_w, conv_d_b, ln_d_g, ln_d_b, w_d_out, w_o, norm2_g, w_ffn_gate, w_ffn_up, w_ffn_down, final_g, loss_target, m_norm1_g, m_w_in, m_conv_a_w, m_conv_a_b, m_lru_wx, m_lru_bx, m_lru_wa, m_lru_ba, m_lru_lambda, m_w_a_out, m_conv_b_w, m_w_b_out, m_sinks, m_w_c_out, m_conv_d_w, m_conv_d_b, m_ln_d_g, m_ln_d_b, m_w_d_out, m_w_o, m_norm2_g, m_w_ffn_gate, m_w_ffn_up, m_w_ffn_down, m_final_g, v_norm1_g, v_w_in, v_conv_a_w, v_conv_a_b, v_lru_wx, v_lru_bx, v_lru_wa, v_lru_ba, v_lru_lambda, v_w_a_out, v_conv_b_w, v_w_b_out, v_sinks, v_w_c_out, v_conv_d_w, v_conv_d_b, v_ln_d_g, v_ln_d_b, v_w_d_out, v_w_o, v_norm2_g, v_w_ffn_gate, v_w_ffn_up, v_w_ffn_down, v_final_g):
    given = dict(locals())
    p = {n: given[n] for n in NAMES}
    mom = {n: given["m_" + n] for n in NAMES}
    var = {n: given["v_" + n] for n in NAMES}
    cx, cy, cc = _coords()
    chip = 2 * cx + cy

    shards = _arena_shards(p)
    fw = _gather_taps(p, "gather_taps")
    shards0, shards1 = [s[0] for s in shards], [s[1] for s in shards]
    fw["arena_in"] = [_gather_layer(shards0, "gather_l0_in", IN_REGION), None]
    fw["arena"] = [None, None]
    flight0 = _gather_start(shards0, fw["arena_in"][0], "gather_l0_rest_start", REST_REGIONS)
    consts = [_layer_consts(p, fw, l) for l in range(DEPTH)]
    consts[0]["g1"] = consts[0]["g1"] + flight0[4][0:1, 0:1]
    flight1 = []

    def rest_of_layer0(after):
        sh, landing = _gather_wait(*flight0[:4], after, "gather_l0_rest_wait", REST_REGIONS)
        arena = _gather_finish(sh, landing, "gather_l0_rest_finish", REST_REGIONS)
        flight1.extend(_gather_start(shards1, arena, "gather_l1_start"))
        consts[0]["g2"] = consts[0]["g2"] + flight1[4][0:1, 0:1]
        return arena

    h = x[0]
    saved = []
    for l in range(DEPTH):
        if l == 1:
            sh, landing = _gather_wait(*flight1[:4], h, "gather_l1_wait")
            fw["arena"][1] = fw["arena_in"][1] = _gather_finish(sh, landing, "gather_l1_finish")
        h, s = _layer_fwd(h, consts[l], fw, l, rest_of_layer0)
        saved.append(s)
    loss_vec, dh, g_final = _loss_head(h, final_g.reshape(1, -1), loss_target[0], "loss_head")
    loss = lax.psum(loss_vec[0, 0], ("x", "y", "c"))

    gss = [None] * DEPTH
    ga = lax.empty((DEPTH, ARENA_ROWS, ARENA_W), BF16)
    for l in reversed(range(DEPTH)):
        dh, ga, gss[l] = _layer_bwd(dh, saved[l], consts[l], fw, l, ga)
    grad_x = dh[None]

    zero = jnp.zeros((1,), jnp.int32)
    from_sibling = _swap_sibling_half(ga[:, None], "grads_swap_sibling")
    chip_sum = _sum_own_plus(ga, cc.reshape(1).astype(jnp.int32), from_sibling, "grads_sum_chip", BF16)
    recv_main, recv_out = _scatter_pieces(chip_sum, "grads_scatter_chips")
    own_main, own_out = _own_piece(chip_sum, chip)
    red_main = _sum_own_plus(own_main[None], zero, recv_main, "grads_sum_all", F32)
    red_out = _sum_own_plus(own_out[None], zero, recv_out, "grads_sum_all_out", F32)
    sib_main, sib_out = _swap_pair(red_main, red_out, "grads_swap_reduced")
    by_layer = lambda mine, theirs: jnp.stack([jnp.where(cc == l, mine, theirs) for l in range(DEPTH)])
    g = _shard_grads(by_layer(red_main, sib_main), by_layer(red_out, sib_out))

    small_full = {n: (g_final.reshape(-1) if n == "final_g" else
                      jnp.stack([gss[l][n].reshape(gss[l][n].shape[-2:] if n.startswith("conv") and n.endswith("_w")
                                                   else p[n].shape[1:]) for l in range(DEPTH)]))
                  for n in SMALL}
    part = _flat_pack([small_full[n] for n in SMALL])
    rows = part.shape[0]
    gathered = _allgather8(jnp.stack([part, part]), "gather_small_grads", pltpu.VMEM)
    small_sum = _flat_unpack(_sum_leading(gathered, "small_grads_sum"), [small_full[n].shape for n in SMALL])
    for n, a in zip(SMALL, small_sum):
        g[n] = lax.dynamic_slice_in_dim(a, chip * LANE, LANE, axis=2) if n in CONV_NAMES else a

    delta, new_m, new_v = {}, {}, {}
    for n in BIG:
        shp = p[n].shape
        two_d = lambda a: a.reshape(-1, shp[-1])
        d, nm, nv = _adamw(two_d(p[n]), two_d(g[n]), two_d(mom[n]), two_d(var[n]), "adamw_" + n)
        delta[n], new_m[n], new_v[n] = d.reshape(shp), nm.reshape(shp), nv.reshape(shp)
    shapes = [p[n].shape for n in SMALL]
    d, nm, nv = _adamw(_flat_pack([p[n] for n in SMALL]), _flat_pack([g[n] for n in SMALL]),
                       _flat_pack([mom[n] for n in SMALL]), _flat_pack([var[n] for n in SMALL]), "adamw_small")
    for n, a, b, cval in zip(SMALL, _flat_unpack(d, shapes), _flat_unpack(nm, shapes), _flat_unpack(nv, shapes)):
        delta[n], new_m[n], new_v[n] = a, b, cval

    return (loss, grad_x, *[g[n] for n in NAMES], *[delta[n] for n in NAMES], *[new_m[n] for n in NAMES],
            *[new_v[n] for n in NAMES])
```

```python
import functools
import math

import numpy as np
import jax
import jax.numpy as jnp
from jax import lax
from jax.experimental import pallas as pl
from jax.experimental.pallas import tpu as pltpu

F32 = jnp.float32
BF16 = jnp.bfloat16
MESH = pl.DeviceIdType.MESH

D_MODEL = 1024
DEPTH = 2
BW = 512
HEAD_DIM = 64
N_Q = 8
N_KV = 2
BLK = 128
D_FF = 2816
IN_W = 8448
EPS = 1e-6
NEG_INF = -1e30
LRU_C = 8.0
CONV_A, CONV_B, CONV_D = 4, 3, 31
LANE = 128
ROW_TILE = 256
VMEM_LIMIT = 56 * 1024 * 1024
MM_VMEM_BUDGET = 36 * 1024 * 1024

C_AX, C_AG, C_BV, C_BC, C_BB = 0, 4, 8, 12, 16
OFF_Q, OFF_K, OFF_V = 2560, 3072, 3200
C_D1, C_D2 = 26, 30
OFF_GL = 4352

ADAM_LR, ADAM_B1, ADAM_B2, ADAM_EPS, ADAM_WD, ADAM_STEP = 0.001, 0.9, 0.999, 1e-08, 0.01, 10

ARENA_W = 1024
R_DOWN, R_GATE, R_UP, R_IN, R_O, R_OUT = 0, 2816, 5632, 8448, 16896, 17920
ARENA_ROWS = 19968
ROW_REGIONS = ((R_DOWN, 704), (R_GATE, 704), (R_UP, 704), (R_IN, 2112), (R_O, 256))
PIECE_OFF = (0, 704, 1408, 2112, 4224)
PIECE_ROWS = 4480
OUT_ROWS, OUT_COLS = 4 * BW, D_MODEL // 4

BIG = ("w_in", "w_a_out", "w_b_out", "w_c_out", "w_d_out", "w_o", "w_ffn_gate", "w_ffn_up", "w_ffn_down")
CONV_NAMES = ("conv_a_w", "conv_b_w", "conv_d_w")
N_TAPS = CONV_A + CONV_B + CONV_D
SMALL = ("norm1_g", "conv_a_w", "conv_a_b", "lru_wx", "lru_bx", "lru_wa", "lru_ba", "lru_lambda", "conv_b_w",
         "sinks", "conv_d_w", "conv_d_b", "ln_d_g", "ln_d_b", "norm2_g", "final_g")
NAMES = ['norm1_g', 'w_in', 'conv_a_w', 'conv_a_b', 'lru_wx', 'lru_bx', 'lru_wa', 'lru_ba', 'lru_lambda', 'w_a_out',
         'conv_b_w', 'w_b_out', 'sinks', 'w_c_out', 'conv_d_w', 'conv_d_b', 'ln_d_g', 'ln_d_b', 'w_d_out', 'w_o',
         'norm2_g', 'w_ffn_gate', 'w_ffn_up', 'w_ffn_down', 'final_g']


def _pick(n, cands, off=0):
    for c in cands:
        if n % c == 0 and off % c == 0:
            return c
    assert off == 0, (n, off)
    return n


class Win:
    def __init__(self, arena, l, off, rows):
        self.arena, self.l, self.off, self.rows = arena, l, off, rows
        self.shape = (rows, arena.shape[-1])


def _params(sem=None):
    return pltpu.CompilerParams(dimension_semantics=sem, vmem_limit_bytes=VMEM_LIMIT)


def _sig(z):
    return 1.0 / (1.0 + jnp.exp(-z))


def _dot(a, b, dims):
    return lax.dot_general(a.astype(BF16), b.astype(BF16), (dims, ((), ())), preferred_element_type=F32)


NN = ((1,), (0,))
NT = ((1,), (1,))
TN = ((0,), (0,))


def _mm(a, b, mode, name, out_dtype=F32, add=None, out=None):
    if mode == "nn":
        (m, k), n = a.shape, b.shape[1]
    elif mode == "nt":
        (m, k), n = a.shape, b.shape[0]
    else:
        (k, m), n = a.shape, b.shape[1]
    b_win = isinstance(b, Win)
    b_off = b.off if b_win else 0
    o_off = out.off if out is not None else 0
    if out is not None:
        out_dtype = out.arena.dtype
    tk = _pick(k, (2816, 2048, 1408, 1024, 768, 512, 256), b_off if mode != "nt" else 0)
    nk = k // tk
    n_off = b_off if mode == "nt" else 0
    a_bytes, b_bytes, o_bytes = a.dtype.itemsize, 2, jnp.dtype(out_dtype).itemsize

    def vmem_bytes(tm_, tn_):
        tile = tm_ * tn_
        return (2 * tk * (tm_ * a_bytes + tn_ * b_bytes) + 2 * tile * o_bytes + (tile * 4 if nk > 1 else 0)
                + (2 * tile * 4 if add is not None else 0) + tile * 4)

    pairs = [(tm_, tn_) for tm_ in (2048, 1024, 768, 512, 256, 128) for tn_ in (1024, 768, 512, 256, 128)
             if m % tm_ == 0 and o_off % tm_ == 0 and n % tn_ == 0 and n_off % tn_ == 0
             and vmem_bytes(tm_, tn_) <= MM_VMEM_BUDGET]
    tm, tn = max(pairs, key=lambda p: (p[0] * p[1], p[0]))
    dims = {"nn": NN, "nt": NT, "tn": TN}[mode]

    def body(*refs):
        a_ref, b_ref = refs[:2]
        c_ref = refs[2] if add is not None else None
        if nk == 1:
            r = _dot(a_ref[...], b_ref[...], dims)
            if add is not None:
                r = r + c_ref[...]
            refs[-1][...] = r.astype(out_dtype)
            return
        o_ref, acc = refs[-2:]
        kk = pl.program_id(2)

        @pl.when(kk == 0)
        def _():
            acc[...] = jnp.zeros_like(acc)

        acc[...] += _dot(a_ref[...], b_ref[...], dims)

        @pl.when(kk == nk - 1)
        def _():
            r = acc[...]
            if add is not None:
                r = r + c_ref[...]
            o_ref[...] = r.astype(out_dtype)

    if mode == "tn":
        a_spec = pl.BlockSpec((tk, tm), lambda i, j, q: (q, i))
    else:
        a_spec = pl.BlockSpec((tm, tk), lambda i, j, q: (i, q))
    if mode == "nt":
        b_blk, b_idx = (tn, tk), (lambda i, j, q: (b_off // tn + j, q))
    else:
        b_blk, b_idx = (tk, tn), (lambda i, j, q: (b_off // tk + q, j))
    if b_win and b.arena.ndim == 3:
        bl = b.l
        b_spec = pl.BlockSpec((None,) + b_blk, lambda i, j, q: (bl,) + b_idx(i, j, q))
    else:
        b_spec = pl.BlockSpec(b_blk, b_idx)
    plain_o = pl.BlockSpec((tm, tn), lambda i, j, q: (i, j))
    in_specs = [a_spec, b_spec] + ([plain_o] if add is not None else [])
    args = (a, b.arena if b_win else b) + ((add,) if add is not None else ())
    aliases = {}
    if out is None:
        o_spec, o_shape = plain_o, jax.ShapeDtypeStruct((m, n), out_dtype)
    else:
        ol = out.l
        if out.arena.ndim == 3:
            o_spec = pl.BlockSpec((None, tm, tn), lambda i, j, q: (ol, o_off // tm + i, j))
        else:
            o_spec = pl.BlockSpec((tm, tn), lambda i, j, q: (o_off // tm + i, j))
        o_shape = jax.ShapeDtypeStruct(out.arena.shape, out_dtype)
        aliases = {len(args): 0}
        in_specs.append(pl.BlockSpec(memory_space=pl.ANY))
        args = args + (out.arena,)
    return pl.pallas_call(
        body, name=name, out_shape=o_shape,
        grid=(m // tm, n // tn, nk), in_specs=in_specs, out_specs=o_spec,
        scratch_shapes=[pltpu.VMEM((tm, tn), F32)] if nk > 1 else [], input_output_aliases=aliases,
        compiler_params=_params(("parallel", "parallel", "arbitrary")),
    )(*args)


def _row_spec(cols, tr=ROW_TILE):
    return pl.BlockSpec((tr, cols), lambda i: (i, 0))


def _vec_spec(cols):
    return pl.BlockSpec((1, cols), lambda i: (0, 0))


def _rms_fwd(x, g, name):
    t, d = x.shape

    def body(x_ref, g_ref, o_ref):
        xv = x_ref[...]
        r = lax.rsqrt(jnp.mean(xv * xv, axis=1, keepdims=True) + EPS)
        o_ref[...] = (xv * r * g_ref[...]).astype(BF16)

    return pl.pallas_call(
        body, name=name, out_shape=jax.ShapeDtypeStruct((t, d), BF16), grid=(t // ROW_TILE,),
        in_specs=[_row_spec(d), _vec_spec(d)], out_specs=_row_spec(d), compiler_params=_params(("parallel",)),
    )(x, g)


def _rms_bwd(x, g, dxn, dres, name):
    t, d = x.shape

    def body(x_ref, g_ref, dy_ref, dr_ref, dx_ref, dg_ref):
        @pl.when(pl.program_id(0) == 0)
        def _():
            dg_ref[...] = jnp.zeros_like(dg_ref)

        xv = x_ref[...]
        dy = dy_ref[...]
        r = lax.rsqrt(jnp.mean(xv * xv, axis=1, keepdims=True) + EPS)
        w = dy * g_ref[...]
        dx_ref[...] = dr_ref[...] + r * w - xv * (r * r * r) * jnp.mean(w * xv, axis=1, keepdims=True)
        dg_ref[...] += jnp.sum(dy * xv * r, axis=0, keepdims=True)

    return pl.pallas_call(
        body, name=name,
        out_shape=(jax.ShapeDtypeStruct((t, d), F32), jax.ShapeDtypeStruct((1, d), F32)), grid=(t // ROW_TILE,),
        in_specs=[_row_spec(d), _vec_spec(d), _row_spec(d), _row_spec(d)], out_specs=(_row_spec(d), _vec_spec(d)),
        compiler_params=_params(("arbitrary",)),
    )(x, g, dxn, dres)


def _loss_head(x, g, tgt, name):
    t, d = x.shape

    def body(x_ref, g_ref, t_ref, loss_ref, dx_ref, dg_ref):
        @pl.when(pl.program_id(0) == 0)
        def _():
            dg_ref[...] = jnp.zeros_like(dg_ref)
            loss_ref[...] = jnp.zeros_like(loss_ref)

        xv = x_ref[...]
        gv = g_ref[...]
        r = lax.rsqrt(jnp.mean(xv * xv, axis=1, keepdims=True) + EPS)
        e = xv * r * gv - t_ref[...]
        loss_ref[...] += jnp.full(loss_ref.shape, (0.5 / d) * jnp.sum(e * e), F32)
        dy = e * (1.0 / d)
        w = dy * gv
        dx_ref[...] = r * w - xv * (r * r * r) * jnp.mean(w * xv, axis=1, keepdims=True)
        dg_ref[...] += jnp.sum(dy * xv * r, axis=0, keepdims=True)

    return pl.pallas_call(
        body, name=name,
        out_shape=(jax.ShapeDtypeStruct((1, LANE), F32), jax.ShapeDtypeStruct((t, d), F32),
                   jax.ShapeDtypeStruct((1, d), F32)),
        grid=(t // ROW_TILE,), in_specs=[_row_spec(d), _vec_spec(d), _row_spec(d)],
        out_specs=(_vec_spec(LANE), _row_spec(d), _vec_spec(d)), compiler_params=_params(("arbitrary",)),
    )(x, g, tgt)


def _swiglu_fwd(gg, uu, name):
    t, f = gg.shape

    def body(g_ref, u_ref, o_ref):
        gv = g_ref[...]
        o_ref[...] = (gv * _sig(gv) * u_ref[...]).astype(BF16)

    return pl.pallas_call(
        body, name=name, out_shape=jax.ShapeDtypeStruct((t, f), BF16), grid=(t // ROW_TILE,),
        in_specs=[_row_spec(f), _row_spec(f)], out_specs=_row_spec(f), compiler_params=_params(("parallel",)),
    )(gg, uu)


def _swiglu_bwd(gg, uu, dact, name):
    t, f = gg.shape

    def body(g_ref, u_ref, d_ref, dg_ref, du_ref):
        gv = g_ref[...]
        dv = d_ref[...]
        s = _sig(gv)
        dg_ref[...] = (dv * u_ref[...] * s * (1.0 + gv * (1.0 - s))).astype(BF16)
        du_ref[...] = (dv * gv * s).astype(BF16)

    return pl.pallas_call(
        body, name=name,
        out_shape=(jax.ShapeDtypeStruct((t, f), BF16), jax.ShapeDtypeStruct((t, f), BF16)), grid=(t // ROW_TILE,),
        in_specs=[_row_spec(f)] * 3, out_specs=(_row_spec(f), _row_spec(f)), compiler_params=_params(("parallel",)),
    )(gg, uu, dact)


MERGE_COLS = 256


def _gate_specs():
    nb = D_MODEL // MERGE_COLS
    base = OFF_GL // MERGE_COLS
    return [pl.BlockSpec((ROW_TILE, MERGE_COLS), functools.partial(lambda i, j, kk: (i, base + nb * kk + j), kk=kk))
            for kk in range(4)]


def _merge_fwd(proj, ys, name):
    t = proj.shape[0]
    yspec = pl.BlockSpec((ROW_TILE, MERGE_COLS), lambda i, j: (i, j))

    def body(g0, g1, g2, g3, y0, y1, y2, y3, o_ref):
        acc = _sig(g0[...]) * y0[...]
        acc += _sig(g1[...]) * y1[...]
        acc += _sig(g2[...]) * y2[...]
        acc += _sig(g3[...]) * y3[...]
        o_ref[...] = acc.astype(BF16)

    return pl.pallas_call(
        body, name=name, out_shape=jax.ShapeDtypeStruct((t, D_MODEL), BF16),
        grid=(t // ROW_TILE, D_MODEL // MERGE_COLS), in_specs=_gate_specs() + [yspec] * 4, out_specs=yspec,
        compiler_params=_params(("parallel", "parallel")),
    )(proj, proj, proj, proj, *ys)


def _merge_bwd(proj, ys, dmerged, name):
    t = proj.shape[0]
    yspec = pl.BlockSpec((ROW_TILE, MERGE_COLS), lambda i, j: (i, j))

    def body(g0, g1, g2, g3, y0, y1, y2, y3, dm_ref, *outs):
        dm = dm_ref[...]
        for gr, yr, dy_ref, dg_ref in zip((g0, g1, g2, g3), (y0, y1, y2, y3), outs[:4], outs[4:]):
            s = _sig(gr[...])
            dy_ref[...] = (dm * s).astype(BF16)
            dg_ref[...] = (dm * yr[...] * s * (1.0 - s)).astype(BF16)

    shp = jax.ShapeDtypeStruct((t, D_MODEL), BF16)
    outs = pl.pallas_call(
        body, name=name, out_shape=(shp,) * 8, grid=(t // ROW_TILE, D_MODEL // MERGE_COLS),
        in_specs=_gate_specs() + [yspec] * 5, out_specs=(yspec,) * 8, compiler_params=_params(("parallel", "parallel")),
    )(proj, proj, proj, proj, *ys, dmerged)
    return outs[:4], outs[4:]


def _ln_silu_fwd(cd, g, b, name):
    t, c = cd.shape

    def body(x_ref, g_ref, b_ref, o_ref):
        xv = x_ref[...]
        mu = jnp.mean(xv, axis=1, keepdims=True)
        xc = xv - mu
        rs = lax.rsqrt(jnp.mean(xc * xc, axis=1, keepdims=True) + EPS)
        z = xc * rs * g_ref[...] + b_ref[...]
        o_ref[...] = (z * _sig(z)).astype(BF16)

    return pl.pallas_call(
        body, name=name, out_shape=jax.ShapeDtypeStruct((t, c), BF16), grid=(t // ROW_TILE,),
        in_specs=[_row_spec(c), _vec_spec(c), _vec_spec(c)], out_specs=_row_spec(c),
        compiler_params=_params(("parallel",)),
    )(cd, g, b)


def _ln_silu_bwd(cd, g, b, dy, name):
    t, c = cd.shape

    def body(x_ref, g_ref, b_ref, dy_ref, dx_ref, dg_ref, db_ref):
        @pl.when(pl.program_id(0) == 0)
        def _():
            dg_ref[...] = jnp.zeros_like(dg_ref)
            db_ref[...] = jnp.zeros_like(db_ref)

        xv = x_ref[...]
        gv = g_ref[...]
        mu = jnp.mean(xv, axis=1, keepdims=True)
        xc = xv - mu
        rs = lax.rsqrt(jnp.mean(xc * xc, axis=1, keepdims=True) + EPS)
        xh = xc * rs
        z = xh * gv + b_ref[...]
        s = _sig(z)
        dz = dy_ref[...] * s * (1.0 + z * (1.0 - s))
        dg_ref[...] += jnp.sum(dz * xh, axis=0, keepdims=True)
        db_ref[...] += jnp.sum(dz, axis=0, keepdims=True)
        dxh = dz * gv
        dx_ref[...] = rs * (dxh - jnp.mean(dxh, axis=1, keepdims=True) - xh * jnp.mean(dxh * xh, axis=1, keepdims=True))

    return pl.pallas_call(
        body, name=name,
        out_shape=(jax.ShapeDtypeStruct((t, c), F32), jax.ShapeDtypeStruct((1, c), F32),
                   jax.ShapeDtypeStruct((1, c), F32)),
        grid=(t // ROW_TILE,), in_specs=[_row_spec(c), _vec_spec(c), _vec_spec(c), _row_spec(c)],
        out_specs=(_row_spec(c), _vec_spec(c), _vec_spec(c)), compiler_params=_params(("arbitrary",)),
    )(cd, g, b, dy)


def _shift_dn(x, k):
    if k == 0:
        return x
    row = lax.broadcasted_iota(jnp.int32, x.shape, 0)
    return jnp.where(row >= k, pltpu.roll(x, k, 0), 0.0)


def _shift_up(x, k):
    if k == 0:
        return x
    t = x.shape[0]
    row = lax.broadcasted_iota(jnp.int32, x.shape, 0)
    return jnp.where(row < t - k, pltpu.roll(x, t - k, 0), 0.0)


def _conv_fwd(x, w_ref, taps):
    acc = w_ref[pl.ds(taps - 1, 1), :] * x
    for k in range(taps - 1):
        acc += w_ref[pl.ds(k, 1), :] * _shift_dn(x, taps - 1 - k)
    return acc


def _conv_bwd(x, dy, w_ref, dw_ref, taps):
    dx = w_ref[pl.ds(taps - 1, 1), :] * dy
    dw_ref[pl.ds(taps - 1, 1), :] = jnp.sum(dy * x, axis=0, keepdims=True)
    for k in range(taps - 1):
        s = taps - 1 - k
        dx += w_ref[pl.ds(k, 1), :] * _shift_up(dy, s)
        dw_ref[pl.ds(k, 1), :] = jnp.sum(dy * _shift_dn(x, s), axis=0, keepdims=True)
    return dx


def _scan_fwd(a, u):
    t = a.shape[0]
    k = 1
    while k < t:
        u = u + a * _shift_dn(u, k)
        if 2 * k < t:
            a = a * _shift_dn(a, k)
        k *= 2
    return u


def _scan_rev(a, u):
    t = a.shape[0]
    k = 1
    while k < t:
        u = u + a * _shift_up(u, k)
        if 2 * k < t:
            a = a * _shift_up(a, k)
        k *= 2
    return u


def _one_minus_exp(y):
    return jnp.where(y > -1e-3, -(y + 0.5 * y * y + (1.0 / 6.0) * y * y * y), 1.0 - jnp.exp(y))


GELU_C = math.sqrt(2.0 / math.pi)


def _gelu(x):
    th = jnp.tanh(GELU_C * (x + 0.044715 * x * x * x))
    return 0.5 * x * (1.0 + th), th


def _softplus(x):
    return jnp.maximum(x, 0.0) + jnp.log(1.0 + jnp.exp(-jnp.abs(x)))


def _chunk_spec(t, blk0):
    return pl.BlockSpec((t, LANE), functools.partial(lambda c, b: (0, b + c), b=blk0))


def _tap_spec(taps):
    return pl.BlockSpec((taps, LANE), lambda c: (0, c))


def _cvec_spec():
    return pl.BlockSpec((1, LANE), lambda c: (0, c))


def _cmat_spec():
    return pl.BlockSpec((1, LANE, LANE), lambda c: (c, 0, 0))


def _lru_forward(ax, wA_ref, bA_ref, wx_ref, bx_ref, wa_ref, ba_ref, lam_ref):
    ca = _conv_fwd(ax, wA_ref, CONV_A) + bA_ref[...]
    gi = _sig(_dot(ca, wx_ref[0], NN) + bx_ref[...])
    gr = _sig(_dot(ca, wa_ref[0], NN) + ba_ref[...])
    sp = _softplus(-lam_ref[...])
    la = -LRU_C * gr * sp
    a = jnp.exp(la)
    mult = jnp.sqrt(_one_minus_exp(2.0 * la))
    h = _scan_fwd(a, ca * gi * mult)
    return ca, gi, gr, sp, a, mult, h


def _a_fwd(proj, wA, bA, wx, bx, wa, ba, lam, name):
    t = proj.shape[0]

    def body(ax_ref, ag_ref, wA_ref, bA_ref, wx_ref, bx_ref, wa_ref, ba_ref, lam_ref, o_ref):
        h = _lru_forward(ax_ref[...], wA_ref, bA_ref, wx_ref, bx_ref, wa_ref, ba_ref, lam_ref)[-1]
        o_ref[...] = (h * _gelu(ag_ref[...])[0]).astype(BF16)

    return pl.pallas_call(
        body, name=name, out_shape=jax.ShapeDtypeStruct((t, BW), BF16), grid=(BW // LANE,),
        in_specs=[_chunk_spec(t, C_AX), _chunk_spec(t, C_AG), _tap_spec(CONV_A), _cvec_spec(), _cmat_spec(),
                  _cvec_spec(), _cmat_spec(), _cvec_spec(), _cvec_spec()],
        out_specs=_chunk_spec(t, 0), compiler_params=_params(("parallel",)),
    )(proj, proj, wA, bA, wx, bx, wa, ba, lam)


def _a_bwd(proj, dya, wA, bA, wx, bx, wa, ba, lam, name):
    t = proj.shape[0]

    def body(ax_ref, ag_ref, dy_ref, wA_ref, bA_ref, wx_ref, bx_ref, wa_ref, ba_ref, lam_ref,
             dax_ref, dag_ref, dwA_ref, dbA_ref, dwx_ref, dbx_ref, dwa_ref, dba_ref, dlam_ref):
        ax = ax_ref[...]
        ag = ag_ref[...]
        dy = dy_ref[...]
        ca, gi, gr, sp, a, mult, h = _lru_forward(ax, wA_ref, bA_ref, wx_ref, bx_ref, wa_ref, ba_ref, lam_ref)
        gel, th = _gelu(ag)
        dgel = 0.5 * (1.0 + th) + 0.5 * ag * (1.0 - th * th) * GELU_C * (1.0 + 3.0 * 0.044715 * ag * ag)
        dag_ref[...] = (dy * h * dgel).astype(BF16)
        s = _scan_rev(_shift_up(a, 1), dy * gel)
        da = s * _shift_dn(h, 1)
        dca = s * gi * mult
        dgi = s * ca * mult
        dmult = s * ca * gi
        dla = da * a - dmult * a * a / mult
        dgr = dla * (-LRU_C * sp)
        dsp = jnp.sum(dla * (-LRU_C * gr), axis=0, keepdims=True)
        dlam_ref[...] = -_sig(-lam_ref[...]) * dsp
        dzi = dgi * gi * (1.0 - gi)
        dzr = dgr * gr * (1.0 - gr)
        dbx_ref[...] = jnp.sum(dzi, axis=0, keepdims=True)
        dba_ref[...] = jnp.sum(dzr, axis=0, keepdims=True)
        dwx_ref[0] = _dot(ca, dzi, TN)
        dwa_ref[0] = _dot(ca, dzr, TN)
        dca += _dot(dzi, wx_ref[0], NT) + _dot(dzr, wa_ref[0], NT)
        dbA_ref[...] = jnp.sum(dca, axis=0, keepdims=True)
        dax_ref[...] = _conv_bwd(ax, dca, wA_ref, dwA_ref, CONV_A).astype(BF16)

    big = jax.ShapeDtypeStruct((t, BW), BF16)
    vec = jax.ShapeDtypeStruct((1, BW), F32)
    mat = jax.ShapeDtypeStruct((BW // LANE, LANE, LANE), F32)
    return pl.pallas_call(
        body, name=name,
        out_shape=(big, big, jax.ShapeDtypeStruct((CONV_A, BW), F32), vec, mat, vec, mat, vec, vec),
        grid=(BW // LANE,),
        in_specs=[_chunk_spec(t, C_AX), _chunk_spec(t, C_AG), _chunk_spec(t, 0), _tap_spec(CONV_A), _cvec_spec(),
                  _cmat_spec(), _cvec_spec(), _cmat_spec(), _cvec_spec(), _cvec_spec()],
        out_specs=(_chunk_spec(t, 0), _chunk_spec(t, 0), _tap_spec(CONV_A), _cvec_spec(), _cmat_spec(), _cvec_spec(),
                   _cmat_spec(), _cvec_spec(), _cvec_spec()),
        compiler_params=_params(("parallel",)),
    )(proj, proj, dya, wA, bA, wx, bx, wa, ba, lam)


def _b_fwd(proj, wB, name):
    t = proj.shape[0]

    def body(bv_ref, bc_ref, bb_ref, w_ref, o_ref):
        o_ref[...] = (bb_ref[...] * _conv_fwd(bc_ref[...] * bv_ref[...], w_ref, CONV_B)).astype(BF16)

    return pl.pallas_call(
        body, name=name, out_shape=jax.ShapeDtypeStruct((t, BW), BF16), grid=(BW // LANE,),
        in_specs=[_chunk_spec(t, C_BV), _chunk_spec(t, C_BC), _chunk_spec(t, C_BB), _tap_spec(CONV_B)],
        out_specs=_chunk_spec(t, 0), compiler_params=_params(("parallel",)),
    )(proj, proj, proj, wB)


def _b_bwd(proj, dyb, wB, name):
    t = proj.shape[0]

    def body(bv_ref, bc_ref, bb_ref, dy_ref, w_ref, dbv_ref, dbc_ref, dbb_ref, dw_ref):
        bv = bv_ref[...]
        bc = bc_ref[...]
        dy = dy_ref[...]
        p = bc * bv
        dbb_ref[...] = (dy * _conv_fwd(p, w_ref, CONV_B)).astype(BF16)
        dp = _conv_bwd(p, dy * bb_ref[...], w_ref, dw_ref, CONV_B)
        dbc_ref[...] = (dp * bv).astype(BF16)
        dbv_ref[...] = (dp * bc).astype(BF16)

    big = jax.ShapeDtypeStruct((t, BW), BF16)
    return pl.pallas_call(
        body, name=name, out_shape=(big, big, big, jax.ShapeDtypeStruct((CONV_B, BW), F32)), grid=(BW // LANE,),
        in_specs=[_chunk_spec(t, C_BV), _chunk_spec(t, C_BC), _chunk_spec(t, C_BB), _chunk_spec(t, 0),
                  _tap_spec(CONV_B)],
        out_specs=(_chunk_spec(t, 0),) * 3 + (_tap_spec(CONV_B),), compiler_params=_params(("parallel",)),
    )(proj, proj, proj, dyb, wB)


def _d_conv_fwd(proj, wD, bD, name):
    t = proj.shape[0]

    def body(d1_ref, d2_ref, w_ref, b_ref, o_ref):
        o_ref[...] = _conv_fwd(d1_ref[...] * _sig(d2_ref[...]), w_ref, CONV_D) + b_ref[...]

    return pl.pallas_call(
        body, name=name, out_shape=jax.ShapeDtypeStruct((t, BW), F32), grid=(BW // LANE,),
        in_specs=[_chunk_spec(t, C_D1), _chunk_spec(t, C_D2), _tap_spec(CONV_D), _cvec_spec()],
        out_specs=_chunk_spec(t, 0), compiler_params=_params(("parallel",)),
    )(proj, proj, wD, bD)


def _d_conv_bwd(proj, dcd, wD, name):
    t = proj.shape[0]

    def body(d1_ref, d2_ref, dy_ref, w_ref, dd1_ref, dd2_ref, dw_ref, db_ref):
        d1 = d1_ref[...]
        s = _sig(d2_ref[...])
        dy = dy_ref[...]
        db_ref[...] = jnp.sum(dy, axis=0, keepdims=True)
        dd = _conv_bwd(d1 * s, dy, w_ref, dw_ref, CONV_D)
        dd1_ref[...] = (dd * s).astype(BF16)
        dd2_ref[...] = (dd * d1 * s * (1.0 - s)).astype(BF16)

    big = jax.ShapeDtypeStruct((t, BW), BF16)
    return pl.pallas_call(
        body, name=name,
        out_shape=(big, big, jax.ShapeDtypeStruct((CONV_D, BW), F32), jax.ShapeDtypeStruct((1, BW), F32)),
        grid=(BW // LANE,),
        in_specs=[_chunk_spec(t, C_D1), _chunk_spec(t, C_D2), _chunk_spec(t, 0), _tap_spec(CONV_D)],
        out_specs=(_chunk_spec(t, 0), _chunk_spec(t, 0), _tap_spec(CONV_D), _cvec_spec()),
        compiler_params=_params(("parallel",)),
    )(proj, proj, dcd, wD)


SCALE = HEAD_DIM ** -0.5
GROUP = N_Q // N_KV


def _attn_probs(q_ref, k_ref, ss_ref, h, n):
    qi = lax.broadcasted_iota(jnp.int32, (BLK, BLK), 0)
    ki = lax.broadcasted_iota(jnp.int32, (BLK, BLK), 1)
    dist = (qi - ki).astype(F32)
    sink = ss_ref[0, h]
    slope = ss_ref[1, h]
    s0 = pl.multiple_of(n * BLK, BLK)
    sp = pl.multiple_of(jnp.maximum(n - 1, 0) * BLK, BLK)
    q = q_ref[0, pl.ds(s0, BLK), :]
    kc = k_ref[0, pl.ds(s0, BLK), :]
    kp = k_ref[0, pl.ds(sp, BLK), :]
    sc = jnp.where(ki <= qi, _dot(q, kc, NT) * SCALE - slope * dist, NEG_INF)
    first = jnp.where(n >= 1, 0, BLK)
    sv = jnp.where(ki > qi + first, _dot(q, kp, NT) * SCALE - slope * (dist + BLK), NEG_INF)
    m = jnp.maximum(jnp.maximum(jnp.max(sc, axis=1, keepdims=True), jnp.max(sv, axis=1, keepdims=True)), sink)
    pc = jnp.exp(sc - m)
    pp = jnp.exp(sv - m)
    ps = jnp.exp(sink - m)
    z = jnp.sum(pc, axis=1, keepdims=True) + jnp.sum(pp, axis=1, keepdims=True) + ps
    return s0, sp, q, kc, kp, pc, pp, ps, z


def _attn_specs(t):
    qs = pl.BlockSpec((1, t, HEAD_DIM), lambda h: (h, 0, 0))
    ks = pl.BlockSpec((1, t, HEAD_DIM), lambda h: (h // GROUP, 0, 0))
    ss = pl.BlockSpec(memory_space=pltpu.SMEM)
    return qs, ks, ss


def _attn_fwd(q, k, v, ss, name):
    t = q.shape[1]
    qs, ks, sspec = _attn_specs(t)

    def body(q_ref, k_ref, v_ref, ss_ref, o_ref):
        h = pl.program_id(0)

        def blk(n, carry):
            s0, sp, _, _, _, pc, pp, _, z = _attn_probs(q_ref, k_ref, ss_ref, h, n)
            o = _dot(pc, v_ref[0, pl.ds(s0, BLK), :], NN) + _dot(pp, v_ref[0, pl.ds(sp, BLK), :], NN)
            o_ref[0, pl.ds(s0, BLK), :] = (o / z).astype(BF16)
            return carry

        lax.fori_loop(0, t // BLK, blk, 0)

    return pl.pallas_call(
        body, name=name, out_shape=jax.ShapeDtypeStruct((N_Q, t, HEAD_DIM), BF16), grid=(N_Q,),
        in_specs=[qs, ks, ks, sspec], out_specs=qs, compiler_params=_params(("parallel",)),
    )(q, k, v, ss)


def _attn_bwd(q, k, v, do, ss, name):
    t = q.shape[1]
    qs, ks, sspec = _attn_specs(t)

    def body(q_ref, k_ref, v_ref, do_ref, ss_ref, dq_ref, dk_ref, dv_ref, ds_ref):
        h = pl.program_id(0)

        @pl.when(h % GROUP == 0)
        def _():
            dk_ref[...] = jnp.zeros_like(dk_ref)
            dv_ref[...] = jnp.zeros_like(dv_ref)

        def blk(n, dsink):
            s0, sp, q, kc, kp, pc, pp, ps, z = _attn_probs(q_ref, k_ref, ss_ref, h, n)
            rz = 1.0 / z
            pc = pc * rz
            pp = pp * rz
            do_b = do_ref[0, pl.ds(s0, BLK), :]
            dpc = _dot(do_b, v_ref[0, pl.ds(s0, BLK), :], NT)
            dpp = _dot(do_b, v_ref[0, pl.ds(sp, BLK), :], NT)
            delta = jnp.sum(pc * dpc, axis=1, keepdims=True) + jnp.sum(pp * dpp, axis=1, keepdims=True)
            dsc = pc * (dpc - delta)
            dsp = pp * (dpp - delta)
            dq_ref[0, pl.ds(s0, BLK), :] = ((_dot(dsc, kc, NN) + _dot(dsp, kp, NN)) * SCALE).astype(BF16)
            dk_ref[0, pl.ds(s0, BLK), :] += _dot(dsc, q, TN) * SCALE
            dk_ref[0, pl.ds(sp, BLK), :] += _dot(dsp, q, TN) * SCALE
            dv_ref[0, pl.ds(s0, BLK), :] += _dot(pc, do_b, TN)
            dv_ref[0, pl.ds(sp, BLK), :] += _dot(pp, do_b, TN)
            return dsink - ps * rz * delta

        dsink = lax.fori_loop(0, t // BLK, blk, jnp.zeros((BLK, 1), F32))
        ds_ref[...] = jnp.full(ds_ref.shape, jnp.sum(dsink), F32)

    kv = jax.ShapeDtypeStruct((N_KV, t, HEAD_DIM), F32)
    return pl.pallas_call(
        body, name=name,
        out_shape=(jax.ShapeDtypeStruct((N_Q, t, HEAD_DIM), BF16), kv, kv, jax.ShapeDtypeStruct((N_Q, 8, LANE), F32)),
        grid=(N_Q,), in_specs=[qs, ks, ks, qs, sspec],
        out_specs=(qs, ks, ks, pl.BlockSpec((1, 8, LANE), lambda h: (h, 0, 0))),
        compiler_params=_params(("arbitrary",)),
    )(q, k, v, do, ss)


def _heads(x2d, n):
    t = x2d.shape[0]
    return x2d.reshape(t, n, HEAD_DIM).transpose(1, 0, 2)


def _unheads(x3d):
    n, t, _ = x3d.shape
    return x3d.transpose(1, 0, 2).reshape(t, n * HEAD_DIM)


SMALL_ELEMS = 256 * 1024
TILE_ELEMS = 640 * 1024


def _row_tile(r, c):
    if r * c <= SMALL_ELEMS:
        return r
    return _pick(r, [t for t in (512, 256, 128, 64, 32, 16, 8) if t * c <= TILE_ELEMS])


def _adamw(w, g, m, v, name):
    r, c = w.shape
    tr = _row_tile(r, c)
    spec = pl.BlockSpec((tr, c), lambda i: (i, 0))

    def body(w_ref, g_ref, m_ref, v_ref, d_ref, nm_ref, nv_ref):
        gv = g_ref[...]
        nm = ADAM_B1 * m_ref[...] + (1.0 - ADAM_B1) * gv
        nv = ADAM_B2 * v_ref[...] + (1.0 - ADAM_B2) * (gv * gv)
        m_hat = nm / (1.0 - ADAM_B1 ** ADAM_STEP)
        v_hat = nv / (1.0 - ADAM_B2 ** ADAM_STEP)
        d_ref[...] = -ADAM_LR * (m_hat / (jnp.sqrt(v_hat) + ADAM_EPS) + ADAM_WD * w_ref[...])
        nm_ref[...] = nm
        nv_ref[...] = nv

    shp = jax.ShapeDtypeStruct((r, c), F32)
    return pl.pallas_call(
        body, name=name, out_shape=(shp, shp, shp), grid=(r // tr,), in_specs=[spec] * 4, out_specs=(spec,) * 3,
        compiler_params=_params(("parallel",)),
    )(w, g, m, v)


def _sum_leading(x, name):
    n, r, c = x.shape
    tr = _row_tile(r, c)

    def body(x_ref, o_ref):
        acc = x_ref[0]
        for i in range(1, n):
            acc = acc + x_ref[i]
        o_ref[...] = acc

    return pl.pallas_call(
        body, name=name, out_shape=jax.ShapeDtypeStruct((r, c), F32), grid=(r // tr,),
        in_specs=[pl.BlockSpec((n, tr, c), lambda i: (0, i, 0))], out_specs=pl.BlockSpec((tr, c), lambda i: (i, 0)),
        compiler_params=_params(("parallel",)),
    )(x)


def _sum_own_plus(p, sel, recv, name, out_dtype):
    _, r, c = p.shape
    n = recv.shape[0]
    tr = _pick(r, (512, 448, 256, 128, 64, 16))

    def body(sel_ref, p_ref, r_ref, o_ref):
        acc = p_ref[0].astype(F32)
        for i in range(n):
            acc = acc + r_ref[i].astype(F32)
        o_ref[...] = acc.astype(out_dtype)

    grid_spec = pltpu.PrefetchScalarGridSpec(
        num_scalar_prefetch=1, grid=(r // tr,),
        in_specs=[pl.BlockSpec((1, tr, c), lambda i, s: (s[0], i, 0)), pl.BlockSpec((n, tr, c), lambda i, s: (0, i, 0))],
        out_specs=pl.BlockSpec((tr, c), lambda i, s: (i, 0)))
    return pl.pallas_call(
        body, name=name, out_shape=jax.ShapeDtypeStruct((r, c), out_dtype), grid_spec=grid_spec,
        compiler_params=_params(("parallel",)),
    )(sel, p, recv)


def _coords():
    return lax.axis_index("x"), lax.axis_index("y"), lax.axis_index("c")


def _allgather8(x2, name, space):
    _, m, n = x2.shape

    def body(x_ref, out_ref, send_sems, recv_sems, local_sem):
        x, y, c = _coords()
        me, sibling = (x, y, c), (x, y, 1 - c)
        chips = [(1 - x, y), (x, 1 - y), (1 - x, 1 - y)]
        mine_src = x_ref.at[c]

        def rows(px, py, pc):
            return out_ref.at[4 * px + 2 * py + pc]

        def copy(k, block, to, src=None):
            return pltpu.make_async_remote_copy(
                src_ref=rows(*block) if src is None else src, dst_ref=rows(*block),
                send_sem=send_sems.at[k], recv_sem=recv_sems.at[k], device_id=to, device_id_type=MESH)

        mine = pltpu.make_async_copy(mine_src, rows(*me), local_sem)
        mine.start()
        first = [copy(0, me, sibling, src=mine_src)]
        first += [copy(1 + j, me, (*chip, c), src=mine_src) for j, chip in enumerate(chips)]
        for cp in first:
            cp.start()
        passed = [copy(4 + j, (*chip, c), sibling) for j, chip in enumerate(chips)]
        for j, chip in enumerate(chips):
            copy(1 + j, (*chip, c), me).wait_recv()
            passed[j].start()
        copy(0, sibling, me).wait_recv()
        for j, chip in enumerate(chips):
            copy(4 + j, (*chip, 1 - c), me).wait_recv()
        for cp in first + passed:
            cp.wait_send()
        mine.wait()

    return pl.pallas_call(
        body, name=name, out_shape=jax.ShapeDtypeStruct((8, m, n), x2.dtype),
        in_specs=[pl.BlockSpec(memory_space=space)], out_specs=pl.BlockSpec(memory_space=space),
        scratch_shapes=[pltpu.SemaphoreType.DMA((7,)), pltpu.SemaphoreType.DMA((7,)), pltpu.SemaphoreType.DMA],
        compiler_params=pltpu.CompilerParams(vmem_limit_bytes=VMEM_LIMIT),
    )(x2)


N_REG = len(ROW_REGIONS) + 1


def _chip_window(ref, lead, r, j):
    view = ref if lead is None else ref.at[lead]
    if r < len(ROW_REGIONS):
        off, rows = ROW_REGIONS[r]
        return view.at[pl.ds(pl.multiple_of(off + j * rows, 16), rows), :]
    return view.at[pl.ds(R_OUT, OUT_ROWS), pl.ds(pl.multiple_of(j * OUT_COLS, LANE), OUT_COLS)]


HBM_SPEC = pl.BlockSpec(memory_space=pltpu.HBM)
SEM_SPEC = pl.BlockSpec(memory_space=pltpu.SEMAPHORE)


def _half_window(ref, r, j, h):
    if r < len(ROW_REGIONS):
        off, rows = ROW_REGIONS[r]
        return ref.at[pl.ds(pl.multiple_of(off + j * rows + h * (rows // 2), 16), rows // 2), :]
    half = OUT_ROWS // 2
    return ref.at[pl.ds(pl.multiple_of(R_OUT + h * half, 16), half),
                  pl.ds(pl.multiple_of(j * OUT_COLS, LANE), OUT_COLS)]


def _other_chips():
    x, y, _ = _coords()
    return [(1 - x, y), (x, 1 - y), (1 - x, 1 - y)]


def _ici_copies(srcs, arena_ref, send_sems, recv_sems, regions):
    x, y, c = _coords()
    sends, arrivals = [], []
    for k, (cx, cy) in enumerate(_other_chips()):
        for r in regions:
            def remote(src, j):
                return pltpu.make_async_remote_copy(
                    src_ref=src, dst_ref=_half_window(arena_ref, r, j, c), send_sem=send_sems.at[3 * r + k],
                    recv_sem=recv_sems.at[3 * r + k], device_id=(cx, cy, c), device_id_type=MESH)
            rows = srcs[r].shape[0] // 2
            sends.append(remote(srcs[r].at[pl.ds(pl.multiple_of(c * rows, 16), rows), :], 2 * x + y))
            arrivals.append(remote(_half_window(arena_ref, r, 2 * cx + cy, c), 2 * cx + cy))
    return sends, arrivals


def _sibling_copies(srcs, arena_ref, send_sems, recv_sems, regions):
    x, y, c = _coords()
    sends, arrivals = [], []

    def remote(win, r, k, src=None):
        return pltpu.make_async_remote_copy(
            src_ref=win if src is None else src, dst_ref=win, send_sem=send_sems.at[r, k],
            recv_sem=recv_sems.at[r, k], device_id=(x, y, 1 - c), device_id_type=MESH)

    for k, (cx, cy) in enumerate(_other_chips()):
        for r in regions:
            sends.append(remote(_half_window(arena_ref, r, 2 * cx + cy, c), r, k))
            arrivals.append(remote(_half_window(arena_ref, r, 2 * cx + cy, 1 - c), r, k))
    for r in regions:
        own = _chip_window(arena_ref, None, r, 2 * x + y)
        sends.append(remote(own, r, 3, src=srcs[r]))
        arrivals.append(remote(own, r, 3))
    return sends, arrivals


ICI_SEMS = pltpu.SemaphoreType.DMA((3 * N_REG,))
SIBLING_SEMS = pltpu.SemaphoreType.DMA((N_REG, 4))
ARENA_SHAPE = (ARENA_ROWS, ARENA_W)
ALL_REGIONS = tuple(range(N_REG))
IN_REGION = (3,)
REST_REGIONS = (0, 1, 2, 4, 5)


def _gather_layer(shards, name, regions=ALL_REGIONS):
    def body(*refs):
        srcs, arena_ref = refs[:N_REG], refs[N_REG]
        ici_send, ici_recv, sib_send, sib_recv = refs[N_REG + 1:]
        sends, arrivals = _ici_copies(srcs, arena_ref, ici_send, ici_recv, regions)
        passes, landings = _sibling_copies(srcs, arena_ref, sib_send, sib_recv, regions)
        for cp in sends + passes[len(arrivals):]:
            cp.start()
        for arrival, onward in zip(arrivals, passes):
            arrival.wait_recv()
            onward.start()
        for cp in landings:
            cp.wait_recv()
        for cp in sends + passes:
            cp.wait_send()

    return pl.pallas_call(
        body, name=name, out_shape=jax.ShapeDtypeStruct(ARENA_SHAPE, BF16),
        in_specs=[pl.BlockSpec(memory_space=pl.ANY)] * N_REG, out_specs=pl.BlockSpec(memory_space=pl.ANY),
        scratch_shapes=[ICI_SEMS, ICI_SEMS, SIBLING_SEMS, SIBLING_SEMS],
    )(*shards)


def _gather_start(shards, after, name, regions=ALL_REGIONS):
    def body(*refs):
        srcs, arena_ref = refs[:N_REG], refs[N_REG]
        send_sems, recv_sems = refs[N_REG + 2], refs[N_REG + 3]
        token = refs[-1]
        for cp in _ici_copies(srcs, arena_ref, send_sems, recv_sems, regions)[0]:
            cp.start()
        token[...] = jnp.zeros_like(token)

    hbm = lambda a: pltpu.with_memory_space_constraint(a, pltpu.HBM)
    outs = pl.pallas_call(
        body, name=name,
        out_shape=(ICI_SEMS, ICI_SEMS, *[pltpu.HBM(s.shape, s.dtype) for s in shards],
                   pltpu.HBM(ARENA_SHAPE, BF16), pltpu.HBM(after.shape, after.dtype),
                   jax.ShapeDtypeStruct((8, LANE), F32)),
        in_specs=[HBM_SPEC] * (N_REG + 2),
        out_specs=(SEM_SPEC, SEM_SPEC, *[HBM_SPEC] * (N_REG + 2), pl.BlockSpec(memory_space=pltpu.VMEM)),
        input_output_aliases={i: 2 + i for i in range(N_REG + 2)},
        compiler_params=pltpu.CompilerParams(has_side_effects=pltpu.SideEffectType.DATAFLOW_SIDE_EFFECTING),
    )(*[hbm(s) for s in shards], hbm(lax.empty(ARENA_SHAPE, BF16)), hbm(after))
    return outs[0], outs[1], outs[2:2 + N_REG], outs[2 + N_REG], outs[-1], outs[3 + N_REG]


def _gather_wait(send_sems, recv_sems, shards, arena, after, name, regions=ALL_REGIONS):
    def body(*refs):
        srcs, arena_ref = refs[:N_REG], refs[N_REG]
        sends, arrivals = _ici_copies(srcs, arena_ref, refs[N_REG + 1], refs[N_REG + 2], regions)
        for cp in sends:
            cp.wait_send()
        for cp in arrivals:
            cp.wait_recv()

    outs = pl.pallas_call(
        body, name=name,
        out_shape=(*[pltpu.HBM(s.shape, s.dtype) for s in shards], pltpu.HBM(ARENA_SHAPE, BF16)),
        in_specs=[HBM_SPEC] * (N_REG + 1) + [SEM_SPEC, SEM_SPEC, pl.BlockSpec(memory_space=pl.ANY)],
        out_specs=(HBM_SPEC,) * (N_REG + 1), input_output_aliases={i: i for i in range(N_REG + 1)},
        compiler_params=pltpu.CompilerParams(has_side_effects=pltpu.SideEffectType.DATAFLOW_SIDE_EFFECTING),
    )(*shards, arena, send_sems, recv_sems, after)
    return outs[:N_REG], outs[N_REG]


def _gather_finish(shards, arena, name, regions=ALL_REGIONS):
    def body(*refs):
        srcs, arena_ref = refs[:N_REG], refs[N_REG + 1]
        sends, arrivals = _sibling_copies(srcs, arena_ref, refs[N_REG + 2], refs[N_REG + 3], regions)
        for cp in sends:
            cp.start()
        for cp in arrivals:
            cp.wait_recv()
        for cp in sends:
            cp.wait_send()

    return pl.pallas_call(
        body, name=name, out_shape=jax.ShapeDtypeStruct(ARENA_SHAPE, BF16),
        in_specs=[pl.BlockSpec(memory_space=pl.ANY)] * (N_REG + 1), out_specs=pl.BlockSpec(memory_space=pl.ANY),
        scratch_shapes=[SIBLING_SEMS, SIBLING_SEMS], input_output_aliases={N_REG: 0},
    )(*shards, arena)


HALF_PIECE_OFF = tuple(o // 2 for o in PIECE_OFF)
HALF_PIECE_ROWS = PIECE_ROWS // 2
HALF_OUT_ROWS = OUT_ROWS // 2
SWAP_SEMS = pltpu.SemaphoreType.DMA((4 * N_REG,))
SCATTER_SEMS = pltpu.SemaphoreType.DMA((6,))


def _packed_shapes(slots, dtype):
    return (jax.ShapeDtypeStruct((slots, HALF_PIECE_ROWS, ARENA_W), dtype),
            jax.ShapeDtypeStruct((slots, HALF_OUT_ROWS, OUT_COLS), dtype))


def _swap_halves(ga, name):
    def body(g_ref, main_ref, outp_ref, send_sems, recv_sems):
        x, y, c = _coords()
        cps = []
        for j in range(4):
            for r in range(N_REG):
                if r < len(ROW_REGIONS):
                    dst = main_ref.at[j, pl.ds(HALF_PIECE_OFF[r], ROW_REGIONS[r][1] // 2), :]
                else:
                    dst = outp_ref.at[j]
                cps.append(pltpu.make_async_remote_copy(
                    src_ref=_half_window(g_ref, r, j, 1 - c), dst_ref=dst, send_sem=send_sems.at[j * N_REG + r],
                    recv_sem=recv_sems.at[j * N_REG + r], device_id=(x, y, 1 - c), device_id_type=MESH))
        for cp in cps:
            cp.start()
        for cp in cps:
            cp.wait()

    return pl.pallas_call(
        body, name=name, out_shape=_packed_shapes(4, ga.dtype),
        in_specs=[pl.BlockSpec(memory_space=pl.ANY)], out_specs=(pl.BlockSpec(memory_space=pl.ANY),) * 2,
        scratch_shapes=[SWAP_SEMS, SWAP_SEMS],
    )(ga)


def _own_halves(ga, cc):
    mains = [jnp.concatenate([lax.dynamic_slice(ga, (off + j * rows + cc * (rows // 2), 0), (rows // 2, ARENA_W))
                              for off, rows in ROW_REGIONS]) for j in range(4)]
    outs = [lax.dynamic_slice(ga, (R_OUT + cc * HALF_OUT_ROWS, j * OUT_COLS), (HALF_OUT_ROWS, OUT_COLS))
            for j in range(4)]
    return jnp.stack(mains), jnp.stack(outs)


def _scatter_copies(main_ref, outp_ref, rmain_ref, routp_ref, send_sems, recv_sems):
    _, _, c = _coords()
    cps = []
    for k, (cx, cy) in enumerate(_other_chips()):
        for i, (src, dst) in enumerate(((main_ref, rmain_ref), (outp_ref, routp_ref))):
            cps.append(pltpu.make_async_remote_copy(
                src_ref=src.at[2 * cx + cy], dst_ref=dst.at[k], send_sem=send_sems.at[2 * k + i],
                recv_sem=recv_sems.at[2 * k + i], device_id=(cx, cy, c), device_id_type=MESH))
    return cps


def _scatter_halves(main, outp, name):
    def body(main_ref, outp_ref, rmain_ref, routp_ref, send_sems, recv_sems):
        cps = _scatter_copies(main_ref, outp_ref, rmain_ref, routp_ref, send_sems, recv_sems)
        for cp in cps:
            cp.start()
        for cp in cps:
            cp.wait()

    return pl.pallas_call(
        body, name=name, out_shape=_packed_shapes(3, main.dtype),
        in_specs=[pl.BlockSpec(memory_space=pl.ANY)] * 2, out_specs=(pl.BlockSpec(memory_space=pl.ANY),) * 2,
        scratch_shapes=[SCATTER_SEMS, SCATTER_SEMS],
    )(main, outp)


def _scatter_start(main, outp, name):
    def body(main_ref, outp_ref, rmain_ref, routp_ref, send_sems, recv_sems, *rest):
        for cp in _scatter_copies(main_ref, outp_ref, rmain_ref, routp_ref, send_sems, recv_sems):
            cp.start()
        rest[-1][...] = jnp.zeros_like(rest[-1])

    hbm = lambda a: pltpu.with_memory_space_constraint(a, pltpu.HBM)
    land = [lax.empty(s.shape, s.dtype) for s in _packed_shapes(3, main.dtype)]
    bufs = [main, outp, *land]
    outs = pl.pallas_call(
        body, name=name,
        out_shape=(SCATTER_SEMS, SCATTER_SEMS, *[pltpu.HBM(b.shape, b.dtype) for b in bufs],
                   jax.ShapeDtypeStruct((8, LANE), F32)),
        in_specs=[HBM_SPEC] * 4, out_specs=(SEM_SPEC, SEM_SPEC, *[HBM_SPEC] * 4, pl.BlockSpec(memory_space=pltpu.VMEM)),
        input_output_aliases={i: 2 + i for i in range(4)},
        compiler_params=pltpu.CompilerParams(has_side_effects=pltpu.SideEffectType.DATAFLOW_SIDE_EFFECTING),
    )(*[hbm(b) for b in bufs])
    return outs[0], outs[1], outs[2:6], outs[6]


def _scatter_wait(send_sems, recv_sems, bufs, after, name):
    def body(main_ref, outp_ref, rmain_ref, routp_ref, send_sems, recv_sems, *rest):
        for cp in _scatter_copies(main_ref, outp_ref, rmain_ref, routp_ref, send_sems, recv_sems):
            cp.wait_send()
            cp.wait_recv()

    return pl.pallas_call(
        body, name=name, out_shape=tuple(pltpu.HBM(b.shape, b.dtype) for b in bufs),
        in_specs=[HBM_SPEC] * 4 + [SEM_SPEC, SEM_SPEC, pl.BlockSpec(memory_space=pl.ANY)],
        out_specs=(HBM_SPEC,) * 4, input_output_aliases={i: i for i in range(4)},
        compiler_params=pltpu.CompilerParams(has_side_effects=pltpu.SideEffectType.DATAFLOW_SIDE_EFFECTING),
    )(*bufs, send_sems, recv_sems, after)


def _swap_many(arrs, name):
    n = len(arrs)

    def body(*refs):
        x, y, c = _coords()
        send_sems, recv_sems = refs[2 * n], refs[2 * n + 1]
        cps = [pltpu.make_async_remote_copy(
            src_ref=refs[i], dst_ref=refs[n + i], send_sem=send_sems.at[i], recv_sem=recv_sems.at[i],
            device_id=(x, y, 1 - c), device_id_type=MESH) for i in range(n)]
        for cp in cps:
            cp.start()
        for cp in cps:
            cp.wait()

    return pl.pallas_call(
        body, name=name, out_shape=tuple(jax.ShapeDtypeStruct(a.shape, a.dtype) for a in arrs),
        in_specs=[pl.BlockSpec(memory_space=pl.ANY)] * n, out_specs=(pl.BlockSpec(memory_space=pl.ANY),) * n,
        scratch_shapes=[pltpu.SemaphoreType.DMA((n,)), pltpu.SemaphoreType.DMA((n,))],
    )(*arrs)


def _join_halves(mine, theirs, cc):
    return jnp.where(cc == 0, jnp.concatenate([mine, theirs]), jnp.concatenate([theirs, mine]))


def _reduced_layer(red, sib, cc):
    parts = []
    for (_, rows), off in zip(ROW_REGIONS, HALF_PIECE_OFF):
        parts.append(_join_halves(red[0][off:off + rows // 2], sib[0][off:off + rows // 2], cc))
    return jnp.concatenate(parts), _join_halves(red[1], sib[1], cc)


OUT_NAMES = ("w_a_out", "w_b_out", "w_c_out", "w_d_out")


def _arena_shards(w):
    t = lambda a: a.astype(BF16).transpose(0, 2, 1)
    return (w["w_ffn_down"].astype(BF16), t(w["w_ffn_gate"]), t(w["w_ffn_up"]), t(w["w_in"]), w["w_o"].astype(BF16),
            jnp.concatenate([w[n].astype(BF16) for n in OUT_NAMES], axis=1))


def _shard_grads(main, outp):
    t = lambda r: main[:, PIECE_OFF[r]:PIECE_OFF[r] + ROW_REGIONS[r][1]]
    g = dict(w_ffn_down=t(0), w_ffn_gate=t(1).transpose(0, 2, 1), w_ffn_up=t(2).transpose(0, 2, 1),
             w_in=t(3).transpose(0, 2, 1), w_o=t(4))
    for i, n in enumerate(OUT_NAMES):
        g[n] = outp[:, i * BW:(i + 1) * BW]
    return g


def _gather_taps(p, name):
    mine = jnp.concatenate([p[n] for n in CONV_NAMES], axis=1).reshape(DEPTH * N_TAPS, LANE)
    rows = -(-mine.shape[0] // 8) * 8
    mine = jnp.concatenate([mine, jnp.zeros((rows - mine.shape[0], LANE), F32)])
    g = _allgather8(jnp.stack([mine, mine]), name, pltpu.VMEM)[0::2, :DEPTH * N_TAPS]
    full = g.reshape(4, DEPTH, N_TAPS, LANE).transpose(1, 2, 0, 3).reshape(DEPTH, N_TAPS, BW)
    return dict(conv_a_w=full[:, :CONV_A], conv_b_w=full[:, CONV_A:CONV_A + CONV_B], conv_d_w=full[:, CONV_A + CONV_B:])


def _flat_pack(arrs):
    flat = jnp.concatenate([a.reshape(-1).astype(F32) for a in arrs])
    rows = -(-flat.shape[0] // (8 * LANE)) * 8
    return jnp.concatenate([flat, jnp.zeros((rows * LANE - flat.shape[0],), F32)]).reshape(rows, LANE)


def _flat_unpack(packed, shapes):
    flat, out, off = packed.reshape(-1), [], 0
    for s in shapes:
        cnt = int(np.prod(s))
        out.append(flat[off:off + cnt].reshape(s))
        off += cnt
    return out


def _blockdiag_chunks(w):
    w4 = w.reshape(4, 2, 64, 64)
    z = jnp.zeros((4, 2, 64, 2, 64), F32)
    z = z.at[:, 0, :, 0, :].set(w4[:, 0]).at[:, 1, :, 1, :].set(w4[:, 1])
    return z.reshape(4, LANE, LANE)


def _blockdiag_extract(d):
    d5 = d.reshape(4, 2, 64, 2, 64)
    return jnp.stack([d5[:, 0, :, 0, :], d5[:, 1, :, 1, :]], axis=1).reshape(8, 64, 64)


SLOPES = np.asarray([2.0 ** (-8.0 * (i + 1) / N_Q) for i in range(N_Q)], np.float32)


def _layer_consts(p, fw, l):
    row = lambda a: a[l].reshape(1, -1)
    return dict(
        g1=row(p["norm1_g"]), g2=row(p["norm2_g"]), wA=fw["conv_a_w"][l], bA=row(p["conv_a_b"]),
        wx=_blockdiag_chunks(p["lru_wx"][l]), bx=row(p["lru_bx"]), wa=_blockdiag_chunks(p["lru_wa"][l]),
        ba=row(p["lru_ba"]), lam=row(p["lru_lambda"]), wB=fw["conv_b_w"][l],
        ss=jnp.stack([p["sinks"][l], jnp.asarray(SLOPES)]), wD=fw["conv_d_w"][l], bD=row(p["conv_d_b"]),
        lg=row(p["ln_d_g"]), lb=row(p["ln_d_b"]))


def _layer_fwd(x, c, fw, l, rest_of_weights=None):
    t = f"l{l}_"
    xn = _rms_fwd(x, c["g1"], t + "rms1")
    wt = lambda off, rows: Win(fw["arena"][l], None, off, rows)
    proj = _mm(xn, Win(fw["arena_in"][l], None, R_IN, IN_W), "nt", t + "proj")
    ya = _a_fwd(proj, c["wA"], c["bA"], c["wx"], c["bx"], c["wa"], c["ba"], c["lam"], t + "a_fwd")
    yb = _b_fwd(proj, c["wB"], t + "b_fwd")
    q3 = _heads(proj[:, OFF_Q:OFF_K], N_Q)
    k3 = _heads(proj[:, OFF_K:OFF_V], N_KV)
    v3 = _heads(proj[:, OFF_V:OFF_V + N_KV * HEAD_DIM], N_KV)
    yc = _unheads(_attn_fwd(q3, k3, v3, c["ss"], t + "attn_fwd"))
    cd = _d_conv_fwd(proj, c["wD"], c["bD"], t + "d_conv_fwd")
    yd = _ln_silu_fwd(cd, c["lg"], c["lb"], t + "d_ln_fwd")
    ys = (ya, yb, yc, yd)
    if fw["arena"][l] is None:
        fw["arena"][l] = rest_of_weights(yd)
    big_y = tuple(_mm(y, wt(R_OUT + i * BW, BW), "nn", t + f"out{i}") for i, y in enumerate(ys))
    merged = _merge_fwd(proj, big_y, t + "merge_fwd")
    hres = _mm(merged, wt(R_O, D_MODEL), "nn", t + "wo", add=x)
    hn = _rms_fwd(hres, c["g2"], t + "rms2")
    gg = _mm(hn, wt(R_GATE, D_FF), "nt", t + "ffn_gate")
    uu = _mm(hn, wt(R_UP, D_FF), "nt", t + "ffn_up")
    act = _swiglu_fwd(gg, uu, t + "swiglu_fwd")
    xout = _mm(act, wt(R_DOWN, D_FF), "nn", t + "ffn_down", add=hres)
    saved = dict(x=x, xn=xn, proj=proj, ys=ys, q3=q3, k3=k3, v3=v3, cd=cd, big_y=big_y, merged=merged, hres=hres,
                 hn=hn, gg=gg, uu=uu, act=act)
    return xout, saved


def _layer_bwd(dxout, s, c, fw, l, ga):
    t = f"l{l}_"
    gs = {}
    wt = lambda off, rows: Win(fw["arena"][l], None, off, rows)
    gt = lambda off, rows: Win(ga, None, off, rows)
    dact = _mm(dxout, wt(R_DOWN, D_FF), "nt", t + "d_act")
    ga = _mm(s["act"], dxout, "tn", t + "dw_down", out=gt(R_DOWN, D_FF))
    dgg, duu = _swiglu_bwd(s["gg"], s["uu"], dact, t + "swiglu_bwd")
    ga = _mm(dgg, s["hn"], "tn", t + "dw_gate", out=gt(R_GATE, D_FF))
    ga = _mm(duu, s["hn"], "tn", t + "dw_up", out=gt(R_UP, D_FF))
    dhn = _mm(dgg, wt(R_GATE, D_FF), "nn", t + "d_hn_g")
    dhn = _mm(duu, wt(R_UP, D_FF), "nn", t + "d_hn_u", add=dhn)
    dhres, gs["norm2_g"] = _rms_bwd(s["hres"], c["g2"], dhn, dxout, t + "rms2_bwd")
    dmerged = _mm(dhres, wt(R_O, D_MODEL), "nt", t + "d_merged")
    ga = _mm(s["merged"], dhres, "tn", t + "dw_o", out=gt(R_O, D_MODEL))
    dbig_y, dgl = _merge_bwd(s["proj"], s["big_y"], dmerged, t + "merge_bwd")
    dys = []
    for i in range(4):
        ga = _mm(s["ys"][i], dbig_y[i], "tn", t + f"dw_out{i}", out=gt(R_OUT + i * BW, BW))
        dys.append(_mm(dbig_y[i], wt(R_OUT + i * BW, BW), "nt", t + f"d_y{i}"))
    proj = s["proj"]
    (dax, dag, gs["conv_a_w"], gs["conv_a_b"], dwx, gs["lru_bx"], dwa, gs["lru_ba"], gs["lru_lambda"]) = _a_bwd(
        proj, dys[0], c["wA"], c["bA"], c["wx"], c["bx"], c["wa"], c["ba"], c["lam"], t + "a_bwd")
    gs["lru_wx"] = _blockdiag_extract(dwx)
    gs["lru_wa"] = _blockdiag_extract(dwa)
    dbv, dbc, dbb, gs["conv_b_w"] = _b_bwd(proj, dys[1], c["wB"], t + "b_bwd")
    dq3, dk3, dv3, dsink = _attn_bwd(s["q3"], s["k3"], s["v3"], _heads(dys[2], N_Q), c["ss"], t + "attn_bwd")
    gs["sinks"] = dsink[:, 0, 0]
    dcd, gs["ln_d_g"], gs["ln_d_b"] = _ln_silu_bwd(s["cd"], c["lg"], c["lb"], dys[3], t + "d_ln_bwd")
    dd1, dd2, gs["conv_d_w"], gs["conv_d_b"] = _d_conv_bwd(proj, dcd, c["wD"], t + "d_conv_bwd")
    dproj = jnp.concatenate(
        [dax, dag, dbv, dbc, dbb, _unheads(dq3), _unheads(dk3).astype(BF16), _unheads(dv3).astype(BF16), dd1, dd2,
         *dgl], axis=1)
    ga = _mm(dproj, s["xn"], "tn", t + "dw_in", out=gt(R_IN, IN_W))
    dxn = _mm(dproj, Win(fw["arena_in"][l], None, R_IN, IN_W), "nn", t + "d_xn")
    dx, gs["norm1_g"] = _rms_bwd(s["x"], c["g1"], dxn, dhres, t + "rms1_bwd")
    return dx, ga, gs


def kernel(x, norm1_g, w_in, conv_a_w, conv_a_b, lru_wx, lru_bx, lru_wa, lru_ba, lru_lambda, w_a_out, conv_b_w, w_b_out, sinks, w_c_out, conv_d_w, conv_d_b, ln_d_g, ln_d_b, w_d_out, w_o, norm2_g, w_ffn_gate, w_ffn_up, w_ffn_down, final_g, loss_target, m_norm1_g, m_w_in, m_conv_a_w, m_conv_a_b, m_lru_wx, m_lru_bx, m_lru_wa, m_lru_ba, m_lru_lambda, m_w_a_out, m_conv_b_w, m_w_b_out, m_sinks, m_w_c_out, m_conv_d_w, m_conv_d_b, m_ln_d_g, m_ln_d_b, m_w_d_out, m_w_o, m_norm2_g, m_w_ffn_gate, m_w_ffn_up, m_w_ffn_down, m_final_g, v_norm1_g, v_w_in, v_conv_a_w, v_conv_a_b, v_lru_wx, v_lru_bx, v_lru_wa, v_lru_ba, v_lru_lambda, v_w_a_out, v_conv_b_w, v_w_b_out, v_sinks, v_w_c_out, v_conv_d_w, v_conv_d_b, v_ln_d_g, v_ln_d_b, v_w_d_out, v_w_o, v_norm2_g, v_w_ffn_gate, v_w_ffn_up, v_w_ffn_down, v_final_g):
    given = dict(locals())
    p = {n: given[n] for n in NAMES}
    mom = {n: given["m_" + n] for n in NAMES}
    var = {n: given["v_" + n] for n in NAMES}
    cx, cy, cc = _coords()
    chip = 2 * cx + cy

    shards = _arena_shards(p)
    fw = _gather_taps(p, "gather_taps")
    shards0, shards1 = [s[0] for s in shards], [s[1] for s in shards]
    flight0 = _gather_start(shards0, _gather_layer(shards0, "gather_l0_in", IN_REGION), "gather_l0_rest_start",
                            REST_REGIONS)
    fw["arena_in"] = [flight0[5], None]
    fw["arena"] = [None, None]
    consts = [_layer_consts(p, fw, l) for l in range(DEPTH)]
    consts[0]["g1"] = consts[0]["g1"] + flight0[4][0:1, 0:1]
    flight1 = []

    def rest_of_layer0(after):
        sh, landing = _gather_wait(*flight0[:4], after, "gather_l0_rest_wait", REST_REGIONS)
        arena = _gather_finish(sh, landing, "gather_l0_rest_finish", REST_REGIONS)
        flight1.extend(_gather_start(shards1, arena, "gather_l1_start"))
        return flight1[5]

    h = x[0]
    saved = []
    for l in range(DEPTH):
        if l == 1:
            sh, landing = _gather_wait(*flight1[:4], h, "gather_l1_wait")
            fw["arena"][1] = fw["arena_in"][1] = _gather_finish(sh, landing, "gather_l1_finish")
        h, s = _layer_fwd(h, consts[l], fw, l, rest_of_layer0)
        saved.append(s)
    loss_vec, dh, g_final = _loss_head(h, final_g.reshape(1, -1), loss_target[0], "loss_head")
    loss = lax.psum(loss_vec[0, 0], ("x", "y", "c"))

    zero = jnp.zeros((1,), jnp.int32)
    chip_sel = chip.reshape(1).astype(jnp.int32)

    def chip_sums(ga, t):
        own, got = _own_halves(ga, cc), _swap_halves(ga, t + "grads_swap_halves")
        return [_sum_own_plus(o.reshape((1, -1, o.shape[-1])), zero, r.reshape((1, -1, r.shape[-1])),
                              t + f"grads_sum_chip{i}", BF16).reshape(o.shape) for i, (o, r) in enumerate(zip(own, got))]

    def all_sums(sums, got, t):
        return [_sum_own_plus(s, chip_sel, r, t + f"grads_sum_all{i}", F32) for i, (s, r) in enumerate(zip(sums, got))]

    gss = [None] * DEPTH
    dh, ga1, gss[1] = _layer_bwd(dh, saved[1], consts[1], fw, 1, lax.empty(ARENA_SHAPE, BF16))
    send_sems, recv_sems, bufs, token = _scatter_start(*chip_sums(ga1, "l1_"), "l1_grads_scatter_start")
    dh, ga0, gss[0] = _layer_bwd(dh + token[0:1, 0:1], saved[0], consts[0], fw, 0, lax.empty(ARENA_SHAPE, BF16))
    grad_x = dh[None]
    bufs = _scatter_wait(send_sems, recv_sems, bufs, dh, "l1_grads_scatter_wait")
    red1 = all_sums(bufs[:2], bufs[2:], "l1_")
    sums0 = chip_sums(ga0, "l0_")
    red0 = all_sums(sums0, _scatter_halves(*sums0, "l0_grads_scatter"), "l0_")
    sib = _swap_many(red0 + red1, "grads_swap_reduced")
    layers = [_reduced_layer(red0, sib[:2], cc), _reduced_layer(red1, sib[2:], cc)]
    g = _shard_grads(jnp.stack([m for m, _ in layers]), jnp.stack([o for _, o in layers]))

    small_full = {n: (g_final.reshape(-1) if n == "final_g" else
                      jnp.stack([gss[l][n].reshape(gss[l][n].shape[-2:] if n.startswith("conv") and n.endswith("_w")
                                                   else p[n].shape[1:]) for l in range(DEPTH)]))
                  for n in SMALL}
    part = _flat_pack([small_full[n] for n in SMALL])
    rows = part.shape[0]
    gathered = _allgather8(jnp.stack([part, part]), "gather_small_grads", pltpu.VMEM)
    small_sum = _flat_unpack(_sum_leading(gathered, "small_grads_sum"), [small_full[n].shape for n in SMALL])
    for n, a in zip(SMALL, small_sum):
        g[n] = lax.dynamic_slice_in_dim(a, chip * LANE, LANE, axis=2) if n in CONV_NAMES else a

    delta, new_m, new_v = {}, {}, {}
    for n in BIG:
        shp = p[n].shape
        two_d = lambda a: a.reshape(-1, shp[-1])
        d, nm, nv = _adamw(two_d(p[n]), two_d(g[n]), two_d(mom[n]), two_d(var[n]), "adamw_" + n)
        delta[n], new_m[n], new_v[n] = d.reshape(shp), nm.reshape(shp), nv.reshape(shp)
    shapes = [p[n].shape for n in SMALL]
    d, nm, nv = _adamw(_flat_pack([p[n] for n in SMALL]), _flat_pack([g[n] for n in SMALL]),
                       _flat_pack([mom[n] for n in SMALL]), _flat_pack([var[n] for n in SMALL]), "adamw_small")
    for n, a, b, cval in zip(SMALL, _flat_unpack(d, shapes), _flat_unpack(nm, shapes), _flat_unpack(nv, shapes)):
        delta[n], new_m[n], new_v[n] = a, b, cval

    return (loss, grad_x, *[g[n] for n in NAMES], *[delta[n] for n in NAMES], *[new_m[n] for n in NAMES],
            *[new_v[n] for n in NAMES])
```

```python
import functools
import math

import numpy as np
import jax
import jax.numpy as jnp
from jax import lax
from jax.experimental import pallas as pl
from jax.experimental.pallas import tpu as pltpu

F32 = jnp.float32
BF16 = jnp.bfloat16
MESH = pl.DeviceIdType.MESH

D_MODEL = 1024
DEPTH = 2
BW = 512
HEAD_DIM = 64
N_Q = 8
N_KV = 2
BLK = 128
D_FF = 2816
IN_W = 8448
EPS = 1e-6
NEG_INF = -1e30
LRU_C = 8.0
CONV_A, CONV_B, CONV_D = 4, 3, 31
LANE = 128
ROW_TILE = 256
VMEM_LIMIT = 56 * 1024 * 1024
MM_VMEM_BUDGET = 36 * 1024 * 1024

C_AX, C_AG, C_BV, C_BC, C_BB = 0, 4, 8, 12, 16
OFF_Q, OFF_K, OFF_V = 2560, 3072, 3200
C_D1, C_D2 = 26, 30
OFF_GL = 4352

ADAM_LR, ADAM_B1, ADAM_B2, ADAM_EPS, ADAM_WD, ADAM_STEP = 0.001, 0.9, 0.999, 1e-08, 0.01, 10

ARENA_W = 1024
R_DOWN, R_GATE, R_UP, R_IN, R_O, R_OUT = 0, 2816, 5632, 8448, 16896, 17920
ARENA_ROWS = 19968
ROW_REGIONS = ((R_DOWN, 704), (R_GATE, 704), (R_UP, 704), (R_IN, 2112), (R_O, 256))
PIECE_OFF = (0, 704, 1408, 2112, 4224)
PIECE_ROWS = 4480
OUT_ROWS, OUT_COLS = 4 * BW, D_MODEL // 4

BIG = ("w_in", "w_a_out", "w_b_out", "w_c_out", "w_d_out", "w_o", "w_ffn_gate", "w_ffn_up", "w_ffn_down")
CONV_NAMES = ("conv_a_w", "conv_b_w", "conv_d_w")
N_TAPS = CONV_A + CONV_B + CONV_D
SMALL = ("norm1_g", "conv_a_w", "conv_a_b", "lru_wx", "lru_bx", "lru_wa", "lru_ba", "lru_lambda", "conv_b_w",
         "sinks", "conv_d_w", "conv_d_b", "ln_d_g", "ln_d_b", "norm2_g", "final_g")
NAMES = ['norm1_g', 'w_in', 'conv_a_w', 'conv_a_b', 'lru_wx', 'lru_bx', 'lru_wa', 'lru_ba', 'lru_lambda', 'w_a_out',
         'conv_b_w', 'w_b_out', 'sinks', 'w_c_out', 'conv_d_w', 'conv_d_b', 'ln_d_g', 'ln_d_b', 'w_d_out', 'w_o',
         'norm2_g', 'w_ffn_gate', 'w_ffn_up', 'w_ffn_down', 'final_g']


def _pick(n, cands, off=0):
    for c in cands:
        if n % c == 0 and off % c == 0:
            return c
    assert off == 0, (n, off)
    return n


class Win:
    def __init__(self, arena, l, off, rows):
        self.arena, self.l, self.off, self.rows = arena, l, off, rows
        self.shape = (rows, arena.shape[-1])


def _params(sem=None):
    return pltpu.CompilerParams(dimension_semantics=sem, vmem_limit_bytes=VMEM_LIMIT)


def _sig(z):
    return 1.0 / (1.0 + jnp.exp(-z))


def _dot(a, b, dims):
    return lax.dot_general(a.astype(BF16), b.astype(BF16), (dims, ((), ())), preferred_element_type=F32)


NN = ((1,), (0,))
NT = ((1,), (1,))
TN = ((0,), (0,))


def _mm(a, b, mode, name, out_dtype=F32, add=None, out=None):
    if mode == "nn":
        (m, k), n = a.shape, b.shape[1]
    elif mode == "nt":
        (m, k), n = a.shape, b.shape[0]
    else:
        (k, m), n = a.shape, b.shape[1]
    b_win = isinstance(b, Win)
    b_off = b.off if b_win else 0
    o_off = out.off if out is not None else 0
    if out is not None:
        out_dtype = out.arena.dtype
    tk = _pick(k, (2816, 2048, 1408, 1024, 768, 512, 256), b_off if mode != "nt" else 0)
    nk = k // tk
    n_off = b_off if mode == "nt" else 0
    a_bytes, b_bytes, o_bytes = a.dtype.itemsize, 2, jnp.dtype(out_dtype).itemsize

    def vmem_bytes(tm_, tn_):
        tile = tm_ * tn_
        return (2 * tk * (tm_ * a_bytes + tn_ * b_bytes) + 2 * tile * o_bytes + (tile * 4 if nk > 1 else 0)
                + (2 * tile * 4 if add is not None else 0) + tile * 4)

    pairs = [(tm_, tn_) for tm_ in (2048, 1024, 768, 512, 256, 128) for tn_ in (1024, 768, 512, 256, 128)
             if m % tm_ == 0 and o_off % tm_ == 0 and n % tn_ == 0 and n_off % tn_ == 0
             and vmem_bytes(tm_, tn_) <= MM_VMEM_BUDGET]
    tm, tn = max(pairs, key=lambda p: (p[0] * p[1], p[0]))
    dims = {"nn": NN, "nt": NT, "tn": TN}[mode]

    def body(*refs):
        a_ref, b_ref = refs[:2]
        c_ref = refs[2] if add is not None else None
        if nk == 1:
            r = _dot(a_ref[...], b_ref[...], dims)
            if add is not None:
                r = r + c_ref[...]
            refs[-1][...] = r.astype(out_dtype)
            return
        o_ref, acc = refs[-2:]
        kk = pl.program_id(2)

        @pl.when(kk == 0)
        def _():
            acc[...] = jnp.zeros_like(acc)

        acc[...] += _dot(a_ref[...], b_ref[...], dims)

        @pl.when(kk == nk - 1)
        def _():
            r = acc[...]
            if add is not None:
                r = r + c_ref[...]
            o_ref[...] = r.astype(out_dtype)

    if mode == "tn":
        a_spec = pl.BlockSpec((tk, tm), lambda i, j, q: (q, i))
    else:
        a_spec = pl.BlockSpec((tm, tk), lambda i, j, q: (i, q))
    if mode == "nt":
        b_blk, b_idx = (tn, tk), (lambda i, j, q: (b_off // tn + j, q))
    else:
        b_blk, b_idx = (tk, tn), (lambda i, j, q: (b_off // tk + q, j))
    if b_win and b.arena.ndim == 3:
        bl = b.l
        b_spec = pl.BlockSpec((None,) + b_blk, lambda i, j, q: (bl,) + b_idx(i, j, q))
    else:
        b_spec = pl.BlockSpec(b_blk, b_idx)
    plain_o = pl.BlockSpec((tm, tn), lambda i, j, q: (i, j))
    in_specs = [a_spec, b_spec] + ([plain_o] if add is not None else [])
    args = (a, b.arena if b_win else b) + ((add,) if add is not None else ())
    aliases = {}
    if out is None:
        o_spec, o_shape = plain_o, jax.ShapeDtypeStruct((m, n), out_dtype)
    else:
        ol = out.l
        if out.arena.ndim == 3:
            o_spec = pl.BlockSpec((None, tm, tn), lambda i, j, q: (ol, o_off // tm + i, j))
        else:
            o_spec = pl.BlockSpec((tm, tn), lambda i, j, q: (o_off // tm + i, j))
        o_shape = jax.ShapeDtypeStruct(out.arena.shape, out_dtype)
        aliases = {len(args): 0}
        in_specs.append(pl.BlockSpec(memory_space=pl.ANY))
        args = args + (out.arena,)
    return pl.pallas_call(
        body, name=name, out_shape=o_shape,
        grid=(m // tm, n // tn, nk), in_specs=in_specs, out_specs=o_spec,
        scratch_shapes=[pltpu.VMEM((tm, tn), F32)] if nk > 1 else [], input_output_aliases=aliases,
        compiler_params=_params(("parallel", "parallel", "arbitrary")),
    )(*args)


def _row_spec(cols, tr=ROW_TILE):
    return pl.BlockSpec((tr, cols), lambda i: (i, 0))


def _vec_spec(cols):
    return pl.BlockSpec((1, cols), lambda i: (0, 0))


def _rms_fwd(x, g, name):
    t, d = x.shape

    def body(x_ref, g_ref, o_ref):
        xv = x_ref[...]
        r = lax.rsqrt(jnp.mean(xv * xv, axis=1, keepdims=True) + EPS)
        o_ref[...] = (xv * r * g_ref[...]).astype(BF16)

    return pl.pallas_call(
        body, name=name, out_shape=jax.ShapeDtypeStruct((t, d), BF16), grid=(t // ROW_TILE,),
        in_specs=[_row_spec(d), _vec_spec(d)], out_specs=_row_spec(d), compiler_params=_params(("parallel",)),
    )(x, g)


def _rms_bwd(x, g, dxn, dres, name):
    t, d = x.shape

    def body(x_ref, g_ref, dy_ref, dr_ref, dx_ref, dg_ref):
        @pl.when(pl.program_id(0) == 0)
        def _():
            dg_ref[...] = jnp.zeros_like(dg_ref)

        xv = x_ref[...]
        dy = dy_ref[...]
        r = lax.rsqrt(jnp.mean(xv * xv, axis=1, keepdims=True) + EPS)
        w = dy * g_ref[...]
        dx_ref[...] = dr_ref[...] + r * w - xv * (r * r * r) * jnp.mean(w * xv, axis=1, keepdims=True)
        dg_ref[...] += jnp.sum(dy * xv * r, axis=0, keepdims=True)

    return pl.pallas_call(
        body, name=name,
        out_shape=(jax.ShapeDtypeStruct((t, d), F32), jax.ShapeDtypeStruct((1, d), F32)), grid=(t // ROW_TILE,),
        in_specs=[_row_spec(d), _vec_spec(d), _row_spec(d), _row_spec(d)], out_specs=(_row_spec(d), _vec_spec(d)),
        compiler_params=_params(("arbitrary",)),
    )(x, g, dxn, dres)


def _loss_head(x, g, tgt, name):
    t, d = x.shape

    def body(x_ref, g_ref, t_ref, loss_ref, dx_ref, dg_ref):
        @pl.when(pl.program_id(0) == 0)
        def _():
            dg_ref[...] = jnp.zeros_like(dg_ref)
            loss_ref[...] = jnp.zeros_like(loss_ref)

        xv = x_ref[...]
        gv = g_ref[...]
        r = lax.rsqrt(jnp.mean(xv * xv, axis=1, keepdims=True) + EPS)
        e = xv * r * gv - t_ref[...]
        loss_ref[...] += jnp.full(loss_ref.shape, (0.5 / d) * jnp.sum(e * e), F32)
        dy = e * (1.0 / d)
        w = dy * gv
        dx_ref[...] = r * w - xv * (r * r * r) * jnp.mean(w * xv, axis=1, keepdims=True)
        dg_ref[...] += jnp.sum(dy * xv * r, axis=0, keepdims=True)

    return pl.pallas_call(
        body, name=name,
        out_shape=(jax.ShapeDtypeStruct((1, LANE), F32), jax.ShapeDtypeStruct((t, d), F32),
                   jax.ShapeDtypeStruct((1, d), F32)),
        grid=(t // ROW_TILE,), in_specs=[_row_spec(d), _vec_spec(d), _row_spec(d)],
        out_specs=(_vec_spec(LANE), _row_spec(d), _vec_spec(d)), compiler_params=_params(("arbitrary",)),
    )(x, g, tgt)


def _swiglu_fwd(gg, uu, name):
    t, f = gg.shape

    def body(g_ref, u_ref, o_ref):
        gv = g_ref[...]
        o_ref[...] = (gv * _sig(gv) * u_ref[...]).astype(BF16)

    return pl.pallas_call(
        body, name=name, out_shape=jax.ShapeDtypeStruct((t, f), BF16), grid=(t // ROW_TILE,),
        in_specs=[_row_spec(f), _row_spec(f)], out_specs=_row_spec(f), compiler_params=_params(("parallel",)),
    )(gg, uu)


def _swiglu_bwd(gg, uu, dact, name):
    t, f = gg.shape

    def body(g_ref, u_ref, d_ref, dg_ref, du_ref):
        gv = g_ref[...]
        dv = d_ref[...]
        s = _sig(gv)
        dg_ref[...] = (dv * u_ref[...] * s * (1.0 + gv * (1.0 - s))).astype(BF16)
        du_ref[...] = (dv * gv * s).astype(BF16)

    return pl.pallas_call(
        body, name=name,
        out_shape=(jax.ShapeDtypeStruct((t, f), BF16), jax.ShapeDtypeStruct((t, f), BF16)), grid=(t // ROW_TILE,),
        in_specs=[_row_spec(f)] * 3, out_specs=(_row_spec(f), _row_spec(f)), compiler_params=_params(("parallel",)),
    )(gg, uu, dact)


MERGE_COLS = 256


def _gate_specs():
    nb = D_MODEL // MERGE_COLS
    base = OFF_GL // MERGE_COLS
    return [pl.BlockSpec((ROW_TILE, MERGE_COLS), functools.partial(lambda i, j, kk: (i, base + nb * kk + j), kk=kk))
            for kk in range(4)]


def _merge_fwd(proj, ys, name):
    t = proj.shape[0]
    yspec = pl.BlockSpec((ROW_TILE, MERGE_COLS), lambda i, j: (i, j))

    def body(g0, g1, g2, g3, y0, y1, y2, y3, o_ref):
        acc = _sig(g0[...]) * y0[...]
        acc += _sig(g1[...]) * y1[...]
        acc += _sig(g2[...]) * y2[...]
        acc += _sig(g3[...]) * y3[...]
        o_ref[...] = acc.astype(BF16)

    return pl.pallas_call(
        body, name=name, out_shape=jax.ShapeDtypeStruct((t, D_MODEL), BF16),
        grid=(t // ROW_TILE, D_MODEL // MERGE_COLS), in_specs=_gate_specs() + [yspec] * 4, out_specs=yspec,
        compiler_params=_params(("parallel", "parallel")),
    )(proj, proj, proj, proj, *ys)


def _merge_bwd(proj, ys, dmerged, name):
    t = proj.shape[0]
    yspec = pl.BlockSpec((ROW_TILE, MERGE_COLS), lambda i, j: (i, j))

    def body(g0, g1, g2, g3, y0, y1, y2, y3, dm_ref, *outs):
        dm = dm_ref[...]
        for gr, yr, dy_ref, dg_ref in zip((g0, g1, g2, g3), (y0, y1, y2, y3), outs[:4], outs[4:]):
            s = _sig(gr[...])
            dy_ref[...] = (dm * s).astype(BF16)
            dg_ref[...] = (dm * yr[...] * s * (1.0 - s)).astype(BF16)

    shp = jax.ShapeDtypeStruct((t, D_MODEL), BF16)
    outs = pl.pallas_call(
        body, name=name, out_shape=(shp,) * 8, grid=(t // ROW_TILE, D_MODEL // MERGE_COLS),
        in_specs=_gate_specs() + [yspec] * 5, out_specs=(yspec,) * 8, compiler_params=_params(("parallel", "parallel")),
    )(proj, proj, proj, proj, *ys, dmerged)
    return outs[:4], outs[4:]


def _ln_silu_fwd(cd, g, b, name):
    t, c = cd.shape

    def body(x_ref, g_ref, b_ref, o_ref):
        xv = x_ref[...]
        mu = jnp.mean(xv, axis=1, keepdims=True)
        xc = xv - mu
        rs = lax.rsqrt(jnp.mean(xc * xc, axis=1, keepdims=True) + EPS)
        z = xc * rs * g_ref[...] + b_ref[...]
        o_ref[...] = (z * _sig(z)).astype(BF16)

    return pl.pallas_call(
        body, name=name, out_shape=jax.ShapeDtypeStruct((t, c), BF16), grid=(t // ROW_TILE,),
        in_specs=[_row_spec(c), _vec_spec(c), _vec_spec(c)], out_specs=_row_spec(c),
        compiler_params=_params(("parallel",)),
    )(cd, g, b)


def _ln_silu_bwd(cd, g, b, dy, name):
    t, c = cd.shape

    def body(x_ref, g_ref, b_ref, dy_ref, dx_ref, dg_ref, db_ref):
        @pl.when(pl.program_id(0) == 0)
        def _():
            dg_ref[...] = jnp.zeros_like(dg_ref)
            db_ref[...] = jnp.zeros_like(db_ref)

        xv = x_ref[...]
        gv = g_ref[...]
        mu = jnp.mean(xv, axis=1, keepdims=True)
        xc = xv - mu
        rs = lax.rsqrt(jnp.mean(xc * xc, axis=1, keepdims=True) + EPS)
        xh = xc * rs
        z = xh * gv + b_ref[...]
        s = _sig(z)
        dz = dy_ref[...] * s * (1.0 + z * (1.0 - s))
        dg_ref[...] += jnp.sum(dz * xh, axis=0, keepdims=True)
        db_ref[...] += jnp.sum(dz, axis=0, keepdims=True)
        dxh = dz * gv
        dx_ref[...] = rs * (dxh - jnp.mean(dxh, axis=1, keepdims=True) - xh * jnp.mean(dxh * xh, axis=1, keepdims=True))

    return pl.pallas_call(
        body, name=name,
        out_shape=(jax.ShapeDtypeStruct((t, c), F32), jax.ShapeDtypeStruct((1, c), F32),
                   jax.ShapeDtypeStruct((1, c), F32)),
        grid=(t // ROW_TILE,), in_specs=[_row_spec(c), _vec_spec(c), _vec_spec(c), _row_spec(c)],
        out_specs=(_row_spec(c), _vec_spec(c), _vec_spec(c)), compiler_params=_params(("arbitrary",)),
    )(cd, g, b, dy)


def _shift_dn(x, k):
    if k == 0:
        return x
    row = lax.broadcasted_iota(jnp.int32, x.shape, 0)
    return jnp.where(row >= k, pltpu.roll(x, k, 0), 0.0)


def _shift_up(x, k):
    if k == 0:
        return x
    t = x.shape[0]
    row = lax.broadcasted_iota(jnp.int32, x.shape, 0)
    return jnp.where(row < t - k, pltpu.roll(x, t - k, 0), 0.0)


def _conv_fwd(x, w_ref, taps):
    acc = w_ref[pl.ds(taps - 1, 1), :] * x
    for k in range(taps - 1):
        acc += w_ref[pl.ds(k, 1), :] * _shift_dn(x, taps - 1 - k)
    return acc


def _conv_bwd(x, dy, w_ref, dw_ref, taps):
    dx = w_ref[pl.ds(taps - 1, 1), :] * dy
    dw_ref[pl.ds(taps - 1, 1), :] = jnp.sum(dy * x, axis=0, keepdims=True)
    for k in range(taps - 1):
        s = taps - 1 - k
        dx += w_ref[pl.ds(k, 1), :] * _shift_up(dy, s)
        dw_ref[pl.ds(k, 1), :] = jnp.sum(dy * _shift_dn(x, s), axis=0, keepdims=True)
    return dx


def _scan_fwd(a, u):
    t = a.shape[0]
    k = 1
    while k < t:
        u = u + a * _shift_dn(u, k)
        if 2 * k < t:
            a = a * _shift_dn(a, k)
        k *= 2
    return u


def _scan_rev(a, u):
    t = a.shape[0]
    k = 1
    while k < t:
        u = u + a * _shift_up(u, k)
        if 2 * k < t:
            a = a * _shift_up(a, k)
        k *= 2
    return u


def _one_minus_exp(y):
    return jnp.where(y > -1e-3, -(y + 0.5 * y * y + (1.0 / 6.0) * y * y * y), 1.0 - jnp.exp(y))


GELU_C = math.sqrt(2.0 / math.pi)


def _gelu(x):
    th = jnp.tanh(GELU_C * (x + 0.044715 * x * x * x))
    return 0.5 * x * (1.0 + th), th


def _softplus(x):
    return jnp.maximum(x, 0.0) + jnp.log(1.0 + jnp.exp(-jnp.abs(x)))


def _chunk_spec(t, blk0):
    return pl.BlockSpec((t, LANE), functools.partial(lambda c, b: (0, b + c), b=blk0))


def _tap_spec(taps):
    return pl.BlockSpec((taps, LANE), lambda c: (0, c))


def _cvec_spec():
    return pl.BlockSpec((1, LANE), lambda c: (0, c))


def _cmat_spec():
    return pl.BlockSpec((1, LANE, LANE), lambda c: (c, 0, 0))


def _lru_forward(ax, wA_ref, bA_ref, wx_ref, bx_ref, wa_ref, ba_ref, lam_ref):
    ca = _conv_fwd(ax, wA_ref, CONV_A) + bA_ref[...]
    gi = _sig(_dot(ca, wx_ref[0], NN) + bx_ref[...])
    gr = _sig(_dot(ca, wa_ref[0], NN) + ba_ref[...])
    sp = _softplus(-lam_ref[...])
    la = -LRU_C * gr * sp
    a = jnp.exp(la)
    mult = jnp.sqrt(_one_minus_exp(2.0 * la))
    h = _scan_fwd(a, ca * gi * mult)
    return ca, gi, gr, sp, a, mult, h


def _a_fwd(proj, wA, bA, wx, bx, wa, ba, lam, name):
    t = proj.shape[0]

    def body(ax_ref, ag_ref, wA_ref, bA_ref, wx_ref, bx_ref, wa_ref, ba_ref, lam_ref, o_ref):
        h = _lru_forward(ax_ref[...], wA_ref, bA_ref, wx_ref, bx_ref, wa_ref, ba_ref, lam_ref)[-1]
        o_ref[...] = (h * _gelu(ag_ref[...])[0]).astype(BF16)

    return pl.pallas_call(
        body, name=name, out_shape=jax.ShapeDtypeStruct((t, BW), BF16), grid=(BW // LANE,),
        in_specs=[_chunk_spec(t, C_AX), _chunk_spec(t, C_AG), _tap_spec(CONV_A), _cvec_spec(), _cmat_spec(),
                  _cvec_spec(), _cmat_spec(), _cvec_spec(), _cvec_spec()],
        out_specs=_chunk_spec(t, 0), compiler_params=_params(("parallel",)),
    )(proj, proj, wA, bA, wx, bx, wa, ba, lam)


def _a_bwd(proj, dya, wA, bA, wx, bx, wa, ba, lam, name):
    t = proj.shape[0]

    def body(ax_ref, ag_ref, dy_ref, wA_ref, bA_ref, wx_ref, bx_ref, wa_ref, ba_ref, lam_ref,
             dax_ref, dag_ref, dwA_ref, dbA_ref, dwx_ref, dbx_ref, dwa_ref, dba_ref, dlam_ref):
        ax = ax_ref[...]
        ag = ag_ref[...]
        dy = dy_ref[...]
        ca, gi, gr, sp, a, mult, h = _lru_forward(ax, wA_ref, bA_ref, wx_ref, bx_ref, wa_ref, ba_ref, lam_ref)
        gel, th = _gelu(ag)
        dgel = 0.5 * (1.0 + th) + 0.5 * ag * (1.0 - th * th) * GELU_C * (1.0 + 3.0 * 0.044715 * ag * ag)
        dag_ref[...] = (dy * h * dgel).astype(BF16)
        s = _scan_rev(_shift_up(a, 1), dy * gel)
        da = s * _shift_dn(h, 1)
        dca = s * gi * mult
        dgi = s * ca * mult
        dmult = s * ca * gi
        dla = da * a - dmult * a * a / mult
        dgr = dla * (-LRU_C * sp)
        dsp = jnp.sum(dla * (-LRU_C * gr), axis=0, keepdims=True)
        dlam_ref[...] = -_sig(-lam_ref[...]) * dsp
        dzi = dgi * gi * (1.0 - gi)
        dzr = dgr * gr * (1.0 - gr)
        dbx_ref[...] = jnp.sum(dzi, axis=0, keepdims=True)
        dba_ref[...] = jnp.sum(dzr, axis=0, keepdims=True)
        dwx_ref[0] = _dot(ca, dzi, TN)
        dwa_ref[0] = _dot(ca, dzr, TN)
        dca += _dot(dzi, wx_ref[0], NT) + _dot(dzr, wa_ref[0], NT)
        dbA_ref[...] = jnp.sum(dca, axis=0, keepdims=True)
        dax_ref[...] = _conv_bwd(ax, dca, wA_ref, dwA_ref, CONV_A).astype(BF16)

    big = jax.ShapeDtypeStruct((t, BW), BF16)
    vec = jax.ShapeDtypeStruct((1, BW), F32)
    mat = jax.ShapeDtypeStruct((BW // LANE, LANE, LANE), F32)
    return pl.pallas_call(
        body, name=name,
        out_shape=(big, big, jax.ShapeDtypeStruct((CONV_A, BW), F32), vec, mat, vec, mat, vec, vec),
        grid=(BW // LANE,),
        in_specs=[_chunk_spec(t, C_AX), _chunk_spec(t, C_AG), _chunk_spec(t, 0), _tap_spec(CONV_A), _cvec_spec(),
                  _cmat_spec(), _cvec_spec(), _cmat_spec(), _cvec_spec(), _cvec_spec()],
        out_specs=(_chunk_spec(t, 0), _chunk_spec(t, 0), _tap_spec(CONV_A), _cvec_spec(), _cmat_spec(), _cvec_spec(),
                   _cmat_spec(), _cvec_spec(), _cvec_spec()),
        compiler_params=_params(("parallel",)),
    )(proj, proj, dya, wA, bA, wx, bx, wa, ba, lam)


def _b_fwd(proj, wB, name):
    t = proj.shape[0]

    def body(bv_ref, bc_ref, bb_ref, w_ref, o_ref):
        o_ref[...] = (bb_ref[...] * _conv_fwd(bc_ref[...] * bv_ref[...], w_ref, CONV_B)).astype(BF16)

    return pl.pallas_call(
        body, name=name, out_shape=jax.ShapeDtypeStruct((t, BW), BF16), grid=(BW // LANE,),
        in_specs=[_chunk_spec(t, C_BV), _chunk_spec(t, C_BC), _chunk_spec(t, C_BB), _tap_spec(CONV_B)],
        out_specs=_chunk_spec(t, 0), compiler_params=_params(("parallel",)),
    )(proj, proj, proj, wB)


def _b_bwd(proj, dyb, wB, name):
    t = proj.shape[0]

    def body(bv_ref, bc_ref, bb_ref, dy_ref, w_ref, dbv_ref, dbc_ref, dbb_ref, dw_ref):
        bv = bv_ref[...]
        bc = bc_ref[...]
        dy = dy_ref[...]
        p = bc * bv
        dbb_ref[...] = (dy * _conv_fwd(p, w_ref, CONV_B)).astype(BF16)
        dp = _conv_bwd(p, dy * bb_ref[...], w_ref, dw_ref, CONV_B)
        dbc_ref[...] = (dp * bv).astype(BF16)
        dbv_ref[...] = (dp * bc).astype(BF16)

    big = jax.ShapeDtypeStruct((t, BW), BF16)
    return pl.pallas_call(
        body, name=name, out_shape=(big, big, big, jax.ShapeDtypeStruct((CONV_B, BW), F32)), grid=(BW // LANE,),
        in_specs=[_chunk_spec(t, C_BV), _chunk_spec(t, C_BC), _chunk_spec(t, C_BB), _chunk_spec(t, 0),
                  _tap_spec(CONV_B)],
        out_specs=(_chunk_spec(t, 0),) * 3 + (_tap_spec(CONV_B),), compiler_params=_params(("parallel",)),
    )(proj, proj, proj, dyb, wB)


def _d_conv_fwd(proj, wD, bD, name):
    t = proj.shape[0]

    def body(d1_ref, d2_ref, w_ref, b_ref, o_ref):
        o_ref[...] = _conv_fwd(d1_ref[...] * _sig(d2_ref[...]), w_ref, CONV_D) + b_ref[...]

    return pl.pallas_call(
        body, name=name, out_shape=jax.ShapeDtypeStruct((t, BW), F32), grid=(BW // LANE,),
        in_specs=[_chunk_spec(t, C_D1), _chunk_spec(t, C_D2), _tap_spec(CONV_D), _cvec_spec()],
        out_specs=_chunk_spec(t, 0), compiler_params=_params(("parallel",)),
    )(proj, proj, wD, bD)


def _d_conv_bwd(proj, dcd, wD, name):
    t = proj.shape[0]

    def body(d1_ref, d2_ref, dy_ref, w_ref, dd1_ref, dd2_ref, dw_ref, db_ref):
        d1 = d1_ref[...]
        s = _sig(d2_ref[...])
        dy = dy_ref[...]
        db_ref[...] = jnp.sum(dy, axis=0, keepdims=True)
        dd = _conv_bwd(d1 * s, dy, w_ref, dw_ref, CONV_D)
        dd1_ref[...] = (dd * s).astype(BF16)
        dd2_ref[...] = (dd * d1 * s * (1.0 - s)).astype(BF16)

    big = jax.ShapeDtypeStruct((t, BW), BF16)
    return pl.pallas_call(
        body, name=name,
        out_shape=(big, big, jax.ShapeDtypeStruct((CONV_D, BW), F32), jax.ShapeDtypeStruct((1, BW), F32)),
        grid=(BW // LANE,),
        in_specs=[_chunk_spec(t, C_D1), _chunk_spec(t, C_D2), _chunk_spec(t, 0), _tap_spec(CONV_D)],
        out_specs=(_chunk_spec(t, 0), _chunk_spec(t, 0), _tap_spec(CONV_D), _cvec_spec()),
        compiler_params=_params(("parallel",)),
    )(proj, proj, dcd, wD)


SCALE = HEAD_DIM ** -0.5
GROUP = N_Q // N_KV


GROWS = GROUP * BLK


def _per_head(ss_ref, row, g):
    head = lax.broadcasted_iota(jnp.int32, (GROWS, 1), 0) // BLK
    col = jnp.full((GROWS, 1), ss_ref[row, g * GROUP + GROUP - 1], F32)
    for i in range(GROUP - 1):
        col = jnp.where(head == i, ss_ref[row, g * GROUP + i], col)
    return col


def _attn_probs(q_ref, k_ref, ss_ref, g, n):
    qi = lax.broadcasted_iota(jnp.int32, (GROWS, BLK), 0) % BLK
    ki = lax.broadcasted_iota(jnp.int32, (GROWS, BLK), 1)
    dist = (qi - ki).astype(F32)
    sink = _per_head(ss_ref, 0, g)
    slope = _per_head(ss_ref, 1, g)
    s0 = pl.multiple_of(n * BLK, BLK)
    sp = pl.multiple_of(jnp.maximum(n - 1, 0) * BLK, BLK)
    q = q_ref[:, pl.ds(s0, BLK), :].reshape(GROWS, HEAD_DIM)
    kc = k_ref[0, pl.ds(s0, BLK), :]
    kp = k_ref[0, pl.ds(sp, BLK), :]
    sc = jnp.where(ki <= qi, _dot(q, kc, NT) * SCALE - slope * dist, NEG_INF)
    first = jnp.where(n >= 1, 0, BLK)
    sv = jnp.where(ki > qi + first, _dot(q, kp, NT) * SCALE - slope * (dist + BLK), NEG_INF)
    m = jnp.maximum(jnp.maximum(jnp.max(sc, axis=1, keepdims=True), jnp.max(sv, axis=1, keepdims=True)), sink)
    pc = jnp.exp(sc - m)
    pp = jnp.exp(sv - m)
    ps = jnp.exp(sink - m)
    z = jnp.sum(pc, axis=1, keepdims=True) + jnp.sum(pp, axis=1, keepdims=True) + ps
    return s0, sp, q, kc, kp, pc, pp, ps, z


def _attn_specs(t):
    qs = pl.BlockSpec((GROUP, t, HEAD_DIM), lambda g: (g, 0, 0))
    ks = pl.BlockSpec((1, t, HEAD_DIM), lambda g: (g, 0, 0))
    ss = pl.BlockSpec(memory_space=pltpu.SMEM)
    return qs, ks, ss


def _attn_fwd(q, k, v, ss, name):
    t = q.shape[1]
    qs, ks, sspec = _attn_specs(t)

    def body(q_ref, k_ref, v_ref, ss_ref, o_ref):
        g = pl.program_id(0)

        def blk(n, carry):
            s0, sp, _, _, _, pc, pp, _, z = _attn_probs(q_ref, k_ref, ss_ref, g, n)
            o = _dot(pc, v_ref[0, pl.ds(s0, BLK), :], NN) + _dot(pp, v_ref[0, pl.ds(sp, BLK), :], NN)
            o_ref[:, pl.ds(s0, BLK), :] = (o / z).astype(BF16).reshape(GROUP, BLK, HEAD_DIM)
            return carry

        lax.fori_loop(0, t // BLK, blk, 0)

    return pl.pallas_call(
        body, name=name, out_shape=jax.ShapeDtypeStruct((N_Q, t, HEAD_DIM), BF16), grid=(N_KV,),
        in_specs=[qs, ks, ks, sspec], out_specs=qs, compiler_params=_params(("parallel",)),
    )(q, k, v, ss)


def _attn_bwd(q, k, v, do, ss, name):
    t = q.shape[1]
    qs, ks, sspec = _attn_specs(t)

    def body(q_ref, k_ref, v_ref, do_ref, ss_ref, dq_ref, dk_ref, dv_ref, ds_ref):
        g = pl.program_id(0)
        dk_ref[...] = jnp.zeros_like(dk_ref)
        dv_ref[...] = jnp.zeros_like(dv_ref)

        def blk(n, dsink):
            s0, sp, q, kc, kp, pc, pp, ps, z = _attn_probs(q_ref, k_ref, ss_ref, g, n)
            rz = 1.0 / z
            pc = pc * rz
            pp = pp * rz
            do_b = do_ref[:, pl.ds(s0, BLK), :].reshape(GROWS, HEAD_DIM)
            dpc = _dot(do_b, v_ref[0, pl.ds(s0, BLK), :], NT)
            dpp = _dot(do_b, v_ref[0, pl.ds(sp, BLK), :], NT)
            delta = jnp.sum(pc * dpc, axis=1, keepdims=True) + jnp.sum(pp * dpp, axis=1, keepdims=True)
            dsc = pc * (dpc - delta)
            dsp = pp * (dpp - delta)
            dq = (_dot(dsc, kc, NN) + _dot(dsp, kp, NN)) * SCALE
            dq_ref[:, pl.ds(s0, BLK), :] = dq.astype(BF16).reshape(GROUP, BLK, HEAD_DIM)
            dk_ref[0, pl.ds(s0, BLK), :] += _dot(dsc, q, TN) * SCALE
            dk_ref[0, pl.ds(sp, BLK), :] += _dot(dsp, q, TN) * SCALE
            dv_ref[0, pl.ds(s0, BLK), :] += _dot(pc, do_b, TN)
            dv_ref[0, pl.ds(sp, BLK), :] += _dot(pp, do_b, TN)
            return dsink - ps * rz * delta

        dsink = lax.fori_loop(0, t // BLK, blk, jnp.zeros((GROWS, 1), F32))
        for i in range(GROUP):
            ds_ref[i] = jnp.full(ds_ref.shape[1:], jnp.sum(dsink[i * BLK:(i + 1) * BLK]), F32)

    kv = jax.ShapeDtypeStruct((N_KV, t, HEAD_DIM), F32)
    return pl.pallas_call(
        body, name=name,
        out_shape=(jax.ShapeDtypeStruct((N_Q, t, HEAD_DIM), BF16), kv, kv, jax.ShapeDtypeStruct((N_Q, 8, LANE), F32)),
        grid=(N_KV,), in_specs=[qs, ks, ks, qs, sspec],
        out_specs=(qs, ks, ks, pl.BlockSpec((GROUP, 8, LANE), lambda g: (g, 0, 0))),
        compiler_params=_params(("parallel",)),
    )(q, k, v, do, ss)


def _heads(x2d, n):
    t = x2d.shape[0]
    return x2d.reshape(t, n, HEAD_DIM).transpose(1, 0, 2)


def _unheads(x3d):
    n, t, _ = x3d.shape
    return x3d.transpose(1, 0, 2).reshape(t, n * HEAD_DIM)


SMALL_ELEMS = 256 * 1024
TILE_ELEMS = 640 * 1024


def _row_tile(r, c):
    if r * c <= SMALL_ELEMS:
        return r
    return _pick(r, [t for t in (512, 256, 128, 64, 32, 16, 8) if t * c <= TILE_ELEMS])


def _adamw(w, g, m, v, name):
    r, c = w.shape
    tr = _row_tile(r, c)
    spec = pl.BlockSpec((tr, c), lambda i: (i, 0))

    def body(w_ref, g_ref, m_ref, v_ref, d_ref, nm_ref, nv_ref):
        gv = g_ref[...]
        nm = ADAM_B1 * m_ref[...] + (1.0 - ADAM_B1) * gv
        nv = ADAM_B2 * v_ref[...] + (1.0 - ADAM_B2) * (gv * gv)
        m_hat = nm / (1.0 - ADAM_B1 ** ADAM_STEP)
        v_hat = nv / (1.0 - ADAM_B2 ** ADAM_STEP)
        d_ref[...] = -ADAM_LR * (m_hat / (jnp.sqrt(v_hat) + ADAM_EPS) + ADAM_WD * w_ref[...])
        nm_ref[...] = nm
        nv_ref[...] = nv

    shp = jax.ShapeDtypeStruct((r, c), F32)
    return pl.pallas_call(
        body, name=name, out_shape=(shp, shp, shp), grid=(r // tr,), in_specs=[spec] * 4, out_specs=(spec,) * 3,
        compiler_params=_params(("parallel",)),
    )(w, g, m, v)


def _sum_leading(x, name):
    n, r, c = x.shape
    tr = _row_tile(r, c)

    def body(x_ref, o_ref):
        acc = x_ref[0]
        for i in range(1, n):
            acc = acc + x_ref[i]
        o_ref[...] = acc

    return pl.pallas_call(
        body, name=name, out_shape=jax.ShapeDtypeStruct((r, c), F32), grid=(r // tr,),
        in_specs=[pl.BlockSpec((n, tr, c), lambda i: (0, i, 0))], out_specs=pl.BlockSpec((tr, c), lambda i: (i, 0)),
        compiler_params=_params(("parallel",)),
    )(x)


def _sum_own_plus(p, sel, recv, name, out_dtype):
    _, r, c = p.shape
    n = recv.shape[0]
    tr = _pick(r, (512, 448, 256, 128, 64, 16))

    def body(sel_ref, p_ref, r_ref, o_ref):
        acc = p_ref[0].astype(F32)
        for i in range(n):
            acc = acc + r_ref[i].astype(F32)
        o_ref[...] = acc.astype(out_dtype)

    grid_spec = pltpu.PrefetchScalarGridSpec(
        num_scalar_prefetch=1, grid=(r // tr,),
        in_specs=[pl.BlockSpec((1, tr, c), lambda i, s: (s[0], i, 0)), pl.BlockSpec((n, tr, c), lambda i, s: (0, i, 0))],
        out_specs=pl.BlockSpec((tr, c), lambda i, s: (i, 0)))
    return pl.pallas_call(
        body, name=name, out_shape=jax.ShapeDtypeStruct((r, c), out_dtype), grid_spec=grid_spec,
        compiler_params=_params(("parallel",)),
    )(sel, p, recv)


def _coords():
    return lax.axis_index("x"), lax.axis_index("y"), lax.axis_index("c")


def _allgather8(x2, name, space):
    _, m, n = x2.shape

    def body(x_ref, out_ref, send_sems, recv_sems, local_sem):
        x, y, c = _coords()
        me, sibling = (x, y, c), (x, y, 1 - c)
        chips = [(1 - x, y), (x, 1 - y), (1 - x, 1 - y)]
        mine_src = x_ref.at[c]

        def rows(px, py, pc):
            return out_ref.at[4 * px + 2 * py + pc]

        def copy(k, block, to, src=None):
            return pltpu.make_async_remote_copy(
                src_ref=rows(*block) if src is None else src, dst_ref=rows(*block),
                send_sem=send_sems.at[k], recv_sem=recv_sems.at[k], device_id=to, device_id_type=MESH)

        mine = pltpu.make_async_copy(mine_src, rows(*me), local_sem)
        mine.start()
        first = [copy(0, me, sibling, src=mine_src)]
        first += [copy(1 + j, me, (*chip, c), src=mine_src) for j, chip in enumerate(chips)]
        for cp in first:
            cp.start()
        passed = [copy(4 + j, (*chip, c), sibling) for j, chip in enumerate(chips)]
        for j, chip in enumerate(chips):
            copy(1 + j, (*chip, c), me).wait_recv()
            passed[j].start()
        copy(0, sibling, me).wait_recv()
        for j, chip in enumerate(chips):
            copy(4 + j, (*chip, 1 - c), me).wait_recv()
        for cp in first + passed:
            cp.wait_send()
        mine.wait()

    return pl.pallas_call(
        body, name=name, out_shape=jax.ShapeDtypeStruct((8, m, n), x2.dtype),
        in_specs=[pl.BlockSpec(memory_space=space)], out_specs=pl.BlockSpec(memory_space=space),
        scratch_shapes=[pltpu.SemaphoreType.DMA((7,)), pltpu.SemaphoreType.DMA((7,)), pltpu.SemaphoreType.DMA],
        compiler_params=pltpu.CompilerParams(vmem_limit_bytes=VMEM_LIMIT),
    )(x2)


N_REG = len(ROW_REGIONS) + 1


def _chip_window(ref, lead, r, j):
    view = ref if lead is None else ref.at[lead]
    if r < len(ROW_REGIONS):
        off, rows = ROW_REGIONS[r]
        return view.at[pl.ds(pl.multiple_of(off + j * rows, 16), rows), :]
    return view.at[pl.ds(R_OUT, OUT_ROWS), pl.ds(pl.multiple_of(j * OUT_COLS, LANE), OUT_COLS)]


HBM_SPEC = pl.BlockSpec(memory_space=pltpu.HBM)
SEM_SPEC = pl.BlockSpec(memory_space=pltpu.SEMAPHORE)


def _half_window(ref, r, j, h):
    if r < len(ROW_REGIONS):
        off, rows = ROW_REGIONS[r]
        return ref.at[pl.ds(pl.multiple_of(off + j * rows + h * (rows // 2), 16), rows // 2), :]
    half = OUT_ROWS // 2
    return ref.at[pl.ds(pl.multiple_of(R_OUT + h * half, 16), half),
                  pl.ds(pl.multiple_of(j * OUT_COLS, LANE), OUT_COLS)]


def _other_chips():
    x, y, _ = _coords()
    return [(1 - x, y), (x, 1 - y), (1 - x, 1 - y)]


def _ici_copies(srcs, arena_ref, send_sems, recv_sems, regions):
    x, y, c = _coords()
    sends, arrivals = [], []
    for k, (cx, cy) in enumerate(_other_chips()):
        for r in regions:
            def remote(src, j):
                return pltpu.make_async_remote_copy(
                    src_ref=src, dst_ref=_half_window(arena_ref, r, j, c), send_sem=send_sems.at[3 * r + k],
                    recv_sem=recv_sems.at[3 * r + k], device_id=(cx, cy, c), device_id_type=MESH)
            rows = srcs[r].shape[0] // 2
            sends.append(remote(srcs[r].at[pl.ds(pl.multiple_of(c * rows, 16), rows), :], 2 * x + y))
            arrivals.append(remote(_half_window(arena_ref, r, 2 * cx + cy, c), 2 * cx + cy))
    return sends, arrivals


def _sibling_copies(srcs, arena_ref, send_sems, recv_sems, regions):
    x, y, c = _coords()
    sends, arrivals = [], []

    def remote(win, r, k, src=None):
        return pltpu.make_async_remote_copy(
            src_ref=win if src is None else src, dst_ref=win, send_sem=send_sems.at[r, k],
            recv_sem=recv_sems.at[r, k], device_id=(x, y, 1 - c), device_id_type=MESH)

    for k, (cx, cy) in enumerate(_other_chips()):
        for r in regions:
            sends.append(remote(_half_window(arena_ref, r, 2 * cx + cy, c), r, k))
            arrivals.append(remote(_half_window(arena_ref, r, 2 * cx + cy, 1 - c), r, k))
    for r in regions:
        own = _chip_window(arena_ref, None, r, 2 * x + y)
        sends.append(remote(own, r, 3, src=srcs[r]))
        arrivals.append(remote(own, r, 3))
    return sends, arrivals


ICI_SEMS = pltpu.SemaphoreType.DMA((3 * N_REG,))
SIBLING_SEMS = pltpu.SemaphoreType.DMA((N_REG, 4))
ARENA_SHAPE = (ARENA_ROWS, ARENA_W)
ALL_REGIONS = tuple(range(N_REG))
IN_REGION = (3,)
REST_REGIONS = (0, 1, 2, 4, 5)


def _gather_layer(shards, name, regions=ALL_REGIONS):
    def body(*refs):
        srcs, arena_ref = refs[:N_REG], refs[N_REG]
        ici_send, ici_recv, sib_send, sib_recv = refs[N_REG + 1:]
        sends, arrivals = _ici_copies(srcs, arena_ref, ici_send, ici_recv, regions)
        passes, landings = _sibling_copies(srcs, arena_ref, sib_send, sib_recv, regions)
        for cp in sends + passes[len(arrivals):]:
            cp.start()
        for arrival, onward in zip(arrivals, passes):
            arrival.wait_recv()
            onward.start()
        for cp in landings:
            cp.wait_recv()
        for cp in sends + passes:
            cp.wait_send()

    return pl.pallas_call(
        body, name=name, out_shape=jax.ShapeDtypeStruct(ARENA_SHAPE, BF16),
        in_specs=[pl.BlockSpec(memory_space=pl.ANY)] * N_REG, out_specs=pl.BlockSpec(memory_space=pl.ANY),
        scratch_shapes=[ICI_SEMS, ICI_SEMS, SIBLING_SEMS, SIBLING_SEMS],
    )(*shards)


def _gather_start(shards, after, name, regions=ALL_REGIONS):
    def body(*refs):
        srcs, arena_ref = refs[:N_REG], refs[N_REG]
        send_sems, recv_sems = refs[N_REG + 2], refs[N_REG + 3]
        token = refs[-1]
        for cp in _ici_copies(srcs, arena_ref, send_sems, recv_sems, regions)[0]:
            cp.start()
        token[...] = jnp.zeros_like(token)

    hbm = lambda a: pltpu.with_memory_space_constraint(a, pltpu.HBM)
    outs = pl.pallas_call(
        body, name=name,
        out_shape=(ICI_SEMS, ICI_SEMS, *[pltpu.HBM(s.shape, s.dtype) for s in shards],
                   pltpu.HBM(ARENA_SHAPE, BF16), pltpu.HBM(after.shape, after.dtype),
                   jax.ShapeDtypeStruct((8, LANE), F32)),
        in_specs=[HBM_SPEC] * (N_REG + 2),
        out_specs=(SEM_SPEC, SEM_SPEC, *[HBM_SPEC] * (N_REG + 2), pl.BlockSpec(memory_space=pltpu.VMEM)),
        input_output_aliases={i: 2 + i for i in range(N_REG + 2)},
        compiler_params=pltpu.CompilerParams(has_side_effects=pltpu.SideEffectType.DATAFLOW_SIDE_EFFECTING),
    )(*[hbm(s) for s in shards], hbm(lax.empty(ARENA_SHAPE, BF16)), hbm(after))
    return outs[0], outs[1], outs[2:2 + N_REG], outs[2 + N_REG], outs[-1], outs[3 + N_REG]


def _gather_wait(send_sems, recv_sems, shards, arena, after, name, regions=ALL_REGIONS):
    def body(*refs):
        srcs, arena_ref = refs[:N_REG], refs[N_REG]
        sends, arrivals = _ici_copies(srcs, arena_ref, refs[N_REG + 1], refs[N_REG + 2], regions)
        for cp in sends:
            cp.wait_send()
        for cp in arrivals:
            cp.wait_recv()

    outs = pl.pallas_call(
        body, name=name,
        out_shape=(*[pltpu.HBM(s.shape, s.dtype) for s in shards], pltpu.HBM(ARENA_SHAPE, BF16)),
        in_specs=[HBM_SPEC] * (N_REG + 1) + [SEM_SPEC, SEM_SPEC, pl.BlockSpec(memory_space=pl.ANY)],
        out_specs=(HBM_SPEC,) * (N_REG + 1), input_output_aliases={i: i for i in range(N_REG + 1)},
        compiler_params=pltpu.CompilerParams(has_side_effects=pltpu.SideEffectType.DATAFLOW_SIDE_EFFECTING),
    )(*shards, arena, send_sems, recv_sems, after)
    return outs[:N_REG], outs[N_REG]


def _gather_finish(shards, arena, name, regions=ALL_REGIONS):
    def body(*refs):
        srcs, arena_ref = refs[:N_REG], refs[N_REG + 1]
        sends, arrivals = _sibling_copies(srcs, arena_ref, refs[N_REG + 2], refs[N_REG + 3], regions)
        for cp in sends:
            cp.start()
        for cp in arrivals:
            cp.wait_recv()
        for cp in sends:
            cp.wait_send()

    return pl.pallas_call(
        body, name=name, out_shape=jax.ShapeDtypeStruct(ARENA_SHAPE, BF16),
        in_specs=[pl.BlockSpec(memory_space=pl.ANY)] * (N_REG + 1), out_specs=pl.BlockSpec(memory_space=pl.ANY),
        scratch_shapes=[SIBLING_SEMS, SIBLING_SEMS], input_output_aliases={N_REG: 0},
    )(*shards, arena)


HALF_PIECE_OFF = tuple(o // 2 for o in PIECE_OFF)
HALF_PIECE_ROWS = PIECE_ROWS // 2
HALF_OUT_ROWS = OUT_ROWS // 2
SWAP_SEMS = pltpu.SemaphoreType.DMA((4 * N_REG,))
SCATTER_SEMS = pltpu.SemaphoreType.DMA((6,))


def _packed_shapes(slots, dtype):
    return (jax.ShapeDtypeStruct((slots, HALF_PIECE_ROWS, ARENA_W), dtype),
            jax.ShapeDtypeStruct((slots, HALF_OUT_ROWS, OUT_COLS), dtype))


def _swap_halves(ga, name):
    def body(g_ref, main_ref, outp_ref, send_sems, recv_sems):
        x, y, c = _coords()
        cps = []
        for j in range(4):
            for r in range(N_REG):
                if r < len(ROW_REGIONS):
                    dst = main_ref.at[j, pl.ds(HALF_PIECE_OFF[r], ROW_REGIONS[r][1] // 2), :]
                else:
                    dst = outp_ref.at[j]
                cps.append(pltpu.make_async_remote_copy(
                    src_ref=_half_window(g_ref, r, j, 1 - c), dst_ref=dst, send_sem=send_sems.at[j * N_REG + r],
                    recv_sem=recv_sems.at[j * N_REG + r], device_id=(x, y, 1 - c), device_id_type=MESH))
        for cp in cps:
            cp.start()
        for cp in cps:
            cp.wait()

    return pl.pallas_call(
        body, name=name, out_shape=_packed_shapes(4, ga.dtype),
        in_specs=[pl.BlockSpec(memory_space=pl.ANY)], out_specs=(pl.BlockSpec(memory_space=pl.ANY),) * 2,
        scratch_shapes=[SWAP_SEMS, SWAP_SEMS],
    )(ga)


def _own_halves(ga, cc):
    mains = [jnp.concatenate([lax.dynamic_slice(ga, (off + j * rows + cc * (rows // 2), 0), (rows // 2, ARENA_W))
                              for off, rows in ROW_REGIONS]) for j in range(4)]
    outs = [lax.dynamic_slice(ga, (R_OUT + cc * HALF_OUT_ROWS, j * OUT_COLS), (HALF_OUT_ROWS, OUT_COLS))
            for j in range(4)]
    return jnp.stack(mains), jnp.stack(outs)


def _scatter_copies(main_ref, outp_ref, rmain_ref, routp_ref, send_sems, recv_sems):
    _, _, c = _coords()
    cps = []
    for k, (cx, cy) in enumerate(_other_chips()):
        for i, (src, dst) in enumerate(((main_ref, rmain_ref), (outp_ref, routp_ref))):
            cps.append(pltpu.make_async_remote_copy(
                src_ref=src.at[2 * cx + cy], dst_ref=dst.at[k], send_sem=send_sems.at[2 * k + i],
                recv_sem=recv_sems.at[2 * k + i], device_id=(cx, cy, c), device_id_type=MESH))
    return cps


def _scatter_halves(main, outp, name):
    def body(main_ref, outp_ref, rmain_ref, routp_ref, send_sems, recv_sems):
        cps = _scatter_copies(main_ref, outp_ref, rmain_ref, routp_ref, send_sems, recv_sems)
        for cp in cps:
            cp.start()
        for cp in cps:
            cp.wait()

    return pl.pallas_call(
        body, name=name, out_shape=_packed_shapes(3, main.dtype),
        in_specs=[pl.BlockSpec(memory_space=pl.ANY)] * 2, out_specs=(pl.BlockSpec(memory_space=pl.ANY),) * 2,
        scratch_shapes=[SCATTER_SEMS, SCATTER_SEMS],
    )(main, outp)


def _scatter_start(main, outp, name):
    def body(main_ref, outp_ref, rmain_ref, routp_ref, send_sems, recv_sems, *rest):
        for cp in _scatter_copies(main_ref, outp_ref, rmain_ref, routp_ref, send_sems, recv_sems):
            cp.start()
        rest[-1][...] = jnp.zeros_like(rest[-1])

    hbm = lambda a: pltpu.with_memory_space_constraint(a, pltpu.HBM)
    land = [lax.empty(s.shape, s.dtype) for s in _packed_shapes(3, main.dtype)]
    bufs = [main, outp, *land]
    outs = pl.pallas_call(
        body, name=name,
        out_shape=(SCATTER_SEMS, SCATTER_SEMS, *[pltpu.HBM(b.shape, b.dtype) for b in bufs],
                   jax.ShapeDtypeStruct((8, LANE), F32)),
        in_specs=[HBM_SPEC] * 4, out_specs=(SEM_SPEC, SEM_SPEC, *[HBM_SPEC] * 4, pl.BlockSpec(memory_space=pltpu.VMEM)),
        input_output_aliases={i: 2 + i for i in range(4)},
        compiler_params=pltpu.CompilerParams(has_side_effects=pltpu.SideEffectType.DATAFLOW_SIDE_EFFECTING),
    )(*[hbm(b) for b in bufs])
    return outs[0], outs[1], outs[2:6], outs[6]


def _scatter_wait(send_sems, recv_sems, bufs, after, name):
    def body(main_ref, outp_ref, rmain_ref, routp_ref, send_sems, recv_sems, *rest):
        for cp in _scatter_copies(main_ref, outp_ref, rmain_ref, routp_ref, send_sems, recv_sems):
            cp.wait_send()
            cp.wait_recv()

    return pl.pallas_call(
        body, name=name, out_shape=tuple(pltpu.HBM(b.shape, b.dtype) for b in bufs),
        in_specs=[HBM_SPEC] * 4 + [SEM_SPEC, SEM_SPEC, pl.BlockSpec(memory_space=pl.ANY)],
        out_specs=(HBM_SPEC,) * 4, input_output_aliases={i: i for i in range(4)},
        compiler_params=pltpu.CompilerParams(has_side_effects=pltpu.SideEffectType.DATAFLOW_SIDE_EFFECTING),
    )(*bufs, send_sems, recv_sems, after)


def _swap_many(arrs, name):
    n = len(arrs)

    def body(*refs):
        x, y, c = _coords()
        send_sems, recv_sems = refs[2 * n], refs[2 * n + 1]
        cps = [pltpu.make_async_remote_copy(
            src_ref=refs[i], dst_ref=refs[n + i], send_sem=send_sems.at[i], recv_sem=recv_sems.at[i],
            device_id=(x, y, 1 - c), device_id_type=MESH) for i in range(n)]
        for cp in cps:
            cp.start()
        for cp in cps:
            cp.wait()

    return pl.pallas_call(
        body, name=name, out_shape=tuple(jax.ShapeDtypeStruct(a.shape, a.dtype) for a in arrs),
        in_specs=[pl.BlockSpec(memory_space=pl.ANY)] * n, out_specs=(pl.BlockSpec(memory_space=pl.ANY),) * n,
        scratch_shapes=[pltpu.SemaphoreType.DMA((n,)), pltpu.SemaphoreType.DMA((n,))],
    )(*arrs)


def _join_halves(mine, theirs, cc):
    return jnp.where(cc == 0, jnp.concatenate([mine, theirs]), jnp.concatenate([theirs, mine]))


def _reduced_layer(red, sib, cc):
    parts = []
    for (_, rows), off in zip(ROW_REGIONS, HALF_PIECE_OFF):
        parts.append(_join_halves(red[0][off:off + rows // 2], sib[0][off:off + rows // 2], cc))
    return jnp.concatenate(parts), _join_halves(red[1], sib[1], cc)


OUT_NAMES = ("w_a_out", "w_b_out", "w_c_out", "w_d_out")


def _arena_shards(w):
    t = lambda a: a.astype(BF16).transpose(0, 2, 1)
    return (w["w_ffn_down"].astype(BF16), t(w["w_ffn_gate"]), t(w["w_ffn_up"]), t(w["w_in"]), w["w_o"].astype(BF16),
            jnp.concatenate([w[n].astype(BF16) for n in OUT_NAMES], axis=1))


def _shard_grads(main, outp):
    t = lambda r: main[:, PIECE_OFF[r]:PIECE_OFF[r] + ROW_REGIONS[r][1]]
    g = dict(w_ffn_down=t(0), w_ffn_gate=t(1).transpose(0, 2, 1), w_ffn_up=t(2).transpose(0, 2, 1),
             w_in=t(3).transpose(0, 2, 1), w_o=t(4))
    for i, n in enumerate(OUT_NAMES):
        g[n] = outp[:, i * BW:(i + 1) * BW]
    return g


def _gather_taps(p, name):
    mine = jnp.concatenate([p[n] for n in CONV_NAMES], axis=1).reshape(DEPTH * N_TAPS, LANE)
    rows = -(-mine.shape[0] // 8) * 8
    mine = jnp.concatenate([mine, jnp.zeros((rows - mine.shape[0], LANE), F32)])
    g = _allgather8(jnp.stack([mine, mine]), name, pltpu.VMEM)[0::2, :DEPTH * N_TAPS]
    full = g.reshape(4, DEPTH, N_TAPS, LANE).transpose(1, 2, 0, 3).reshape(DEPTH, N_TAPS, BW)
    return dict(conv_a_w=full[:, :CONV_A], conv_b_w=full[:, CONV_A:CONV_A + CONV_B], conv_d_w=full[:, CONV_A + CONV_B:])


def _flat_pack(arrs):
    flat = jnp.concatenate([a.reshape(-1).astype(F32) for a in arrs])
    rows = -(-flat.shape[0] // (8 * LANE)) * 8
    return jnp.concatenate([flat, jnp.zeros((rows * LANE - flat.shape[0],), F32)]).reshape(rows, LANE)


def _flat_unpack(packed, shapes):
    flat, out, off = packed.reshape(-1), [], 0
    for s in shapes:
        cnt = int(np.prod(s))
        out.append(flat[off:off + cnt].reshape(s))
        off += cnt
    return out


def _blockdiag_chunks(w):
    w4 = w.reshape(4, 2, 64, 64)
    z = jnp.zeros((4, 2, 64, 2, 64), F32)
    z = z.at[:, 0, :, 0, :].set(w4[:, 0]).at[:, 1, :, 1, :].set(w4[:, 1])
    return z.reshape(4, LANE, LANE)


def _blockdiag_extract(d):
    d5 = d.reshape(4, 2, 64, 2, 64)
    return jnp.stack([d5[:, 0, :, 0, :], d5[:, 1, :, 1, :]], axis=1).reshape(8, 64, 64)


SLOPES = np.asarray([2.0 ** (-8.0 * (i + 1) / N_Q) for i in range(N_Q)], np.float32)


def _layer_consts(p, fw, l):
    row = lambda a: a[l].reshape(1, -1)
    return dict(
        g1=row(p["norm1_g"]), g2=row(p["norm2_g"]), wA=fw["conv_a_w"][l], bA=row(p["conv_a_b"]),
        wx=_blockdiag_chunks(p["lru_wx"][l]), bx=row(p["lru_bx"]), wa=_blockdiag_chunks(p["lru_wa"][l]),
        ba=row(p["lru_ba"]), lam=row(p["lru_lambda"]), wB=fw["conv_b_w"][l],
        ss=jnp.stack([p["sinks"][l], jnp.asarray(SLOPES)]), wD=fw["conv_d_w"][l], bD=row(p["conv_d_b"]),
        lg=row(p["ln_d_g"]), lb=row(p["ln_d_b"]))


def _layer_fwd(x, c, fw, l, rest_of_weights=None):
    t = f"l{l}_"
    xn = _rms_fwd(x, c["g1"], t + "rms1")
    wt = lambda off, rows: Win(fw["arena"][l], None, off, rows)
    proj = _mm(xn, Win(fw["arena_in"][l], None, R_IN, IN_W), "nt", t + "proj")
    ya = _a_fwd(proj, c["wA"], c["bA"], c["wx"], c["bx"], c["wa"], c["ba"], c["lam"], t + "a_fwd")
    yb = _b_fwd(proj, c["wB"], t + "b_fwd")
    q3 = _heads(proj[:, OFF_Q:OFF_K], N_Q)
    k3 = _heads(proj[:, OFF_K:OFF_V], N_KV)
    v3 = _heads(proj[:, OFF_V:OFF_V + N_KV * HEAD_DIM], N_KV)
    yc = _unheads(_attn_fwd(q3, k3, v3, c["ss"], t + "attn_fwd"))
    cd = _d_conv_fwd(proj, c["wD"], c["bD"], t + "d_conv_fwd")
    yd = _ln_silu_fwd(cd, c["lg"], c["lb"], t + "d_ln_fwd")
    ys = (ya, yb, yc, yd)
    if fw["arena"][l] is None:
        fw["arena"][l] = rest_of_weights(yd)
    big_y = tuple(_mm(y, wt(R_OUT + i * BW, BW), "nn", t + f"out{i}") for i, y in enumerate(ys))
    merged = _merge_fwd(proj, big_y, t + "merge_fwd")
    hres = _mm(merged, wt(R_O, D_MODEL), "nn", t + "wo", add=x)
    hn = _rms_fwd(hres, c["g2"], t + "rms2")
    gg = _mm(hn, wt(R_GATE, D_FF), "nt", t + "ffn_gate")
    uu = _mm(hn, wt(R_UP, D_FF), "nt", t + "ffn_up")
    act = _swiglu_fwd(gg, uu, t + "swiglu_fwd")
    xout = _mm(act, wt(R_DOWN, D_FF), "nn", t + "ffn_down", add=hres)
    saved = dict(x=x, xn=xn, proj=proj, ys=ys, q3=q3, k3=k3, v3=v3, cd=cd, big_y=big_y, merged=merged, hres=hres,
                 hn=hn, gg=gg, uu=uu, act=act)
    return xout, saved


def _layer_bwd(dxout, s, c, fw, l, ga):
    t = f"l{l}_"
    gs = {}
    wt = lambda off, rows: Win(fw["arena"][l], None, off, rows)
    gt = lambda off, rows: Win(ga, None, off, rows)
    dact = _mm(dxout, wt(R_DOWN, D_FF), "nt", t + "d_act")
    ga = _mm(s["act"], dxout, "tn", t + "dw_down", out=gt(R_DOWN, D_FF))
    dgg, duu = _swiglu_bwd(s["gg"], s["uu"], dact, t + "swiglu_bwd")
    ga = _mm(dgg, s["hn"], "tn", t + "dw_gate", out=gt(R_GATE, D_FF))
    ga = _mm(duu, s["hn"], "tn", t + "dw_up", out=gt(R_UP, D_FF))
    dhn = _mm(dgg, wt(R_GATE, D_FF), "nn", t + "d_hn_g")
    dhn = _mm(duu, wt(R_UP, D_FF), "nn", t + "d_hn_u", add=dhn)
    dhres, gs["norm2_g"] = _rms_bwd(s["hres"], c["g2"], dhn, dxout, t + "rms2_bwd")
    dmerged = _mm(dhres, wt(R_O, D_MODEL), "nt", t + "d_merged")
    ga = _mm(s["merged"], dhres, "tn", t + "dw_o", out=gt(R_O, D_MODEL))
    dbig_y, dgl = _merge_bwd(s["proj"], s["big_y"], dmerged, t + "merge_bwd")
    dys = []
    for i in range(4):
        ga = _mm(s["ys"][i], dbig_y[i], "tn", t + f"dw_out{i}", out=gt(R_OUT + i * BW, BW))
        dys.append(_mm(dbig_y[i], wt(R_OUT + i * BW, BW), "nt", t + f"d_y{i}"))
    proj = s["proj"]
    (dax, dag, gs["conv_a_w"], gs["conv_a_b"], dwx, gs["lru_bx"], dwa, gs["lru_ba"], gs["lru_lambda"]) = _a_bwd(
        proj, dys[0], c["wA"], c["bA"], c["wx"], c["bx"], c["wa"], c["ba"], c["lam"], t + "a_bwd")
    gs["lru_wx"] = _blockdiag_extract(dwx)
    gs["lru_wa"] = _blockdiag_extract(dwa)
    dbv, dbc, dbb, gs["conv_b_w"] = _b_bwd(proj, dys[1], c["wB"], t + "b_bwd")
    dq3, dk3, dv3, dsink = _attn_bwd(s["q3"], s["k3"], s["v3"], _heads(dys[2], N_Q), c["ss"], t + "attn_bwd")
    gs["sinks"] = dsink[:, 0, 0]
    dcd, gs["ln_d_g"], gs["ln_d_b"] = _ln_silu_bwd(s["cd"], c["lg"], c["lb"], dys[3], t + "d_ln_bwd")
    dd1, dd2, gs["conv_d_w"], gs["conv_d_b"] = _d_conv_bwd(proj, dcd, c["wD"], t + "d_conv_bwd")
    dproj = jnp.concatenate(
        [dax, dag, dbv, dbc, dbb, _unheads(dq3), _unheads(dk3).astype(BF16), _unheads(dv3).astype(BF16), dd1, dd2,
         *dgl], axis=1)
    ga = _mm(dproj, s["xn"], "tn", t + "dw_in", out=gt(R_IN, IN_W))
    dxn = _mm(dproj, Win(fw["arena_in"][l], None, R_IN, IN_W), "nn", t + "d_xn")
    dx, gs["norm1_g"] = _rms_bwd(s["x"], c["g1"], dxn, dhres, t + "rms1_bwd")
    return dx, ga, gs


def kernel(x, norm1_g, w_in, conv_a_w, conv_a_b, lru_wx, lru_bx, lru_wa, lru_ba, lru_lambda, w_a_out, conv_b_w, w_b_out, sinks, w_c_out, conv_d_w, conv_d_b, ln_d_g, ln_d_b, w_d_out, w_o, norm2_g, w_ffn_gate, w_ffn_up, w_ffn_down, final_g, loss_target, m_norm1_g, m_w_in, m_conv_a_w, m_conv_a_b, m_lru_wx, m_lru_bx, m_lru_wa, m_lru_ba, m_lru_lambda, m_w_a_out, m_conv_b_w, m_w_b_out, m_sinks, m_w_c_out, m_conv_d_w, m_conv_d_b, m_ln_d_g, m_ln_d_b, m_w_d_out, m_w_o, m_norm2_g, m_w_ffn_gate, m_w_ffn_up, m_w_ffn_down, m_final_g, v_norm1_g, v_w_in, v_conv_a_w, v_conv_a_b, v_lru_wx, v_lru_bx, v_lru_wa, v_lru_ba, v_lru_lambda, v_w_a_out, v_conv_b_w, v_w_b_out, v_sinks, v_w_c_out, v_conv_d_w, v_conv_d_b, v_ln_d_g, v_ln_d_b, v_w_d_out, v_w_o, v_norm2_g, v_w_ffn_gate, v_w_ffn_up, v_w_ffn_down, v_final_g):
    given = dict(locals())
    p = {n: given[n] for n in NAMES}
    mom = {n: given["m_" + n] for n in NAMES}
    var = {n: given["v_" + n] for n in NAMES}
    cx, cy, cc = _coords()
    chip = 2 * cx + cy

    shards = _arena_shards(p)
    fw = _gather_taps(p, "gather_taps")
    shards0, shards1 = [s[0] for s in shards], [s[1] for s in shards]
    flight0 = _gather_start(shards0, _gather_layer(shards0, "gather_l0_in", IN_REGION), "gather_l0_rest_start",
                            REST_REGIONS)
    fw["arena_in"] = [flight0[5], None]
    fw["arena"] = [None, None]
    consts = [_layer_consts(p, fw, l) for l in range(DEPTH)]
    consts[0]["g1"] = consts[0]["g1"] + flight0[4][0:1, 0:1]
    flight1 = []

    def rest_of_layer0(after):
        sh, landing = _gather_wait(*flight0[:4], after, "gather_l0_rest_wait", REST_REGIONS)
        arena = _gather_finish(sh, landing, "gather_l0_rest_finish", REST_REGIONS)
        flight1.extend(_gather_start(shards1, arena, "gather_l1_start"))
        return flight1[5]

    h = x[0]
    saved = []
    for l in range(DEPTH):
        if l == 1:
            sh, landing = _gather_wait(*flight1[:4], h, "gather_l1_wait")
            fw["arena"][1] = fw["arena_in"][1] = _gather_finish(sh, landing, "gather_l1_finish")
        h, s = _layer_fwd(h, consts[l], fw, l, rest_of_layer0)
        saved.append(s)
    loss_vec, dh, g_final = _loss_head(h, final_g.reshape(1, -1), loss_target[0], "loss_head")
    loss = lax.psum(loss_vec[0, 0], ("x", "y", "c"))

    zero = jnp.zeros((1,), jnp.int32)
    chip_sel = chip.reshape(1).astype(jnp.int32)

    def chip_sums(ga, t):
        own, got = _own_halves(ga, cc), _swap_halves(ga, t + "grads_swap_halves")
        return [_sum_own_plus(o.reshape((1, -1, o.shape[-1])), zero, r.reshape((1, -1, r.shape[-1])),
                              t + f"grads_sum_chip{i}", BF16).reshape(o.shape) for i, (o, r) in enumerate(zip(own, got))]

    def all_sums(sums, got, t):
        return [_sum_own_plus(s, chip_sel, r, t + f"grads_sum_all{i}", F32) for i, (s, r) in enumerate(zip(sums, got))]

    gss = [None] * DEPTH
    dh, ga1, gss[1] = _layer_bwd(dh, saved[1], consts[1], fw, 1, lax.empty(ARENA_SHAPE, BF16))
    send_sems, recv_sems, bufs, token = _scatter_start(*chip_sums(ga1, "l1_"), "l1_grads_scatter_start")
    dh, ga0, gss[0] = _layer_bwd(dh + token[0:1, 0:1], saved[0], consts[0], fw, 0, lax.empty(ARENA_SHAPE, BF16))
    grad_x = dh[None]
    bufs = _scatter_wait(send_sems, recv_sems, bufs, dh, "l1_grads_scatter_wait")
    red1 = all_sums(bufs[:2], bufs[2:], "l1_")
    sums0 = chip_sums(ga0, "l0_")
    red0 = all_sums(sums0, _scatter_halves(*sums0, "l0_grads_scatter"), "l0_")
    sib = _swap_many(red0 + red1, "grads_swap_reduced")
    layers = [_reduced_layer(red0, sib[:2], cc), _reduced_layer(red1, sib[2:], cc)]
    g = _shard_grads(jnp.stack([m for m, _ in layers]), jnp.stack([o for _, o in layers]))

    small_full = {n: (g_final.reshape(-1) if n == "final_g" else
                      jnp.stack([gss[l][n].reshape(gss[l][n].shape[-2:] if n.startswith("conv") and n.endswith("_w")
                                                   else p[n].shape[1:]) for l in range(DEPTH)]))
                  for n in SMALL}
    part = _flat_pack([small_full[n] for n in SMALL])
    rows = part.shape[0]
    gathered = _allgather8(jnp.stack([part, part]), "gather_small_grads", pltpu.VMEM)
    small_sum = _flat_unpack(_sum_leading(gathered, "small_grads_sum"), [small_full[n].shape for n in SMALL])
    for n, a in zip(SMALL, small_sum):
        g[n] = lax.dynamic_slice_in_dim(a, chip * LANE, LANE, axis=2) if n in CONV_NAMES else a

    delta, new_m, new_v = {}, {}, {}
    for n in BIG:
        shp = p[n].shape
        two_d = lambda a: a.reshape(-1, shp[-1])
        d, nm, nv = _adamw(two_d(p[n]), two_d(g[n]), two_d(mom[n]), two_d(var[n]), "adamw_" + n)
        delta[n], new_m[n], new_v[n] = d.reshape(shp), nm.reshape(shp), nv.reshape(shp)
    shapes = [p[n].shape for n in SMALL]
    d, nm, nv = _adamw(_flat_pack([p[n] for n in SMALL]), _flat_pack([g[n] for n in SMALL]),
                       _flat_pack([mom[n] for n in SMALL]), _flat_pack([var[n] for n in SMALL]), "adamw_small")
    for n, a, b, cval in zip(SMALL, _flat_unpack(d, shapes), _flat_unpack(nm, shapes), _flat_unpack(nv, shapes)):
        delta[n], new_m[n], new_v[n] = a, b, cval

    return (loss, grad_x, *[g[n] for n in NAMES], *[delta[n] for n in NAMES], *[new_m[n] for n in NAMES],
            *[new_v[n] for n in NAMES])
```

```python
import functools
import math

import numpy as np
import jax
import jax.numpy as jnp
from jax import lax
from jax.experimental import pallas as pl
from jax.experimental.pallas import tpu as pltpu

F32 = jnp.float32
BF16 = jnp.bfloat16
MESH = pl.DeviceIdType.MESH

D_MODEL = 1024
DEPTH = 2
BW = 512
HEAD_DIM = 64
N_Q = 8
N_KV = 2
BLK = 128
D_FF = 2816
IN_W = 8448
EPS = 1e-6
NEG_INF = -1e30
LRU_C = 8.0
CONV_A, CONV_B, CONV_D = 4, 3, 31
LANE = 128
ROW_TILE = 256
VMEM_LIMIT = 56 * 1024 * 1024
MM_VMEM_BUDGET = 36 * 1024 * 1024

C_AX, C_AG, C_BV, C_BC, C_BB = 0, 4, 8, 12, 16
OFF_Q, OFF_K, OFF_V = 2560, 3072, 3200
C_D1, C_D2 = 26, 30
OFF_GL = 4352

ADAM_LR, ADAM_B1, ADAM_B2, ADAM_EPS, ADAM_WD, ADAM_STEP = 0.001, 0.9, 0.999, 1e-08, 0.01, 10

ARENA_W = 1024
R_DOWN, R_GATE, R_UP, R_IN, R_O, R_OUT = 0, 2816, 5632, 8448, 16896, 17920
ARENA_ROWS = 19968
ROW_REGIONS = ((R_DOWN, 704), (R_GATE, 704), (R_UP, 704), (R_IN, 2112), (R_O, 256))
PIECE_OFF = (0, 704, 1408, 2112, 4224)
PIECE_ROWS = 4480
OUT_ROWS, OUT_COLS = 4 * BW, D_MODEL // 4

BIG = ("w_in", "w_a_out", "w_b_out", "w_c_out", "w_d_out", "w_o", "w_ffn_gate", "w_ffn_up", "w_ffn_down")
CONV_NAMES = ("conv_a_w", "conv_b_w", "conv_d_w")
N_TAPS = CONV_A + CONV_B + CONV_D
SMALL = ("norm1_g", "conv_a_w", "conv_a_b", "lru_wx", "lru_bx", "lru_wa", "lru_ba", "lru_lambda", "conv_b_w",
         "sinks", "conv_d_w", "conv_d_b", "ln_d_g", "ln_d_b", "norm2_g", "final_g")
NAMES = ['norm1_g', 'w_in', 'conv_a_w', 'conv_a_b', 'lru_wx', 'lru_bx', 'lru_wa', 'lru_ba', 'lru_lambda', 'w_a_out',
         'conv_b_w', 'w_b_out', 'sinks', 'w_c_out', 'conv_d_w', 'conv_d_b', 'ln_d_g', 'ln_d_b', 'w_d_out', 'w_o',
         'norm2_g', 'w_ffn_gate', 'w_ffn_up', 'w_ffn_down', 'final_g']


def _pick(n, cands, off=0):
    for c in cands:
        if n % c == 0 and off % c == 0:
            return c
    assert off == 0, (n, off)
    return n


class Win:
    def __init__(self, arena, l, off, rows):
        self.arena, self.l, self.off, self.rows = arena, l, off, rows
        self.shape = (rows, arena.shape[-1])


def _params(sem=None):
    return pltpu.CompilerParams(dimension_semantics=sem, vmem_limit_bytes=VMEM_LIMIT)


def _sig(z):
    return 1.0 / (1.0 + jnp.exp(-z))


def _dot(a, b, dims):
    return lax.dot_general(a.astype(BF16), b.astype(BF16), (dims, ((), ())), preferred_element_type=F32)


NN = ((1,), (0,))
NT = ((1,), (1,))
TN = ((0,), (0,))


def _mm(a, b, mode, name, out_dtype=F32, add=None, out=None):
    if mode == "nn":
        (m, k), n = a.shape, b.shape[1]
    elif mode == "nt":
        (m, k), n = a.shape, b.shape[0]
    else:
        (k, m), n = a.shape, b.shape[1]
    b_win = isinstance(b, Win)
    b_off = b.off if b_win else 0
    o_off = out.off if out is not None else 0
    if out is not None:
        out_dtype = out.arena.dtype
    tk = _pick(k, (2816, 2048, 1408, 1024, 768, 512, 256), b_off if mode != "nt" else 0)
    nk = k // tk
    n_off = b_off if mode == "nt" else 0
    a_bytes, b_bytes, o_bytes = a.dtype.itemsize, 2, jnp.dtype(out_dtype).itemsize

    def vmem_bytes(tm_, tn_):
        tile = tm_ * tn_
        return (2 * tk * (tm_ * a_bytes + tn_ * b_bytes) + 2 * tile * o_bytes + (tile * 4 if nk > 1 else 0)
                + (2 * tile * 4 if add is not None else 0) + tile * 4)

    pairs = [(tm_, tn_) for tm_ in (2048, 1024, 768, 512, 256, 128) for tn_ in (1024, 768, 512, 256, 128)
             if m % tm_ == 0 and o_off % tm_ == 0 and n % tn_ == 0 and n_off % tn_ == 0
             and vmem_bytes(tm_, tn_) <= MM_VMEM_BUDGET]
    tm, tn = max(pairs, key=lambda p: (p[0] * p[1], p[0]))
    dims = {"nn": NN, "nt": NT, "tn": TN}[mode]

    def body(*refs):
        a_ref, b_ref = refs[:2]
        c_ref = refs[2] if add is not None else None
        if nk == 1:
            r = _dot(a_ref[...], b_ref[...], dims)
            if add is not None:
                r = r + c_ref[...]
            refs[-1][...] = r.astype(out_dtype)
            return
        o_ref, acc = refs[-2:]
        kk = pl.program_id(2)

        @pl.when(kk == 0)
        def _():
            acc[...] = jnp.zeros_like(acc)

        acc[...] += _dot(a_ref[...], b_ref[...], dims)

        @pl.when(kk == nk - 1)
        def _():
            r = acc[...]
            if add is not None:
                r = r + c_ref[...]
            o_ref[...] = r.astype(out_dtype)

    if mode == "tn":
        a_spec = pl.BlockSpec((tk, tm), lambda i, j, q: (q, i))
    else:
        a_spec = pl.BlockSpec((tm, tk), lambda i, j, q: (i, q))
    if mode == "nt":
        b_blk, b_idx = (tn, tk), (lambda i, j, q: (b_off // tn + j, q))
    else:
        b_blk, b_idx = (tk, tn), (lambda i, j, q: (b_off // tk + q, j))
    if b_win and b.arena.ndim == 3:
        bl = b.l
        b_spec = pl.BlockSpec((None,) + b_blk, lambda i, j, q: (bl,) + b_idx(i, j, q))
    else:
        b_spec = pl.BlockSpec(b_blk, b_idx)
    plain_o = pl.BlockSpec((tm, tn), lambda i, j, q: (i, j))
    in_specs = [a_spec, b_spec] + ([plain_o] if add is not None else [])
    args = (a, b.arena if b_win else b) + ((add,) if add is not None else ())
    aliases = {}
    if out is None:
        o_spec, o_shape = plain_o, jax.ShapeDtypeStruct((m, n), out_dtype)
    else:
        ol = out.l
        if out.arena.ndim == 3:
            o_spec = pl.BlockSpec((None, tm, tn), lambda i, j, q: (ol, o_off // tm + i, j))
        else:
            o_spec = pl.BlockSpec((tm, tn), lambda i, j, q: (o_off // tm + i, j))
        o_shape = jax.ShapeDtypeStruct(out.arena.shape, out_dtype)
        aliases = {len(args): 0}
        in_specs.append(pl.BlockSpec(memory_space=pl.ANY))
        args = args + (out.arena,)
    return pl.pallas_call(
        body, name=name, out_shape=o_shape,
        grid=(m // tm, n // tn, nk), in_specs=in_specs, out_specs=o_spec,
        scratch_shapes=[pltpu.VMEM((tm, tn), F32)] if nk > 1 else [], input_output_aliases=aliases,
        compiler_params=_params(("parallel", "parallel", "arbitrary")),
    )(*args)


def _row_spec(cols, tr=ROW_TILE):
    return pl.BlockSpec((tr, cols), lambda i: (i, 0))


def _vec_spec(cols):
    return pl.BlockSpec((1, cols), lambda i: (0, 0))


def _rms_fwd(x, g, name):
    t, d = x.shape

    def body(x_ref, g_ref, o_ref):
        xv = x_ref[...]
        r = lax.rsqrt(jnp.mean(xv * xv, axis=1, keepdims=True) + EPS)
        o_ref[...] = (xv * r * g_ref[...]).astype(BF16)

    return pl.pallas_call(
        body, name=name, out_shape=jax.ShapeDtypeStruct((t, d), BF16), grid=(t // ROW_TILE,),
        in_specs=[_row_spec(d), _vec_spec(d)], out_specs=_row_spec(d), compiler_params=_params(("parallel",)),
    )(x, g)


def _rms_bwd(x, g, dxn, dres, name):
    t, d = x.shape

    def body(x_ref, g_ref, dy_ref, dr_ref, dx_ref, dg_ref):
        @pl.when(pl.program_id(0) == 0)
        def _():
            dg_ref[...] = jnp.zeros_like(dg_ref)

        xv = x_ref[...]
        dy = dy_ref[...]
        r = lax.rsqrt(jnp.mean(xv * xv, axis=1, keepdims=True) + EPS)
        w = dy * g_ref[...]
        dx_ref[...] = dr_ref[...] + r * w - xv * (r * r * r) * jnp.mean(w * xv, axis=1, keepdims=True)
        dg_ref[...] += jnp.sum(dy * xv * r, axis=0, keepdims=True)

    return pl.pallas_call(
        body, name=name,
        out_shape=(jax.ShapeDtypeStruct((t, d), F32), jax.ShapeDtypeStruct((1, d), F32)), grid=(t // ROW_TILE,),
        in_specs=[_row_spec(d), _vec_spec(d), _row_spec(d), _row_spec(d)], out_specs=(_row_spec(d), _vec_spec(d)),
        compiler_params=_params(("arbitrary",)),
    )(x, g, dxn, dres)


def _loss_head(x, g, tgt, name):
    t, d = x.shape

    def body(x_ref, g_ref, t_ref, loss_ref, dx_ref, dg_ref):
        @pl.when(pl.program_id(0) == 0)
        def _():
            dg_ref[...] = jnp.zeros_like(dg_ref)
            loss_ref[...] = jnp.zeros_like(loss_ref)

        xv = x_ref[...]
        gv = g_ref[...]
        r = lax.rsqrt(jnp.mean(xv * xv, axis=1, keepdims=True) + EPS)
        e = xv * r * gv - t_ref[...]
        loss_ref[...] += jnp.full(loss_ref.shape, (0.5 / d) * jnp.sum(e * e), F32)
        dy = e * (1.0 / d)
        w = dy * gv
        dx_ref[...] = r * w - xv * (r * r * r) * jnp.mean(w * xv, axis=1, keepdims=True)
        dg_ref[...] += jnp.sum(dy * xv * r, axis=0, keepdims=True)

    return pl.pallas_call(
        body, name=name,
        out_shape=(jax.ShapeDtypeStruct((1, LANE), F32), jax.ShapeDtypeStruct((t, d), F32),
                   jax.ShapeDtypeStruct((1, d), F32)),
        grid=(t // ROW_TILE,), in_specs=[_row_spec(d), _vec_spec(d), _row_spec(d)],
        out_specs=(_vec_spec(LANE), _row_spec(d), _vec_spec(d)), compiler_params=_params(("arbitrary",)),
    )(x, g, tgt)


def _swiglu_fwd(gg, uu, name):
    t, f = gg.shape

    def body(g_ref, u_ref, o_ref):
        gv = g_ref[...]
        o_ref[...] = (gv * _sig(gv) * u_ref[...]).astype(BF16)

    return pl.pallas_call(
        body, name=name, out_shape=jax.ShapeDtypeStruct((t, f), BF16), grid=(t // ROW_TILE,),
        in_specs=[_row_spec(f), _row_spec(f)], out_specs=_row_spec(f), compiler_params=_params(("parallel",)),
    )(gg, uu)


def _swiglu_bwd(gg, uu, dact, name):
    t, f = gg.shape

    def body(g_ref, u_ref, d_ref, dg_ref, du_ref):
        gv = g_ref[...]
        dv = d_ref[...]
        s = _sig(gv)
        dg_ref[...] = (dv * u_ref[...] * s * (1.0 + gv * (1.0 - s))).astype(BF16)
        du_ref[...] = (dv * gv * s).astype(BF16)

    return pl.pallas_call(
        body, name=name,
        out_shape=(jax.ShapeDtypeStruct((t, f), BF16), jax.ShapeDtypeStruct((t, f), BF16)), grid=(t // ROW_TILE,),
        in_specs=[_row_spec(f)] * 3, out_specs=(_row_spec(f), _row_spec(f)), compiler_params=_params(("parallel",)),
    )(gg, uu, dact)


MERGE_COLS = 256


def _gate_specs():
    nb = D_MODEL // MERGE_COLS
    base = OFF_GL // MERGE_COLS
    return [pl.BlockSpec((ROW_TILE, MERGE_COLS), functools.partial(lambda i, j, kk: (i, base + nb * kk + j), kk=kk))
            for kk in range(4)]


def _merge_fwd(proj, ys, name):
    t = proj.shape[0]
    yspec = pl.BlockSpec((ROW_TILE, MERGE_COLS), lambda i, j: (i, j))

    def body(g0, g1, g2, g3, y0, y1, y2, y3, o_ref):
        acc = _sig(g0[...]) * y0[...]
        acc += _sig(g1[...]) * y1[...]
        acc += _sig(g2[...]) * y2[...]
        acc += _sig(g3[...]) * y3[...]
        o_ref[...] = acc.astype(BF16)

    return pl.pallas_call(
        body, name=name, out_shape=jax.ShapeDtypeStruct((t, D_MODEL), BF16),
        grid=(t // ROW_TILE, D_MODEL // MERGE_COLS), in_specs=_gate_specs() + [yspec] * 4, out_specs=yspec,
        compiler_params=_params(("parallel", "parallel")),
    )(proj, proj, proj, proj, *ys)


def _merge_bwd(proj, ys, dmerged, name):
    t = proj.shape[0]
    yspec = pl.BlockSpec((ROW_TILE, MERGE_COLS), lambda i, j: (i, j))

    def body(g0, g1, g2, g3, y0, y1, y2, y3, dm_ref, *outs):
        dm = dm_ref[...]
        for gr, yr, dy_ref, dg_ref in zip((g0, g1, g2, g3), (y0, y1, y2, y3), outs[:4], outs[4:]):
            s = _sig(gr[...])
            dy_ref[...] = (dm * s).astype(BF16)
            dg_ref[...] = (dm * yr[...] * s * (1.0 - s)).astype(BF16)

    shp = jax.ShapeDtypeStruct((t, D_MODEL), BF16)
    outs = pl.pallas_call(
        body, name=name, out_shape=(shp,) * 8, grid=(t // ROW_TILE, D_MODEL // MERGE_COLS),
        in_specs=_gate_specs() + [yspec] * 5, out_specs=(yspec,) * 8, compiler_params=_params(("parallel", "parallel")),
    )(proj, proj, proj, proj, *ys, dmerged)
    return outs[:4], outs[4:]


def _ln_silu_fwd(cd, g, b, name):
    t, c = cd.shape

    def body(x_ref, g_ref, b_ref, o_ref):
        xv = x_ref[...]
        mu = jnp.mean(xv, axis=1, keepdims=True)
        xc = xv - mu
        rs = lax.rsqrt(jnp.mean(xc * xc, axis=1, keepdims=True) + EPS)
        z = xc * rs * g_ref[...] + b_ref[...]
        o_ref[...] = (z * _sig(z)).astype(BF16)

    return pl.pallas_call(
        body, name=name, out_shape=jax.ShapeDtypeStruct((t, c), BF16), grid=(t // ROW_TILE,),
        in_specs=[_row_spec(c), _vec_spec(c), _vec_spec(c)], out_specs=_row_spec(c),
        compiler_params=_params(("parallel",)),
    )(cd, g, b)


def _ln_silu_bwd(cd, g, b, dy, name):
    t, c = cd.shape

    def body(x_ref, g_ref, b_ref, dy_ref, dx_ref, dg_ref, db_ref):
        @pl.when(pl.program_id(0) == 0)
        def _():
            dg_ref[...] = jnp.zeros_like(dg_ref)
            db_ref[...] = jnp.zeros_like(db_ref)

        xv = x_ref[...]
        gv = g_ref[...]
        mu = jnp.mean(xv, axis=1, keepdims=True)
        xc = xv - mu
        rs = lax.rsqrt(jnp.mean(xc * xc, axis=1, keepdims=True) + EPS)
        xh = xc * rs
        z = xh * gv + b_ref[...]
        s = _sig(z)
        dz = dy_ref[...] * s * (1.0 + z * (1.0 - s))
        dg_ref[...] += jnp.sum(dz * xh, axis=0, keepdims=True)
        db_ref[...] += jnp.sum(dz, axis=0, keepdims=True)
        dxh = dz * gv
        dx_ref[...] = rs * (dxh - jnp.mean(dxh, axis=1, keepdims=True) - xh * jnp.mean(dxh * xh, axis=1, keepdims=True))

    return pl.pallas_call(
        body, name=name,
        out_shape=(jax.ShapeDtypeStruct((t, c), F32), jax.ShapeDtypeStruct((1, c), F32),
                   jax.ShapeDtypeStruct((1, c), F32)),
        grid=(t // ROW_TILE,), in_specs=[_row_spec(c), _vec_spec(c), _vec_spec(c), _row_spec(c)],
        out_specs=(_row_spec(c), _vec_spec(c), _vec_spec(c)), compiler_params=_params(("arbitrary",)),
    )(cd, g, b, dy)


def _shift_dn(x, k):
    if k == 0:
        return x
    row = lax.broadcasted_iota(jnp.int32, x.shape, 0)
    return jnp.where(row >= k, pltpu.roll(x, k, 0), 0.0)


def _shift_up(x, k):
    if k == 0:
        return x
    t = x.shape[0]
    row = lax.broadcasted_iota(jnp.int32, x.shape, 0)
    return jnp.where(row < t - k, pltpu.roll(x, t - k, 0), 0.0)


def _conv_fwd(x, w_ref, taps):
    acc = w_ref[pl.ds(taps - 1, 1), :] * x
    for k in range(taps - 1):
        acc += w_ref[pl.ds(k, 1), :] * _shift_dn(x, taps - 1 - k)
    return acc


def _conv_bwd(x, dy, w_ref, dw_ref, taps):
    dx = w_ref[pl.ds(taps - 1, 1), :] * dy
    dw_ref[pl.ds(taps - 1, 1), :] = jnp.sum(dy * x, axis=0, keepdims=True)
    for k in range(taps - 1):
        s = taps - 1 - k
        dx += w_ref[pl.ds(k, 1), :] * _shift_up(dy, s)
        dw_ref[pl.ds(k, 1), :] = jnp.sum(dy * _shift_dn(x, s), axis=0, keepdims=True)
    return dx


def _scan_fwd(a, u):
    t = a.shape[0]
    k = 1
    while k < t:
        u = u + a * _shift_dn(u, k)
        if 2 * k < t:
            a = a * _shift_dn(a, k)
        k *= 2
    return u


def _scan_rev(a, u):
    t = a.shape[0]
    k = 1
    while k < t:
        u = u + a * _shift_up(u, k)
        if 2 * k < t:
            a = a * _shift_up(a, k)
        k *= 2
    return u


def _one_minus_exp(y):
    return jnp.where(y > -1e-3, -(y + 0.5 * y * y + (1.0 / 6.0) * y * y * y), 1.0 - jnp.exp(y))


GELU_C = math.sqrt(2.0 / math.pi)


def _gelu(x):
    th = jnp.tanh(GELU_C * (x + 0.044715 * x * x * x))
    return 0.5 * x * (1.0 + th), th


def _softplus(x):
    return jnp.maximum(x, 0.0) + jnp.log(1.0 + jnp.exp(-jnp.abs(x)))


def _chunk_spec(t, blk0):
    return pl.BlockSpec((t, LANE), functools.partial(lambda c, b: (0, b + c), b=blk0))


def _tap_spec(taps):
    return pl.BlockSpec((taps, LANE), lambda c: (0, c))


def _cvec_spec():
    return pl.BlockSpec((1, LANE), lambda c: (0, c))


def _cmat_spec():
    return pl.BlockSpec((1, LANE, LANE), lambda c: (c, 0, 0))


def _lru_forward(ax, wA_ref, bA_ref, wx_ref, bx_ref, wa_ref, ba_ref, lam_ref):
    ca = _conv_fwd(ax, wA_ref, CONV_A) + bA_ref[...]
    gi = _sig(_dot(ca, wx_ref[0], NN) + bx_ref[...])
    gr = _sig(_dot(ca, wa_ref[0], NN) + ba_ref[...])
    sp = _softplus(-lam_ref[...])
    la = -LRU_C * gr * sp
    a = jnp.exp(la)
    mult = jnp.sqrt(_one_minus_exp(2.0 * la))
    h = _scan_fwd(a, ca * gi * mult)
    return ca, gi, gr, sp, a, mult, h


def _a_fwd(proj, wA, bA, wx, bx, wa, ba, lam, name):
    t = proj.shape[0]

    def body(ax_ref, ag_ref, wA_ref, bA_ref, wx_ref, bx_ref, wa_ref, ba_ref, lam_ref, o_ref):
        h = _lru_forward(ax_ref[...], wA_ref, bA_ref, wx_ref, bx_ref, wa_ref, ba_ref, lam_ref)[-1]
        o_ref[...] = (h * _gelu(ag_ref[...])[0]).astype(BF16)

    return pl.pallas_call(
        body, name=name, out_shape=jax.ShapeDtypeStruct((t, BW), BF16), grid=(BW // LANE,),
        in_specs=[_chunk_spec(t, C_AX), _chunk_spec(t, C_AG), _tap_spec(CONV_A), _cvec_spec(), _cmat_spec(),
                  _cvec_spec(), _cmat_spec(), _cvec_spec(), _cvec_spec()],
        out_specs=_chunk_spec(t, 0), compiler_params=_params(("parallel",)),
    )(proj, proj, wA, bA, wx, bx, wa, ba, lam)


def _a_bwd(proj, dya, wA, bA, wx, bx, wa, ba, lam, name):
    t = proj.shape[0]

    def body(ax_ref, ag_ref, dy_ref, wA_ref, bA_ref, wx_ref, bx_ref, wa_ref, ba_ref, lam_ref,
             dax_ref, dag_ref, dwA_ref, dbA_ref, dwx_ref, dbx_ref, dwa_ref, dba_ref, dlam_ref):
        ax = ax_ref[...]
        ag = ag_ref[...]
        dy = dy_ref[...]
        ca, gi, gr, sp, a, mult, h = _lru_forward(ax, wA_ref, bA_ref, wx_ref, bx_ref, wa_ref, ba_ref, lam_ref)
        gel, th = _gelu(ag)
        dgel = 0.5 * (1.0 + th) + 0.5 * ag * (1.0 - th * th) * GELU_C * (1.0 + 3.0 * 0.044715 * ag * ag)
        dag_ref[...] = (dy * h * dgel).astype(BF16)
        s = _scan_rev(_shift_up(a, 1), dy * gel)
        da = s * _shift_dn(h, 1)
        dca = s * gi * mult
        dgi = s * ca * mult
        dmult = s * ca * gi
        dla = da * a - dmult * a * a / mult
        dgr = dla * (-LRU_C * sp)
        dsp = jnp.sum(dla * (-LRU_C * gr), axis=0, keepdims=True)
        dlam_ref[...] = -_sig(-lam_ref[...]) * dsp
        dzi = dgi * gi * (1.0 - gi)
        dzr = dgr * gr * (1.0 - gr)
        dbx_ref[...] = jnp.sum(dzi, axis=0, keepdims=True)
        dba_ref[...] = jnp.sum(dzr, axis=0, keepdims=True)
        dwx_ref[0] = _dot(ca, dzi, TN)
        dwa_ref[0] = _dot(ca, dzr, TN)
        dca += _dot(dzi, wx_ref[0], NT) + _dot(dzr, wa_ref[0], NT)
        dbA_ref[...] = jnp.sum(dca, axis=0, keepdims=True)
        dax_ref[...] = _conv_bwd(ax, dca, wA_ref, dwA_ref, CONV_A).astype(BF16)

    big = jax.ShapeDtypeStruct((t, BW), BF16)
    vec = jax.ShapeDtypeStruct((1, BW), F32)
    mat = jax.ShapeDtypeStruct((BW // LANE, LANE, LANE), F32)
    return pl.pallas_call(
        body, name=name,
        out_shape=(big, big, jax.ShapeDtypeStruct((CONV_A, BW), F32), vec, mat, vec, mat, vec, vec),
        grid=(BW // LANE,),
        in_specs=[_chunk_spec(t, C_AX), _chunk_spec(t, C_AG), _chunk_spec(t, 0), _tap_spec(CONV_A), _cvec_spec(),
                  _cmat_spec(), _cvec_spec(), _cmat_spec(), _cvec_spec(), _cvec_spec()],
        out_specs=(_chunk_spec(t, 0), _chunk_spec(t, 0), _tap_spec(CONV_A), _cvec_spec(), _cmat_spec(), _cvec_spec(),
                   _cmat_spec(), _cvec_spec(), _cvec_spec()),
        compiler_params=_params(("parallel",)),
    )(proj, proj, dya, wA, bA, wx, bx, wa, ba, lam)


def _b_fwd(proj, wB, name):
    t = proj.shape[0]

    def body(bv_ref, bc_ref, bb_ref, w_ref, o_ref):
        o_ref[...] = (bb_ref[...] * _conv_fwd(bc_ref[...] * bv_ref[...], w_ref, CONV_B)).astype(BF16)

    return pl.pallas_call(
        body, name=name, out_shape=jax.ShapeDtypeStruct((t, BW), BF16), grid=(BW // LANE,),
        in_specs=[_chunk_spec(t, C_BV), _chunk_spec(t, C_BC), _chunk_spec(t, C_BB), _tap_spec(CONV_B)],
        out_specs=_chunk_spec(t, 0), compiler_params=_params(("parallel",)),
    )(proj, proj, proj, wB)


def _b_bwd(proj, dyb, wB, name):
    t = proj.shape[0]

    def body(bv_ref, bc_ref, bb_ref, dy_ref, w_ref, dbv_ref, dbc_ref, dbb_ref, dw_ref):
        bv = bv_ref[...]
        bc = bc_ref[...]
        dy = dy_ref[...]
        p = bc * bv
        dbb_ref[...] = (dy * _conv_fwd(p, w_ref, CONV_B)).astype(BF16)
        dp = _conv_bwd(p, dy * bb_ref[...], w_ref, dw_ref, CONV_B)
        dbc_ref[...] = (dp * bv).astype(BF16)
        dbv_ref[...] = (dp * bc).astype(BF16)

    big = jax.ShapeDtypeStruct((t, BW), BF16)
    return pl.pallas_call(
        body, name=name, out_shape=(big, big, big, jax.ShapeDtypeStruct((CONV_B, BW), F32)), grid=(BW // LANE,),
        in_specs=[_chunk_spec(t, C_BV), _chunk_spec(t, C_BC), _chunk_spec(t, C_BB), _chunk_spec(t, 0),
                  _tap_spec(CONV_B)],
        out_specs=(_chunk_spec(t, 0),) * 3 + (_tap_spec(CONV_B),), compiler_params=_params(("parallel",)),
    )(proj, proj, proj, dyb, wB)


def _d_conv_fwd(proj, wD, bD, name):
    t = proj.shape[0]

    def body(d1_ref, d2_ref, w_ref, b_ref, o_ref):
        o_ref[...] = _conv_fwd(d1_ref[...] * _sig(d2_ref[...]), w_ref, CONV_D) + b_ref[...]

    return pl.pallas_call(
        body, name=name, out_shape=jax.ShapeDtypeStruct((t, BW), F32), grid=(BW // LANE,),
        in_specs=[_chunk_spec(t, C_D1), _chunk_spec(t, C_D2), _tap_spec(CONV_D), _cvec_spec()],
        out_specs=_chunk_spec(t, 0), compiler_params=_params(("parallel",)),
    )(proj, proj, wD, bD)


def _d_conv_bwd(proj, dcd, wD, name):
    t = proj.shape[0]

    def body(d1_ref, d2_ref, dy_ref, w_ref, dd1_ref, dd2_ref, dw_ref, db_ref):
        d1 = d1_ref[...]
        s = _sig(d2_ref[...])
        dy = dy_ref[...]
        db_ref[...] = jnp.sum(dy, axis=0, keepdims=True)
        dd = _conv_bwd(d1 * s, dy, w_ref, dw_ref, CONV_D)
        dd1_ref[...] = (dd * s).astype(BF16)
        dd2_ref[...] = (dd * d1 * s * (1.0 - s)).astype(BF16)

    big = jax.ShapeDtypeStruct((t, BW), BF16)
    return pl.pallas_call(
        body, name=name,
        out_shape=(big, big, jax.ShapeDtypeStruct((CONV_D, BW), F32), jax.ShapeDtypeStruct((1, BW), F32)),
        grid=(BW // LANE,),
        in_specs=[_chunk_spec(t, C_D1), _chunk_spec(t, C_D2), _chunk_spec(t, 0), _tap_spec(CONV_D)],
        out_specs=(_chunk_spec(t, 0), _chunk_spec(t, 0), _tap_spec(CONV_D), _cvec_spec()),
        compiler_params=_params(("parallel",)),
    )(proj, proj, dcd, wD)


SCALE = HEAD_DIM ** -0.5
GROUP = N_Q // N_KV


GROWS = GROUP * BLK


def _per_head(ss_ref, row, g):
    head = lax.broadcasted_iota(jnp.int32, (GROWS, 1), 0) // BLK
    col = jnp.full((GROWS, 1), ss_ref[row, g * GROUP + GROUP - 1], F32)
    for i in range(GROUP - 1):
        col = jnp.where(head == i, ss_ref[row, g * GROUP + i], col)
    return col


def _attn_probs(q_ref, k_ref, ss_ref, g, n):
    qi = lax.broadcasted_iota(jnp.int32, (GROWS, BLK), 0) % BLK
    ki = lax.broadcasted_iota(jnp.int32, (GROWS, BLK), 1)
    dist = (qi - ki).astype(F32)
    sink = _per_head(ss_ref, 0, g)
    slope = _per_head(ss_ref, 1, g)
    s0 = pl.multiple_of(n * BLK, BLK)
    sp = pl.multiple_of(jnp.maximum(n - 1, 0) * BLK, BLK)
    q = q_ref[:, pl.ds(s0, BLK), :].reshape(GROWS, HEAD_DIM)
    kc = k_ref[0, pl.ds(s0, BLK), :]
    kp = k_ref[0, pl.ds(sp, BLK), :]
    sc = jnp.where(ki <= qi, _dot(q, kc, NT) * SCALE - slope * dist, NEG_INF)
    first = jnp.where(n >= 1, 0, BLK)
    sv = jnp.where(ki > qi + first, _dot(q, kp, NT) * SCALE - slope * (dist + BLK), NEG_INF)
    m = jnp.maximum(jnp.maximum(jnp.max(sc, axis=1, keepdims=True), jnp.max(sv, axis=1, keepdims=True)), sink)
    pc = jnp.exp(sc - m)
    pp = jnp.exp(sv - m)
    ps = jnp.exp(sink - m)
    z = jnp.sum(pc, axis=1, keepdims=True) + jnp.sum(pp, axis=1, keepdims=True) + ps
    return s0, sp, q, kc, kp, pc, pp, ps, z


def _attn_specs(t):
    qs = pl.BlockSpec((GROUP, t, HEAD_DIM), lambda g: (g, 0, 0))
    ks = pl.BlockSpec((1, t, HEAD_DIM), lambda g: (g, 0, 0))
    ss = pl.BlockSpec(memory_space=pltpu.SMEM)
    return qs, ks, ss


def _attn_fwd(q, k, v, ss, name):
    t = q.shape[1]
    qs, ks, sspec = _attn_specs(t)

    def body(q_ref, k_ref, v_ref, ss_ref, o_ref):
        g = pl.program_id(0)

        def blk(n, carry):
            s0, sp, _, _, _, pc, pp, _, z = _attn_probs(q_ref, k_ref, ss_ref, g, n)
            o = _dot(pc, v_ref[0, pl.ds(s0, BLK), :], NN) + _dot(pp, v_ref[0, pl.ds(sp, BLK), :], NN)
            o_ref[:, pl.ds(s0, BLK), :] = (o / z).astype(BF16).reshape(GROUP, BLK, HEAD_DIM)
            return carry

        lax.fori_loop(0, t // BLK, blk, 0)

    return pl.pallas_call(
        body, name=name, out_shape=jax.ShapeDtypeStruct((N_Q, t, HEAD_DIM), BF16), grid=(N_KV,),
        in_specs=[qs, ks, ks, sspec], out_specs=qs, compiler_params=_params(("parallel",)),
    )(q, k, v, ss)


def _attn_bwd(q, k, v, do, ss, name):
    t = q.shape[1]
    qs, ks, sspec = _attn_specs(t)

    def body(q_ref, k_ref, v_ref, do_ref, ss_ref, dq_ref, dk_ref, dv_ref, ds_ref):
        g = pl.program_id(0)
        dk_ref[...] = jnp.zeros_like(dk_ref)
        dv_ref[...] = jnp.zeros_like(dv_ref)

        def blk(n, dsink):
            s0, sp, q, kc, kp, pc, pp, ps, z = _attn_probs(q_ref, k_ref, ss_ref, g, n)
            rz = 1.0 / z
            pc = pc * rz
            pp = pp * rz
            do_b = do_ref[:, pl.ds(s0, BLK), :].reshape(GROWS, HEAD_DIM)
            dpc = _dot(do_b, v_ref[0, pl.ds(s0, BLK), :], NT)
            dpp = _dot(do_b, v_ref[0, pl.ds(sp, BLK), :], NT)
            delta = jnp.sum(pc * dpc, axis=1, keepdims=True) + jnp.sum(pp * dpp, axis=1, keepdims=True)
            dsc = pc * (dpc - delta)
            dsp = pp * (dpp - delta)
            dq = (_dot(dsc, kc, NN) + _dot(dsp, kp, NN)) * SCALE
            dq_ref[:, pl.ds(s0, BLK), :] = dq.astype(BF16).reshape(GROUP, BLK, HEAD_DIM)
            dk_ref[0, pl.ds(s0, BLK), :] += _dot(dsc, q, TN) * SCALE
            dk_ref[0, pl.ds(sp, BLK), :] += _dot(dsp, q, TN) * SCALE
            dv_ref[0, pl.ds(s0, BLK), :] += _dot(pc, do_b, TN)
            dv_ref[0, pl.ds(sp, BLK), :] += _dot(pp, do_b, TN)
            return dsink - ps * rz * delta

        dsink = lax.fori_loop(0, t // BLK, blk, jnp.zeros((GROWS, 1), F32))
        for i in range(GROUP):
            ds_ref[i] = jnp.full(ds_ref.shape[1:], jnp.sum(dsink[i * BLK:(i + 1) * BLK]), F32)

    kv = jax.ShapeDtypeStruct((N_KV, t, HEAD_DIM), F32)
    return pl.pallas_call(
        body, name=name,
        out_shape=(jax.ShapeDtypeStruct((N_Q, t, HEAD_DIM), BF16), kv, kv, jax.ShapeDtypeStruct((N_Q, 8, LANE), F32)),
        grid=(N_KV,), in_specs=[qs, ks, ks, qs, sspec],
        out_specs=(qs, ks, ks, pl.BlockSpec((GROUP, 8, LANE), lambda g: (g, 0, 0))),
        compiler_params=_params(("parallel",)),
    )(q, k, v, do, ss)


def _heads(x2d, n):
    t = x2d.shape[0]
    return x2d.reshape(t, n, HEAD_DIM).transpose(1, 0, 2)


def _unheads(x3d):
    n, t, _ = x3d.shape
    return x3d.transpose(1, 0, 2).reshape(t, n * HEAD_DIM)


SMALL_ELEMS = 256 * 1024
TILE_ELEMS = 640 * 1024


def _row_tile(r, c):
    if r * c <= SMALL_ELEMS:
        return r
    return _pick(r, [t for t in (512, 256, 128, 64, 32, 16, 8) if t * c <= TILE_ELEMS])


def _adamw(w, g, m, v, name):
    r, c = w.shape
    tr = _row_tile(r, c)
    spec = pl.BlockSpec((tr, c), lambda i: (i, 0))

    def body(w_ref, g_ref, m_ref, v_ref, d_ref, nm_ref, nv_ref):
        gv = g_ref[...]
        nm = ADAM_B1 * m_ref[...] + (1.0 - ADAM_B1) * gv
        nv = ADAM_B2 * v_ref[...] + (1.0 - ADAM_B2) * (gv * gv)
        m_hat = nm / (1.0 - ADAM_B1 ** ADAM_STEP)
        v_hat = nv / (1.0 - ADAM_B2 ** ADAM_STEP)
        d_ref[...] = -ADAM_LR * (m_hat / (jnp.sqrt(v_hat) + ADAM_EPS) + ADAM_WD * w_ref[...])
        nm_ref[...] = nm
        nv_ref[...] = nv

    shp = jax.ShapeDtypeStruct((r, c), F32)
    return pl.pallas_call(
        body, name=name, out_shape=(shp, shp, shp), grid=(r // tr,), in_specs=[spec] * 4, out_specs=(spec,) * 3,
        compiler_params=_params(("parallel",)),
    )(w, g, m, v)


def _sum_leading(x, name):
    n, r, c = x.shape
    tr = _row_tile(r, c)

    def body(x_ref, o_ref):
        acc = x_ref[0]
        for i in range(1, n):
            acc = acc + x_ref[i]
        o_ref[...] = acc

    return pl.pallas_call(
        body, name=name, out_shape=jax.ShapeDtypeStruct((r, c), F32), grid=(r // tr,),
        in_specs=[pl.BlockSpec((n, tr, c), lambda i: (0, i, 0))], out_specs=pl.BlockSpec((tr, c), lambda i: (i, 0)),
        compiler_params=_params(("parallel",)),
    )(x)


def _sum_own_plus(p, sel, recv, name, out_dtype):
    _, r, c = p.shape
    n = recv.shape[0]
    tr = _pick(r, (512, 448, 256, 128, 64, 16))

    def body(sel_ref, p_ref, r_ref, o_ref):
        acc = p_ref[0].astype(F32)
        for i in range(n):
            acc = acc + r_ref[i].astype(F32)
        o_ref[...] = acc.astype(out_dtype)

    grid_spec = pltpu.PrefetchScalarGridSpec(
        num_scalar_prefetch=1, grid=(r // tr,),
        in_specs=[pl.BlockSpec((1, tr, c), lambda i, s: (s[0], i, 0)), pl.BlockSpec((n, tr, c), lambda i, s: (0, i, 0))],
        out_specs=pl.BlockSpec((tr, c), lambda i, s: (i, 0)))
    return pl.pallas_call(
        body, name=name, out_shape=jax.ShapeDtypeStruct((r, c), out_dtype), grid_spec=grid_spec,
        compiler_params=_params(("parallel",)),
    )(sel, p, recv)


def _coords():
    return lax.axis_index("x"), lax.axis_index("y"), lax.axis_index("c")


def _allgather8(x2, name, space):
    _, m, n = x2.shape

    def body(x_ref, out_ref, send_sems, recv_sems, local_sem):
        x, y, c = _coords()
        me, sibling = (x, y, c), (x, y, 1 - c)
        chips = [(1 - x, y), (x, 1 - y), (1 - x, 1 - y)]
        mine_src = x_ref.at[c]

        def rows(px, py, pc):
            return out_ref.at[4 * px + 2 * py + pc]

        def copy(k, block, to, src=None):
            return pltpu.make_async_remote_copy(
                src_ref=rows(*block) if src is None else src, dst_ref=rows(*block),
                send_sem=send_sems.at[k], recv_sem=recv_sems.at[k], device_id=to, device_id_type=MESH)

        mine = pltpu.make_async_copy(mine_src, rows(*me), local_sem)
        mine.start()
        first = [copy(0, me, sibling, src=mine_src)]
        first += [copy(1 + j, me, (*chip, c), src=mine_src) for j, chip in enumerate(chips)]
        for cp in first:
            cp.start()
        passed = [copy(4 + j, (*chip, c), sibling) for j, chip in enumerate(chips)]
        for j, chip in enumerate(chips):
            copy(1 + j, (*chip, c), me).wait_recv()
            passed[j].start()
        copy(0, sibling, me).wait_recv()
        for j, chip in enumerate(chips):
            copy(4 + j, (*chip, 1 - c), me).wait_recv()
        for cp in first + passed:
            cp.wait_send()
        mine.wait()

    return pl.pallas_call(
        body, name=name, out_shape=jax.ShapeDtypeStruct((8, m, n), x2.dtype),
        in_specs=[pl.BlockSpec(memory_space=space)], out_specs=pl.BlockSpec(memory_space=space),
        scratch_shapes=[pltpu.SemaphoreType.DMA((7,)), pltpu.SemaphoreType.DMA((7,)), pltpu.SemaphoreType.DMA],
        compiler_params=pltpu.CompilerParams(vmem_limit_bytes=VMEM_LIMIT),
    )(x2)


N_REG = len(ROW_REGIONS) + 1


def _chip_window(ref, lead, r, j):
    view = ref if lead is None else ref.at[lead]
    if r < len(ROW_REGIONS):
        off, rows = ROW_REGIONS[r]
        return view.at[pl.ds(pl.multiple_of(off + j * rows, 16), rows), :]
    return view.at[pl.ds(R_OUT, OUT_ROWS), pl.ds(pl.multiple_of(j * OUT_COLS, LANE), OUT_COLS)]


HBM_SPEC = pl.BlockSpec(memory_space=pltpu.HBM)
SEM_SPEC = pl.BlockSpec(memory_space=pltpu.SEMAPHORE)


def _half_window(ref, r, j, h):
    if r < len(ROW_REGIONS):
        off, rows = ROW_REGIONS[r]
        return ref.at[pl.ds(pl.multiple_of(off + j * rows + h * (rows // 2), 16), rows // 2), :]
    half = OUT_ROWS // 2
    return ref.at[pl.ds(pl.multiple_of(R_OUT + h * half, 16), half),
                  pl.ds(pl.multiple_of(j * OUT_COLS, LANE), OUT_COLS)]


def _other_chips():
    x, y, _ = _coords()
    return [(1 - x, y), (x, 1 - y), (1 - x, 1 - y)]


def _ici_copies(srcs, arena_ref, send_sems, recv_sems, regions):
    x, y, c = _coords()
    sends, arrivals = [], []
    for k, (cx, cy) in enumerate(_other_chips()):
        for r in regions:
            def remote(src, j):
                return pltpu.make_async_remote_copy(
                    src_ref=src, dst_ref=_half_window(arena_ref, r, j, c), send_sem=send_sems.at[3 * r + k],
                    recv_sem=recv_sems.at[3 * r + k], device_id=(cx, cy, c), device_id_type=MESH)
            rows = srcs[r].shape[0] // 2
            sends.append(remote(srcs[r].at[pl.ds(pl.multiple_of(c * rows, 16), rows), :], 2 * x + y))
            arrivals.append(remote(_half_window(arena_ref, r, 2 * cx + cy, c), 2 * cx + cy))
    return sends, arrivals


def _sibling_copies(srcs, arena_ref, send_sems, recv_sems, regions):
    x, y, c = _coords()
    sends, arrivals = [], []

    def remote(win, r, k, src=None):
        return pltpu.make_async_remote_copy(
            src_ref=win if src is None else src, dst_ref=win, send_sem=send_sems.at[r, k],
            recv_sem=recv_sems.at[r, k], device_id=(x, y, 1 - c), device_id_type=MESH)

    for k, (cx, cy) in enumerate(_other_chips()):
        for r in regions:
            sends.append(remote(_half_window(arena_ref, r, 2 * cx + cy, c), r, k))
            arrivals.append(remote(_half_window(arena_ref, r, 2 * cx + cy, 1 - c), r, k))
    for r in regions:
        own = _chip_window(arena_ref, None, r, 2 * x + y)
        sends.append(remote(own, r, 3, src=srcs[r]))
        arrivals.append(remote(own, r, 3))
    return sends, arrivals


ICI_SEMS = pltpu.SemaphoreType.DMA((3 * N_REG,))
SIBLING_SEMS = pltpu.SemaphoreType.DMA((N_REG, 4))
ARENA_SHAPE = (ARENA_ROWS, ARENA_W)
ALL_REGIONS = tuple(range(N_REG))
IN_REGION = (3,)
REST_REGIONS = (0, 1, 2, 4, 5)


def _gather_layer(shards, name, regions=ALL_REGIONS):
    def body(*refs):
        srcs, arena_ref = refs[:N_REG], refs[N_REG]
        ici_send, ici_recv, sib_send, sib_recv = refs[N_REG + 1:]
        sends, arrivals = _ici_copies(srcs, arena_ref, ici_send, ici_recv, regions)
        passes, landings = _sibling_copies(srcs, arena_ref, sib_send, sib_recv, regions)
        for cp in sends + passes[len(arrivals):]:
            cp.start()
        for arrival, onward in zip(arrivals, passes):
            arrival.wait_recv()
            onward.start()
        for cp in landings:
            cp.wait_recv()
        for cp in sends + passes:
            cp.wait_send()

    return pl.pallas_call(
        body, name=name, out_shape=jax.ShapeDtypeStruct(ARENA_SHAPE, BF16),
        in_specs=[pl.BlockSpec(memory_space=pl.ANY)] * N_REG, out_specs=pl.BlockSpec(memory_space=pl.ANY),
        scratch_shapes=[ICI_SEMS, ICI_SEMS, SIBLING_SEMS, SIBLING_SEMS],
    )(*shards)


def _gather_start(shards, after, name, regions=ALL_REGIONS):
    def body(*refs):
        srcs, arena_ref = refs[:N_REG], refs[N_REG]
        send_sems, recv_sems = refs[N_REG + 2], refs[N_REG + 3]
        token = refs[-1]
        for cp in _ici_copies(srcs, arena_ref, send_sems, recv_sems, regions)[0]:
            cp.start()
        token[...] = jnp.zeros_like(token)

    hbm = lambda a: pltpu.with_memory_space_constraint(a, pltpu.HBM)
    outs = pl.pallas_call(
        body, name=name,
        out_shape=(ICI_SEMS, ICI_SEMS, *[pltpu.HBM(s.shape, s.dtype) for s in shards],
                   pltpu.HBM(ARENA_SHAPE, BF16), pltpu.HBM(after.shape, after.dtype),
                   jax.ShapeDtypeStruct((8, LANE), F32)),
        in_specs=[HBM_SPEC] * (N_REG + 2),
        out_specs=(SEM_SPEC, SEM_SPEC, *[HBM_SPEC] * (N_REG + 2), pl.BlockSpec(memory_space=pltpu.VMEM)),
        input_output_aliases={i: 2 + i for i in range(N_REG + 2)},
        compiler_params=pltpu.CompilerParams(has_side_effects=pltpu.SideEffectType.DATAFLOW_SIDE_EFFECTING),
    )(*[hbm(s) for s in shards], hbm(lax.empty(ARENA_SHAPE, BF16)), hbm(after))
    return outs[0], outs[1], outs[2:2 + N_REG], outs[2 + N_REG], outs[-1], outs[3 + N_REG]


def _gather_wait(send_sems, recv_sems, shards, arena, after, name, regions=ALL_REGIONS):
    def body(*refs):
        srcs, arena_ref = refs[:N_REG], refs[N_REG]
        sends, arrivals = _ici_copies(srcs, arena_ref, refs[N_REG + 1], refs[N_REG + 2], regions)
        for cp in sends:
            cp.wait_send()
        for cp in arrivals:
            cp.wait_recv()

    outs = pl.pallas_call(
        body, name=name,
        out_shape=(*[pltpu.HBM(s.shape, s.dtype) for s in shards], pltpu.HBM(ARENA_SHAPE, BF16)),
        in_specs=[HBM_SPEC] * (N_REG + 1) + [SEM_SPEC, SEM_SPEC, pl.BlockSpec(memory_space=pl.ANY)],
        out_specs=(HBM_SPEC,) * (N_REG + 1), input_output_aliases={i: i for i in range(N_REG + 1)},
        compiler_params=pltpu.CompilerParams(has_side_effects=pltpu.SideEffectType.DATAFLOW_SIDE_EFFECTING),
    )(*shards, arena, send_sems, recv_sems, after)
    return outs[:N_REG], outs[N_REG]


def _gather_finish(shards, arena, name, regions=ALL_REGIONS):
    def body(*refs):
        srcs, arena_ref = refs[:N_REG], refs[N_REG + 1]
        sends, arrivals = _sibling_copies(srcs, arena_ref, refs[N_REG + 2], refs[N_REG + 3], regions)
        for cp in sends:
            cp.start()
        for cp in arrivals:
            cp.wait_recv()
        for cp in sends:
            cp.wait_send()

    return pl.pallas_call(
        body, name=name, out_shape=jax.ShapeDtypeStruct(ARENA_SHAPE, BF16),
        in_specs=[pl.BlockSpec(memory_space=pl.ANY)] * (N_REG + 1), out_specs=pl.BlockSpec(memory_space=pl.ANY),
        scratch_shapes=[SIBLING_SEMS, SIBLING_SEMS], input_output_aliases={N_REG: 0},
    )(*shards, arena)


HALF_PIECE_OFF = tuple(o // 2 for o in PIECE_OFF)
HALF_PIECE_ROWS = PIECE_ROWS // 2
HALF_OUT_ROWS = OUT_ROWS // 2
SWAP_SEMS = pltpu.SemaphoreType.DMA((4 * N_REG,))
SCATTER_SEMS = pltpu.SemaphoreType.DMA((6,))


def _packed_shapes(slots, dtype):
    return (jax.ShapeDtypeStruct((slots, HALF_PIECE_ROWS, ARENA_W), dtype),
            jax.ShapeDtypeStruct((slots, HALF_OUT_ROWS, OUT_COLS), dtype))


def _swap_halves(ga, name):
    def body(g_ref, main_ref, outp_ref, send_sems, recv_sems):
        x, y, c = _coords()
        cps = []
        for j in range(4):
            for r in range(N_REG):
                if r < len(ROW_REGIONS):
                    dst = main_ref.at[j, pl.ds(HALF_PIECE_OFF[r], ROW_REGIONS[r][1] // 2), :]
                else:
                    dst = outp_ref.at[j]
                cps.append(pltpu.make_async_remote_copy(
                    src_ref=_half_window(g_ref, r, j, 1 - c), dst_ref=dst, send_sem=send_sems.at[j * N_REG + r],
                    recv_sem=recv_sems.at[j * N_REG + r], device_id=(x, y, 1 - c), device_id_type=MESH))
        for cp in cps:
            cp.start()
        for cp in cps:
            cp.wait()

    return pl.pallas_call(
        body, name=name, out_shape=_packed_shapes(4, ga.dtype),
        in_specs=[pl.BlockSpec(memory_space=pl.ANY)], out_specs=(pl.BlockSpec(memory_space=pl.ANY),) * 2,
        scratch_shapes=[SWAP_SEMS, SWAP_SEMS],
    )(ga)


def _own_halves(ga, cc):
    mains = [jnp.concatenate([lax.dynamic_slice(ga, (off + j * rows + cc * (rows // 2), 0), (rows // 2, ARENA_W))
                              for off, rows in ROW_REGIONS]) for j in range(4)]
    outs = [lax.dynamic_slice(ga, (R_OUT + cc * HALF_OUT_ROWS, j * OUT_COLS), (HALF_OUT_ROWS, OUT_COLS))
            for j in range(4)]
    return jnp.stack(mains), jnp.stack(outs)


def _scatter_copies(main_ref, outp_ref, rmain_ref, routp_ref, send_sems, recv_sems):
    _, _, c = _coords()
    cps = []
    for k, (cx, cy) in enumerate(_other_chips()):
        for i, (src, dst) in enumerate(((main_ref, rmain_ref), (outp_ref, routp_ref))):
            cps.append(pltpu.make_async_remote_copy(
                src_ref=src.at[2 * cx + cy], dst_ref=dst.at[k], send_sem=send_sems.at[2 * k + i],
                recv_sem=recv_sems.at[2 * k + i], device_id=(cx, cy, c), device_id_type=MESH))
    return cps


def _scatter_start(main, outp, name):
    def body(main_ref, outp_ref, rmain_ref, routp_ref, send_sems, recv_sems, *rest):
        for cp in _scatter_copies(main_ref, outp_ref, rmain_ref, routp_ref, send_sems, recv_sems):
            cp.start()
        rest[-1][...] = jnp.zeros_like(rest[-1])

    hbm = lambda a: pltpu.with_memory_space_constraint(a, pltpu.HBM)
    land = [lax.empty(s.shape, s.dtype) for s in _packed_shapes(3, main.dtype)]
    bufs = [main, outp, *land]
    outs = pl.pallas_call(
        body, name=name,
        out_shape=(SCATTER_SEMS, SCATTER_SEMS, *[pltpu.HBM(b.shape, b.dtype) for b in bufs],
                   jax.ShapeDtypeStruct((8, LANE), F32)),
        in_specs=[HBM_SPEC] * 4, out_specs=(SEM_SPEC, SEM_SPEC, *[HBM_SPEC] * 4, pl.BlockSpec(memory_space=pltpu.VMEM)),
        input_output_aliases={i: 2 + i for i in range(4)},
        compiler_params=pltpu.CompilerParams(has_side_effects=pltpu.SideEffectType.DATAFLOW_SIDE_EFFECTING),
    )(*[hbm(b) for b in bufs])
    return outs[0], outs[1], outs[2:6], outs[6]


def _scatter_wait(send_sems, recv_sems, bufs, after, name):
    def body(main_ref, outp_ref, rmain_ref, routp_ref, send_sems, recv_sems, *rest):
        for cp in _scatter_copies(main_ref, outp_ref, rmain_ref, routp_ref, send_sems, recv_sems):
            cp.wait_send()
            cp.wait_recv()

    return pl.pallas_call(
        body, name=name, out_shape=tuple(pltpu.HBM(b.shape, b.dtype) for b in bufs),
        in_specs=[HBM_SPEC] * 4 + [SEM_SPEC, SEM_SPEC, pl.BlockSpec(memory_space=pl.ANY)],
        out_specs=(HBM_SPEC,) * 4, input_output_aliases={i: i for i in range(4)},
        compiler_params=pltpu.CompilerParams(has_side_effects=pltpu.SideEffectType.DATAFLOW_SIDE_EFFECTING),
    )(*bufs, send_sems, recv_sems, after)


def _swap_many(arrs, name):
    n = len(arrs)

    def body(*refs):
        x, y, c = _coords()
        send_sems, recv_sems = refs[2 * n], refs[2 * n + 1]
        cps = [pltpu.make_async_remote_copy(
            src_ref=refs[i], dst_ref=refs[n + i], send_sem=send_sems.at[i], recv_sem=recv_sems.at[i],
            device_id=(x, y, 1 - c), device_id_type=MESH) for i in range(n)]
        for cp in cps:
            cp.start()
        for cp in cps:
            cp.wait()

    return pl.pallas_call(
        body, name=name, out_shape=tuple(jax.ShapeDtypeStruct(a.shape, a.dtype) for a in arrs),
        in_specs=[pl.BlockSpec(memory_space=pl.ANY)] * n, out_specs=(pl.BlockSpec(memory_space=pl.ANY),) * n,
        scratch_shapes=[pltpu.SemaphoreType.DMA((n,)), pltpu.SemaphoreType.DMA((n,))],
    )(*arrs)


def _join_halves(mine, theirs, cc):
    return jnp.where(cc == 0, jnp.concatenate([mine, theirs]), jnp.concatenate([theirs, mine]))


def _reduced_layer(red, sib, cc):
    parts = []
    for (_, rows), off in zip(ROW_REGIONS, HALF_PIECE_OFF):
        parts.append(_join_halves(red[0][off:off + rows // 2], sib[0][off:off + rows // 2], cc))
    return jnp.concatenate(parts), _join_halves(red[1], sib[1], cc)


OUT_NAMES = ("w_a_out", "w_b_out", "w_c_out", "w_d_out")


def _arena_shards(w):
    t = lambda a: a.astype(BF16).transpose(0, 2, 1)
    return (w["w_ffn_down"].astype(BF16), t(w["w_ffn_gate"]), t(w["w_ffn_up"]), t(w["w_in"]), w["w_o"].astype(BF16),
            jnp.concatenate([w[n].astype(BF16) for n in OUT_NAMES], axis=1))


def _shard_grads(main, outp):
    t = lambda r: main[:, PIECE_OFF[r]:PIECE_OFF[r] + ROW_REGIONS[r][1]]
    g = dict(w_ffn_down=t(0), w_ffn_gate=t(1).transpose(0, 2, 1), w_ffn_up=t(2).transpose(0, 2, 1),
             w_in=t(3).transpose(0, 2, 1), w_o=t(4))
    for i, n in enumerate(OUT_NAMES):
        g[n] = outp[:, i * BW:(i + 1) * BW]
    return g


def _gather_taps(p, name):
    mine = jnp.concatenate([p[n] for n in CONV_NAMES], axis=1).reshape(DEPTH * N_TAPS, LANE)
    rows = -(-mine.shape[0] // 8) * 8
    mine = jnp.concatenate([mine, jnp.zeros((rows - mine.shape[0], LANE), F32)])
    g = _allgather8(jnp.stack([mine, mine]), name, pltpu.VMEM)[0::2, :DEPTH * N_TAPS]
    full = g.reshape(4, DEPTH, N_TAPS, LANE).transpose(1, 2, 0, 3).reshape(DEPTH, N_TAPS, BW)
    return dict(conv_a_w=full[:, :CONV_A], conv_b_w=full[:, CONV_A:CONV_A + CONV_B], conv_d_w=full[:, CONV_A + CONV_B:])


def _flat_pack(arrs):
    flat = jnp.concatenate([a.reshape(-1).astype(F32) for a in arrs])
    rows = -(-flat.shape[0] // (8 * LANE)) * 8
    return jnp.concatenate([flat, jnp.zeros((rows * LANE - flat.shape[0],), F32)]).reshape(rows, LANE)


def _flat_unpack(packed, shapes):
    flat, out, off = packed.reshape(-1), [], 0
    for s in shapes:
        cnt = int(np.prod(s))
        out.append(flat[off:off + cnt].reshape(s))
        off += cnt
    return out


def _blockdiag_chunks(w):
    w4 = w.reshape(4, 2, 64, 64)
    z = jnp.zeros((4, 2, 64, 2, 64), F32)
    z = z.at[:, 0, :, 0, :].set(w4[:, 0]).at[:, 1, :, 1, :].set(w4[:, 1])
    return z.reshape(4, LANE, LANE)


def _blockdiag_extract(d):
    d5 = d.reshape(4, 2, 64, 2, 64)
    return jnp.stack([d5[:, 0, :, 0, :], d5[:, 1, :, 1, :]], axis=1).reshape(8, 64, 64)


SLOPES = np.asarray([2.0 ** (-8.0 * (i + 1) / N_Q) for i in range(N_Q)], np.float32)


def _layer_consts(p, fw, l):
    row = lambda a: a[l].reshape(1, -1)
    return dict(
        g1=row(p["norm1_g"]), g2=row(p["norm2_g"]), wA=fw["conv_a_w"][l], bA=row(p["conv_a_b"]),
        wx=_blockdiag_chunks(p["lru_wx"][l]), bx=row(p["lru_bx"]), wa=_blockdiag_chunks(p["lru_wa"][l]),
        ba=row(p["lru_ba"]), lam=row(p["lru_lambda"]), wB=fw["conv_b_w"][l],
        ss=jnp.stack([p["sinks"][l], jnp.asarray(SLOPES)]), wD=fw["conv_d_w"][l], bD=row(p["conv_d_b"]),
        lg=row(p["ln_d_g"]), lb=row(p["ln_d_b"]))


def _layer_fwd(x, c, fw, l, rest_of_weights=None):
    t = f"l{l}_"
    xn = _rms_fwd(x, c["g1"], t + "rms1")
    wt = lambda off, rows: Win(fw["arena"][l], None, off, rows)
    proj = _mm(xn, Win(fw["arena_in"][l], None, R_IN, IN_W), "nt", t + "proj")
    ya = _a_fwd(proj, c["wA"], c["bA"], c["wx"], c["bx"], c["wa"], c["ba"], c["lam"], t + "a_fwd")
    yb = _b_fwd(proj, c["wB"], t + "b_fwd")
    q3 = _heads(proj[:, OFF_Q:OFF_K], N_Q)
    k3 = _heads(proj[:, OFF_K:OFF_V], N_KV)
    v3 = _heads(proj[:, OFF_V:OFF_V + N_KV * HEAD_DIM], N_KV)
    yc = _unheads(_attn_fwd(q3, k3, v3, c["ss"], t + "attn_fwd"))
    cd = _d_conv_fwd(proj, c["wD"], c["bD"], t + "d_conv_fwd")
    yd = _ln_silu_fwd(cd, c["lg"], c["lb"], t + "d_ln_fwd")
    ys = (ya, yb, yc, yd)
    if fw["arena"][l] is None:
        fw["arena"][l] = rest_of_weights(yd)
    big_y = tuple(_mm(y, wt(R_OUT + i * BW, BW), "nn", t + f"out{i}") for i, y in enumerate(ys))
    merged = _merge_fwd(proj, big_y, t + "merge_fwd")
    hres = _mm(merged, wt(R_O, D_MODEL), "nn", t + "wo", add=x)
    hn = _rms_fwd(hres, c["g2"], t + "rms2")
    gg = _mm(hn, wt(R_GATE, D_FF), "nt", t + "ffn_gate")
    uu = _mm(hn, wt(R_UP, D_FF), "nt", t + "ffn_up")
    act = _swiglu_fwd(gg, uu, t + "swiglu_fwd")
    xout = _mm(act, wt(R_DOWN, D_FF), "nn", t + "ffn_down", add=hres)
    saved = dict(x=x, xn=xn, proj=proj, ys=ys, q3=q3, k3=k3, v3=v3, cd=cd, big_y=big_y, merged=merged, hres=hres,
                 hn=hn, gg=gg, uu=uu, act=act)
    return xout, saved


def _layer_bwd(dxout, s, c, fw, l, ga, weight_grads_done=None):
    t = f"l{l}_"
    gs = {}
    wt = lambda off, rows: Win(fw["arena"][l], None, off, rows)
    gt = lambda off, rows: Win(ga, None, off, rows)
    dact = _mm(dxout, wt(R_DOWN, D_FF), "nt", t + "d_act")
    ga = _mm(s["act"], dxout, "tn", t + "dw_down", out=gt(R_DOWN, D_FF))
    dgg, duu = _swiglu_bwd(s["gg"], s["uu"], dact, t + "swiglu_bwd")
    ga = _mm(dgg, s["hn"], "tn", t + "dw_gate", out=gt(R_GATE, D_FF))
    ga = _mm(duu, s["hn"], "tn", t + "dw_up", out=gt(R_UP, D_FF))
    dhn = _mm(dgg, wt(R_GATE, D_FF), "nn", t + "d_hn_g")
    dhn = _mm(duu, wt(R_UP, D_FF), "nn", t + "d_hn_u", add=dhn)
    dhres, gs["norm2_g"] = _rms_bwd(s["hres"], c["g2"], dhn, dxout, t + "rms2_bwd")
    dmerged = _mm(dhres, wt(R_O, D_MODEL), "nt", t + "d_merged")
    ga = _mm(s["merged"], dhres, "tn", t + "dw_o", out=gt(R_O, D_MODEL))
    dbig_y, dgl = _merge_bwd(s["proj"], s["big_y"], dmerged, t + "merge_bwd")
    dys = []
    for i in range(4):
        ga = _mm(s["ys"][i], dbig_y[i], "tn", t + f"dw_out{i}", out=gt(R_OUT + i * BW, BW))
        dys.append(_mm(dbig_y[i], wt(R_OUT + i * BW, BW), "nt", t + f"d_y{i}"))
    proj = s["proj"]
    (dax, dag, gs["conv_a_w"], gs["conv_a_b"], dwx, gs["lru_bx"], dwa, gs["lru_ba"], gs["lru_lambda"]) = _a_bwd(
        proj, dys[0], c["wA"], c["bA"], c["wx"], c["bx"], c["wa"], c["ba"], c["lam"], t + "a_bwd")
    gs["lru_wx"] = _blockdiag_extract(dwx)
    gs["lru_wa"] = _blockdiag_extract(dwa)
    dbv, dbc, dbb, gs["conv_b_w"] = _b_bwd(proj, dys[1], c["wB"], t + "b_bwd")
    dq3, dk3, dv3, dsink = _attn_bwd(s["q3"], s["k3"], s["v3"], _heads(dys[2], N_Q), c["ss"], t + "attn_bwd")
    gs["sinks"] = dsink[:, 0, 0]
    dcd, gs["ln_d_g"], gs["ln_d_b"] = _ln_silu_bwd(s["cd"], c["lg"], c["lb"], dys[3], t + "d_ln_bwd")
    dd1, dd2, gs["conv_d_w"], gs["conv_d_b"] = _d_conv_bwd(proj, dcd, c["wD"], t + "d_conv_bwd")
    dproj = jnp.concatenate(
        [dax, dag, dbv, dbc, dbb, _unheads(dq3), _unheads(dk3).astype(BF16), _unheads(dv3).astype(BF16), dd1, dd2,
         *dgl], axis=1)
    ga = _mm(dproj, s["xn"], "tn", t + "dw_in", out=gt(R_IN, IN_W))
    g1 = c["g1"]
    if weight_grads_done is not None:
        g1 = g1 + weight_grads_done(ga)
    dxn = _mm(dproj, Win(fw["arena_in"][l], None, R_IN, IN_W), "nn", t + "d_xn")
    dx, gs["norm1_g"] = _rms_bwd(s["x"], g1, dxn, dhres, t + "rms1_bwd")
    return dx, ga, gs


def kernel(x, norm1_g, w_in, conv_a_w, conv_a_b, lru_wx, lru_bx, lru_wa, lru_ba, lru_lambda, w_a_out, conv_b_w, w_b_out, sinks, w_c_out, conv_d_w, conv_d_b, ln_d_g, ln_d_b, w_d_out, w_o, norm2_g, w_ffn_gate, w_ffn_up, w_ffn_down, final_g, loss_target, m_norm1_g, m_w_in, m_conv_a_w, m_conv_a_b, m_lru_wx, m_lru_bx, m_lru_wa, m_lru_ba, m_lru_lambda, m_w_a_out, m_conv_b_w, m_w_b_out, m_sinks, m_w_c_out, m_conv_d_w, m_conv_d_b, m_ln_d_g, m_ln_d_b, m_w_d_out, m_w_o, m_norm2_g, m_w_ffn_gate, m_w_ffn_up, m_w_ffn_down, m_final_g, v_norm1_g, v_w_in, v_conv_a_w, v_conv_a_b, v_lru_wx, v_lru_bx, v_lru_wa, v_lru_ba, v_lru_lambda, v_w_a_out, v_conv_b_w, v_w_b_out, v_sinks, v_w_c_out, v_conv_d_w, v_conv_d_b, v_ln_d_g, v_ln_d_b, v_w_d_out, v_w_o, v_norm2_g, v_w_ffn_gate, v_w_ffn_up, v_w_ffn_down, v_final_g):
    given = dict(locals())
    p = {n: given[n] for n in NAMES}
    mom = {n: given["m_" + n] for n in NAMES}
    var = {n: given["v_" + n] for n in NAMES}
    cx, cy, cc = _coords()
    chip = 2 * cx + cy

    shards = _arena_shards(p)
    fw = _gather_taps(p, "gather_taps")
    shards0, shards1 = [s[0] for s in shards], [s[1] for s in shards]
    flight0 = _gather_start(shards0, _gather_layer(shards0, "gather_l0_in", IN_REGION), "gather_l0_rest_start",
                            REST_REGIONS)
    fw["arena_in"] = [flight0[5], None]
    fw["arena"] = [None, None]
    consts = [_layer_consts(p, fw, l) for l in range(DEPTH)]
    consts[0]["g1"] = consts[0]["g1"] + flight0[4][0:1, 0:1]
    flight1 = []

    def rest_of_layer0(after):
        sh, landing = _gather_wait(*flight0[:4], after, "gather_l0_rest_wait", REST_REGIONS)
        arena = _gather_finish(sh, landing, "gather_l0_rest_finish", REST_REGIONS)
        flight1.extend(_gather_start(shards1, arena, "gather_l1_start"))
        return flight1[5]

    h = x[0]
    saved = []
    for l in range(DEPTH):
        if l == 1:
            sh, landing = _gather_wait(*flight1[:4], h, "gather_l1_wait")
            fw["arena"][1] = fw["arena_in"][1] = _gather_finish(sh, landing, "gather_l1_finish")
        h, s = _layer_fwd(h, consts[l], fw, l, rest_of_layer0)
        saved.append(s)
    loss_vec, dh, g_final = _loss_head(h, final_g.reshape(1, -1), loss_target[0], "loss_head")
    loss = lax.psum(loss_vec[0, 0], ("x", "y", "c"))

    zero = jnp.zeros((1,), jnp.int32)
    chip_sel = chip.reshape(1).astype(jnp.int32)

    def chip_sums(ga, t):
        own, got = _own_halves(ga, cc), _swap_halves(ga, t + "grads_swap_halves")
        return [_sum_own_plus(o.reshape((1, -1, o.shape[-1])), zero, r.reshape((1, -1, r.shape[-1])),
                              t + f"grads_sum_chip{i}", BF16).reshape(o.shape) for i, (o, r) in enumerate(zip(own, got))]

    def all_sums(sums, got, t):
        return [_sum_own_plus(s, chip_sel, r, t + f"grads_sum_all{i}", F32) for i, (s, r) in enumerate(zip(sums, got))]

    gss = [None] * DEPTH
    dh, ga1, gss[1] = _layer_bwd(dh, saved[1], consts[1], fw, 1, lax.empty(ARENA_SHAPE, BF16))
    send_sems, recv_sems, bufs, token = _scatter_start(*chip_sums(ga1, "l1_"), "l1_grads_scatter_start")
    scatter0 = []

    def start_layer0_scatter(ga0):
        scatter0.extend(_scatter_start(*chip_sums(ga0, "l0_"), "l0_grads_scatter_start"))
        return scatter0[3][0:1, 0:1]

    dh, _, gss[0] = _layer_bwd(dh + token[0:1, 0:1], saved[0], consts[0], fw, 0, lax.empty(ARENA_SHAPE, BF16),
                               start_layer0_scatter)
    grad_x = dh[None]

    small_full = {n: (g_final.reshape(-1) if n == "final_g" else
                      jnp.stack([gss[l][n].reshape(gss[l][n].shape[-2:] if n.startswith("conv") and n.endswith("_w")
                                                   else p[n].shape[1:]) for l in range(DEPTH)]))
                  for n in SMALL}
    part = _flat_pack([small_full[n] for n in SMALL])
    rows = part.shape[0]
    gathered = _allgather8(jnp.stack([part, part]), "gather_small_grads", pltpu.VMEM)
    small_packed = _sum_leading(gathered, "small_grads_sum")
    small_sum = _flat_unpack(small_packed, [small_full[n].shape for n in SMALL])

    bufs0 = _scatter_wait(*scatter0[:3], small_packed, "l0_grads_scatter_wait")
    bufs1 = _scatter_wait(send_sems, recv_sems, bufs, dh, "l1_grads_scatter_wait")
    red0 = all_sums(bufs0[:2], bufs0[2:], "l0_")
    red1 = all_sums(bufs1[:2], bufs1[2:], "l1_")
    sib = _swap_many(red0 + red1, "grads_swap_reduced")
    layers = [_reduced_layer(red0, sib[:2], cc), _reduced_layer(red1, sib[2:], cc)]
    g = _shard_grads(jnp.stack([m for m, _ in layers]), jnp.stack([o for _, o in layers]))
    for n, a in zip(SMALL, small_sum):
        g[n] = lax.dynamic_slice_in_dim(a, chip * LANE, LANE, axis=2) if n in CONV_NAMES else a

    delta, new_m, new_v = {}, {}, {}
    for n in BIG:
        shp = p[n].shape
        two_d = lambda a: a.reshape(-1, shp[-1])
        d, nm, nv = _adamw(two_d(p[n]), two_d(g[n]), two_d(mom[n]), two_d(var[n]), "adamw_" + n)
        delta[n], new_m[n], new_v[n] = d.reshape(shp), nm.reshape(shp), nv.reshape(shp)
    shapes = [p[n].shape for n in SMALL]
    d, nm, nv = _adamw(_flat_pack([p[n] for n in SMALL]), _flat_pack([g[n] for n in SMALL]),
                       _flat_pack([mom[n] for n in SMALL]), _flat_pack([var[n] for n in SMALL]), "adamw_small")
    for n, a, b, cval in zip(SMALL, _flat_unpack(d, shapes), _flat_unpack(nm, shapes), _flat_unpack(nv, shapes)):
        delta[n], new_m[n], new_v[n] = a, b, cval

    return (loss, grad_x, *[g[n] for n in NAMES], *[delta[n] for n in NAMES], *[new_m[n] for n in NAMES],
            *[new_v[n] for n in NAMES])
```

```python
import functools
import math

import numpy as np
import jax
import jax.numpy as jnp
from jax import lax
from jax.experimental import pallas as pl
from jax.experimental.pallas import tpu as pltpu

F32 = jnp.float32
BF16 = jnp.bfloat16
MESH = pl.DeviceIdType.MESH

D_MODEL = 1024
DEPTH = 2
BW = 512
HEAD_DIM = 64
N_Q = 8
N_KV = 2
BLK = 128
D_FF = 2816
IN_W = 8448
EPS = 1e-6
NEG_INF = -1e30
LRU_C = 8.0
CONV_A, CONV_B, CONV_D = 4, 3, 31
LANE = 128
ROW_TILE = 256
VMEM_LIMIT = 56 * 1024 * 1024
MM_VMEM_BUDGET = 36 * 1024 * 1024

C_AX, C_AG, C_BV, C_BC, C_BB = 0, 4, 8, 12, 16
OFF_Q, OFF_K, OFF_V = 2560, 3072, 3200
C_D1, C_D2 = 26, 30
OFF_GL = 4352

ADAM_LR, ADAM_B1, ADAM_B2, ADAM_EPS, ADAM_WD, ADAM_STEP = 0.001, 0.9, 0.999, 1e-08, 0.01, 10

ARENA_W = 1024
R_DOWN, R_GATE, R_UP, R_IN, R_O, R_OUT = 0, 2816, 5632, 8448, 16896, 17920
ARENA_ROWS = 19968
ROW_REGIONS = ((R_DOWN, 704), (R_GATE, 704), (R_UP, 704), (R_IN, 2112), (R_O, 256))
PIECE_OFF = (0, 704, 1408, 2112, 4224)
PIECE_ROWS = 4480
OUT_ROWS, OUT_COLS = 4 * BW, D_MODEL // 4

BIG = ("w_in", "w_a_out", "w_b_out", "w_c_out", "w_d_out", "w_o", "w_ffn_gate", "w_ffn_up", "w_ffn_down")
CONV_NAMES = ("conv_a_w", "conv_b_w", "conv_d_w")
N_TAPS = CONV_A + CONV_B + CONV_D
SMALL = ("norm1_g", "conv_a_w", "conv_a_b", "lru_wx", "lru_bx", "lru_wa", "lru_ba", "lru_lambda", "conv_b_w",
         "sinks", "conv_d_w", "conv_d_b", "ln_d_g", "ln_d_b", "norm2_g", "final_g")
NAMES = ['norm1_g', 'w_in', 'conv_a_w', 'conv_a_b', 'lru_wx', 'lru_bx', 'lru_wa', 'lru_ba', 'lru_lambda', 'w_a_out',
         'conv_b_w', 'w_b_out', 'sinks', 'w_c_out', 'conv_d_w', 'conv_d_b', 'ln_d_g', 'ln_d_b', 'w_d_out', 'w_o',
         'norm2_g', 'w_ffn_gate', 'w_ffn_up', 'w_ffn_down', 'final_g']


def _pick(n, cands, off=0):
    for c in cands:
        if n % c == 0 and off % c == 0:
            return c
    assert off == 0, (n, off)
    return n


class Win:
    def __init__(self, arena, l, off, rows):
        self.arena, self.l, self.off, self.rows = arena, l, off, rows
        self.shape = (rows, arena.shape[-1])


def _params(sem=None):
    return pltpu.CompilerParams(dimension_semantics=sem, vmem_limit_bytes=VMEM_LIMIT)


def _sig(z):
    return 1.0 / (1.0 + jnp.exp(-z))


def _dot(a, b, dims):
    return lax.dot_general(a.astype(BF16), b.astype(BF16), (dims, ((), ())), preferred_element_type=F32)


NN = ((1,), (0,))
NT = ((1,), (1,))
TN = ((0,), (0,))


def _mm(a, b, mode, name, out_dtype=F32, add=None, out=None, after=None):
    if mode == "nn":
        (m, k), n = a.shape, b.shape[1]
    elif mode == "nt":
        (m, k), n = a.shape, b.shape[0]
    else:
        (k, m), n = a.shape, b.shape[1]
    b_win = isinstance(b, Win)
    b_off = b.off if b_win else 0
    o_off = out.off if out is not None else 0
    if out is not None:
        out_dtype = out.arena.dtype
    tk = _pick(k, (2816, 2048, 1408, 1024, 768, 512, 256), b_off if mode != "nt" else 0)
    nk = k // tk
    n_off = b_off if mode == "nt" else 0
    a_bytes, b_bytes, o_bytes = a.dtype.itemsize, 2, jnp.dtype(out_dtype).itemsize

    def vmem_bytes(tm_, tn_):
        tile = tm_ * tn_
        return (2 * tk * (tm_ * a_bytes + tn_ * b_bytes) + 2 * tile * o_bytes + (tile * 4 if nk > 1 else 0)
                + (2 * tile * 4 if add is not None else 0) + tile * 4)

    pairs = [(tm_, tn_) for tm_ in (2048, 1024, 768, 512, 256, 128) for tn_ in (1024, 768, 512, 256, 128)
             if m % tm_ == 0 and o_off % tm_ == 0 and n % tn_ == 0 and n_off % tn_ == 0
             and vmem_bytes(tm_, tn_) <= MM_VMEM_BUDGET]
    tm, tn = max(pairs, key=lambda p: (p[0] * p[1], p[0]))
    dims = {"nn": NN, "nt": NT, "tn": TN}[mode]

    def body(*refs):
        a_ref, b_ref = refs[:2]
        c_ref = refs[2] if add is not None else None
        if nk == 1:
            r = _dot(a_ref[...], b_ref[...], dims)
            if add is not None:
                r = r + c_ref[...]
            refs[-1][...] = r.astype(out_dtype)
            return
        o_ref, acc = refs[-2:]
        kk = pl.program_id(2)

        @pl.when(kk == 0)
        def _():
            acc[...] = jnp.zeros_like(acc)

        acc[...] += _dot(a_ref[...], b_ref[...], dims)

        @pl.when(kk == nk - 1)
        def _():
            r = acc[...]
            if add is not None:
                r = r + c_ref[...]
            o_ref[...] = r.astype(out_dtype)

    if mode == "tn":
        a_spec = pl.BlockSpec((tk, tm), lambda i, j, q: (q, i))
    else:
        a_spec = pl.BlockSpec((tm, tk), lambda i, j, q: (i, q))
    if mode == "nt":
        b_blk, b_idx = (tn, tk), (lambda i, j, q: (b_off // tn + j, q))
    else:
        b_blk, b_idx = (tk, tn), (lambda i, j, q: (b_off // tk + q, j))
    if b_win and b.arena.ndim == 3:
        bl = b.l
        b_spec = pl.BlockSpec((None,) + b_blk, lambda i, j, q: (bl,) + b_idx(i, j, q))
    else:
        b_spec = pl.BlockSpec(b_blk, b_idx)
    plain_o = pl.BlockSpec((tm, tn), lambda i, j, q: (i, j))
    in_specs = [a_spec, b_spec] + ([plain_o] if add is not None else [])
    args = (a, b.arena if b_win else b) + ((add,) if add is not None else ())
    if after is not None:
        in_specs.append(pl.BlockSpec(after.shape, lambda i, j, q: (0, 0)))
        args = args + (after,)
    aliases = {}
    if out is None:
        o_spec, o_shape = plain_o, jax.ShapeDtypeStruct((m, n), out_dtype)
    else:
        ol = out.l
        if out.arena.ndim == 3:
            o_spec = pl.BlockSpec((None, tm, tn), lambda i, j, q: (ol, o_off // tm + i, j))
        else:
            o_spec = pl.BlockSpec((tm, tn), lambda i, j, q: (o_off // tm + i, j))
        o_shape = jax.ShapeDtypeStruct(out.arena.shape, out_dtype)
        aliases = {len(args): 0}
        in_specs.append(pl.BlockSpec(memory_space=pl.ANY))
        args = args + (out.arena,)
    return pl.pallas_call(
        body, name=name, out_shape=o_shape,
        grid=(m // tm, n // tn, nk), in_specs=in_specs, out_specs=o_spec,
        scratch_shapes=[pltpu.VMEM((tm, tn), F32)] if nk > 1 else [], input_output_aliases=aliases,
        compiler_params=_params(("parallel", "parallel", "arbitrary")),
    )(*args)


def _row_spec(cols, tr=ROW_TILE):
    return pl.BlockSpec((tr, cols), lambda i: (i, 0))


def _vec_spec(cols):
    return pl.BlockSpec((1, cols), lambda i: (0, 0))


def _rms_fwd(x, g, name):
    t, d = x.shape

    def body(x_ref, g_ref, o_ref):
        xv = x_ref[...]
        r = lax.rsqrt(jnp.mean(xv * xv, axis=1, keepdims=True) + EPS)
        o_ref[...] = (xv * r * g_ref[...]).astype(BF16)

    return pl.pallas_call(
        body, name=name, out_shape=jax.ShapeDtypeStruct((t, d), BF16), grid=(t // ROW_TILE,),
        in_specs=[_row_spec(d), _vec_spec(d)], out_specs=_row_spec(d), compiler_params=_params(("parallel",)),
    )(x, g)


def _rms_bwd(x, g, dxn, dres, name):
    t, d = x.shape

    def body(x_ref, g_ref, dy_ref, dr_ref, dx_ref, dg_ref):
        @pl.when(pl.program_id(0) == 0)
        def _():
            dg_ref[...] = jnp.zeros_like(dg_ref)

        xv = x_ref[...]
        dy = dy_ref[...]
        r = lax.rsqrt(jnp.mean(xv * xv, axis=1, keepdims=True) + EPS)
        w = dy * g_ref[...]
        dx_ref[...] = dr_ref[...] + r * w - xv * (r * r * r) * jnp.mean(w * xv, axis=1, keepdims=True)
        dg_ref[...] += jnp.sum(dy * xv * r, axis=0, keepdims=True)

    return pl.pallas_call(
        body, name=name,
        out_shape=(jax.ShapeDtypeStruct((t, d), F32), jax.ShapeDtypeStruct((1, d), F32)), grid=(t // ROW_TILE,),
        in_specs=[_row_spec(d), _vec_spec(d), _row_spec(d), _row_spec(d)], out_specs=(_row_spec(d), _vec_spec(d)),
        compiler_params=_params(("arbitrary",)),
    )(x, g, dxn, dres)


def _loss_head(x, g, tgt, name):
    t, d = x.shape

    def body(x_ref, g_ref, t_ref, loss_ref, dx_ref, dg_ref):
        @pl.when(pl.program_id(0) == 0)
        def _():
            dg_ref[...] = jnp.zeros_like(dg_ref)
            loss_ref[...] = jnp.zeros_like(loss_ref)

        xv = x_ref[...]
        gv = g_ref[...]
        r = lax.rsqrt(jnp.mean(xv * xv, axis=1, keepdims=True) + EPS)
        e = xv * r * gv - t_ref[...]
        loss_ref[...] += jnp.full(loss_ref.shape, (0.5 / d) * jnp.sum(e * e), F32)
        dy = e * (1.0 / d)
        w = dy * gv
        dx_ref[...] = r * w - xv * (r * r * r) * jnp.mean(w * xv, axis=1, keepdims=True)
        dg_ref[...] += jnp.sum(dy * xv * r, axis=0, keepdims=True)

    return pl.pallas_call(
        body, name=name,
        out_shape=(jax.ShapeDtypeStruct((1, LANE), F32), jax.ShapeDtypeStruct((t, d), F32),
                   jax.ShapeDtypeStruct((1, d), F32)),
        grid=(t // ROW_TILE,), in_specs=[_row_spec(d), _vec_spec(d), _row_spec(d)],
        out_specs=(_vec_spec(LANE), _row_spec(d), _vec_spec(d)), compiler_params=_params(("arbitrary",)),
    )(x, g, tgt)


def _swiglu_fwd(gg, uu, name):
    t, f = gg.shape

    def body(g_ref, u_ref, o_ref):
        gv = g_ref[...]
        o_ref[...] = (gv * _sig(gv) * u_ref[...]).astype(BF16)

    return pl.pallas_call(
        body, name=name, out_shape=jax.ShapeDtypeStruct((t, f), BF16), grid=(t // ROW_TILE,),
        in_specs=[_row_spec(f), _row_spec(f)], out_specs=_row_spec(f), compiler_params=_params(("parallel",)),
    )(gg, uu)


def _swiglu_bwd(gg, uu, dact, name):
    t, f = gg.shape

    def body(g_ref, u_ref, d_ref, dg_ref, du_ref):
        gv = g_ref[...]
        dv = d_ref[...]
        s = _sig(gv)
        dg_ref[...] = (dv * u_ref[...] * s * (1.0 + gv * (1.0 - s))).astype(BF16)
        du_ref[...] = (dv * gv * s).astype(BF16)

    return pl.pallas_call(
        body, name=name,
        out_shape=(jax.ShapeDtypeStruct((t, f), BF16), jax.ShapeDtypeStruct((t, f), BF16)), grid=(t // ROW_TILE,),
        in_specs=[_row_spec(f)] * 3, out_specs=(_row_spec(f), _row_spec(f)), compiler_params=_params(("parallel",)),
    )(gg, uu, dact)


MERGE_COLS = 256
MERGE_ROWS = 1024


def _gate_specs(mr):
    nb = D_MODEL // MERGE_COLS
    base = OFF_GL // MERGE_COLS
    return [pl.BlockSpec((mr, MERGE_COLS), functools.partial(lambda i, j, kk: (i, base + nb * kk + j), kk=kk))
            for kk in range(4)]


def _merge_fwd(proj, ys, name):
    t = proj.shape[0]
    mr = min(t, MERGE_ROWS)
    yspec = pl.BlockSpec((mr, MERGE_COLS), lambda i, j: (i, j))

    def body(g0, g1, g2, g3, y0, y1, y2, y3, o_ref):
        acc = _sig(g0[...]) * y0[...]
        acc += _sig(g1[...]) * y1[...]
        acc += _sig(g2[...]) * y2[...]
        acc += _sig(g3[...]) * y3[...]
        o_ref[...] = acc.astype(BF16)

    return pl.pallas_call(
        body, name=name, out_shape=jax.ShapeDtypeStruct((t, D_MODEL), BF16),
        grid=(t // mr, D_MODEL // MERGE_COLS), in_specs=_gate_specs(mr) + [yspec] * 4, out_specs=yspec,
        compiler_params=_params(("parallel", "parallel")),
    )(proj, proj, proj, proj, *ys)


def _merge_bwd(proj, ys, dmerged, name):
    t = proj.shape[0]
    mr = min(t, MERGE_ROWS)
    yspec = pl.BlockSpec((mr, MERGE_COLS), lambda i, j: (i, j))

    def body(g0, g1, g2, g3, y0, y1, y2, y3, dm_ref, *outs):
        dm = dm_ref[...]
        for gr, yr, dy_ref, dg_ref in zip((g0, g1, g2, g3), (y0, y1, y2, y3), outs[:4], outs[4:]):
            s = _sig(gr[...])
            dy_ref[...] = (dm * s).astype(BF16)
            dg_ref[...] = (dm * yr[...] * s * (1.0 - s)).astype(BF16)

    shp = jax.ShapeDtypeStruct((t, D_MODEL), BF16)
    outs = pl.pallas_call(
        body, name=name, out_shape=(shp,) * 8, grid=(t // mr, D_MODEL // MERGE_COLS),
        in_specs=_gate_specs(mr) + [yspec] * 5, out_specs=(yspec,) * 8, compiler_params=_params(("parallel", "parallel")),
    )(proj, proj, proj, proj, *ys, dmerged)
    return outs[:4], outs[4:]


def _ln_silu_fwd(cd, g, b, name):
    t, c = cd.shape

    def body(x_ref, g_ref, b_ref, o_ref):
        xv = x_ref[...]
        mu = jnp.mean(xv, axis=1, keepdims=True)
        xc = xv - mu
        rs = lax.rsqrt(jnp.mean(xc * xc, axis=1, keepdims=True) + EPS)
        z = xc * rs * g_ref[...] + b_ref[...]
        o_ref[...] = (z * _sig(z)).astype(BF16)

    return pl.pallas_call(
        body, name=name, out_shape=jax.ShapeDtypeStruct((t, c), BF16), grid=(t // ROW_TILE,),
        in_specs=[_row_spec(c), _vec_spec(c), _vec_spec(c)], out_specs=_row_spec(c),
        compiler_params=_params(("parallel",)),
    )(cd, g, b)


def _ln_silu_bwd(cd, g, b, dy, name):
    t, c = cd.shape

    def body(x_ref, g_ref, b_ref, dy_ref, dx_ref, dg_ref, db_ref):
        @pl.when(pl.program_id(0) == 0)
        def _():
            dg_ref[...] = jnp.zeros_like(dg_ref)
            db_ref[...] = jnp.zeros_like(db_ref)

        xv = x_ref[...]
        gv = g_ref[...]
        mu = jnp.mean(xv, axis=1, keepdims=True)
        xc = xv - mu
        rs = lax.rsqrt(jnp.mean(xc * xc, axis=1, keepdims=True) + EPS)
        xh = xc * rs
        z = xh * gv + b_ref[...]
        s = _sig(z)
        dz = dy_ref[...] * s * (1.0 + z * (1.0 - s))
        dg_ref[...] += jnp.sum(dz * xh, axis=0, keepdims=True)
        db_ref[...] += jnp.sum(dz, axis=0, keepdims=True)
        dxh = dz * gv
        dx_ref[...] = rs * (dxh - jnp.mean(dxh, axis=1, keepdims=True) - xh * jnp.mean(dxh * xh, axis=1, keepdims=True))

    return pl.pallas_call(
        body, name=name,
        out_shape=(jax.ShapeDtypeStruct((t, c), F32), jax.ShapeDtypeStruct((1, c), F32),
                   jax.ShapeDtypeStruct((1, c), F32)),
        grid=(t // ROW_TILE,), in_specs=[_row_spec(c), _vec_spec(c), _vec_spec(c), _row_spec(c)],
        out_specs=(_row_spec(c), _vec_spec(c), _vec_spec(c)), compiler_params=_params(("arbitrary",)),
    )(cd, g, b, dy)


def _shift_dn(x, k):
    if k == 0:
        return x
    row = lax.broadcasted_iota(jnp.int32, x.shape, 0)
    return jnp.where(row >= k, pltpu.roll(x, k, 0), 0.0)


def _shift_up(x, k):
    if k == 0:
        return x
    t = x.shape[0]
    row = lax.broadcasted_iota(jnp.int32, x.shape, 0)
    return jnp.where(row < t - k, pltpu.roll(x, t - k, 0), 0.0)


def _conv_fwd(x, w_ref, taps):
    acc = w_ref[pl.ds(taps - 1, 1), :] * x
    for k in range(taps - 1):
        acc += w_ref[pl.ds(k, 1), :] * _shift_dn(x, taps - 1 - k)
    return acc


def _conv_bwd(x, dy, w_ref, dw_ref, taps):
    dx = w_ref[pl.ds(taps - 1, 1), :] * dy
    dw_ref[pl.ds(taps - 1, 1), :] = jnp.sum(dy * x, axis=0, keepdims=True)
    for k in range(taps - 1):
        s = taps - 1 - k
        dx += w_ref[pl.ds(k, 1), :] * _shift_up(dy, s)
        dw_ref[pl.ds(k, 1), :] = jnp.sum(dy * _shift_dn(x, s), axis=0, keepdims=True)
    return dx


def _scan_fwd(a, u):
    t = a.shape[0]
    k = 1
    while k < t:
        u = u + a * _shift_dn(u, k)
        if 2 * k < t:
            a = a * _shift_dn(a, k)
        k *= 2
    return u


def _scan_rev(a, u):
    t = a.shape[0]
    k = 1
    while k < t:
        u = u + a * _shift_up(u, k)
        if 2 * k < t:
            a = a * _shift_up(a, k)
        k *= 2
    return u


def _one_minus_exp(y):
    return jnp.where(y > -1e-3, -(y + 0.5 * y * y + (1.0 / 6.0) * y * y * y), 1.0 - jnp.exp(y))


GELU_C = math.sqrt(2.0 / math.pi)


def _gelu(x):
    th = jnp.tanh(GELU_C * (x + 0.044715 * x * x * x))
    return 0.5 * x * (1.0 + th), th


def _softplus(x):
    return jnp.maximum(x, 0.0) + jnp.log(1.0 + jnp.exp(-jnp.abs(x)))


def _chunk_spec(t, blk0):
    return pl.BlockSpec((t, LANE), functools.partial(lambda c, b: (0, b + c), b=blk0))


def _tap_spec(taps):
    return pl.BlockSpec((taps, LANE), lambda c: (0, c))


def _cvec_spec():
    return pl.BlockSpec((1, LANE), lambda c: (0, c))


def _cmat_spec():
    return pl.BlockSpec((1, LANE, LANE), lambda c: (c, 0, 0))


def _lru_forward(ax, wA_ref, bA_ref, wx_ref, bx_ref, wa_ref, ba_ref, lam_ref):
    ca = _conv_fwd(ax, wA_ref, CONV_A) + bA_ref[...]
    gi = _sig(_dot(ca, wx_ref[0], NN) + bx_ref[...])
    gr = _sig(_dot(ca, wa_ref[0], NN) + ba_ref[...])
    sp = _softplus(-lam_ref[...])
    la = -LRU_C * gr * sp
    a = jnp.exp(la)
    mult = jnp.sqrt(_one_minus_exp(2.0 * la))
    h = _scan_fwd(a, ca * gi * mult)
    return ca, gi, gr, sp, a, mult, h


def _a_fwd(proj, wA, bA, wx, bx, wa, ba, lam, name):
    t = proj.shape[0]

    def body(ax_ref, ag_ref, wA_ref, bA_ref, wx_ref, bx_ref, wa_ref, ba_ref, lam_ref, o_ref):
        h = _lru_forward(ax_ref[...], wA_ref, bA_ref, wx_ref, bx_ref, wa_ref, ba_ref, lam_ref)[-1]
        o_ref[...] = (h * _gelu(ag_ref[...])[0]).astype(BF16)

    return pl.pallas_call(
        body, name=name, out_shape=jax.ShapeDtypeStruct((t, BW), BF16), grid=(BW // LANE,),
        in_specs=[_chunk_spec(t, C_AX), _chunk_spec(t, C_AG), _tap_spec(CONV_A), _cvec_spec(), _cmat_spec(),
                  _cvec_spec(), _cmat_spec(), _cvec_spec(), _cvec_spec()],
        out_specs=_chunk_spec(t, 0), compiler_params=_params(("parallel",)),
    )(proj, proj, wA, bA, wx, bx, wa, ba, lam)


def _a_bwd(proj, dya, wA, bA, wx, bx, wa, ba, lam, name):
    t = proj.shape[0]

    def body(ax_ref, ag_ref, dy_ref, wA_ref, bA_ref, wx_ref, bx_ref, wa_ref, ba_ref, lam_ref,
             dax_ref, dag_ref, dwA_ref, dbA_ref, dwx_ref, dbx_ref, dwa_ref, dba_ref, dlam_ref):
        ax = ax_ref[...]
        ag = ag_ref[...]
        dy = dy_ref[...]
        ca, gi, gr, sp, a, mult, h = _lru_forward(ax, wA_ref, bA_ref, wx_ref, bx_ref, wa_ref, ba_ref, lam_ref)
        gel, th = _gelu(ag)
        dgel = 0.5 * (1.0 + th) + 0.5 * ag * (1.0 - th * th) * GELU_C * (1.0 + 3.0 * 0.044715 * ag * ag)
        dag_ref[...] = (dy * h * dgel).astype(BF16)
        s = _scan_rev(_shift_up(a, 1), dy * gel)
        da = s * _shift_dn(h, 1)
        dca = s * gi * mult
        dgi = s * ca * mult
        dmult = s * ca * gi
        dla = da * a - dmult * a * a / mult
        dgr = dla * (-LRU_C * sp)
        dsp = jnp.sum(dla * (-LRU_C * gr), axis=0, keepdims=True)
        dlam_ref[...] = -_sig(-lam_ref[...]) * dsp
        dzi = dgi * gi * (1.0 - gi)
        dzr = dgr * gr * (1.0 - gr)
        dbx_ref[...] = jnp.sum(dzi, axis=0, keepdims=True)
        dba_ref[...] = jnp.sum(dzr, axis=0, keepdims=True)
        dwx_ref[0] = _dot(ca, dzi, TN)
        dwa_ref[0] = _dot(ca, dzr, TN)
        dca += _dot(dzi, wx_ref[0], NT) + _dot(dzr, wa_ref[0], NT)
        dbA_ref[...] = jnp.sum(dca, axis=0, keepdims=True)
        dax_ref[...] = _conv_bwd(ax, dca, wA_ref, dwA_ref, CONV_A).astype(BF16)

    big = jax.ShapeDtypeStruct((t, BW), BF16)
    vec = jax.ShapeDtypeStruct((1, BW), F32)
    mat = jax.ShapeDtypeStruct((BW // LANE, LANE, LANE), F32)
    return pl.pallas_call(
        body, name=name,
        out_shape=(big, big, jax.ShapeDtypeStruct((CONV_A, BW), F32), vec, mat, vec, mat, vec, vec),
        grid=(BW // LANE,),
        in_specs=[_chunk_spec(t, C_AX), _chunk_spec(t, C_AG), _chunk_spec(t, 0), _tap_spec(CONV_A), _cvec_spec(),
                  _cmat_spec(), _cvec_spec(), _cmat_spec(), _cvec_spec(), _cvec_spec()],
        out_specs=(_chunk_spec(t, 0), _chunk_spec(t, 0), _tap_spec(CONV_A), _cvec_spec(), _cmat_spec(), _cvec_spec(),
                   _cmat_spec(), _cvec_spec(), _cvec_spec()),
        compiler_params=_params(("parallel",)),
    )(proj, proj, dya, wA, bA, wx, bx, wa, ba, lam)


def _b_fwd(proj, wB, name):
    t = proj.shape[0]

    def body(bv_ref, bc_ref, bb_ref, w_ref, o_ref):
        o_ref[...] = (bb_ref[...] * _conv_fwd(bc_ref[...] * bv_ref[...], w_ref, CONV_B)).astype(BF16)

    return pl.pallas_call(
        body, name=name, out_shape=jax.ShapeDtypeStruct((t, BW), BF16), grid=(BW // LANE,),
        in_specs=[_chunk_spec(t, C_BV), _chunk_spec(t, C_BC), _chunk_spec(t, C_BB), _tap_spec(CONV_B)],
        out_specs=_chunk_spec(t, 0), compiler_params=_params(("parallel",)),
    )(proj, proj, proj, wB)


def _b_bwd(proj, dyb, wB, name):
    t = proj.shape[0]

    def body(bv_ref, bc_ref, bb_ref, dy_ref, w_ref, dbv_ref, dbc_ref, dbb_ref, dw_ref):
        bv = bv_ref[...]
        bc = bc_ref[...]
        dy = dy_ref[...]
        p = bc * bv
        dbb_ref[...] = (dy * _conv_fwd(p, w_ref, CONV_B)).astype(BF16)
        dp = _conv_bwd(p, dy * bb_ref[...], w_ref, dw_ref, CONV_B)
        dbc_ref[...] = (dp * bv).astype(BF16)
        dbv_ref[...] = (dp * bc).astype(BF16)

    big = jax.ShapeDtypeStruct((t, BW), BF16)
    return pl.pallas_call(
        body, name=name, out_shape=(big, big, big, jax.ShapeDtypeStruct((CONV_B, BW), F32)), grid=(BW // LANE,),
        in_specs=[_chunk_spec(t, C_BV), _chunk_spec(t, C_BC), _chunk_spec(t, C_BB), _chunk_spec(t, 0),
                  _tap_spec(CONV_B)],
        out_specs=(_chunk_spec(t, 0),) * 3 + (_tap_spec(CONV_B),), compiler_params=_params(("parallel",)),
    )(proj, proj, proj, dyb, wB)


def _d_conv_fwd(proj, wD, bD, name):
    t = proj.shape[0]

    def body(d1_ref, d2_ref, w_ref, b_ref, o_ref):
        o_ref[...] = _conv_fwd(d1_ref[...] * _sig(d2_ref[...]), w_ref, CONV_D) + b_ref[...]

    return pl.pallas_call(
        body, name=name, out_shape=jax.ShapeDtypeStruct((t, BW), F32), grid=(BW // LANE,),
        in_specs=[_chunk_spec(t, C_D1), _chunk_spec(t, C_D2), _tap_spec(CONV_D), _cvec_spec()],
        out_specs=_chunk_spec(t, 0), compiler_params=_params(("parallel",)),
    )(proj, proj, wD, bD)


def _d_conv_bwd(proj, dcd, wD, name):
    t = proj.shape[0]

    def body(d1_ref, d2_ref, dy_ref, w_ref, dd1_ref, dd2_ref, dw_ref, db_ref):
        d1 = d1_ref[...]
        s = _sig(d2_ref[...])
        dy = dy_ref[...]
        db_ref[...] = jnp.sum(dy, axis=0, keepdims=True)
        dd = _conv_bwd(d1 * s, dy, w_ref, dw_ref, CONV_D)
        dd1_ref[...] = (dd * s).astype(BF16)
        dd2_ref[...] = (dd * d1 * s * (1.0 - s)).astype(BF16)

    big = jax.ShapeDtypeStruct((t, BW), BF16)
    return pl.pallas_call(
        body, name=name,
        out_shape=(big, big, jax.ShapeDtypeStruct((CONV_D, BW), F32), jax.ShapeDtypeStruct((1, BW), F32)),
        grid=(BW // LANE,),
        in_specs=[_chunk_spec(t, C_D1), _chunk_spec(t, C_D2), _chunk_spec(t, 0), _tap_spec(CONV_D)],
        out_specs=(_chunk_spec(t, 0), _chunk_spec(t, 0), _tap_spec(CONV_D), _cvec_spec()),
        compiler_params=_params(("parallel",)),
    )(proj, proj, dcd, wD)


SCALE = HEAD_DIM ** -0.5
GROUP = N_Q // N_KV


GROWS = GROUP * BLK


def _per_head(ss_ref, row, g):
    head = lax.broadcasted_iota(jnp.int32, (GROWS, 1), 0) // BLK
    col = jnp.full((GROWS, 1), ss_ref[row, g * GROUP + GROUP - 1], F32)
    for i in range(GROUP - 1):
        col = jnp.where(head == i, ss_ref[row, g * GROUP + i], col)
    return col


def _attn_probs(q_ref, k_ref, ss_ref, g, n):
    qi = lax.broadcasted_iota(jnp.int32, (GROWS, BLK), 0) % BLK
    ki = lax.broadcasted_iota(jnp.int32, (GROWS, BLK), 1)
    dist = (qi - ki).astype(F32)
    sink = _per_head(ss_ref, 0, g)
    slope = _per_head(ss_ref, 1, g)
    s0 = pl.multiple_of(n * BLK, BLK)
    sp = pl.multiple_of(jnp.maximum(n - 1, 0) * BLK, BLK)
    q = q_ref[:, pl.ds(s0, BLK), :].reshape(GROWS, HEAD_DIM)
    kc = k_ref[0, pl.ds(s0, BLK), :]
    kp = k_ref[0, pl.ds(sp, BLK), :]
    sc = jnp.where(ki <= qi, _dot(q, kc, NT) * SCALE - slope * dist, NEG_INF)
    first = jnp.where(n >= 1, 0, BLK)
    sv = jnp.where(ki > qi + first, _dot(q, kp, NT) * SCALE - slope * (dist + BLK), NEG_INF)
    m = jnp.maximum(jnp.maximum(jnp.max(sc, axis=1, keepdims=True), jnp.max(sv, axis=1, keepdims=True)), sink)
    pc = jnp.exp(sc - m)
    pp = jnp.exp(sv - m)
    ps = jnp.exp(sink - m)
    z = jnp.sum(pc, axis=1, keepdims=True) + jnp.sum(pp, axis=1, keepdims=True) + ps
    return s0, sp, q, kc, kp, pc, pp, ps, z


def _attn_specs(t):
    qs = pl.BlockSpec((GROUP, t, HEAD_DIM), lambda g: (g, 0, 0))
    ks = pl.BlockSpec((1, t, HEAD_DIM), lambda g: (g, 0, 0))
    ss = pl.BlockSpec(memory_space=pltpu.SMEM)
    return qs, ks, ss


def _attn_fwd(q, k, v, ss, name):
    t = q.shape[1]
    qs, ks, sspec = _attn_specs(t)

    def body(q_ref, k_ref, v_ref, ss_ref, o_ref):
        g = pl.program_id(0)

        def blk(n, carry):
            s0, sp, _, _, _, pc, pp, _, z = _attn_probs(q_ref, k_ref, ss_ref, g, n)
            o = _dot(pc, v_ref[0, pl.ds(s0, BLK), :], NN) + _dot(pp, v_ref[0, pl.ds(sp, BLK), :], NN)
            o_ref[:, pl.ds(s0, BLK), :] = (o / z).astype(BF16).reshape(GROUP, BLK, HEAD_DIM)
            return carry

        lax.fori_loop(0, t // BLK, blk, 0)

    return pl.pallas_call(
        body, name=name, out_shape=jax.ShapeDtypeStruct((N_Q, t, HEAD_DIM), BF16), grid=(N_KV,),
        in_specs=[qs, ks, ks, sspec], out_specs=qs, compiler_params=_params(("parallel",)),
    )(q, k, v, ss)


def _attn_bwd(q, k, v, do, ss, name):
    t = q.shape[1]
    qs, ks, sspec = _attn_specs(t)

    def body(q_ref, k_ref, v_ref, do_ref, ss_ref, dq_ref, dk_ref, dv_ref, ds_ref):
        g = pl.program_id(0)
        dk_ref[...] = jnp.zeros_like(dk_ref)
        dv_ref[...] = jnp.zeros_like(dv_ref)

        def blk(n, dsink):
            s0, sp, q, kc, kp, pc, pp, ps, z = _attn_probs(q_ref, k_ref, ss_ref, g, n)
            rz = 1.0 / z
            pc = pc * rz
            pp = pp * rz
            do_b = do_ref[:, pl.ds(s0, BLK), :].reshape(GROWS, HEAD_DIM)
            dpc = _dot(do_b, v_ref[0, pl.ds(s0, BLK), :], NT)
            dpp = _dot(do_b, v_ref[0, pl.ds(sp, BLK), :], NT)
            delta = jnp.sum(pc * dpc, axis=1, keepdims=True) + jnp.sum(pp * dpp, axis=1, keepdims=True)
            dsc = pc * (dpc - delta)
            dsp = pp * (dpp - delta)
            dq = (_dot(dsc, kc, NN) + _dot(dsp, kp, NN)) * SCALE
            dq_ref[:, pl.ds(s0, BLK), :] = dq.astype(BF16).reshape(GROUP, BLK, HEAD_DIM)
            dk_ref[0, pl.ds(s0, BLK), :] += _dot(dsc, q, TN) * SCALE
            dk_ref[0, pl.ds(sp, BLK), :] += _dot(dsp, q, TN) * SCALE
            dv_ref[0, pl.ds(s0, BLK), :] += _dot(pc, do_b, TN)
            dv_ref[0, pl.ds(sp, BLK), :] += _dot(pp, do_b, TN)
            return dsink - ps * rz * delta

        dsink = lax.fori_loop(0, t // BLK, blk, jnp.zeros((GROWS, 1), F32))
        for i in range(GROUP):
            ds_ref[i] = jnp.full(ds_ref.shape[1:], jnp.sum(dsink[i * BLK:(i + 1) * BLK]), F32)

    kv = jax.ShapeDtypeStruct((N_KV, t, HEAD_DIM), F32)
    return pl.pallas_call(
        body, name=name,
        out_shape=(jax.ShapeDtypeStruct((N_Q, t, HEAD_DIM), BF16), kv, kv, jax.ShapeDtypeStruct((N_Q, 8, LANE), F32)),
        grid=(N_KV,), in_specs=[qs, ks, ks, qs, sspec],
        out_specs=(qs, ks, ks, pl.BlockSpec((GROUP, 8, LANE), lambda g: (g, 0, 0))),
        compiler_params=_params(("parallel",)),
    )(q, k, v, do, ss)


def _heads(x2d, n):
    t = x2d.shape[0]
    return x2d.reshape(t, n, HEAD_DIM).transpose(1, 0, 2)


def _unheads(x3d):
    n, t, _ = x3d.shape
    return x3d.transpose(1, 0, 2).reshape(t, n * HEAD_DIM)


SMALL_ELEMS = 256 * 1024
TILE_ELEMS = 640 * 1024


def _row_tile(r, c):
    if r * c <= SMALL_ELEMS:
        return r
    return _pick(r, [t for t in (512, 256, 128, 64, 32, 16, 8) if t * c <= TILE_ELEMS])


def _adamw(w, g, m, v, name):
    r, c = w.shape
    tr = _row_tile(r, c)
    spec = pl.BlockSpec((tr, c), lambda i: (i, 0))

    def body(w_ref, g_ref, m_ref, v_ref, d_ref, nm_ref, nv_ref):
        gv = g_ref[...]
        nm = ADAM_B1 * m_ref[...] + (1.0 - ADAM_B1) * gv
        nv = ADAM_B2 * v_ref[...] + (1.0 - ADAM_B2) * (gv * gv)
        m_hat = nm / (1.0 - ADAM_B1 ** ADAM_STEP)
        v_hat = nv / (1.0 - ADAM_B2 ** ADAM_STEP)
        d_ref[...] = -ADAM_LR * (m_hat / (jnp.sqrt(v_hat) + ADAM_EPS) + ADAM_WD * w_ref[...])
        nm_ref[...] = nm
        nv_ref[...] = nv

    shp = jax.ShapeDtypeStruct((r, c), F32)
    return pl.pallas_call(
        body, name=name, out_shape=(shp, shp, shp), grid=(r // tr,), in_specs=[spec] * 4, out_specs=(spec,) * 3,
        compiler_params=_params(("parallel",)),
    )(w, g, m, v)


def _sum_leading(x, name):
    n, r, c = x.shape
    tr = _row_tile(r, c)

    def body(x_ref, o_ref):
        acc = x_ref[0]
        for i in range(1, n):
            acc = acc + x_ref[i]
        o_ref[...] = acc

    return pl.pallas_call(
        body, name=name, out_shape=jax.ShapeDtypeStruct((r, c), F32), grid=(r // tr,),
        in_specs=[pl.BlockSpec((n, tr, c), lambda i: (0, i, 0))], out_specs=pl.BlockSpec((tr, c), lambda i: (i, 0)),
        compiler_params=_params(("parallel",)),
    )(x)


def _sum_own_plus(p, sel, recv, name, out_dtype):
    _, r, c = p.shape
    n = recv.shape[0]
    tr = _pick(r, (512, 448, 256, 128, 64, 16))

    def body(sel_ref, p_ref, r_ref, o_ref):
        acc = p_ref[0].astype(F32)
        for i in range(n):
            acc = acc + r_ref[i].astype(F32)
        o_ref[...] = acc.astype(out_dtype)

    grid_spec = pltpu.PrefetchScalarGridSpec(
        num_scalar_prefetch=1, grid=(r // tr,),
        in_specs=[pl.BlockSpec((1, tr, c), lambda i, s: (s[0], i, 0)), pl.BlockSpec((n, tr, c), lambda i, s: (0, i, 0))],
        out_specs=pl.BlockSpec((tr, c), lambda i, s: (i, 0)))
    return pl.pallas_call(
        body, name=name, out_shape=jax.ShapeDtypeStruct((r, c), out_dtype), grid_spec=grid_spec,
        compiler_params=_params(("parallel",)),
    )(sel, p, recv)


def _coords():
    return lax.axis_index("x"), lax.axis_index("y"), lax.axis_index("c")


def _allgather8(x2, name, space):
    _, m, n = x2.shape

    def body(x_ref, out_ref, send_sems, recv_sems, local_sem):
        x, y, c = _coords()
        me, sibling = (x, y, c), (x, y, 1 - c)
        chips = [(1 - x, y), (x, 1 - y), (1 - x, 1 - y)]
        mine_src = x_ref.at[c]

        def rows(px, py, pc):
            return out_ref.at[4 * px + 2 * py + pc]

        def copy(k, block, to, src=None):
            return pltpu.make_async_remote_copy(
                src_ref=rows(*block) if src is None else src, dst_ref=rows(*block),
                send_sem=send_sems.at[k], recv_sem=recv_sems.at[k], device_id=to, device_id_type=MESH)

        mine = pltpu.make_async_copy(mine_src, rows(*me), local_sem)
        mine.start()
        first = [copy(0, me, sibling, src=mine_src)]
        first += [copy(1 + j, me, (*chip, c), src=mine_src) for j, chip in enumerate(chips)]
        for cp in first:
            cp.start()
        passed = [copy(4 + j, (*chip, c), sibling) for j, chip in enumerate(chips)]
        for j, chip in enumerate(chips):
            copy(1 + j, (*chip, c), me).wait_recv()
            passed[j].start()
        copy(0, sibling, me).wait_recv()
        for j, chip in enumerate(chips):
            copy(4 + j, (*chip, 1 - c), me).wait_recv()
        for cp in first + passed:
            cp.wait_send()
        mine.wait()

    return pl.pallas_call(
        body, name=name, out_shape=jax.ShapeDtypeStruct((8, m, n), x2.dtype),
        in_specs=[pl.BlockSpec(memory_space=space)], out_specs=pl.BlockSpec(memory_space=space),
        scratch_shapes=[pltpu.SemaphoreType.DMA((7,)), pltpu.SemaphoreType.DMA((7,)), pltpu.SemaphoreType.DMA],
        compiler_params=pltpu.CompilerParams(vmem_limit_bytes=VMEM_LIMIT),
    )(x2)


N_REG = len(ROW_REGIONS) + 1


def _chip_window(ref, lead, r, j):
    view = ref if lead is None else ref.at[lead]
    if r < len(ROW_REGIONS):
        off, rows = ROW_REGIONS[r]
        return view.at[pl.ds(pl.multiple_of(off + j * rows, 16), rows), :]
    return view.at[pl.ds(R_OUT, OUT_ROWS), pl.ds(pl.multiple_of(j * OUT_COLS, LANE), OUT_COLS)]


HBM_SPEC = pl.BlockSpec(memory_space=pltpu.HBM)
SEM_SPEC = pl.BlockSpec(memory_space=pltpu.SEMAPHORE)


def _half_window(ref, r, j, h):
    if r < len(ROW_REGIONS):
        off, rows = ROW_REGIONS[r]
        return ref.at[pl.ds(pl.multiple_of(off + j * rows + h * (rows // 2), 16), rows // 2), :]
    half = OUT_ROWS // 2
    return ref.at[pl.ds(pl.multiple_of(R_OUT + h * half, 16), half),
                  pl.ds(pl.multiple_of(j * OUT_COLS, LANE), OUT_COLS)]


def _other_chips():
    x, y, _ = _coords()
    return [(1 - x, y), (x, 1 - y), (1 - x, 1 - y)]


def _ici_copies(srcs, arena_ref, send_sems, recv_sems, regions):
    x, y, c = _coords()
    sends, arrivals = [], []
    for k, (cx, cy) in enumerate(_other_chips()):
        for r in regions:
            def remote(src, j):
                return pltpu.make_async_remote_copy(
                    src_ref=src, dst_ref=_half_window(arena_ref, r, j, c), send_sem=send_sems.at[3 * r + k],
                    recv_sem=recv_sems.at[3 * r + k], device_id=(cx, cy, c), device_id_type=MESH)
            rows = srcs[r].shape[0] // 2
            sends.append(remote(srcs[r].at[pl.ds(pl.multiple_of(c * rows, 16), rows), :], 2 * x + y))
            arrivals.append(remote(_half_window(arena_ref, r, 2 * cx + cy, c), 2 * cx + cy))
    return sends, arrivals


def _sibling_copies(srcs, arena_ref, send_sems, recv_sems, regions):
    x, y, c = _coords()
    sends, arrivals = [], []

    def remote(win, r, k, src=None):
        return pltpu.make_async_remote_copy(
            src_ref=win if src is None else src, dst_ref=win, send_sem=send_sems.at[r, k],
            recv_sem=recv_sems.at[r, k], device_id=(x, y, 1 - c), device_id_type=MESH)

    for k, (cx, cy) in enumerate(_other_chips()):
        for r in regions:
            sends.append(remote(_half_window(arena_ref, r, 2 * cx + cy, c), r, k))
            arrivals.append(remote(_half_window(arena_ref, r, 2 * cx + cy, 1 - c), r, k))
    for r in regions:
        own = _chip_window(arena_ref, None, r, 2 * x + y)
        sends.append(remote(own, r, 3, src=srcs[r]))
        arrivals.append(remote(own, r, 3))
    return sends, arrivals


ICI_SEMS = pltpu.SemaphoreType.DMA((3 * N_REG,))
SIBLING_SEMS = pltpu.SemaphoreType.DMA((N_REG, 4))
ARENA_SHAPE = (ARENA_ROWS, ARENA_W)
ALL_REGIONS = tuple(range(N_REG))
IN_REGION = (3,)
REST_REGIONS = (0, 1, 2, 4, 5)


def _gather_layer(shards, name, regions=ALL_REGIONS):
    def body(*refs):
        srcs, arena_ref = refs[:N_REG], refs[N_REG]
        ici_send, ici_recv, sib_send, sib_recv = refs[N_REG + 1:]
        sends, arrivals = _ici_copies(srcs, arena_ref, ici_send, ici_recv, regions)
        passes, landings = _sibling_copies(srcs, arena_ref, sib_send, sib_recv, regions)
        for cp in sends + passes[len(arrivals):]:
            cp.start()
        for arrival, onward in zip(arrivals, passes):
            arrival.wait_recv()
            onward.start()
        for cp in landings:
            cp.wait_recv()
        for cp in sends + passes:
            cp.wait_send()

    return pl.pallas_call(
        body, name=name, out_shape=jax.ShapeDtypeStruct(ARENA_SHAPE, BF16),
        in_specs=[pl.BlockSpec(memory_space=pl.ANY)] * N_REG, out_specs=pl.BlockSpec(memory_space=pl.ANY),
        scratch_shapes=[ICI_SEMS, ICI_SEMS, SIBLING_SEMS, SIBLING_SEMS],
    )(*shards)


def _gather_start(shards, after, name, regions=ALL_REGIONS):
    def body(*refs):
        srcs, arena_ref = refs[:N_REG], refs[N_REG]
        send_sems, recv_sems = refs[N_REG + 2], refs[N_REG + 3]
        token = refs[-1]
        for cp in _ici_copies(srcs, arena_ref, send_sems, recv_sems, regions)[0]:
            cp.start()
        token[...] = jnp.zeros_like(token)

    hbm = lambda a: pltpu.with_memory_space_constraint(a, pltpu.HBM)
    outs = pl.pallas_call(
        body, name=name,
        out_shape=(ICI_SEMS, ICI_SEMS, *[pltpu.HBM(s.shape, s.dtype) for s in shards],
                   pltpu.HBM(ARENA_SHAPE, BF16), pltpu.HBM(after.shape, after.dtype),
                   jax.ShapeDtypeStruct((8, LANE), F32)),
        in_specs=[HBM_SPEC] * (N_REG + 2),
        out_specs=(SEM_SPEC, SEM_SPEC, *[HBM_SPEC] * (N_REG + 2), pl.BlockSpec(memory_space=pltpu.VMEM)),
        input_output_aliases={i: 2 + i for i in range(N_REG + 2)},
        compiler_params=pltpu.CompilerParams(has_side_effects=pltpu.SideEffectType.DATAFLOW_SIDE_EFFECTING),
    )(*[hbm(s) for s in shards], hbm(lax.empty(ARENA_SHAPE, BF16)), hbm(after))
    return outs[0], outs[1], outs[2:2 + N_REG], outs[2 + N_REG], outs[-1], outs[3 + N_REG]


def _gather_wait(send_sems, recv_sems, shards, arena, after, name, regions=ALL_REGIONS):
    def body(*refs):
        srcs, arena_ref = refs[:N_REG], refs[N_REG]
        sends, arrivals = _ici_copies(srcs, arena_ref, refs[N_REG + 1], refs[N_REG + 2], regions)
        for cp in sends:
            cp.wait_send()
        for cp in arrivals:
            cp.wait_recv()

    outs = pl.pallas_call(
        body, name=name,
        out_shape=(*[pltpu.HBM(s.shape, s.dtype) for s in shards], pltpu.HBM(ARENA_SHAPE, BF16)),
        in_specs=[HBM_SPEC] * (N_REG + 1) + [SEM_SPEC, SEM_SPEC, pl.BlockSpec(memory_space=pl.ANY)],
        out_specs=(HBM_SPEC,) * (N_REG + 1), input_output_aliases={i: i for i in range(N_REG + 1)},
        compiler_params=pltpu.CompilerParams(has_side_effects=pltpu.SideEffectType.DATAFLOW_SIDE_EFFECTING),
    )(*shards, arena, send_sems, recv_sems, after)
    return outs[:N_REG], outs[N_REG]


def _gather_finish(shards, arena, name, regions=ALL_REGIONS):
    def body(*refs):
        srcs, arena_ref = refs[:N_REG], refs[N_REG + 1]
        sends, arrivals = _sibling_copies(srcs, arena_ref, refs[N_REG + 2], refs[N_REG + 3], regions)
        for cp in sends:
            cp.start()
        for cp in arrivals:
            cp.wait_recv()
        for cp in sends:
            cp.wait_send()

    return pl.pallas_call(
        body, name=name, out_shape=jax.ShapeDtypeStruct(ARENA_SHAPE, BF16),
        in_specs=[pl.BlockSpec(memory_space=pl.ANY)] * (N_REG + 1), out_specs=pl.BlockSpec(memory_space=pl.ANY),
        scratch_shapes=[SIBLING_SEMS, SIBLING_SEMS], input_output_aliases={N_REG: 0},
    )(*shards, arena)


HALF_PIECE_OFF = tuple(o // 2 for o in PIECE_OFF)
HALF_PIECE_ROWS = PIECE_ROWS // 2
HALF_OUT_ROWS = OUT_ROWS // 2
SWAP_SEMS = pltpu.SemaphoreType.DMA((4 * N_REG,))
SCATTER_SEMS = pltpu.SemaphoreType.DMA((6,))


def _packed_shapes(slots, dtype):
    return (jax.ShapeDtypeStruct((slots, HALF_PIECE_ROWS, ARENA_W), dtype),
            jax.ShapeDtypeStruct((slots, HALF_OUT_ROWS, OUT_COLS), dtype))


def _swap_halves(ga, name):
    def body(g_ref, main_ref, outp_ref, send_sems, recv_sems):
        x, y, c = _coords()
        cps = []
        for j in range(4):
            for r in range(N_REG):
                if r < len(ROW_REGIONS):
                    dst = main_ref.at[j, pl.ds(HALF_PIECE_OFF[r], ROW_REGIONS[r][1] // 2), :]
                else:
                    dst = outp_ref.at[j]
                cps.append(pltpu.make_async_remote_copy(
                    src_ref=_half_window(g_ref, r, j, 1 - c), dst_ref=dst, send_sem=send_sems.at[j * N_REG + r],
                    recv_sem=recv_sems.at[j * N_REG + r], device_id=(x, y, 1 - c), device_id_type=MESH))
        for cp in cps:
            cp.start()
        for cp in cps:
            cp.wait()

    return pl.pallas_call(
        body, name=name, out_shape=_packed_shapes(4, ga.dtype),
        in_specs=[pl.BlockSpec(memory_space=pl.ANY)], out_specs=(pl.BlockSpec(memory_space=pl.ANY),) * 2,
        scratch_shapes=[SWAP_SEMS, SWAP_SEMS],
    )(ga)


def _own_halves(ga, cc):
    mains = [jnp.concatenate([lax.dynamic_slice(ga, (off + j * rows + cc * (rows // 2), 0), (rows // 2, ARENA_W))
                              for off, rows in ROW_REGIONS]) for j in range(4)]
    outs = [lax.dynamic_slice(ga, (R_OUT + cc * HALF_OUT_ROWS, j * OUT_COLS), (HALF_OUT_ROWS, OUT_COLS))
            for j in range(4)]
    return jnp.stack(mains), jnp.stack(outs)


def _scatter_copies(main_ref, outp_ref, rmain_ref, routp_ref, send_sems, recv_sems):
    _, _, c = _coords()
    cps = []
    for k, (cx, cy) in enumerate(_other_chips()):
        for i, (src, dst) in enumerate(((main_ref, rmain_ref), (outp_ref, routp_ref))):
            cps.append(pltpu.make_async_remote_copy(
                src_ref=src.at[2 * cx + cy], dst_ref=dst.at[k], send_sem=send_sems.at[2 * k + i],
                recv_sem=recv_sems.at[2 * k + i], device_id=(cx, cy, c), device_id_type=MESH))
    return cps


def _scatter_start(main, outp, name):
    def body(main_ref, outp_ref, rmain_ref, routp_ref, send_sems, recv_sems, *rest):
        for cp in _scatter_copies(main_ref, outp_ref, rmain_ref, routp_ref, send_sems, recv_sems):
            cp.start()
        rest[-1][...] = jnp.zeros_like(rest[-1])

    hbm = lambda a: pltpu.with_memory_space_constraint(a, pltpu.HBM)
    land = [lax.empty(s.shape, s.dtype) for s in _packed_shapes(3, main.dtype)]
    bufs = [main, outp, *land]
    outs = pl.pallas_call(
        body, name=name,
        out_shape=(SCATTER_SEMS, SCATTER_SEMS, *[pltpu.HBM(b.shape, b.dtype) for b in bufs],
                   jax.ShapeDtypeStruct((8, LANE), F32)),
        in_specs=[HBM_SPEC] * 4, out_specs=(SEM_SPEC, SEM_SPEC, *[HBM_SPEC] * 4, pl.BlockSpec(memory_space=pltpu.VMEM)),
        input_output_aliases={i: 2 + i for i in range(4)},
        compiler_params=pltpu.CompilerParams(has_side_effects=pltpu.SideEffectType.DATAFLOW_SIDE_EFFECTING),
    )(*[hbm(b) for b in bufs])
    return outs[0], outs[1], outs[2:6], outs[6]


def _scatter_wait(send_sems, recv_sems, bufs, after, name):
    def body(main_ref, outp_ref, rmain_ref, routp_ref, send_sems, recv_sems, *rest):
        for cp in _scatter_copies(main_ref, outp_ref, rmain_ref, routp_ref, send_sems, recv_sems):
            cp.wait_send()
            cp.wait_recv()

    return pl.pallas_call(
        body, name=name, out_shape=tuple(pltpu.HBM(b.shape, b.dtype) for b in bufs),
        in_specs=[HBM_SPEC] * 4 + [SEM_SPEC, SEM_SPEC, pl.BlockSpec(memory_space=pl.ANY)],
        out_specs=(HBM_SPEC,) * 4, input_output_aliases={i: i for i in range(4)},
        compiler_params=pltpu.CompilerParams(has_side_effects=pltpu.SideEffectType.DATAFLOW_SIDE_EFFECTING),
    )(*bufs, send_sems, recv_sems, after)


def _swap_many(arrs, name):
    n = len(arrs)

    def body(*refs):
        x, y, c = _coords()
        send_sems, recv_sems = refs[2 * n], refs[2 * n + 1]
        cps = [pltpu.make_async_remote_copy(
            src_ref=refs[i], dst_ref=refs[n + i], send_sem=send_sems.at[i], recv_sem=recv_sems.at[i],
            device_id=(x, y, 1 - c), device_id_type=MESH) for i in range(n)]
        for cp in cps:
            cp.start()
        for cp in cps:
            cp.wait()

    return pl.pallas_call(
        body, name=name, out_shape=tuple(jax.ShapeDtypeStruct(a.shape, a.dtype) for a in arrs),
        in_specs=[pl.BlockSpec(memory_space=pl.ANY)] * n, out_specs=(pl.BlockSpec(memory_space=pl.ANY),) * n,
        scratch_shapes=[pltpu.SemaphoreType.DMA((n,)), pltpu.SemaphoreType.DMA((n,))],
    )(*arrs)


def _join_halves(mine, theirs, cc):
    return jnp.where(cc == 0, jnp.concatenate([mine, theirs]), jnp.concatenate([theirs, mine]))


def _reduced_layer(red, sib, cc):
    parts = []
    for (_, rows), off in zip(ROW_REGIONS, HALF_PIECE_OFF):
        parts.append(_join_halves(red[0][off:off + rows // 2], sib[0][off:off + rows // 2], cc))
    return jnp.concatenate(parts), _join_halves(red[1], sib[1], cc)


OUT_NAMES = ("w_a_out", "w_b_out", "w_c_out", "w_d_out")


def _arena_shards(w):
    t = lambda a: a.astype(BF16).transpose(0, 2, 1)
    return (w["w_ffn_down"].astype(BF16), t(w["w_ffn_gate"]), t(w["w_ffn_up"]), t(w["w_in"]), w["w_o"].astype(BF16),
            jnp.concatenate([w[n].astype(BF16) for n in OUT_NAMES], axis=1))


def _shard_grads(main, outp):
    t = lambda r: main[:, PIECE_OFF[r]:PIECE_OFF[r] + ROW_REGIONS[r][1]]
    g = dict(w_ffn_down=t(0), w_ffn_gate=t(1).transpose(0, 2, 1), w_ffn_up=t(2).transpose(0, 2, 1),
             w_in=t(3).transpose(0, 2, 1), w_o=t(4))
    for i, n in enumerate(OUT_NAMES):
        g[n] = outp[:, i * BW:(i + 1) * BW]
    return g


def _gather_taps(p, name):
    mine = jnp.concatenate([p[n] for n in CONV_NAMES], axis=1).reshape(DEPTH * N_TAPS, LANE)
    rows = -(-mine.shape[0] // 8) * 8
    mine = jnp.concatenate([mine, jnp.zeros((rows - mine.shape[0], LANE), F32)])
    g = _allgather8(jnp.stack([mine, mine]), name, pltpu.VMEM)[0::2, :DEPTH * N_TAPS]
    full = g.reshape(4, DEPTH, N_TAPS, LANE).transpose(1, 2, 0, 3).reshape(DEPTH, N_TAPS, BW)
    return dict(conv_a_w=full[:, :CONV_A], conv_b_w=full[:, CONV_A:CONV_A + CONV_B], conv_d_w=full[:, CONV_A + CONV_B:])


def _flat_pack(arrs):
    flat = jnp.concatenate([a.reshape(-1).astype(F32) for a in arrs])
    rows = -(-flat.shape[0] // (8 * LANE)) * 8
    return jnp.concatenate([flat, jnp.zeros((rows * LANE - flat.shape[0],), F32)]).reshape(rows, LANE)


def _flat_unpack(packed, shapes):
    flat, out, off = packed.reshape(-1), [], 0
    for s in shapes:
        cnt = int(np.prod(s))
        out.append(flat[off:off + cnt].reshape(s))
        off += cnt
    return out


def _blockdiag_chunks(w):
    w4 = w.reshape(4, 2, 64, 64)
    z = jnp.zeros((4, 2, 64, 2, 64), F32)
    z = z.at[:, 0, :, 0, :].set(w4[:, 0]).at[:, 1, :, 1, :].set(w4[:, 1])
    return z.reshape(4, LANE, LANE)


def _blockdiag_extract(d):
    d5 = d.reshape(4, 2, 64, 2, 64)
    return jnp.stack([d5[:, 0, :, 0, :], d5[:, 1, :, 1, :]], axis=1).reshape(8, 64, 64)


SLOPES = np.asarray([2.0 ** (-8.0 * (i + 1) / N_Q) for i in range(N_Q)], np.float32)


def _layer_consts(p, fw, l):
    row = lambda a: a[l].reshape(1, -1)
    return dict(
        g1=row(p["norm1_g"]), g2=row(p["norm2_g"]), wA=fw["conv_a_w"][l], bA=row(p["conv_a_b"]),
        wx=_blockdiag_chunks(p["lru_wx"][l]), bx=row(p["lru_bx"]), wa=_blockdiag_chunks(p["lru_wa"][l]),
        ba=row(p["lru_ba"]), lam=row(p["lru_lambda"]), wB=fw["conv_b_w"][l],
        ss=jnp.stack([p["sinks"][l], jnp.asarray(SLOPES)]), wD=fw["conv_d_w"][l], bD=row(p["conv_d_b"]),
        lg=row(p["ln_d_g"]), lb=row(p["ln_d_b"]))


def _layer_fwd(x, c, fw, l, rest_of_weights=None):
    t = f"l{l}_"
    xn = _rms_fwd(x, c["g1"], t + "rms1")
    wt = lambda off, rows: Win(fw["arena"][l], None, off, rows)
    proj = _mm(xn, Win(fw["arena_in"][l], None, R_IN, IN_W), "nt", t + "proj")
    ya = _a_fwd(proj, c["wA"], c["bA"], c["wx"], c["bx"], c["wa"], c["ba"], c["lam"], t + "a_fwd")
    yb = _b_fwd(proj, c["wB"], t + "b_fwd")
    q3 = _heads(proj[:, OFF_Q:OFF_K], N_Q)
    k3 = _heads(proj[:, OFF_K:OFF_V], N_KV)
    v3 = _heads(proj[:, OFF_V:OFF_V + N_KV * HEAD_DIM], N_KV)
    yc = _unheads(_attn_fwd(q3, k3, v3, c["ss"], t + "attn_fwd"))
    cd = _d_conv_fwd(proj, c["wD"], c["bD"], t + "d_conv_fwd")
    yd = _ln_silu_fwd(cd, c["lg"], c["lb"], t + "d_ln_fwd")
    ys = (ya, yb, yc, yd)
    if fw["arena"][l] is None:
        fw["arena"][l] = rest_of_weights(yd)
    big_y = tuple(_mm(y, wt(R_OUT + i * BW, BW), "nn", t + f"out{i}") for i, y in enumerate(ys))
    merged = _merge_fwd(proj, big_y, t + "merge_fwd")
    hres = _mm(merged, wt(R_O, D_MODEL), "nn", t + "wo", add=x)
    hn = _rms_fwd(hres, c["g2"], t + "rms2")
    gg = _mm(hn, wt(R_GATE, D_FF), "nt", t + "ffn_gate")
    uu = _mm(hn, wt(R_UP, D_FF), "nt", t + "ffn_up")
    act = _swiglu_fwd(gg, uu, t + "swiglu_fwd")
    xout = _mm(act, wt(R_DOWN, D_FF), "nn", t + "ffn_down", add=hres)
    saved = dict(x=x, xn=xn, proj=proj, ys=ys, q3=q3, k3=k3, v3=v3, cd=cd, big_y=big_y, merged=merged, hres=hres,
                 hn=hn, gg=gg, uu=uu, act=act)
    return xout, saved


def _layer_bwd(dxout, s, c, fw, l, ga, weight_grads_done=None):
    t = f"l{l}_"
    gs = {}
    wt = lambda off, rows: Win(fw["arena"][l], None, off, rows)
    gt = lambda off, rows: Win(ga, None, off, rows)
    dact = _mm(dxout, wt(R_DOWN, D_FF), "nt", t + "d_act")
    ga = _mm(s["act"], dxout, "tn", t + "dw_down", out=gt(R_DOWN, D_FF))
    dgg, duu = _swiglu_bwd(s["gg"], s["uu"], dact, t + "swiglu_bwd")
    ga = _mm(dgg, s["hn"], "tn", t + "dw_gate", out=gt(R_GATE, D_FF))
    ga = _mm(duu, s["hn"], "tn", t + "dw_up", out=gt(R_UP, D_FF))
    dhn = _mm(dgg, wt(R_GATE, D_FF), "nn", t + "d_hn_g")
    dhn = _mm(duu, wt(R_UP, D_FF), "nn", t + "d_hn_u", add=dhn)
    dhres, gs["norm2_g"] = _rms_bwd(s["hres"], c["g2"], dhn, dxout, t + "rms2_bwd")
    dmerged = _mm(dhres, wt(R_O, D_MODEL), "nt", t + "d_merged")
    ga = _mm(s["merged"], dhres, "tn", t + "dw_o", out=gt(R_O, D_MODEL))
    dbig_y, dgl = _merge_bwd(s["proj"], s["big_y"], dmerged, t + "merge_bwd")
    dys = []
    for i in range(4):
        ga = _mm(s["ys"][i], dbig_y[i], "tn", t + f"dw_out{i}", out=gt(R_OUT + i * BW, BW))
        dys.append(_mm(dbig_y[i], wt(R_OUT + i * BW, BW), "nt", t + f"d_y{i}"))
    proj = s["proj"]
    (dax, dag, gs["conv_a_w"], gs["conv_a_b"], dwx, gs["lru_bx"], dwa, gs["lru_ba"], gs["lru_lambda"]) = _a_bwd(
        proj, dys[0], c["wA"], c["bA"], c["wx"], c["bx"], c["wa"], c["ba"], c["lam"], t + "a_bwd")
    gs["lru_wx"] = _blockdiag_extract(dwx)
    gs["lru_wa"] = _blockdiag_extract(dwa)
    dbv, dbc, dbb, gs["conv_b_w"] = _b_bwd(proj, dys[1], c["wB"], t + "b_bwd")
    dq3, dk3, dv3, dsink = _attn_bwd(s["q3"], s["k3"], s["v3"], _heads(dys[2], N_Q), c["ss"], t + "attn_bwd")
    gs["sinks"] = dsink[:, 0, 0]
    dcd, gs["ln_d_g"], gs["ln_d_b"] = _ln_silu_bwd(s["cd"], c["lg"], c["lb"], dys[3], t + "d_ln_bwd")
    dd1, dd2, gs["conv_d_w"], gs["conv_d_b"] = _d_conv_bwd(proj, dcd, c["wD"], t + "d_conv_bwd")
    dproj = jnp.concatenate(
        [dax, dag, dbv, dbc, dbb, _unheads(dq3), _unheads(dk3).astype(BF16), _unheads(dv3).astype(BF16), dd1, dd2,
         *dgl], axis=1)
    ga = _mm(dproj, s["xn"], "tn", t + "dw_in", out=gt(R_IN, IN_W))
    token = weight_grads_done(ga) if weight_grads_done is not None else None
    dxn = _mm(dproj, Win(fw["arena_in"][l], None, R_IN, IN_W), "nn", t + "d_xn", after=token)
    dx, gs["norm1_g"] = _rms_bwd(s["x"], c["g1"], dxn, dhres, t + "rms1_bwd")
    return dx, ga, gs


def kernel(x, norm1_g, w_in, conv_a_w, conv_a_b, lru_wx, lru_bx, lru_wa, lru_ba, lru_lambda, w_a_out, conv_b_w, w_b_out, sinks, w_c_out, conv_d_w, conv_d_b, ln_d_g, ln_d_b, w_d_out, w_o, norm2_g, w_ffn_gate, w_ffn_up, w_ffn_down, final_g, loss_target, m_norm1_g, m_w_in, m_conv_a_w, m_conv_a_b, m_lru_wx, m_lru_bx, m_lru_wa, m_lru_ba, m_lru_lambda, m_w_a_out, m_conv_b_w, m_w_b_out, m_sinks, m_w_c_out, m_conv_d_w, m_conv_d_b, m_ln_d_g, m_ln_d_b, m_w_d_out, m_w_o, m_norm2_g, m_w_ffn_gate, m_w_ffn_up, m_w_ffn_down, m_final_g, v_norm1_g, v_w_in, v_conv_a_w, v_conv_a_b, v_lru_wx, v_lru_bx, v_lru_wa, v_lru_ba, v_lru_lambda, v_w_a_out, v_conv_b_w, v_w_b_out, v_sinks, v_w_c_out, v_conv_d_w, v_conv_d_b, v_ln_d_g, v_ln_d_b, v_w_d_out, v_w_o, v_norm2_g, v_w_ffn_gate, v_w_ffn_up, v_w_ffn_down, v_final_g):
    given = dict(locals())
    p = {n: given[n] for n in NAMES}
    mom = {n: given["m_" + n] for n in NAMES}
    var = {n: given["v_" + n] for n in NAMES}
    cx, cy, cc = _coords()
    chip = 2 * cx + cy

    shards = _arena_shards(p)
    fw = _gather_taps(p, "gather_taps")
    shards0, shards1 = [s[0] for s in shards], [s[1] for s in shards]
    flight0 = _gather_start(shards0, _gather_layer(shards0, "gather_l0_in", IN_REGION), "gather_l0_rest_start",
                            REST_REGIONS)
    fw["arena_in"] = [flight0[5], None]
    fw["arena"] = [None, None]
    consts = [_layer_consts(p, fw, l) for l in range(DEPTH)]
    consts[0]["g1"] = consts[0]["g1"] + flight0[4][0:1, 0:1]
    flight1 = []

    def rest_of_layer0(after):
        sh, landing = _gather_wait(*flight0[:4], after, "gather_l0_rest_wait", REST_REGIONS)
        arena = _gather_finish(sh, landing, "gather_l0_rest_finish", REST_REGIONS)
        flight1.extend(_gather_start(shards1, arena, "gather_l1_start"))
        return flight1[5]

    h = x[0]
    saved = []
    for l in range(DEPTH):
        if l == 1:
            sh, landing = _gather_wait(*flight1[:4], h, "gather_l1_wait")
            fw["arena"][1] = fw["arena_in"][1] = _gather_finish(sh, landing, "gather_l1_finish")
        h, s = _layer_fwd(h, consts[l], fw, l, rest_of_layer0)
        saved.append(s)
    loss_vec, dh, g_final = _loss_head(h, final_g.reshape(1, -1), loss_target[0], "loss_head")
    loss = lax.psum(loss_vec[0, 0], ("x", "y", "c"))

    zero = jnp.zeros((1,), jnp.int32)
    chip_sel = chip.reshape(1).astype(jnp.int32)

    def chip_sums(ga, t):
        own, got = _own_halves(ga, cc), _swap_halves(ga, t + "grads_swap_halves")
        return [_sum_own_plus(o.reshape((1, -1, o.shape[-1])), zero, r.reshape((1, -1, r.shape[-1])),
                              t + f"grads_sum_chip{i}", BF16).reshape(o.shape) for i, (o, r) in enumerate(zip(own, got))]

    def all_sums(sums, got, t):
        return [_sum_own_plus(s, chip_sel, r, t + f"grads_sum_all{i}", F32) for i, (s, r) in enumerate(zip(sums, got))]

    gss = [None] * DEPTH
    dh, ga1, gss[1] = _layer_bwd(dh, saved[1], consts[1], fw, 1, lax.empty(ARENA_SHAPE, BF16))
    send_sems, recv_sems, bufs, token = _scatter_start(*chip_sums(ga1, "l1_"), "l1_grads_scatter_start")
    scatter0 = []

    def start_layer0_scatter(ga0):
        scatter0.extend(_scatter_start(*chip_sums(ga0, "l0_"), "l0_grads_scatter_start"))
        return scatter0[3]

    dh, _, gss[0] = _layer_bwd(dh + token[0:1, 0:1], saved[0], consts[0], fw, 0, lax.empty(ARENA_SHAPE, BF16),
                               start_layer0_scatter)
    grad_x = dh[None]

    small_full = {n: (g_final.reshape(-1) if n == "final_g" else
                      jnp.stack([gss[l][n].reshape(gss[l][n].shape[-2:] if n.startswith("conv") and n.endswith("_w")
                                                   else p[n].shape[1:]) for l in range(DEPTH)]))
                  for n in SMALL}
    part = _flat_pack([small_full[n] for n in SMALL])
    rows = part.shape[0]
    gathered = _allgather8(jnp.stack([part, part]), "gather_small_grads", pltpu.VMEM)
    small_packed = _sum_leading(gathered, "small_grads_sum")
    small_sum = _flat_unpack(small_packed, [small_full[n].shape for n in SMALL])

    bufs0 = _scatter_wait(*scatter0[:3], small_packed, "l0_grads_scatter_wait")
    bufs1 = _scatter_wait(send_sems, recv_sems, bufs, dh, "l1_grads_scatter_wait")
    red0 = all_sums(bufs0[:2], bufs0[2:], "l0_")
    red1 = all_sums(bufs1[:2], bufs1[2:], "l1_")
    sib = _swap_many(red0 + red1, "grads_swap_reduced")
    layers = [_reduced_layer(red0, sib[:2], cc), _reduced_layer(red1, sib[2:], cc)]
    g = _shard_grads(jnp.stack([m for m, _ in layers]), jnp.stack([o for _, o in layers]))
    for n, a in zip(SMALL, small_sum):
        g[n] = lax.dynamic_slice_in_dim(a, chip * LANE, LANE, axis=2) if n in CONV_NAMES else a

    delta, new_m, new_v = {}, {}, {}
    for n in BIG:
        shp = p[n].shape
        two_d = lambda a: a.reshape(-1, shp[-1])
        d, nm, nv = _adamw(two_d(p[n]), two_d(g[n]), two_d(mom[n]), two_d(var[n]), "adamw_" + n)
        delta[n], new_m[n], new_v[n] = d.reshape(shp), nm.reshape(shp), nv.reshape(shp)
    shapes = [p[n].shape for n in SMALL]
    d, nm, nv = _adamw(_flat_pack([p[n] for n in SMALL]), _flat_pack([g[n] for n in SMALL]),
                       _flat_pack([mom[n] for n in SMALL]), _flat_pack([var[n] for n in SMALL]), "adamw_small")
    for n, a, b, cval in zip(SMALL, _flat_unpack(d, shapes), _flat_unpack(nm, shapes), _flat_unpack(nv, shapes)):
        delta[n], new_m[n], new_v[n] = a, b, cval

    return (loss, grad_x, *[g[n] for n in NAMES], *[delta[n] for n in NAMES], *[new_m[n] for n in NAMES],
            *[new_v[n] for n in NAMES])
```

```python
import functools
import math

import numpy as np
import jax
import jax.numpy as jnp
from jax import lax
from jax.experimental import pallas as pl
from jax.experimental.pallas import tpu as pltpu

F32 = jnp.float32
BF16 = jnp.bfloat16
MESH = pl.DeviceIdType.MESH

D_MODEL = 1024
DEPTH = 2
BW = 512
HEAD_DIM = 64
N_Q = 8
N_KV = 2
BLK = 128
D_FF = 2816
IN_W = 8448
EPS = 1e-6
NEG_INF = -1e30
LRU_C = 8.0
CONV_A, CONV_B, CONV_D = 4, 3, 31
LANE = 128
ROW_TILE = 256
VMEM_LIMIT = 56 * 1024 * 1024
MM_VMEM_BUDGET = 36 * 1024 * 1024

C_AX, C_AG, C_BV, C_BC, C_BB = 0, 4, 8, 12, 16
OFF_Q, OFF_K, OFF_V = 2560, 3072, 3200
C_D1, C_D2 = 26, 30
OFF_GL = 4352

ADAM_LR, ADAM_B1, ADAM_B2, ADAM_EPS, ADAM_WD, ADAM_STEP = 0.001, 0.9, 0.999, 1e-08, 0.01, 10

ARENA_W = 1024
R_DOWN, R_GATE, R_UP, R_IN, R_O, R_OUT = 0, 2816, 5632, 8448, 16896, 17920
ARENA_ROWS = 19968
ROW_REGIONS = ((R_DOWN, 704), (R_GATE, 704), (R_UP, 704), (R_IN, 2112), (R_O, 256))
PIECE_OFF = (0, 704, 1408, 2112, 4224)
PIECE_ROWS = 4480
OUT_ROWS, OUT_COLS = 4 * BW, D_MODEL // 4

BIG = ("w_in", "w_a_out", "w_b_out", "w_c_out", "w_d_out", "w_o", "w_ffn_gate", "w_ffn_up", "w_ffn_down")
CONV_NAMES = ("conv_a_w", "conv_b_w", "conv_d_w")
N_TAPS = CONV_A + CONV_B + CONV_D
SMALL = ("norm1_g", "conv_a_w", "conv_a_b", "lru_wx", "lru_bx", "lru_wa", "lru_ba", "lru_lambda", "conv_b_w",
         "sinks", "conv_d_w", "conv_d_b", "ln_d_g", "ln_d_b", "norm2_g", "final_g")
NAMES = ['norm1_g', 'w_in', 'conv_a_w', 'conv_a_b', 'lru_wx', 'lru_bx', 'lru_wa', 'lru_ba', 'lru_lambda', 'w_a_out',
         'conv_b_w', 'w_b_out', 'sinks', 'w_c_out', 'conv_d_w', 'conv_d_b', 'ln_d_g', 'ln_d_b', 'w_d_out', 'w_o',
         'norm2_g', 'w_ffn_gate', 'w_ffn_up', 'w_ffn_down', 'final_g']


def _pick(n, cands, off=0):
    for c in cands:
        if n % c == 0 and off % c == 0:
            return c
    assert off == 0, (n, off)
    return n


class Win:
    def __init__(self, arena, l, off, rows):
        self.arena, self.l, self.off, self.rows = arena, l, off, rows
        self.shape = (rows, arena.shape[-1])


def _params(sem=None):
    return pltpu.CompilerParams(dimension_semantics=sem, vmem_limit_bytes=VMEM_LIMIT)


def _sig(z):
    return 1.0 / (1.0 + jnp.exp(-z))


def _dot(a, b, dims):
    return lax.dot_general(a.astype(BF16), b.astype(BF16), (dims, ((), ())), preferred_element_type=F32)


NN = ((1,), (0,))
NT = ((1,), (1,))
TN = ((0,), (0,))


def _mm(a, b, mode, name, out_dtype=F32, add=None, out=None, after=None):
    if mode == "nn":
        (m, k), n = a.shape, b.shape[1]
    elif mode == "nt":
        (m, k), n = a.shape, b.shape[0]
    else:
        (k, m), n = a.shape, b.shape[1]
    b_win = isinstance(b, Win)
    b_off = b.off if b_win else 0
    o_off = out.off if out is not None else 0
    if out is not None:
        out_dtype = out.arena.dtype
    tk = _pick(k, (2816, 2048, 1408, 1024, 768, 512, 256), b_off if mode != "nt" else 0)
    nk = k // tk
    n_off = b_off if mode == "nt" else 0
    a_bytes, b_bytes, o_bytes = a.dtype.itemsize, 2, jnp.dtype(out_dtype).itemsize

    def vmem_bytes(tm_, tn_):
        tile = tm_ * tn_
        return (2 * tk * (tm_ * a_bytes + tn_ * b_bytes) + 2 * tile * o_bytes + (tile * 4 if nk > 1 else 0)
                + (2 * tile * 4 if add is not None else 0) + tile * 4)

    pairs = [(tm_, tn_) for tm_ in (2048, 1024, 768, 512, 256, 128) for tn_ in (1024, 768, 512, 256, 128)
             if m % tm_ == 0 and o_off % tm_ == 0 and n % tn_ == 0 and n_off % tn_ == 0
             and vmem_bytes(tm_, tn_) <= MM_VMEM_BUDGET]
    tm, tn = max(pairs, key=lambda p: (p[0] * p[1], p[0]))
    dims = {"nn": NN, "nt": NT, "tn": TN}[mode]

    def body(*refs):
        a_ref, b_ref = refs[:2]
        c_ref = refs[2] if add is not None else None
        if nk == 1:
            r = _dot(a_ref[...], b_ref[...], dims)
            if add is not None:
                r = r + c_ref[...]
            refs[-1][...] = r.astype(out_dtype)
            return
        o_ref, acc = refs[-2:]
        kk = pl.program_id(2)

        @pl.when(kk == 0)
        def _():
            acc[...] = jnp.zeros_like(acc)

        acc[...] += _dot(a_ref[...], b_ref[...], dims)

        @pl.when(kk == nk - 1)
        def _():
            r = acc[...]
            if add is not None:
                r = r + c_ref[...]
            o_ref[...] = r.astype(out_dtype)

    if mode == "tn":
        a_spec = pl.BlockSpec((tk, tm), lambda i, j, q: (q, i))
    else:
        a_spec = pl.BlockSpec((tm, tk), lambda i, j, q: (i, q))
    if mode == "nt":
        b_blk, b_idx = (tn, tk), (lambda i, j, q: (b_off // tn + j, q))
    else:
        b_blk, b_idx = (tk, tn), (lambda i, j, q: (b_off // tk + q, j))
    if b_win and b.arena.ndim == 3:
        bl = b.l
        b_spec = pl.BlockSpec((None,) + b_blk, lambda i, j, q: (bl,) + b_idx(i, j, q))
    else:
        b_spec = pl.BlockSpec(b_blk, b_idx)
    plain_o = pl.BlockSpec((tm, tn), lambda i, j, q: (i, j))
    in_specs = [a_spec, b_spec] + ([plain_o] if add is not None else [])
    args = (a, b.arena if b_win else b) + ((add,) if add is not None else ())
    if after is not None:
        in_specs.append(pl.BlockSpec(after.shape, lambda i, j, q: (0, 0)))
        args = args + (after,)
    aliases = {}
    if out is None:
        o_spec, o_shape = plain_o, jax.ShapeDtypeStruct((m, n), out_dtype)
    else:
        ol = out.l
        if out.arena.ndim == 3:
            o_spec = pl.BlockSpec((None, tm, tn), lambda i, j, q: (ol, o_off // tm + i, j))
        else:
            o_spec = pl.BlockSpec((tm, tn), lambda i, j, q: (o_off // tm + i, j))
        o_shape = jax.ShapeDtypeStruct(out.arena.shape, out_dtype)
        aliases = {len(args): 0}
        in_specs.append(pl.BlockSpec(memory_space=pl.ANY))
        args = args + (out.arena,)
    return pl.pallas_call(
        body, name=name, out_shape=o_shape,
        grid=(m // tm, n // tn, nk), in_specs=in_specs, out_specs=o_spec,
        scratch_shapes=[pltpu.VMEM((tm, tn), F32)] if nk > 1 else [], input_output_aliases=aliases,
        compiler_params=_params(("parallel", "parallel", "arbitrary")),
    )(*args)


def _row_spec(cols, tr=ROW_TILE):
    return pl.BlockSpec((tr, cols), lambda i: (i, 0))


def _vec_spec(cols):
    return pl.BlockSpec((1, cols), lambda i: (0, 0))


def _rms_fwd(x, g, name):
    t, d = x.shape

    def body(x_ref, g_ref, o_ref):
        xv = x_ref[...]
        r = lax.rsqrt(jnp.mean(xv * xv, axis=1, keepdims=True) + EPS)
        o_ref[...] = (xv * r * g_ref[...]).astype(BF16)

    return pl.pallas_call(
        body, name=name, out_shape=jax.ShapeDtypeStruct((t, d), BF16), grid=(t // ROW_TILE,),
        in_specs=[_row_spec(d), _vec_spec(d)], out_specs=_row_spec(d), compiler_params=_params(("parallel",)),
    )(x, g)


def _rms_bwd(x, g, dxn, dres, name):
    t, d = x.shape

    def body(x_ref, g_ref, dy_ref, dr_ref, dx_ref, dg_ref):
        @pl.when(pl.program_id(0) == 0)
        def _():
            dg_ref[...] = jnp.zeros_like(dg_ref)

        xv = x_ref[...]
        dy = dy_ref[...]
        r = lax.rsqrt(jnp.mean(xv * xv, axis=1, keepdims=True) + EPS)
        w = dy * g_ref[...]
        dx_ref[...] = dr_ref[...] + r * w - xv * (r * r * r) * jnp.mean(w * xv, axis=1, keepdims=True)
        dg_ref[...] += jnp.sum(dy * xv * r, axis=0, keepdims=True)

    return pl.pallas_call(
        body, name=name,
        out_shape=(jax.ShapeDtypeStruct((t, d), F32), jax.ShapeDtypeStruct((1, d), F32)), grid=(t // ROW_TILE,),
        in_specs=[_row_spec(d), _vec_spec(d), _row_spec(d), _row_spec(d)], out_specs=(_row_spec(d), _vec_spec(d)),
        compiler_params=_params(("arbitrary",)),
    )(x, g, dxn, dres)


def _loss_head(x, g, tgt, name):
    t, d = x.shape

    def body(x_ref, g_ref, t_ref, loss_ref, dx_ref, dg_ref):
        @pl.when(pl.program_id(0) == 0)
        def _():
            dg_ref[...] = jnp.zeros_like(dg_ref)
            loss_ref[...] = jnp.zeros_like(loss_ref)

        xv = x_ref[...]
        gv = g_ref[...]
        r = lax.rsqrt(jnp.mean(xv * xv, axis=1, keepdims=True) + EPS)
        e = xv * r * gv - t_ref[...]
        loss_ref[...] += jnp.full(loss_ref.shape, (0.5 / d) * jnp.sum(e * e), F32)
        dy = e * (1.0 / d)
        w = dy * gv
        dx_ref[...] = r * w - xv * (r * r * r) * jnp.mean(w * xv, axis=1, keepdims=True)
        dg_ref[...] += jnp.sum(dy * xv * r, axis=0, keepdims=True)

    return pl.pallas_call(
        body, name=name,
        out_shape=(jax.ShapeDtypeStruct((1, LANE), F32), jax.ShapeDtypeStruct((t, d), F32),
                   jax.ShapeDtypeStruct((1, d), F32)),
        grid=(t // ROW_TILE,), in_specs=[_row_spec(d), _vec_spec(d), _row_spec(d)],
        out_specs=(_vec_spec(LANE), _row_spec(d), _vec_spec(d)), compiler_params=_params(("arbitrary",)),
    )(x, g, tgt)


def _swiglu_fwd(gg, uu, name):
    t, f = gg.shape

    def body(g_ref, u_ref, o_ref):
        gv = g_ref[...]
        o_ref[...] = (gv * _sig(gv) * u_ref[...]).astype(BF16)

    return pl.pallas_call(
        body, name=name, out_shape=jax.ShapeDtypeStruct((t, f), BF16), grid=(t // ROW_TILE,),
        in_specs=[_row_spec(f), _row_spec(f)], out_specs=_row_spec(f), compiler_params=_params(("parallel",)),
    )(gg, uu)


def _swiglu_bwd(gg, uu, dact, name):
    t, f = gg.shape

    def body(g_ref, u_ref, d_ref, dg_ref, du_ref):
        gv = g_ref[...]
        dv = d_ref[...]
        s = _sig(gv)
        dg_ref[...] = (dv * u_ref[...] * s * (1.0 + gv * (1.0 - s))).astype(BF16)
        du_ref[...] = (dv * gv * s).astype(BF16)

    return pl.pallas_call(
        body, name=name,
        out_shape=(jax.ShapeDtypeStruct((t, f), BF16), jax.ShapeDtypeStruct((t, f), BF16)), grid=(t // ROW_TILE,),
        in_specs=[_row_spec(f)] * 3, out_specs=(_row_spec(f), _row_spec(f)), compiler_params=_params(("parallel",)),
    )(gg, uu, dact)


MERGE_COLS = 256
MERGE_ROWS = 1024


def _gate_specs(mr):
    nb = D_MODEL // MERGE_COLS
    base = OFF_GL // MERGE_COLS
    return [pl.BlockSpec((mr, MERGE_COLS), functools.partial(lambda i, j, kk: (i, base + nb * kk + j), kk=kk))
            for kk in range(4)]


def _merge_fwd(proj, ys, name):
    t = proj.shape[0]
    mr = min(t, MERGE_ROWS)
    yspec = pl.BlockSpec((mr, MERGE_COLS), lambda i, j: (i, j))

    def body(g0, g1, g2, g3, y0, y1, y2, y3, o_ref):
        acc = _sig(g0[...]) * y0[...]
        acc += _sig(g1[...]) * y1[...]
        acc += _sig(g2[...]) * y2[...]
        acc += _sig(g3[...]) * y3[...]
        o_ref[...] = acc.astype(BF16)

    return pl.pallas_call(
        body, name=name, out_shape=jax.ShapeDtypeStruct((t, D_MODEL), BF16),
        grid=(t // mr, D_MODEL // MERGE_COLS), in_specs=_gate_specs(mr) + [yspec] * 4, out_specs=yspec,
        compiler_params=_params(("parallel", "parallel")),
    )(proj, proj, proj, proj, *ys)


def _merge_bwd(proj, ys, dmerged, name):
    t = proj.shape[0]
    mr = min(t, MERGE_ROWS)
    yspec = pl.BlockSpec((mr, MERGE_COLS), lambda i, j: (i, j))

    def body(g0, g1, g2, g3, y0, y1, y2, y3, dm_ref, *outs):
        dm = dm_ref[...]
        for gr, yr, dy_ref, dg_ref in zip((g0, g1, g2, g3), (y0, y1, y2, y3), outs[:4], outs[4:]):
            s = _sig(gr[...])
            dy_ref[...] = (dm * s).astype(BF16)
            dg_ref[...] = (dm * yr[...] * s * (1.0 - s)).astype(BF16)

    shp = jax.ShapeDtypeStruct((t, D_MODEL), BF16)
    outs = pl.pallas_call(
        body, name=name, out_shape=(shp,) * 8, grid=(t // mr, D_MODEL // MERGE_COLS),
        in_specs=_gate_specs(mr) + [yspec] * 5, out_specs=(yspec,) * 8, compiler_params=_params(("parallel", "parallel")),
    )(proj, proj, proj, proj, *ys, dmerged)
    return outs[:4], outs[4:]


def _ln_silu_fwd(cd, g, b, name):
    t, c = cd.shape

    def body(x_ref, g_ref, b_ref, o_ref):
        xv = x_ref[...]
        mu = jnp.mean(xv, axis=1, keepdims=True)
        xc = xv - mu
        rs = lax.rsqrt(jnp.mean(xc * xc, axis=1, keepdims=True) + EPS)
        z = xc * rs * g_ref[...] + b_ref[...]
        o_ref[...] = (z * _sig(z)).astype(BF16)

    return pl.pallas_call(
        body, name=name, out_shape=jax.ShapeDtypeStruct((t, c), BF16), grid=(t // ROW_TILE,),
        in_specs=[_row_spec(c), _vec_spec(c), _vec_spec(c)], out_specs=_row_spec(c),
        compiler_params=_params(("parallel",)),
    )(cd, g, b)


def _ln_silu_bwd(cd, g, b, dy, name):
    t, c = cd.shape

    def body(x_ref, g_ref, b_ref, dy_ref, dx_ref, dg_ref, db_ref):
        @pl.when(pl.program_id(0) == 0)
        def _():
            dg_ref[...] = jnp.zeros_like(dg_ref)
            db_ref[...] = jnp.zeros_like(db_ref)

        xv = x_ref[...]
        gv = g_ref[...]
        mu = jnp.mean(xv, axis=1, keepdims=True)
        xc = xv - mu
        rs = lax.rsqrt(jnp.mean(xc * xc, axis=1, keepdims=True) + EPS)
        xh = xc * rs
        z = xh * gv + b_ref[...]
        s = _sig(z)
        dz = dy_ref[...] * s * (1.0 + z * (1.0 - s))
        dg_ref[...] += jnp.sum(dz * xh, axis=0, keepdims=True)
        db_ref[...] += jnp.sum(dz, axis=0, keepdims=True)
        dxh = dz * gv
        dx_ref[...] = rs * (dxh - jnp.mean(dxh, axis=1, keepdims=True) - xh * jnp.mean(dxh * xh, axis=1, keepdims=True))

    return pl.pallas_call(
        body, name=name,
        out_shape=(jax.ShapeDtypeStruct((t, c), F32), jax.ShapeDtypeStruct((1, c), F32),
                   jax.ShapeDtypeStruct((1, c), F32)),
        grid=(t // ROW_TILE,), in_specs=[_row_spec(c), _vec_spec(c), _vec_spec(c), _row_spec(c)],
        out_specs=(_row_spec(c), _vec_spec(c), _vec_spec(c)), compiler_params=_params(("arbitrary",)),
    )(cd, g, b, dy)


def _shift_dn(x, k):
    if k == 0:
        return x
    row = lax.broadcasted_iota(jnp.int32, x.shape, 0)
    return jnp.where(row >= k, pltpu.roll(x, k, 0), 0.0)


def _shift_up(x, k):
    if k == 0:
        return x
    t = x.shape[0]
    row = lax.broadcasted_iota(jnp.int32, x.shape, 0)
    return jnp.where(row < t - k, pltpu.roll(x, t - k, 0), 0.0)


def _conv_fwd(x, w_ref, taps):
    acc = w_ref[pl.ds(taps - 1, 1), :] * x
    for k in range(taps - 1):
        acc += w_ref[pl.ds(k, 1), :] * _shift_dn(x, taps - 1 - k)
    return acc


def _conv_bwd(x, dy, w_ref, dw_ref, taps):
    dx = w_ref[pl.ds(taps - 1, 1), :] * dy
    dw_ref[pl.ds(taps - 1, 1), :] = jnp.sum(dy * x, axis=0, keepdims=True)
    for k in range(taps - 1):
        s = taps - 1 - k
        dx += w_ref[pl.ds(k, 1), :] * _shift_up(dy, s)
        dw_ref[pl.ds(k, 1), :] = jnp.sum(dy * _shift_dn(x, s), axis=0, keepdims=True)
    return dx


def _scan_fwd(a, u):
    t = a.shape[0]
    k = 1
    while k < t:
        u = u + a * _shift_dn(u, k)
        if 2 * k < t:
            a = a * _shift_dn(a, k)
        k *= 2
    return u


def _scan_rev(a, u):
    t = a.shape[0]
    k = 1
    while k < t:
        u = u + a * _shift_up(u, k)
        if 2 * k < t:
            a = a * _shift_up(a, k)
        k *= 2
    return u


def _one_minus_exp(y):
    return jnp.where(y > -1e-3, -(y + 0.5 * y * y + (1.0 / 6.0) * y * y * y), 1.0 - jnp.exp(y))


GELU_C = math.sqrt(2.0 / math.pi)


def _gelu(x):
    th = jnp.tanh(GELU_C * (x + 0.044715 * x * x * x))
    return 0.5 * x * (1.0 + th), th


def _softplus(x):
    return jnp.maximum(x, 0.0) + jnp.log(1.0 + jnp.exp(-jnp.abs(x)))


def _chunk_spec(t, blk0):
    return pl.BlockSpec((t, LANE), functools.partial(lambda c, b: (0, b + c), b=blk0))


def _tap_spec(taps):
    return pl.BlockSpec((taps, LANE), lambda c: (0, c))


def _cvec_spec():
    return pl.BlockSpec((1, LANE), lambda c: (0, c))


def _cmat_spec():
    return pl.BlockSpec((1, LANE, LANE), lambda c: (c, 0, 0))


def _lru_forward(ax, wA_ref, bA_ref, wx_ref, bx_ref, wa_ref, ba_ref, lam_ref):
    ca = _conv_fwd(ax, wA_ref, CONV_A) + bA_ref[...]
    gi = _sig(_dot(ca, wx_ref[0], NN) + bx_ref[...])
    gr = _sig(_dot(ca, wa_ref[0], NN) + ba_ref[...])
    sp = _softplus(-lam_ref[...])
    la = -LRU_C * gr * sp
    a = jnp.exp(la)
    mult = jnp.sqrt(_one_minus_exp(2.0 * la))
    h = _scan_fwd(a, ca * gi * mult)
    return ca, gi, gr, sp, a, mult, h


def _a_fwd(proj, wA, bA, wx, bx, wa, ba, lam, name):
    t = proj.shape[0]

    def body(ax_ref, ag_ref, wA_ref, bA_ref, wx_ref, bx_ref, wa_ref, ba_ref, lam_ref, o_ref):
        h = _lru_forward(ax_ref[...], wA_ref, bA_ref, wx_ref, bx_ref, wa_ref, ba_ref, lam_ref)[-1]
        o_ref[...] = (h * _gelu(ag_ref[...])[0]).astype(BF16)

    return pl.pallas_call(
        body, name=name, out_shape=jax.ShapeDtypeStruct((t, BW), BF16), grid=(BW // LANE,),
        in_specs=[_chunk_spec(t, C_AX), _chunk_spec(t, C_AG), _tap_spec(CONV_A), _cvec_spec(), _cmat_spec(),
                  _cvec_spec(), _cmat_spec(), _cvec_spec(), _cvec_spec()],
        out_specs=_chunk_spec(t, 0), compiler_params=_params(("parallel",)),
    )(proj, proj, wA, bA, wx, bx, wa, ba, lam)


def _a_bwd(proj, dya, wA, bA, wx, bx, wa, ba, lam, name):
    t = proj.shape[0]

    def body(ax_ref, ag_ref, dy_ref, wA_ref, bA_ref, wx_ref, bx_ref, wa_ref, ba_ref, lam_ref,
             dax_ref, dag_ref, dwA_ref, dbA_ref, dwx_ref, dbx_ref, dwa_ref, dba_ref, dlam_ref):
        ax = ax_ref[...]
        ag = ag_ref[...]
        dy = dy_ref[...]
        ca, gi, gr, sp, a, mult, h = _lru_forward(ax, wA_ref, bA_ref, wx_ref, bx_ref, wa_ref, ba_ref, lam_ref)
        gel, th = _gelu(ag)
        dgel = 0.5 * (1.0 + th) + 0.5 * ag * (1.0 - th * th) * GELU_C * (1.0 + 3.0 * 0.044715 * ag * ag)
        dag_ref[...] = (dy * h * dgel).astype(BF16)
        s = _scan_rev(_shift_up(a, 1), dy * gel)
        da = s * _shift_dn(h, 1)
        dca = s * gi * mult
        dgi = s * ca * mult
        dmult = s * ca * gi
        dla = da * a - dmult * a * a / mult
        dgr = dla * (-LRU_C * sp)
        dsp = jnp.sum(dla * (-LRU_C * gr), axis=0, keepdims=True)
        dlam_ref[...] = -_sig(-lam_ref[...]) * dsp
        dzi = dgi * gi * (1.0 - gi)
        dzr = dgr * gr * (1.0 - gr)
        dbx_ref[...] = jnp.sum(dzi, axis=0, keepdims=True)
        dba_ref[...] = jnp.sum(dzr, axis=0, keepdims=True)
        dwx_ref[0] = _dot(ca, dzi, TN)
        dwa_ref[0] = _dot(ca, dzr, TN)
        dca += _dot(dzi, wx_ref[0], NT) + _dot(dzr, wa_ref[0], NT)
        dbA_ref[...] = jnp.sum(dca, axis=0, keepdims=True)
        dax_ref[...] = _conv_bwd(ax, dca, wA_ref, dwA_ref, CONV_A).astype(BF16)

    big = jax.ShapeDtypeStruct((t, BW), BF16)
    vec = jax.ShapeDtypeStruct((1, BW), F32)
    mat = jax.ShapeDtypeStruct((BW // LANE, LANE, LANE), F32)
    return pl.pallas_call(
        body, name=name,
        out_shape=(big, big, jax.ShapeDtypeStruct((CONV_A, BW), F32), vec, mat, vec, mat, vec, vec),
        grid=(BW // LANE,),
        in_specs=[_chunk_spec(t, C_AX), _chunk_spec(t, C_AG), _chunk_spec(t, 0), _tap_spec(CONV_A), _cvec_spec(),
                  _cmat_spec(), _cvec_spec(), _cmat_spec(), _cvec_spec(), _cvec_spec()],
        out_specs=(_chunk_spec(t, 0), _chunk_spec(t, 0), _tap_spec(CONV_A), _cvec_spec(), _cmat_spec(), _cvec_spec(),
                   _cmat_spec(), _cvec_spec(), _cvec_spec()),
        compiler_params=_params(("parallel",)),
    )(proj, proj, dya, wA, bA, wx, bx, wa, ba, lam)


def _b_fwd(proj, wB, name):
    t = proj.shape[0]

    def body(bv_ref, bc_ref, bb_ref, w_ref, o_ref):
        o_ref[...] = (bb_ref[...] * _conv_fwd(bc_ref[...] * bv_ref[...], w_ref, CONV_B)).astype(BF16)

    return pl.pallas_call(
        body, name=name, out_shape=jax.ShapeDtypeStruct((t, BW), BF16), grid=(BW // LANE,),
        in_specs=[_chunk_spec(t, C_BV), _chunk_spec(t, C_BC), _chunk_spec(t, C_BB), _tap_spec(CONV_B)],
        out_specs=_chunk_spec(t, 0), compiler_params=_params(("parallel",)),
    )(proj, proj, proj, wB)


def _b_bwd(proj, dyb, wB, name):
    t = proj.shape[0]

    def body(bv_ref, bc_ref, bb_ref, dy_ref, w_ref, dbv_ref, dbc_ref, dbb_ref, dw_ref):
        bv = bv_ref[...]
        bc = bc_ref[...]
        dy = dy_ref[...]
        p = bc * bv
        dbb_ref[...] = (dy * _conv_fwd(p, w_ref, CONV_B)).astype(BF16)
        dp = _conv_bwd(p, dy * bb_ref[...], w_ref, dw_ref, CONV_B)
        dbc_ref[...] = (dp * bv).astype(BF16)
        dbv_ref[...] = (dp * bc).astype(BF16)

    big = jax.ShapeDtypeStruct((t, BW), BF16)
    return pl.pallas_call(
        body, name=name, out_shape=(big, big, big, jax.ShapeDtypeStruct((CONV_B, BW), F32)), grid=(BW // LANE,),
        in_specs=[_chunk_spec(t, C_BV), _chunk_spec(t, C_BC), _chunk_spec(t, C_BB), _chunk_spec(t, 0),
                  _tap_spec(CONV_B)],
        out_specs=(_chunk_spec(t, 0),) * 3 + (_tap_spec(CONV_B),), compiler_params=_params(("parallel",)),
    )(proj, proj, proj, dyb, wB)


def _d_conv_fwd(proj, wD, bD, name):
    t = proj.shape[0]

    def body(d1_ref, d2_ref, w_ref, b_ref, o_ref):
        o_ref[...] = _conv_fwd(d1_ref[...] * _sig(d2_ref[...]), w_ref, CONV_D) + b_ref[...]

    return pl.pallas_call(
        body, name=name, out_shape=jax.ShapeDtypeStruct((t, BW), F32), grid=(BW // LANE,),
        in_specs=[_chunk_spec(t, C_D1), _chunk_spec(t, C_D2), _tap_spec(CONV_D), _cvec_spec()],
        out_specs=_chunk_spec(t, 0), compiler_params=_params(("parallel",)),
    )(proj, proj, wD, bD)


def _d_conv_bwd(proj, dcd, wD, name):
    t = proj.shape[0]

    def body(d1_ref, d2_ref, dy_ref, w_ref, dd1_ref, dd2_ref, dw_ref, db_ref):
        d1 = d1_ref[...]
        s = _sig(d2_ref[...])
        dy = dy_ref[...]
        db_ref[...] = jnp.sum(dy, axis=0, keepdims=True)
        dd = _conv_bwd(d1 * s, dy, w_ref, dw_ref, CONV_D)
        dd1_ref[...] = (dd * s).astype(BF16)
        dd2_ref[...] = (dd * d1 * s * (1.0 - s)).astype(BF16)

    big = jax.ShapeDtypeStruct((t, BW), BF16)
    return pl.pallas_call(
        body, name=name,
        out_shape=(big, big, jax.ShapeDtypeStruct((CONV_D, BW), F32), jax.ShapeDtypeStruct((1, BW), F32)),
        grid=(BW // LANE,),
        in_specs=[_chunk_spec(t, C_D1), _chunk_spec(t, C_D2), _chunk_spec(t, 0), _tap_spec(CONV_D)],
        out_specs=(_chunk_spec(t, 0), _chunk_spec(t, 0), _tap_spec(CONV_D), _cvec_spec()),
        compiler_params=_params(("parallel",)),
    )(proj, proj, dcd, wD)


SCALE = HEAD_DIM ** -0.5
GROUP = N_Q // N_KV


GROWS = GROUP * BLK


def _per_head(ss_ref, row, g):
    head = lax.broadcasted_iota(jnp.int32, (GROWS, 1), 0) // BLK
    col = jnp.full((GROWS, 1), ss_ref[row, g * GROUP + GROUP - 1], F32)
    for i in range(GROUP - 1):
        col = jnp.where(head == i, ss_ref[row, g * GROUP + i], col)
    return col


def _attn_probs(q_ref, k_ref, ss_ref, g, n):
    qi = lax.broadcasted_iota(jnp.int32, (GROWS, BLK), 0) % BLK
    ki = lax.broadcasted_iota(jnp.int32, (GROWS, BLK), 1)
    dist = (qi - ki).astype(F32)
    sink = _per_head(ss_ref, 0, g)
    slope = _per_head(ss_ref, 1, g)
    s0 = pl.multiple_of(n * BLK, BLK)
    sp = pl.multiple_of(jnp.maximum(n - 1, 0) * BLK, BLK)
    q = q_ref[:, pl.ds(s0, BLK), :].reshape(GROWS, HEAD_DIM)
    kc = k_ref[0, pl.ds(s0, BLK), :]
    kp = k_ref[0, pl.ds(sp, BLK), :]
    sc = jnp.where(ki <= qi, _dot(q, kc, NT) * SCALE - slope * dist, NEG_INF)
    first = jnp.where(n >= 1, 0, BLK)
    sv = jnp.where(ki > qi + first, _dot(q, kp, NT) * SCALE - slope * (dist + BLK), NEG_INF)
    m = jnp.maximum(jnp.maximum(jnp.max(sc, axis=1, keepdims=True), jnp.max(sv, axis=1, keepdims=True)), sink)
    pc = jnp.exp(sc - m)
    pp = jnp.exp(sv - m)
    ps = jnp.exp(sink - m)
    z = jnp.sum(pc, axis=1, keepdims=True) + jnp.sum(pp, axis=1, keepdims=True) + ps
    return s0, sp, q, kc, kp, pc, pp, ps, z


def _attn_specs(t):
    qs = pl.BlockSpec((GROUP, t, HEAD_DIM), lambda g: (g, 0, 0))
    ks = pl.BlockSpec((1, t, HEAD_DIM), lambda g: (g, 0, 0))
    ss = pl.BlockSpec(memory_space=pltpu.SMEM)
    return qs, ks, ss


def _attn_fwd(q, k, v, ss, name):
    t = q.shape[1]
    qs, ks, sspec = _attn_specs(t)

    def body(q_ref, k_ref, v_ref, ss_ref, o_ref):
        g = pl.program_id(0)

        def blk(n, carry):
            s0, sp, _, _, _, pc, pp, _, z = _attn_probs(q_ref, k_ref, ss_ref, g, n)
            o = _dot(pc, v_ref[0, pl.ds(s0, BLK), :], NN) + _dot(pp, v_ref[0, pl.ds(sp, BLK), :], NN)
            o_ref[:, pl.ds(s0, BLK), :] = (o / z).astype(BF16).reshape(GROUP, BLK, HEAD_DIM)
            return carry

        lax.fori_loop(0, t // BLK, blk, 0)

    return pl.pallas_call(
        body, name=name, out_shape=jax.ShapeDtypeStruct((N_Q, t, HEAD_DIM), BF16), grid=(N_KV,),
        in_specs=[qs, ks, ks, sspec], out_specs=qs, compiler_params=_params(("parallel",)),
    )(q, k, v, ss)


def _attn_bwd(q, k, v, do, ss, name):
    t = q.shape[1]
    qs, ks, sspec = _attn_specs(t)

    def body(q_ref, k_ref, v_ref, do_ref, ss_ref, dq_ref, dk_ref, dv_ref, ds_ref):
        g = pl.program_id(0)
        dk_ref[...] = jnp.zeros_like(dk_ref)
        dv_ref[...] = jnp.zeros_like(dv_ref)

        def blk(n, dsink):
            s0, sp, q, kc, kp, pc, pp, ps, z = _attn_probs(q_ref, k_ref, ss_ref, g, n)
            rz = 1.0 / z
            pc = pc * rz
            pp = pp * rz
            do_b = do_ref[:, pl.ds(s0, BLK), :].reshape(GROWS, HEAD_DIM)
            dpc = _dot(do_b, v_ref[0, pl.ds(s0, BLK), :], NT)
            dpp = _dot(do_b, v_ref[0, pl.ds(sp, BLK), :], NT)
            delta = jnp.sum(pc * dpc, axis=1, keepdims=True) + jnp.sum(pp * dpp, axis=1, keepdims=True)
            dsc = pc * (dpc - delta)
            dsp = pp * (dpp - delta)
            dq = (_dot(dsc, kc, NN) + _dot(dsp, kp, NN)) * SCALE
            dq_ref[:, pl.ds(s0, BLK), :] = dq.astype(BF16).reshape(GROUP, BLK, HEAD_DIM)
            dk_ref[0, pl.ds(s0, BLK), :] += _dot(dsc, q, TN) * SCALE
            dk_ref[0, pl.ds(sp, BLK), :] += _dot(dsp, q, TN) * SCALE
            dv_ref[0, pl.ds(s0, BLK), :] += _dot(pc, do_b, TN)
            dv_ref[0, pl.ds(sp, BLK), :] += _dot(pp, do_b, TN)
            return dsink - ps * rz * delta

        dsink = lax.fori_loop(0, t // BLK, blk, jnp.zeros((GROWS, 1), F32))
        for i in range(GROUP):
            ds_ref[i] = jnp.full(ds_ref.shape[1:], jnp.sum(dsink[i * BLK:(i + 1) * BLK]), F32)

    kv = jax.ShapeDtypeStruct((N_KV, t, HEAD_DIM), F32)
    return pl.pallas_call(
        body, name=name,
        out_shape=(jax.ShapeDtypeStruct((N_Q, t, HEAD_DIM), BF16), kv, kv, jax.ShapeDtypeStruct((N_Q, 8, LANE), F32)),
        grid=(N_KV,), in_specs=[qs, ks, ks, qs, sspec],
        out_specs=(qs, ks, ks, pl.BlockSpec((GROUP, 8, LANE), lambda g: (g, 0, 0))),
        compiler_params=_params(("parallel",)),
    )(q, k, v, do, ss)


def _heads(x2d, n):
    t = x2d.shape[0]
    return x2d.reshape(t, n, HEAD_DIM).transpose(1, 0, 2)


def _unheads(x3d):
    n, t, _ = x3d.shape
    return x3d.transpose(1, 0, 2).reshape(t, n * HEAD_DIM)


SMALL_ELEMS = 256 * 1024
TILE_ELEMS = 640 * 1024


def _row_tile(r, c):
    if r * c <= SMALL_ELEMS:
        return r
    return _pick(r, [t for t in (512, 256, 128, 64, 32, 16, 8) if t * c <= TILE_ELEMS])


def _adamw_update(w, gv, m, v):
    nm = ADAM_B1 * m + (1.0 - ADAM_B1) * gv
    nv = ADAM_B2 * v + (1.0 - ADAM_B2) * (gv * gv)
    m_hat = nm / (1.0 - ADAM_B1 ** ADAM_STEP)
    v_hat = nv / (1.0 - ADAM_B2 ** ADAM_STEP)
    return -ADAM_LR * (m_hat / (jnp.sqrt(v_hat) + ADAM_EPS) + ADAM_WD * w), nm, nv


def _adamw(w, g, m, v, name):
    r, c = w.shape
    tr = _row_tile(r, c)
    spec = pl.BlockSpec((tr, c), lambda i: (i, 0))

    def body(w_ref, g_ref, m_ref, v_ref, d_ref, nm_ref, nv_ref):
        d_ref[...], nm_ref[...], nv_ref[...] = _adamw_update(w_ref[...], g_ref[...], m_ref[...], v_ref[...])

    shp = jax.ShapeDtypeStruct((r, c), F32)
    return pl.pallas_call(
        body, name=name, out_shape=(shp, shp, shp), grid=(r // tr,), in_specs=[spec] * 4, out_specs=(spec,) * 3,
        compiler_params=_params(("parallel",)),
    )(w, g, m, v)


def _adamw_layer(w, m, v, g, l, prev, name):
    _, r, c = w.shape
    tr = _row_tile(r, c)
    layer = pl.BlockSpec((None, tr, c), lambda i: (l, i, 0))

    def body(w_ref, m_ref, v_ref, g_ref, *rest):
        go_ref, d_ref, nm_ref, nv_ref, token = rest[-5:]
        gv = g_ref[...]
        go_ref[...] = gv
        token[...] = jnp.zeros_like(token)
        d_ref[...], nm_ref[...], nv_ref[...] = _adamw_update(w_ref[...], gv, m_ref[...], v_ref[...])

    carried = list(prev[:4]) if prev is not None else []
    shp = jax.ShapeDtypeStruct(w.shape, F32)
    return pl.pallas_call(
        body, name=name, out_shape=(shp,) * 4 + (jax.ShapeDtypeStruct((8, LANE), F32),), grid=(r // tr,),
        in_specs=[layer] * 3 + [pl.BlockSpec((tr, c), lambda i: (i, 0))] + [pl.BlockSpec(memory_space=pl.ANY)] * len(carried),
        out_specs=(layer,) * 4 + (pl.BlockSpec((8, LANE), lambda i: (0, 0)),),
        input_output_aliases={4 + i: i for i in range(len(carried))}, compiler_params=_params(("arbitrary",)),
    )(w, m, v, g, *carried)


def _sum_leading(x, name):
    n, r, c = x.shape
    tr = _row_tile(r, c)

    def body(x_ref, o_ref):
        acc = x_ref[0]
        for i in range(1, n):
            acc = acc + x_ref[i]
        o_ref[...] = acc

    return pl.pallas_call(
        body, name=name, out_shape=jax.ShapeDtypeStruct((r, c), F32), grid=(r // tr,),
        in_specs=[pl.BlockSpec((n, tr, c), lambda i: (0, i, 0))], out_specs=pl.BlockSpec((tr, c), lambda i: (i, 0)),
        compiler_params=_params(("parallel",)),
    )(x)


def _sum_own_plus(p, sel, recv, name, out_dtype):
    _, r, c = p.shape
    n = recv.shape[0]
    tr = _pick(r, (512, 448, 256, 128, 64, 16))

    def body(sel_ref, p_ref, r_ref, o_ref):
        acc = p_ref[0].astype(F32)
        for i in range(n):
            acc = acc + r_ref[i].astype(F32)
        o_ref[...] = acc.astype(out_dtype)

    grid_spec = pltpu.PrefetchScalarGridSpec(
        num_scalar_prefetch=1, grid=(r // tr,),
        in_specs=[pl.BlockSpec((1, tr, c), lambda i, s: (s[0], i, 0)), pl.BlockSpec((n, tr, c), lambda i, s: (0, i, 0))],
        out_specs=pl.BlockSpec((tr, c), lambda i, s: (i, 0)))
    return pl.pallas_call(
        body, name=name, out_shape=jax.ShapeDtypeStruct((r, c), out_dtype), grid_spec=grid_spec,
        compiler_params=_params(("parallel",)),
    )(sel, p, recv)


def _coords():
    return lax.axis_index("x"), lax.axis_index("y"), lax.axis_index("c")


def _allgather8(x2, name, space):
    _, m, n = x2.shape

    def body(x_ref, out_ref, send_sems, recv_sems, local_sem):
        x, y, c = _coords()
        me, sibling = (x, y, c), (x, y, 1 - c)
        chips = [(1 - x, y), (x, 1 - y), (1 - x, 1 - y)]
        mine_src = x_ref.at[c]

        def rows(px, py, pc):
            return out_ref.at[4 * px + 2 * py + pc]

        def copy(k, block, to, src=None):
            return pltpu.make_async_remote_copy(
                src_ref=rows(*block) if src is None else src, dst_ref=rows(*block),
                send_sem=send_sems.at[k], recv_sem=recv_sems.at[k], device_id=to, device_id_type=MESH)

        mine = pltpu.make_async_copy(mine_src, rows(*me), local_sem)
        mine.start()
        first = [copy(0, me, sibling, src=mine_src)]
        first += [copy(1 + j, me, (*chip, c), src=mine_src) for j, chip in enumerate(chips)]
        for cp in first:
            cp.start()
        passed = [copy(4 + j, (*chip, c), sibling) for j, chip in enumerate(chips)]
        for j, chip in enumerate(chips):
            copy(1 + j, (*chip, c), me).wait_recv()
            passed[j].start()
        copy(0, sibling, me).wait_recv()
        for j, chip in enumerate(chips):
            copy(4 + j, (*chip, 1 - c), me).wait_recv()
        for cp in first + passed:
            cp.wait_send()
        mine.wait()

    return pl.pallas_call(
        body, name=name, out_shape=jax.ShapeDtypeStruct((8, m, n), x2.dtype),
        in_specs=[pl.BlockSpec(memory_space=space)], out_specs=pl.BlockSpec(memory_space=space),
        scratch_shapes=[pltpu.SemaphoreType.DMA((7,)), pltpu.SemaphoreType.DMA((7,)), pltpu.SemaphoreType.DMA],
        compiler_params=pltpu.CompilerParams(vmem_limit_bytes=VMEM_LIMIT),
    )(x2)


N_REG = len(ROW_REGIONS) + 1


def _chip_window(ref, lead, r, j):
    view = ref if lead is None else ref.at[lead]
    if r < len(ROW_REGIONS):
        off, rows = ROW_REGIONS[r]
        return view.at[pl.ds(pl.multiple_of(off + j * rows, 16), rows), :]
    return view.at[pl.ds(R_OUT, OUT_ROWS), pl.ds(pl.multiple_of(j * OUT_COLS, LANE), OUT_COLS)]


HBM_SPEC = pl.BlockSpec(memory_space=pltpu.HBM)
SEM_SPEC = pl.BlockSpec(memory_space=pltpu.SEMAPHORE)


def _half_window(ref, r, j, h):
    if r < len(ROW_REGIONS):
        off, rows = ROW_REGIONS[r]
        return ref.at[pl.ds(pl.multiple_of(off + j * rows + h * (rows // 2), 16), rows // 2), :]
    half = OUT_ROWS // 2
    return ref.at[pl.ds(pl.multiple_of(R_OUT + h * half, 16), half),
                  pl.ds(pl.multiple_of(j * OUT_COLS, LANE), OUT_COLS)]


def _other_chips():
    x, y, _ = _coords()
    return [(1 - x, y), (x, 1 - y), (1 - x, 1 - y)]


def _ici_copies(srcs, arena_ref, send_sems, recv_sems, regions):
    x, y, c = _coords()
    sends, arrivals = [], []
    for k, (cx, cy) in enumerate(_other_chips()):
        for r in regions:
            def remote(src, j):
                return pltpu.make_async_remote_copy(
                    src_ref=src, dst_ref=_half_window(arena_ref, r, j, c), send_sem=send_sems.at[3 * r + k],
                    recv_sem=recv_sems.at[3 * r + k], device_id=(cx, cy, c), device_id_type=MESH)
            rows = srcs[r].shape[0] // 2
            sends.append(remote(srcs[r].at[pl.ds(pl.multiple_of(c * rows, 16), rows), :], 2 * x + y))
            arrivals.append(remote(_half_window(arena_ref, r, 2 * cx + cy, c), 2 * cx + cy))
    return sends, arrivals


def _sibling_copies(srcs, arena_ref, send_sems, recv_sems, regions):
    x, y, c = _coords()
    sends, arrivals = [], []

    def remote(win, r, k, src=None):
        return pltpu.make_async_remote_copy(
            src_ref=win if src is None else src, dst_ref=win, send_sem=send_sems.at[r, k],
            recv_sem=recv_sems.at[r, k], device_id=(x, y, 1 - c), device_id_type=MESH)

    for k, (cx, cy) in enumerate(_other_chips()):
        for r in regions:
            sends.append(remote(_half_window(arena_ref, r, 2 * cx + cy, c), r, k))
            arrivals.append(remote(_half_window(arena_ref, r, 2 * cx + cy, 1 - c), r, k))
    for r in regions:
        own = _chip_window(arena_ref, None, r, 2 * x + y)
        sends.append(remote(own, r, 3, src=srcs[r]))
        arrivals.append(remote(own, r, 3))
    return sends, arrivals


ICI_SEMS = pltpu.SemaphoreType.DMA((3 * N_REG,))
SIBLING_SEMS = pltpu.SemaphoreType.DMA((N_REG, 4))
ARENA_SHAPE = (ARENA_ROWS, ARENA_W)
ALL_REGIONS = tuple(range(N_REG))
IN_REGION = (3,)
REST_REGIONS = (0, 1, 2, 4, 5)


def _gather_layer(shards, name, regions=ALL_REGIONS):
    def body(*refs):
        srcs, arena_ref = refs[:N_REG], refs[N_REG]
        ici_send, ici_recv, sib_send, sib_recv = refs[N_REG + 1:]
        sends, arrivals = _ici_copies(srcs, arena_ref, ici_send, ici_recv, regions)
        passes, landings = _sibling_copies(srcs, arena_ref, sib_send, sib_recv, regions)
        for cp in sends + passes[len(arrivals):]:
            cp.start()
        for arrival, onward in zip(arrivals, passes):
            arrival.wait_recv()
            onward.start()
        for cp in landings:
            cp.wait_recv()
        for cp in sends + passes:
            cp.wait_send()

    return pl.pallas_call(
        body, name=name, out_shape=jax.ShapeDtypeStruct(ARENA_SHAPE, BF16),
        in_specs=[pl.BlockSpec(memory_space=pl.ANY)] * N_REG, out_specs=pl.BlockSpec(memory_space=pl.ANY),
        scratch_shapes=[ICI_SEMS, ICI_SEMS, SIBLING_SEMS, SIBLING_SEMS],
    )(*shards)


def _gather_start(shards, after, name, regions=ALL_REGIONS):
    def body(*refs):
        srcs, arena_ref = refs[:N_REG], refs[N_REG]
        send_sems, recv_sems = refs[N_REG + 2], refs[N_REG + 3]
        token = refs[-1]
        for cp in _ici_copies(srcs, arena_ref, send_sems, recv_sems, regions)[0]:
            cp.start()
        token[...] = jnp.zeros_like(token)

    hbm = lambda a: pltpu.with_memory_space_constraint(a, pltpu.HBM)
    outs = pl.pallas_call(
        body, name=name,
        out_shape=(ICI_SEMS, ICI_SEMS, *[pltpu.HBM(s.shape, s.dtype) for s in shards],
                   pltpu.HBM(ARENA_SHAPE, BF16), pltpu.HBM(after.shape, after.dtype),
                   jax.ShapeDtypeStruct((8, LANE), F32)),
        in_specs=[HBM_SPEC] * (N_REG + 2),
        out_specs=(SEM_SPEC, SEM_SPEC, *[HBM_SPEC] * (N_REG + 2), pl.BlockSpec(memory_space=pltpu.VMEM)),
        input_output_aliases={i: 2 + i for i in range(N_REG + 2)},
        compiler_params=pltpu.CompilerParams(has_side_effects=pltpu.SideEffectType.DATAFLOW_SIDE_EFFECTING),
    )(*[hbm(s) for s in shards], hbm(lax.empty(ARENA_SHAPE, BF16)), hbm(after))
    return outs[0], outs[1], outs[2:2 + N_REG], outs[2 + N_REG], outs[-1], outs[3 + N_REG]


def _gather_wait(send_sems, recv_sems, shards, arena, after, name, regions=ALL_REGIONS):
    def body(*refs):
        srcs, arena_ref = refs[:N_REG], refs[N_REG]
        sends, arrivals = _ici_copies(srcs, arena_ref, refs[N_REG + 1], refs[N_REG + 2], regions)
        for cp in sends:
            cp.wait_send()
        for cp in arrivals:
            cp.wait_recv()

    outs = pl.pallas_call(
        body, name=name,
        out_shape=(*[pltpu.HBM(s.shape, s.dtype) for s in shards], pltpu.HBM(ARENA_SHAPE, BF16)),
        in_specs=[HBM_SPEC] * (N_REG + 1) + [SEM_SPEC, SEM_SPEC, pl.BlockSpec(memory_space=pl.ANY)],
        out_specs=(HBM_SPEC,) * (N_REG + 1), input_output_aliases={i: i for i in range(N_REG + 1)},
        compiler_params=pltpu.CompilerParams(has_side_effects=pltpu.SideEffectType.DATAFLOW_SIDE_EFFECTING),
    )(*shards, arena, send_sems, recv_sems, after)
    return outs[:N_REG], outs[N_REG]


def _gather_finish(shards, arena, name, regions=ALL_REGIONS):
    def body(*refs):
        srcs, arena_ref = refs[:N_REG], refs[N_REG + 1]
        sends, arrivals = _sibling_copies(srcs, arena_ref, refs[N_REG + 2], refs[N_REG + 3], regions)
        for cp in sends:
            cp.start()
        for cp in arrivals:
            cp.wait_recv()
        for cp in sends:
            cp.wait_send()

    return pl.pallas_call(
        body, name=name, out_shape=jax.ShapeDtypeStruct(ARENA_SHAPE, BF16),
        in_specs=[pl.BlockSpec(memory_space=pl.ANY)] * (N_REG + 1), out_specs=pl.BlockSpec(memory_space=pl.ANY),
        scratch_shapes=[SIBLING_SEMS, SIBLING_SEMS], input_output_aliases={N_REG: 0},
    )(*shards, arena)


HALF_PIECE_OFF = tuple(o // 2 for o in PIECE_OFF)
HALF_PIECE_ROWS = PIECE_ROWS // 2
HALF_OUT_ROWS = OUT_ROWS // 2
SWAP_SEMS = pltpu.SemaphoreType.DMA((4 * N_REG,))
SCATTER_SEMS = pltpu.SemaphoreType.DMA((6,))


def _packed_shapes(slots, dtype):
    return (jax.ShapeDtypeStruct((slots, HALF_PIECE_ROWS, ARENA_W), dtype),
            jax.ShapeDtypeStruct((slots, HALF_OUT_ROWS, OUT_COLS), dtype))


def _swap_halves(ga, name):
    def body(g_ref, main_ref, outp_ref, send_sems, recv_sems):
        x, y, c = _coords()
        cps = []
        for j in range(4):
            for r in range(N_REG):
                if r < len(ROW_REGIONS):
                    dst = main_ref.at[j, pl.ds(HALF_PIECE_OFF[r], ROW_REGIONS[r][1] // 2), :]
                else:
                    dst = outp_ref.at[j]
                cps.append(pltpu.make_async_remote_copy(
                    src_ref=_half_window(g_ref, r, j, 1 - c), dst_ref=dst, send_sem=send_sems.at[j * N_REG + r],
                    recv_sem=recv_sems.at[j * N_REG + r], device_id=(x, y, 1 - c), device_id_type=MESH))
        for cp in cps:
            cp.start()
        for cp in cps:
            cp.wait()

    return pl.pallas_call(
        body, name=name, out_shape=_packed_shapes(4, ga.dtype),
        in_specs=[pl.BlockSpec(memory_space=pl.ANY)], out_specs=(pl.BlockSpec(memory_space=pl.ANY),) * 2,
        scratch_shapes=[SWAP_SEMS, SWAP_SEMS],
    )(ga)


def _own_halves(ga, cc):
    mains = [jnp.concatenate([lax.dynamic_slice(ga, (off + j * rows + cc * (rows // 2), 0), (rows // 2, ARENA_W))
                              for off, rows in ROW_REGIONS]) for j in range(4)]
    outs = [lax.dynamic_slice(ga, (R_OUT + cc * HALF_OUT_ROWS, j * OUT_COLS), (HALF_OUT_ROWS, OUT_COLS))
            for j in range(4)]
    return jnp.stack(mains), jnp.stack(outs)


def _scatter_copies(main_ref, outp_ref, rmain_ref, routp_ref, send_sems, recv_sems):
    _, _, c = _coords()
    cps = []
    for k, (cx, cy) in enumerate(_other_chips()):
        for i, (src, dst) in enumerate(((main_ref, rmain_ref), (outp_ref, routp_ref))):
            cps.append(pltpu.make_async_remote_copy(
                src_ref=src.at[2 * cx + cy], dst_ref=dst.at[k], send_sem=send_sems.at[2 * k + i],
                recv_sem=recv_sems.at[2 * k + i], device_id=(cx, cy, c), device_id_type=MESH))
    return cps


def _scatter_start(main, outp, name):
    def body(main_ref, outp_ref, rmain_ref, routp_ref, send_sems, recv_sems, *rest):
        for cp in _scatter_copies(main_ref, outp_ref, rmain_ref, routp_ref, send_sems, recv_sems):
            cp.start()
        rest[-1][...] = jnp.zeros_like(rest[-1])

    hbm = lambda a: pltpu.with_memory_space_constraint(a, pltpu.HBM)
    land = [lax.empty(s.shape, s.dtype) for s in _packed_shapes(3, main.dtype)]
    bufs = [main, outp, *land]
    outs = pl.pallas_call(
        body, name=name,
        out_shape=(SCATTER_SEMS, SCATTER_SEMS, *[pltpu.HBM(b.shape, b.dtype) for b in bufs],
                   jax.ShapeDtypeStruct((8, LANE), F32)),
        in_specs=[HBM_SPEC] * 4, out_specs=(SEM_SPEC, SEM_SPEC, *[HBM_SPEC] * 4, pl.BlockSpec(memory_space=pltpu.VMEM)),
        input_output_aliases={i: 2 + i for i in range(4)},
        compiler_params=pltpu.CompilerParams(has_side_effects=pltpu.SideEffectType.DATAFLOW_SIDE_EFFECTING),
    )(*[hbm(b) for b in bufs])
    return outs[0], outs[1], outs[2:6], outs[6]


def _scatter_wait(send_sems, recv_sems, bufs, after, name):
    def body(main_ref, outp_ref, rmain_ref, routp_ref, send_sems, recv_sems, *rest):
        for cp in _scatter_copies(main_ref, outp_ref, rmain_ref, routp_ref, send_sems, recv_sems):
            cp.wait_send()
            cp.wait_recv()

    return pl.pallas_call(
        body, name=name, out_shape=tuple(pltpu.HBM(b.shape, b.dtype) for b in bufs),
        in_specs=[HBM_SPEC] * 4 + [SEM_SPEC, SEM_SPEC, pl.BlockSpec(memory_space=pl.ANY)],
        out_specs=(HBM_SPEC,) * 4, input_output_aliases={i: i for i in range(4)},
        compiler_params=pltpu.CompilerParams(has_side_effects=pltpu.SideEffectType.DATAFLOW_SIDE_EFFECTING),
    )(*bufs, send_sems, recv_sems, after)


def _swap_many(arrs, name):
    n = len(arrs)

    def body(*refs):
        x, y, c = _coords()
        send_sems, recv_sems = refs[2 * n], refs[2 * n + 1]
        cps = [pltpu.make_async_remote_copy(
            src_ref=refs[i], dst_ref=refs[n + i], send_sem=send_sems.at[i], recv_sem=recv_sems.at[i],
            device_id=(x, y, 1 - c), device_id_type=MESH) for i in range(n)]
        for cp in cps:
            cp.start()
        for cp in cps:
            cp.wait()

    return pl.pallas_call(
        body, name=name, out_shape=tuple(jax.ShapeDtypeStruct(a.shape, a.dtype) for a in arrs),
        in_specs=[pl.BlockSpec(memory_space=pl.ANY)] * n, out_specs=(pl.BlockSpec(memory_space=pl.ANY),) * n,
        scratch_shapes=[pltpu.SemaphoreType.DMA((n,)), pltpu.SemaphoreType.DMA((n,))],
    )(*arrs)


def _join_halves(mine, theirs, cc):
    return jnp.where(cc == 0, jnp.concatenate([mine, theirs]), jnp.concatenate([theirs, mine]))


def _reduced_layer(red, sib, cc):
    parts = []
    for (_, rows), off in zip(ROW_REGIONS, HALF_PIECE_OFF):
        parts.append(_join_halves(red[0][off:off + rows // 2], sib[0][off:off + rows // 2], cc))
    return jnp.concatenate(parts), _join_halves(red[1], sib[1], cc)


OUT_NAMES = ("w_a_out", "w_b_out", "w_c_out", "w_d_out")


def _arena_shards(w):
    t = lambda a: a.astype(BF16).transpose(0, 2, 1)
    return (w["w_ffn_down"].astype(BF16), t(w["w_ffn_gate"]), t(w["w_ffn_up"]), t(w["w_in"]), w["w_o"].astype(BF16),
            jnp.concatenate([w[n].astype(BF16) for n in OUT_NAMES], axis=1))


def _shard_grads(main, outp):
    t = lambda r: main[PIECE_OFF[r]:PIECE_OFF[r] + ROW_REGIONS[r][1]]
    g = dict(w_ffn_down=t(0), w_ffn_gate=t(1).T, w_ffn_up=t(2).T, w_in=t(3).T, w_o=t(4))
    for i, n in enumerate(OUT_NAMES):
        g[n] = outp[i * BW:(i + 1) * BW]
    return g


def _gather_taps(p, name):
    mine = jnp.concatenate([p[n] for n in CONV_NAMES], axis=1).reshape(DEPTH * N_TAPS, LANE)
    rows = -(-mine.shape[0] // 8) * 8
    mine = jnp.concatenate([mine, jnp.zeros((rows - mine.shape[0], LANE), F32)])
    g = _allgather8(jnp.stack([mine, mine]), name, pltpu.VMEM)[0::2, :DEPTH * N_TAPS]
    full = g.reshape(4, DEPTH, N_TAPS, LANE).transpose(1, 2, 0, 3).reshape(DEPTH, N_TAPS, BW)
    return dict(conv_a_w=full[:, :CONV_A], conv_b_w=full[:, CONV_A:CONV_A + CONV_B], conv_d_w=full[:, CONV_A + CONV_B:])


def _flat_pack(arrs):
    flat = jnp.concatenate([a.reshape(-1).astype(F32) for a in arrs])
    rows = -(-flat.shape[0] // (8 * LANE)) * 8
    return jnp.concatenate([flat, jnp.zeros((rows * LANE - flat.shape[0],), F32)]).reshape(rows, LANE)


def _flat_unpack(packed, shapes):
    flat, out, off = packed.reshape(-1), [], 0
    for s in shapes:
        cnt = int(np.prod(s))
        out.append(flat[off:off + cnt].reshape(s))
        off += cnt
    return out


def _blockdiag_chunks(w):
    w4 = w.reshape(4, 2, 64, 64)
    z = jnp.zeros((4, 2, 64, 2, 64), F32)
    z = z.at[:, 0, :, 0, :].set(w4[:, 0]).at[:, 1, :, 1, :].set(w4[:, 1])
    return z.reshape(4, LANE, LANE)


def _blockdiag_extract(d):
    d5 = d.reshape(4, 2, 64, 2, 64)
    return jnp.stack([d5[:, 0, :, 0, :], d5[:, 1, :, 1, :]], axis=1).reshape(8, 64, 64)


SLOPES = np.asarray([2.0 ** (-8.0 * (i + 1) / N_Q) for i in range(N_Q)], np.float32)


def _layer_consts(p, fw, l):
    row = lambda a: a[l].reshape(1, -1)
    return dict(
        g1=row(p["norm1_g"]), g2=row(p["norm2_g"]), wA=fw["conv_a_w"][l], bA=row(p["conv_a_b"]),
        wx=_blockdiag_chunks(p["lru_wx"][l]), bx=row(p["lru_bx"]), wa=_blockdiag_chunks(p["lru_wa"][l]),
        ba=row(p["lru_ba"]), lam=row(p["lru_lambda"]), wB=fw["conv_b_w"][l],
        ss=jnp.stack([p["sinks"][l], jnp.asarray(SLOPES)]), wD=fw["conv_d_w"][l], bD=row(p["conv_d_b"]),
        lg=row(p["ln_d_g"]), lb=row(p["ln_d_b"]))


def _layer_fwd(x, c, fw, l, rest_of_weights=None):
    t = f"l{l}_"
    xn = _rms_fwd(x, c["g1"], t + "rms1")
    wt = lambda off, rows: Win(fw["arena"][l], None, off, rows)
    proj = _mm(xn, Win(fw["arena_in"][l], None, R_IN, IN_W), "nt", t + "proj")
    ya = _a_fwd(proj, c["wA"], c["bA"], c["wx"], c["bx"], c["wa"], c["ba"], c["lam"], t + "a_fwd")
    yb = _b_fwd(proj, c["wB"], t + "b_fwd")
    q3 = _heads(proj[:, OFF_Q:OFF_K], N_Q)
    k3 = _heads(proj[:, OFF_K:OFF_V], N_KV)
    v3 = _heads(proj[:, OFF_V:OFF_V + N_KV * HEAD_DIM], N_KV)
    yc = _unheads(_attn_fwd(q3, k3, v3, c["ss"], t + "attn_fwd"))
    cd = _d_conv_fwd(proj, c["wD"], c["bD"], t + "d_conv_fwd")
    yd = _ln_silu_fwd(cd, c["lg"], c["lb"], t + "d_ln_fwd")
    ys = (ya, yb, yc, yd)
    if fw["arena"][l] is None:
        fw["arena"][l] = rest_of_weights(yd)
    big_y = tuple(_mm(y, wt(R_OUT + i * BW, BW), "nn", t + f"out{i}") for i, y in enumerate(ys))
    merged = _merge_fwd(proj, big_y, t + "merge_fwd")
    hres = _mm(merged, wt(R_O, D_MODEL), "nn", t + "wo", add=x)
    hn = _rms_fwd(hres, c["g2"], t + "rms2")
    gg = _mm(hn, wt(R_GATE, D_FF), "nt", t + "ffn_gate")
    uu = _mm(hn, wt(R_UP, D_FF), "nt", t + "ffn_up")
    act = _swiglu_fwd(gg, uu, t + "swiglu_fwd")
    xout = _mm(act, wt(R_DOWN, D_FF), "nn", t + "ffn_down", add=hres)
    saved = dict(x=x, xn=xn, proj=proj, ys=ys, q3=q3, k3=k3, v3=v3, cd=cd, big_y=big_y, merged=merged, hres=hres,
                 hn=hn, gg=gg, uu=uu, act=act)
    return xout, saved


def _layer_bwd(dxout, s, c, fw, l, ga, weight_grads_done=None):
    t = f"l{l}_"
    gs = {}
    wt = lambda off, rows: Win(fw["arena"][l], None, off, rows)
    gt = lambda off, rows: Win(ga, None, off, rows)
    dact = _mm(dxout, wt(R_DOWN, D_FF), "nt", t + "d_act")
    ga = _mm(s["act"], dxout, "tn", t + "dw_down", out=gt(R_DOWN, D_FF))
    dgg, duu = _swiglu_bwd(s["gg"], s["uu"], dact, t + "swiglu_bwd")
    ga = _mm(dgg, s["hn"], "tn", t + "dw_gate", out=gt(R_GATE, D_FF))
    ga = _mm(duu, s["hn"], "tn", t + "dw_up", out=gt(R_UP, D_FF))
    dhn = _mm(dgg, wt(R_GATE, D_FF), "nn", t + "d_hn_g")
    dhn = _mm(duu, wt(R_UP, D_FF), "nn", t + "d_hn_u", add=dhn)
    dhres, gs["norm2_g"] = _rms_bwd(s["hres"], c["g2"], dhn, dxout, t + "rms2_bwd")
    dmerged = _mm(dhres, wt(R_O, D_MODEL), "nt", t + "d_merged")
    ga = _mm(s["merged"], dhres, "tn", t + "dw_o", out=gt(R_O, D_MODEL))
    dbig_y, dgl = _merge_bwd(s["proj"], s["big_y"], dmerged, t + "merge_bwd")
    dys = []
    for i in range(4):
        ga = _mm(s["ys"][i], dbig_y[i], "tn", t + f"dw_out{i}", out=gt(R_OUT + i * BW, BW))
        dys.append(_mm(dbig_y[i], wt(R_OUT + i * BW, BW), "nt", t + f"d_y{i}"))
    proj = s["proj"]
    (dax, dag, gs["conv_a_w"], gs["conv_a_b"], dwx, gs["lru_bx"], dwa, gs["lru_ba"], gs["lru_lambda"]) = _a_bwd(
        proj, dys[0], c["wA"], c["bA"], c["wx"], c["bx"], c["wa"], c["ba"], c["lam"], t + "a_bwd")
    gs["lru_wx"] = _blockdiag_extract(dwx)
    gs["lru_wa"] = _blockdiag_extract(dwa)
    dbv, dbc, dbb, gs["conv_b_w"] = _b_bwd(proj, dys[1], c["wB"], t + "b_bwd")
    dq3, dk3, dv3, dsink = _attn_bwd(s["q3"], s["k3"], s["v3"], _heads(dys[2], N_Q), c["ss"], t + "attn_bwd")
    gs["sinks"] = dsink[:, 0, 0]
    dcd, gs["ln_d_g"], gs["ln_d_b"] = _ln_silu_bwd(s["cd"], c["lg"], c["lb"], dys[3], t + "d_ln_bwd")
    dd1, dd2, gs["conv_d_w"], gs["conv_d_b"] = _d_conv_bwd(proj, dcd, c["wD"], t + "d_conv_bwd")
    dproj = jnp.concatenate(
        [dax, dag, dbv, dbc, dbb, _unheads(dq3), _unheads(dk3).astype(BF16), _unheads(dv3).astype(BF16), dd1, dd2,
         *dgl], axis=1)
    ga = _mm(dproj, s["xn"], "tn", t + "dw_in", out=gt(R_IN, IN_W))
    token = weight_grads_done(ga) if weight_grads_done is not None else None
    dxn = _mm(dproj, Win(fw["arena_in"][l], None, R_IN, IN_W), "nn", t + "d_xn", after=token)
    dx, gs["norm1_g"] = _rms_bwd(s["x"], c["g1"], dxn, dhres, t + "rms1_bwd")
    return dx, ga, gs


def kernel(x, norm1_g, w_in, conv_a_w, conv_a_b, lru_wx, lru_bx, lru_wa, lru_ba, lru_lambda, w_a_out, conv_b_w, w_b_out, sinks, w_c_out, conv_d_w, conv_d_b, ln_d_g, ln_d_b, w_d_out, w_o, norm2_g, w_ffn_gate, w_ffn_up, w_ffn_down, final_g, loss_target, m_norm1_g, m_w_in, m_conv_a_w, m_conv_a_b, m_lru_wx, m_lru_bx, m_lru_wa, m_lru_ba, m_lru_lambda, m_w_a_out, m_conv_b_w, m_w_b_out, m_sinks, m_w_c_out, m_conv_d_w, m_conv_d_b, m_ln_d_g, m_ln_d_b, m_w_d_out, m_w_o, m_norm2_g, m_w_ffn_gate, m_w_ffn_up, m_w_ffn_down, m_final_g, v_norm1_g, v_w_in, v_conv_a_w, v_conv_a_b, v_lru_wx, v_lru_bx, v_lru_wa, v_lru_ba, v_lru_lambda, v_w_a_out, v_conv_b_w, v_w_b_out, v_sinks, v_w_c_out, v_conv_d_w, v_conv_d_b, v_ln_d_g, v_ln_d_b, v_w_d_out, v_w_o, v_norm2_g, v_w_ffn_gate, v_w_ffn_up, v_w_ffn_down, v_final_g):
    given = dict(locals())
    p = {n: given[n] for n in NAMES}
    mom = {n: given["m_" + n] for n in NAMES}
    var = {n: given["v_" + n] for n in NAMES}
    cx, cy, cc = _coords()
    chip = 2 * cx + cy

    shards = _arena_shards(p)
    fw = _gather_taps(p, "gather_taps")
    shards0, shards1 = [s[0] for s in shards], [s[1] for s in shards]
    flight0 = _gather_start(shards0, _gather_layer(shards0, "gather_l0_in", IN_REGION), "gather_l0_rest_start",
                            REST_REGIONS)
    fw["arena_in"] = [flight0[5], None]
    fw["arena"] = [None, None]
    consts = [_layer_consts(p, fw, l) for l in range(DEPTH)]
    consts[0]["g1"] = consts[0]["g1"] + flight0[4][0:1, 0:1]
    flight1 = []

    def rest_of_layer0(after):
        sh, landing = _gather_wait(*flight0[:4], after, "gather_l0_rest_wait", REST_REGIONS)
        arena = _gather_finish(sh, landing, "gather_l0_rest_finish", REST_REGIONS)
        flight1.extend(_gather_start(shards1, arena, "gather_l1_start"))
        return flight1[5]

    h = x[0]
    saved = []
    for l in range(DEPTH):
        if l == 1:
            sh, landing = _gather_wait(*flight1[:4], h, "gather_l1_wait")
            fw["arena"][1] = fw["arena_in"][1] = _gather_finish(sh, landing, "gather_l1_finish")
        h, s = _layer_fwd(h, consts[l], fw, l, rest_of_layer0)
        saved.append(s)
    loss_vec, dh, g_final = _loss_head(h, final_g.reshape(1, -1), loss_target[0], "loss_head")
    loss = lax.psum(loss_vec[0, 0], ("x", "y", "c"))

    zero = jnp.zeros((1,), jnp.int32)
    chip_sel = chip.reshape(1).astype(jnp.int32)

    def chip_sums(ga, t):
        own, got = _own_halves(ga, cc), _swap_halves(ga, t + "grads_swap_halves")
        return [_sum_own_plus(o.reshape((1, -1, o.shape[-1])), zero, r.reshape((1, -1, r.shape[-1])),
                              t + f"grads_sum_chip{i}", BF16).reshape(o.shape) for i, (o, r) in enumerate(zip(own, got))]

    def all_sums(sums, got, t):
        return [_sum_own_plus(s, chip_sel, r, t + f"grads_sum_all{i}", F32) for i, (s, r) in enumerate(zip(sums, got))]

    gss = [None] * DEPTH
    dh, ga1, gss[1] = _layer_bwd(dh, saved[1], consts[1], fw, 1, lax.empty(ARENA_SHAPE, BF16))
    send_sems, recv_sems, bufs, token = _scatter_start(*chip_sums(ga1, "l1_"), "l1_grads_scatter_start")
    scatter0 = []

    def start_layer0_scatter(ga0):
        scatter0.extend(_scatter_start(*chip_sums(ga0, "l0_"), "l0_grads_scatter_start"))
        return scatter0[3]

    dh, _, gss[0] = _layer_bwd(dh + token[0:1, 0:1], saved[0], consts[0], fw, 0, lax.empty(ARENA_SHAPE, BF16),
                               start_layer0_scatter)
    grad_x = dh[None]

    def finish_layer(bufs_l, l, carried, t):
        red = all_sums(bufs_l[:2], bufs_l[2:], t)
        g_l = _shard_grads(*_reduced_layer(red, _swap_many(red, t + "grads_swap_reduced"), cc))
        return {n: _adamw_layer(p[n], mom[n], var[n], g_l[n], l, carried and carried[n], t + "adamw_" + n)
                for n in BIG}

    big = finish_layer(_scatter_wait(send_sems, recv_sems, bufs, scatter0[3], "l1_grads_scatter_wait"), 1, None, "l1_")

    small_full = {n: (g_final.reshape(-1) if n == "final_g" else
                      jnp.stack([gss[l][n].reshape(gss[l][n].shape[-2:] if n.startswith("conv") and n.endswith("_w")
                                                   else p[n].shape[1:]) for l in range(DEPTH)]))
                  for n in SMALL}
    part = _flat_pack([small_full[n] for n in SMALL])
    rows = part.shape[0]
    gathered = _allgather8(jnp.stack([part, part]), "gather_small_grads", pltpu.VMEM)
    small_packed = _sum_leading(gathered, "small_grads_sum")
    small_sum = _flat_unpack(small_packed, [small_full[n].shape for n in SMALL])

    after = small_packed[:8] + big["w_in"][4]
    big = finish_layer(_scatter_wait(*scatter0[:3], after, "l0_grads_scatter_wait"), 0, big, "l0_")
    g, delta, new_m, new_v = ({n: big[n][i] for n in BIG} for i in range(4))
    for n, a in zip(SMALL, small_sum):
        g[n] = lax.dynamic_slice_in_dim(a, chip * LANE, LANE, axis=2) if n in CONV_NAMES else a

    shapes = [p[n].shape for n in SMALL]
    d, nm, nv = _adamw(_flat_pack([p[n] for n in SMALL]), _flat_pack([g[n] for n in SMALL]),
                       _flat_pack([mom[n] for n in SMALL]), _flat_pack([var[n] for n in SMALL]), "adamw_small")
    for n, a, b, cval in zip(SMALL, _flat_unpack(d, shapes), _flat_unpack(nm, shapes), _flat_unpack(nv, shapes)):
        delta[n], new_m[n], new_v[n] = a, b, cval

    return (loss, grad_x, *[g[n] for n in NAMES], *[delta[n] for n in NAMES], *[new_m[n] for n in NAMES],
            *[new_v[n] for n in NAMES])
```

```python
import functools
import math

import numpy as np
import jax
import jax.numpy as jnp
from jax import lax
from jax.experimental import pallas as pl
from jax.experimental.pallas import tpu as pltpu

F32 = jnp.float32
BF16 = jnp.bfloat16
MESH = pl.DeviceIdType.MESH

D_MODEL = 1024
DEPTH = 2
BW = 512
HEAD_DIM = 64
N_Q = 8
N_KV = 2
BLK = 128
D_FF = 2816
IN_W = 8448
EPS = 1e-6
NEG_INF = -1e30
LRU_C = 8.0
CONV_A, CONV_B, CONV_D = 4, 3, 31
LANE = 128
ROW_TILE = 256
VMEM_LIMIT = 56 * 1024 * 1024
MM_VMEM_BUDGET = 36 * 1024 * 1024

C_AX, C_AG, C_BV, C_BC, C_BB = 0, 4, 8, 12, 16
OFF_Q, OFF_K, OFF_V = 2560, 3072, 3200
C_D1, C_D2 = 26, 30
OFF_GL = 4352

ADAM_LR, ADAM_B1, ADAM_B2, ADAM_EPS, ADAM_WD, ADAM_STEP = 0.001, 0.9, 0.999, 1e-08, 0.01, 10

ARENA_W = 1024
R_DOWN, R_GATE, R_UP, R_IN, R_O, R_OUT = 0, 2816, 5632, 8448, 16896, 17920
ARENA_ROWS = 19968
ROW_REGIONS = ((R_DOWN, 704), (R_GATE, 704), (R_UP, 704), (R_IN, 2112), (R_O, 256))
PIECE_OFF = (0, 704, 1408, 2112, 4224)
PIECE_ROWS = 4480
OUT_ROWS, OUT_COLS = 4 * BW, D_MODEL // 4

BIG = ("w_in", "w_a_out", "w_b_out", "w_c_out", "w_d_out", "w_o", "w_ffn_gate", "w_ffn_up", "w_ffn_down")
CONV_NAMES = ("conv_a_w", "conv_b_w", "conv_d_w")
N_TAPS = CONV_A + CONV_B + CONV_D
SMALL = ("norm1_g", "conv_a_w", "conv_a_b", "lru_wx", "lru_bx", "lru_wa", "lru_ba", "lru_lambda", "conv_b_w",
         "sinks", "conv_d_w", "conv_d_b", "ln_d_g", "ln_d_b", "norm2_g", "final_g")
NAMES = ['norm1_g', 'w_in', 'conv_a_w', 'conv_a_b', 'lru_wx', 'lru_bx', 'lru_wa', 'lru_ba', 'lru_lambda', 'w_a_out',
         'conv_b_w', 'w_b_out', 'sinks', 'w_c_out', 'conv_d_w', 'conv_d_b', 'ln_d_g', 'ln_d_b', 'w_d_out', 'w_o',
         'norm2_g', 'w_ffn_gate', 'w_ffn_up', 'w_ffn_down', 'final_g']


def _pick(n, cands, off=0):
    for c in cands:
        if n % c == 0 and off % c == 0:
            return c
    assert off == 0, (n, off)
    return n


class Win:
    def __init__(self, arena, l, off, rows):
        self.arena, self.l, self.off, self.rows = arena, l, off, rows
        self.shape = (rows, arena.shape[-1])


def _params(sem=None):
    return pltpu.CompilerParams(dimension_semantics=sem, vmem_limit_bytes=VMEM_LIMIT)


def _sig(z):
    return 1.0 / (1.0 + jnp.exp(-z))


def _dot(a, b, dims):
    return lax.dot_general(a.astype(BF16), b.astype(BF16), (dims, ((), ())), preferred_element_type=F32)


NN = ((1,), (0,))
NT = ((1,), (1,))
TN = ((0,), (0,))


def _mm(a, b, mode, name, out_dtype=F32, add=None, out=None, after=None):
    if mode == "nn":
        (m, k), n = a.shape, b.shape[1]
    elif mode == "nt":
        (m, k), n = a.shape, b.shape[0]
    else:
        (k, m), n = a.shape, b.shape[1]
    b_win = isinstance(b, Win)
    b_off = b.off if b_win else 0
    o_off = out.off if out is not None else 0
    if out is not None:
        out_dtype = out.arena.dtype
    tk = _pick(k, (2816, 2048, 1408, 1024, 768, 512, 256), b_off if mode != "nt" else 0)
    nk = k // tk
    n_off = b_off if mode == "nt" else 0
    a_bytes, b_bytes, o_bytes = a.dtype.itemsize, 2, jnp.dtype(out_dtype).itemsize

    def vmem_bytes(tm_, tn_):
        tile = tm_ * tn_
        return (2 * tk * (tm_ * a_bytes + tn_ * b_bytes) + 2 * tile * o_bytes + (tile * 4 if nk > 1 else 0)
                + (2 * tile * 4 if add is not None else 0) + tile * 4)

    pairs = [(tm_, tn_) for tm_ in (2048, 1024, 768, 512, 256, 128) for tn_ in (1024, 768, 512, 256, 128)
             if m % tm_ == 0 and o_off % tm_ == 0 and n % tn_ == 0 and n_off % tn_ == 0
             and vmem_bytes(tm_, tn_) <= MM_VMEM_BUDGET]
    tm, tn = max(pairs, key=lambda p: (p[0] * p[1], p[0]))
    dims = {"nn": NN, "nt": NT, "tn": TN}[mode]

    def body(*refs):
        a_ref, b_ref = refs[:2]
        c_ref = refs[2] if add is not None else None
        if nk == 1:
            r = _dot(a_ref[...], b_ref[...], dims)
            if add is not None:
                r = r + c_ref[...]
            refs[-1][...] = r.astype(out_dtype)
            return
        o_ref, acc = refs[-2:]
        kk = pl.program_id(2)

        @pl.when(kk == 0)
        def _():
            acc[...] = jnp.zeros_like(acc)

        acc[...] += _dot(a_ref[...], b_ref[...], dims)

        @pl.when(kk == nk - 1)
        def _():
            r = acc[...]
            if add is not None:
                r = r + c_ref[...]
            o_ref[...] = r.astype(out_dtype)

    if mode == "tn":
        a_spec = pl.BlockSpec((tk, tm), lambda i, j, q: (q, i))
    else:
        a_spec = pl.BlockSpec((tm, tk), lambda i, j, q: (i, q))
    if mode == "nt":
        b_blk, b_idx = (tn, tk), (lambda i, j, q: (b_off // tn + j, q))
    else:
        b_blk, b_idx = (tk, tn), (lambda i, j, q: (b_off // tk + q, j))
    if b_win and b.arena.ndim == 3:
        bl = b.l
        b_spec = pl.BlockSpec((None,) + b_blk, lambda i, j, q: (bl,) + b_idx(i, j, q))
    else:
        b_spec = pl.BlockSpec(b_blk, b_idx)
    plain_o = pl.BlockSpec((tm, tn), lambda i, j, q: (i, j))
    in_specs = [a_spec, b_spec] + ([plain_o] if add is not None else [])
    args = (a, b.arena if b_win else b) + ((add,) if add is not None else ())
    if after is not None:
        in_specs.append(pl.BlockSpec(after.shape, lambda i, j, q: (0, 0)))
        args = args + (after,)
    aliases = {}
    if out is None:
        o_spec, o_shape = plain_o, jax.ShapeDtypeStruct((m, n), out_dtype)
    else:
        ol = out.l
        if out.arena.ndim == 3:
            o_spec = pl.BlockSpec((None, tm, tn), lambda i, j, q: (ol, o_off // tm + i, j))
        else:
            o_spec = pl.BlockSpec((tm, tn), lambda i, j, q: (o_off // tm + i, j))
        o_shape = jax.ShapeDtypeStruct(out.arena.shape, out_dtype)
        aliases = {len(args): 0}
        in_specs.append(pl.BlockSpec(memory_space=pl.ANY))
        args = args + (out.arena,)
    return pl.pallas_call(
        body, name=name, out_shape=o_shape,
        grid=(m // tm, n // tn, nk), in_specs=in_specs, out_specs=o_spec,
        scratch_shapes=[pltpu.VMEM((tm, tn), F32)] if nk > 1 else [], input_output_aliases=aliases,
        compiler_params=_params(("parallel", "parallel", "arbitrary")),
    )(*args)


def _row_spec(cols, tr=ROW_TILE):
    return pl.BlockSpec((tr, cols), lambda i: (i, 0))


def _vec_spec(cols):
    return pl.BlockSpec((1, cols), lambda i: (0, 0))


def _rms_fwd(x, g, name):
    t, d = x.shape

    def body(x_ref, g_ref, o_ref):
        xv = x_ref[...]
        r = lax.rsqrt(jnp.mean(xv * xv, axis=1, keepdims=True) + EPS)
        o_ref[...] = (xv * r * g_ref[...]).astype(BF16)

    return pl.pallas_call(
        body, name=name, out_shape=jax.ShapeDtypeStruct((t, d), BF16), grid=(t // ROW_TILE,),
        in_specs=[_row_spec(d), _vec_spec(d)], out_specs=_row_spec(d), compiler_params=_params(("parallel",)),
    )(x, g)


def _rms_bwd(x, g, dxn, dres, name):
    t, d = x.shape

    def body(x_ref, g_ref, dy_ref, dr_ref, dx_ref, dg_ref):
        @pl.when(pl.program_id(0) == 0)
        def _():
            dg_ref[...] = jnp.zeros_like(dg_ref)

        xv = x_ref[...]
        dy = dy_ref[...]
        r = lax.rsqrt(jnp.mean(xv * xv, axis=1, keepdims=True) + EPS)
        w = dy * g_ref[...]
        dx_ref[...] = dr_ref[...] + r * w - xv * (r * r * r) * jnp.mean(w * xv, axis=1, keepdims=True)
        dg_ref[...] += jnp.sum(dy * xv * r, axis=0, keepdims=True)

    return pl.pallas_call(
        body, name=name,
        out_shape=(jax.ShapeDtypeStruct((t, d), F32), jax.ShapeDtypeStruct((1, d), F32)), grid=(t // ROW_TILE,),
        in_specs=[_row_spec(d), _vec_spec(d), _row_spec(d), _row_spec(d)], out_specs=(_row_spec(d), _vec_spec(d)),
        compiler_params=_params(("arbitrary",)),
    )(x, g, dxn, dres)


def _loss_head(x, g, tgt, name):
    t, d = x.shape

    def body(x_ref, g_ref, t_ref, loss_ref, dx_ref, dg_ref):
        @pl.when(pl.program_id(0) == 0)
        def _():
            dg_ref[...] = jnp.zeros_like(dg_ref)
            loss_ref[...] = jnp.zeros_like(loss_ref)

        xv = x_ref[...]
        gv = g_ref[...]
        r = lax.rsqrt(jnp.mean(xv * xv, axis=1, keepdims=True) + EPS)
        e = xv * r * gv - t_ref[...]
        loss_ref[...] += jnp.full(loss_ref.shape, (0.5 / d) * jnp.sum(e * e), F32)
        dy = e * (1.0 / d)
        w = dy * gv
        dx_ref[...] = r * w - xv * (r * r * r) * jnp.mean(w * xv, axis=1, keepdims=True)
        dg_ref[...] += jnp.sum(dy * xv * r, axis=0, keepdims=True)

    return pl.pallas_call(
        body, name=name,
        out_shape=(jax.ShapeDtypeStruct((1, LANE), F32), jax.ShapeDtypeStruct((t, d), F32),
                   jax.ShapeDtypeStruct((1, d), F32)),
        grid=(t // ROW_TILE,), in_specs=[_row_spec(d), _vec_spec(d), _row_spec(d)],
        out_specs=(_vec_spec(LANE), _row_spec(d), _vec_spec(d)), compiler_params=_params(("arbitrary",)),
    )(x, g, tgt)


def _swiglu_fwd(gg, uu, name):
    t, f = gg.shape

    def body(g_ref, u_ref, o_ref):
        gv = g_ref[...]
        o_ref[...] = (gv * _sig(gv) * u_ref[...]).astype(BF16)

    return pl.pallas_call(
        body, name=name, out_shape=jax.ShapeDtypeStruct((t, f), BF16), grid=(t // ROW_TILE,),
        in_specs=[_row_spec(f), _row_spec(f)], out_specs=_row_spec(f), compiler_params=_params(("parallel",)),
    )(gg, uu)


def _swiglu_bwd(gg, uu, dact, name):
    t, f = gg.shape

    def body(g_ref, u_ref, d_ref, dg_ref, du_ref):
        gv = g_ref[...]
        dv = d_ref[...]
        s = _sig(gv)
        dg_ref[...] = (dv * u_ref[...] * s * (1.0 + gv * (1.0 - s))).astype(BF16)
        du_ref[...] = (dv * gv * s).astype(BF16)

    return pl.pallas_call(
        body, name=name,
        out_shape=(jax.ShapeDtypeStruct((t, f), BF16), jax.ShapeDtypeStruct((t, f), BF16)), grid=(t // ROW_TILE,),
        in_specs=[_row_spec(f)] * 3, out_specs=(_row_spec(f), _row_spec(f)), compiler_params=_params(("parallel",)),
    )(gg, uu, dact)


MERGE_COLS = 256
MERGE_ROWS = 1024


def _gate_specs(mr):
    nb = D_MODEL // MERGE_COLS
    base = OFF_GL // MERGE_COLS
    return [pl.BlockSpec((mr, MERGE_COLS), functools.partial(lambda i, j, kk: (i, base + nb * kk + j), kk=kk))
            for kk in range(4)]


def _merge_fwd(proj, ys, name):
    t = proj.shape[0]
    mr = min(t, MERGE_ROWS)
    yspec = pl.BlockSpec((mr, MERGE_COLS), lambda i, j: (i, j))

    def body(g0, g1, g2, g3, y0, y1, y2, y3, o_ref):
        acc = _sig(g0[...]) * y0[...]
        acc += _sig(g1[...]) * y1[...]
        acc += _sig(g2[...]) * y2[...]
        acc += _sig(g3[...]) * y3[...]
        o_ref[...] = acc.astype(BF16)

    return pl.pallas_call(
        body, name=name, out_shape=jax.ShapeDtypeStruct((t, D_MODEL), BF16),
        grid=(t // mr, D_MODEL // MERGE_COLS), in_specs=_gate_specs(mr) + [yspec] * 4, out_specs=yspec,
        compiler_params=_params(("parallel", "parallel")),
    )(proj, proj, proj, proj, *ys)


def _merge_bwd(proj, ys, dmerged, name):
    t = proj.shape[0]
    mr = min(t, MERGE_ROWS)
    yspec = pl.BlockSpec((mr, MERGE_COLS), lambda i, j: (i, j))

    def body(g0, g1, g2, g3, y0, y1, y2, y3, dm_ref, *outs):
        dm = dm_ref[...]
        for gr, yr, dy_ref, dg_ref in zip((g0, g1, g2, g3), (y0, y1, y2, y3), outs[:4], outs[4:]):
            s = _sig(gr[...])
            dy_ref[...] = (dm * s).astype(BF16)
            dg_ref[...] = (dm * yr[...] * s * (1.0 - s)).astype(BF16)

    shp = jax.ShapeDtypeStruct((t, D_MODEL), BF16)
    outs = pl.pallas_call(
        body, name=name, out_shape=(shp,) * 8, grid=(t // mr, D_MODEL // MERGE_COLS),
        in_specs=_gate_specs(mr) + [yspec] * 5, out_specs=(yspec,) * 8, compiler_params=_params(("parallel", "parallel")),
    )(proj, proj, proj, proj, *ys, dmerged)
    return outs[:4], outs[4:]


def _ln_silu_fwd(cd, g, b, name):
    t, c = cd.shape

    def body(x_ref, g_ref, b_ref, o_ref):
        xv = x_ref[...]
        mu = jnp.mean(xv, axis=1, keepdims=True)
        xc = xv - mu
        rs = lax.rsqrt(jnp.mean(xc * xc, axis=1, keepdims=True) + EPS)
        z = xc * rs * g_ref[...] + b_ref[...]
        o_ref[...] = (z * _sig(z)).astype(BF16)

    return pl.pallas_call(
        body, name=name, out_shape=jax.ShapeDtypeStruct((t, c), BF16), grid=(t // ROW_TILE,),
        in_specs=[_row_spec(c), _vec_spec(c), _vec_spec(c)], out_specs=_row_spec(c),
        compiler_params=_params(("parallel",)),
    )(cd, g, b)


def _ln_silu_bwd(cd, g, b, dy, name):
    t, c = cd.shape

    def body(x_ref, g_ref, b_ref, dy_ref, dx_ref, dg_ref, db_ref):
        @pl.when(pl.program_id(0) == 0)
        def _():
            dg_ref[...] = jnp.zeros_like(dg_ref)
            db_ref[...] = jnp.zeros_like(db_ref)

        xv = x_ref[...]
        gv = g_ref[...]
        mu = jnp.mean(xv, axis=1, keepdims=True)
        xc = xv - mu
        rs = lax.rsqrt(jnp.mean(xc * xc, axis=1, keepdims=True) + EPS)
        xh = xc * rs
        z = xh * gv + b_ref[...]
        s = _sig(z)
        dz = dy_ref[...] * s * (1.0 + z * (1.0 - s))
        dg_ref[...] += jnp.sum(dz * xh, axis=0, keepdims=True)
        db_ref[...] += jnp.sum(dz, axis=0, keepdims=True)
        dxh = dz * gv
        dx_ref[...] = rs * (dxh - jnp.mean(dxh, axis=1, keepdims=True) - xh * jnp.mean(dxh * xh, axis=1, keepdims=True))

    return pl.pallas_call(
        body, name=name,
        out_shape=(jax.ShapeDtypeStruct((t, c), F32), jax.ShapeDtypeStruct((1, c), F32),
                   jax.ShapeDtypeStruct((1, c), F32)),
        grid=(t // ROW_TILE,), in_specs=[_row_spec(c), _vec_spec(c), _vec_spec(c), _row_spec(c)],
        out_specs=(_row_spec(c), _vec_spec(c), _vec_spec(c)), compiler_params=_params(("arbitrary",)),
    )(cd, g, b, dy)


def _shift_dn(x, k):
    if k == 0:
        return x
    row = lax.broadcasted_iota(jnp.int32, x.shape, 0)
    return jnp.where(row >= k, pltpu.roll(x, k, 0), 0.0)


def _shift_up(x, k):
    if k == 0:
        return x
    t = x.shape[0]
    row = lax.broadcasted_iota(jnp.int32, x.shape, 0)
    return jnp.where(row < t - k, pltpu.roll(x, t - k, 0), 0.0)


def _conv_fwd(x, w_ref, taps):
    acc = w_ref[pl.ds(taps - 1, 1), :] * x
    for k in range(taps - 1):
        acc += w_ref[pl.ds(k, 1), :] * _shift_dn(x, taps - 1 - k)
    return acc


def _conv_bwd(x, dy, w_ref, dw_ref, taps):
    dx = w_ref[pl.ds(taps - 1, 1), :] * dy
    dw_ref[pl.ds(taps - 1, 1), :] = jnp.sum(dy * x, axis=0, keepdims=True)
    for k in range(taps - 1):
        s = taps - 1 - k
        dx += w_ref[pl.ds(k, 1), :] * _shift_up(dy, s)
        dw_ref[pl.ds(k, 1), :] = jnp.sum(dy * _shift_dn(x, s), axis=0, keepdims=True)
    return dx


def _scan_fwd(a, u):
    t = a.shape[0]
    k = 1
    while k < t:
        u = u + a * _shift_dn(u, k)
        if 2 * k < t:
            a = a * _shift_dn(a, k)
        k *= 2
    return u


def _scan_rev(a, u):
    t = a.shape[0]
    k = 1
    while k < t:
        u = u + a * _shift_up(u, k)
        if 2 * k < t:
            a = a * _shift_up(a, k)
        k *= 2
    return u


def _one_minus_exp(y):
    return jnp.where(y > -1e-3, -(y + 0.5 * y * y + (1.0 / 6.0) * y * y * y), 1.0 - jnp.exp(y))


GELU_C = math.sqrt(2.0 / math.pi)


def _gelu(x):
    th = jnp.tanh(GELU_C * (x + 0.044715 * x * x * x))
    return 0.5 * x * (1.0 + th), th


def _softplus(x):
    return jnp.maximum(x, 0.0) + jnp.log(1.0 + jnp.exp(-jnp.abs(x)))


def _chunk_spec(t, blk0):
    return pl.BlockSpec((t, LANE), functools.partial(lambda c, b: (0, b + c), b=blk0))


def _tap_spec(taps):
    return pl.BlockSpec((taps, LANE), lambda c: (0, c))


def _cvec_spec():
    return pl.BlockSpec((1, LANE), lambda c: (0, c))


def _cmat_spec():
    return pl.BlockSpec((1, LANE, LANE), lambda c: (c, 0, 0))


def _lru_forward(ax, wA_ref, bA_ref, wx_ref, bx_ref, wa_ref, ba_ref, lam_ref):
    ca = _conv_fwd(ax, wA_ref, CONV_A) + bA_ref[...]
    gi = _sig(_dot(ca, wx_ref[0], NN) + bx_ref[...])
    gr = _sig(_dot(ca, wa_ref[0], NN) + ba_ref[...])
    sp = _softplus(-lam_ref[...])
    la = -LRU_C * gr * sp
    a = jnp.exp(la)
    mult = jnp.sqrt(_one_minus_exp(2.0 * la))
    h = _scan_fwd(a, ca * gi * mult)
    return ca, gi, gr, sp, a, mult, h


def _a_fwd(proj, wA, bA, wx, bx, wa, ba, lam, name):
    t = proj.shape[0]

    def body(ax_ref, ag_ref, wA_ref, bA_ref, wx_ref, bx_ref, wa_ref, ba_ref, lam_ref, o_ref):
        h = _lru_forward(ax_ref[...], wA_ref, bA_ref, wx_ref, bx_ref, wa_ref, ba_ref, lam_ref)[-1]
        o_ref[...] = (h * _gelu(ag_ref[...])[0]).astype(BF16)

    return pl.pallas_call(
        body, name=name, out_shape=jax.ShapeDtypeStruct((t, BW), BF16), grid=(BW // LANE,),
        in_specs=[_chunk_spec(t, C_AX), _chunk_spec(t, C_AG), _tap_spec(CONV_A), _cvec_spec(), _cmat_spec(),
                  _cvec_spec(), _cmat_spec(), _cvec_spec(), _cvec_spec()],
        out_specs=_chunk_spec(t, 0), compiler_params=_params(("parallel",)),
    )(proj, proj, wA, bA, wx, bx, wa, ba, lam)


def _a_bwd(proj, dya, wA, bA, wx, bx, wa, ba, lam, name):
    t = proj.shape[0]

    def body(ax_ref, ag_ref, dy_ref, wA_ref, bA_ref, wx_ref, bx_ref, wa_ref, ba_ref, lam_ref,
             dax_ref, dag_ref, dwA_ref, dbA_ref, dwx_ref, dbx_ref, dwa_ref, dba_ref, dlam_ref):
        ax = ax_ref[...]
        ag = ag_ref[...]
        dy = dy_ref[...]
        ca, gi, gr, sp, a, mult, h = _lru_forward(ax, wA_ref, bA_ref, wx_ref, bx_ref, wa_ref, ba_ref, lam_ref)
        gel, th = _gelu(ag)
        dgel = 0.5 * (1.0 + th) + 0.5 * ag * (1.0 - th * th) * GELU_C * (1.0 + 3.0 * 0.044715 * ag * ag)
        dag_ref[...] = (dy * h * dgel).astype(BF16)
        s = _scan_rev(_shift_up(a, 1), dy * gel)
        da = s * _shift_dn(h, 1)
        dca = s * gi * mult
        dgi = s * ca * mult
        dmult = s * ca * gi
        dla = da * a - dmult * a * a / mult
        dgr = dla * (-LRU_C * sp)
        dsp = jnp.sum(dla * (-LRU_C * gr), axis=0, keepdims=True)
        dlam_ref[...] = -_sig(-lam_ref[...]) * dsp
        dzi = dgi * gi * (1.0 - gi)
        dzr = dgr * gr * (1.0 - gr)
        dbx_ref[...] = jnp.sum(dzi, axis=0, keepdims=True)
        dba_ref[...] = jnp.sum(dzr, axis=0, keepdims=True)
        dwx_ref[0] = _dot(ca, dzi, TN)
        dwa_ref[0] = _dot(ca, dzr, TN)
        dca += _dot(dzi, wx_ref[0], NT) + _dot(dzr, wa_ref[0], NT)
        dbA_ref[...] = jnp.sum(dca, axis=0, keepdims=True)
        dax_ref[...] = _conv_bwd(ax, dca, wA_ref, dwA_ref, CONV_A).astype(BF16)

    big = jax.ShapeDtypeStruct((t, BW), BF16)
    vec = jax.ShapeDtypeStruct((1, BW), F32)
    mat = jax.ShapeDtypeStruct((BW // LANE, LANE, LANE), F32)
    return pl.pallas_call(
        body, name=name,
        out_shape=(big, big, jax.ShapeDtypeStruct((CONV_A, BW), F32), vec, mat, vec, mat, vec, vec),
        grid=(BW // LANE,),
        in_specs=[_chunk_spec(t, C_AX), _chunk_spec(t, C_AG), _chunk_spec(t, 0), _tap_spec(CONV_A), _cvec_spec(),
                  _cmat_spec(), _cvec_spec(), _cmat_spec(), _cvec_spec(), _cvec_spec()],
        out_specs=(_chunk_spec(t, 0), _chunk_spec(t, 0), _tap_spec(CONV_A), _cvec_spec(), _cmat_spec(), _cvec_spec(),
                   _cmat_spec(), _cvec_spec(), _cvec_spec()),
        compiler_params=_params(("parallel",)),
    )(proj, proj, dya, wA, bA, wx, bx, wa, ba, lam)


def _b_fwd(proj, wB, name):
    t = proj.shape[0]

    def body(bv_ref, bc_ref, bb_ref, w_ref, o_ref):
        o_ref[...] = (bb_ref[...] * _conv_fwd(bc_ref[...] * bv_ref[...], w_ref, CONV_B)).astype(BF16)

    return pl.pallas_call(
        body, name=name, out_shape=jax.ShapeDtypeStruct((t, BW), BF16), grid=(BW // LANE,),
        in_specs=[_chunk_spec(t, C_BV), _chunk_spec(t, C_BC), _chunk_spec(t, C_BB), _tap_spec(CONV_B)],
        out_specs=_chunk_spec(t, 0), compiler_params=_params(("parallel",)),
    )(proj, proj, proj, wB)


def _b_bwd(proj, dyb, wB, name):
    t = proj.shape[0]

    def body(bv_ref, bc_ref, bb_ref, dy_ref, w_ref, dbv_ref, dbc_ref, dbb_ref, dw_ref):
        bv = bv_ref[...]
        bc = bc_ref[...]
        dy = dy_ref[...]
        p = bc * bv
        dbb_ref[...] = (dy * _conv_fwd(p, w_ref, CONV_B)).astype(BF16)
        dp = _conv_bwd(p, dy * bb_ref[...], w_ref, dw_ref, CONV_B)
        dbc_ref[...] = (dp * bv).astype(BF16)
        dbv_ref[...] = (dp * bc).astype(BF16)

    big = jax.ShapeDtypeStruct((t, BW), BF16)
    return pl.pallas_call(
        body, name=name, out_shape=(big, big, big, jax.ShapeDtypeStruct((CONV_B, BW), F32)), grid=(BW // LANE,),
        in_specs=[_chunk_spec(t, C_BV), _chunk_spec(t, C_BC), _chunk_spec(t, C_BB), _chunk_spec(t, 0),
                  _tap_spec(CONV_B)],
        out_specs=(_chunk_spec(t, 0),) * 3 + (_tap_spec(CONV_B),), compiler_params=_params(("parallel",)),
    )(proj, proj, proj, dyb, wB)


def _d_conv_fwd(proj, wD, bD, name):
    t = proj.shape[0]

    def body(d1_ref, d2_ref, w_ref, b_ref, o_ref):
        o_ref[...] = _conv_fwd(d1_ref[...] * _sig(d2_ref[...]), w_ref, CONV_D) + b_ref[...]

    return pl.pallas_call(
        body, name=name, out_shape=jax.ShapeDtypeStruct((t, BW), F32), grid=(BW // LANE,),
        in_specs=[_chunk_spec(t, C_D1), _chunk_spec(t, C_D2), _tap_spec(CONV_D), _cvec_spec()],
        out_specs=_chunk_spec(t, 0), compiler_params=_params(("parallel",)),
    )(proj, proj, wD, bD)


def _d_conv_bwd(proj, dcd, wD, name):
    t = proj.shape[0]

    def body(d1_ref, d2_ref, dy_ref, w_ref, dd1_ref, dd2_ref, dw_ref, db_ref):
        d1 = d1_ref[...]
        s = _sig(d2_ref[...])
        dy = dy_ref[...]
        db_ref[...] = jnp.sum(dy, axis=0, keepdims=True)
        dd = _conv_bwd(d1 * s, dy, w_ref, dw_ref, CONV_D)
        dd1_ref[...] = (dd * s).astype(BF16)
        dd2_ref[...] = (dd * d1 * s * (1.0 - s)).astype(BF16)

    big = jax.ShapeDtypeStruct((t, BW), BF16)
    return pl.pallas_call(
        body, name=name,
        out_shape=(big, big, jax.ShapeDtypeStruct((CONV_D, BW), F32), jax.ShapeDtypeStruct((1, BW), F32)),
        grid=(BW // LANE,),
        in_specs=[_chunk_spec(t, C_D1), _chunk_spec(t, C_D2), _chunk_spec(t, 0), _tap_spec(CONV_D)],
        out_specs=(_chunk_spec(t, 0), _chunk_spec(t, 0), _tap_spec(CONV_D), _cvec_spec()),
        compiler_params=_params(("parallel",)),
    )(proj, proj, dcd, wD)


SCALE = HEAD_DIM ** -0.5
GROUP = N_Q // N_KV


GROWS = GROUP * BLK


def _per_head(ss_ref, row, g):
    head = lax.broadcasted_iota(jnp.int32, (GROWS, 1), 0) // BLK
    col = jnp.full((GROWS, 1), ss_ref[row, g * GROUP + GROUP - 1], F32)
    for i in range(GROUP - 1):
        col = jnp.where(head == i, ss_ref[row, g * GROUP + i], col)
    return col


def _attn_probs(q_ref, k_ref, ss_ref, g, n):
    qi = lax.broadcasted_iota(jnp.int32, (GROWS, BLK), 0) % BLK
    ki = lax.broadcasted_iota(jnp.int32, (GROWS, BLK), 1)
    dist = (qi - ki).astype(F32)
    sink = _per_head(ss_ref, 0, g)
    slope = _per_head(ss_ref, 1, g)
    s0 = pl.multiple_of(n * BLK, BLK)
    sp = pl.multiple_of(jnp.maximum(n - 1, 0) * BLK, BLK)
    q = q_ref[:, pl.ds(s0, BLK), :].reshape(GROWS, HEAD_DIM)
    kc = k_ref[0, pl.ds(s0, BLK), :]
    kp = k_ref[0, pl.ds(sp, BLK), :]
    sc = jnp.where(ki <= qi, _dot(q, kc, NT) * SCALE - slope * dist, NEG_INF)
    first = jnp.where(n >= 1, 0, BLK)
    sv = jnp.where(ki > qi + first, _dot(q, kp, NT) * SCALE - slope * (dist + BLK), NEG_INF)
    m = jnp.maximum(jnp.maximum(jnp.max(sc, axis=1, keepdims=True), jnp.max(sv, axis=1, keepdims=True)), sink)
    pc = jnp.exp(sc - m)
    pp = jnp.exp(sv - m)
    ps = jnp.exp(sink - m)
    z = jnp.sum(pc, axis=1, keepdims=True) + jnp.sum(pp, axis=1, keepdims=True) + ps
    return s0, sp, q, kc, kp, pc, pp, ps, z


def _attn_specs(t):
    qs = pl.BlockSpec((GROUP, t, HEAD_DIM), lambda g: (g, 0, 0))
    ks = pl.BlockSpec((1, t, HEAD_DIM), lambda g: (g, 0, 0))
    ss = pl.BlockSpec(memory_space=pltpu.SMEM)
    return qs, ks, ss


def _attn_fwd(q, k, v, ss, name):
    t = q.shape[1]
    qs, ks, sspec = _attn_specs(t)

    def body(q_ref, k_ref, v_ref, ss_ref, o_ref):
        g = pl.program_id(0)

        def blk(n, carry):
            s0, sp, _, _, _, pc, pp, _, z = _attn_probs(q_ref, k_ref, ss_ref, g, n)
            o = _dot(pc, v_ref[0, pl.ds(s0, BLK), :], NN) + _dot(pp, v_ref[0, pl.ds(sp, BLK), :], NN)
            o_ref[:, pl.ds(s0, BLK), :] = (o / z).astype(BF16).reshape(GROUP, BLK, HEAD_DIM)
            return carry

        lax.fori_loop(0, t // BLK, blk, 0)

    return pl.pallas_call(
        body, name=name, out_shape=jax.ShapeDtypeStruct((N_Q, t, HEAD_DIM), BF16), grid=(N_KV,),
        in_specs=[qs, ks, ks, sspec], out_specs=qs, compiler_params=_params(("parallel",)),
    )(q, k, v, ss)


def _attn_bwd(q, k, v, do, ss, name):
    t = q.shape[1]
    qs, ks, sspec = _attn_specs(t)

    def body(q_ref, k_ref, v_ref, do_ref, ss_ref, dq_ref, dk_ref, dv_ref, ds_ref):
        g = pl.program_id(0)
        dk_ref[...] = jnp.zeros_like(dk_ref)
        dv_ref[...] = jnp.zeros_like(dv_ref)

        def blk(n, dsink):
            s0, sp, q, kc, kp, pc, pp, ps, z = _attn_probs(q_ref, k_ref, ss_ref, g, n)
            rz = 1.0 / z
            pc = pc * rz
            pp = pp * rz
            do_b = do_ref[:, pl.ds(s0, BLK), :].reshape(GROWS, HEAD_DIM)
            dpc = _dot(do_b, v_ref[0, pl.ds(s0, BLK), :], NT)
            dpp = _dot(do_b, v_ref[0, pl.ds(sp, BLK), :], NT)
            delta = jnp.sum(pc * dpc, axis=1, keepdims=True) + jnp.sum(pp * dpp, axis=1, keepdims=True)
            dsc = pc * (dpc - delta)
            dsp = pp * (dpp - delta)
            dq = (_dot(dsc, kc, NN) + _dot(dsp, kp, NN)) * SCALE
            dq_ref[:, pl.ds(s0, BLK), :] = dq.astype(BF16).reshape(GROUP, BLK, HEAD_DIM)
            dk_ref[0, pl.ds(s0, BLK), :] += _dot(dsc, q, TN) * SCALE
            dk_ref[0, pl.ds(sp, BLK), :] += _dot(dsp, q, TN) * SCALE
            dv_ref[0, pl.ds(s0, BLK), :] += _dot(pc, do_b, TN)
            dv_ref[0, pl.ds(sp, BLK), :] += _dot(pp, do_b, TN)
            return dsink - ps * rz * delta

        dsink = lax.fori_loop(0, t // BLK, blk, jnp.zeros((GROWS, 1), F32))
        for i in range(GROUP):
            ds_ref[i] = jnp.full(ds_ref.shape[1:], jnp.sum(dsink[i * BLK:(i + 1) * BLK]), F32)

    kv = jax.ShapeDtypeStruct((N_KV, t, HEAD_DIM), F32)
    return pl.pallas_call(
        body, name=name,
        out_shape=(jax.ShapeDtypeStruct((N_Q, t, HEAD_DIM), BF16), kv, kv, jax.ShapeDtypeStruct((N_Q, 8, LANE), F32)),
        grid=(N_KV,), in_specs=[qs, ks, ks, qs, sspec],
        out_specs=(qs, ks, ks, pl.BlockSpec((GROUP, 8, LANE), lambda g: (g, 0, 0))),
        compiler_params=_params(("parallel",)),
    )(q, k, v, do, ss)


def _heads(x2d, n):
    t = x2d.shape[0]
    return x2d.reshape(t, n, HEAD_DIM).transpose(1, 0, 2)


def _unheads(x3d):
    n, t, _ = x3d.shape
    return x3d.transpose(1, 0, 2).reshape(t, n * HEAD_DIM)


SMALL_ELEMS = 256 * 1024
TILE_ELEMS = 640 * 1024


def _row_tile(r, c):
    if r * c <= SMALL_ELEMS:
        return r
    return _pick(r, [t for t in (512, 256, 128, 64, 32, 16, 8) if t * c <= TILE_ELEMS])


def _adamw_update(w, gv, m, v):
    nm = ADAM_B1 * m + (1.0 - ADAM_B1) * gv
    nv = ADAM_B2 * v + (1.0 - ADAM_B2) * (gv * gv)
    m_hat = nm / (1.0 - ADAM_B1 ** ADAM_STEP)
    v_hat = nv / (1.0 - ADAM_B2 ** ADAM_STEP)
    return -ADAM_LR * (m_hat / (jnp.sqrt(v_hat) + ADAM_EPS) + ADAM_WD * w), nm, nv


def _adamw(w, g, m, v, name):
    r, c = w.shape
    tr = _row_tile(r, c)
    spec = pl.BlockSpec((tr, c), lambda i: (i, 0))

    def body(w_ref, g_ref, m_ref, v_ref, d_ref, nm_ref, nv_ref):
        d_ref[...], nm_ref[...], nv_ref[...] = _adamw_update(w_ref[...], g_ref[...], m_ref[...], v_ref[...])

    shp = jax.ShapeDtypeStruct((r, c), F32)
    return pl.pallas_call(
        body, name=name, out_shape=(shp, shp, shp), grid=(r // tr,), in_specs=[spec] * 4, out_specs=(spec,) * 3,
        compiler_params=_params(("parallel",)),
    )(w, g, m, v)


def _adamw_layer(w, m, v, g, l, prev, name):
    r, c = g.shape
    tr = _row_tile(r, c)
    layer = pl.BlockSpec((tr, c), lambda i: (l * (r // tr) + i, 0))

    def body(w_ref, m_ref, v_ref, g_ref, *rest):
        go_ref, d_ref, nm_ref, nv_ref, token = rest[-5:]
        gv = g_ref[...]
        go_ref[...] = gv
        token[...] = jnp.zeros_like(token)
        d_ref[...], nm_ref[...], nv_ref[...] = _adamw_update(w_ref[...], gv, m_ref[...], v_ref[...])

    carried = list(prev[:4]) if prev is not None else []
    shp = jax.ShapeDtypeStruct(w.shape, F32)
    return pl.pallas_call(
        body, name=name, out_shape=(shp,) * 4 + (jax.ShapeDtypeStruct((8, LANE), F32),), grid=(r // tr,),
        in_specs=[layer] * 3 + [pl.BlockSpec((tr, c), lambda i: (i, 0))] + [pl.BlockSpec(memory_space=pl.ANY)] * len(carried),
        out_specs=(layer,) * 4 + (pl.BlockSpec((8, LANE), lambda i: (0, 0)),),
        input_output_aliases={4 + i: i for i in range(len(carried))}, compiler_params=_params(("arbitrary",)),
    )(w, m, v, g, *carried)


def _sum_leading(x, name):
    n, r, c = x.shape
    tr = _row_tile(r, c)

    def body(x_ref, o_ref):
        acc = x_ref[0]
        for i in range(1, n):
            acc = acc + x_ref[i]
        o_ref[...] = acc

    return pl.pallas_call(
        body, name=name, out_shape=jax.ShapeDtypeStruct((r, c), F32), grid=(r // tr,),
        in_specs=[pl.BlockSpec((n, tr, c), lambda i: (0, i, 0))], out_specs=pl.BlockSpec((tr, c), lambda i: (i, 0)),
        compiler_params=_params(("parallel",)),
    )(x)


def _sum_own_plus(p, sel, recv, name, out_dtype):
    _, r, c = p.shape
    n = recv.shape[0]
    tr = _pick(r, (512, 448, 256, 128, 64, 16))

    def body(sel_ref, p_ref, r_ref, o_ref):
        acc = p_ref[0].astype(F32)
        for i in range(n):
            acc = acc + r_ref[i].astype(F32)
        o_ref[...] = acc.astype(out_dtype)

    grid_spec = pltpu.PrefetchScalarGridSpec(
        num_scalar_prefetch=1, grid=(r // tr,),
        in_specs=[pl.BlockSpec((1, tr, c), lambda i, s: (s[0], i, 0)), pl.BlockSpec((n, tr, c), lambda i, s: (0, i, 0))],
        out_specs=pl.BlockSpec((tr, c), lambda i, s: (i, 0)))
    return pl.pallas_call(
        body, name=name, out_shape=jax.ShapeDtypeStruct((r, c), out_dtype), grid_spec=grid_spec,
        compiler_params=_params(("parallel",)),
    )(sel, p, recv)


def _coords():
    return lax.axis_index("x"), lax.axis_index("y"), lax.axis_index("c")


def _allgather8(x2, name, space):
    _, m, n = x2.shape

    def body(x_ref, out_ref, send_sems, recv_sems, local_sem):
        x, y, c = _coords()
        me, sibling = (x, y, c), (x, y, 1 - c)
        chips = [(1 - x, y), (x, 1 - y), (1 - x, 1 - y)]
        mine_src = x_ref.at[c]

        def rows(px, py, pc):
            return out_ref.at[4 * px + 2 * py + pc]

        def copy(k, block, to, src=None):
            return pltpu.make_async_remote_copy(
                src_ref=rows(*block) if src is None else src, dst_ref=rows(*block),
                send_sem=send_sems.at[k], recv_sem=recv_sems.at[k], device_id=to, device_id_type=MESH)

        mine = pltpu.make_async_copy(mine_src, rows(*me), local_sem)
        mine.start()
        first = [copy(0, me, sibling, src=mine_src)]
        first += [copy(1 + j, me, (*chip, c), src=mine_src) for j, chip in enumerate(chips)]
        for cp in first:
            cp.start()
        passed = [copy(4 + j, (*chip, c), sibling) for j, chip in enumerate(chips)]
        for j, chip in enumerate(chips):
            copy(1 + j, (*chip, c), me).wait_recv()
            passed[j].start()
        copy(0, sibling, me).wait_recv()
        for j, chip in enumerate(chips):
            copy(4 + j, (*chip, 1 - c), me).wait_recv()
        for cp in first + passed:
            cp.wait_send()
        mine.wait()

    return pl.pallas_call(
        body, name=name, out_shape=jax.ShapeDtypeStruct((8, m, n), x2.dtype),
        in_specs=[pl.BlockSpec(memory_space=space)], out_specs=pl.BlockSpec(memory_space=space),
        scratch_shapes=[pltpu.SemaphoreType.DMA((7,)), pltpu.SemaphoreType.DMA((7,)), pltpu.SemaphoreType.DMA],
        compiler_params=pltpu.CompilerParams(vmem_limit_bytes=VMEM_LIMIT),
    )(x2)


N_REG = len(ROW_REGIONS) + 1


def _chip_window(ref, lead, r, j):
    view = ref if lead is None else ref.at[lead]
    if r < len(ROW_REGIONS):
        off, rows = ROW_REGIONS[r]
        return view.at[pl.ds(pl.multiple_of(off + j * rows, 16), rows), :]
    return view.at[pl.ds(R_OUT, OUT_ROWS), pl.ds(pl.multiple_of(j * OUT_COLS, LANE), OUT_COLS)]


HBM_SPEC = pl.BlockSpec(memory_space=pltpu.HBM)
SEM_SPEC = pl.BlockSpec(memory_space=pltpu.SEMAPHORE)


def _half_window(ref, r, j, h):
    if r < len(ROW_REGIONS):
        off, rows = ROW_REGIONS[r]
        return ref.at[pl.ds(pl.multiple_of(off + j * rows + h * (rows // 2), 16), rows // 2), :]
    half = OUT_ROWS // 2
    return ref.at[pl.ds(pl.multiple_of(R_OUT + h * half, 16), half),
                  pl.ds(pl.multiple_of(j * OUT_COLS, LANE), OUT_COLS)]


def _other_chips():
    x, y, _ = _coords()
    return [(1 - x, y), (x, 1 - y), (1 - x, 1 - y)]


def _ici_copies(srcs, arena_ref, send_sems, recv_sems, regions):
    x, y, c = _coords()
    sends, arrivals = [], []
    for k, (cx, cy) in enumerate(_other_chips()):
        for r in regions:
            def remote(src, j):
                return pltpu.make_async_remote_copy(
                    src_ref=src, dst_ref=_half_window(arena_ref, r, j, c), send_sem=send_sems.at[3 * r + k],
                    recv_sem=recv_sems.at[3 * r + k], device_id=(cx, cy, c), device_id_type=MESH)
            rows = srcs[r].shape[0] // 2
            sends.append(remote(srcs[r].at[pl.ds(pl.multiple_of(c * rows, 16), rows), :], 2 * x + y))
            arrivals.append(remote(_half_window(arena_ref, r, 2 * cx + cy, c), 2 * cx + cy))
    return sends, arrivals


def _sibling_copies(srcs, arena_ref, send_sems, recv_sems, regions):
    x, y, c = _coords()
    sends, arrivals = [], []

    def remote(win, r, k, src=None):
        return pltpu.make_async_remote_copy(
            src_ref=win if src is None else src, dst_ref=win, send_sem=send_sems.at[r, k],
            recv_sem=recv_sems.at[r, k], device_id=(x, y, 1 - c), device_id_type=MESH)

    for k, (cx, cy) in enumerate(_other_chips()):
        for r in regions:
            sends.append(remote(_half_window(arena_ref, r, 2 * cx + cy, c), r, k))
            arrivals.append(remote(_half_window(arena_ref, r, 2 * cx + cy, 1 - c), r, k))
    for r in regions:
        own = _chip_window(arena_ref, None, r, 2 * x + y)
        sends.append(remote(own, r, 3, src=srcs[r]))
        arrivals.append(remote(own, r, 3))
    return sends, arrivals


ICI_SEMS = pltpu.SemaphoreType.DMA((3 * N_REG,))
SIBLING_SEMS = pltpu.SemaphoreType.DMA((N_REG, 4))
ARENA_SHAPE = (ARENA_ROWS, ARENA_W)
ALL_REGIONS = tuple(range(N_REG))
IN_REGION = (3,)
REST_REGIONS = (0, 1, 2, 4, 5)


def _gather_layer(shards, name, regions=ALL_REGIONS):
    def body(*refs):
        srcs, arena_ref = refs[:N_REG], refs[N_REG]
        ici_send, ici_recv, sib_send, sib_recv = refs[N_REG + 1:]
        sends, arrivals = _ici_copies(srcs, arena_ref, ici_send, ici_recv, regions)
        passes, landings = _sibling_copies(srcs, arena_ref, sib_send, sib_recv, regions)
        for cp in sends + passes[len(arrivals):]:
            cp.start()
        for arrival, onward in zip(arrivals, passes):
            arrival.wait_recv()
            onward.start()
        for cp in landings:
            cp.wait_recv()
        for cp in sends + passes:
            cp.wait_send()

    return pl.pallas_call(
        body, name=name, out_shape=jax.ShapeDtypeStruct(ARENA_SHAPE, BF16),
        in_specs=[pl.BlockSpec(memory_space=pl.ANY)] * N_REG, out_specs=pl.BlockSpec(memory_space=pl.ANY),
        scratch_shapes=[ICI_SEMS, ICI_SEMS, SIBLING_SEMS, SIBLING_SEMS],
    )(*shards)


def _gather_start(shards, after, name, regions=ALL_REGIONS):
    def body(*refs):
        srcs, arena_ref = refs[:N_REG], refs[N_REG]
        send_sems, recv_sems = refs[N_REG + 2], refs[N_REG + 3]
        token = refs[-1]
        for cp in _ici_copies(srcs, arena_ref, send_sems, recv_sems, regions)[0]:
            cp.start()
        token[...] = jnp.zeros_like(token)

    hbm = lambda a: pltpu.with_memory_space_constraint(a, pltpu.HBM)
    outs = pl.pallas_call(
        body, name=name,
        out_shape=(ICI_SEMS, ICI_SEMS, *[pltpu.HBM(s.shape, s.dtype) for s in shards],
                   pltpu.HBM(ARENA_SHAPE, BF16), pltpu.HBM(after.shape, after.dtype),
                   jax.ShapeDtypeStruct((8, LANE), F32)),
        in_specs=[HBM_SPEC] * (N_REG + 2),
        out_specs=(SEM_SPEC, SEM_SPEC, *[HBM_SPEC] * (N_REG + 2), pl.BlockSpec(memory_space=pltpu.VMEM)),
        input_output_aliases={i: 2 + i for i in range(N_REG + 2)},
        compiler_params=pltpu.CompilerParams(has_side_effects=pltpu.SideEffectType.DATAFLOW_SIDE_EFFECTING),
    )(*[hbm(s) for s in shards], hbm(lax.empty(ARENA_SHAPE, BF16)), hbm(after))
    return outs[0], outs[1], outs[2:2 + N_REG], outs[2 + N_REG], outs[-1], outs[3 + N_REG]


def _gather_wait(send_sems, recv_sems, shards, arena, after, name, regions=ALL_REGIONS):
    def body(*refs):
        srcs, arena_ref = refs[:N_REG], refs[N_REG]
        sends, arrivals = _ici_copies(srcs, arena_ref, refs[N_REG + 1], refs[N_REG + 2], regions)
        for cp in sends:
            cp.wait_send()
        for cp in arrivals:
            cp.wait_recv()

    outs = pl.pallas_call(
        body, name=name,
        out_shape=(*[pltpu.HBM(s.shape, s.dtype) for s in shards], pltpu.HBM(ARENA_SHAPE, BF16)),
        in_specs=[HBM_SPEC] * (N_REG + 1) + [SEM_SPEC, SEM_SPEC, pl.BlockSpec(memory_space=pl.ANY)],
        out_specs=(HBM_SPEC,) * (N_REG + 1), input_output_aliases={i: i for i in range(N_REG + 1)},
        compiler_params=pltpu.CompilerParams(has_side_effects=pltpu.SideEffectType.DATAFLOW_SIDE_EFFECTING),
    )(*shards, arena, send_sems, recv_sems, after)
    return outs[:N_REG], outs[N_REG]


def _gather_finish(shards, arena, name, regions=ALL_REGIONS):
    def body(*refs):
        srcs, arena_ref = refs[:N_REG], refs[N_REG + 1]
        sends, arrivals = _sibling_copies(srcs, arena_ref, refs[N_REG + 2], refs[N_REG + 3], regions)
        for cp in sends:
            cp.start()
        for cp in arrivals:
            cp.wait_recv()
        for cp in sends:
            cp.wait_send()

    return pl.pallas_call(
        body, name=name, out_shape=jax.ShapeDtypeStruct(ARENA_SHAPE, BF16),
        in_specs=[pl.BlockSpec(memory_space=pl.ANY)] * (N_REG + 1), out_specs=pl.BlockSpec(memory_space=pl.ANY),
        scratch_shapes=[SIBLING_SEMS, SIBLING_SEMS], input_output_aliases={N_REG: 0},
    )(*shards, arena)


HALF_PIECE_OFF = tuple(o // 2 for o in PIECE_OFF)
HALF_PIECE_ROWS = PIECE_ROWS // 2
HALF_OUT_ROWS = OUT_ROWS // 2
SWAP_SEMS = pltpu.SemaphoreType.DMA((4 * N_REG,))
SCATTER_SEMS = pltpu.SemaphoreType.DMA((6,))


def _packed_shapes(slots, dtype):
    return (jax.ShapeDtypeStruct((slots, HALF_PIECE_ROWS, ARENA_W), dtype),
            jax.ShapeDtypeStruct((slots, HALF_OUT_ROWS, OUT_COLS), dtype))


def _swap_halves(ga, name):
    def body(g_ref, main_ref, outp_ref, send_sems, recv_sems):
        x, y, c = _coords()
        cps = []
        for j in range(4):
            for r in range(N_REG):
                if r < len(ROW_REGIONS):
                    dst = main_ref.at[j, pl.ds(HALF_PIECE_OFF[r], ROW_REGIONS[r][1] // 2), :]
                else:
                    dst = outp_ref.at[j]
                cps.append(pltpu.make_async_remote_copy(
                    src_ref=_half_window(g_ref, r, j, 1 - c), dst_ref=dst, send_sem=send_sems.at[j * N_REG + r],
                    recv_sem=recv_sems.at[j * N_REG + r], device_id=(x, y, 1 - c), device_id_type=MESH))
        for cp in cps:
            cp.start()
        for cp in cps:
            cp.wait()

    return pl.pallas_call(
        body, name=name, out_shape=_packed_shapes(4, ga.dtype),
        in_specs=[pl.BlockSpec(memory_space=pl.ANY)], out_specs=(pl.BlockSpec(memory_space=pl.ANY),) * 2,
        scratch_shapes=[SWAP_SEMS, SWAP_SEMS],
    )(ga)


def _own_halves(ga, cc):
    mains = [jnp.concatenate([lax.dynamic_slice(ga, (off + j * rows + cc * (rows // 2), 0), (rows // 2, ARENA_W))
                              for off, rows in ROW_REGIONS]) for j in range(4)]
    outs = [lax.dynamic_slice(ga, (R_OUT + cc * HALF_OUT_ROWS, j * OUT_COLS), (HALF_OUT_ROWS, OUT_COLS))
            for j in range(4)]
    return jnp.stack(mains), jnp.stack(outs)


def _scatter_copies(main_ref, outp_ref, rmain_ref, routp_ref, send_sems, recv_sems):
    _, _, c = _coords()
    cps = []
    for k, (cx, cy) in enumerate(_other_chips()):
        for i, (src, dst) in enumerate(((main_ref, rmain_ref), (outp_ref, routp_ref))):
            cps.append(pltpu.make_async_remote_copy(
                src_ref=src.at[2 * cx + cy], dst_ref=dst.at[k], send_sem=send_sems.at[2 * k + i],
                recv_sem=recv_sems.at[2 * k + i], device_id=(cx, cy, c), device_id_type=MESH))
    return cps


def _scatter_start(main, outp, name):
    def body(main_ref, outp_ref, rmain_ref, routp_ref, send_sems, recv_sems, *rest):
        for cp in _scatter_copies(main_ref, outp_ref, rmain_ref, routp_ref, send_sems, recv_sems):
            cp.start()
        rest[-1][...] = jnp.zeros_like(rest[-1])

    hbm = lambda a: pltpu.with_memory_space_constraint(a, pltpu.HBM)
    land = [lax.empty(s.shape, s.dtype) for s in _packed_shapes(3, main.dtype)]
    bufs = [main, outp, *land]
    outs = pl.pallas_call(
        body, name=name,
        out_shape=(SCATTER_SEMS, SCATTER_SEMS, *[pltpu.HBM(b.shape, b.dtype) for b in bufs],
                   jax.ShapeDtypeStruct((8, LANE), F32)),
        in_specs=[HBM_SPEC] * 4, out_specs=(SEM_SPEC, SEM_SPEC, *[HBM_SPEC] * 4, pl.BlockSpec(memory_space=pltpu.VMEM)),
        input_output_aliases={i: 2 + i for i in range(4)},
        compiler_params=pltpu.CompilerParams(has_side_effects=pltpu.SideEffectType.DATAFLOW_SIDE_EFFECTING),
    )(*[hbm(b) for b in bufs])
    return outs[0], outs[1], outs[2:6], outs[6]


def _scatter_wait(send_sems, recv_sems, bufs, after, name):
    def body(main_ref, outp_ref, rmain_ref, routp_ref, send_sems, recv_sems, *rest):
        for cp in _scatter_copies(main_ref, outp_ref, rmain_ref, routp_ref, send_sems, recv_sems):
            cp.wait_send()
            cp.wait_recv()

    return pl.pallas_call(
        body, name=name, out_shape=tuple(pltpu.HBM(b.shape, b.dtype) for b in bufs),
        in_specs=[HBM_SPEC] * 4 + [SEM_SPEC, SEM_SPEC, pl.BlockSpec(memory_space=pl.ANY)],
        out_specs=(HBM_SPEC,) * 4, input_output_aliases={i: i for i in range(4)},
        compiler_params=pltpu.CompilerParams(has_side_effects=pltpu.SideEffectType.DATAFLOW_SIDE_EFFECTING),
    )(*bufs, send_sems, recv_sems, after)


def _swap_many(arrs, name):
    n = len(arrs)

    def body(*refs):
        x, y, c = _coords()
        send_sems, recv_sems = refs[2 * n], refs[2 * n + 1]
        cps = [pltpu.make_async_remote_copy(
            src_ref=refs[i], dst_ref=refs[n + i], send_sem=send_sems.at[i], recv_sem=recv_sems.at[i],
            device_id=(x, y, 1 - c), device_id_type=MESH) for i in range(n)]
        for cp in cps:
            cp.start()
        for cp in cps:
            cp.wait()

    return pl.pallas_call(
        body, name=name, out_shape=tuple(jax.ShapeDtypeStruct(a.shape, a.dtype) for a in arrs),
        in_specs=[pl.BlockSpec(memory_space=pl.ANY)] * n, out_specs=(pl.BlockSpec(memory_space=pl.ANY),) * n,
        scratch_shapes=[pltpu.SemaphoreType.DMA((n,)), pltpu.SemaphoreType.DMA((n,))],
    )(*arrs)


def _join_halves(mine, theirs, cc):
    return jnp.where(cc == 0, jnp.concatenate([mine, theirs]), jnp.concatenate([theirs, mine]))


def _reduced_layer(red, sib, cc):
    parts = []
    for (_, rows), off in zip(ROW_REGIONS, HALF_PIECE_OFF):
        parts.append(_join_halves(red[0][off:off + rows // 2], sib[0][off:off + rows // 2], cc))
    return jnp.concatenate(parts), _join_halves(red[1], sib[1], cc)


OUT_NAMES = ("w_a_out", "w_b_out", "w_c_out", "w_d_out")


def _arena_shards(w):
    t = lambda a: a.astype(BF16).transpose(0, 2, 1)
    return (w["w_ffn_down"].astype(BF16), t(w["w_ffn_gate"]), t(w["w_ffn_up"]), t(w["w_in"]), w["w_o"].astype(BF16),
            jnp.concatenate([w[n].astype(BF16) for n in OUT_NAMES], axis=1))


def _shard_grads(main, outp):
    t = lambda r: main[PIECE_OFF[r]:PIECE_OFF[r] + ROW_REGIONS[r][1]]
    g = dict(w_ffn_down=t(0), w_ffn_gate=t(1).T, w_ffn_up=t(2).T, w_in=t(3).T, w_o=t(4))
    for i, n in enumerate(OUT_NAMES):
        g[n] = outp[i * BW:(i + 1) * BW]
    return g


def _gather_taps(p, name):
    mine = jnp.concatenate([p[n] for n in CONV_NAMES], axis=1).reshape(DEPTH * N_TAPS, LANE)
    rows = -(-mine.shape[0] // 8) * 8
    mine = jnp.concatenate([mine, jnp.zeros((rows - mine.shape[0], LANE), F32)])
    g = _allgather8(jnp.stack([mine, mine]), name, pltpu.VMEM)[0::2, :DEPTH * N_TAPS]
    full = g.reshape(4, DEPTH, N_TAPS, LANE).transpose(1, 2, 0, 3).reshape(DEPTH, N_TAPS, BW)
    return dict(conv_a_w=full[:, :CONV_A], conv_b_w=full[:, CONV_A:CONV_A + CONV_B], conv_d_w=full[:, CONV_A + CONV_B:])


def _flat_pack(arrs):
    flat = jnp.concatenate([a.reshape(-1).astype(F32) for a in arrs])
    rows = -(-flat.shape[0] // (8 * LANE)) * 8
    return jnp.concatenate([flat, jnp.zeros((rows * LANE - flat.shape[0],), F32)]).reshape(rows, LANE)


def _flat_unpack(packed, shapes):
    flat, out, off = packed.reshape(-1), [], 0
    for s in shapes:
        cnt = int(np.prod(s))
        out.append(flat[off:off + cnt].reshape(s))
        off += cnt
    return out


def _blockdiag_chunks(w):
    w4 = w.reshape(4, 2, 64, 64)
    z = jnp.zeros((4, 2, 64, 2, 64), F32)
    z = z.at[:, 0, :, 0, :].set(w4[:, 0]).at[:, 1, :, 1, :].set(w4[:, 1])
    return z.reshape(4, LANE, LANE)


def _blockdiag_extract(d):
    d5 = d.reshape(4, 2, 64, 2, 64)
    return jnp.stack([d5[:, 0, :, 0, :], d5[:, 1, :, 1, :]], axis=1).reshape(8, 64, 64)


SLOPES = np.asarray([2.0 ** (-8.0 * (i + 1) / N_Q) for i in range(N_Q)], np.float32)


def _layer_consts(p, fw, l):
    row = lambda a: a[l].reshape(1, -1)
    return dict(
        g1=row(p["norm1_g"]), g2=row(p["norm2_g"]), wA=fw["conv_a_w"][l], bA=row(p["conv_a_b"]),
        wx=_blockdiag_chunks(p["lru_wx"][l]), bx=row(p["lru_bx"]), wa=_blockdiag_chunks(p["lru_wa"][l]),
        ba=row(p["lru_ba"]), lam=row(p["lru_lambda"]), wB=fw["conv_b_w"][l],
        ss=jnp.stack([p["sinks"][l], jnp.asarray(SLOPES)]), wD=fw["conv_d_w"][l], bD=row(p["conv_d_b"]),
        lg=row(p["ln_d_g"]), lb=row(p["ln_d_b"]))


def _layer_fwd(x, c, fw, l, rest_of_weights=None):
    t = f"l{l}_"
    xn = _rms_fwd(x, c["g1"], t + "rms1")
    wt = lambda off, rows: Win(fw["arena"][l], None, off, rows)
    proj = _mm(xn, Win(fw["arena_in"][l], None, R_IN, IN_W), "nt", t + "proj")
    ya = _a_fwd(proj, c["wA"], c["bA"], c["wx"], c["bx"], c["wa"], c["ba"], c["lam"], t + "a_fwd")
    yb = _b_fwd(proj, c["wB"], t + "b_fwd")
    q3 = _heads(proj[:, OFF_Q:OFF_K], N_Q)
    k3 = _heads(proj[:, OFF_K:OFF_V], N_KV)
    v3 = _heads(proj[:, OFF_V:OFF_V + N_KV * HEAD_DIM], N_KV)
    yc = _unheads(_attn_fwd(q3, k3, v3, c["ss"], t + "attn_fwd"))
    cd = _d_conv_fwd(proj, c["wD"], c["bD"], t + "d_conv_fwd")
    yd = _ln_silu_fwd(cd, c["lg"], c["lb"], t + "d_ln_fwd")
    ys = (ya, yb, yc, yd)
    if fw["arena"][l] is None:
        fw["arena"][l] = rest_of_weights(yd)
    big_y = tuple(_mm(y, wt(R_OUT + i * BW, BW), "nn", t + f"out{i}") for i, y in enumerate(ys))
    merged = _merge_fwd(proj, big_y, t + "merge_fwd")
    hres = _mm(merged, wt(R_O, D_MODEL), "nn", t + "wo", add=x)
    hn = _rms_fwd(hres, c["g2"], t + "rms2")
    gg = _mm(hn, wt(R_GATE, D_FF), "nt", t + "ffn_gate")
    uu = _mm(hn, wt(R_UP, D_FF), "nt", t + "ffn_up")
    act = _swiglu_fwd(gg, uu, t + "swiglu_fwd")
    xout = _mm(act, wt(R_DOWN, D_FF), "nn", t + "ffn_down", add=hres)
    saved = dict(x=x, xn=xn, proj=proj, ys=ys, q3=q3, k3=k3, v3=v3, cd=cd, big_y=big_y, merged=merged, hres=hres,
                 hn=hn, gg=gg, uu=uu, act=act)
    return xout, saved


def _layer_bwd(dxout, s, c, fw, l, ga, weight_grads_done=None):
    t = f"l{l}_"
    gs = {}
    wt = lambda off, rows: Win(fw["arena"][l], None, off, rows)
    gt = lambda off, rows: Win(ga, None, off, rows)
    dact = _mm(dxout, wt(R_DOWN, D_FF), "nt", t + "d_act")
    ga = _mm(s["act"], dxout, "tn", t + "dw_down", out=gt(R_DOWN, D_FF))
    dgg, duu = _swiglu_bwd(s["gg"], s["uu"], dact, t + "swiglu_bwd")
    ga = _mm(dgg, s["hn"], "tn", t + "dw_gate", out=gt(R_GATE, D_FF))
    ga = _mm(duu, s["hn"], "tn", t + "dw_up", out=gt(R_UP, D_FF))
    dhn = _mm(dgg, wt(R_GATE, D_FF), "nn", t + "d_hn_g")
    dhn = _mm(duu, wt(R_UP, D_FF), "nn", t + "d_hn_u", add=dhn)
    dhres, gs["norm2_g"] = _rms_bwd(s["hres"], c["g2"], dhn, dxout, t + "rms2_bwd")
    dmerged = _mm(dhres, wt(R_O, D_MODEL), "nt", t + "d_merged")
    ga = _mm(s["merged"], dhres, "tn", t + "dw_o", out=gt(R_O, D_MODEL))
    dbig_y, dgl = _merge_bwd(s["proj"], s["big_y"], dmerged, t + "merge_bwd")
    dys = []
    for i in range(4):
        ga = _mm(s["ys"][i], dbig_y[i], "tn", t + f"dw_out{i}", out=gt(R_OUT + i * BW, BW))
        dys.append(_mm(dbig_y[i], wt(R_OUT + i * BW, BW), "nt", t + f"d_y{i}"))
    proj = s["proj"]
    (dax, dag, gs["conv_a_w"], gs["conv_a_b"], dwx, gs["lru_bx"], dwa, gs["lru_ba"], gs["lru_lambda"]) = _a_bwd(
        proj, dys[0], c["wA"], c["bA"], c["wx"], c["bx"], c["wa"], c["ba"], c["lam"], t + "a_bwd")
    gs["lru_wx"] = _blockdiag_extract(dwx)
    gs["lru_wa"] = _blockdiag_extract(dwa)
    dbv, dbc, dbb, gs["conv_b_w"] = _b_bwd(proj, dys[1], c["wB"], t + "b_bwd")
    dq3, dk3, dv3, dsink = _attn_bwd(s["q3"], s["k3"], s["v3"], _heads(dys[2], N_Q), c["ss"], t + "attn_bwd")
    gs["sinks"] = dsink[:, 0, 0]
    dcd, gs["ln_d_g"], gs["ln_d_b"] = _ln_silu_bwd(s["cd"], c["lg"], c["lb"], dys[3], t + "d_ln_bwd")
    dd1, dd2, gs["conv_d_w"], gs["conv_d_b"] = _d_conv_bwd(proj, dcd, c["wD"], t + "d_conv_bwd")
    dproj = jnp.concatenate(
        [dax, dag, dbv, dbc, dbb, _unheads(dq3), _unheads(dk3).astype(BF16), _unheads(dv3).astype(BF16), dd1, dd2,
         *dgl], axis=1)
    ga = _mm(dproj, s["xn"], "tn", t + "dw_in", out=gt(R_IN, IN_W))
    token = weight_grads_done(ga) if weight_grads_done is not None else None
    dxn = _mm(dproj, Win(fw["arena_in"][l], None, R_IN, IN_W), "nn", t + "d_xn", after=token)
    dx, gs["norm1_g"] = _rms_bwd(s["x"], c["g1"], dxn, dhres, t + "rms1_bwd")
    return dx, ga, gs


def kernel(x, norm1_g, w_in, conv_a_w, conv_a_b, lru_wx, lru_bx, lru_wa, lru_ba, lru_lambda, w_a_out, conv_b_w, w_b_out, sinks, w_c_out, conv_d_w, conv_d_b, ln_d_g, ln_d_b, w_d_out, w_o, norm2_g, w_ffn_gate, w_ffn_up, w_ffn_down, final_g, loss_target, m_norm1_g, m_w_in, m_conv_a_w, m_conv_a_b, m_lru_wx, m_lru_bx, m_lru_wa, m_lru_ba, m_lru_lambda, m_w_a_out, m_conv_b_w, m_w_b_out, m_sinks, m_w_c_out, m_conv_d_w, m_conv_d_b, m_ln_d_g, m_ln_d_b, m_w_d_out, m_w_o, m_norm2_g, m_w_ffn_gate, m_w_ffn_up, m_w_ffn_down, m_final_g, v_norm1_g, v_w_in, v_conv_a_w, v_conv_a_b, v_lru_wx, v_lru_bx, v_lru_wa, v_lru_ba, v_lru_lambda, v_w_a_out, v_conv_b_w, v_w_b_out, v_sinks, v_w_c_out, v_conv_d_w, v_conv_d_b, v_ln_d_g, v_ln_d_b, v_w_d_out, v_w_o, v_norm2_g, v_w_ffn_gate, v_w_ffn_up, v_w_ffn_down, v_final_g):
    given = dict(locals())
    p = {n: given[n] for n in NAMES}
    mom = {n: given["m_" + n] for n in NAMES}
    var = {n: given["v_" + n] for n in NAMES}
    cx, cy, cc = _coords()
    chip = 2 * cx + cy

    shards = _arena_shards(p)
    fw = _gather_taps(p, "gather_taps")
    shards0, shards1 = [s[0] for s in shards], [s[1] for s in shards]
    flight0 = _gather_start(shards0, _gather_layer(shards0, "gather_l0_in", IN_REGION), "gather_l0_rest_start",
                            REST_REGIONS)
    fw["arena_in"] = [flight0[5], None]
    fw["arena"] = [None, None]
    consts = [_layer_consts(p, fw, l) for l in range(DEPTH)]
    consts[0]["g1"] = consts[0]["g1"] + flight0[4][0:1, 0:1]
    flight1 = []

    def rest_of_layer0(after):
        sh, landing = _gather_wait(*flight0[:4], after, "gather_l0_rest_wait", REST_REGIONS)
        arena = _gather_finish(sh, landing, "gather_l0_rest_finish", REST_REGIONS)
        flight1.extend(_gather_start(shards1, arena, "gather_l1_start"))
        return flight1[5]

    h = x[0]
    saved = []
    for l in range(DEPTH):
        if l == 1:
            sh, landing = _gather_wait(*flight1[:4], h, "gather_l1_wait")
            fw["arena"][1] = fw["arena_in"][1] = _gather_finish(sh, landing, "gather_l1_finish")
        h, s = _layer_fwd(h, consts[l], fw, l, rest_of_layer0)
        saved.append(s)
    loss_vec, dh, g_final = _loss_head(h, final_g.reshape(1, -1), loss_target[0], "loss_head")
    loss = lax.psum(loss_vec[0, 0], ("x", "y", "c"))

    zero = jnp.zeros((1,), jnp.int32)
    chip_sel = chip.reshape(1).astype(jnp.int32)

    def chip_sums(ga, t):
        own, got = _own_halves(ga, cc), _swap_halves(ga, t + "grads_swap_halves")
        return [_sum_own_plus(o.reshape((1, -1, o.shape[-1])), zero, r.reshape((1, -1, r.shape[-1])),
                              t + f"grads_sum_chip{i}", BF16).reshape(o.shape) for i, (o, r) in enumerate(zip(own, got))]

    def all_sums(sums, got, t):
        return [_sum_own_plus(s, chip_sel, r, t + f"grads_sum_all{i}", F32) for i, (s, r) in enumerate(zip(sums, got))]

    gss = [None] * DEPTH
    dh, ga1, gss[1] = _layer_bwd(dh, saved[1], consts[1], fw, 1, lax.empty(ARENA_SHAPE, BF16))
    send_sems, recv_sems, bufs, token = _scatter_start(*chip_sums(ga1, "l1_"), "l1_grads_scatter_start")
    scatter0 = []

    def start_layer0_scatter(ga0):
        scatter0.extend(_scatter_start(*chip_sums(ga0, "l0_"), "l0_grads_scatter_start"))
        return scatter0[3]

    dh, _, gss[0] = _layer_bwd(dh + token[0:1, 0:1], saved[0], consts[0], fw, 0, lax.empty(ARENA_SHAPE, BF16),
                               start_layer0_scatter)
    grad_x = dh[None]

    flat = lambda a: a.reshape(-1, a.shape[-1])

    def finish_layer(bufs_l, l, carried, t):
        red = all_sums(bufs_l[:2], bufs_l[2:], t)
        g_l = _shard_grads(*_reduced_layer(red, _swap_many(red, t + "grads_swap_reduced"), cc))
        return {n: _adamw_layer(flat(p[n]), flat(mom[n]), flat(var[n]), g_l[n], l, carried and carried[n],
                                t + "adamw_" + n) for n in BIG}

    big = finish_layer(_scatter_wait(send_sems, recv_sems, bufs, scatter0[3], "l1_grads_scatter_wait"), 1, None, "l1_")
    done1 = big["w_in"][4][0:1, 0:1]

    small_full = {n: (g_final.reshape(-1) if n == "final_g" else
                      jnp.stack([gss[l][n].reshape(gss[l][n].shape[-2:] if n.startswith("conv") and n.endswith("_w")
                                                   else p[n].shape[1:]) for l in range(DEPTH)]))
                  for n in SMALL}
    part = _flat_pack([small_full[n] for n in SMALL]) + done1
    rows = part.shape[0]
    gathered = _allgather8(jnp.stack([part, part]), "gather_small_grads", pltpu.VMEM)
    small_packed = _sum_leading(gathered, "small_grads_sum")
    small_sum = _flat_unpack(small_packed, [small_full[n].shape for n in SMALL])

    big = finish_layer(_scatter_wait(*scatter0[:3], small_packed, "l0_grads_scatter_wait"), 0, big, "l0_")
    g, delta, new_m, new_v = ({n: big[n][i].reshape(p[n].shape) for n in BIG} for i in range(4))
    for n, a in zip(SMALL, small_sum):
        g[n] = lax.dynamic_slice_in_dim(a, chip * LANE, LANE, axis=2) if n in CONV_NAMES else a

    shapes = [p[n].shape for n in SMALL]
    d, nm, nv = _adamw(_flat_pack([p[n] for n in SMALL]), _flat_pack([g[n] for n in SMALL]),
                       _flat_pack([mom[n] for n in SMALL]), _flat_pack([var[n] for n in SMALL]), "adamw_small")
    for n, a, b, cval in zip(SMALL, _flat_unpack(d, shapes), _flat_unpack(nm, shapes), _flat_unpack(nv, shapes)):
        delta[n], new_m[n], new_v[n] = a, b, cval

    return (loss, grad_x, *[g[n] for n in NAMES], *[delta[n] for n in NAMES], *[new_m[n] for n in NAMES],
            *[new_v[n] for n in NAMES])
```

```python
import functools
import math

import numpy as np
import jax
import jax.numpy as jnp
from jax import lax
from jax.experimental import pallas as pl
from jax.experimental.pallas import tpu as pltpu

F32 = jnp.float32
BF16 = jnp.bfloat16
MESH = pl.DeviceIdType.MESH

D_MODEL = 1024
DEPTH = 2
BW = 512
HEAD_DIM = 64
N_Q = 8
N_KV = 2
BLK = 128
D_FF = 2816
IN_W = 8448
EPS = 1e-6
NEG_INF = -1e30
LRU_C = 8.0
CONV_A, CONV_B, CONV_D = 4, 3, 31
LANE = 128
ROW_TILE = 256
VMEM_LIMIT = 56 * 1024 * 1024
MM_VMEM_BUDGET = 36 * 1024 * 1024

C_AX, C_AG, C_BV, C_BC, C_BB = 0, 4, 8, 12, 16
OFF_Q, OFF_K, OFF_V = 2560, 3072, 3200
C_D1, C_D2 = 26, 30
OFF_GL = 4352

ADAM_LR, ADAM_B1, ADAM_B2, ADAM_EPS, ADAM_WD, ADAM_STEP = 0.001, 0.9, 0.999, 1e-08, 0.01, 10

ARENA_W = 1024
R_DOWN, R_GATE, R_UP, R_IN, R_O, R_OUT = 0, 2816, 5632, 8448, 16896, 17920
ARENA_ROWS = 19968
ROW_REGIONS = ((R_DOWN, 704), (R_GATE, 704), (R_UP, 704), (R_IN, 2112), (R_O, 256))
PIECE_OFF = (0, 704, 1408, 2112, 4224)
PIECE_ROWS = 4480
OUT_ROWS, OUT_COLS = 4 * BW, D_MODEL // 4

BIG = ("w_in", "w_a_out", "w_b_out", "w_c_out", "w_d_out", "w_o", "w_ffn_gate", "w_ffn_up", "w_ffn_down")
CONV_NAMES = ("conv_a_w", "conv_b_w", "conv_d_w")
N_TAPS = CONV_A + CONV_B + CONV_D
SMALL = ("norm1_g", "conv_a_w", "conv_a_b", "lru_wx", "lru_bx", "lru_wa", "lru_ba", "lru_lambda", "conv_b_w",
         "sinks", "conv_d_w", "conv_d_b", "ln_d_g", "ln_d_b", "norm2_g", "final_g")
NAMES = ['norm1_g', 'w_in', 'conv_a_w', 'conv_a_b', 'lru_wx', 'lru_bx', 'lru_wa', 'lru_ba', 'lru_lambda', 'w_a_out',
         'conv_b_w', 'w_b_out', 'sinks', 'w_c_out', 'conv_d_w', 'conv_d_b', 'ln_d_g', 'ln_d_b', 'w_d_out', 'w_o',
         'norm2_g', 'w_ffn_gate', 'w_ffn_up', 'w_ffn_down', 'final_g']


def _pick(n, cands, off=0):
    for c in cands:
        if n % c == 0 and off % c == 0:
            return c
    assert off == 0, (n, off)
    return n


class Win:
    def __init__(self, arena, l, off, rows):
        self.arena, self.l, self.off, self.rows = arena, l, off, rows
        self.shape = (rows, arena.shape[-1])


def _params(sem=None):
    return pltpu.CompilerParams(dimension_semantics=sem, vmem_limit_bytes=VMEM_LIMIT)


def _sig(z):
    return 1.0 / (1.0 + jnp.exp(-z))


def _dot(a, b, dims):
    return lax.dot_general(a.astype(BF16), b.astype(BF16), (dims, ((), ())), preferred_element_type=F32)


NN = ((1,), (0,))
NT = ((1,), (1,))
TN = ((0,), (0,))


def _mm(a, b, mode, name, out_dtype=F32, add=None, out=None, after=None):
    if mode == "nn":
        (m, k), n = a.shape, b.shape[1]
    elif mode == "nt":
        (m, k), n = a.shape, b.shape[0]
    else:
        (k, m), n = a.shape, b.shape[1]
    b_win = isinstance(b, Win)
    b_off = b.off if b_win else 0
    o_off = out.off if out is not None else 0
    if out is not None:
        out_dtype = out.arena.dtype
    tk = _pick(k, (2816, 2048, 1408, 1024, 768, 512, 256), b_off if mode != "nt" else 0)
    nk = k // tk
    n_off = b_off if mode == "nt" else 0
    a_bytes, b_bytes, o_bytes = a.dtype.itemsize, 2, jnp.dtype(out_dtype).itemsize

    def vmem_bytes(tm_, tn_):
        tile = tm_ * tn_
        return (2 * tk * (tm_ * a_bytes + tn_ * b_bytes) + 2 * tile * o_bytes + (tile * 4 if nk > 1 else 0)
                + (2 * tile * 4 if add is not None else 0) + tile * 4)

    pairs = [(tm_, tn_) for tm_ in (2048, 1024, 768, 512, 256, 128) for tn_ in (1024, 768, 512, 256, 128)
             if m % tm_ == 0 and o_off % tm_ == 0 and n % tn_ == 0 and n_off % tn_ == 0
             and vmem_bytes(tm_, tn_) <= MM_VMEM_BUDGET]
    tm, tn = max(pairs, key=lambda p: (p[0] * p[1], p[0]))
    dims = {"nn": NN, "nt": NT, "tn": TN}[mode]

    def body(*refs):
        a_ref, b_ref = refs[:2]
        c_ref = refs[2] if add is not None else None
        if nk == 1:
            r = _dot(a_ref[...], b_ref[...], dims)
            if add is not None:
                r = r + c_ref[...]
            refs[-1][...] = r.astype(out_dtype)
            return
        o_ref, acc = refs[-2:]
        kk = pl.program_id(2)

        @pl.when(kk == 0)
        def _():
            acc[...] = jnp.zeros_like(acc)

        acc[...] += _dot(a_ref[...], b_ref[...], dims)

        @pl.when(kk == nk - 1)
        def _():
            r = acc[...]
            if add is not None:
                r = r + c_ref[...]
            o_ref[...] = r.astype(out_dtype)

    if mode == "tn":
        a_spec = pl.BlockSpec((tk, tm), lambda i, j, q: (q, i))
    else:
        a_spec = pl.BlockSpec((tm, tk), lambda i, j, q: (i, q))
    if mode == "nt":
        b_blk, b_idx = (tn, tk), (lambda i, j, q: (b_off // tn + j, q))
    else:
        b_blk, b_idx = (tk, tn), (lambda i, j, q: (b_off // tk + q, j))
    if b_win and b.arena.ndim == 3:
        bl = b.l
        b_spec = pl.BlockSpec((None,) + b_blk, lambda i, j, q: (bl,) + b_idx(i, j, q))
    else:
        b_spec = pl.BlockSpec(b_blk, b_idx)
    plain_o = pl.BlockSpec((tm, tn), lambda i, j, q: (i, j))
    in_specs = [a_spec, b_spec] + ([plain_o] if add is not None else [])
    args = (a, b.arena if b_win else b) + ((add,) if add is not None else ())
    if after is not None:
        in_specs.append(pl.BlockSpec(after.shape, lambda i, j, q: (0, 0)))
        args = args + (after,)
    aliases = {}
    if out is None:
        o_spec, o_shape = plain_o, jax.ShapeDtypeStruct((m, n), out_dtype)
    else:
        ol = out.l
        if out.arena.ndim == 3:
            o_spec = pl.BlockSpec((None, tm, tn), lambda i, j, q: (ol, o_off // tm + i, j))
        else:
            o_spec = pl.BlockSpec((tm, tn), lambda i, j, q: (o_off // tm + i, j))
        o_shape = jax.ShapeDtypeStruct(out.arena.shape, out_dtype)
        aliases = {len(args): 0}
        in_specs.append(pl.BlockSpec(memory_space=pl.ANY))
        args = args + (out.arena,)
    return pl.pallas_call(
        body, name=name, out_shape=o_shape,
        grid=(m // tm, n // tn, nk), in_specs=in_specs, out_specs=o_spec,
        scratch_shapes=[pltpu.VMEM((tm, tn), F32)] if nk > 1 else [], input_output_aliases=aliases,
        compiler_params=_params(("parallel", "parallel", "arbitrary")),
    )(*args)


def _row_spec(cols, tr=ROW_TILE):
    return pl.BlockSpec((tr, cols), lambda i: (i, 0))


def _vec_spec(cols):
    return pl.BlockSpec((1, cols), lambda i: (0, 0))


def _rms_fwd(x, g, name):
    t, d = x.shape

    def body(x_ref, g_ref, o_ref):
        xv = x_ref[...]
        r = lax.rsqrt(jnp.mean(xv * xv, axis=1, keepdims=True) + EPS)
        o_ref[...] = (xv * r * g_ref[...]).astype(BF16)

    return pl.pallas_call(
        body, name=name, out_shape=jax.ShapeDtypeStruct((t, d), BF16), grid=(t // ROW_TILE,),
        in_specs=[_row_spec(d), _vec_spec(d)], out_specs=_row_spec(d), compiler_params=_params(("parallel",)),
    )(x, g)


def _rms_bwd(x, g, dxn, dres, name):
    t, d = x.shape

    def body(x_ref, g_ref, dy_ref, dr_ref, dx_ref, dg_ref):
        @pl.when(pl.program_id(0) == 0)
        def _():
            dg_ref[...] = jnp.zeros_like(dg_ref)

        xv = x_ref[...]
        dy = dy_ref[...]
        r = lax.rsqrt(jnp.mean(xv * xv, axis=1, keepdims=True) + EPS)
        w = dy * g_ref[...]
        dx_ref[...] = dr_ref[...] + r * w - xv * (r * r * r) * jnp.mean(w * xv, axis=1, keepdims=True)
        dg_ref[...] += jnp.sum(dy * xv * r, axis=0, keepdims=True)

    return pl.pallas_call(
        body, name=name,
        out_shape=(jax.ShapeDtypeStruct((t, d), F32), jax.ShapeDtypeStruct((1, d), F32)), grid=(t // ROW_TILE,),
        in_specs=[_row_spec(d), _vec_spec(d), _row_spec(d), _row_spec(d)], out_specs=(_row_spec(d), _vec_spec(d)),
        compiler_params=_params(("arbitrary",)),
    )(x, g, dxn, dres)


def _loss_head(x, g, tgt, name):
    t, d = x.shape

    def body(x_ref, g_ref, t_ref, loss_ref, dx_ref, dg_ref):
        @pl.when(pl.program_id(0) == 0)
        def _():
            dg_ref[...] = jnp.zeros_like(dg_ref)
            loss_ref[...] = jnp.zeros_like(loss_ref)

        xv = x_ref[...]
        gv = g_ref[...]
        r = lax.rsqrt(jnp.mean(xv * xv, axis=1, keepdims=True) + EPS)
        e = xv * r * gv - t_ref[...]
        loss_ref[...] += jnp.full(loss_ref.shape, (0.5 / d) * jnp.sum(e * e), F32)
        dy = e * (1.0 / d)
        w = dy * gv
        dx_ref[...] = r * w - xv * (r * r * r) * jnp.mean(w * xv, axis=1, keepdims=True)
        dg_ref[...] += jnp.sum(dy * xv * r, axis=0, keepdims=True)

    return pl.pallas_call(
        body, name=name,
        out_shape=(jax.ShapeDtypeStruct((1, LANE), F32), jax.ShapeDtypeStruct((t, d), F32),
                   jax.ShapeDtypeStruct((1, d), F32)),
        grid=(t // ROW_TILE,), in_specs=[_row_spec(d), _vec_spec(d), _row_spec(d)],
        out_specs=(_vec_spec(LANE), _row_spec(d), _vec_spec(d)), compiler_params=_params(("arbitrary",)),
    )(x, g, tgt)


def _swiglu_fwd(gg, uu, name):
    t, f = gg.shape

    def body(g_ref, u_ref, o_ref):
        gv = g_ref[...]
        o_ref[...] = (gv * _sig(gv) * u_ref[...]).astype(BF16)

    return pl.pallas_call(
        body, name=name, out_shape=jax.ShapeDtypeStruct((t, f), BF16), grid=(t // ROW_TILE,),
        in_specs=[_row_spec(f), _row_spec(f)], out_specs=_row_spec(f), compiler_params=_params(("parallel",)),
    )(gg, uu)


def _swiglu_bwd(gg, uu, dact, name):
    t, f = gg.shape

    def body(g_ref, u_ref, d_ref, dg_ref, du_ref):
        gv = g_ref[...]
        dv = d_ref[...]
        s = _sig(gv)
        dg_ref[...] = (dv * u_ref[...] * s * (1.0 + gv * (1.0 - s))).astype(BF16)
        du_ref[...] = (dv * gv * s).astype(BF16)

    return pl.pallas_call(
        body, name=name,
        out_shape=(jax.ShapeDtypeStruct((t, f), BF16), jax.ShapeDtypeStruct((t, f), BF16)), grid=(t // ROW_TILE,),
        in_specs=[_row_spec(f)] * 3, out_specs=(_row_spec(f), _row_spec(f)), compiler_params=_params(("parallel",)),
    )(gg, uu, dact)


MERGE_COLS = 256
MERGE_ROWS = 1024


def _gate_specs(mr):
    nb = D_MODEL // MERGE_COLS
    base = OFF_GL // MERGE_COLS
    return [pl.BlockSpec((mr, MERGE_COLS), functools.partial(lambda i, j, kk: (i, base + nb * kk + j), kk=kk))
            for kk in range(4)]


def _merge_fwd(proj, ys, name):
    t = proj.shape[0]
    mr = min(t, MERGE_ROWS)
    yspec = pl.BlockSpec((mr, MERGE_COLS), lambda i, j: (i, j))

    def body(g0, g1, g2, g3, y0, y1, y2, y3, o_ref):
        acc = _sig(g0[...]) * y0[...]
        acc += _sig(g1[...]) * y1[...]
        acc += _sig(g2[...]) * y2[...]
        acc += _sig(g3[...]) * y3[...]
        o_ref[...] = acc.astype(BF16)

    return pl.pallas_call(
        body, name=name, out_shape=jax.ShapeDtypeStruct((t, D_MODEL), BF16),
        grid=(t // mr, D_MODEL // MERGE_COLS), in_specs=_gate_specs(mr) + [yspec] * 4, out_specs=yspec,
        compiler_params=_params(("parallel", "parallel")),
    )(proj, proj, proj, proj, *ys)


def _merge_bwd(proj, ys, dmerged, name):
    t = proj.shape[0]
    mr = min(t, MERGE_ROWS)
    yspec = pl.BlockSpec((mr, MERGE_COLS), lambda i, j: (i, j))

    def body(g0, g1, g2, g3, y0, y1, y2, y3, dm_ref, *outs):
        dm = dm_ref[...]
        for gr, yr, dy_ref, dg_ref in zip((g0, g1, g2, g3), (y0, y1, y2, y3), outs[:4], outs[4:]):
            s = _sig(gr[...])
            dy_ref[...] = (dm * s).astype(BF16)
            dg_ref[...] = (dm * yr[...] * s * (1.0 - s)).astype(BF16)

    shp = jax.ShapeDtypeStruct((t, D_MODEL), BF16)
    outs = pl.pallas_call(
        body, name=name, out_shape=(shp,) * 8, grid=(t // mr, D_MODEL // MERGE_COLS),
        in_specs=_gate_specs(mr) + [yspec] * 5, out_specs=(yspec,) * 8, compiler_params=_params(("parallel", "parallel")),
    )(proj, proj, proj, proj, *ys, dmerged)
    return outs[:4], outs[4:]


def _ln_silu_fwd(cd, g, b, name):
    t, c = cd.shape

    def body(x_ref, g_ref, b_ref, o_ref):
        xv = x_ref[...]
        mu = jnp.mean(xv, axis=1, keepdims=True)
        xc = xv - mu
        rs = lax.rsqrt(jnp.mean(xc * xc, axis=1, keepdims=True) + EPS)
        z = xc * rs * g_ref[...] + b_ref[...]
        o_ref[...] = (z * _sig(z)).astype(BF16)

    return pl.pallas_call(
        body, name=name, out_shape=jax.ShapeDtypeStruct((t, c), BF16), grid=(t // ROW_TILE,),
        in_specs=[_row_spec(c), _vec_spec(c), _vec_spec(c)], out_specs=_row_spec(c),
        compiler_params=_params(("parallel",)),
    )(cd, g, b)


def _ln_silu_bwd(cd, g, b, dy, name):
    t, c = cd.shape

    def body(x_ref, g_ref, b_ref, dy_ref, dx_ref, dg_ref, db_ref):
        @pl.when(pl.program_id(0) == 0)
        def _():
            dg_ref[...] = jnp.zeros_like(dg_ref)
            db_ref[...] = jnp.zeros_like(db_ref)

        xv = x_ref[...]
        gv = g_ref[...]
        mu = jnp.mean(xv, axis=1, keepdims=True)
        xc = xv - mu
        rs = lax.rsqrt(jnp.mean(xc * xc, axis=1, keepdims=True) + EPS)
        xh = xc * rs
        z = xh * gv + b_ref[...]
        s = _sig(z)
        dz = dy_ref[...] * s * (1.0 + z * (1.0 - s))
        dg_ref[...] += jnp.sum(dz * xh, axis=0, keepdims=True)
        db_ref[...] += jnp.sum(dz, axis=0, keepdims=True)
        dxh = dz * gv
        dx_ref[...] = rs * (dxh - jnp.mean(dxh, axis=1, keepdims=True) - xh * jnp.mean(dxh * xh, axis=1, keepdims=True))

    return pl.pallas_call(
        body, name=name,
        out_shape=(jax.ShapeDtypeStruct((t, c), F32), jax.ShapeDtypeStruct((1, c), F32),
                   jax.ShapeDtypeStruct((1, c), F32)),
        grid=(t // ROW_TILE,), in_specs=[_row_spec(c), _vec_spec(c), _vec_spec(c), _row_spec(c)],
        out_specs=(_row_spec(c), _vec_spec(c), _vec_spec(c)), compiler_params=_params(("arbitrary",)),
    )(cd, g, b, dy)


def _shift_dn(x, k):
    if k == 0:
        return x
    row = lax.broadcasted_iota(jnp.int32, x.shape, 0)
    return jnp.where(row >= k, pltpu.roll(x, k, 0), 0.0)


def _shift_up(x, k):
    if k == 0:
        return x
    t = x.shape[0]
    row = lax.broadcasted_iota(jnp.int32, x.shape, 0)
    return jnp.where(row < t - k, pltpu.roll(x, t - k, 0), 0.0)


def _conv_fwd(x, w_ref, taps):
    acc = w_ref[pl.ds(taps - 1, 1), :] * x
    for k in range(taps - 1):
        acc += w_ref[pl.ds(k, 1), :] * _shift_dn(x, taps - 1 - k)
    return acc


def _conv_bwd(x, dy, w_ref, dw_ref, taps):
    dx = w_ref[pl.ds(taps - 1, 1), :] * dy
    dw_ref[pl.ds(taps - 1, 1), :] = jnp.sum(dy * x, axis=0, keepdims=True)
    for k in range(taps - 1):
        s = taps - 1 - k
        dx += w_ref[pl.ds(k, 1), :] * _shift_up(dy, s)
        dw_ref[pl.ds(k, 1), :] = jnp.sum(dy * _shift_dn(x, s), axis=0, keepdims=True)
    return dx


def _scan_fwd(a, u):
    t = a.shape[0]
    k = 1
    while k < t:
        u = u + a * _shift_dn(u, k)
        if 2 * k < t:
            a = a * _shift_dn(a, k)
        k *= 2
    return u


def _scan_rev(a, u):
    t = a.shape[0]
    k = 1
    while k < t:
        u = u + a * _shift_up(u, k)
        if 2 * k < t:
            a = a * _shift_up(a, k)
        k *= 2
    return u


def _one_minus_exp(y):
    return jnp.where(y > -1e-3, -(y + 0.5 * y * y + (1.0 / 6.0) * y * y * y), 1.0 - jnp.exp(y))


GELU_C = math.sqrt(2.0 / math.pi)


def _gelu(x):
    th = jnp.tanh(GELU_C * (x + 0.044715 * x * x * x))
    return 0.5 * x * (1.0 + th), th


def _softplus(x):
    return jnp.maximum(x, 0.0) + jnp.log(1.0 + jnp.exp(-jnp.abs(x)))


def _chunk_spec(t, blk0):
    return pl.BlockSpec((t, LANE), functools.partial(lambda c, b: (0, b + c), b=blk0))


def _tap_spec(taps):
    return pl.BlockSpec((taps, LANE), lambda c: (0, c))


def _cvec_spec():
    return pl.BlockSpec((1, LANE), lambda c: (0, c))


def _cmat_spec():
    return pl.BlockSpec((1, LANE, LANE), lambda c: (c, 0, 0))


def _lru_forward(ax, wA_ref, bA_ref, wx_ref, bx_ref, wa_ref, ba_ref, lam_ref, h=None):
    ca = _conv_fwd(ax, wA_ref, CONV_A) + bA_ref[...]
    gi = _sig(_dot(ca, wx_ref[0], NN) + bx_ref[...])
    gr = _sig(_dot(ca, wa_ref[0], NN) + ba_ref[...])
    sp = _softplus(-lam_ref[...])
    la = -LRU_C * gr * sp
    a = jnp.exp(la)
    mult = jnp.sqrt(_one_minus_exp(2.0 * la))
    if h is None:
        h = _scan_fwd(a, ca * gi * mult)
    return ca, gi, gr, sp, a, mult, h


def _a_fwd(proj, wA, bA, wx, bx, wa, ba, lam, name):
    t = proj.shape[0]

    def body(ax_ref, ag_ref, wA_ref, bA_ref, wx_ref, bx_ref, wa_ref, ba_ref, lam_ref, o_ref, h_ref):
        h = _lru_forward(ax_ref[...], wA_ref, bA_ref, wx_ref, bx_ref, wa_ref, ba_ref, lam_ref)[-1]
        h_ref[...] = h
        o_ref[...] = (h * _gelu(ag_ref[...])[0]).astype(BF16)

    return pl.pallas_call(
        body, name=name, out_shape=(jax.ShapeDtypeStruct((t, BW), BF16), jax.ShapeDtypeStruct((t, BW), F32)),
        grid=(BW // LANE,),
        in_specs=[_chunk_spec(t, C_AX), _chunk_spec(t, C_AG), _tap_spec(CONV_A), _cvec_spec(), _cmat_spec(),
                  _cvec_spec(), _cmat_spec(), _cvec_spec(), _cvec_spec()],
        out_specs=(_chunk_spec(t, 0), _chunk_spec(t, 0)), compiler_params=_params(("parallel",)),
    )(proj, proj, wA, bA, wx, bx, wa, ba, lam)


def _a_bwd(proj, h_fwd, dya, wA, bA, wx, bx, wa, ba, lam, name):
    t = proj.shape[0]

    def body(ax_ref, ag_ref, h_ref, dy_ref, wA_ref, bA_ref, wx_ref, bx_ref, wa_ref, ba_ref, lam_ref,
             dax_ref, dag_ref, dwA_ref, dbA_ref, dwx_ref, dbx_ref, dwa_ref, dba_ref, dlam_ref):
        ax = ax_ref[...]
        ag = ag_ref[...]
        dy = dy_ref[...]
        ca, gi, gr, sp, a, mult, h = _lru_forward(ax, wA_ref, bA_ref, wx_ref, bx_ref, wa_ref, ba_ref, lam_ref,
                                                  h_ref[...])
        gel, th = _gelu(ag)
        dgel = 0.5 * (1.0 + th) + 0.5 * ag * (1.0 - th * th) * GELU_C * (1.0 + 3.0 * 0.044715 * ag * ag)
        dag_ref[...] = (dy * h * dgel).astype(BF16)
        s = _scan_rev(_shift_up(a, 1), dy * gel)
        da = s * _shift_dn(h, 1)
        dca = s * gi * mult
        dgi = s * ca * mult
        dmult = s * ca * gi
        dla = da * a - dmult * a * a / mult
        dgr = dla * (-LRU_C * sp)
        dsp = jnp.sum(dla * (-LRU_C * gr), axis=0, keepdims=True)
        dlam_ref[...] = -_sig(-lam_ref[...]) * dsp
        dzi = dgi * gi * (1.0 - gi)
        dzr = dgr * gr * (1.0 - gr)
        dbx_ref[...] = jnp.sum(dzi, axis=0, keepdims=True)
        dba_ref[...] = jnp.sum(dzr, axis=0, keepdims=True)
        dwx_ref[0] = _dot(ca, dzi, TN)
        dwa_ref[0] = _dot(ca, dzr, TN)
        dca += _dot(dzi, wx_ref[0], NT) + _dot(dzr, wa_ref[0], NT)
        dbA_ref[...] = jnp.sum(dca, axis=0, keepdims=True)
        dax_ref[...] = _conv_bwd(ax, dca, wA_ref, dwA_ref, CONV_A).astype(BF16)

    big = jax.ShapeDtypeStruct((t, BW), BF16)
    vec = jax.ShapeDtypeStruct((1, BW), F32)
    mat = jax.ShapeDtypeStruct((BW // LANE, LANE, LANE), F32)
    return pl.pallas_call(
        body, name=name,
        out_shape=(big, big, jax.ShapeDtypeStruct((CONV_A, BW), F32), vec, mat, vec, mat, vec, vec),
        grid=(BW // LANE,),
        in_specs=[_chunk_spec(t, C_AX), _chunk_spec(t, C_AG), _chunk_spec(t, 0), _chunk_spec(t, 0), _tap_spec(CONV_A),
                  _cvec_spec(), _cmat_spec(), _cvec_spec(), _cmat_spec(), _cvec_spec(), _cvec_spec()],
        out_specs=(_chunk_spec(t, 0), _chunk_spec(t, 0), _tap_spec(CONV_A), _cvec_spec(), _cmat_spec(), _cvec_spec(),
                   _cmat_spec(), _cvec_spec(), _cvec_spec()),
        compiler_params=_params(("parallel",)),
    )(proj, proj, h_fwd, dya, wA, bA, wx, bx, wa, ba, lam)


def _b_fwd(proj, wB, name):
    t = proj.shape[0]

    def body(bv_ref, bc_ref, bb_ref, w_ref, o_ref):
        o_ref[...] = (bb_ref[...] * _conv_fwd(bc_ref[...] * bv_ref[...], w_ref, CONV_B)).astype(BF16)

    return pl.pallas_call(
        body, name=name, out_shape=jax.ShapeDtypeStruct((t, BW), BF16), grid=(BW // LANE,),
        in_specs=[_chunk_spec(t, C_BV), _chunk_spec(t, C_BC), _chunk_spec(t, C_BB), _tap_spec(CONV_B)],
        out_specs=_chunk_spec(t, 0), compiler_params=_params(("parallel",)),
    )(proj, proj, proj, wB)


def _b_bwd(proj, dyb, wB, name):
    t = proj.shape[0]

    def body(bv_ref, bc_ref, bb_ref, dy_ref, w_ref, dbv_ref, dbc_ref, dbb_ref, dw_ref):
        bv = bv_ref[...]
        bc = bc_ref[...]
        dy = dy_ref[...]
        p = bc * bv
        dbb_ref[...] = (dy * _conv_fwd(p, w_ref, CONV_B)).astype(BF16)
        dp = _conv_bwd(p, dy * bb_ref[...], w_ref, dw_ref, CONV_B)
        dbc_ref[...] = (dp * bv).astype(BF16)
        dbv_ref[...] = (dp * bc).astype(BF16)

    big = jax.ShapeDtypeStruct((t, BW), BF16)
    return pl.pallas_call(
        body, name=name, out_shape=(big, big, big, jax.ShapeDtypeStruct((CONV_B, BW), F32)), grid=(BW // LANE,),
        in_specs=[_chunk_spec(t, C_BV), _chunk_spec(t, C_BC), _chunk_spec(t, C_BB), _chunk_spec(t, 0),
                  _tap_spec(CONV_B)],
        out_specs=(_chunk_spec(t, 0),) * 3 + (_tap_spec(CONV_B),), compiler_params=_params(("parallel",)),
    )(proj, proj, proj, dyb, wB)


def _d_conv_fwd(proj, wD, bD, name):
    t = proj.shape[0]

    def body(d1_ref, d2_ref, w_ref, b_ref, o_ref):
        o_ref[...] = _conv_fwd(d1_ref[...] * _sig(d2_ref[...]), w_ref, CONV_D) + b_ref[...]

    return pl.pallas_call(
        body, name=name, out_shape=jax.ShapeDtypeStruct((t, BW), F32), grid=(BW // LANE,),
        in_specs=[_chunk_spec(t, C_D1), _chunk_spec(t, C_D2), _tap_spec(CONV_D), _cvec_spec()],
        out_specs=_chunk_spec(t, 0), compiler_params=_params(("parallel",)),
    )(proj, proj, wD, bD)


def _d_conv_bwd(proj, dcd, wD, name):
    t = proj.shape[0]

    def body(d1_ref, d2_ref, dy_ref, w_ref, dd1_ref, dd2_ref, dw_ref, db_ref):
        d1 = d1_ref[...]
        s = _sig(d2_ref[...])
        dy = dy_ref[...]
        db_ref[...] = jnp.sum(dy, axis=0, keepdims=True)
        dd = _conv_bwd(d1 * s, dy, w_ref, dw_ref, CONV_D)
        dd1_ref[...] = (dd * s).astype(BF16)
        dd2_ref[...] = (dd * d1 * s * (1.0 - s)).astype(BF16)

    big = jax.ShapeDtypeStruct((t, BW), BF16)
    return pl.pallas_call(
        body, name=name,
        out_shape=(big, big, jax.ShapeDtypeStruct((CONV_D, BW), F32), jax.ShapeDtypeStruct((1, BW), F32)),
        grid=(BW // LANE,),
        in_specs=[_chunk_spec(t, C_D1), _chunk_spec(t, C_D2), _chunk_spec(t, 0), _tap_spec(CONV_D)],
        out_specs=(_chunk_spec(t, 0), _chunk_spec(t, 0), _tap_spec(CONV_D), _cvec_spec()),
        compiler_params=_params(("parallel",)),
    )(proj, proj, dcd, wD)


SCALE = HEAD_DIM ** -0.5
GROUP = N_Q // N_KV


GROWS = GROUP * BLK


def _per_head(ss_ref, row, g):
    head = lax.broadcasted_iota(jnp.int32, (GROWS, 1), 0) // BLK
    col = jnp.full((GROWS, 1), ss_ref[row, g * GROUP + GROUP - 1], F32)
    for i in range(GROUP - 1):
        col = jnp.where(head == i, ss_ref[row, g * GROUP + i], col)
    return col


def _attn_probs(q_ref, k_ref, ss_ref, g, n):
    qi = lax.broadcasted_iota(jnp.int32, (GROWS, BLK), 0) % BLK
    ki = lax.broadcasted_iota(jnp.int32, (GROWS, BLK), 1)
    dist = (qi - ki).astype(F32)
    sink = _per_head(ss_ref, 0, g)
    slope = _per_head(ss_ref, 1, g)
    s0 = pl.multiple_of(n * BLK, BLK)
    sp = pl.multiple_of(jnp.maximum(n - 1, 0) * BLK, BLK)
    q = q_ref[:, pl.ds(s0, BLK), :].reshape(GROWS, HEAD_DIM)
    kc = k_ref[0, pl.ds(s0, BLK), :]
    kp = k_ref[0, pl.ds(sp, BLK), :]
    sc = jnp.where(ki <= qi, _dot(q, kc, NT) * SCALE - slope * dist, NEG_INF)
    first = jnp.where(n >= 1, 0, BLK)
    sv = jnp.where(ki > qi + first, _dot(q, kp, NT) * SCALE - slope * (dist + BLK), NEG_INF)
    m = jnp.maximum(jnp.maximum(jnp.max(sc, axis=1, keepdims=True), jnp.max(sv, axis=1, keepdims=True)), sink)
    pc = jnp.exp(sc - m)
    pp = jnp.exp(sv - m)
    ps = jnp.exp(sink - m)
    z = jnp.sum(pc, axis=1, keepdims=True) + jnp.sum(pp, axis=1, keepdims=True) + ps
    return s0, sp, q, kc, kp, pc, pp, ps, z


def _attn_specs(t):
    qs = pl.BlockSpec((GROUP, t, HEAD_DIM), lambda g: (g, 0, 0))
    ks = pl.BlockSpec((1, t, HEAD_DIM), lambda g: (g, 0, 0))
    ss = pl.BlockSpec(memory_space=pltpu.SMEM)
    return qs, ks, ss


def _attn_fwd(q, k, v, ss, name):
    t = q.shape[1]
    qs, ks, sspec = _attn_specs(t)

    def body(q_ref, k_ref, v_ref, ss_ref, o_ref):
        g = pl.program_id(0)

        def blk(n, carry):
            s0, sp, _, _, _, pc, pp, _, z = _attn_probs(q_ref, k_ref, ss_ref, g, n)
            o = _dot(pc, v_ref[0, pl.ds(s0, BLK), :], NN) + _dot(pp, v_ref[0, pl.ds(sp, BLK), :], NN)
            o_ref[:, pl.ds(s0, BLK), :] = (o / z).astype(BF16).reshape(GROUP, BLK, HEAD_DIM)
            return carry

        lax.fori_loop(0, t // BLK, blk, 0)

    return pl.pallas_call(
        body, name=name, out_shape=jax.ShapeDtypeStruct((N_Q, t, HEAD_DIM), BF16), grid=(N_KV,),
        in_specs=[qs, ks, ks, sspec], out_specs=qs, compiler_params=_params(("parallel",)),
    )(q, k, v, ss)


def _attn_bwd(q, k, v, do, ss, name):
    t = q.shape[1]
    qs, ks, sspec = _attn_specs(t)

    def body(q_ref, k_ref, v_ref, do_ref, ss_ref, dq_ref, dk_ref, dv_ref, ds_ref):
        g = pl.program_id(0)
        dk_ref[...] = jnp.zeros_like(dk_ref)
        dv_ref[...] = jnp.zeros_like(dv_ref)

        def blk(n, dsink):
            s0, sp, q, kc, kp, pc, pp, ps, z = _attn_probs(q_ref, k_ref, ss_ref, g, n)
            rz = 1.0 / z
            pc = pc * rz
            pp = pp * rz
            do_b = do_ref[:, pl.ds(s0, BLK), :].reshape(GROWS, HEAD_DIM)
            dpc = _dot(do_b, v_ref[0, pl.ds(s0, BLK), :], NT)
            dpp = _dot(do_b, v_ref[0, pl.ds(sp, BLK), :], NT)
            delta = jnp.sum(pc * dpc, axis=1, keepdims=True) + jnp.sum(pp * dpp, axis=1, keepdims=True)
            dsc = pc * (dpc - delta)
            dsp = pp * (dpp - delta)
            dq = (_dot(dsc, kc, NN) + _dot(dsp, kp, NN)) * SCALE
            dq_ref[:, pl.ds(s0, BLK), :] = dq.astype(BF16).reshape(GROUP, BLK, HEAD_DIM)
            dk_ref[0, pl.ds(s0, BLK), :] += _dot(dsc, q, TN) * SCALE
            dk_ref[0, pl.ds(sp, BLK), :] += _dot(dsp, q, TN) * SCALE
            dv_ref[0, pl.ds(s0, BLK), :] += _dot(pc, do_b, TN)
            dv_ref[0, pl.ds(sp, BLK), :] += _dot(pp, do_b, TN)
            return dsink - ps * rz * delta

        dsink = lax.fori_loop(0, t // BLK, blk, jnp.zeros((GROWS, 1), F32))
        for i in range(GROUP):
            ds_ref[i] = jnp.full(ds_ref.shape[1:], jnp.sum(dsink[i * BLK:(i + 1) * BLK]), F32)

    kv = jax.ShapeDtypeStruct((N_KV, t, HEAD_DIM), F32)
    return pl.pallas_call(
        body, name=name,
        out_shape=(jax.ShapeDtypeStruct((N_Q, t, HEAD_DIM), BF16), kv, kv, jax.ShapeDtypeStruct((N_Q, 8, LANE), F32)),
        grid=(N_KV,), in_specs=[qs, ks, ks, qs, sspec],
        out_specs=(qs, ks, ks, pl.BlockSpec((GROUP, 8, LANE), lambda g: (g, 0, 0))),
        compiler_params=_params(("parallel",)),
    )(q, k, v, do, ss)


def _heads(x2d, n):
    t = x2d.shape[0]
    return x2d.reshape(t, n, HEAD_DIM).transpose(1, 0, 2)


def _unheads(x3d):
    n, t, _ = x3d.shape
    return x3d.transpose(1, 0, 2).reshape(t, n * HEAD_DIM)


SMALL_ELEMS = 256 * 1024
TILE_ELEMS = 640 * 1024


def _row_tile(r, c):
    if r * c <= SMALL_ELEMS:
        return r
    return _pick(r, [t for t in (512, 256, 128, 64, 32, 16, 8) if t * c <= TILE_ELEMS])


def _adamw_update(w, gv, m, v):
    nm = ADAM_B1 * m + (1.0 - ADAM_B1) * gv
    nv = ADAM_B2 * v + (1.0 - ADAM_B2) * (gv * gv)
    m_hat = nm / (1.0 - ADAM_B1 ** ADAM_STEP)
    v_hat = nv / (1.0 - ADAM_B2 ** ADAM_STEP)
    return -ADAM_LR * (m_hat / (jnp.sqrt(v_hat) + ADAM_EPS) + ADAM_WD * w), nm, nv


def _adamw(w, g, m, v, name):
    r, c = w.shape
    tr = _row_tile(r, c)
    spec = pl.BlockSpec((tr, c), lambda i: (i, 0))

    def body(w_ref, g_ref, m_ref, v_ref, d_ref, nm_ref, nv_ref):
        d_ref[...], nm_ref[...], nv_ref[...] = _adamw_update(w_ref[...], g_ref[...], m_ref[...], v_ref[...])

    shp = jax.ShapeDtypeStruct((r, c), F32)
    return pl.pallas_call(
        body, name=name, out_shape=(shp, shp, shp), grid=(r // tr,), in_specs=[spec] * 4, out_specs=(spec,) * 3,
        compiler_params=_params(("parallel",)),
    )(w, g, m, v)


def _adamw_layer(w, m, v, g, l, prev, name):
    r, c = g.shape
    tr = _row_tile(r, c)
    layer = pl.BlockSpec((tr, c), lambda i: (l * (r // tr) + i, 0))

    def body(w_ref, m_ref, v_ref, g_ref, *rest):
        go_ref, d_ref, nm_ref, nv_ref, token = rest[-5:]
        gv = g_ref[...]
        go_ref[...] = gv
        token[...] = jnp.zeros_like(token)
        d_ref[...], nm_ref[...], nv_ref[...] = _adamw_update(w_ref[...], gv, m_ref[...], v_ref[...])

    carried = list(prev[:4]) if prev is not None else []
    shp = jax.ShapeDtypeStruct(w.shape, F32)
    return pl.pallas_call(
        body, name=name, out_shape=(shp,) * 4 + (jax.ShapeDtypeStruct((8, LANE), F32),), grid=(r // tr,),
        in_specs=[layer] * 3 + [pl.BlockSpec((tr, c), lambda i: (i, 0))] + [pl.BlockSpec(memory_space=pl.ANY)] * len(carried),
        out_specs=(layer,) * 4 + (pl.BlockSpec((8, LANE), lambda i: (0, 0)),),
        input_output_aliases={4 + i: i for i in range(len(carried))}, compiler_params=_params(("arbitrary",)),
    )(w, m, v, g, *carried)


def _sum_leading(x, name):
    n, r, c = x.shape
    tr = _row_tile(r, c)

    def body(x_ref, o_ref):
        acc = x_ref[0]
        for i in range(1, n):
            acc = acc + x_ref[i]
        o_ref[...] = acc

    return pl.pallas_call(
        body, name=name, out_shape=jax.ShapeDtypeStruct((r, c), F32), grid=(r // tr,),
        in_specs=[pl.BlockSpec((n, tr, c), lambda i: (0, i, 0))], out_specs=pl.BlockSpec((tr, c), lambda i: (i, 0)),
        compiler_params=_params(("parallel",)),
    )(x)


def _sum_own_plus(p, sel, recv, name, out_dtype):
    _, r, c = p.shape
    n = recv.shape[0]
    tr = _pick(r, (512, 448, 256, 128, 64, 16))

    def body(sel_ref, p_ref, r_ref, o_ref):
        acc = p_ref[0].astype(F32)
        for i in range(n):
            acc = acc + r_ref[i].astype(F32)
        o_ref[...] = acc.astype(out_dtype)

    grid_spec = pltpu.PrefetchScalarGridSpec(
        num_scalar_prefetch=1, grid=(r // tr,),
        in_specs=[pl.BlockSpec((1, tr, c), lambda i, s: (s[0], i, 0)), pl.BlockSpec((n, tr, c), lambda i, s: (0, i, 0))],
        out_specs=pl.BlockSpec((tr, c), lambda i, s: (i, 0)))
    return pl.pallas_call(
        body, name=name, out_shape=jax.ShapeDtypeStruct((r, c), out_dtype), grid_spec=grid_spec,
        compiler_params=_params(("parallel",)),
    )(sel, p, recv)


def _coords():
    return lax.axis_index("x"), lax.axis_index("y"), lax.axis_index("c")


def _allgather8(x2, name, space):
    _, m, n = x2.shape

    def body(x_ref, out_ref, send_sems, recv_sems, local_sem):
        x, y, c = _coords()
        me, sibling = (x, y, c), (x, y, 1 - c)
        chips = [(1 - x, y), (x, 1 - y), (1 - x, 1 - y)]
        mine_src = x_ref.at[c]

        def rows(px, py, pc):
            return out_ref.at[4 * px + 2 * py + pc]

        def copy(k, block, to, src=None):
            return pltpu.make_async_remote_copy(
                src_ref=rows(*block) if src is None else src, dst_ref=rows(*block),
                send_sem=send_sems.at[k], recv_sem=recv_sems.at[k], device_id=to, device_id_type=MESH)

        mine = pltpu.make_async_copy(mine_src, rows(*me), local_sem)
        mine.start()
        first = [copy(0, me, sibling, src=mine_src)]
        first += [copy(1 + j, me, (*chip, c), src=mine_src) for j, chip in enumerate(chips)]
        for cp in first:
            cp.start()
        passed = [copy(4 + j, (*chip, c), sibling) for j, chip in enumerate(chips)]
        for j, chip in enumerate(chips):
            copy(1 + j, (*chip, c), me).wait_recv()
            passed[j].start()
        copy(0, sibling, me).wait_recv()
        for j, chip in enumerate(chips):
            copy(4 + j, (*chip, 1 - c), me).wait_recv()
        for cp in first + passed:
            cp.wait_send()
        mine.wait()

    return pl.pallas_call(
        body, name=name, out_shape=jax.ShapeDtypeStruct((8, m, n), x2.dtype),
        in_specs=[pl.BlockSpec(memory_space=space)], out_specs=pl.BlockSpec(memory_space=space),
        scratch_shapes=[pltpu.SemaphoreType.DMA((7,)), pltpu.SemaphoreType.DMA((7,)), pltpu.SemaphoreType.DMA],
        compiler_params=pltpu.CompilerParams(vmem_limit_bytes=VMEM_LIMIT),
    )(x2)


N_REG = len(ROW_REGIONS) + 1


def _chip_window(ref, lead, r, j):
    view = ref if lead is None else ref.at[lead]
    if r < len(ROW_REGIONS):
        off, rows = ROW_REGIONS[r]
        return view.at[pl.ds(pl.multiple_of(off + j * rows, 16), rows), :]
    return view.at[pl.ds(R_OUT, OUT_ROWS), pl.ds(pl.multiple_of(j * OUT_COLS, LANE), OUT_COLS)]


HBM_SPEC = pl.BlockSpec(memory_space=pltpu.HBM)
SEM_SPEC = pl.BlockSpec(memory_space=pltpu.SEMAPHORE)


def _half_window(ref, r, j, h):
    if r < len(ROW_REGIONS):
        off, rows = ROW_REGIONS[r]
        return ref.at[pl.ds(pl.multiple_of(off + j * rows + h * (rows // 2), 16), rows // 2), :]
    half = OUT_ROWS // 2
    return ref.at[pl.ds(pl.multiple_of(R_OUT + h * half, 16), half),
                  pl.ds(pl.multiple_of(j * OUT_COLS, LANE), OUT_COLS)]


def _other_chips():
    x, y, _ = _coords()
    return [(1 - x, y), (x, 1 - y), (1 - x, 1 - y)]


def _ici_copies(srcs, arena_ref, send_sems, recv_sems, regions):
    x, y, c = _coords()
    sends, arrivals = [], []
    for k, (cx, cy) in enumerate(_other_chips()):
        for r in regions:
            def remote(src, j):
                return pltpu.make_async_remote_copy(
                    src_ref=src, dst_ref=_half_window(arena_ref, r, j, c), send_sem=send_sems.at[3 * r + k],
                    recv_sem=recv_sems.at[3 * r + k], device_id=(cx, cy, c), device_id_type=MESH)
            rows = srcs[r].shape[0] // 2
            sends.append(remote(srcs[r].at[pl.ds(pl.multiple_of(c * rows, 16), rows), :], 2 * x + y))
            arrivals.append(remote(_half_window(arena_ref, r, 2 * cx + cy, c), 2 * cx + cy))
    return sends, arrivals


def _sibling_copies(srcs, arena_ref, send_sems, recv_sems, regions):
    x, y, c = _coords()
    sends, arrivals = [], []

    def remote(win, r, k, src=None):
        return pltpu.make_async_remote_copy(
            src_ref=win if src is None else src, dst_ref=win, send_sem=send_sems.at[r, k],
            recv_sem=recv_sems.at[r, k], device_id=(x, y, 1 - c), device_id_type=MESH)

    for k, (cx, cy) in enumerate(_other_chips()):
        for r in regions:
            sends.append(remote(_half_window(arena_ref, r, 2 * cx + cy, c), r, k))
            arrivals.append(remote(_half_window(arena_ref, r, 2 * cx + cy, 1 - c), r, k))
    for r in regions:
        own = _chip_window(arena_ref, None, r, 2 * x + y)
        sends.append(remote(own, r, 3, src=srcs[r]))
        arrivals.append(remote(own, r, 3))
    return sends, arrivals


ICI_SEMS = pltpu.SemaphoreType.DMA((3 * N_REG,))
SIBLING_SEMS = pltpu.SemaphoreType.DMA((N_REG, 4))
ARENA_SHAPE = (ARENA_ROWS, ARENA_W)
ALL_REGIONS = tuple(range(N_REG))
IN_REGION = (3,)
REST_REGIONS = (0, 1, 2, 4, 5)


def _gather_layer(shards, name, regions=ALL_REGIONS):
    def body(*refs):
        srcs, arena_ref = refs[:N_REG], refs[N_REG]
        ici_send, ici_recv, sib_send, sib_recv = refs[N_REG + 1:]
        sends, arrivals = _ici_copies(srcs, arena_ref, ici_send, ici_recv, regions)
        passes, landings = _sibling_copies(srcs, arena_ref, sib_send, sib_recv, regions)
        for cp in sends + passes[len(arrivals):]:
            cp.start()
        for arrival, onward in zip(arrivals, passes):
            arrival.wait_recv()
            onward.start()
        for cp in landings:
            cp.wait_recv()
        for cp in sends + passes:
            cp.wait_send()

    return pl.pallas_call(
        body, name=name, out_shape=jax.ShapeDtypeStruct(ARENA_SHAPE, BF16),
        in_specs=[pl.BlockSpec(memory_space=pl.ANY)] * N_REG, out_specs=pl.BlockSpec(memory_space=pl.ANY),
        scratch_shapes=[ICI_SEMS, ICI_SEMS, SIBLING_SEMS, SIBLING_SEMS],
    )(*shards)


def _gather_start(shards, after, name, regions=ALL_REGIONS):
    def body(*refs):
        srcs, arena_ref = refs[:N_REG], refs[N_REG]
        send_sems, recv_sems = refs[N_REG + 2], refs[N_REG + 3]
        token = refs[-1]
        for cp in _ici_copies(srcs, arena_ref, send_sems, recv_sems, regions)[0]:
            cp.start()
        token[...] = jnp.zeros_like(token)

    hbm = lambda a: pltpu.with_memory_space_constraint(a, pltpu.HBM)
    outs = pl.pallas_call(
        body, name=name,
        out_shape=(ICI_SEMS, ICI_SEMS, *[pltpu.HBM(s.shape, s.dtype) for s in shards],
                   pltpu.HBM(ARENA_SHAPE, BF16), pltpu.HBM(after.shape, after.dtype),
                   jax.ShapeDtypeStruct((8, LANE), F32)),
        in_specs=[HBM_SPEC] * (N_REG + 2),
        out_specs=(SEM_SPEC, SEM_SPEC, *[HBM_SPEC] * (N_REG + 2), pl.BlockSpec(memory_space=pltpu.VMEM)),
        input_output_aliases={i: 2 + i for i in range(N_REG + 2)},
        compiler_params=pltpu.CompilerParams(has_side_effects=pltpu.SideEffectType.DATAFLOW_SIDE_EFFECTING),
    )(*[hbm(s) for s in shards], hbm(lax.empty(ARENA_SHAPE, BF16)), hbm(after))
    return outs[0], outs[1], outs[2:2 + N_REG], outs[2 + N_REG], outs[-1], outs[3 + N_REG]


def _gather_wait(send_sems, recv_sems, shards, arena, after, name, regions=ALL_REGIONS):
    def body(*refs):
        srcs, arena_ref = refs[:N_REG], refs[N_REG]
        sends, arrivals = _ici_copies(srcs, arena_ref, refs[N_REG + 1], refs[N_REG + 2], regions)
        for cp in sends:
            cp.wait_send()
        for cp in arrivals:
            cp.wait_recv()

    outs = pl.pallas_call(
        body, name=name,
        out_shape=(*[pltpu.HBM(s.shape, s.dtype) for s in shards], pltpu.HBM(ARENA_SHAPE, BF16)),
        in_specs=[HBM_SPEC] * (N_REG + 1) + [SEM_SPEC, SEM_SPEC, pl.BlockSpec(memory_space=pl.ANY)],
        out_specs=(HBM_SPEC,) * (N_REG + 1), input_output_aliases={i: i for i in range(N_REG + 1)},
        compiler_params=pltpu.CompilerParams(has_side_effects=pltpu.SideEffectType.DATAFLOW_SIDE_EFFECTING),
    )(*shards, arena, send_sems, recv_sems, after)
    return outs[:N_REG], outs[N_REG]


def _gather_finish(shards, arena, name, regions=ALL_REGIONS):
    def body(*refs):
        srcs, arena_ref = refs[:N_REG], refs[N_REG + 1]
        sends, arrivals = _sibling_copies(srcs, arena_ref, refs[N_REG + 2], refs[N_REG + 3], regions)
        for cp in sends:
            cp.start()
        for cp in arrivals:
            cp.wait_recv()
        for cp in sends:
            cp.wait_send()

    return pl.pallas_call(
        body, name=name, out_shape=jax.ShapeDtypeStruct(ARENA_SHAPE, BF16),
        in_specs=[pl.BlockSpec(memory_space=pl.ANY)] * (N_REG + 1), out_specs=pl.BlockSpec(memory_space=pl.ANY),
        scratch_shapes=[SIBLING_SEMS, SIBLING_SEMS], input_output_aliases={N_REG: 0},
    )(*shards, arena)


HALF_PIECE_OFF = tuple(o // 2 for o in PIECE_OFF)
HALF_PIECE_ROWS = PIECE_ROWS // 2
HALF_OUT_ROWS = OUT_ROWS // 2
SWAP_SEMS = pltpu.SemaphoreType.DMA((4 * N_REG,))
SCATTER_SEMS = pltpu.SemaphoreType.DMA((6,))


def _packed_shapes(slots, dtype):
    return (jax.ShapeDtypeStruct((slots, HALF_PIECE_ROWS, ARENA_W), dtype),
            jax.ShapeDtypeStruct((slots, HALF_OUT_ROWS, OUT_COLS), dtype))


def _swap_halves(ga, name):
    def body(g_ref, main_ref, outp_ref, send_sems, recv_sems):
        x, y, c = _coords()
        cps = []
        for j in range(4):
            for r in range(N_REG):
                if r < len(ROW_REGIONS):
                    dst = main_ref.at[j, pl.ds(HALF_PIECE_OFF[r], ROW_REGIONS[r][1] // 2), :]
                else:
                    dst = outp_ref.at[j]
                cps.append(pltpu.make_async_remote_copy(
                    src_ref=_half_window(g_ref, r, j, 1 - c), dst_ref=dst, send_sem=send_sems.at[j * N_REG + r],
                    recv_sem=recv_sems.at[j * N_REG + r], device_id=(x, y, 1 - c), device_id_type=MESH))
        for cp in cps:
            cp.start()
        for cp in cps:
            cp.wait()

    return pl.pallas_call(
        body, name=name, out_shape=_packed_shapes(4, ga.dtype),
        in_specs=[pl.BlockSpec(memory_space=pl.ANY)], out_specs=(pl.BlockSpec(memory_space=pl.ANY),) * 2,
        scratch_shapes=[SWAP_SEMS, SWAP_SEMS],
    )(ga)


def _own_halves(ga, cc):
    mains = [jnp.concatenate([lax.dynamic_slice(ga, (off + j * rows + cc * (rows // 2), 0), (rows // 2, ARENA_W))
                              for off, rows in ROW_REGIONS]) for j in range(4)]
    outs = [lax.dynamic_slice(ga, (R_OUT + cc * HALF_OUT_ROWS, j * OUT_COLS), (HALF_OUT_ROWS, OUT_COLS))
            for j in range(4)]
    return jnp.stack(mains), jnp.stack(outs)


def _scatter_copies(main_ref, outp_ref, rmain_ref, routp_ref, send_sems, recv_sems):
    _, _, c = _coords()
    cps = []
    for k, (cx, cy) in enumerate(_other_chips()):
        for i, (src, dst) in enumerate(((main_ref, rmain_ref), (outp_ref, routp_ref))):
            cps.append(pltpu.make_async_remote_copy(
                src_ref=src.at[2 * cx + cy], dst_ref=dst.at[k], send_sem=send_sems.at[2 * k + i],
                recv_sem=recv_sems.at[2 * k + i], device_id=(cx, cy, c), device_id_type=MESH))
    return cps


def _scatter_start(main, outp, name):
    def body(main_ref, outp_ref, rmain_ref, routp_ref, send_sems, recv_sems, *rest):
        for cp in _scatter_copies(main_ref, outp_ref, rmain_ref, routp_ref, send_sems, recv_sems):
            cp.start()
        rest[-1][...] = jnp.zeros_like(rest[-1])

    hbm = lambda a: pltpu.with_memory_space_constraint(a, pltpu.HBM)
    land = [lax.empty(s.shape, s.dtype) for s in _packed_shapes(3, main.dtype)]
    bufs = [main, outp, *land]
    outs = pl.pallas_call(
        body, name=name,
        out_shape=(SCATTER_SEMS, SCATTER_SEMS, *[pltpu.HBM(b.shape, b.dtype) for b in bufs],
                   jax.ShapeDtypeStruct((8, LANE), F32)),
        in_specs=[HBM_SPEC] * 4, out_specs=(SEM_SPEC, SEM_SPEC, *[HBM_SPEC] * 4, pl.BlockSpec(memory_space=pltpu.VMEM)),
        input_output_aliases={i: 2 + i for i in range(4)},
        compiler_params=pltpu.CompilerParams(has_side_effects=pltpu.SideEffectType.DATAFLOW_SIDE_EFFECTING),
    )(*[hbm(b) for b in bufs])
    return outs[0], outs[1], outs[2:6], outs[6]


def _scatter_wait(send_sems, recv_sems, bufs, after, name):
    def body(main_ref, outp_ref, rmain_ref, routp_ref, send_sems, recv_sems, *rest):
        for cp in _scatter_copies(main_ref, outp_ref, rmain_ref, routp_ref, send_sems, recv_sems):
            cp.wait_send()
            cp.wait_recv()

    return pl.pallas_call(
        body, name=name, out_shape=tuple(pltpu.HBM(b.shape, b.dtype) for b in bufs),
        in_specs=[HBM_SPEC] * 4 + [SEM_SPEC, SEM_SPEC, pl.BlockSpec(memory_space=pl.ANY)],
        out_specs=(HBM_SPEC,) * 4, input_output_aliases={i: i for i in range(4)},
        compiler_params=pltpu.CompilerParams(has_side_effects=pltpu.SideEffectType.DATAFLOW_SIDE_EFFECTING),
    )(*bufs, send_sems, recv_sems, after)


def _swap_many(arrs, name):
    n = len(arrs)

    def body(*refs):
        x, y, c = _coords()
        send_sems, recv_sems = refs[2 * n], refs[2 * n + 1]
        cps = [pltpu.make_async_remote_copy(
            src_ref=refs[i], dst_ref=refs[n + i], send_sem=send_sems.at[i], recv_sem=recv_sems.at[i],
            device_id=(x, y, 1 - c), device_id_type=MESH) for i in range(n)]
        for cp in cps:
            cp.start()
        for cp in cps:
            cp.wait()

    return pl.pallas_call(
        body, name=name, out_shape=tuple(jax.ShapeDtypeStruct(a.shape, a.dtype) for a in arrs),
        in_specs=[pl.BlockSpec(memory_space=pl.ANY)] * n, out_specs=(pl.BlockSpec(memory_space=pl.ANY),) * n,
        scratch_shapes=[pltpu.SemaphoreType.DMA((n,)), pltpu.SemaphoreType.DMA((n,))],
    )(*arrs)


def _join_halves(mine, theirs, cc):
    return jnp.where(cc == 0, jnp.concatenate([mine, theirs]), jnp.concatenate([theirs, mine]))


def _reduced_layer(red, sib, cc):
    parts = []
    for (_, rows), off in zip(ROW_REGIONS, HALF_PIECE_OFF):
        parts.append(_join_halves(red[0][off:off + rows // 2], sib[0][off:off + rows // 2], cc))
    return jnp.concatenate(parts), _join_halves(red[1], sib[1], cc)


OUT_NAMES = ("w_a_out", "w_b_out", "w_c_out", "w_d_out")


def _arena_shards(w):
    t = lambda a: a.astype(BF16).transpose(0, 2, 1)
    return (w["w_ffn_down"].astype(BF16), t(w["w_ffn_gate"]), t(w["w_ffn_up"]), t(w["w_in"]), w["w_o"].astype(BF16),
            jnp.concatenate([w[n].astype(BF16) for n in OUT_NAMES], axis=1))


def _shard_grads(main, outp):
    t = lambda r: main[PIECE_OFF[r]:PIECE_OFF[r] + ROW_REGIONS[r][1]]
    g = dict(w_ffn_down=t(0), w_ffn_gate=t(1).T, w_ffn_up=t(2).T, w_in=t(3).T, w_o=t(4))
    for i, n in enumerate(OUT_NAMES):
        g[n] = outp[i * BW:(i + 1) * BW]
    return g


def _gather_taps(p, name):
    mine = jnp.concatenate([p[n] for n in CONV_NAMES], axis=1).reshape(DEPTH * N_TAPS, LANE)
    rows = -(-mine.shape[0] // 8) * 8
    mine = jnp.concatenate([mine, jnp.zeros((rows - mine.shape[0], LANE), F32)])
    g = _allgather8(jnp.stack([mine, mine]), name, pltpu.VMEM)[0::2, :DEPTH * N_TAPS]
    full = g.reshape(4, DEPTH, N_TAPS, LANE).transpose(1, 2, 0, 3).reshape(DEPTH, N_TAPS, BW)
    return dict(conv_a_w=full[:, :CONV_A], conv_b_w=full[:, CONV_A:CONV_A + CONV_B], conv_d_w=full[:, CONV_A + CONV_B:])


def _flat_pack(arrs):
    flat = jnp.concatenate([a.reshape(-1).astype(F32) for a in arrs])
    rows = -(-flat.shape[0] // (8 * LANE)) * 8
    return jnp.concatenate([flat, jnp.zeros((rows * LANE - flat.shape[0],), F32)]).reshape(rows, LANE)


def _flat_unpack(packed, shapes):
    flat, out, off = packed.reshape(-1), [], 0
    for s in shapes:
        cnt = int(np.prod(s))
        out.append(flat[off:off + cnt].reshape(s))
        off += cnt
    return out


def _blockdiag_chunks(w):
    w4 = w.reshape(4, 2, 64, 64)
    z = jnp.zeros((4, 2, 64, 2, 64), F32)
    z = z.at[:, 0, :, 0, :].set(w4[:, 0]).at[:, 1, :, 1, :].set(w4[:, 1])
    return z.reshape(4, LANE, LANE)


def _blockdiag_extract(d):
    d5 = d.reshape(4, 2, 64, 2, 64)
    return jnp.stack([d5[:, 0, :, 0, :], d5[:, 1, :, 1, :]], axis=1).reshape(8, 64, 64)


SLOPES = np.asarray([2.0 ** (-8.0 * (i + 1) / N_Q) for i in range(N_Q)], np.float32)


def _layer_consts(p, fw, l):
    row = lambda a: a[l].reshape(1, -1)
    return dict(
        g1=row(p["norm1_g"]), g2=row(p["norm2_g"]), wA=fw["conv_a_w"][l], bA=row(p["conv_a_b"]),
        wx=_blockdiag_chunks(p["lru_wx"][l]), bx=row(p["lru_bx"]), wa=_blockdiag_chunks(p["lru_wa"][l]),
        ba=row(p["lru_ba"]), lam=row(p["lru_lambda"]), wB=fw["conv_b_w"][l],
        ss=jnp.stack([p["sinks"][l], jnp.asarray(SLOPES)]), wD=fw["conv_d_w"][l], bD=row(p["conv_d_b"]),
        lg=row(p["ln_d_g"]), lb=row(p["ln_d_b"]))


def _layer_fwd(x, c, fw, l, rest_of_weights=None):
    t = f"l{l}_"
    xn = _rms_fwd(x, c["g1"], t + "rms1")
    wt = lambda off, rows: Win(fw["arena"][l], None, off, rows)
    proj = _mm(xn, Win(fw["arena_in"][l], None, R_IN, IN_W), "nt", t + "proj")
    ya, lru_h = _a_fwd(proj, c["wA"], c["bA"], c["wx"], c["bx"], c["wa"], c["ba"], c["lam"], t + "a_fwd")
    yb = _b_fwd(proj, c["wB"], t + "b_fwd")
    q3 = _heads(proj[:, OFF_Q:OFF_K], N_Q)
    k3 = _heads(proj[:, OFF_K:OFF_V], N_KV)
    v3 = _heads(proj[:, OFF_V:OFF_V + N_KV * HEAD_DIM], N_KV)
    yc = _unheads(_attn_fwd(q3, k3, v3, c["ss"], t + "attn_fwd"))
    cd = _d_conv_fwd(proj, c["wD"], c["bD"], t + "d_conv_fwd")
    yd = _ln_silu_fwd(cd, c["lg"], c["lb"], t + "d_ln_fwd")
    ys = (ya, yb, yc, yd)
    if fw["arena"][l] is None:
        fw["arena"][l] = rest_of_weights(yd)
    big_y = tuple(_mm(y, wt(R_OUT + i * BW, BW), "nn", t + f"out{i}") for i, y in enumerate(ys))
    merged = _merge_fwd(proj, big_y, t + "merge_fwd")
    hres = _mm(merged, wt(R_O, D_MODEL), "nn", t + "wo", add=x)
    hn = _rms_fwd(hres, c["g2"], t + "rms2")
    gg = _mm(hn, wt(R_GATE, D_FF), "nt", t + "ffn_gate")
    uu = _mm(hn, wt(R_UP, D_FF), "nt", t + "ffn_up")
    act = _swiglu_fwd(gg, uu, t + "swiglu_fwd")
    xout = _mm(act, wt(R_DOWN, D_FF), "nn", t + "ffn_down", add=hres)
    saved = dict(x=x, xn=xn, proj=proj, ys=ys, q3=q3, k3=k3, v3=v3, cd=cd, big_y=big_y, merged=merged, hres=hres,
                 hn=hn, gg=gg, uu=uu, act=act, lru_h=lru_h)
    return xout, saved


def _layer_bwd(dxout, s, c, fw, l, ga, weight_grads_done=None):
    t = f"l{l}_"
    gs = {}
    wt = lambda off, rows: Win(fw["arena"][l], None, off, rows)
    gt = lambda off, rows: Win(ga, None, off, rows)
    dact = _mm(dxout, wt(R_DOWN, D_FF), "nt", t + "d_act")
    ga = _mm(s["act"], dxout, "tn", t + "dw_down", out=gt(R_DOWN, D_FF))
    dgg, duu = _swiglu_bwd(s["gg"], s["uu"], dact, t + "swiglu_bwd")
    ga = _mm(dgg, s["hn"], "tn", t + "dw_gate", out=gt(R_GATE, D_FF))
    ga = _mm(duu, s["hn"], "tn", t + "dw_up", out=gt(R_UP, D_FF))
    dhn = _mm(dgg, wt(R_GATE, D_FF), "nn", t + "d_hn_g")
    dhn = _mm(duu, wt(R_UP, D_FF), "nn", t + "d_hn_u", add=dhn)
    dhres, gs["norm2_g"] = _rms_bwd(s["hres"], c["g2"], dhn, dxout, t + "rms2_bwd")
    dmerged = _mm(dhres, wt(R_O, D_MODEL), "nt", t + "d_merged")
    ga = _mm(s["merged"], dhres, "tn", t + "dw_o", out=gt(R_O, D_MODEL))
    dbig_y, dgl = _merge_bwd(s["proj"], s["big_y"], dmerged, t + "merge_bwd")
    dys = []
    for i in range(4):
        ga = _mm(s["ys"][i], dbig_y[i], "tn", t + f"dw_out{i}", out=gt(R_OUT + i * BW, BW))
        dys.append(_mm(dbig_y[i], wt(R_OUT + i * BW, BW), "nt", t + f"d_y{i}"))
    proj = s["proj"]
    (dax, dag, gs["conv_a_w"], gs["conv_a_b"], dwx, gs["lru_bx"], dwa, gs["lru_ba"], gs["lru_lambda"]) = _a_bwd(
        proj, s["lru_h"], dys[0], c["wA"], c["bA"], c["wx"], c["bx"], c["wa"], c["ba"], c["lam"], t + "a_bwd")
    gs["lru_wx"] = _blockdiag_extract(dwx)
    gs["lru_wa"] = _blockdiag_extract(dwa)
    dbv, dbc, dbb, gs["conv_b_w"] = _b_bwd(proj, dys[1], c["wB"], t + "b_bwd")
    dq3, dk3, dv3, dsink = _attn_bwd(s["q3"], s["k3"], s["v3"], _heads(dys[2], N_Q), c["ss"], t + "attn_bwd")
    gs["sinks"] = dsink[:, 0, 0]
    dcd, gs["ln_d_g"], gs["ln_d_b"] = _ln_silu_bwd(s["cd"], c["lg"], c["lb"], dys[3], t + "d_ln_bwd")
    dd1, dd2, gs["conv_d_w"], gs["conv_d_b"] = _d_conv_bwd(proj, dcd, c["wD"], t + "d_conv_bwd")
    dproj = jnp.concatenate(
        [dax, dag, dbv, dbc, dbb, _unheads(dq3), _unheads(dk3).astype(BF16), _unheads(dv3).astype(BF16), dd1, dd2,
         *dgl], axis=1)
    ga = _mm(dproj, s["xn"], "tn", t + "dw_in", out=gt(R_IN, IN_W))
    token = weight_grads_done(ga) if weight_grads_done is not None else None
    dxn = _mm(dproj, Win(fw["arena_in"][l], None, R_IN, IN_W), "nn", t + "d_xn", after=token)
    dx, gs["norm1_g"] = _rms_bwd(s["x"], c["g1"], dxn, dhres, t + "rms1_bwd")
    return dx, ga, gs


def kernel(x, norm1_g, w_in, conv_a_w, conv_a_b, lru_wx, lru_bx, lru_wa, lru_ba, lru_lambda, w_a_out, conv_b_w, w_b_out, sinks, w_c_out, conv_d_w, conv_d_b, ln_d_g, ln_d_b, w_d_out, w_o, norm2_g, w_ffn_gate, w_ffn_up, w_ffn_down, final_g, loss_target, m_norm1_g, m_w_in, m_conv_a_w, m_conv_a_b, m_lru_wx, m_lru_bx, m_lru_wa, m_lru_ba, m_lru_lambda, m_w_a_out, m_conv_b_w, m_w_b_out, m_sinks, m_w_c_out, m_conv_d_w, m_conv_d_b, m_ln_d_g, m_ln_d_b, m_w_d_out, m_w_o, m_norm2_g, m_w_ffn_gate, m_w_ffn_up, m_w_ffn_down, m_final_g, v_norm1_g, v_w_in, v_conv_a_w, v_conv_a_b, v_lru_wx, v_lru_bx, v_lru_wa, v_lru_ba, v_lru_lambda, v_w_a_out, v_conv_b_w, v_w_b_out, v_sinks, v_w_c_out, v_conv_d_w, v_conv_d_b, v_ln_d_g, v_ln_d_b, v_w_d_out, v_w_o, v_norm2_g, v_w_ffn_gate, v_w_ffn_up, v_w_ffn_down, v_final_g):
    given = dict(locals())
    p = {n: given[n] for n in NAMES}
    mom = {n: given["m_" + n] for n in NAMES}
    var = {n: given["v_" + n] for n in NAMES}
    cx, cy, cc = _coords()
    chip = 2 * cx + cy

    shards = _arena_shards(p)
    fw = _gather_taps(p, "gather_taps")
    shards0, shards1 = [s[0] for s in shards], [s[1] for s in shards]
    flight0 = _gather_start(shards0, _gather_layer(shards0, "gather_l0_in", IN_REGION), "gather_l0_rest_start",
                            REST_REGIONS)
    fw["arena_in"] = [flight0[5], None]
    fw["arena"] = [None, None]
    consts = [_layer_consts(p, fw, l) for l in range(DEPTH)]
    consts[0]["g1"] = consts[0]["g1"] + flight0[4][0:1, 0:1]
    flight1 = []

    def rest_of_layer0(after):
        sh, landing = _gather_wait(*flight0[:4], after, "gather_l0_rest_wait", REST_REGIONS)
        arena = _gather_finish(sh, landing, "gather_l0_rest_finish", REST_REGIONS)
        flight1.extend(_gather_start(shards1, arena, "gather_l1_start"))
        return flight1[5]

    h = x[0]
    saved = []
    for l in range(DEPTH):
        if l == 1:
            sh, landing = _gather_wait(*flight1[:4], h, "gather_l1_wait")
            fw["arena"][1] = fw["arena_in"][1] = _gather_finish(sh, landing, "gather_l1_finish")
        h, s = _layer_fwd(h, consts[l], fw, l, rest_of_layer0)
        saved.append(s)
    loss_vec, dh, g_final = _loss_head(h, final_g.reshape(1, -1), loss_target[0], "loss_head")
    loss = lax.psum(loss_vec[0, 0], ("x", "y", "c"))

    zero = jnp.zeros((1,), jnp.int32)
    chip_sel = chip.reshape(1).astype(jnp.int32)

    def chip_sums(ga, t):
        own, got = _own_halves(ga, cc), _swap_halves(ga, t + "grads_swap_halves")
        return [_sum_own_plus(o.reshape((1, -1, o.shape[-1])), zero, r.reshape((1, -1, r.shape[-1])),
                              t + f"grads_sum_chip{i}", BF16).reshape(o.shape) for i, (o, r) in enumerate(zip(own, got))]

    def all_sums(sums, got, t):
        return [_sum_own_plus(s, chip_sel, r, t + f"grads_sum_all{i}", F32) for i, (s, r) in enumerate(zip(sums, got))]

    gss = [None] * DEPTH
    dh, ga1, gss[1] = _layer_bwd(dh, saved[1], consts[1], fw, 1, lax.empty(ARENA_SHAPE, BF16))
    send_sems, recv_sems, bufs, token = _scatter_start(*chip_sums(ga1, "l1_"), "l1_grads_scatter_start")
    scatter0 = []

    def start_layer0_scatter(ga0):
        scatter0.extend(_scatter_start(*chip_sums(ga0, "l0_"), "l0_grads_scatter_start"))
        return scatter0[3]

    dh, _, gss[0] = _layer_bwd(dh + token[0:1, 0:1], saved[0], consts[0], fw, 0, lax.empty(ARENA_SHAPE, BF16),
                               start_layer0_scatter)
    grad_x = dh[None]

    flat = lambda a: a.reshape(-1, a.shape[-1])

    def finish_layer(bufs_l, l, carried, t):
        red = all_sums(bufs_l[:2], bufs_l[2:], t)
        g_l = _shard_grads(*_reduced_layer(red, _swap_many(red, t + "grads_swap_reduced"), cc))
        return {n: _adamw_layer(flat(p[n]), flat(mom[n]), flat(var[n]), g_l[n], l, carried and carried[n],
                                t + "adamw_" + n) for n in BIG}

    big = finish_layer(_scatter_wait(send_sems, recv_sems, bufs, scatter0[3], "l1_grads_scatter_wait"), 1, None, "l1_")
    done1 = big["w_in"][4][0:1, 0:1]

    small_full = {n: (g_final.reshape(-1) if n == "final_g" else
                      jnp.stack([gss[l][n].reshape(gss[l][n].shape[-2:] if n.startswith("conv") and n.endswith("_w")
                                                   else p[n].shape[1:]) for l in range(DEPTH)]))
                  for n in SMALL}
    part = _flat_pack([small_full[n] for n in SMALL]) + done1
    rows = part.shape[0]
    gathered = _allgather8(jnp.stack([part, part]), "gather_small_grads", pltpu.VMEM)
    small_packed = _sum_leading(gathered, "small_grads_sum")
    small_sum = _flat_unpack(small_packed, [small_full[n].shape for n in SMALL])

    big = finish_layer(_scatter_wait(*scatter0[:3], small_packed, "l0_grads_scatter_wait"), 0, big, "l0_")
    g, delta, new_m, new_v = ({n: big[n][i].reshape(p[n].shape) for n in BIG} for i in range(4))
    for n, a in zip(SMALL, small_sum):
        g[n] = lax.dynamic_slice_in_dim(a, chip * LANE, LANE, axis=2) if n in CONV_NAMES else a

    shapes = [p[n].shape for n in SMALL]
    d, nm, nv = _adamw(_flat_pack([p[n] for n in SMALL]), _flat_pack([g[n] for n in SMALL]),
                       _flat_pack([mom[n] for n in SMALL]), _flat_pack([var[n] for n in SMALL]), "adamw_small")
    for n, a, b, cval in zip(SMALL, _flat_unpack(d, shapes), _flat_unpack(nm, shapes), _flat_unpack(nv, shapes)):
        delta[n], new_m[n], new_v[n] = a, b, cval

    return (loss, grad_x, *[g[n] for n in NAMES], *[delta[n] for n in NAMES], *[new_m[n] for n in NAMES],
            *[new_v[n] for n in NAMES])
```

```python
import functools
import math

import numpy as np
import jax
import jax.numpy as jnp
from jax import lax
from jax.experimental import pallas as pl
from jax.experimental.pallas import tpu as pltpu

F32 = jnp.float32
BF16 = jnp.bfloat16
MESH = pl.DeviceIdType.MESH

D_MODEL = 1024
DEPTH = 2
BW = 512
HEAD_DIM = 64
N_Q = 8
N_KV = 2
BLK = 128
D_FF = 2816
IN_W = 8448
EPS = 1e-6
NEG_INF = -1e30
LRU_C = 8.0
CONV_A, CONV_B, CONV_D = 4, 3, 31
LANE = 128
ROW_TILE = 256
VMEM_LIMIT = 56 * 1024 * 1024
MM_VMEM_BUDGET = 36 * 1024 * 1024

C_AX, C_AG, C_BV, C_BC, C_BB = 0, 4, 8, 12, 16
OFF_Q, OFF_K, OFF_V = 2560, 3072, 3200
C_D1, C_D2 = 26, 30
OFF_GL = 4352

ADAM_LR, ADAM_B1, ADAM_B2, ADAM_EPS, ADAM_WD, ADAM_STEP = 0.001, 0.9, 0.999, 1e-08, 0.01, 10

ARENA_W = 1024
R_DOWN, R_GATE, R_UP, R_IN, R_O, R_OUT = 0, 2816, 5632, 8448, 16896, 17920
ARENA_ROWS = 19968
ROW_REGIONS = ((R_DOWN, 704), (R_GATE, 704), (R_UP, 704), (R_IN, 2112), (R_O, 256))
PIECE_OFF = (0, 704, 1408, 2112, 4224)
PIECE_ROWS = 4480
OUT_ROWS, OUT_COLS = 4 * BW, D_MODEL // 4

BIG = ("w_in", "w_a_out", "w_b_out", "w_c_out", "w_d_out", "w_o", "w_ffn_gate", "w_ffn_up", "w_ffn_down")
CONV_NAMES = ("conv_a_w", "conv_b_w", "conv_d_w")
N_TAPS = CONV_A + CONV_B + CONV_D
SMALL = ("norm1_g", "conv_a_w", "conv_a_b", "lru_wx", "lru_bx", "lru_wa", "lru_ba", "lru_lambda", "conv_b_w",
         "sinks", "conv_d_w", "conv_d_b", "ln_d_g", "ln_d_b", "norm2_g", "final_g")
NAMES = ['norm1_g', 'w_in', 'conv_a_w', 'conv_a_b', 'lru_wx', 'lru_bx', 'lru_wa', 'lru_ba', 'lru_lambda', 'w_a_out',
         'conv_b_w', 'w_b_out', 'sinks', 'w_c_out', 'conv_d_w', 'conv_d_b', 'ln_d_g', 'ln_d_b', 'w_d_out', 'w_o',
         'norm2_g', 'w_ffn_gate', 'w_ffn_up', 'w_ffn_down', 'final_g']


def _pick(n, cands, off=0):
    for c in cands:
        if n % c == 0 and off % c == 0:
            return c
    assert off == 0, (n, off)
    return n


class Win:
    def __init__(self, arena, l, off, rows):
        self.arena, self.l, self.off, self.rows = arena, l, off, rows
        self.shape = (rows, arena.shape[-1])


def _params(sem=None):
    return pltpu.CompilerParams(dimension_semantics=sem, vmem_limit_bytes=VMEM_LIMIT)


def _sig(z):
    return 1.0 / (1.0 + jnp.exp(-z))


def _dot(a, b, dims):
    return lax.dot_general(a.astype(BF16), b.astype(BF16), (dims, ((), ())), preferred_element_type=F32)


NN = ((1,), (0,))
NT = ((1,), (1,))
TN = ((0,), (0,))


def _mm(a, b, mode, name, out_dtype=F32, add=None, out=None, after=None):
    if mode == "nn":
        (m, k), n = a.shape, b.shape[1]
    elif mode == "nt":
        (m, k), n = a.shape, b.shape[0]
    else:
        (k, m), n = a.shape, b.shape[1]
    b_win = isinstance(b, Win)
    b_off = b.off if b_win else 0
    o_off = out.off if out is not None else 0
    if out is not None:
        out_dtype = out.arena.dtype
    tk = _pick(k, (2816, 2048, 1408, 1024, 768, 512, 256), b_off if mode != "nt" else 0)
    nk = k // tk
    n_off = b_off if mode == "nt" else 0
    a_bytes, b_bytes, o_bytes = a.dtype.itemsize, 2, jnp.dtype(out_dtype).itemsize

    def vmem_bytes(tm_, tn_):
        tile = tm_ * tn_
        return (2 * tk * (tm_ * a_bytes + tn_ * b_bytes) + 2 * tile * o_bytes + (tile * 4 if nk > 1 else 0)
                + (2 * tile * 4 if add is not None else 0) + tile * 4)

    pairs = [(tm_, tn_) for tm_ in (2048, 1024, 768, 512, 256, 128) for tn_ in (1024, 768, 512, 256, 128)
             if m % tm_ == 0 and o_off % tm_ == 0 and n % tn_ == 0 and n_off % tn_ == 0
             and vmem_bytes(tm_, tn_) <= MM_VMEM_BUDGET]
    tm, tn = max(pairs, key=lambda p: (p[0] * p[1], p[0]))
    dims = {"nn": NN, "nt": NT, "tn": TN}[mode]

    def body(*refs):
        a_ref, b_ref = refs[:2]
        c_ref = refs[2] if add is not None else None
        if nk == 1:
            r = _dot(a_ref[...], b_ref[...], dims)
            if add is not None:
                r = r + c_ref[...]
            refs[-1][...] = r.astype(out_dtype)
            return
        o_ref, acc = refs[-2:]
        kk = pl.program_id(2)

        @pl.when(kk == 0)
        def _():
            acc[...] = jnp.zeros_like(acc)

        acc[...] += _dot(a_ref[...], b_ref[...], dims)

        @pl.when(kk == nk - 1)
        def _():
            r = acc[...]
            if add is not None:
                r = r + c_ref[...]
            o_ref[...] = r.astype(out_dtype)

    if mode == "tn":
        a_spec = pl.BlockSpec((tk, tm), lambda i, j, q: (q, i))
    else:
        a_spec = pl.BlockSpec((tm, tk), lambda i, j, q: (i, q))
    if mode == "nt":
        b_blk, b_idx = (tn, tk), (lambda i, j, q: (b_off // tn + j, q))
    else:
        b_blk, b_idx = (tk, tn), (lambda i, j, q: (b_off // tk + q, j))
    if b_win and b.arena.ndim == 3:
        bl = b.l
        b_spec = pl.BlockSpec((None,) + b_blk, lambda i, j, q: (bl,) + b_idx(i, j, q))
    else:
        b_spec = pl.BlockSpec(b_blk, b_idx)
    plain_o = pl.BlockSpec((tm, tn), lambda i, j, q: (i, j))
    in_specs = [a_spec, b_spec] + ([plain_o] if add is not None else [])
    args = (a, b.arena if b_win else b) + ((add,) if add is not None else ())
    if after is not None:
        in_specs.append(pl.BlockSpec(after.shape, lambda i, j, q: (0, 0)))
        args = args + (after,)
    aliases = {}
    if out is None:
        o_spec, o_shape = plain_o, jax.ShapeDtypeStruct((m, n), out_dtype)
    else:
        ol = out.l
        if out.arena.ndim == 3:
            o_spec = pl.BlockSpec((None, tm, tn), lambda i, j, q: (ol, o_off // tm + i, j))
        else:
            o_spec = pl.BlockSpec((tm, tn), lambda i, j, q: (o_off // tm + i, j))
        o_shape = jax.ShapeDtypeStruct(out.arena.shape, out_dtype)
        aliases = {len(args): 0}
        in_specs.append(pl.BlockSpec(memory_space=pl.ANY))
        args = args + (out.arena,)
    return pl.pallas_call(
        body, name=name, out_shape=o_shape,
        grid=(m // tm, n // tn, nk), in_specs=in_specs, out_specs=o_spec,
        scratch_shapes=[pltpu.VMEM((tm, tn), F32)] if nk > 1 else [], input_output_aliases=aliases,
        compiler_params=_params(("parallel", "parallel", "arbitrary")),
    )(*args)


def _row_spec(cols, tr=ROW_TILE):
    return pl.BlockSpec((tr, cols), lambda i: (i, 0))


def _vec_spec(cols):
    return pl.BlockSpec((1, cols), lambda i: (0, 0))


def _rms_fwd(x, g, name):
    t, d = x.shape

    def body(x_ref, g_ref, o_ref):
        xv = x_ref[...]
        r = lax.rsqrt(jnp.mean(xv * xv, axis=1, keepdims=True) + EPS)
        o_ref[...] = (xv * r * g_ref[...]).astype(BF16)

    return pl.pallas_call(
        body, name=name, out_shape=jax.ShapeDtypeStruct((t, d), BF16), grid=(t // ROW_TILE,),
        in_specs=[_row_spec(d), _vec_spec(d)], out_specs=_row_spec(d), compiler_params=_params(("parallel",)),
    )(x, g)


def _rms_bwd(x, g, dxn, dres, name):
    t, d = x.shape

    def body(x_ref, g_ref, dy_ref, dr_ref, dx_ref, dg_ref):
        @pl.when(pl.program_id(0) == 0)
        def _():
            dg_ref[...] = jnp.zeros_like(dg_ref)

        xv = x_ref[...]
        dy = dy_ref[...]
        r = lax.rsqrt(jnp.mean(xv * xv, axis=1, keepdims=True) + EPS)
        w = dy * g_ref[...]
        dx_ref[...] = dr_ref[...] + r * w - xv * (r * r * r) * jnp.mean(w * xv, axis=1, keepdims=True)
        dg_ref[...] += jnp.sum(dy * xv * r, axis=0, keepdims=True)

    return pl.pallas_call(
        body, name=name,
        out_shape=(jax.ShapeDtypeStruct((t, d), F32), jax.ShapeDtypeStruct((1, d), F32)), grid=(t // ROW_TILE,),
        in_specs=[_row_spec(d), _vec_spec(d), _row_spec(d), _row_spec(d)], out_specs=(_row_spec(d), _vec_spec(d)),
        compiler_params=_params(("arbitrary",)),
    )(x, g, dxn, dres)


def _loss_head(x, g, tgt, name):
    t, d = x.shape

    def body(x_ref, g_ref, t_ref, loss_ref, dx_ref, dg_ref):
        @pl.when(pl.program_id(0) == 0)
        def _():
            dg_ref[...] = jnp.zeros_like(dg_ref)
            loss_ref[...] = jnp.zeros_like(loss_ref)

        xv = x_ref[...]
        gv = g_ref[...]
        r = lax.rsqrt(jnp.mean(xv * xv, axis=1, keepdims=True) + EPS)
        e = xv * r * gv - t_ref[...]
        loss_ref[...] += jnp.full(loss_ref.shape, (0.5 / d) * jnp.sum(e * e), F32)
        dy = e * (1.0 / d)
        w = dy * gv
        dx_ref[...] = r * w - xv * (r * r * r) * jnp.mean(w * xv, axis=1, keepdims=True)
        dg_ref[...] += jnp.sum(dy * xv * r, axis=0, keepdims=True)

    return pl.pallas_call(
        body, name=name,
        out_shape=(jax.ShapeDtypeStruct((1, LANE), F32), jax.ShapeDtypeStruct((t, d), F32),
                   jax.ShapeDtypeStruct((1, d), F32)),
        grid=(t // ROW_TILE,), in_specs=[_row_spec(d), _vec_spec(d), _row_spec(d)],
        out_specs=(_vec_spec(LANE), _row_spec(d), _vec_spec(d)), compiler_params=_params(("arbitrary",)),
    )(x, g, tgt)


def _swiglu_fwd(gg, uu, name):
    t, f = gg.shape

    def body(g_ref, u_ref, o_ref):
        gv = g_ref[...]
        o_ref[...] = (gv * _sig(gv) * u_ref[...]).astype(BF16)

    return pl.pallas_call(
        body, name=name, out_shape=jax.ShapeDtypeStruct((t, f), BF16), grid=(t // ROW_TILE,),
        in_specs=[_row_spec(f), _row_spec(f)], out_specs=_row_spec(f), compiler_params=_params(("parallel",)),
    )(gg, uu)


def _swiglu_bwd(gg, uu, dact, name):
    t, f = gg.shape

    def body(g_ref, u_ref, d_ref, dg_ref, du_ref):
        gv = g_ref[...]
        dv = d_ref[...]
        s = _sig(gv)
        dg_ref[...] = (dv * u_ref[...] * s * (1.0 + gv * (1.0 - s))).astype(BF16)
        du_ref[...] = (dv * gv * s).astype(BF16)

    return pl.pallas_call(
        body, name=name,
        out_shape=(jax.ShapeDtypeStruct((t, f), BF16), jax.ShapeDtypeStruct((t, f), BF16)), grid=(t // ROW_TILE,),
        in_specs=[_row_spec(f)] * 3, out_specs=(_row_spec(f), _row_spec(f)), compiler_params=_params(("parallel",)),
    )(gg, uu, dact)


MERGE_COLS = 256
MERGE_ROWS = 1024


def _gate_specs(mr):
    nb = D_MODEL // MERGE_COLS
    base = OFF_GL // MERGE_COLS
    return [pl.BlockSpec((mr, MERGE_COLS), functools.partial(lambda i, j, kk: (i, base + nb * kk + j), kk=kk))
            for kk in range(4)]


def _merge_fwd(proj, ys, name):
    t = proj.shape[0]
    mr = min(t, MERGE_ROWS)
    yspec = pl.BlockSpec((mr, MERGE_COLS), lambda i, j: (i, j))

    def body(g0, g1, g2, g3, y0, y1, y2, y3, o_ref):
        acc = _sig(g0[...]) * y0[...]
        acc += _sig(g1[...]) * y1[...]
        acc += _sig(g2[...]) * y2[...]
        acc += _sig(g3[...]) * y3[...]
        o_ref[...] = acc.astype(BF16)

    return pl.pallas_call(
        body, name=name, out_shape=jax.ShapeDtypeStruct((t, D_MODEL), BF16),
        grid=(t // mr, D_MODEL // MERGE_COLS), in_specs=_gate_specs(mr) + [yspec] * 4, out_specs=yspec,
        compiler_params=_params(("parallel", "parallel")),
    )(proj, proj, proj, proj, *ys)


def _merge_bwd(proj, ys, dmerged, name):
    t = proj.shape[0]
    mr = min(t, MERGE_ROWS)
    yspec = pl.BlockSpec((mr, MERGE_COLS), lambda i, j: (i, j))

    def body(g0, g1, g2, g3, y0, y1, y2, y3, dm_ref, *outs):
        dm = dm_ref[...]
        for gr, yr, dy_ref, dg_ref in zip((g0, g1, g2, g3), (y0, y1, y2, y3), outs[:4], outs[4:]):
            s = _sig(gr[...])
            dy_ref[...] = (dm * s).astype(BF16)
            dg_ref[...] = (dm * yr[...] * s * (1.0 - s)).astype(BF16)

    shp = jax.ShapeDtypeStruct((t, D_MODEL), BF16)
    outs = pl.pallas_call(
        body, name=name, out_shape=(shp,) * 8, grid=(t // mr, D_MODEL // MERGE_COLS),
        in_specs=_gate_specs(mr) + [yspec] * 5, out_specs=(yspec,) * 8, compiler_params=_params(("parallel", "parallel")),
    )(proj, proj, proj, proj, *ys, dmerged)
    return outs[:4], outs[4:]


def _ln_silu_fwd(cd, g, b, name):
    t, c = cd.shape

    def body(x_ref, g_ref, b_ref, o_ref):
        xv = x_ref[...]
        mu = jnp.mean(xv, axis=1, keepdims=True)
        xc = xv - mu
        rs = lax.rsqrt(jnp.mean(xc * xc, axis=1, keepdims=True) + EPS)
        z = xc * rs * g_ref[...] + b_ref[...]
        o_ref[...] = (z * _sig(z)).astype(BF16)

    return pl.pallas_call(
        body, name=name, out_shape=jax.ShapeDtypeStruct((t, c), BF16), grid=(t // ROW_TILE,),
        in_specs=[_row_spec(c), _vec_spec(c), _vec_spec(c)], out_specs=_row_spec(c),
        compiler_params=_params(("parallel",)),
    )(cd, g, b)


def _ln_silu_bwd(cd, g, b, dy, name):
    t, c = cd.shape

    def body(x_ref, g_ref, b_ref, dy_ref, dx_ref, dg_ref, db_ref):
        @pl.when(pl.program_id(0) == 0)
        def _():
            dg_ref[...] = jnp.zeros_like(dg_ref)
            db_ref[...] = jnp.zeros_like(db_ref)

        xv = x_ref[...]
        gv = g_ref[...]
        mu = jnp.mean(xv, axis=1, keepdims=True)
        xc = xv - mu
        rs = lax.rsqrt(jnp.mean(xc * xc, axis=1, keepdims=True) + EPS)
        xh = xc * rs
        z = xh * gv + b_ref[...]
        s = _sig(z)
        dz = dy_ref[...] * s * (1.0 + z * (1.0 - s))
        dg_ref[...] += jnp.sum(dz * xh, axis=0, keepdims=True)
        db_ref[...] += jnp.sum(dz, axis=0, keepdims=True)
        dxh = dz * gv
        dx_ref[...] = rs * (dxh - jnp.mean(dxh, axis=1, keepdims=True) - xh * jnp.mean(dxh * xh, axis=1, keepdims=True))

    return pl.pallas_call(
        body, name=name,
        out_shape=(jax.ShapeDtypeStruct((t, c), F32), jax.ShapeDtypeStruct((1, c), F32),
                   jax.ShapeDtypeStruct((1, c), F32)),
        grid=(t // ROW_TILE,), in_specs=[_row_spec(c), _vec_spec(c), _vec_spec(c), _row_spec(c)],
        out_specs=(_row_spec(c), _vec_spec(c), _vec_spec(c)), compiler_params=_params(("arbitrary",)),
    )(cd, g, b, dy)


def _shift_dn(x, k):
    if k == 0:
        return x
    row = lax.broadcasted_iota(jnp.int32, x.shape, 0)
    return jnp.where(row >= k, pltpu.roll(x, k, 0), 0.0)


def _shift_up(x, k):
    if k == 0:
        return x
    t = x.shape[0]
    row = lax.broadcasted_iota(jnp.int32, x.shape, 0)
    return jnp.where(row < t - k, pltpu.roll(x, t - k, 0), 0.0)


def _conv_fwd(x, w_ref, taps):
    acc = w_ref[pl.ds(taps - 1, 1), :] * x
    for k in range(taps - 1):
        acc += w_ref[pl.ds(k, 1), :] * _shift_dn(x, taps - 1 - k)
    return acc


def _conv_bwd(x, dy, w_ref, dw_ref, taps):
    dx = w_ref[pl.ds(taps - 1, 1), :] * dy
    dw_ref[pl.ds(taps - 1, 1), :] = jnp.sum(dy * x, axis=0, keepdims=True)
    for k in range(taps - 1):
        s = taps - 1 - k
        dx += w_ref[pl.ds(k, 1), :] * _shift_up(dy, s)
        dw_ref[pl.ds(k, 1), :] = jnp.sum(dy * _shift_dn(x, s), axis=0, keepdims=True)
    return dx


def _scan_fwd(a, u):
    t = a.shape[0]
    k = 1
    while k < t:
        u = u + a * _shift_dn(u, k)
        if 2 * k < t:
            a = a * _shift_dn(a, k)
        k *= 2
    return u


def _scan_rev(a, u):
    t = a.shape[0]
    k = 1
    while k < t:
        u = u + a * _shift_up(u, k)
        if 2 * k < t:
            a = a * _shift_up(a, k)
        k *= 2
    return u


def _one_minus_exp(y):
    return jnp.where(y > -1e-3, -(y + 0.5 * y * y + (1.0 / 6.0) * y * y * y), 1.0 - jnp.exp(y))


GELU_C = math.sqrt(2.0 / math.pi)


def _gelu(x):
    th = jnp.tanh(GELU_C * (x + 0.044715 * x * x * x))
    return 0.5 * x * (1.0 + th), th


def _softplus(x):
    return jnp.maximum(x, 0.0) + jnp.log(1.0 + jnp.exp(-jnp.abs(x)))


def _chunk_spec(t, blk0):
    return pl.BlockSpec((t, LANE), functools.partial(lambda c, b: (0, b + c), b=blk0))


def _tap_spec(taps):
    return pl.BlockSpec((taps, LANE), lambda c: (0, c))


def _cvec_spec():
    return pl.BlockSpec((1, LANE), lambda c: (0, c))


def _cmat_spec():
    return pl.BlockSpec((1, LANE, LANE), lambda c: (c, 0, 0))


def _lru_forward(ax, wA_ref, bA_ref, wx_ref, bx_ref, wa_ref, ba_ref, lam_ref, h=None):
    ca = _conv_fwd(ax, wA_ref, CONV_A) + bA_ref[...]
    gi = _sig(_dot(ca, wx_ref[0], NN) + bx_ref[...])
    gr = _sig(_dot(ca, wa_ref[0], NN) + ba_ref[...])
    sp = _softplus(-lam_ref[...])
    la = -LRU_C * gr * sp
    a = jnp.exp(la)
    mult = jnp.sqrt(_one_minus_exp(2.0 * la))
    if h is None:
        h = _scan_fwd(a, ca * gi * mult)
    return ca, gi, gr, sp, a, mult, h


def _a_fwd(proj, wA, bA, wx, bx, wa, ba, lam, name):
    t = proj.shape[0]

    def body(ax_ref, ag_ref, wA_ref, bA_ref, wx_ref, bx_ref, wa_ref, ba_ref, lam_ref, o_ref, h_ref):
        h = _lru_forward(ax_ref[...], wA_ref, bA_ref, wx_ref, bx_ref, wa_ref, ba_ref, lam_ref)[-1]
        h_ref[...] = h
        o_ref[...] = (h * _gelu(ag_ref[...])[0]).astype(BF16)

    return pl.pallas_call(
        body, name=name, out_shape=(jax.ShapeDtypeStruct((t, BW), BF16), jax.ShapeDtypeStruct((t, BW), F32)),
        grid=(BW // LANE,),
        in_specs=[_chunk_spec(t, C_AX), _chunk_spec(t, C_AG), _tap_spec(CONV_A), _cvec_spec(), _cmat_spec(),
                  _cvec_spec(), _cmat_spec(), _cvec_spec(), _cvec_spec()],
        out_specs=(_chunk_spec(t, 0), _chunk_spec(t, 0)), compiler_params=_params(("parallel",)),
    )(proj, proj, wA, bA, wx, bx, wa, ba, lam)


def _a_bwd(proj, h_fwd, dya, wA, bA, wx, bx, wa, ba, lam, name):
    t = proj.shape[0]

    def body(ax_ref, ag_ref, h_ref, dy_ref, wA_ref, bA_ref, wx_ref, bx_ref, wa_ref, ba_ref, lam_ref,
             dax_ref, dag_ref, dwA_ref, dbA_ref, dwx_ref, dbx_ref, dwa_ref, dba_ref, dlam_ref):
        ax = ax_ref[...]
        ag = ag_ref[...]
        dy = dy_ref[...]
        ca, gi, gr, sp, a, mult, h = _lru_forward(ax, wA_ref, bA_ref, wx_ref, bx_ref, wa_ref, ba_ref, lam_ref,
                                                  h_ref[...])
        gel, th = _gelu(ag)
        dgel = 0.5 * (1.0 + th) + 0.5 * ag * (1.0 - th * th) * GELU_C * (1.0 + 3.0 * 0.044715 * ag * ag)
        dag_ref[...] = (dy * h * dgel).astype(BF16)
        s = _scan_rev(_shift_up(a, 1), dy * gel)
        da = s * _shift_dn(h, 1)
        dca = s * gi * mult
        dgi = s * ca * mult
        dmult = s * ca * gi
        dla = da * a - dmult * a * a / mult
        dgr = dla * (-LRU_C * sp)
        dsp = jnp.sum(dla * (-LRU_C * gr), axis=0, keepdims=True)
        dlam_ref[...] = -_sig(-lam_ref[...]) * dsp
        dzi = dgi * gi * (1.0 - gi)
        dzr = dgr * gr * (1.0 - gr)
        dbx_ref[...] = jnp.sum(dzi, axis=0, keepdims=True)
        dba_ref[...] = jnp.sum(dzr, axis=0, keepdims=True)
        dwx_ref[0] = _dot(ca, dzi, TN)
        dwa_ref[0] = _dot(ca, dzr, TN)
        dca += _dot(dzi, wx_ref[0], NT) + _dot(dzr, wa_ref[0], NT)
        dbA_ref[...] = jnp.sum(dca, axis=0, keepdims=True)
        dax_ref[...] = _conv_bwd(ax, dca, wA_ref, dwA_ref, CONV_A).astype(BF16)

    big = jax.ShapeDtypeStruct((t, BW), BF16)
    vec = jax.ShapeDtypeStruct((1, BW), F32)
    mat = jax.ShapeDtypeStruct((BW // LANE, LANE, LANE), F32)
    return pl.pallas_call(
        body, name=name,
        out_shape=(big, big, jax.ShapeDtypeStruct((CONV_A, BW), F32), vec, mat, vec, mat, vec, vec),
        grid=(BW // LANE,),
        in_specs=[_chunk_spec(t, C_AX), _chunk_spec(t, C_AG), _chunk_spec(t, 0), _chunk_spec(t, 0), _tap_spec(CONV_A),
                  _cvec_spec(), _cmat_spec(), _cvec_spec(), _cmat_spec(), _cvec_spec(), _cvec_spec()],
        out_specs=(_chunk_spec(t, 0), _chunk_spec(t, 0), _tap_spec(CONV_A), _cvec_spec(), _cmat_spec(), _cvec_spec(),
                   _cmat_spec(), _cvec_spec(), _cvec_spec()),
        compiler_params=_params(("parallel",)),
    )(proj, proj, h_fwd, dya, wA, bA, wx, bx, wa, ba, lam)


def _b_fwd(proj, wB, name):
    t = proj.shape[0]

    def body(bv_ref, bc_ref, bb_ref, w_ref, o_ref):
        o_ref[...] = (bb_ref[...] * _conv_fwd(bc_ref[...] * bv_ref[...], w_ref, CONV_B)).astype(BF16)

    return pl.pallas_call(
        body, name=name, out_shape=jax.ShapeDtypeStruct((t, BW), BF16), grid=(BW // LANE,),
        in_specs=[_chunk_spec(t, C_BV), _chunk_spec(t, C_BC), _chunk_spec(t, C_BB), _tap_spec(CONV_B)],
        out_specs=_chunk_spec(t, 0), compiler_params=_params(("parallel",)),
    )(proj, proj, proj, wB)


def _b_bwd(proj, dyb, wB, name):
    t = proj.shape[0]

    def body(bv_ref, bc_ref, bb_ref, dy_ref, w_ref, dbv_ref, dbc_ref, dbb_ref, dw_ref):
        bv = bv_ref[...]
        bc = bc_ref[...]
        dy = dy_ref[...]
        p = bc * bv
        dbb_ref[...] = (dy * _conv_fwd(p, w_ref, CONV_B)).astype(BF16)
        dp = _conv_bwd(p, dy * bb_ref[...], w_ref, dw_ref, CONV_B)
        dbc_ref[...] = (dp * bv).astype(BF16)
        dbv_ref[...] = (dp * bc).astype(BF16)

    big = jax.ShapeDtypeStruct((t, BW), BF16)
    return pl.pallas_call(
        body, name=name, out_shape=(big, big, big, jax.ShapeDtypeStruct((CONV_B, BW), F32)), grid=(BW // LANE,),
        in_specs=[_chunk_spec(t, C_BV), _chunk_spec(t, C_BC), _chunk_spec(t, C_BB), _chunk_spec(t, 0),
                  _tap_spec(CONV_B)],
        out_specs=(_chunk_spec(t, 0),) * 3 + (_tap_spec(CONV_B),), compiler_params=_params(("parallel",)),
    )(proj, proj, proj, dyb, wB)


def _d_conv_fwd(proj, wD, bD, name):
    t = proj.shape[0]

    def body(d1_ref, d2_ref, w_ref, b_ref, o_ref):
        o_ref[...] = _conv_fwd(d1_ref[...] * _sig(d2_ref[...]), w_ref, CONV_D) + b_ref[...]

    return pl.pallas_call(
        body, name=name, out_shape=jax.ShapeDtypeStruct((t, BW), F32), grid=(BW // LANE,),
        in_specs=[_chunk_spec(t, C_D1), _chunk_spec(t, C_D2), _tap_spec(CONV_D), _cvec_spec()],
        out_specs=_chunk_spec(t, 0), compiler_params=_params(("parallel",)),
    )(proj, proj, wD, bD)


def _d_conv_bwd(proj, dcd, wD, name):
    t = proj.shape[0]

    def body(d1_ref, d2_ref, dy_ref, w_ref, dd1_ref, dd2_ref, dw_ref, db_ref):
        d1 = d1_ref[...]
        s = _sig(d2_ref[...])
        dy = dy_ref[...]
        db_ref[...] = jnp.sum(dy, axis=0, keepdims=True)
        dd = _conv_bwd(d1 * s, dy, w_ref, dw_ref, CONV_D)
        dd1_ref[...] = (dd * s).astype(BF16)
        dd2_ref[...] = (dd * d1 * s * (1.0 - s)).astype(BF16)

    big = jax.ShapeDtypeStruct((t, BW), BF16)
    return pl.pallas_call(
        body, name=name,
        out_shape=(big, big, jax.ShapeDtypeStruct((CONV_D, BW), F32), jax.ShapeDtypeStruct((1, BW), F32)),
        grid=(BW // LANE,),
        in_specs=[_chunk_spec(t, C_D1), _chunk_spec(t, C_D2), _chunk_spec(t, 0), _tap_spec(CONV_D)],
        out_specs=(_chunk_spec(t, 0), _chunk_spec(t, 0), _tap_spec(CONV_D), _cvec_spec()),
        compiler_params=_params(("parallel",)),
    )(proj, proj, dcd, wD)


SCALE = HEAD_DIM ** -0.5
GROUP = N_Q // N_KV


GROWS = GROUP * BLK


def _per_head(ss_ref, row, g):
    head = lax.broadcasted_iota(jnp.int32, (GROWS, 1), 0) // BLK
    col = jnp.full((GROWS, 1), ss_ref[row, g * GROUP + GROUP - 1], F32)
    for i in range(GROUP - 1):
        col = jnp.where(head == i, ss_ref[row, g * GROUP + i], col)
    return col


def _attn_probs(q_ref, k_ref, ss_ref, g, n):
    qi = lax.broadcasted_iota(jnp.int32, (GROWS, BLK), 0) % BLK
    ki = lax.broadcasted_iota(jnp.int32, (GROWS, BLK), 1)
    dist = (qi - ki).astype(F32)
    sink = _per_head(ss_ref, 0, g)
    slope = _per_head(ss_ref, 1, g)
    s0 = pl.multiple_of(n * BLK, BLK)
    sp = pl.multiple_of(jnp.maximum(n - 1, 0) * BLK, BLK)
    q = q_ref[:, pl.ds(s0, BLK), :].reshape(GROWS, HEAD_DIM)
    kc = k_ref[0, pl.ds(s0, BLK), :]
    kp = k_ref[0, pl.ds(sp, BLK), :]
    sc = jnp.where(ki <= qi, _dot(q, kc, NT) * SCALE - slope * dist, NEG_INF)
    first = jnp.where(n >= 1, 0, BLK)
    sv = jnp.where(ki > qi + first, _dot(q, kp, NT) * SCALE - slope * (dist + BLK), NEG_INF)
    m = jnp.maximum(jnp.maximum(jnp.max(sc, axis=1, keepdims=True), jnp.max(sv, axis=1, keepdims=True)), sink)
    pc = jnp.exp(sc - m)
    pp = jnp.exp(sv - m)
    ps = jnp.exp(sink - m)
    z = jnp.sum(pc, axis=1, keepdims=True) + jnp.sum(pp, axis=1, keepdims=True) + ps
    return s0, sp, q, kc, kp, pc, pp, ps, z


def _attn_specs(t):
    qs = pl.BlockSpec((GROUP, t, HEAD_DIM), lambda g: (g, 0, 0))
    ks = pl.BlockSpec((1, t, HEAD_DIM), lambda g: (g, 0, 0))
    ss = pl.BlockSpec(memory_space=pltpu.SMEM)
    return qs, ks, ss


def _attn_fwd(q, k, v, ss, name):
    t = q.shape[1]
    qs, ks, sspec = _attn_specs(t)

    def body(q_ref, k_ref, v_ref, ss_ref, o_ref):
        g = pl.program_id(0)

        def blk(n, carry):
            s0, sp, _, _, _, pc, pp, _, z = _attn_probs(q_ref, k_ref, ss_ref, g, n)
            o = _dot(pc, v_ref[0, pl.ds(s0, BLK), :], NN) + _dot(pp, v_ref[0, pl.ds(sp, BLK), :], NN)
            o_ref[:, pl.ds(s0, BLK), :] = (o / z).astype(BF16).reshape(GROUP, BLK, HEAD_DIM)
            return carry

        lax.fori_loop(0, t // BLK, blk, 0)

    return pl.pallas_call(
        body, name=name, out_shape=jax.ShapeDtypeStruct((N_Q, t, HEAD_DIM), BF16), grid=(N_KV,),
        in_specs=[qs, ks, ks, sspec], out_specs=qs, compiler_params=_params(("parallel",)),
    )(q, k, v, ss)


def _attn_bwd(q, k, v, do, ss, name):
    t = q.shape[1]
    qs, ks, sspec = _attn_specs(t)

    def body(q_ref, k_ref, v_ref, do_ref, ss_ref, dq_ref, dk_ref, dv_ref, ds_ref):
        g = pl.program_id(0)
        dk_ref[...] = jnp.zeros_like(dk_ref)
        dv_ref[...] = jnp.zeros_like(dv_ref)

        def blk(n, dsink):
            s0, sp, q, kc, kp, pc, pp, ps, z = _attn_probs(q_ref, k_ref, ss_ref, g, n)
            rz = 1.0 / z
            pc = pc * rz
            pp = pp * rz
            do_b = do_ref[:, pl.ds(s0, BLK), :].reshape(GROWS, HEAD_DIM)
            dpc = _dot(do_b, v_ref[0, pl.ds(s0, BLK), :], NT)
            dpp = _dot(do_b, v_ref[0, pl.ds(sp, BLK), :], NT)
            delta = jnp.sum(pc * dpc, axis=1, keepdims=True) + jnp.sum(pp * dpp, axis=1, keepdims=True)
            dsc = pc * (dpc - delta)
            dsp = pp * (dpp - delta)
            dq = (_dot(dsc, kc, NN) + _dot(dsp, kp, NN)) * SCALE
            dq_ref[:, pl.ds(s0, BLK), :] = dq.astype(BF16).reshape(GROUP, BLK, HEAD_DIM)
            dk_ref[0, pl.ds(s0, BLK), :] += _dot(dsc, q, TN) * SCALE
            dk_ref[0, pl.ds(sp, BLK), :] += _dot(dsp, q, TN) * SCALE
            dv_ref[0, pl.ds(s0, BLK), :] += _dot(pc, do_b, TN)
            dv_ref[0, pl.ds(sp, BLK), :] += _dot(pp, do_b, TN)
            return dsink - ps * rz * delta

        dsink = lax.fori_loop(0, t // BLK, blk, jnp.zeros((GROWS, 1), F32))
        for i in range(GROUP):
            ds_ref[i] = jnp.full(ds_ref.shape[1:], jnp.sum(dsink[i * BLK:(i + 1) * BLK]), F32)

    kv = jax.ShapeDtypeStruct((N_KV, t, HEAD_DIM), F32)
    return pl.pallas_call(
        body, name=name,
        out_shape=(jax.ShapeDtypeStruct((N_Q, t, HEAD_DIM), BF16), kv, kv, jax.ShapeDtypeStruct((N_Q, 8, LANE), F32)),
        grid=(N_KV,), in_specs=[qs, ks, ks, qs, sspec],
        out_specs=(qs, ks, ks, pl.BlockSpec((GROUP, 8, LANE), lambda g: (g, 0, 0))),
        compiler_params=_params(("parallel",)),
    )(q, k, v, do, ss)


def _heads(x2d, n):
    t = x2d.shape[0]
    return x2d.reshape(t, n, HEAD_DIM).transpose(1, 0, 2)


def _unheads(x3d):
    n, t, _ = x3d.shape
    return x3d.transpose(1, 0, 2).reshape(t, n * HEAD_DIM)


SMALL_ELEMS = 256 * 1024
TILE_ELEMS = 640 * 1024


def _row_tile(r, c):
    if r * c <= SMALL_ELEMS:
        return r
    return _pick(r, [t for t in (512, 256, 128, 64, 32, 16, 8) if t * c <= TILE_ELEMS])


def _adamw_update(w, gv, m, v):
    nm = ADAM_B1 * m + (1.0 - ADAM_B1) * gv
    nv = ADAM_B2 * v + (1.0 - ADAM_B2) * (gv * gv)
    m_hat = nm / (1.0 - ADAM_B1 ** ADAM_STEP)
    v_hat = nv / (1.0 - ADAM_B2 ** ADAM_STEP)
    return -ADAM_LR * (m_hat / (jnp.sqrt(v_hat) + ADAM_EPS) + ADAM_WD * w), nm, nv


def _adamw(w, g, m, v, name):
    r, c = w.shape
    tr = _row_tile(r, c)
    spec = pl.BlockSpec((tr, c), lambda i: (i, 0))

    def body(w_ref, g_ref, m_ref, v_ref, d_ref, nm_ref, nv_ref):
        d_ref[...], nm_ref[...], nv_ref[...] = _adamw_update(w_ref[...], g_ref[...], m_ref[...], v_ref[...])

    shp = jax.ShapeDtypeStruct((r, c), F32)
    return pl.pallas_call(
        body, name=name, out_shape=(shp, shp, shp), grid=(r // tr,), in_specs=[spec] * 4, out_specs=(spec,) * 3,
        compiler_params=_params(("parallel",)),
    )(w, g, m, v)


def _adamw_layer(w, m, v, g, l, prev, name):
    r, c = g.shape
    tr = _row_tile(r, c)
    layer = pl.BlockSpec((tr, c), lambda i: (l * (r // tr) + i, 0))

    def body(w_ref, m_ref, v_ref, g_ref, *rest):
        go_ref, d_ref, nm_ref, nv_ref, token = rest[-5:]
        gv = g_ref[...]
        go_ref[...] = gv
        token[...] = jnp.zeros_like(token)
        d_ref[...], nm_ref[...], nv_ref[...] = _adamw_update(w_ref[...], gv, m_ref[...], v_ref[...])

    carried = list(prev[:4]) if prev is not None else []
    shp = jax.ShapeDtypeStruct(w.shape, F32)
    return pl.pallas_call(
        body, name=name, out_shape=(shp,) * 4 + (jax.ShapeDtypeStruct((8, LANE), F32),), grid=(r // tr,),
        in_specs=[layer] * 3 + [pl.BlockSpec((tr, c), lambda i: (i, 0))] + [pl.BlockSpec(memory_space=pl.ANY)] * len(carried),
        out_specs=(layer,) * 4 + (pl.BlockSpec((8, LANE), lambda i: (0, 0)),),
        input_output_aliases={4 + i: i for i in range(len(carried))}, compiler_params=_params(("arbitrary",)),
    )(w, m, v, g, *carried)


def _sum_leading(x, name):
    n, r, c = x.shape
    tr = _row_tile(r, c)

    def body(x_ref, o_ref):
        acc = x_ref[0]
        for i in range(1, n):
            acc = acc + x_ref[i]
        o_ref[...] = acc

    return pl.pallas_call(
        body, name=name, out_shape=jax.ShapeDtypeStruct((r, c), F32), grid=(r // tr,),
        in_specs=[pl.BlockSpec((n, tr, c), lambda i: (0, i, 0))], out_specs=pl.BlockSpec((tr, c), lambda i: (i, 0)),
        compiler_params=_params(("parallel",)),
    )(x)


def _sum_own_plus(p, sel, recv, name, out_dtype):
    _, r, c = p.shape
    n = recv.shape[0]
    tr = _pick(r, (512, 448, 256, 128, 64, 16))

    def body(sel_ref, p_ref, r_ref, o_ref):
        acc = p_ref[0].astype(F32)
        for i in range(n):
            acc = acc + r_ref[i].astype(F32)
        o_ref[...] = acc.astype(out_dtype)

    grid_spec = pltpu.PrefetchScalarGridSpec(
        num_scalar_prefetch=1, grid=(r // tr,),
        in_specs=[pl.BlockSpec((1, tr, c), lambda i, s: (s[0], i, 0)), pl.BlockSpec((n, tr, c), lambda i, s: (0, i, 0))],
        out_specs=pl.BlockSpec((tr, c), lambda i, s: (i, 0)))
    return pl.pallas_call(
        body, name=name, out_shape=jax.ShapeDtypeStruct((r, c), out_dtype), grid_spec=grid_spec,
        compiler_params=_params(("parallel",)),
    )(sel, p, recv)


def _coords():
    return lax.axis_index("x"), lax.axis_index("y"), lax.axis_index("c")


def _allgather8(x2, name, space):
    _, m, n = x2.shape

    def body(x_ref, out_ref, send_sems, recv_sems, local_sem):
        x, y, c = _coords()
        me, sibling = (x, y, c), (x, y, 1 - c)
        chips = [(1 - x, y), (x, 1 - y), (1 - x, 1 - y)]
        mine_src = x_ref.at[c]

        def rows(px, py, pc):
            return out_ref.at[4 * px + 2 * py + pc]

        def copy(k, block, to, src=None):
            return pltpu.make_async_remote_copy(
                src_ref=rows(*block) if src is None else src, dst_ref=rows(*block),
                send_sem=send_sems.at[k], recv_sem=recv_sems.at[k], device_id=to, device_id_type=MESH)

        mine = pltpu.make_async_copy(mine_src, rows(*me), local_sem)
        mine.start()
        first = [copy(0, me, sibling, src=mine_src)]
        first += [copy(1 + j, me, (*chip, c), src=mine_src) for j, chip in enumerate(chips)]
        for cp in first:
            cp.start()
        passed = [copy(4 + j, (*chip, c), sibling) for j, chip in enumerate(chips)]
        for j, chip in enumerate(chips):
            copy(1 + j, (*chip, c), me).wait_recv()
            passed[j].start()
        copy(0, sibling, me).wait_recv()
        for j, chip in enumerate(chips):
            copy(4 + j, (*chip, 1 - c), me).wait_recv()
        for cp in first + passed:
            cp.wait_send()
        mine.wait()

    return pl.pallas_call(
        body, name=name, out_shape=jax.ShapeDtypeStruct((8, m, n), x2.dtype),
        in_specs=[pl.BlockSpec(memory_space=space)], out_specs=pl.BlockSpec(memory_space=space),
        scratch_shapes=[pltpu.SemaphoreType.DMA((7,)), pltpu.SemaphoreType.DMA((7,)), pltpu.SemaphoreType.DMA],
        compiler_params=pltpu.CompilerParams(vmem_limit_bytes=VMEM_LIMIT),
    )(x2)


N_REG = len(ROW_REGIONS) + 1


def _chip_window(ref, lead, r, j):
    view = ref if lead is None else ref.at[lead]
    if r < len(ROW_REGIONS):
        off, rows = ROW_REGIONS[r]
        return view.at[pl.ds(pl.multiple_of(off + j * rows, 16), rows), :]
    return view.at[pl.ds(R_OUT, OUT_ROWS), pl.ds(pl.multiple_of(j * OUT_COLS, LANE), OUT_COLS)]


HBM_SPEC = pl.BlockSpec(memory_space=pltpu.HBM)
SEM_SPEC = pl.BlockSpec(memory_space=pltpu.SEMAPHORE)


def _half_window(ref, r, j, h):
    if r < len(ROW_REGIONS):
        off, rows = ROW_REGIONS[r]
        return ref.at[pl.ds(pl.multiple_of(off + j * rows + h * (rows // 2), 16), rows // 2), :]
    half = OUT_ROWS // 2
    return ref.at[pl.ds(pl.multiple_of(R_OUT + h * half, 16), half),
                  pl.ds(pl.multiple_of(j * OUT_COLS, LANE), OUT_COLS)]


def _other_chips():
    x, y, _ = _coords()
    return [(1 - x, y), (x, 1 - y), (1 - x, 1 - y)]


def _ici_copies(srcs, arena_ref, send_sems, recv_sems, regions):
    x, y, c = _coords()
    sends, arrivals = [], []
    for k, (cx, cy) in enumerate(_other_chips()):
        for r in regions:
            def remote(src, j):
                return pltpu.make_async_remote_copy(
                    src_ref=src, dst_ref=_half_window(arena_ref, r, j, c), send_sem=send_sems.at[3 * r + k],
                    recv_sem=recv_sems.at[3 * r + k], device_id=(cx, cy, c), device_id_type=MESH)
            rows = srcs[r].shape[0] // 2
            sends.append(remote(srcs[r].at[pl.ds(pl.multiple_of(c * rows, 16), rows), :], 2 * x + y))
            arrivals.append(remote(_half_window(arena_ref, r, 2 * cx + cy, c), 2 * cx + cy))
    return sends, arrivals


def _sibling_copies(srcs, arena_ref, send_sems, recv_sems, regions):
    x, y, c = _coords()
    sends, arrivals = [], []

    def remote(win, r, k, src=None):
        return pltpu.make_async_remote_copy(
            src_ref=win if src is None else src, dst_ref=win, send_sem=send_sems.at[r, k],
            recv_sem=recv_sems.at[r, k], device_id=(x, y, 1 - c), device_id_type=MESH)

    for k, (cx, cy) in enumerate(_other_chips()):
        for r in regions:
            sends.append(remote(_half_window(arena_ref, r, 2 * cx + cy, c), r, k))
            arrivals.append(remote(_half_window(arena_ref, r, 2 * cx + cy, 1 - c), r, k))
    for r in regions:
        own = _chip_window(arena_ref, None, r, 2 * x + y)
        sends.append(remote(own, r, 3, src=srcs[r]))
        arrivals.append(remote(own, r, 3))
    return sends, arrivals


ICI_SEMS = pltpu.SemaphoreType.DMA((3 * N_REG,))
SIBLING_SEMS = pltpu.SemaphoreType.DMA((N_REG, 4))
ARENA_SHAPE = (ARENA_ROWS, ARENA_W)
ALL_REGIONS = tuple(range(N_REG))
IN_REGION = (3,)
REST_REGIONS = (0, 1, 2, 4, 5)


def _gather_layer(shards, name, regions=ALL_REGIONS):
    def body(*refs):
        srcs, arena_ref = refs[:N_REG], refs[N_REG]
        ici_send, ici_recv, sib_send, sib_recv = refs[N_REG + 1:]
        sends, arrivals = _ici_copies(srcs, arena_ref, ici_send, ici_recv, regions)
        passes, landings = _sibling_copies(srcs, arena_ref, sib_send, sib_recv, regions)
        for cp in sends + passes[len(arrivals):]:
            cp.start()
        for arrival, onward in zip(arrivals, passes):
            arrival.wait_recv()
            onward.start()
        for cp in landings:
            cp.wait_recv()
        for cp in sends + passes:
            cp.wait_send()

    return pl.pallas_call(
        body, name=name, out_shape=jax.ShapeDtypeStruct(ARENA_SHAPE, BF16),
        in_specs=[pl.BlockSpec(memory_space=pl.ANY)] * N_REG, out_specs=pl.BlockSpec(memory_space=pl.ANY),
        scratch_shapes=[ICI_SEMS, ICI_SEMS, SIBLING_SEMS, SIBLING_SEMS],
    )(*shards)


def _gather_start(shards, after, name, regions=ALL_REGIONS):
    def body(*refs):
        srcs, arena_ref = refs[:N_REG], refs[N_REG]
        send_sems, recv_sems = refs[N_REG + 2], refs[N_REG + 3]
        token = refs[-1]
        for cp in _ici_copies(srcs, arena_ref, send_sems, recv_sems, regions)[0]:
            cp.start()
        token[...] = jnp.zeros_like(token)

    hbm = lambda a: pltpu.with_memory_space_constraint(a, pltpu.HBM)
    outs = pl.pallas_call(
        body, name=name,
        out_shape=(ICI_SEMS, ICI_SEMS, *[pltpu.HBM(s.shape, s.dtype) for s in shards],
                   pltpu.HBM(ARENA_SHAPE, BF16), pltpu.HBM(after.shape, after.dtype),
                   jax.ShapeDtypeStruct((8, LANE), F32)),
        in_specs=[HBM_SPEC] * (N_REG + 2),
        out_specs=(SEM_SPEC, SEM_SPEC, *[HBM_SPEC] * (N_REG + 2), pl.BlockSpec(memory_space=pltpu.VMEM)),
        input_output_aliases={i: 2 + i for i in range(N_REG + 2)},
        compiler_params=pltpu.CompilerParams(has_side_effects=pltpu.SideEffectType.DATAFLOW_SIDE_EFFECTING),
    )(*[hbm(s) for s in shards], hbm(lax.empty(ARENA_SHAPE, BF16)), hbm(after))
    return outs[0], outs[1], outs[2:2 + N_REG], outs[2 + N_REG], outs[-1], outs[3 + N_REG]


def _gather_wait(send_sems, recv_sems, shards, arena, after, name, regions=ALL_REGIONS):
    def body(*refs):
        srcs, arena_ref = refs[:N_REG], refs[N_REG]
        sends, arrivals = _ici_copies(srcs, arena_ref, refs[N_REG + 1], refs[N_REG + 2], regions)
        for cp in sends:
            cp.wait_send()
        for cp in arrivals:
            cp.wait_recv()

    outs = pl.pallas_call(
        body, name=name,
        out_shape=(*[pltpu.HBM(s.shape, s.dtype) for s in shards], pltpu.HBM(ARENA_SHAPE, BF16)),
        in_specs=[HBM_SPEC] * (N_REG + 1) + [SEM_SPEC, SEM_SPEC, pl.BlockSpec(memory_space=pl.ANY)],
        out_specs=(HBM_SPEC,) * (N_REG + 1), input_output_aliases={i: i for i in range(N_REG + 1)},
        compiler_params=pltpu.CompilerParams(has_side_effects=pltpu.SideEffectType.DATAFLOW_SIDE_EFFECTING),
    )(*shards, arena, send_sems, recv_sems, after)
    return outs[:N_REG], outs[N_REG]


def _gather_finish(shards, arena, name, regions=ALL_REGIONS):
    def body(*refs):
        srcs, arena_ref = refs[:N_REG], refs[N_REG + 1]
        sends, arrivals = _sibling_copies(srcs, arena_ref, refs[N_REG + 2], refs[N_REG + 3], regions)
        for cp in sends:
            cp.start()
        for cp in arrivals:
            cp.wait_recv()
        for cp in sends:
            cp.wait_send()

    return pl.pallas_call(
        body, name=name, out_shape=jax.ShapeDtypeStruct(ARENA_SHAPE, BF16),
        in_specs=[pl.BlockSpec(memory_space=pl.ANY)] * (N_REG + 1), out_specs=pl.BlockSpec(memory_space=pl.ANY),
        scratch_shapes=[SIBLING_SEMS, SIBLING_SEMS], input_output_aliases={N_REG: 0},
    )(*shards, arena)


HALF_PIECE_OFF = tuple(o // 2 for o in PIECE_OFF)
HALF_PIECE_ROWS = PIECE_ROWS // 2
HALF_OUT_ROWS = OUT_ROWS // 2
SWAP_SEMS = pltpu.SemaphoreType.DMA((4 * N_REG,))
SCATTER_SEMS = pltpu.SemaphoreType.DMA((6,))


def _packed_shapes(slots, dtype):
    return (jax.ShapeDtypeStruct((slots, HALF_PIECE_ROWS, ARENA_W), dtype),
            jax.ShapeDtypeStruct((slots, HALF_OUT_ROWS, OUT_COLS), dtype))


def _swap_copies(g_ref, main_ref, outp_ref, send_sems, recv_sems):
    x, y, c = _coords()
    cps = []
    for j in range(4):
        for r in range(N_REG):
            if r < len(ROW_REGIONS):
                dst = main_ref.at[j, pl.ds(HALF_PIECE_OFF[r], ROW_REGIONS[r][1] // 2), :]
            else:
                dst = outp_ref.at[j]
            cps.append(pltpu.make_async_remote_copy(
                src_ref=_half_window(g_ref, r, j, 1 - c), dst_ref=dst, send_sem=send_sems.at[j * N_REG + r],
                recv_sem=recv_sems.at[j * N_REG + r], device_id=(x, y, 1 - c), device_id_type=MESH))
    return cps


def _swap_start(ga, name):
    def body(g_ref, main_ref, outp_ref, send_sems, recv_sems, *rest):
        for cp in _swap_copies(g_ref, main_ref, outp_ref, send_sems, recv_sems):
            cp.start()

    hbm = lambda a: pltpu.with_memory_space_constraint(a, pltpu.HBM)
    bufs = [ga] + [lax.empty(s.shape, s.dtype) for s in _packed_shapes(4, ga.dtype)]
    outs = pl.pallas_call(
        body, name=name, out_shape=(SWAP_SEMS, SWAP_SEMS, *[pltpu.HBM(b.shape, b.dtype) for b in bufs]),
        in_specs=[HBM_SPEC] * 3, out_specs=(SEM_SPEC, SEM_SPEC, *[HBM_SPEC] * 3),
        input_output_aliases={i: 2 + i for i in range(3)},
        compiler_params=pltpu.CompilerParams(has_side_effects=pltpu.SideEffectType.DATAFLOW_SIDE_EFFECTING),
    )(*[hbm(b) for b in bufs])
    return outs[0], outs[1], outs[2:]


def _swap_wait(send_sems, recv_sems, bufs, after, name):
    def body(g_ref, main_ref, outp_ref, send_sems, recv_sems, *rest):
        for cp in _swap_copies(g_ref, main_ref, outp_ref, send_sems, recv_sems):
            cp.wait_send()
            cp.wait_recv()

    return pl.pallas_call(
        body, name=name, out_shape=tuple(pltpu.HBM(b.shape, b.dtype) for b in bufs),
        in_specs=[HBM_SPEC] * 3 + [SEM_SPEC, SEM_SPEC, pl.BlockSpec(memory_space=pl.ANY)],
        out_specs=(HBM_SPEC,) * 3, input_output_aliases={i: i for i in range(3)},
        compiler_params=pltpu.CompilerParams(has_side_effects=pltpu.SideEffectType.DATAFLOW_SIDE_EFFECTING),
    )(*bufs, send_sems, recv_sems, after)


def _own_halves(ga, cc):
    mains = [jnp.concatenate([lax.dynamic_slice(ga, (off + j * rows + cc * (rows // 2), 0), (rows // 2, ARENA_W))
                              for off, rows in ROW_REGIONS]) for j in range(4)]
    outs = [lax.dynamic_slice(ga, (R_OUT + cc * HALF_OUT_ROWS, j * OUT_COLS), (HALF_OUT_ROWS, OUT_COLS))
            for j in range(4)]
    return jnp.stack(mains), jnp.stack(outs)


def _scatter_copies(main_ref, outp_ref, rmain_ref, routp_ref, send_sems, recv_sems):
    _, _, c = _coords()
    cps = []
    for k, (cx, cy) in enumerate(_other_chips()):
        for i, (src, dst) in enumerate(((main_ref, rmain_ref), (outp_ref, routp_ref))):
            cps.append(pltpu.make_async_remote_copy(
                src_ref=src.at[2 * cx + cy], dst_ref=dst.at[k], send_sem=send_sems.at[2 * k + i],
                recv_sem=recv_sems.at[2 * k + i], device_id=(cx, cy, c), device_id_type=MESH))
    return cps


def _scatter_start(main, outp, name):
    def body(main_ref, outp_ref, rmain_ref, routp_ref, send_sems, recv_sems, *rest):
        for cp in _scatter_copies(main_ref, outp_ref, rmain_ref, routp_ref, send_sems, recv_sems):
            cp.start()
        rest[-1][...] = jnp.zeros_like(rest[-1])

    hbm = lambda a: pltpu.with_memory_space_constraint(a, pltpu.HBM)
    land = [lax.empty(s.shape, s.dtype) for s in _packed_shapes(3, main.dtype)]
    bufs = [main, outp, *land]
    outs = pl.pallas_call(
        body, name=name,
        out_shape=(SCATTER_SEMS, SCATTER_SEMS, *[pltpu.HBM(b.shape, b.dtype) for b in bufs],
                   jax.ShapeDtypeStruct((8, LANE), F32)),
        in_specs=[HBM_SPEC] * 4, out_specs=(SEM_SPEC, SEM_SPEC, *[HBM_SPEC] * 4, pl.BlockSpec(memory_space=pltpu.VMEM)),
        input_output_aliases={i: 2 + i for i in range(4)},
        compiler_params=pltpu.CompilerParams(has_side_effects=pltpu.SideEffectType.DATAFLOW_SIDE_EFFECTING),
    )(*[hbm(b) for b in bufs])
    return outs[0], outs[1], outs[2:6], outs[6]


def _scatter_wait(send_sems, recv_sems, bufs, after, name):
    def body(main_ref, outp_ref, rmain_ref, routp_ref, send_sems, recv_sems, *rest):
        for cp in _scatter_copies(main_ref, outp_ref, rmain_ref, routp_ref, send_sems, recv_sems):
            cp.wait_send()
            cp.wait_recv()

    return pl.pallas_call(
        body, name=name, out_shape=tuple(pltpu.HBM(b.shape, b.dtype) for b in bufs),
        in_specs=[HBM_SPEC] * 4 + [SEM_SPEC, SEM_SPEC, pl.BlockSpec(memory_space=pl.ANY)],
        out_specs=(HBM_SPEC,) * 4, input_output_aliases={i: i for i in range(4)},
        compiler_params=pltpu.CompilerParams(has_side_effects=pltpu.SideEffectType.DATAFLOW_SIDE_EFFECTING),
    )(*bufs, send_sems, recv_sems, after)


def _swap_many(arrs, name):
    n = len(arrs)

    def body(*refs):
        x, y, c = _coords()
        send_sems, recv_sems = refs[2 * n], refs[2 * n + 1]
        cps = [pltpu.make_async_remote_copy(
            src_ref=refs[i], dst_ref=refs[n + i], send_sem=send_sems.at[i], recv_sem=recv_sems.at[i],
            device_id=(x, y, 1 - c), device_id_type=MESH) for i in range(n)]
        for cp in cps:
            cp.start()
        for cp in cps:
            cp.wait()

    return pl.pallas_call(
        body, name=name, out_shape=tuple(jax.ShapeDtypeStruct(a.shape, a.dtype) for a in arrs),
        in_specs=[pl.BlockSpec(memory_space=pl.ANY)] * n, out_specs=(pl.BlockSpec(memory_space=pl.ANY),) * n,
        scratch_shapes=[pltpu.SemaphoreType.DMA((n,)), pltpu.SemaphoreType.DMA((n,))],
    )(*arrs)


def _join_halves(mine, theirs, cc):
    return jnp.where(cc == 0, jnp.concatenate([mine, theirs]), jnp.concatenate([theirs, mine]))


def _reduced_layer(red, sib, cc):
    parts = []
    for (_, rows), off in zip(ROW_REGIONS, HALF_PIECE_OFF):
        parts.append(_join_halves(red[0][off:off + rows // 2], sib[0][off:off + rows // 2], cc))
    return jnp.concatenate(parts), _join_halves(red[1], sib[1], cc)


OUT_NAMES = ("w_a_out", "w_b_out", "w_c_out", "w_d_out")


def _arena_shards(w):
    t = lambda a: a.astype(BF16).transpose(0, 2, 1)
    return (w["w_ffn_down"].astype(BF16), t(w["w_ffn_gate"]), t(w["w_ffn_up"]), t(w["w_in"]), w["w_o"].astype(BF16),
            jnp.concatenate([w[n].astype(BF16) for n in OUT_NAMES], axis=1))


def _shard_grads(main, outp):
    t = lambda r: main[PIECE_OFF[r]:PIECE_OFF[r] + ROW_REGIONS[r][1]]
    g = dict(w_ffn_down=t(0), w_ffn_gate=t(1).T, w_ffn_up=t(2).T, w_in=t(3).T, w_o=t(4))
    for i, n in enumerate(OUT_NAMES):
        g[n] = outp[i * BW:(i + 1) * BW]
    return g


def _gather_taps(p, name):
    mine = jnp.concatenate([p[n] for n in CONV_NAMES], axis=1).reshape(DEPTH * N_TAPS, LANE)
    rows = -(-mine.shape[0] // 8) * 8
    mine = jnp.concatenate([mine, jnp.zeros((rows - mine.shape[0], LANE), F32)])
    g = _allgather8(jnp.stack([mine, mine]), name, pltpu.VMEM)[0::2, :DEPTH * N_TAPS]
    full = g.reshape(4, DEPTH, N_TAPS, LANE).transpose(1, 2, 0, 3).reshape(DEPTH, N_TAPS, BW)
    return dict(conv_a_w=full[:, :CONV_A], conv_b_w=full[:, CONV_A:CONV_A + CONV_B], conv_d_w=full[:, CONV_A + CONV_B:])


def _flat_pack(arrs):
    flat = jnp.concatenate([a.reshape(-1).astype(F32) for a in arrs])
    rows = -(-flat.shape[0] // (8 * LANE)) * 8
    return jnp.concatenate([flat, jnp.zeros((rows * LANE - flat.shape[0],), F32)]).reshape(rows, LANE)


def _flat_unpack(packed, shapes):
    flat, out, off = packed.reshape(-1), [], 0
    for s in shapes:
        cnt = int(np.prod(s))
        out.append(flat[off:off + cnt].reshape(s))
        off += cnt
    return out


def _blockdiag_chunks(w):
    w4 = w.reshape(4, 2, 64, 64)
    z = jnp.zeros((4, 2, 64, 2, 64), F32)
    z = z.at[:, 0, :, 0, :].set(w4[:, 0]).at[:, 1, :, 1, :].set(w4[:, 1])
    return z.reshape(4, LANE, LANE)


def _blockdiag_extract(d):
    d5 = d.reshape(4, 2, 64, 2, 64)
    return jnp.stack([d5[:, 0, :, 0, :], d5[:, 1, :, 1, :]], axis=1).reshape(8, 64, 64)


SLOPES = np.asarray([2.0 ** (-8.0 * (i + 1) / N_Q) for i in range(N_Q)], np.float32)


def _layer_consts(p, fw, l):
    row = lambda a: a[l].reshape(1, -1)
    return dict(
        g1=row(p["norm1_g"]), g2=row(p["norm2_g"]), wA=fw["conv_a_w"][l], bA=row(p["conv_a_b"]),
        wx=_blockdiag_chunks(p["lru_wx"][l]), bx=row(p["lru_bx"]), wa=_blockdiag_chunks(p["lru_wa"][l]),
        ba=row(p["lru_ba"]), lam=row(p["lru_lambda"]), wB=fw["conv_b_w"][l],
        ss=jnp.stack([p["sinks"][l], jnp.asarray(SLOPES)]), wD=fw["conv_d_w"][l], bD=row(p["conv_d_b"]),
        lg=row(p["ln_d_g"]), lb=row(p["ln_d_b"]))


def _layer_fwd(x, c, fw, l, rest_of_weights=None):
    t = f"l{l}_"
    xn = _rms_fwd(x, c["g1"], t + "rms1")
    wt = lambda off, rows: Win(fw["arena"][l], None, off, rows)
    proj = _mm(xn, Win(fw["arena_in"][l], None, R_IN, IN_W), "nt", t + "proj")
    ya, lru_h = _a_fwd(proj, c["wA"], c["bA"], c["wx"], c["bx"], c["wa"], c["ba"], c["lam"], t + "a_fwd")
    yb = _b_fwd(proj, c["wB"], t + "b_fwd")
    q3 = _heads(proj[:, OFF_Q:OFF_K], N_Q)
    k3 = _heads(proj[:, OFF_K:OFF_V], N_KV)
    v3 = _heads(proj[:, OFF_V:OFF_V + N_KV * HEAD_DIM], N_KV)
    yc = _unheads(_attn_fwd(q3, k3, v3, c["ss"], t + "attn_fwd"))
    cd = _d_conv_fwd(proj, c["wD"], c["bD"], t + "d_conv_fwd")
    yd = _ln_silu_fwd(cd, c["lg"], c["lb"], t + "d_ln_fwd")
    ys = (ya, yb, yc, yd)
    if fw["arena"][l] is None:
        fw["arena"][l] = rest_of_weights(yd)
    big_y = tuple(_mm(y, wt(R_OUT + i * BW, BW), "nn", t + f"out{i}") for i, y in enumerate(ys))
    merged = _merge_fwd(proj, big_y, t + "merge_fwd")
    hres = _mm(merged, wt(R_O, D_MODEL), "nn", t + "wo", add=x)
    hn = _rms_fwd(hres, c["g2"], t + "rms2")
    gg = _mm(hn, wt(R_GATE, D_FF), "nt", t + "ffn_gate")
    uu = _mm(hn, wt(R_UP, D_FF), "nt", t + "ffn_up")
    act = _swiglu_fwd(gg, uu, t + "swiglu_fwd")
    xout = _mm(act, wt(R_DOWN, D_FF), "nn", t + "ffn_down", add=hres)
    saved = dict(x=x, xn=xn, proj=proj, ys=ys, q3=q3, k3=k3, v3=v3, cd=cd, big_y=big_y, merged=merged, hres=hres,
                 hn=hn, gg=gg, uu=uu, act=act, lru_h=lru_h)
    return xout, saved


def _layer_bwd(dxout, s, c, fw, l, ga, weight_grads_done=None):
    t = f"l{l}_"
    gs = {}
    wt = lambda off, rows: Win(fw["arena"][l], None, off, rows)
    gt = lambda off, rows: Win(ga, None, off, rows)
    dact = _mm(dxout, wt(R_DOWN, D_FF), "nt", t + "d_act")
    ga = _mm(s["act"], dxout, "tn", t + "dw_down", out=gt(R_DOWN, D_FF))
    dgg, duu = _swiglu_bwd(s["gg"], s["uu"], dact, t + "swiglu_bwd")
    ga = _mm(dgg, s["hn"], "tn", t + "dw_gate", out=gt(R_GATE, D_FF))
    ga = _mm(duu, s["hn"], "tn", t + "dw_up", out=gt(R_UP, D_FF))
    dhn = _mm(dgg, wt(R_GATE, D_FF), "nn", t + "d_hn_g")
    dhn = _mm(duu, wt(R_UP, D_FF), "nn", t + "d_hn_u", add=dhn)
    dhres, gs["norm2_g"] = _rms_bwd(s["hres"], c["g2"], dhn, dxout, t + "rms2_bwd")
    dmerged = _mm(dhres, wt(R_O, D_MODEL), "nt", t + "d_merged")
    ga = _mm(s["merged"], dhres, "tn", t + "dw_o", out=gt(R_O, D_MODEL))
    dbig_y, dgl = _merge_bwd(s["proj"], s["big_y"], dmerged, t + "merge_bwd")
    dys = []
    for i in range(4):
        ga = _mm(s["ys"][i], dbig_y[i], "tn", t + f"dw_out{i}", out=gt(R_OUT + i * BW, BW))
        dys.append(_mm(dbig_y[i], wt(R_OUT + i * BW, BW), "nt", t + f"d_y{i}"))
    proj = s["proj"]
    (dax, dag, gs["conv_a_w"], gs["conv_a_b"], dwx, gs["lru_bx"], dwa, gs["lru_ba"], gs["lru_lambda"]) = _a_bwd(
        proj, s["lru_h"], dys[0], c["wA"], c["bA"], c["wx"], c["bx"], c["wa"], c["ba"], c["lam"], t + "a_bwd")
    gs["lru_wx"] = _blockdiag_extract(dwx)
    gs["lru_wa"] = _blockdiag_extract(dwa)
    dbv, dbc, dbb, gs["conv_b_w"] = _b_bwd(proj, dys[1], c["wB"], t + "b_bwd")
    dq3, dk3, dv3, dsink = _attn_bwd(s["q3"], s["k3"], s["v3"], _heads(dys[2], N_Q), c["ss"], t + "attn_bwd")
    gs["sinks"] = dsink[:, 0, 0]
    dcd, gs["ln_d_g"], gs["ln_d_b"] = _ln_silu_bwd(s["cd"], c["lg"], c["lb"], dys[3], t + "d_ln_bwd")
    dd1, dd2, gs["conv_d_w"], gs["conv_d_b"] = _d_conv_bwd(proj, dcd, c["wD"], t + "d_conv_bwd")
    dproj = jnp.concatenate(
        [dax, dag, dbv, dbc, dbb, _unheads(dq3), _unheads(dk3).astype(BF16), _unheads(dv3).astype(BF16), dd1, dd2,
         *dgl], axis=1)
    ga = _mm(dproj, s["xn"], "tn", t + "dw_in", out=gt(R_IN, IN_W))
    token = weight_grads_done(ga) if weight_grads_done is not None else None
    dxn = _mm(dproj, Win(fw["arena_in"][l], None, R_IN, IN_W), "nn", t + "d_xn", after=token)
    dx, gs["norm1_g"] = _rms_bwd(s["x"], c["g1"], dxn, dhres, t + "rms1_bwd")
    return dx, ga, gs


def kernel(x, norm1_g, w_in, conv_a_w, conv_a_b, lru_wx, lru_bx, lru_wa, lru_ba, lru_lambda, w_a_out, conv_b_w, w_b_out, sinks, w_c_out, conv_d_w, conv_d_b, ln_d_g, ln_d_b, w_d_out, w_o, norm2_g, w_ffn_gate, w_ffn_up, w_ffn_down, final_g, loss_target, m_norm1_g, m_w_in, m_conv_a_w, m_conv_a_b, m_lru_wx, m_lru_bx, m_lru_wa, m_lru_ba, m_lru_lambda, m_w_a_out, m_conv_b_w, m_w_b_out, m_sinks, m_w_c_out, m_conv_d_w, m_conv_d_b, m_ln_d_g, m_ln_d_b, m_w_d_out, m_w_o, m_norm2_g, m_w_ffn_gate, m_w_ffn_up, m_w_ffn_down, m_final_g, v_norm1_g, v_w_in, v_conv_a_w, v_conv_a_b, v_lru_wx, v_lru_bx, v_lru_wa, v_lru_ba, v_lru_lambda, v_w_a_out, v_conv_b_w, v_w_b_out, v_sinks, v_w_c_out, v_conv_d_w, v_conv_d_b, v_ln_d_g, v_ln_d_b, v_w_d_out, v_w_o, v_norm2_g, v_w_ffn_gate, v_w_ffn_up, v_w_ffn_down, v_final_g):
    given = dict(locals())
    p = {n: given[n] for n in NAMES}
    mom = {n: given["m_" + n] for n in NAMES}
    var = {n: given["v_" + n] for n in NAMES}
    cx, cy, cc = _coords()
    chip = 2 * cx + cy

    shards = _arena_shards(p)
    fw = _gather_taps(p, "gather_taps")
    shards0, shards1 = [s[0] for s in shards], [s[1] for s in shards]
    flight0 = _gather_start(shards0, _gather_layer(shards0, "gather_l0_in", IN_REGION), "gather_l0_rest_start",
                            REST_REGIONS)
    fw["arena_in"] = [flight0[5], None]
    fw["arena"] = [None, None]
    consts = [_layer_consts(p, fw, l) for l in range(DEPTH)]
    consts[0]["g1"] = consts[0]["g1"] + flight0[4][0:1, 0:1]
    flight1 = []

    def rest_of_layer0(after):
        sh, landing = _gather_wait(*flight0[:4], after, "gather_l0_rest_wait", REST_REGIONS)
        arena = _gather_finish(sh, landing, "gather_l0_rest_finish", REST_REGIONS)
        flight1.extend(_gather_start(shards1, arena, "gather_l1_start"))
        return flight1[5]

    h = x[0]
    saved = []
    for l in range(DEPTH):
        if l == 1:
            sh, landing = _gather_wait(*flight1[:4], h, "gather_l1_wait")
            fw["arena"][1] = fw["arena_in"][1] = _gather_finish(sh, landing, "gather_l1_finish")
        h, s = _layer_fwd(h, consts[l], fw, l, rest_of_layer0)
        saved.append(s)
    loss_vec, dh, g_final = _loss_head(h, final_g.reshape(1, -1), loss_target[0], "loss_head")
    loss = lax.psum(loss_vec[0, 0], ("x", "y", "c"))

    zero = jnp.zeros((1,), jnp.int32)
    chip_sel = chip.reshape(1).astype(jnp.int32)

    def chip_sums(ga, t):
        ss, rs, flying = _swap_start(ga, t + "grads_swap_start")
        own = _own_halves(flying[0], cc)
        got = _swap_wait(ss, rs, flying, own[0], t + "grads_swap_wait")[1:]
        return [_sum_own_plus(o.reshape((1, -1, o.shape[-1])), zero, r.reshape((1, -1, r.shape[-1])),
                              t + f"grads_sum_chip{i}", BF16).reshape(o.shape) for i, (o, r) in enumerate(zip(own, got))]

    def all_sums(sums, got, t):
        return [_sum_own_plus(s, chip_sel, r, t + f"grads_sum_all{i}", F32) for i, (s, r) in enumerate(zip(sums, got))]

    gss = [None] * DEPTH
    dh, ga1, gss[1] = _layer_bwd(dh, saved[1], consts[1], fw, 1, lax.empty(ARENA_SHAPE, BF16))
    send_sems, recv_sems, bufs, token = _scatter_start(*chip_sums(ga1, "l1_"), "l1_grads_scatter_start")
    scatter0 = []

    def start_layer0_scatter(ga0):
        scatter0.extend(_scatter_start(*chip_sums(ga0, "l0_"), "l0_grads_scatter_start"))
        return scatter0[3]

    dh, _, gss[0] = _layer_bwd(dh + token[0:1, 0:1], saved[0], consts[0], fw, 0, lax.empty(ARENA_SHAPE, BF16),
                               start_layer0_scatter)
    grad_x = dh[None]

    flat = lambda a: a.reshape(-1, a.shape[-1])

    def finish_layer(bufs_l, l, carried, t):
        red = all_sums(bufs_l[:2], bufs_l[2:], t)
        g_l = _shard_grads(*_reduced_layer(red, _swap_many(red, t + "grads_swap_reduced"), cc))
        return {n: _adamw_layer(flat(p[n]), flat(mom[n]), flat(var[n]), g_l[n], l, carried and carried[n],
                                t + "adamw_" + n) for n in BIG}

    big = finish_layer(_scatter_wait(send_sems, recv_sems, bufs, scatter0[3], "l1_grads_scatter_wait"), 1, None, "l1_")
    done1 = big["w_in"][4][0:1, 0:1]

    small_full = {n: (g_final.reshape(-1) if n == "final_g" else
                      jnp.stack([gss[l][n].reshape(gss[l][n].shape[-2:] if n.startswith("conv") and n.endswith("_w")
                                                   else p[n].shape[1:]) for l in range(DEPTH)]))
                  for n in SMALL}
    part = _flat_pack([small_full[n] for n in SMALL]) + done1
    rows = part.shape[0]
    gathered = _allgather8(jnp.stack([part, part]), "gather_small_grads", pltpu.VMEM)
    small_packed = _sum_leading(gathered, "small_grads_sum")
    small_sum = _flat_unpack(small_packed, [small_full[n].shape for n in SMALL])

    big = finish_layer(_scatter_wait(*scatter0[:3], small_packed, "l0_grads_scatter_wait"), 0, big, "l0_")
    g, delta, new_m, new_v = ({n: big[n][i].reshape(p[n].shape) for n in BIG} for i in range(4))
    for n, a in zip(SMALL, small_sum):
        g[n] = lax.dynamic_slice_in_dim(a, chip * LANE, LANE, axis=2) if n in CONV_NAMES else a

    shapes = [p[n].shape for n in SMALL]
    d, nm, nv = _adamw(_flat_pack([p[n] for n in SMALL]), _flat_pack([g[n] for n in SMALL]),
                       _flat_pack([mom[n] for n in SMALL]), _flat_pack([var[n] for n in SMALL]), "adamw_small")
    for n, a, b, cval in zip(SMALL, _flat_unpack(d, shapes), _flat_unpack(nm, shapes), _flat_unpack(nv, shapes)):
        delta[n], new_m[n], new_v[n] = a, b, cval

    return (loss, grad_x, *[g[n] for n in NAMES], *[delta[n] for n in NAMES], *[new_m[n] for n in NAMES],
            *[new_v[n] for n in NAMES])
```

```python
import functools
import math

import numpy as np
import jax
import jax.numpy as jnp
from jax import lax
from jax.experimental import pallas as pl
from jax.experimental.pallas import tpu as pltpu

F32 = jnp.float32
BF16 = jnp.bfloat16
MESH = pl.DeviceIdType.MESH

D_MODEL = 1024
DEPTH = 2
BW = 512
HEAD_DIM = 64
N_Q = 8
N_KV = 2
BLK = 128
D_FF = 2816
IN_W = 8448
EPS = 1e-6
NEG_INF = -1e30
LRU_C = 8.0
CONV_A, CONV_B, CONV_D = 4, 3, 31
LANE = 128
ROW_TILE = 256
VMEM_LIMIT = 56 * 1024 * 1024
MM_VMEM_BUDGET = 36 * 1024 * 1024

C_AX, C_AG, C_BV, C_BC, C_BB = 0, 4, 8, 12, 16
OFF_Q, OFF_K, OFF_V = 2560, 3072, 3200
C_D1, C_D2 = 26, 30
OFF_GL = 4352

ADAM_LR, ADAM_B1, ADAM_B2, ADAM_EPS, ADAM_WD, ADAM_STEP = 0.001, 0.9, 0.999, 1e-08, 0.01, 10

ARENA_W = 1024
R_DOWN, R_GATE, R_UP, R_IN, R_O, R_OUT = 0, 2816, 5632, 8448, 16896, 17920
ARENA_ROWS = 19968
ROW_REGIONS = ((R_DOWN, 704), (R_GATE, 704), (R_UP, 704), (R_IN, 2112), (R_O, 256))
PIECE_OFF = (0, 704, 1408, 2112, 4224)
PIECE_ROWS = 4480
OUT_ROWS, OUT_COLS = 4 * BW, D_MODEL // 4

BIG = ("w_in", "w_a_out", "w_b_out", "w_c_out", "w_d_out", "w_o", "w_ffn_gate", "w_ffn_up", "w_ffn_down")
CONV_NAMES = ("conv_a_w", "conv_b_w", "conv_d_w")
N_TAPS = CONV_A + CONV_B + CONV_D
SMALL = ("norm1_g", "conv_a_w", "conv_a_b", "lru_wx", "lru_bx", "lru_wa", "lru_ba", "lru_lambda", "conv_b_w",
         "sinks", "conv_d_w", "conv_d_b", "ln_d_g", "ln_d_b", "norm2_g", "final_g")
NAMES = ['norm1_g', 'w_in', 'conv_a_w', 'conv_a_b', 'lru_wx', 'lru_bx', 'lru_wa', 'lru_ba', 'lru_lambda', 'w_a_out',
         'conv_b_w', 'w_b_out', 'sinks', 'w_c_out', 'conv_d_w', 'conv_d_b', 'ln_d_g', 'ln_d_b', 'w_d_out', 'w_o',
         'norm2_g', 'w_ffn_gate', 'w_ffn_up', 'w_ffn_down', 'final_g']


def _pick(n, cands, off=0):
    for c in cands:
        if n % c == 0 and off % c == 0:
            return c
    assert off == 0, (n, off)
    return n


class Win:
    def __init__(self, arena, l, off, rows):
        self.arena, self.l, self.off, self.rows = arena, l, off, rows
        self.shape = (rows, arena.shape[-1])


def _params(sem=None):
    return pltpu.CompilerParams(dimension_semantics=sem, vmem_limit_bytes=VMEM_LIMIT)


def _sig(z):
    return 1.0 / (1.0 + jnp.exp(-z))


def _dot(a, b, dims):
    return lax.dot_general(a.astype(BF16), b.astype(BF16), (dims, ((), ())), preferred_element_type=F32)


NN = ((1,), (0,))
NT = ((1,), (1,))
TN = ((0,), (0,))


def _mm(a, b, mode, name, out_dtype=F32, add=None, out=None, after=None):
    if mode == "nn":
        (m, k), n = a.shape, b.shape[1]
    elif mode == "nt":
        (m, k), n = a.shape, b.shape[0]
    else:
        (k, m), n = a.shape, b.shape[1]
    b_win = isinstance(b, Win)
    b_off = b.off if b_win else 0
    o_off = out.off if out is not None else 0
    if out is not None:
        out_dtype = out.arena.dtype
    tk = _pick(k, (2816, 2048, 1408, 1024, 768, 512, 256), b_off if mode != "nt" else 0)
    nk = k // tk
    n_off = b_off if mode == "nt" else 0
    a_bytes, b_bytes, o_bytes = a.dtype.itemsize, 2, jnp.dtype(out_dtype).itemsize

    def vmem_bytes(tm_, tn_):
        tile = tm_ * tn_
        return (2 * tk * (tm_ * a_bytes + tn_ * b_bytes) + 2 * tile * o_bytes + (tile * 4 if nk > 1 else 0)
                + (2 * tile * 4 if add is not None else 0) + tile * 4)

    pairs = [(tm_, tn_) for tm_ in (2048, 1024, 768, 512, 256, 128) for tn_ in (1024, 768, 512, 256, 128)
             if m % tm_ == 0 and o_off % tm_ == 0 and n % tn_ == 0 and n_off % tn_ == 0
             and vmem_bytes(tm_, tn_) <= MM_VMEM_BUDGET]
    tm, tn = max(pairs, key=lambda p: (p[0] * p[1], p[0]))
    dims = {"nn": NN, "nt": NT, "tn": TN}[mode]

    def body(*refs):
        a_ref, b_ref = refs[:2]
        c_ref = refs[2] if add is not None else None
        if nk == 1:
            r = _dot(a_ref[...], b_ref[...], dims)
            if add is not None:
                r = r + c_ref[...]
            refs[-1][...] = r.astype(out_dtype)
            return
        o_ref, acc = refs[-2:]
        kk = pl.program_id(2)

        @pl.when(kk == 0)
        def _():
            acc[...] = jnp.zeros_like(acc)

        acc[...] += _dot(a_ref[...], b_ref[...], dims)

        @pl.when(kk == nk - 1)
        def _():
            r = acc[...]
            if add is not None:
                r = r + c_ref[...]
            o_ref[...] = r.astype(out_dtype)

    if mode == "tn":
        a_spec = pl.BlockSpec((tk, tm), lambda i, j, q: (q, i))
    else:
        a_spec = pl.BlockSpec((tm, tk), lambda i, j, q: (i, q))
    if mode == "nt":
        b_blk, b_idx = (tn, tk), (lambda i, j, q: (b_off // tn + j, q))
    else:
        b_blk, b_idx = (tk, tn), (lambda i, j, q: (b_off // tk + q, j))
    if b_win and b.arena.ndim == 3:
        bl = b.l
        b_spec = pl.BlockSpec((None,) + b_blk, lambda i, j, q: (bl,) + b_idx(i, j, q))
    else:
        b_spec = pl.BlockSpec(b_blk, b_idx)
    plain_o = pl.BlockSpec((tm, tn), lambda i, j, q: (i, j))
    in_specs = [a_spec, b_spec] + ([plain_o] if add is not None else [])
    args = (a, b.arena if b_win else b) + ((add,) if add is not None else ())
    if after is not None:
        in_specs.append(pl.BlockSpec(after.shape, lambda i, j, q: (0, 0)))
        args = args + (after,)
    aliases = {}
    if out is None:
        o_spec, o_shape = plain_o, jax.ShapeDtypeStruct((m, n), out_dtype)
    else:
        ol = out.l
        if out.arena.ndim == 3:
            o_spec = pl.BlockSpec((None, tm, tn), lambda i, j, q: (ol, o_off // tm + i, j))
        else:
            o_spec = pl.BlockSpec((tm, tn), lambda i, j, q: (o_off // tm + i, j))
        o_shape = jax.ShapeDtypeStruct(out.arena.shape, out_dtype)
        aliases = {len(args): 0}
        in_specs.append(pl.BlockSpec(memory_space=pl.ANY))
        args = args + (out.arena,)
    return pl.pallas_call(
        body, name=name, out_shape=o_shape,
        grid=(m // tm, n // tn, nk), in_specs=in_specs, out_specs=o_spec,
        scratch_shapes=[pltpu.VMEM((tm, tn), F32)] if nk > 1 else [], input_output_aliases=aliases,
        compiler_params=_params(("parallel", "parallel", "arbitrary")),
    )(*args)


def _row_spec(cols, tr=ROW_TILE):
    return pl.BlockSpec((tr, cols), lambda i: (i, 0))


def _vec_spec(cols):
    return pl.BlockSpec((1, cols), lambda i: (0, 0))


def _rms_fwd(x, g, name):
    t, d = x.shape

    def body(x_ref, g_ref, o_ref):
        xv = x_ref[...]
        r = lax.rsqrt(jnp.mean(xv * xv, axis=1, keepdims=True) + EPS)
        o_ref[...] = (xv * r * g_ref[...]).astype(BF16)

    return pl.pallas_call(
        body, name=name, out_shape=jax.ShapeDtypeStruct((t, d), BF16), grid=(t // ROW_TILE,),
        in_specs=[_row_spec(d), _vec_spec(d)], out_specs=_row_spec(d), compiler_params=_params(("parallel",)),
    )(x, g)


def _rms_bwd(x, g, dxn, dres, name):
    t, d = x.shape

    def body(x_ref, g_ref, dy_ref, dr_ref, dx_ref, dg_ref):
        @pl.when(pl.program_id(0) == 0)
        def _():
            dg_ref[...] = jnp.zeros_like(dg_ref)

        xv = x_ref[...]
        dy = dy_ref[...]
        r = lax.rsqrt(jnp.mean(xv * xv, axis=1, keepdims=True) + EPS)
        w = dy * g_ref[...]
        dx_ref[...] = dr_ref[...] + r * w - xv * (r * r * r) * jnp.mean(w * xv, axis=1, keepdims=True)
        dg_ref[...] += jnp.sum(dy * xv * r, axis=0, keepdims=True)

    return pl.pallas_call(
        body, name=name,
        out_shape=(jax.ShapeDtypeStruct((t, d), F32), jax.ShapeDtypeStruct((1, d), F32)), grid=(t // ROW_TILE,),
        in_specs=[_row_spec(d), _vec_spec(d), _row_spec(d), _row_spec(d)], out_specs=(_row_spec(d), _vec_spec(d)),
        compiler_params=_params(("arbitrary",)),
    )(x, g, dxn, dres)


def _loss_head(x, g, tgt, name):
    t, d = x.shape

    def body(x_ref, g_ref, t_ref, loss_ref, dx_ref, dg_ref):
        @pl.when(pl.program_id(0) == 0)
        def _():
            dg_ref[...] = jnp.zeros_like(dg_ref)
            loss_ref[...] = jnp.zeros_like(loss_ref)

        xv = x_ref[...]
        gv = g_ref[...]
        r = lax.rsqrt(jnp.mean(xv * xv, axis=1, keepdims=True) + EPS)
        e = xv * r * gv - t_ref[...]
        loss_ref[...] += jnp.full(loss_ref.shape, (0.5 / d) * jnp.sum(e * e), F32)
        dy = e * (1.0 / d)
        w = dy * gv
        dx_ref[...] = r * w - xv * (r * r * r) * jnp.mean(w * xv, axis=1, keepdims=True)
        dg_ref[...] += jnp.sum(dy * xv * r, axis=0, keepdims=True)

    return pl.pallas_call(
        body, name=name,
        out_shape=(jax.ShapeDtypeStruct((1, LANE), F32), jax.ShapeDtypeStruct((t, d), F32),
                   jax.ShapeDtypeStruct((1, d), F32)),
        grid=(t // ROW_TILE,), in_specs=[_row_spec(d), _vec_spec(d), _row_spec(d)],
        out_specs=(_vec_spec(LANE), _row_spec(d), _vec_spec(d)), compiler_params=_params(("arbitrary",)),
    )(x, g, tgt)


def _swiglu_fwd(gg, uu, name):
    t, f = gg.shape

    def body(g_ref, u_ref, o_ref):
        gv = g_ref[...]
        o_ref[...] = (gv * _sig(gv) * u_ref[...]).astype(BF16)

    return pl.pallas_call(
        body, name=name, out_shape=jax.ShapeDtypeStruct((t, f), BF16), grid=(t // ROW_TILE,),
        in_specs=[_row_spec(f), _row_spec(f)], out_specs=_row_spec(f), compiler_params=_params(("parallel",)),
    )(gg, uu)


def _swiglu_bwd(gg, uu, dact, name):
    t, f = gg.shape

    def body(g_ref, u_ref, d_ref, dg_ref, du_ref):
        gv = g_ref[...]
        dv = d_ref[...]
        s = _sig(gv)
        dg_ref[...] = (dv * u_ref[...] * s * (1.0 + gv * (1.0 - s))).astype(BF16)
        du_ref[...] = (dv * gv * s).astype(BF16)

    return pl.pallas_call(
        body, name=name,
        out_shape=(jax.ShapeDtypeStruct((t, f), BF16), jax.ShapeDtypeStruct((t, f), BF16)), grid=(t // ROW_TILE,),
        in_specs=[_row_spec(f)] * 3, out_specs=(_row_spec(f), _row_spec(f)), compiler_params=_params(("parallel",)),
    )(gg, uu, dact)


MERGE_COLS = 256
MERGE_ROWS = 1024


def _gate_specs(mr):
    nb = D_MODEL // MERGE_COLS
    base = OFF_GL // MERGE_COLS
    return [pl.BlockSpec((mr, MERGE_COLS), functools.partial(lambda i, j, kk: (i, base + nb * kk + j), kk=kk))
            for kk in range(4)]


def _merge_fwd(proj, ys, name):
    t = proj.shape[0]
    mr = min(t, MERGE_ROWS)
    yspec = pl.BlockSpec((mr, MERGE_COLS), lambda i, j: (i, j))

    def body(g0, g1, g2, g3, y0, y1, y2, y3, o_ref):
        acc = _sig(g0[...]) * y0[...]
        acc += _sig(g1[...]) * y1[...]
        acc += _sig(g2[...]) * y2[...]
        acc += _sig(g3[...]) * y3[...]
        o_ref[...] = acc.astype(BF16)

    return pl.pallas_call(
        body, name=name, out_shape=jax.ShapeDtypeStruct((t, D_MODEL), BF16),
        grid=(t // mr, D_MODEL // MERGE_COLS), in_specs=_gate_specs(mr) + [yspec] * 4, out_specs=yspec,
        compiler_params=_params(("parallel", "parallel")),
    )(proj, proj, proj, proj, *ys)


def _merge_bwd(proj, ys, dmerged, name):
    t = proj.shape[0]
    mr = min(t, MERGE_ROWS)
    yspec = pl.BlockSpec((mr, MERGE_COLS), lambda i, j: (i, j))

    def body(g0, g1, g2, g3, y0, y1, y2, y3, dm_ref, *outs):
        dm = dm_ref[...]
        for gr, yr, dy_ref, dg_ref in zip((g0, g1, g2, g3), (y0, y1, y2, y3), outs[:4], outs[4:]):
            s = _sig(gr[...])
            dy_ref[...] = (dm * s).astype(BF16)
            dg_ref[...] = (dm * yr[...] * s * (1.0 - s)).astype(BF16)

    shp = jax.ShapeDtypeStruct((t, D_MODEL), BF16)
    outs = pl.pallas_call(
        body, name=name, out_shape=(shp,) * 8, grid=(t // mr, D_MODEL // MERGE_COLS),
        in_specs=_gate_specs(mr) + [yspec] * 5, out_specs=(yspec,) * 8, compiler_params=_params(("parallel", "parallel")),
    )(proj, proj, proj, proj, *ys, dmerged)
    return outs[:4], outs[4:]


def _ln_silu_fwd(cd, g, b, name):
    t, c = cd.shape

    def body(x_ref, g_ref, b_ref, o_ref):
        xv = x_ref[...]
        mu = jnp.mean(xv, axis=1, keepdims=True)
        xc = xv - mu
        rs = lax.rsqrt(jnp.mean(xc * xc, axis=1, keepdims=True) + EPS)
        z = xc * rs * g_ref[...] + b_ref[...]
        o_ref[...] = (z * _sig(z)).astype(BF16)

    return pl.pallas_call(
        body, name=name, out_shape=jax.ShapeDtypeStruct((t, c), BF16), grid=(t // ROW_TILE,),
        in_specs=[_row_spec(c), _vec_spec(c), _vec_spec(c)], out_specs=_row_spec(c),
        compiler_params=_params(("parallel",)),
    )(cd, g, b)


def _ln_silu_bwd(cd, g, b, dy, name):
    t, c = cd.shape

    def body(x_ref, g_ref, b_ref, dy_ref, dx_ref, dg_ref, db_ref):
        @pl.when(pl.program_id(0) == 0)
        def _():
            dg_ref[...] = jnp.zeros_like(dg_ref)
            db_ref[...] = jnp.zeros_like(db_ref)

        xv = x_ref[...]
        gv = g_ref[...]
        mu = jnp.mean(xv, axis=1, keepdims=True)
        xc = xv - mu
        rs = lax.rsqrt(jnp.mean(xc * xc, axis=1, keepdims=True) + EPS)
        xh = xc * rs
        z = xh * gv + b_ref[...]
        s = _sig(z)
        dz = dy_ref[...] * s * (1.0 + z * (1.0 - s))
        dg_ref[...] += jnp.sum(dz * xh, axis=0, keepdims=True)
        db_ref[...] += jnp.sum(dz, axis=0, keepdims=True)
        dxh = dz * gv
        dx_ref[...] = rs * (dxh - jnp.mean(dxh, axis=1, keepdims=True) - xh * jnp.mean(dxh * xh, axis=1, keepdims=True))

    return pl.pallas_call(
        body, name=name,
        out_shape=(jax.ShapeDtypeStruct((t, c), F32), jax.ShapeDtypeStruct((1, c), F32),
                   jax.ShapeDtypeStruct((1, c), F32)),
        grid=(t // ROW_TILE,), in_specs=[_row_spec(c), _vec_spec(c), _vec_spec(c), _row_spec(c)],
        out_specs=(_row_spec(c), _vec_spec(c), _vec_spec(c)), compiler_params=_params(("arbitrary",)),
    )(cd, g, b, dy)


def _shift_dn(x, k):
    if k == 0:
        return x
    row = lax.broadcasted_iota(jnp.int32, x.shape, 0)
    return jnp.where(row >= k, pltpu.roll(x, k, 0), 0.0)


def _shift_up(x, k):
    if k == 0:
        return x
    t = x.shape[0]
    row = lax.broadcasted_iota(jnp.int32, x.shape, 0)
    return jnp.where(row < t - k, pltpu.roll(x, t - k, 0), 0.0)


def _conv_fwd(x, w_ref, taps):
    acc = w_ref[pl.ds(taps - 1, 1), :] * x
    for k in range(taps - 1):
        acc += w_ref[pl.ds(k, 1), :] * _shift_dn(x, taps - 1 - k)
    return acc


def _conv_bwd(x, dy, w_ref, dw_ref, taps):
    dx = w_ref[pl.ds(taps - 1, 1), :] * dy
    dw_ref[pl.ds(taps - 1, 1), :] = jnp.sum(dy * x, axis=0, keepdims=True)
    for k in range(taps - 1):
        s = taps - 1 - k
        dx += w_ref[pl.ds(k, 1), :] * _shift_up(dy, s)
        dw_ref[pl.ds(k, 1), :] = jnp.sum(dy * _shift_dn(x, s), axis=0, keepdims=True)
    return dx


def _scan_fwd(a, u):
    t = a.shape[0]
    k = 1
    while k < t:
        u = u + a * _shift_dn(u, k)
        if 2 * k < t:
            a = a * _shift_dn(a, k)
        k *= 2
    return u


def _scan_rev(a, u):
    t = a.shape[0]
    k = 1
    while k < t:
        u = u + a * _shift_up(u, k)
        if 2 * k < t:
            a = a * _shift_up(a, k)
        k *= 2
    return u


def _one_minus_exp(y):
    return jnp.where(y > -1e-3, -(y + 0.5 * y * y + (1.0 / 6.0) * y * y * y), 1.0 - jnp.exp(y))


GELU_C = math.sqrt(2.0 / math.pi)


def _gelu(x):
    th = jnp.tanh(GELU_C * (x + 0.044715 * x * x * x))
    return 0.5 * x * (1.0 + th), th


def _softplus(x):
    return jnp.maximum(x, 0.0) + jnp.log(1.0 + jnp.exp(-jnp.abs(x)))


def _chunk_spec(t, blk0):
    return pl.BlockSpec((t, LANE), functools.partial(lambda c, b: (0, b + c), b=blk0))


def _tap_spec(taps):
    return pl.BlockSpec((taps, LANE), lambda c: (0, c))


def _cvec_spec():
    return pl.BlockSpec((1, LANE), lambda c: (0, c))


def _cmat_spec():
    return pl.BlockSpec((1, LANE, LANE), lambda c: (c, 0, 0))


def _lru_forward(ax, wA_ref, bA_ref, wx_ref, bx_ref, wa_ref, ba_ref, lam_ref, h=None):
    ca = _conv_fwd(ax, wA_ref, CONV_A) + bA_ref[...]
    gi = _sig(_dot(ca, wx_ref[0], NN) + bx_ref[...])
    gr = _sig(_dot(ca, wa_ref[0], NN) + ba_ref[...])
    sp = _softplus(-lam_ref[...])
    la = -LRU_C * gr * sp
    a = jnp.exp(la)
    mult = jnp.sqrt(_one_minus_exp(2.0 * la))
    if h is None:
        h = _scan_fwd(a, ca * gi * mult)
    return ca, gi, gr, sp, a, mult, h


def _a_fwd(proj, wA, bA, wx, bx, wa, ba, lam, name):
    t = proj.shape[0]

    def body(ax_ref, ag_ref, wA_ref, bA_ref, wx_ref, bx_ref, wa_ref, ba_ref, lam_ref, o_ref, h_ref):
        h = _lru_forward(ax_ref[...], wA_ref, bA_ref, wx_ref, bx_ref, wa_ref, ba_ref, lam_ref)[-1]
        h_ref[...] = h
        o_ref[...] = (h * _gelu(ag_ref[...])[0]).astype(BF16)

    return pl.pallas_call(
        body, name=name, out_shape=(jax.ShapeDtypeStruct((t, BW), BF16), jax.ShapeDtypeStruct((t, BW), F32)),
        grid=(BW // LANE,),
        in_specs=[_chunk_spec(t, C_AX), _chunk_spec(t, C_AG), _tap_spec(CONV_A), _cvec_spec(), _cmat_spec(),
                  _cvec_spec(), _cmat_spec(), _cvec_spec(), _cvec_spec()],
        out_specs=(_chunk_spec(t, 0), _chunk_spec(t, 0)), compiler_params=_params(("parallel",)),
    )(proj, proj, wA, bA, wx, bx, wa, ba, lam)


def _a_bwd(proj, h_fwd, dya, wA, bA, wx, bx, wa, ba, lam, name):
    t = proj.shape[0]

    def body(ax_ref, ag_ref, h_ref, dy_ref, wA_ref, bA_ref, wx_ref, bx_ref, wa_ref, ba_ref, lam_ref,
             dax_ref, dag_ref, dwA_ref, dbA_ref, dwx_ref, dbx_ref, dwa_ref, dba_ref, dlam_ref):
        ax = ax_ref[...]
        ag = ag_ref[...]
        dy = dy_ref[...]
        ca, gi, gr, sp, a, mult, h = _lru_forward(ax, wA_ref, bA_ref, wx_ref, bx_ref, wa_ref, ba_ref, lam_ref,
                                                  h_ref[...])
        gel, th = _gelu(ag)
        dgel = 0.5 * (1.0 + th) + 0.5 * ag * (1.0 - th * th) * GELU_C * (1.0 + 3.0 * 0.044715 * ag * ag)
        dag_ref[...] = (dy * h * dgel).astype(BF16)
        s = _scan_rev(_shift_up(a, 1), dy * gel)
        da = s * _shift_dn(h, 1)
        dca = s * gi * mult
        dgi = s * ca * mult
        dmult = s * ca * gi
        dla = da * a - dmult * a * a / mult
        dgr = dla * (-LRU_C * sp)
        dsp = jnp.sum(dla * (-LRU_C * gr), axis=0, keepdims=True)
        dlam_ref[...] = -_sig(-lam_ref[...]) * dsp
        dzi = dgi * gi * (1.0 - gi)
        dzr = dgr * gr * (1.0 - gr)
        dbx_ref[...] = jnp.sum(dzi, axis=0, keepdims=True)
        dba_ref[...] = jnp.sum(dzr, axis=0, keepdims=True)
        dwx_ref[0] = _dot(ca, dzi, TN)
        dwa_ref[0] = _dot(ca, dzr, TN)
        dca += _dot(dzi, wx_ref[0], NT) + _dot(dzr, wa_ref[0], NT)
        dbA_ref[...] = jnp.sum(dca, axis=0, keepdims=True)
        dax_ref[...] = _conv_bwd(ax, dca, wA_ref, dwA_ref, CONV_A).astype(BF16)

    big = jax.ShapeDtypeStruct((t, BW), BF16)
    vec = jax.ShapeDtypeStruct((1, BW), F32)
    mat = jax.ShapeDtypeStruct((BW // LANE, LANE, LANE), F32)
    return pl.pallas_call(
        body, name=name,
        out_shape=(big, big, jax.ShapeDtypeStruct((CONV_A, BW), F32), vec, mat, vec, mat, vec, vec),
        grid=(BW // LANE,),
        in_specs=[_chunk_spec(t, C_AX), _chunk_spec(t, C_AG), _chunk_spec(t, 0), _chunk_spec(t, 0), _tap_spec(CONV_A),
                  _cvec_spec(), _cmat_spec(), _cvec_spec(), _cmat_spec(), _cvec_spec(), _cvec_spec()],
        out_specs=(_chunk_spec(t, 0), _chunk_spec(t, 0), _tap_spec(CONV_A), _cvec_spec(), _cmat_spec(), _cvec_spec(),
                   _cmat_spec(), _cvec_spec(), _cvec_spec()),
        compiler_params=_params(("parallel",)),
    )(proj, proj, h_fwd, dya, wA, bA, wx, bx, wa, ba, lam)


def _b_fwd(proj, wB, name):
    t = proj.shape[0]

    def body(bv_ref, bc_ref, bb_ref, w_ref, o_ref):
        o_ref[...] = (bb_ref[...] * _conv_fwd(bc_ref[...] * bv_ref[...], w_ref, CONV_B)).astype(BF16)

    return pl.pallas_call(
        body, name=name, out_shape=jax.ShapeDtypeStruct((t, BW), BF16), grid=(BW // LANE,),
        in_specs=[_chunk_spec(t, C_BV), _chunk_spec(t, C_BC), _chunk_spec(t, C_BB), _tap_spec(CONV_B)],
        out_specs=_chunk_spec(t, 0), compiler_params=_params(("parallel",)),
    )(proj, proj, proj, wB)


def _b_bwd(proj, dyb, wB, name):
    t = proj.shape[0]

    def body(bv_ref, bc_ref, bb_ref, dy_ref, w_ref, dbv_ref, dbc_ref, dbb_ref, dw_ref):
        bv = bv_ref[...]
        bc = bc_ref[...]
        dy = dy_ref[...]
        p = bc * bv
        dbb_ref[...] = (dy * _conv_fwd(p, w_ref, CONV_B)).astype(BF16)
        dp = _conv_bwd(p, dy * bb_ref[...], w_ref, dw_ref, CONV_B)
        dbc_ref[...] = (dp * bv).astype(BF16)
        dbv_ref[...] = (dp * bc).astype(BF16)

    big = jax.ShapeDtypeStruct((t, BW), BF16)
    return pl.pallas_call(
        body, name=name, out_shape=(big, big, big, jax.ShapeDtypeStruct((CONV_B, BW), F32)), grid=(BW // LANE,),
        in_specs=[_chunk_spec(t, C_BV), _chunk_spec(t, C_BC), _chunk_spec(t, C_BB), _chunk_spec(t, 0),
                  _tap_spec(CONV_B)],
        out_specs=(_chunk_spec(t, 0),) * 3 + (_tap_spec(CONV_B),), compiler_params=_params(("parallel",)),
    )(proj, proj, proj, dyb, wB)


def _d_conv_fwd(proj, wD, bD, name):
    t = proj.shape[0]

    def body(d1_ref, d2_ref, w_ref, b_ref, o_ref):
        o_ref[...] = _conv_fwd(d1_ref[...] * _sig(d2_ref[...]), w_ref, CONV_D) + b_ref[...]

    return pl.pallas_call(
        body, name=name, out_shape=jax.ShapeDtypeStruct((t, BW), F32), grid=(BW // LANE,),
        in_specs=[_chunk_spec(t, C_D1), _chunk_spec(t, C_D2), _tap_spec(CONV_D), _cvec_spec()],
        out_specs=_chunk_spec(t, 0), compiler_params=_params(("parallel",)),
    )(proj, proj, wD, bD)


def _d_conv_bwd(proj, dcd, wD, name):
    t = proj.shape[0]

    def body(d1_ref, d2_ref, dy_ref, w_ref, dd1_ref, dd2_ref, dw_ref, db_ref):
        d1 = d1_ref[...]
        s = _sig(d2_ref[...])
        dy = dy_ref[...]
        db_ref[...] = jnp.sum(dy, axis=0, keepdims=True)
        dd = _conv_bwd(d1 * s, dy, w_ref, dw_ref, CONV_D)
        dd1_ref[...] = (dd * s).astype(BF16)
        dd2_ref[...] = (dd * d1 * s * (1.0 - s)).astype(BF16)

    big = jax.ShapeDtypeStruct((t, BW), BF16)
    return pl.pallas_call(
        body, name=name,
        out_shape=(big, big, jax.ShapeDtypeStruct((CONV_D, BW), F32), jax.ShapeDtypeStruct((1, BW), F32)),
        grid=(BW // LANE,),
        in_specs=[_chunk_spec(t, C_D1), _chunk_spec(t, C_D2), _chunk_spec(t, 0), _tap_spec(CONV_D)],
        out_specs=(_chunk_spec(t, 0), _chunk_spec(t, 0), _tap_spec(CONV_D), _cvec_spec()),
        compiler_params=_params(("parallel",)),
    )(proj, proj, dcd, wD)


SCALE = HEAD_DIM ** -0.5
GROUP = N_Q // N_KV


GROWS = GROUP * BLK


def _per_head(ss_ref, row, g):
    head = lax.broadcasted_iota(jnp.int32, (GROWS, 1), 0) // BLK
    col = jnp.full((GROWS, 1), ss_ref[row, g * GROUP + GROUP - 1], F32)
    for i in range(GROUP - 1):
        col = jnp.where(head == i, ss_ref[row, g * GROUP + i], col)
    return col


def _attn_probs(q_ref, k_ref, ss_ref, g, n):
    qi = lax.broadcasted_iota(jnp.int32, (GROWS, BLK), 0) % BLK
    ki = lax.broadcasted_iota(jnp.int32, (GROWS, BLK), 1)
    dist = (qi - ki).astype(F32)
    sink = _per_head(ss_ref, 0, g)
    slope = _per_head(ss_ref, 1, g)
    s0 = pl.multiple_of(n * BLK, BLK)
    sp = pl.multiple_of(jnp.maximum(n - 1, 0) * BLK, BLK)
    q = q_ref[:, pl.ds(s0, BLK), :].reshape(GROWS, HEAD_DIM)
    kc = k_ref[0, pl.ds(s0, BLK), :]
    kp = k_ref[0, pl.ds(sp, BLK), :]
    sc = jnp.where(ki <= qi, _dot(q, kc, NT) * SCALE - slope * dist, NEG_INF)
    first = jnp.where(n >= 1, 0, BLK)
    sv = jnp.where(ki > qi + first, _dot(q, kp, NT) * SCALE - slope * (dist + BLK), NEG_INF)
    m = jnp.maximum(jnp.maximum(jnp.max(sc, axis=1, keepdims=True), jnp.max(sv, axis=1, keepdims=True)), sink)
    pc = jnp.exp(sc - m)
    pp = jnp.exp(sv - m)
    ps = jnp.exp(sink - m)
    z = jnp.sum(pc, axis=1, keepdims=True) + jnp.sum(pp, axis=1, keepdims=True) + ps
    return s0, sp, q, kc, kp, pc, pp, ps, z


def _attn_specs(t):
    qs = pl.BlockSpec((GROUP, t, HEAD_DIM), lambda g: (g, 0, 0))
    ks = pl.BlockSpec((1, t, HEAD_DIM), lambda g: (g, 0, 0))
    ss = pl.BlockSpec(memory_space=pltpu.SMEM)
    return qs, ks, ss


def _attn_fwd(q, k, v, ss, name):
    t = q.shape[1]
    qs, ks, sspec = _attn_specs(t)

    def body(q_ref, k_ref, v_ref, ss_ref, o_ref):
        g = pl.program_id(0)

        def blk(n, carry):
            s0, sp, _, _, _, pc, pp, _, z = _attn_probs(q_ref, k_ref, ss_ref, g, n)
            o = _dot(pc, v_ref[0, pl.ds(s0, BLK), :], NN) + _dot(pp, v_ref[0, pl.ds(sp, BLK), :], NN)
            o_ref[:, pl.ds(s0, BLK), :] = (o / z).astype(BF16).reshape(GROUP, BLK, HEAD_DIM)
            return carry

        lax.fori_loop(0, t // BLK, blk, 0)

    return pl.pallas_call(
        body, name=name, out_shape=jax.ShapeDtypeStruct((N_Q, t, HEAD_DIM), BF16), grid=(N_KV,),
        in_specs=[qs, ks, ks, sspec], out_specs=qs, compiler_params=_params(("parallel",)),
    )(q, k, v, ss)


def _attn_bwd(q, k, v, do, ss, name):
    t = q.shape[1]
    qs, ks, sspec = _attn_specs(t)

    def body(q_ref, k_ref, v_ref, do_ref, ss_ref, dq_ref, dk_ref, dv_ref, ds_ref):
        g = pl.program_id(0)
        dk_ref[...] = jnp.zeros_like(dk_ref)
        dv_ref[...] = jnp.zeros_like(dv_ref)

        def blk(n, dsink):
            s0, sp, q, kc, kp, pc, pp, ps, z = _attn_probs(q_ref, k_ref, ss_ref, g, n)
            rz = 1.0 / z
            pc = pc * rz
            pp = pp * rz
            do_b = do_ref[:, pl.ds(s0, BLK), :].reshape(GROWS, HEAD_DIM)
            dpc = _dot(do_b, v_ref[0, pl.ds(s0, BLK), :], NT)
            dpp = _dot(do_b, v_ref[0, pl.ds(sp, BLK), :], NT)
            delta = jnp.sum(pc * dpc, axis=1, keepdims=True) + jnp.sum(pp * dpp, axis=1, keepdims=True)
            dsc = pc * (dpc - delta)
            dsp = pp * (dpp - delta)
            dq = (_dot(dsc, kc, NN) + _dot(dsp, kp, NN)) * SCALE
            dq_ref[:, pl.ds(s0, BLK), :] = dq.astype(BF16).reshape(GROUP, BLK, HEAD_DIM)
            dk_ref[0, pl.ds(s0, BLK), :] += _dot(dsc, q, TN) * SCALE
            dk_ref[0, pl.ds(sp, BLK), :] += _dot(dsp, q, TN) * SCALE
            dv_ref[0, pl.ds(s0, BLK), :] += _dot(pc, do_b, TN)
            dv_ref[0, pl.ds(sp, BLK), :] += _dot(pp, do_b, TN)
            return dsink - ps * rz * delta

        dsink = lax.fori_loop(0, t // BLK, blk, jnp.zeros((GROWS, 1), F32))
        for i in range(GROUP):
            ds_ref[i] = jnp.full(ds_ref.shape[1:], jnp.sum(dsink[i * BLK:(i + 1) * BLK]), F32)

    kv = jax.ShapeDtypeStruct((N_KV, t, HEAD_DIM), F32)
    return pl.pallas_call(
        body, name=name,
        out_shape=(jax.ShapeDtypeStruct((N_Q, t, HEAD_DIM), BF16), kv, kv, jax.ShapeDtypeStruct((N_Q, 8, LANE), F32)),
        grid=(N_KV,), in_specs=[qs, ks, ks, qs, sspec],
        out_specs=(qs, ks, ks, pl.BlockSpec((GROUP, 8, LANE), lambda g: (g, 0, 0))),
        compiler_params=_params(("parallel",)),
    )(q, k, v, do, ss)


def _heads(x2d, n):
    t = x2d.shape[0]
    return x2d.reshape(t, n, HEAD_DIM).transpose(1, 0, 2)


def _unheads(x3d):
    n, t, _ = x3d.shape
    return x3d.transpose(1, 0, 2).reshape(t, n * HEAD_DIM)


SMALL_ELEMS = 256 * 1024
TILE_ELEMS = 320 * 1024


def _row_tile(r, c):
    if r * c <= SMALL_ELEMS:
        return r
    return _pick(r, [t for t in (512, 256, 128, 64, 32, 16, 8) if t * c <= TILE_ELEMS])


def _adamw_update(w, gv, m, v):
    nm = ADAM_B1 * m + (1.0 - ADAM_B1) * gv
    nv = ADAM_B2 * v + (1.0 - ADAM_B2) * (gv * gv)
    m_hat = nm / (1.0 - ADAM_B1 ** ADAM_STEP)
    v_hat = nv / (1.0 - ADAM_B2 ** ADAM_STEP)
    return -ADAM_LR * (m_hat / (jnp.sqrt(v_hat) + ADAM_EPS) + ADAM_WD * w), nm, nv


def _adamw(w, g, m, v, name):
    r, c = w.shape
    tr = _row_tile(r, c)
    spec = pl.BlockSpec((tr, c), lambda i: (i, 0))

    def body(w_ref, g_ref, m_ref, v_ref, d_ref, nm_ref, nv_ref):
        d_ref[...], nm_ref[...], nv_ref[...] = _adamw_update(w_ref[...], g_ref[...], m_ref[...], v_ref[...])

    shp = jax.ShapeDtypeStruct((r, c), F32)
    return pl.pallas_call(
        body, name=name, out_shape=(shp, shp, shp), grid=(r // tr,), in_specs=[spec] * 4, out_specs=(spec,) * 3,
        compiler_params=_params(("parallel",)),
    )(w, g, m, v)


def _adamw_layer(w, m, v, g, l, prev, name):
    r, c = g.shape
    tr = _row_tile(r, c)
    layer = pl.BlockSpec((tr, c), lambda i: (l * (r // tr) + i, 0))

    def body(w_ref, m_ref, v_ref, g_ref, *rest):
        go_ref, d_ref, nm_ref, nv_ref, token = rest[-5:]
        gv = g_ref[...]
        go_ref[...] = gv
        token[...] = jnp.zeros_like(token)
        d_ref[...], nm_ref[...], nv_ref[...] = _adamw_update(w_ref[...], gv, m_ref[...], v_ref[...])

    carried = list(prev[:4]) if prev is not None else []
    shp = jax.ShapeDtypeStruct(w.shape, F32)
    return pl.pallas_call(
        body, name=name, out_shape=(shp,) * 4 + (jax.ShapeDtypeStruct((8, LANE), F32),), grid=(r // tr,),
        in_specs=[layer] * 3 + [pl.BlockSpec((tr, c), lambda i: (i, 0))] + [pl.BlockSpec(memory_space=pl.ANY)] * len(carried),
        out_specs=(layer,) * 4 + (pl.BlockSpec((8, LANE), lambda i: (0, 0)),),
        input_output_aliases={4 + i: i for i in range(len(carried))}, compiler_params=_params(("arbitrary",)),
    )(w, m, v, g, *carried)


def _sum_leading(x, name):
    n, r, c = x.shape
    tr = _row_tile(r, c)

    def body(x_ref, o_ref):
        acc = x_ref[0]
        for i in range(1, n):
            acc = acc + x_ref[i]
        o_ref[...] = acc

    return pl.pallas_call(
        body, name=name, out_shape=jax.ShapeDtypeStruct((r, c), F32), grid=(r // tr,),
        in_specs=[pl.BlockSpec((n, tr, c), lambda i: (0, i, 0))], out_specs=pl.BlockSpec((tr, c), lambda i: (i, 0)),
        compiler_params=_params(("parallel",)),
    )(x)


def _sum_own_plus(p, sel, recv, name, out_dtype):
    _, r, c = p.shape
    n = recv.shape[0]
    tr = _pick(r, (512, 448, 256, 128, 64, 16))

    def body(sel_ref, p_ref, r_ref, o_ref):
        acc = p_ref[0].astype(F32)
        for i in range(n):
            acc = acc + r_ref[i].astype(F32)
        o_ref[...] = acc.astype(out_dtype)

    grid_spec = pltpu.PrefetchScalarGridSpec(
        num_scalar_prefetch=1, grid=(r // tr,),
        in_specs=[pl.BlockSpec((1, tr, c), lambda i, s: (s[0], i, 0)), pl.BlockSpec((n, tr, c), lambda i, s: (0, i, 0))],
        out_specs=pl.BlockSpec((tr, c), lambda i, s: (i, 0)))
    return pl.pallas_call(
        body, name=name, out_shape=jax.ShapeDtypeStruct((r, c), out_dtype), grid_spec=grid_spec,
        compiler_params=_params(("parallel",)),
    )(sel, p, recv)


def _coords():
    return lax.axis_index("x"), lax.axis_index("y"), lax.axis_index("c")


def _allgather8(x2, name, space):
    _, m, n = x2.shape

    def body(x_ref, out_ref, send_sems, recv_sems, local_sem):
        x, y, c = _coords()
        me, sibling = (x, y, c), (x, y, 1 - c)
        chips = [(1 - x, y), (x, 1 - y), (1 - x, 1 - y)]
        mine_src = x_ref.at[c]

        def rows(px, py, pc):
            return out_ref.at[4 * px + 2 * py + pc]

        def copy(k, block, to, src=None):
            return pltpu.make_async_remote_copy(
                src_ref=rows(*block) if src is None else src, dst_ref=rows(*block),
                send_sem=send_sems.at[k], recv_sem=recv_sems.at[k], device_id=to, device_id_type=MESH)

        mine = pltpu.make_async_copy(mine_src, rows(*me), local_sem)
        mine.start()
        first = [copy(0, me, sibling, src=mine_src)]
        first += [copy(1 + j, me, (*chip, c), src=mine_src) for j, chip in enumerate(chips)]
        for cp in first:
            cp.start()
        passed = [copy(4 + j, (*chip, c), sibling) for j, chip in enumerate(chips)]
        for j, chip in enumerate(chips):
            copy(1 + j, (*chip, c), me).wait_recv()
            passed[j].start()
        copy(0, sibling, me).wait_recv()
        for j, chip in enumerate(chips):
            copy(4 + j, (*chip, 1 - c), me).wait_recv()
        for cp in first + passed:
            cp.wait_send()
        mine.wait()

    return pl.pallas_call(
        body, name=name, out_shape=jax.ShapeDtypeStruct((8, m, n), x2.dtype),
        in_specs=[pl.BlockSpec(memory_space=space)], out_specs=pl.BlockSpec(memory_space=space),
        scratch_shapes=[pltpu.SemaphoreType.DMA((7,)), pltpu.SemaphoreType.DMA((7,)), pltpu.SemaphoreType.DMA],
        compiler_params=pltpu.CompilerParams(vmem_limit_bytes=VMEM_LIMIT),
    )(x2)


N_REG = len(ROW_REGIONS) + 1


def _chip_window(ref, lead, r, j):
    view = ref if lead is None else ref.at[lead]
    if r < len(ROW_REGIONS):
        off, rows = ROW_REGIONS[r]
        return view.at[pl.ds(pl.multiple_of(off + j * rows, 16), rows), :]
    return view.at[pl.ds(R_OUT, OUT_ROWS), pl.ds(pl.multiple_of(j * OUT_COLS, LANE), OUT_COLS)]


HBM_SPEC = pl.BlockSpec(memory_space=pltpu.HBM)
SEM_SPEC = pl.BlockSpec(memory_space=pltpu.SEMAPHORE)


def _half_window(ref, r, j, h):
    if r < len(ROW_REGIONS):
        off, rows = ROW_REGIONS[r]
        return ref.at[pl.ds(pl.multiple_of(off + j * rows + h * (rows // 2), 16), rows // 2), :]
    half = OUT_ROWS // 2
    return ref.at[pl.ds(pl.multiple_of(R_OUT + h * half, 16), half),
                  pl.ds(pl.multiple_of(j * OUT_COLS, LANE), OUT_COLS)]


def _other_chips():
    x, y, _ = _coords()
    return [(1 - x, y), (x, 1 - y), (1 - x, 1 - y)]


def _ici_copies(srcs, arena_ref, send_sems, recv_sems, regions):
    x, y, c = _coords()
    sends, arrivals = [], []
    for k, (cx, cy) in enumerate(_other_chips()):
        for r in regions:
            def remote(src, j):
                return pltpu.make_async_remote_copy(
                    src_ref=src, dst_ref=_half_window(arena_ref, r, j, c), send_sem=send_sems.at[3 * r + k],
                    recv_sem=recv_sems.at[3 * r + k], device_id=(cx, cy, c), device_id_type=MESH)
            rows = srcs[r].shape[0] // 2
            sends.append(remote(srcs[r].at[pl.ds(pl.multiple_of(c * rows, 16), rows), :], 2 * x + y))
            arrivals.append(remote(_half_window(arena_ref, r, 2 * cx + cy, c), 2 * cx + cy))
    return sends, arrivals


def _sibling_copies(srcs, arena_ref, send_sems, recv_sems, regions):
    x, y, c = _coords()
    sends, arrivals = [], []

    def remote(win, r, k, src=None):
        return pltpu.make_async_remote_copy(
            src_ref=win if src is None else src, dst_ref=win, send_sem=send_sems.at[r, k],
            recv_sem=recv_sems.at[r, k], device_id=(x, y, 1 - c), device_id_type=MESH)

    for k, (cx, cy) in enumerate(_other_chips()):
        for r in regions:
            sends.append(remote(_half_window(arena_ref, r, 2 * cx + cy, c), r, k))
            arrivals.append(remote(_half_window(arena_ref, r, 2 * cx + cy, 1 - c), r, k))
    for r in regions:
        own = _chip_window(arena_ref, None, r, 2 * x + y)
        sends.append(remote(own, r, 3, src=srcs[r]))
        arrivals.append(remote(own, r, 3))
    return sends, arrivals


ICI_SEMS = pltpu.SemaphoreType.DMA((3 * N_REG,))
SIBLING_SEMS = pltpu.SemaphoreType.DMA((N_REG, 4))
ARENA_SHAPE = (ARENA_ROWS, ARENA_W)
ALL_REGIONS = tuple(range(N_REG))
IN_REGION = (3,)
REST_REGIONS = (0, 1, 2, 4, 5)


def _gather_layer(shards, name, regions=ALL_REGIONS):
    def body(*refs):
        srcs, arena_ref = refs[:N_REG], refs[N_REG]
        ici_send, ici_recv, sib_send, sib_recv = refs[N_REG + 1:]
        sends, arrivals = _ici_copies(srcs, arena_ref, ici_send, ici_recv, regions)
        passes, landings = _sibling_copies(srcs, arena_ref, sib_send, sib_recv, regions)
        for cp in sends + passes[len(arrivals):]:
            cp.start()
        for arrival, onward in zip(arrivals, passes):
            arrival.wait_recv()
            onward.start()
        for cp in landings:
            cp.wait_recv()
        for cp in sends + passes:
            cp.wait_send()

    return pl.pallas_call(
        body, name=name, out_shape=jax.ShapeDtypeStruct(ARENA_SHAPE, BF16),
        in_specs=[pl.BlockSpec(memory_space=pl.ANY)] * N_REG, out_specs=pl.BlockSpec(memory_space=pl.ANY),
        scratch_shapes=[ICI_SEMS, ICI_SEMS, SIBLING_SEMS, SIBLING_SEMS],
    )(*shards)


def _gather_start(shards, after, name, regions=ALL_REGIONS):
    def body(*refs):
        srcs, arena_ref = refs[:N_REG], refs[N_REG]
        send_sems, recv_sems = refs[N_REG + 2], refs[N_REG + 3]
        token = refs[-1]
        for cp in _ici_copies(srcs, arena_ref, send_sems, recv_sems, regions)[0]:
            cp.start()
        token[...] = jnp.zeros_like(token)

    hbm = lambda a: pltpu.with_memory_space_constraint(a, pltpu.HBM)
    outs = pl.pallas_call(
        body, name=name,
        out_shape=(ICI_SEMS, ICI_SEMS, *[pltpu.HBM(s.shape, s.dtype) for s in shards],
                   pltpu.HBM(ARENA_SHAPE, BF16), pltpu.HBM(after.shape, after.dtype),
                   jax.ShapeDtypeStruct((8, LANE), F32)),
        in_specs=[HBM_SPEC] * (N_REG + 2),
        out_specs=(SEM_SPEC, SEM_SPEC, *[HBM_SPEC] * (N_REG + 2), pl.BlockSpec(memory_space=pltpu.VMEM)),
        input_output_aliases={i: 2 + i for i in range(N_REG + 2)},
        compiler_params=pltpu.CompilerParams(has_side_effects=pltpu.SideEffectType.DATAFLOW_SIDE_EFFECTING),
    )(*[hbm(s) for s in shards], hbm(lax.empty(ARENA_SHAPE, BF16)), hbm(after))
    return outs[0], outs[1], outs[2:2 + N_REG], outs[2 + N_REG], outs[-1], outs[3 + N_REG]


def _gather_wait(send_sems, recv_sems, shards, arena, after, name, regions=ALL_REGIONS):
    def body(*refs):
        srcs, arena_ref = refs[:N_REG], refs[N_REG]
        sends, arrivals = _ici_copies(srcs, arena_ref, refs[N_REG + 1], refs[N_REG + 2], regions)
        for cp in sends:
            cp.wait_send()
        for cp in arrivals:
            cp.wait_recv()

    outs = pl.pallas_call(
        body, name=name,
        out_shape=(*[pltpu.HBM(s.shape, s.dtype) for s in shards], pltpu.HBM(ARENA_SHAPE, BF16)),
        in_specs=[HBM_SPEC] * (N_REG + 1) + [SEM_SPEC, SEM_SPEC, pl.BlockSpec(memory_space=pl.ANY)],
        out_specs=(HBM_SPEC,) * (N_REG + 1), input_output_aliases={i: i for i in range(N_REG + 1)},
        compiler_params=pltpu.CompilerParams(has_side_effects=pltpu.SideEffectType.DATAFLOW_SIDE_EFFECTING),
    )(*shards, arena, send_sems, recv_sems, after)
    return outs[:N_REG], outs[N_REG]


def _gather_finish(shards, arena, name, regions=ALL_REGIONS):
    def body(*refs):
        srcs, arena_ref = refs[:N_REG], refs[N_REG + 1]
        sends, arrivals = _sibling_copies(srcs, arena_ref, refs[N_REG + 2], refs[N_REG + 3], regions)
        for cp in sends:
            cp.start()
        for cp in arrivals:
            cp.wait_recv()
        for cp in sends:
            cp.wait_send()

    return pl.pallas_call(
        body, name=name, out_shape=jax.ShapeDtypeStruct(ARENA_SHAPE, BF16),
        in_specs=[pl.BlockSpec(memory_space=pl.ANY)] * (N_REG + 1), out_specs=pl.BlockSpec(memory_space=pl.ANY),
        scratch_shapes=[SIBLING_SEMS, SIBLING_SEMS], input_output_aliases={N_REG: 0},
    )(*shards, arena)


HALF_PIECE_OFF = tuple(o // 2 for o in PIECE_OFF)
HALF_PIECE_ROWS = PIECE_ROWS // 2
HALF_OUT_ROWS = OUT_ROWS // 2
SWAP_SEMS = pltpu.SemaphoreType.DMA((4 * N_REG,))
SCATTER_SEMS = pltpu.SemaphoreType.DMA((6,))


def _packed_shapes(slots, dtype):
    return (jax.ShapeDtypeStruct((slots, HALF_PIECE_ROWS, ARENA_W), dtype),
            jax.ShapeDtypeStruct((slots, HALF_OUT_ROWS, OUT_COLS), dtype))


def _swap_copies(g_ref, main_ref, outp_ref, send_sems, recv_sems):
    x, y, c = _coords()
    cps = []
    for j in range(4):
        for r in range(N_REG):
            if r < len(ROW_REGIONS):
                dst = main_ref.at[j, pl.ds(HALF_PIECE_OFF[r], ROW_REGIONS[r][1] // 2), :]
            else:
                dst = outp_ref.at[j]
            cps.append(pltpu.make_async_remote_copy(
                src_ref=_half_window(g_ref, r, j, 1 - c), dst_ref=dst, send_sem=send_sems.at[j * N_REG + r],
                recv_sem=recv_sems.at[j * N_REG + r], device_id=(x, y, 1 - c), device_id_type=MESH))
    return cps


def _swap_start(ga, name):
    def body(g_ref, main_ref, outp_ref, send_sems, recv_sems, *rest):
        for cp in _swap_copies(g_ref, main_ref, outp_ref, send_sems, recv_sems):
            cp.start()

    hbm = lambda a: pltpu.with_memory_space_constraint(a, pltpu.HBM)
    bufs = [ga] + [lax.empty(s.shape, s.dtype) for s in _packed_shapes(4, ga.dtype)]
    outs = pl.pallas_call(
        body, name=name, out_shape=(SWAP_SEMS, SWAP_SEMS, *[pltpu.HBM(b.shape, b.dtype) for b in bufs]),
        in_specs=[HBM_SPEC] * 3, out_specs=(SEM_SPEC, SEM_SPEC, *[HBM_SPEC] * 3),
        input_output_aliases={i: 2 + i for i in range(3)},
        compiler_params=pltpu.CompilerParams(has_side_effects=pltpu.SideEffectType.DATAFLOW_SIDE_EFFECTING),
    )(*[hbm(b) for b in bufs])
    return outs[0], outs[1], outs[2:]


def _swap_wait(send_sems, recv_sems, bufs, after, name):
    def body(g_ref, main_ref, outp_ref, send_sems, recv_sems, *rest):
        for cp in _swap_copies(g_ref, main_ref, outp_ref, send_sems, recv_sems):
            cp.wait_send()
            cp.wait_recv()

    return pl.pallas_call(
        body, name=name, out_shape=tuple(pltpu.HBM(b.shape, b.dtype) for b in bufs),
        in_specs=[HBM_SPEC] * 3 + [SEM_SPEC, SEM_SPEC, pl.BlockSpec(memory_space=pl.ANY)],
        out_specs=(HBM_SPEC,) * 3, input_output_aliases={i: i for i in range(3)},
        compiler_params=pltpu.CompilerParams(has_side_effects=pltpu.SideEffectType.DATAFLOW_SIDE_EFFECTING),
    )(*bufs, send_sems, recv_sems, after)


def _own_halves(ga, cc):
    mains = [jnp.concatenate([lax.dynamic_slice(ga, (off + j * rows + cc * (rows // 2), 0), (rows // 2, ARENA_W))
                              for off, rows in ROW_REGIONS]) for j in range(4)]
    outs = [lax.dynamic_slice(ga, (R_OUT + cc * HALF_OUT_ROWS, j * OUT_COLS), (HALF_OUT_ROWS, OUT_COLS))
            for j in range(4)]
    return jnp.stack(mains), jnp.stack(outs)


def _scatter_copies(main_ref, outp_ref, rmain_ref, routp_ref, send_sems, recv_sems):
    _, _, c = _coords()
    cps = []
    for k, (cx, cy) in enumerate(_other_chips()):
        for i, (src, dst) in enumerate(((main_ref, rmain_ref), (outp_ref, routp_ref))):
            cps.append(pltpu.make_async_remote_copy(
                src_ref=src.at[2 * cx + cy], dst_ref=dst.at[k], send_sem=send_sems.at[2 * k + i],
                recv_sem=recv_sems.at[2 * k + i], device_id=(cx, cy, c), device_id_type=MESH))
    return cps


def _scatter_start(main, outp, name):
    def body(main_ref, outp_ref, rmain_ref, routp_ref, send_sems, recv_sems, *rest):
        for cp in _scatter_copies(main_ref, outp_ref, rmain_ref, routp_ref, send_sems, recv_sems):
            cp.start()
        rest[-1][...] = jnp.zeros_like(rest[-1])

    hbm = lambda a: pltpu.with_memory_space_constraint(a, pltpu.HBM)
    land = [lax.empty(s.shape, s.dtype) for s in _packed_shapes(3, main.dtype)]
    bufs = [main, outp, *land]
    outs = pl.pallas_call(
        body, name=name,
        out_shape=(SCATTER_SEMS, SCATTER_SEMS, *[pltpu.HBM(b.shape, b.dtype) for b in bufs],
                   jax.ShapeDtypeStruct((8, LANE), F32)),
        in_specs=[HBM_SPEC] * 4, out_specs=(SEM_SPEC, SEM_SPEC, *[HBM_SPEC] * 4, pl.BlockSpec(memory_space=pltpu.VMEM)),
        input_output_aliases={i: 2 + i for i in range(4)},
        compiler_params=pltpu.CompilerParams(has_side_effects=pltpu.SideEffectType.DATAFLOW_SIDE_EFFECTING),
    )(*[hbm(b) for b in bufs])
    return outs[0], outs[1], outs[2:6], outs[6]


def _scatter_wait(send_sems, recv_sems, bufs, after, name):
    def body(main_ref, outp_ref, rmain_ref, routp_ref, send_sems, recv_sems, *rest):
        for cp in _scatter_copies(main_ref, outp_ref, rmain_ref, routp_ref, send_sems, recv_sems):
            cp.wait_send()
            cp.wait_recv()

    return pl.pallas_call(
        body, name=name, out_shape=tuple(pltpu.HBM(b.shape, b.dtype) for b in bufs),
        in_specs=[HBM_SPEC] * 4 + [SEM_SPEC, SEM_SPEC, pl.BlockSpec(memory_space=pl.ANY)],
        out_specs=(HBM_SPEC,) * 4, input_output_aliases={i: i for i in range(4)},
        compiler_params=pltpu.CompilerParams(has_side_effects=pltpu.SideEffectType.DATAFLOW_SIDE_EFFECTING),
    )(*bufs, send_sems, recv_sems, after)


def _swap_many(arrs, name):
    n = len(arrs)

    def body(*refs):
        x, y, c = _coords()
        send_sems, recv_sems = refs[2 * n], refs[2 * n + 1]
        cps = [pltpu.make_async_remote_copy(
            src_ref=refs[i], dst_ref=refs[n + i], send_sem=send_sems.at[i], recv_sem=recv_sems.at[i],
            device_id=(x, y, 1 - c), device_id_type=MESH) for i in range(n)]
        for cp in cps:
            cp.start()
        for cp in cps:
            cp.wait()

    return pl.pallas_call(
        body, name=name, out_shape=tuple(jax.ShapeDtypeStruct(a.shape, a.dtype) for a in arrs),
        in_specs=[pl.BlockSpec(memory_space=pl.ANY)] * n, out_specs=(pl.BlockSpec(memory_space=pl.ANY),) * n,
        scratch_shapes=[pltpu.SemaphoreType.DMA((n,)), pltpu.SemaphoreType.DMA((n,))],
    )(*arrs)


def _join_halves(mine, theirs, cc):
    return jnp.where(cc == 0, jnp.concatenate([mine, theirs]), jnp.concatenate([theirs, mine]))


def _reduced_layer(red, sib, cc):
    parts = []
    for (_, rows), off in zip(ROW_REGIONS, HALF_PIECE_OFF):
        parts.append(_join_halves(red[0][off:off + rows // 2], sib[0][off:off + rows // 2], cc))
    return jnp.concatenate(parts), _join_halves(red[1], sib[1], cc)


OUT_NAMES = ("w_a_out", "w_b_out", "w_c_out", "w_d_out")


def _arena_shards(w):
    t = lambda a: a.astype(BF16).transpose(0, 2, 1)
    return (w["w_ffn_down"].astype(BF16), t(w["w_ffn_gate"]), t(w["w_ffn_up"]), t(w["w_in"]), w["w_o"].astype(BF16),
            jnp.concatenate([w[n].astype(BF16) for n in OUT_NAMES], axis=1))


def _shard_grads(main, outp):
    t = lambda r: main[PIECE_OFF[r]:PIECE_OFF[r] + ROW_REGIONS[r][1]]
    g = dict(w_ffn_down=t(0), w_ffn_gate=t(1).T, w_ffn_up=t(2).T, w_in=t(3).T, w_o=t(4))
    for i, n in enumerate(OUT_NAMES):
        g[n] = outp[i * BW:(i + 1) * BW]
    return g


def _gather_taps(p, name):
    mine = jnp.concatenate([p[n] for n in CONV_NAMES], axis=1).reshape(DEPTH * N_TAPS, LANE)
    rows = -(-mine.shape[0] // 8) * 8
    mine = jnp.concatenate([mine, jnp.zeros((rows - mine.shape[0], LANE), F32)])
    g = _allgather8(jnp.stack([mine, mine]), name, pltpu.VMEM)[0::2, :DEPTH * N_TAPS]
    full = g.reshape(4, DEPTH, N_TAPS, LANE).transpose(1, 2, 0, 3).reshape(DEPTH, N_TAPS, BW)
    return dict(conv_a_w=full[:, :CONV_A], conv_b_w=full[:, CONV_A:CONV_A + CONV_B], conv_d_w=full[:, CONV_A + CONV_B:])


def _flat_pack(arrs):
    flat = jnp.concatenate([a.reshape(-1).astype(F32) for a in arrs])
    rows = -(-flat.shape[0] // (8 * LANE)) * 8
    return jnp.concatenate([flat, jnp.zeros((rows * LANE - flat.shape[0],), F32)]).reshape(rows, LANE)


def _flat_unpack(packed, shapes):
    flat, out, off = packed.reshape(-1), [], 0
    for s in shapes:
        cnt = int(np.prod(s))
        out.append(flat[off:off + cnt].reshape(s))
        off += cnt
    return out


def _blockdiag_chunks(w):
    w4 = w.reshape(4, 2, 64, 64)
    z = jnp.zeros((4, 2, 64, 2, 64), F32)
    z = z.at[:, 0, :, 0, :].set(w4[:, 0]).at[:, 1, :, 1, :].set(w4[:, 1])
    return z.reshape(4, LANE, LANE)


def _blockdiag_extract(d):
    d5 = d.reshape(4, 2, 64, 2, 64)
    return jnp.stack([d5[:, 0, :, 0, :], d5[:, 1, :, 1, :]], axis=1).reshape(8, 64, 64)


SLOPES = np.asarray([2.0 ** (-8.0 * (i + 1) / N_Q) for i in range(N_Q)], np.float32)


def _layer_consts(p, fw, l):
    row = lambda a: a[l].reshape(1, -1)
    return dict(
        g1=row(p["norm1_g"]), g2=row(p["norm2_g"]), wA=fw["conv_a_w"][l], bA=row(p["conv_a_b"]),
        wx=_blockdiag_chunks(p["lru_wx"][l]), bx=row(p["lru_bx"]), wa=_blockdiag_chunks(p["lru_wa"][l]),
        ba=row(p["lru_ba"]), lam=row(p["lru_lambda"]), wB=fw["conv_b_w"][l],
        ss=jnp.stack([p["sinks"][l], jnp.asarray(SLOPES)]), wD=fw["conv_d_w"][l], bD=row(p["conv_d_b"]),
        lg=row(p["ln_d_g"]), lb=row(p["ln_d_b"]))


def _layer_fwd(x, c, fw, l, rest_of_weights=None):
    t = f"l{l}_"
    xn = _rms_fwd(x, c["g1"], t + "rms1")
    wt = lambda off, rows: Win(fw["arena"][l], None, off, rows)
    proj = _mm(xn, Win(fw["arena_in"][l], None, R_IN, IN_W), "nt", t + "proj")
    ya, lru_h = _a_fwd(proj, c["wA"], c["bA"], c["wx"], c["bx"], c["wa"], c["ba"], c["lam"], t + "a_fwd")
    yb = _b_fwd(proj, c["wB"], t + "b_fwd")
    q3 = _heads(proj[:, OFF_Q:OFF_K], N_Q)
    k3 = _heads(proj[:, OFF_K:OFF_V], N_KV)
    v3 = _heads(proj[:, OFF_V:OFF_V + N_KV * HEAD_DIM], N_KV)
    yc = _unheads(_attn_fwd(q3, k3, v3, c["ss"], t + "attn_fwd"))
    cd = _d_conv_fwd(proj, c["wD"], c["bD"], t + "d_conv_fwd")
    yd = _ln_silu_fwd(cd, c["lg"], c["lb"], t + "d_ln_fwd")
    ys = (ya, yb, yc, yd)
    if fw["arena"][l] is None:
        fw["arena"][l] = rest_of_weights(yd)
    big_y = tuple(_mm(y, wt(R_OUT + i * BW, BW), "nn", t + f"out{i}") for i, y in enumerate(ys))
    merged = _merge_fwd(proj, big_y, t + "merge_fwd")
    hres = _mm(merged, wt(R_O, D_MODEL), "nn", t + "wo", add=x)
    hn = _rms_fwd(hres, c["g2"], t + "rms2")
    gg = _mm(hn, wt(R_GATE, D_FF), "nt", t + "ffn_gate")
    uu = _mm(hn, wt(R_UP, D_FF), "nt", t + "ffn_up")
    act = _swiglu_fwd(gg, uu, t + "swiglu_fwd")
    xout = _mm(act, wt(R_DOWN, D_FF), "nn", t + "ffn_down", add=hres)
    saved = dict(x=x, xn=xn, proj=proj, ys=ys, q3=q3, k3=k3, v3=v3, cd=cd, big_y=big_y, merged=merged, hres=hres,
                 hn=hn, gg=gg, uu=uu, act=act, lru_h=lru_h)
    return xout, saved


def _layer_bwd(dxout, s, c, fw, l, ga, weight_grads_done=None):
    t = f"l{l}_"
    gs = {}
    wt = lambda off, rows: Win(fw["arena"][l], None, off, rows)
    gt = lambda off, rows: Win(ga, None, off, rows)
    dact = _mm(dxout, wt(R_DOWN, D_FF), "nt", t + "d_act")
    ga = _mm(s["act"], dxout, "tn", t + "dw_down", out=gt(R_DOWN, D_FF))
    dgg, duu = _swiglu_bwd(s["gg"], s["uu"], dact, t + "swiglu_bwd")
    ga = _mm(dgg, s["hn"], "tn", t + "dw_gate", out=gt(R_GATE, D_FF))
    ga = _mm(duu, s["hn"], "tn", t + "dw_up", out=gt(R_UP, D_FF))
    dhn = _mm(dgg, wt(R_GATE, D_FF), "nn", t + "d_hn_g")
    dhn = _mm(duu, wt(R_UP, D_FF), "nn", t + "d_hn_u", add=dhn)
    dhres, gs["norm2_g"] = _rms_bwd(s["hres"], c["g2"], dhn, dxout, t + "rms2_bwd")
    dmerged = _mm(dhres, wt(R_O, D_MODEL), "nt", t + "d_merged")
    ga = _mm(s["merged"], dhres, "tn", t + "dw_o", out=gt(R_O, D_MODEL))
    dbig_y, dgl = _merge_bwd(s["proj"], s["big_y"], dmerged, t + "merge_bwd")
    dys = []
    for i in range(4):
        ga = _mm(s["ys"][i], dbig_y[i], "tn", t + f"dw_out{i}", out=gt(R_OUT + i * BW, BW))
        dys.append(_mm(dbig_y[i], wt(R_OUT + i * BW, BW), "nt", t + f"d_y{i}"))
    proj = s["proj"]
    (dax, dag, gs["conv_a_w"], gs["conv_a_b"], dwx, gs["lru_bx"], dwa, gs["lru_ba"], gs["lru_lambda"]) = _a_bwd(
        proj, s["lru_h"], dys[0], c["wA"], c["bA"], c["wx"], c["bx"], c["wa"], c["ba"], c["lam"], t + "a_bwd")
    gs["lru_wx"] = _blockdiag_extract(dwx)
    gs["lru_wa"] = _blockdiag_extract(dwa)
    dbv, dbc, dbb, gs["conv_b_w"] = _b_bwd(proj, dys[1], c["wB"], t + "b_bwd")
    dq3, dk3, dv3, dsink = _attn_bwd(s["q3"], s["k3"], s["v3"], _heads(dys[2], N_Q), c["ss"], t + "attn_bwd")
    gs["sinks"] = dsink[:, 0, 0]
    dcd, gs["ln_d_g"], gs["ln_d_b"] = _ln_silu_bwd(s["cd"], c["lg"], c["lb"], dys[3], t + "d_ln_bwd")
    dd1, dd2, gs["conv_d_w"], gs["conv_d_b"] = _d_conv_bwd(proj, dcd, c["wD"], t + "d_conv_bwd")
    dproj = jnp.concatenate(
        [dax, dag, dbv, dbc, dbb, _unheads(dq3), _unheads(dk3).astype(BF16), _unheads(dv3).astype(BF16), dd1, dd2,
         *dgl], axis=1)
    ga = _mm(dproj, s["xn"], "tn", t + "dw_in", out=gt(R_IN, IN_W))
    token = weight_grads_done(ga) if weight_grads_done is not None else None
    dxn = _mm(dproj, Win(fw["arena_in"][l], None, R_IN, IN_W), "nn", t + "d_xn", after=token)
    dx, gs["norm1_g"] = _rms_bwd(s["x"], c["g1"], dxn, dhres, t + "rms1_bwd")
    return dx, ga, gs


def kernel(x, norm1_g, w_in, conv_a_w, conv_a_b, lru_wx, lru_bx, lru_wa, lru_ba, lru_lambda, w_a_out, conv_b_w, w_b_out, sinks, w_c_out, conv_d_w, conv_d_b, ln_d_g, ln_d_b, w_d_out, w_o, norm2_g, w_ffn_gate, w_ffn_up, w_ffn_down, final_g, loss_target, m_norm1_g, m_w_in, m_conv_a_w, m_conv_a_b, m_lru_wx, m_lru_bx, m_lru_wa, m_lru_ba, m_lru_lambda, m_w_a_out, m_conv_b_w, m_w_b_out, m_sinks, m_w_c_out, m_conv_d_w, m_conv_d_b, m_ln_d_g, m_ln_d_b, m_w_d_out, m_w_o, m_norm2_g, m_w_ffn_gate, m_w_ffn_up, m_w_ffn_down, m_final_g, v_norm1_g, v_w_in, v_conv_a_w, v_conv_a_b, v_lru_wx, v_lru_bx, v_lru_wa, v_lru_ba, v_lru_lambda, v_w_a_out, v_conv_b_w, v_w_b_out, v_sinks, v_w_c_out, v_conv_d_w, v_conv_d_b, v_ln_d_g, v_ln_d_b, v_w_d_out, v_w_o, v_norm2_g, v_w_ffn_gate, v_w_ffn_up, v_w_ffn_down, v_final_g):
    given = dict(locals())
    p = {n: given[n] for n in NAMES}
    mom = {n: given["m_" + n] for n in NAMES}
    var = {n: given["v_" + n] for n in NAMES}
    cx, cy, cc = _coords()
    chip = 2 * cx + cy

    shards = _arena_shards(p)
    fw = _gather_taps(p, "gather_taps")
    shards0, shards1 = [s[0] for s in shards], [s[1] for s in shards]
    flight0 = _gather_start(shards0, _gather_layer(shards0, "gather_l0_in", IN_REGION), "gather_l0_rest_start",
                            REST_REGIONS)
    fw["arena_in"] = [flight0[5], None]
    fw["arena"] = [None, None]
    consts = [_layer_consts(p, fw, l) for l in range(DEPTH)]
    consts[0]["g1"] = consts[0]["g1"] + flight0[4][0:1, 0:1]
    flight1 = []

    def rest_of_layer0(after):
        sh, landing = _gather_wait(*flight0[:4], after, "gather_l0_rest_wait", REST_REGIONS)
        arena = _gather_finish(sh, landing, "gather_l0_rest_finish", REST_REGIONS)
        flight1.extend(_gather_start(shards1, arena, "gather_l1_start"))
        return flight1[5]

    h = x[0]
    saved = []
    for l in range(DEPTH):
        if l == 1:
            sh, landing = _gather_wait(*flight1[:4], h, "gather_l1_wait")
            fw["arena"][1] = fw["arena_in"][1] = _gather_finish(sh, landing, "gather_l1_finish")
        h, s = _layer_fwd(h, consts[l], fw, l, rest_of_layer0)
        saved.append(s)
    loss_vec, dh, g_final = _loss_head(h, final_g.reshape(1, -1), loss_target[0], "loss_head")
    loss = lax.psum(loss_vec[0, 0], ("x", "y", "c"))

    zero = jnp.zeros((1,), jnp.int32)
    chip_sel = chip.reshape(1).astype(jnp.int32)

    def chip_sums(ga, t):
        ss, rs, flying = _swap_start(ga, t + "grads_swap_start")
        own = _own_halves(flying[0], cc)
        got = _swap_wait(ss, rs, flying, own[0], t + "grads_swap_wait")[1:]
        return [_sum_own_plus(o.reshape((1, -1, o.shape[-1])), zero, r.reshape((1, -1, r.shape[-1])),
                              t + f"grads_sum_chip{i}", BF16).reshape(o.shape) for i, (o, r) in enumerate(zip(own, got))]

    def all_sums(sums, got, t):
        return [_sum_own_plus(s, chip_sel, r, t + f"grads_sum_all{i}", F32) for i, (s, r) in enumerate(zip(sums, got))]

    gss = [None] * DEPTH
    dh, ga1, gss[1] = _layer_bwd(dh, saved[1], consts[1], fw, 1, lax.empty(ARENA_SHAPE, BF16))
    send_sems, recv_sems, bufs, token = _scatter_start(*chip_sums(ga1, "l1_"), "l1_grads_scatter_start")
    scatter0 = []

    def start_layer0_scatter(ga0):
        scatter0.extend(_scatter_start(*chip_sums(ga0, "l0_"), "l0_grads_scatter_start"))
        return scatter0[3]

    dh, _, gss[0] = _layer_bwd(dh + token[0:1, 0:1], saved[0], consts[0], fw, 0, lax.empty(ARENA_SHAPE, BF16),
                               start_layer0_scatter)
    grad_x = dh[None]

    flat = lambda a: a.reshape(-1, a.shape[-1])

    def finish_layer(bufs_l, l, carried, t):
        red = all_sums(bufs_l[:2], bufs_l[2:], t)
        g_l = _shard_grads(*_reduced_layer(red, _swap_many(red, t + "grads_swap_reduced"), cc))
        return {n: _adamw_layer(flat(p[n]), flat(mom[n]), flat(var[n]), g_l[n], l, carried and carried[n],
                                t + "adamw_" + n) for n in BIG}

    big = finish_layer(_scatter_wait(send_sems, recv_sems, bufs, scatter0[3], "l1_grads_scatter_wait"), 1, None, "l1_")
    done1 = big["w_in"][4][0:1, 0:1]

    small_full = {n: (g_final.reshape(-1) if n == "final_g" else
                      jnp.stack([gss[l][n].reshape(gss[l][n].shape[-2:] if n.startswith("conv") and n.endswith("_w")
                                                   else p[n].shape[1:]) for l in range(DEPTH)]))
                  for n in SMALL}
    part = _flat_pack([small_full[n] for n in SMALL]) + done1
    rows = part.shape[0]
    gathered = _allgather8(jnp.stack([part, part]), "gather_small_grads", pltpu.VMEM)
    small_packed = _sum_leading(gathered, "small_grads_sum")
    small_sum = _flat_unpack(small_packed, [small_full[n].shape for n in SMALL])

    big = finish_layer(_scatter_wait(*scatter0[:3], small_packed, "l0_grads_scatter_wait"), 0, big, "l0_")
    g, delta, new_m, new_v = ({n: big[n][i].reshape(p[n].shape) for n in BIG} for i in range(4))
    for n, a in zip(SMALL, small_sum):
        g[n] = lax.dynamic_slice_in_dim(a, chip * LANE, LANE, axis=2) if n in CONV_NAMES else a

    shapes = [p[n].shape for n in SMALL]
    d, nm, nv = _adamw(_flat_pack([p[n] for n in SMALL]), _flat_pack([g[n] for n in SMALL]),
                       _flat_pack([mom[n] for n in SMALL]), _flat_pack([var[n] for n in SMALL]), "adamw_small")
    for n, a, b, cval in zip(SMALL, _flat_unpack(d, shapes), _flat_unpack(nm, shapes), _flat_unpack(nv, shapes)):
        delta[n], new_m[n], new_v[n] = a, b, cval

    return (loss, grad_x, *[g[n] for n in NAMES], *[delta[n] for n in NAMES], *[new_m[n] for n in NAMES],
            *[new_v[n] for n in NAMES])
```

```python
import functools
import math

import numpy as np
import jax
import jax.numpy as jnp
from jax import lax
from jax.experimental import pallas as pl
from jax.experimental.pallas import tpu as pltpu

F32 = jnp.float32
BF16 = jnp.bfloat16
MESH = pl.DeviceIdType.MESH

D_MODEL = 1024
DEPTH = 2
BW = 512
HEAD_DIM = 64
N_Q = 8
N_KV = 2
BLK = 128
D_FF = 2816
IN_W = 8448
EPS = 1e-6
NEG_INF = -1e30
LRU_C = 8.0
CONV_A, CONV_B, CONV_D = 4, 3, 31
LANE = 128
ROW_TILE = 256
VMEM_LIMIT = 56 * 1024 * 1024
MM_VMEM_BUDGET = 36 * 1024 * 1024

C_AX, C_AG, C_BV, C_BC, C_BB = 0, 4, 8, 12, 16
OFF_Q, OFF_K, OFF_V = 2560, 3072, 3200
C_D1, C_D2 = 26, 30
OFF_GL = 4352

ADAM_LR, ADAM_B1, ADAM_B2, ADAM_EPS, ADAM_WD, ADAM_STEP = 0.001, 0.9, 0.999, 1e-08, 0.01, 10

ARENA_W = 1024
R_DOWN, R_GATE, R_UP, R_IN, R_O, R_OUT = 0, 2816, 5632, 8448, 16896, 17920
ARENA_ROWS = 19968
ROW_REGIONS = ((R_DOWN, 704), (R_GATE, 704), (R_UP, 704), (R_IN, 2112), (R_O, 256))
PIECE_OFF = (0, 704, 1408, 2112, 4224)
PIECE_ROWS = 4480
OUT_ROWS, OUT_COLS = 4 * BW, D_MODEL // 4

BIG = ("w_in", "w_a_out", "w_b_out", "w_c_out", "w_d_out", "w_o", "w_ffn_gate", "w_ffn_up", "w_ffn_down")
CONV_NAMES = ("conv_a_w", "conv_b_w", "conv_d_w")
N_TAPS = CONV_A + CONV_B + CONV_D
SMALL = ("norm1_g", "conv_a_w", "conv_a_b", "lru_wx", "lru_bx", "lru_wa", "lru_ba", "lru_lambda", "conv_b_w",
         "sinks", "conv_d_w", "conv_d_b", "ln_d_g", "ln_d_b", "norm2_g", "final_g")
NAMES = ['norm1_g', 'w_in', 'conv_a_w', 'conv_a_b', 'lru_wx', 'lru_bx', 'lru_wa', 'lru_ba', 'lru_lambda', 'w_a_out',
         'conv_b_w', 'w_b_out', 'sinks', 'w_c_out', 'conv_d_w', 'conv_d_b', 'ln_d_g', 'ln_d_b', 'w_d_out', 'w_o',
         'norm2_g', 'w_ffn_gate', 'w_ffn_up', 'w_ffn_down', 'final_g']


def _pick(n, cands, off=0):
    for c in cands:
        if n % c == 0 and off % c == 0:
            return c
    assert off == 0, (n, off)
    return n


class Win:
    def __init__(self, arena, l, off, rows):
        self.arena, self.l, self.off, self.rows = arena, l, off, rows
        self.shape = (rows, arena.shape[-1])


def _params(sem=None):
    return pltpu.CompilerParams(dimension_semantics=sem, vmem_limit_bytes=VMEM_LIMIT)


def _sig(z):
    return 1.0 / (1.0 + jnp.exp(-z))


def _dot(a, b, dims):
    return lax.dot_general(a.astype(BF16), b.astype(BF16), (dims, ((), ())), preferred_element_type=F32)


NN = ((1,), (0,))
NT = ((1,), (1,))
TN = ((0,), (0,))


def _mm(a, b, mode, name, out_dtype=F32, add=None, out=None, after=None):
    if mode == "nn":
        (m, k), n = a.shape, b.shape[1]
    elif mode == "nt":
        (m, k), n = a.shape, b.shape[0]
    else:
        (k, m), n = a.shape, b.shape[1]
    b_win = isinstance(b, Win)
    b_off = b.off if b_win else 0
    o_off = out.off if out is not None else 0
    if out is not None:
        out_dtype = out.arena.dtype
    tk = _pick(k, (2816, 2048, 1408, 1024, 768, 512, 256), b_off if mode != "nt" else 0)
    nk = k // tk
    n_off = b_off if mode == "nt" else 0
    a_bytes, b_bytes, o_bytes = a.dtype.itemsize, 2, jnp.dtype(out_dtype).itemsize

    def vmem_bytes(tm_, tn_):
        tile = tm_ * tn_
        return (2 * tk * (tm_ * a_bytes + tn_ * b_bytes) + 2 * tile * o_bytes + (tile * 4 if nk > 1 else 0)
                + (2 * tile * 4 if add is not None else 0) + tile * 4)

    pairs = [(tm_, tn_) for tm_ in (2048, 1024, 768, 512, 256, 128) for tn_ in (1024, 768, 512, 256, 128)
             if m % tm_ == 0 and o_off % tm_ == 0 and n % tn_ == 0 and n_off % tn_ == 0
             and vmem_bytes(tm_, tn_) <= MM_VMEM_BUDGET]
    tm, tn = max(pairs, key=lambda p: (p[0] * p[1], p[0]))
    dims = {"nn": NN, "nt": NT, "tn": TN}[mode]

    def body(*refs):
        a_ref, b_ref = refs[:2]
        c_ref = refs[2] if add is not None else None
        if nk == 1:
            r = _dot(a_ref[...], b_ref[...], dims)
            if add is not None:
                r = r + c_ref[...]
            refs[-1][...] = r.astype(out_dtype)
            return
        o_ref, acc = refs[-2:]
        kk = pl.program_id(2)

        @pl.when(kk == 0)
        def _():
            acc[...] = jnp.zeros_like(acc)

        acc[...] += _dot(a_ref[...], b_ref[...], dims)

        @pl.when(kk == nk - 1)
        def _():
            r = acc[...]
            if add is not None:
                r = r + c_ref[...]
            o_ref[...] = r.astype(out_dtype)

    if mode == "tn":
        a_spec = pl.BlockSpec((tk, tm), lambda i, j, q: (q, i))
    else:
        a_spec = pl.BlockSpec((tm, tk), lambda i, j, q: (i, q))
    if mode == "nt":
        b_blk, b_idx = (tn, tk), (lambda i, j, q: (b_off // tn + j, q))
    else:
        b_blk, b_idx = (tk, tn), (lambda i, j, q: (b_off // tk + q, j))
    if b_win and b.arena.ndim == 3:
        bl = b.l
        b_spec = pl.BlockSpec((None,) + b_blk, lambda i, j, q: (bl,) + b_idx(i, j, q))
    else:
        b_spec = pl.BlockSpec(b_blk, b_idx)
    plain_o = pl.BlockSpec((tm, tn), lambda i, j, q: (i, j))
    in_specs = [a_spec, b_spec] + ([plain_o] if add is not None else [])
    args = (a, b.arena if b_win else b) + ((add,) if add is not None else ())
    if after is not None:
        in_specs.append(pl.BlockSpec(after.shape, lambda i, j, q: (0, 0)))
        args = args + (after,)
    aliases = {}
    if out is None:
        o_spec, o_shape = plain_o, jax.ShapeDtypeStruct((m, n), out_dtype)
    else:
        ol = out.l
        if out.arena.ndim == 3:
            o_spec = pl.BlockSpec((None, tm, tn), lambda i, j, q: (ol, o_off // tm + i, j))
        else:
            o_spec = pl.BlockSpec((tm, tn), lambda i, j, q: (o_off // tm + i, j))
        o_shape = jax.ShapeDtypeStruct(out.arena.shape, out_dtype)
        aliases = {len(args): 0}
        in_specs.append(pl.BlockSpec(memory_space=pl.ANY))
        args = args + (out.arena,)
    return pl.pallas_call(
        body, name=name, out_shape=o_shape,
        grid=(m // tm, n // tn, nk), in_specs=in_specs, out_specs=o_spec,
        scratch_shapes=[pltpu.VMEM((tm, tn), F32)] if nk > 1 else [], input_output_aliases=aliases,
        compiler_params=_params(("parallel", "parallel", "arbitrary")),
    )(*args)


def _row_spec(cols, tr=ROW_TILE):
    return pl.BlockSpec((tr, cols), lambda i: (i, 0))


def _vec_spec(cols):
    return pl.BlockSpec((1, cols), lambda i: (0, 0))


def _rms_fwd(x, g, name):
    t, d = x.shape

    def body(x_ref, g_ref, o_ref):
        xv = x_ref[...]
        r = lax.rsqrt(jnp.mean(xv * xv, axis=1, keepdims=True) + EPS)
        o_ref[...] = (xv * r * g_ref[...]).astype(BF16)

    return pl.pallas_call(
        body, name=name, out_shape=jax.ShapeDtypeStruct((t, d), BF16), grid=(t // ROW_TILE,),
        in_specs=[_row_spec(d), _vec_spec(d)], out_specs=_row_spec(d), compiler_params=_params(("parallel",)),
    )(x, g)


def _rms_bwd(x, g, dxn, dres, name):
    t, d = x.shape

    def body(x_ref, g_ref, dy_ref, dr_ref, dx_ref, dxb_ref, dg_ref):
        @pl.when(pl.program_id(0) == 0)
        def _():
            dg_ref[...] = jnp.zeros_like(dg_ref)

        xv = x_ref[...]
        dy = dy_ref[...]
        r = lax.rsqrt(jnp.mean(xv * xv, axis=1, keepdims=True) + EPS)
        w = dy * g_ref[...]
        dx = dr_ref[...] + r * w - xv * (r * r * r) * jnp.mean(w * xv, axis=1, keepdims=True)
        dx_ref[...] = dx
        dxb_ref[...] = dx.astype(BF16)
        dg_ref[...] += jnp.sum(dy * xv * r, axis=0, keepdims=True)

    return pl.pallas_call(
        body, name=name,
        out_shape=(jax.ShapeDtypeStruct((t, d), F32), jax.ShapeDtypeStruct((t, d), BF16),
                   jax.ShapeDtypeStruct((1, d), F32)), grid=(t // ROW_TILE,),
        in_specs=[_row_spec(d), _vec_spec(d), _row_spec(d), _row_spec(d)],
        out_specs=(_row_spec(d), _row_spec(d), _vec_spec(d)), compiler_params=_params(("arbitrary",)),
    )(x, g, dxn, dres)


def _loss_head(x, g, tgt, name):
    t, d = x.shape

    def body(x_ref, g_ref, t_ref, loss_ref, dx_ref, dg_ref):
        @pl.when(pl.program_id(0) == 0)
        def _():
            dg_ref[...] = jnp.zeros_like(dg_ref)
            loss_ref[...] = jnp.zeros_like(loss_ref)

        xv = x_ref[...]
        gv = g_ref[...]
        r = lax.rsqrt(jnp.mean(xv * xv, axis=1, keepdims=True) + EPS)
        e = xv * r * gv - t_ref[...]
        loss_ref[...] += jnp.full(loss_ref.shape, (0.5 / d) * jnp.sum(e * e), F32)
        dy = e * (1.0 / d)
        w = dy * gv
        dx_ref[...] = r * w - xv * (r * r * r) * jnp.mean(w * xv, axis=1, keepdims=True)
        dg_ref[...] += jnp.sum(dy * xv * r, axis=0, keepdims=True)

    return pl.pallas_call(
        body, name=name,
        out_shape=(jax.ShapeDtypeStruct((1, LANE), F32), jax.ShapeDtypeStruct((t, d), F32),
                   jax.ShapeDtypeStruct((1, d), F32)),
        grid=(t // ROW_TILE,), in_specs=[_row_spec(d), _vec_spec(d), _row_spec(d)],
        out_specs=(_vec_spec(LANE), _row_spec(d), _vec_spec(d)), compiler_params=_params(("arbitrary",)),
    )(x, g, tgt)


def _swiglu_fwd(gg, uu, name):
    t, f = gg.shape

    def body(g_ref, u_ref, o_ref):
        gv = g_ref[...]
        o_ref[...] = (gv * _sig(gv) * u_ref[...]).astype(BF16)

    return pl.pallas_call(
        body, name=name, out_shape=jax.ShapeDtypeStruct((t, f), BF16), grid=(t // ROW_TILE,),
        in_specs=[_row_spec(f), _row_spec(f)], out_specs=_row_spec(f), compiler_params=_params(("parallel",)),
    )(gg, uu)


def _swiglu_bwd(gg, uu, dact, name):
    t, f = gg.shape

    def body(g_ref, u_ref, d_ref, dg_ref, du_ref):
        gv = g_ref[...]
        dv = d_ref[...]
        s = _sig(gv)
        dg_ref[...] = (dv * u_ref[...] * s * (1.0 + gv * (1.0 - s))).astype(BF16)
        du_ref[...] = (dv * gv * s).astype(BF16)

    return pl.pallas_call(
        body, name=name,
        out_shape=(jax.ShapeDtypeStruct((t, f), BF16), jax.ShapeDtypeStruct((t, f), BF16)), grid=(t // ROW_TILE,),
        in_specs=[_row_spec(f)] * 3, out_specs=(_row_spec(f), _row_spec(f)), compiler_params=_params(("parallel",)),
    )(gg, uu, dact)


MERGE_COLS = 256
MERGE_ROWS = 1024


def _gate_specs(mr):
    nb = D_MODEL // MERGE_COLS
    base = OFF_GL // MERGE_COLS
    return [pl.BlockSpec((mr, MERGE_COLS), functools.partial(lambda i, j, kk: (i, base + nb * kk + j), kk=kk))
            for kk in range(4)]


def _merge_fwd(proj, ys, name):
    t = proj.shape[0]
    mr = min(t, MERGE_ROWS)
    yspec = pl.BlockSpec((mr, MERGE_COLS), lambda i, j: (i, j))

    def body(g0, g1, g2, g3, y0, y1, y2, y3, o_ref):
        acc = _sig(g0[...]) * y0[...]
        acc += _sig(g1[...]) * y1[...]
        acc += _sig(g2[...]) * y2[...]
        acc += _sig(g3[...]) * y3[...]
        o_ref[...] = acc.astype(BF16)

    return pl.pallas_call(
        body, name=name, out_shape=jax.ShapeDtypeStruct((t, D_MODEL), BF16),
        grid=(t // mr, D_MODEL // MERGE_COLS), in_specs=_gate_specs(mr) + [yspec] * 4, out_specs=yspec,
        compiler_params=_params(("parallel", "parallel")),
    )(proj, proj, proj, proj, *ys)


def _merge_bwd(proj, ys, dmerged, name):
    t = proj.shape[0]
    mr = min(t, MERGE_ROWS)
    yspec = pl.BlockSpec((mr, MERGE_COLS), lambda i, j: (i, j))

    def body(g0, g1, g2, g3, y0, y1, y2, y3, dm_ref, *outs):
        dm = dm_ref[...]
        for gr, yr, dy_ref, dg_ref in zip((g0, g1, g2, g3), (y0, y1, y2, y3), outs[:4], outs[4:]):
            s = _sig(gr[...])
            dy_ref[...] = (dm * s).astype(BF16)
            dg_ref[...] = (dm * yr[...] * s * (1.0 - s)).astype(BF16)

    shp = jax.ShapeDtypeStruct((t, D_MODEL), BF16)
    outs = pl.pallas_call(
        body, name=name, out_shape=(shp,) * 8, grid=(t // mr, D_MODEL // MERGE_COLS),
        in_specs=_gate_specs(mr) + [yspec] * 5, out_specs=(yspec,) * 8, compiler_params=_params(("parallel", "parallel")),
    )(proj, proj, proj, proj, *ys, dmerged)
    return outs[:4], outs[4:]


def _ln_silu_fwd(cd, g, b, name):
    t, c = cd.shape

    def body(x_ref, g_ref, b_ref, o_ref):
        xv = x_ref[...]
        mu = jnp.mean(xv, axis=1, keepdims=True)
        xc = xv - mu
        rs = lax.rsqrt(jnp.mean(xc * xc, axis=1, keepdims=True) + EPS)
        z = xc * rs * g_ref[...] + b_ref[...]
        o_ref[...] = (z * _sig(z)).astype(BF16)

    return pl.pallas_call(
        body, name=name, out_shape=jax.ShapeDtypeStruct((t, c), BF16), grid=(t // ROW_TILE,),
        in_specs=[_row_spec(c), _vec_spec(c), _vec_spec(c)], out_specs=_row_spec(c),
        compiler_params=_params(("parallel",)),
    )(cd, g, b)


def _ln_silu_bwd(cd, g, b, dy, name):
    t, c = cd.shape

    def body(x_ref, g_ref, b_ref, dy_ref, dx_ref, dg_ref, db_ref):
        @pl.when(pl.program_id(0) == 0)
        def _():
            dg_ref[...] = jnp.zeros_like(dg_ref)
            db_ref[...] = jnp.zeros_like(db_ref)

        xv = x_ref[...]
        gv = g_ref[...]
        mu = jnp.mean(xv, axis=1, keepdims=True)
        xc = xv - mu
        rs = lax.rsqrt(jnp.mean(xc * xc, axis=1, keepdims=True) + EPS)
        xh = xc * rs
        z = xh * gv + b_ref[...]
        s = _sig(z)
        dz = dy_ref[...] * s * (1.0 + z * (1.0 - s))
        dg_ref[...] += jnp.sum(dz * xh, axis=0, keepdims=True)
        db_ref[...] += jnp.sum(dz, axis=0, keepdims=True)
        dxh = dz * gv
        dx_ref[...] = rs * (dxh - jnp.mean(dxh, axis=1, keepdims=True) - xh * jnp.mean(dxh * xh, axis=1, keepdims=True))

    return pl.pallas_call(
        body, name=name,
        out_shape=(jax.ShapeDtypeStruct((t, c), F32), jax.ShapeDtypeStruct((1, c), F32),
                   jax.ShapeDtypeStruct((1, c), F32)),
        grid=(t // ROW_TILE,), in_specs=[_row_spec(c), _vec_spec(c), _vec_spec(c), _row_spec(c)],
        out_specs=(_row_spec(c), _vec_spec(c), _vec_spec(c)), compiler_params=_params(("arbitrary",)),
    )(cd, g, b, dy)


def _shift_dn(x, k):
    if k == 0:
        return x
    row = lax.broadcasted_iota(jnp.int32, x.shape, 0)
    return jnp.where(row >= k, pltpu.roll(x, k, 0), 0.0)


def _shift_up(x, k):
    if k == 0:
        return x
    t = x.shape[0]
    row = lax.broadcasted_iota(jnp.int32, x.shape, 0)
    return jnp.where(row < t - k, pltpu.roll(x, t - k, 0), 0.0)


def _conv_fwd(x, w_ref, taps):
    acc = w_ref[pl.ds(taps - 1, 1), :] * x
    for k in range(taps - 1):
        acc += w_ref[pl.ds(k, 1), :] * _shift_dn(x, taps - 1 - k)
    return acc


def _conv_bwd(x, dy, w_ref, dw_ref, taps):
    dx = w_ref[pl.ds(taps - 1, 1), :] * dy
    dw_ref[pl.ds(taps - 1, 1), :] = jnp.sum(dy * x, axis=0, keepdims=True)
    for k in range(taps - 1):
        s = taps - 1 - k
        dx += w_ref[pl.ds(k, 1), :] * _shift_up(dy, s)
        dw_ref[pl.ds(k, 1), :] = jnp.sum(dy * _shift_dn(x, s), axis=0, keepdims=True)
    return dx


def _scan_fwd(a, u):
    t = a.shape[0]
    k = 1
    while k < t:
        u = u + a * _shift_dn(u, k)
        if 2 * k < t:
            a = a * _shift_dn(a, k)
        k *= 2
    return u


def _scan_rev(a, u):
    t = a.shape[0]
    k = 1
    while k < t:
        u = u + a * _shift_up(u, k)
        if 2 * k < t:
            a = a * _shift_up(a, k)
        k *= 2
    return u


def _one_minus_exp(y):
    return jnp.where(y > -1e-3, -(y + 0.5 * y * y + (1.0 / 6.0) * y * y * y), 1.0 - jnp.exp(y))


GELU_C = math.sqrt(2.0 / math.pi)


def _gelu(x):
    th = jnp.tanh(GELU_C * (x + 0.044715 * x * x * x))
    return 0.5 * x * (1.0 + th), th


def _softplus(x):
    return jnp.maximum(x, 0.0) + jnp.log(1.0 + jnp.exp(-jnp.abs(x)))


def _chunk_spec(t, blk0):
    return pl.BlockSpec((t, LANE), functools.partial(lambda c, b: (0, b + c), b=blk0))


def _tap_spec(taps):
    return pl.BlockSpec((taps, LANE), lambda c: (0, c))


def _cvec_spec():
    return pl.BlockSpec((1, LANE), lambda c: (0, c))


def _cmat_spec():
    return pl.BlockSpec((1, LANE, LANE), lambda c: (c, 0, 0))


def _lru_forward(ax, wA_ref, bA_ref, wx_ref, bx_ref, wa_ref, ba_ref, lam_ref, h=None):
    ca = _conv_fwd(ax, wA_ref, CONV_A) + bA_ref[...]
    gi = _sig(_dot(ca, wx_ref[0], NN) + bx_ref[...])
    gr = _sig(_dot(ca, wa_ref[0], NN) + ba_ref[...])
    sp = _softplus(-lam_ref[...])
    la = -LRU_C * gr * sp
    a = jnp.exp(la)
    mult = jnp.sqrt(_one_minus_exp(2.0 * la))
    if h is None:
        h = _scan_fwd(a, ca * gi * mult)
    return ca, gi, gr, sp, a, mult, h


def _a_fwd(proj, wA, bA, wx, bx, wa, ba, lam, name):
    t = proj.shape[0]

    def body(ax_ref, ag_ref, wA_ref, bA_ref, wx_ref, bx_ref, wa_ref, ba_ref, lam_ref, o_ref, h_ref):
        h = _lru_forward(ax_ref[...], wA_ref, bA_ref, wx_ref, bx_ref, wa_ref, ba_ref, lam_ref)[-1]
        h_ref[...] = h
        o_ref[...] = (h * _gelu(ag_ref[...])[0]).astype(BF16)

    return pl.pallas_call(
        body, name=name, out_shape=(jax.ShapeDtypeStruct((t, BW), BF16), jax.ShapeDtypeStruct((t, BW), F32)),
        grid=(BW // LANE,),
        in_specs=[_chunk_spec(t, C_AX), _chunk_spec(t, C_AG), _tap_spec(CONV_A), _cvec_spec(), _cmat_spec(),
                  _cvec_spec(), _cmat_spec(), _cvec_spec(), _cvec_spec()],
        out_specs=(_chunk_spec(t, 0), _chunk_spec(t, 0)), compiler_params=_params(("parallel",)),
    )(proj, proj, wA, bA, wx, bx, wa, ba, lam)


def _a_bwd(proj, h_fwd, dya, wA, bA, wx, bx, wa, ba, lam, name):
    t = proj.shape[0]

    def body(ax_ref, ag_ref, h_ref, dy_ref, wA_ref, bA_ref, wx_ref, bx_ref, wa_ref, ba_ref, lam_ref,
             dax_ref, dag_ref, dwA_ref, dbA_ref, dwx_ref, dbx_ref, dwa_ref, dba_ref, dlam_ref):
        ax = ax_ref[...]
        ag = ag_ref[...]
        dy = dy_ref[...]
        ca, gi, gr, sp, a, mult, h = _lru_forward(ax, wA_ref, bA_ref, wx_ref, bx_ref, wa_ref, ba_ref, lam_ref,
                                                  h_ref[...])
        gel, th = _gelu(ag)
        dgel = 0.5 * (1.0 + th) + 0.5 * ag * (1.0 - th * th) * GELU_C * (1.0 + 3.0 * 0.044715 * ag * ag)
        dag_ref[...] = (dy * h * dgel).astype(BF16)
        s = _scan_rev(_shift_up(a, 1), dy * gel)
        da = s * _shift_dn(h, 1)
        dca = s * gi * mult
        dgi = s * ca * mult
        dmult = s * ca * gi
        dla = da * a - dmult * a * a / mult
        dgr = dla * (-LRU_C * sp)
        dsp = jnp.sum(dla * (-LRU_C * gr), axis=0, keepdims=True)
        dlam_ref[...] = -_sig(-lam_ref[...]) * dsp
        dzi = dgi * gi * (1.0 - gi)
        dzr = dgr * gr * (1.0 - gr)
        dbx_ref[...] = jnp.sum(dzi, axis=0, keepdims=True)
        dba_ref[...] = jnp.sum(dzr, axis=0, keepdims=True)
        dwx_ref[0] = _dot(ca, dzi, TN)
        dwa_ref[0] = _dot(ca, dzr, TN)
        dca += _dot(dzi, wx_ref[0], NT) + _dot(dzr, wa_ref[0], NT)
        dbA_ref[...] = jnp.sum(dca, axis=0, keepdims=True)
        dax_ref[...] = _conv_bwd(ax, dca, wA_ref, dwA_ref, CONV_A).astype(BF16)

    big = jax.ShapeDtypeStruct((t, BW), BF16)
    vec = jax.ShapeDtypeStruct((1, BW), F32)
    mat = jax.ShapeDtypeStruct((BW // LANE, LANE, LANE), F32)
    return pl.pallas_call(
        body, name=name,
        out_shape=(big, big, jax.ShapeDtypeStruct((CONV_A, BW), F32), vec, mat, vec, mat, vec, vec),
        grid=(BW // LANE,),
        in_specs=[_chunk_spec(t, C_AX), _chunk_spec(t, C_AG), _chunk_spec(t, 0), _chunk_spec(t, 0), _tap_spec(CONV_A),
                  _cvec_spec(), _cmat_spec(), _cvec_spec(), _cmat_spec(), _cvec_spec(), _cvec_spec()],
        out_specs=(_chunk_spec(t, 0), _chunk_spec(t, 0), _tap_spec(CONV_A), _cvec_spec(), _cmat_spec(), _cvec_spec(),
                   _cmat_spec(), _cvec_spec(), _cvec_spec()),
        compiler_params=_params(("parallel",)),
    )(proj, proj, h_fwd, dya, wA, bA, wx, bx, wa, ba, lam)


def _b_fwd(proj, wB, name):
    t = proj.shape[0]

    def body(bv_ref, bc_ref, bb_ref, w_ref, o_ref):
        o_ref[...] = (bb_ref[...] * _conv_fwd(bc_ref[...] * bv_ref[...], w_ref, CONV_B)).astype(BF16)

    return pl.pallas_call(
        body, name=name, out_shape=jax.ShapeDtypeStruct((t, BW), BF16), grid=(BW // LANE,),
        in_specs=[_chunk_spec(t, C_BV), _chunk_spec(t, C_BC), _chunk_spec(t, C_BB), _tap_spec(CONV_B)],
        out_specs=_chunk_spec(t, 0), compiler_params=_params(("parallel",)),
    )(proj, proj, proj, wB)


def _b_bwd(proj, dyb, wB, name):
    t = proj.shape[0]

    def body(bv_ref, bc_ref, bb_ref, dy_ref, w_ref, dbv_ref, dbc_ref, dbb_ref, dw_ref):
        bv = bv_ref[...]
        bc = bc_ref[...]
        dy = dy_ref[...]
        p = bc * bv
        dbb_ref[...] = (dy * _conv_fwd(p, w_ref, CONV_B)).astype(BF16)
        dp = _conv_bwd(p, dy * bb_ref[...], w_ref, dw_ref, CONV_B)
        dbc_ref[...] = (dp * bv).astype(BF16)
        dbv_ref[...] = (dp * bc).astype(BF16)

    big = jax.ShapeDtypeStruct((t, BW), BF16)
    return pl.pallas_call(
        body, name=name, out_shape=(big, big, big, jax.ShapeDtypeStruct((CONV_B, BW), F32)), grid=(BW // LANE,),
        in_specs=[_chunk_spec(t, C_BV), _chunk_spec(t, C_BC), _chunk_spec(t, C_BB), _chunk_spec(t, 0),
                  _tap_spec(CONV_B)],
        out_specs=(_chunk_spec(t, 0),) * 3 + (_tap_spec(CONV_B),), compiler_params=_params(("parallel",)),
    )(proj, proj, proj, dyb, wB)


def _d_conv_fwd(proj, wD, bD, name):
    t = proj.shape[0]

    def body(d1_ref, d2_ref, w_ref, b_ref, o_ref):
        o_ref[...] = _conv_fwd(d1_ref[...] * _sig(d2_ref[...]), w_ref, CONV_D) + b_ref[...]

    return pl.pallas_call(
        body, name=name, out_shape=jax.ShapeDtypeStruct((t, BW), F32), grid=(BW // LANE,),
        in_specs=[_chunk_spec(t, C_D1), _chunk_spec(t, C_D2), _tap_spec(CONV_D), _cvec_spec()],
        out_specs=_chunk_spec(t, 0), compiler_params=_params(("parallel",)),
    )(proj, proj, wD, bD)


def _d_conv_bwd(proj, dcd, wD, name):
    t = proj.shape[0]

    def body(d1_ref, d2_ref, dy_ref, w_ref, dd1_ref, dd2_ref, dw_ref, db_ref):
        d1 = d1_ref[...]
        s = _sig(d2_ref[...])
        dy = dy_ref[...]
        db_ref[...] = jnp.sum(dy, axis=0, keepdims=True)
        dd = _conv_bwd(d1 * s, dy, w_ref, dw_ref, CONV_D)
        dd1_ref[...] = (dd * s).astype(BF16)
        dd2_ref[...] = (dd * d1 * s * (1.0 - s)).astype(BF16)

    big = jax.ShapeDtypeStruct((t, BW), BF16)
    return pl.pallas_call(
        body, name=name,
        out_shape=(big, big, jax.ShapeDtypeStruct((CONV_D, BW), F32), jax.ShapeDtypeStruct((1, BW), F32)),
        grid=(BW // LANE,),
        in_specs=[_chunk_spec(t, C_D1), _chunk_spec(t, C_D2), _chunk_spec(t, 0), _tap_spec(CONV_D)],
        out_specs=(_chunk_spec(t, 0), _chunk_spec(t, 0), _tap_spec(CONV_D), _cvec_spec()),
        compiler_params=_params(("parallel",)),
    )(proj, proj, dcd, wD)


SCALE = HEAD_DIM ** -0.5
GROUP = N_Q // N_KV


GROWS = GROUP * BLK


def _per_head(ss_ref, row, g):
    head = lax.broadcasted_iota(jnp.int32, (GROWS, 1), 0) // BLK
    col = jnp.full((GROWS, 1), ss_ref[row, g * GROUP + GROUP - 1], F32)
    for i in range(GROUP - 1):
        col = jnp.where(head == i, ss_ref[row, g * GROUP + i], col)
    return col


def _attn_probs(q_ref, k_ref, ss_ref, g, n):
    qi = lax.broadcasted_iota(jnp.int32, (GROWS, BLK), 0) % BLK
    ki = lax.broadcasted_iota(jnp.int32, (GROWS, BLK), 1)
    dist = (qi - ki).astype(F32)
    sink = _per_head(ss_ref, 0, g)
    slope = _per_head(ss_ref, 1, g)
    s0 = pl.multiple_of(n * BLK, BLK)
    sp = pl.multiple_of(jnp.maximum(n - 1, 0) * BLK, BLK)
    q = q_ref[:, pl.ds(s0, BLK), :].reshape(GROWS, HEAD_DIM)
    kc = k_ref[0, pl.ds(s0, BLK), :]
    kp = k_ref[0, pl.ds(sp, BLK), :]
    sc = jnp.where(ki <= qi, _dot(q, kc, NT) * SCALE - slope * dist, NEG_INF)
    first = jnp.where(n >= 1, 0, BLK)
    sv = jnp.where(ki > qi + first, _dot(q, kp, NT) * SCALE - slope * (dist + BLK), NEG_INF)
    m = jnp.maximum(jnp.maximum(jnp.max(sc, axis=1, keepdims=True), jnp.max(sv, axis=1, keepdims=True)), sink)
    pc = jnp.exp(sc - m)
    pp = jnp.exp(sv - m)
    ps = jnp.exp(sink - m)
    z = jnp.sum(pc, axis=1, keepdims=True) + jnp.sum(pp, axis=1, keepdims=True) + ps
    return s0, sp, q, kc, kp, pc, pp, ps, z


def _attn_specs(t):
    qs = pl.BlockSpec((GROUP, t, HEAD_DIM), lambda g: (g, 0, 0))
    ks = pl.BlockSpec((1, t, HEAD_DIM), lambda g: (g, 0, 0))
    ss = pl.BlockSpec(memory_space=pltpu.SMEM)
    return qs, ks, ss


def _attn_fwd(q, k, v, ss, name):
    t = q.shape[1]
    qs, ks, sspec = _attn_specs(t)

    def body(q_ref, k_ref, v_ref, ss_ref, o_ref):
        g = pl.program_id(0)

        def blk(n, carry):
            s0, sp, _, _, _, pc, pp, _, z = _attn_probs(q_ref, k_ref, ss_ref, g, n)
            o = _dot(pc, v_ref[0, pl.ds(s0, BLK), :], NN) + _dot(pp, v_ref[0, pl.ds(sp, BLK), :], NN)
            o_ref[:, pl.ds(s0, BLK), :] = (o / z).astype(BF16).reshape(GROUP, BLK, HEAD_DIM)
            return carry

        lax.fori_loop(0, t // BLK, blk, 0)

    return pl.pallas_call(
        body, name=name, out_shape=jax.ShapeDtypeStruct((N_Q, t, HEAD_DIM), BF16), grid=(N_KV,),
        in_specs=[qs, ks, ks, sspec], out_specs=qs, compiler_params=_params(("parallel",)),
    )(q, k, v, ss)


def _attn_bwd(q, k, v, do, ss, name):
    t = q.shape[1]
    qs, ks, sspec = _attn_specs(t)

    def body(q_ref, k_ref, v_ref, do_ref, ss_ref, dq_ref, dk_ref, dv_ref, ds_ref):
        g = pl.program_id(0)
        dk_ref[...] = jnp.zeros_like(dk_ref)
        dv_ref[...] = jnp.zeros_like(dv_ref)

        def blk(n, dsink):
            s0, sp, q, kc, kp, pc, pp, ps, z = _attn_probs(q_ref, k_ref, ss_ref, g, n)
            rz = 1.0 / z
            pc = pc * rz
            pp = pp * rz
            do_b = do_ref[:, pl.ds(s0, BLK), :].reshape(GROWS, HEAD_DIM)
            dpc = _dot(do_b, v_ref[0, pl.ds(s0, BLK), :], NT)
            dpp = _dot(do_b, v_ref[0, pl.ds(sp, BLK), :], NT)
            delta = jnp.sum(pc * dpc, axis=1, keepdims=True) + jnp.sum(pp * dpp, axis=1, keepdims=True)
            dsc = pc * (dpc - delta)
            dsp = pp * (dpp - delta)
            dq = (_dot(dsc, kc, NN) + _dot(dsp, kp, NN)) * SCALE
            dq_ref[:, pl.ds(s0, BLK), :] = dq.astype(BF16).reshape(GROUP, BLK, HEAD_DIM)
            dk_ref[0, pl.ds(s0, BLK), :] += _dot(dsc, q, TN) * SCALE
            dk_ref[0, pl.ds(sp, BLK), :] += _dot(dsp, q, TN) * SCALE
            dv_ref[0, pl.ds(s0, BLK), :] += _dot(pc, do_b, TN)
            dv_ref[0, pl.ds(sp, BLK), :] += _dot(pp, do_b, TN)
            return dsink - ps * rz * delta

        dsink = lax.fori_loop(0, t // BLK, blk, jnp.zeros((GROWS, 1), F32))
        for i in range(GROUP):
            ds_ref[i] = jnp.full(ds_ref.shape[1:], jnp.sum(dsink[i * BLK:(i + 1) * BLK]), F32)

    kv = jax.ShapeDtypeStruct((N_KV, t, HEAD_DIM), F32)
    return pl.pallas_call(
        body, name=name,
        out_shape=(jax.ShapeDtypeStruct((N_Q, t, HEAD_DIM), BF16), kv, kv, jax.ShapeDtypeStruct((N_Q, 8, LANE), F32)),
        grid=(N_KV,), in_specs=[qs, ks, ks, qs, sspec],
        out_specs=(qs, ks, ks, pl.BlockSpec((GROUP, 8, LANE), lambda g: (g, 0, 0))),
        compiler_params=_params(("parallel",)),
    )(q, k, v, do, ss)


def _heads(x2d, n):
    t = x2d.shape[0]
    return x2d.reshape(t, n, HEAD_DIM).transpose(1, 0, 2)


def _unheads(x3d):
    n, t, _ = x3d.shape
    return x3d.transpose(1, 0, 2).reshape(t, n * HEAD_DIM)


SMALL_ELEMS = 256 * 1024
TILE_ELEMS = 640 * 1024


def _row_tile(r, c):
    if r * c <= SMALL_ELEMS:
        return r
    return _pick(r, [t for t in (512, 256, 128, 64, 32, 16, 8) if t * c <= TILE_ELEMS])


def _adamw_update(w, gv, m, v):
    nm = ADAM_B1 * m + (1.0 - ADAM_B1) * gv
    nv = ADAM_B2 * v + (1.0 - ADAM_B2) * (gv * gv)
    m_hat = nm / (1.0 - ADAM_B1 ** ADAM_STEP)
    v_hat = nv / (1.0 - ADAM_B2 ** ADAM_STEP)
    return -ADAM_LR * (m_hat / (jnp.sqrt(v_hat) + ADAM_EPS) + ADAM_WD * w), nm, nv


def _adamw(w, g, m, v, name):
    r, c = w.shape
    tr = _row_tile(r, c)
    spec = pl.BlockSpec((tr, c), lambda i: (i, 0))

    def body(w_ref, g_ref, m_ref, v_ref, d_ref, nm_ref, nv_ref):
        d_ref[...], nm_ref[...], nv_ref[...] = _adamw_update(w_ref[...], g_ref[...], m_ref[...], v_ref[...])

    shp = jax.ShapeDtypeStruct((r, c), F32)
    return pl.pallas_call(
        body, name=name, out_shape=(shp, shp, shp), grid=(r // tr,), in_specs=[spec] * 4, out_specs=(spec,) * 3,
        compiler_params=_params(("parallel",)),
    )(w, g, m, v)


def _adamw_layer(w, m, v, g, l, prev, name):
    r, c = g.shape
    tr = _row_tile(r, c)
    layer = pl.BlockSpec((tr, c), lambda i: (l * (r // tr) + i, 0))

    def body(w_ref, m_ref, v_ref, g_ref, *rest):
        go_ref, d_ref, nm_ref, nv_ref, token = rest[-5:]
        gv = g_ref[...]
        go_ref[...] = gv
        token[...] = jnp.zeros_like(token)
        d_ref[...], nm_ref[...], nv_ref[...] = _adamw_update(w_ref[...], gv, m_ref[...], v_ref[...])

    carried = list(prev[:4]) if prev is not None else []
    shp = jax.ShapeDtypeStruct(w.shape, F32)
    return pl.pallas_call(
        body, name=name, out_shape=(shp,) * 4 + (jax.ShapeDtypeStruct((8, LANE), F32),), grid=(r // tr,),
        in_specs=[layer] * 3 + [pl.BlockSpec((tr, c), lambda i: (i, 0))] + [pl.BlockSpec(memory_space=pl.ANY)] * len(carried),
        out_specs=(layer,) * 4 + (pl.BlockSpec((8, LANE), lambda i: (0, 0)),),
        input_output_aliases={4 + i: i for i in range(len(carried))}, compiler_params=_params(("arbitrary",)),
    )(w, m, v, g, *carried)


def _sum_leading(x, name):
    n, r, c = x.shape
    tr = _row_tile(r, c)

    def body(x_ref, o_ref):
        acc = x_ref[0]
        for i in range(1, n):
            acc = acc + x_ref[i]
        o_ref[...] = acc

    return pl.pallas_call(
        body, name=name, out_shape=jax.ShapeDtypeStruct((r, c), F32), grid=(r // tr,),
        in_specs=[pl.BlockSpec((n, tr, c), lambda i: (0, i, 0))], out_specs=pl.BlockSpec((tr, c), lambda i: (i, 0)),
        compiler_params=_params(("parallel",)),
    )(x)


def _sum_own_plus(p, sel, recv, name, out_dtype):
    _, r, c = p.shape
    n = recv.shape[0]
    tr = _pick(r, (512, 448, 256, 128, 64, 16))

    def body(sel_ref, p_ref, r_ref, o_ref):
        acc = p_ref[0].astype(F32)
        for i in range(n):
            acc = acc + r_ref[i].astype(F32)
        o_ref[...] = acc.astype(out_dtype)

    grid_spec = pltpu.PrefetchScalarGridSpec(
        num_scalar_prefetch=1, grid=(r // tr,),
        in_specs=[pl.BlockSpec((1, tr, c), lambda i, s: (s[0], i, 0)), pl.BlockSpec((n, tr, c), lambda i, s: (0, i, 0))],
        out_specs=pl.BlockSpec((tr, c), lambda i, s: (i, 0)))
    return pl.pallas_call(
        body, name=name, out_shape=jax.ShapeDtypeStruct((r, c), out_dtype), grid_spec=grid_spec,
        compiler_params=_params(("parallel",)),
    )(sel, p, recv)


def _coords():
    return lax.axis_index("x"), lax.axis_index("y"), lax.axis_index("c")


def _allgather8(x2, name, space):
    _, m, n = x2.shape

    def body(x_ref, out_ref, send_sems, recv_sems, local_sem):
        x, y, c = _coords()
        me, sibling = (x, y, c), (x, y, 1 - c)
        chips = [(1 - x, y), (x, 1 - y), (1 - x, 1 - y)]
        mine_src = x_ref.at[c]

        def rows(px, py, pc):
            return out_ref.at[4 * px + 2 * py + pc]

        def copy(k, block, to, src=None):
            return pltpu.make_async_remote_copy(
                src_ref=rows(*block) if src is None else src, dst_ref=rows(*block),
                send_sem=send_sems.at[k], recv_sem=recv_sems.at[k], device_id=to, device_id_type=MESH)

        mine = pltpu.make_async_copy(mine_src, rows(*me), local_sem)
        mine.start()
        first = [copy(0, me, sibling, src=mine_src)]
        first += [copy(1 + j, me, (*chip, c), src=mine_src) for j, chip in enumerate(chips)]
        for cp in first:
            cp.start()
        passed = [copy(4 + j, (*chip, c), sibling) for j, chip in enumerate(chips)]
        for j, chip in enumerate(chips):
            copy(1 + j, (*chip, c), me).wait_recv()
            passed[j].start()
        copy(0, sibling, me).wait_recv()
        for j, chip in enumerate(chips):
            copy(4 + j, (*chip, 1 - c), me).wait_recv()
        for cp in first + passed:
            cp.wait_send()
        mine.wait()

    return pl.pallas_call(
        body, name=name, out_shape=jax.ShapeDtypeStruct((8, m, n), x2.dtype),
        in_specs=[pl.BlockSpec(memory_space=space)], out_specs=pl.BlockSpec(memory_space=space),
        scratch_shapes=[pltpu.SemaphoreType.DMA((7,)), pltpu.SemaphoreType.DMA((7,)), pltpu.SemaphoreType.DMA],
        compiler_params=pltpu.CompilerParams(vmem_limit_bytes=VMEM_LIMIT),
    )(x2)


N_REG = len(ROW_REGIONS) + 1


def _chip_window(ref, lead, r, j):
    view = ref if lead is None else ref.at[lead]
    if r < len(ROW_REGIONS):
        off, rows = ROW_REGIONS[r]
        return view.at[pl.ds(pl.multiple_of(off + j * rows, 16), rows), :]
    return view.at[pl.ds(R_OUT, OUT_ROWS), pl.ds(pl.multiple_of(j * OUT_COLS, LANE), OUT_COLS)]


HBM_SPEC = pl.BlockSpec(memory_space=pltpu.HBM)
SEM_SPEC = pl.BlockSpec(memory_space=pltpu.SEMAPHORE)


def _half_window(ref, r, j, h):
    if r < len(ROW_REGIONS):
        off, rows = ROW_REGIONS[r]
        return ref.at[pl.ds(pl.multiple_of(off + j * rows + h * (rows // 2), 16), rows // 2), :]
    half = OUT_ROWS // 2
    return ref.at[pl.ds(pl.multiple_of(R_OUT + h * half, 16), half),
                  pl.ds(pl.multiple_of(j * OUT_COLS, LANE), OUT_COLS)]


def _other_chips():
    x, y, _ = _coords()
    return [(1 - x, y), (x, 1 - y), (1 - x, 1 - y)]


def _ici_copies(srcs, arena_ref, send_sems, recv_sems, regions):
    x, y, c = _coords()
    sends, arrivals = [], []
    for k, (cx, cy) in enumerate(_other_chips()):
        for r in regions:
            def remote(src, j):
                return pltpu.make_async_remote_copy(
                    src_ref=src, dst_ref=_half_window(arena_ref, r, j, c), send_sem=send_sems.at[3 * r + k],
                    recv_sem=recv_sems.at[3 * r + k], device_id=(cx, cy, c), device_id_type=MESH)
            rows = srcs[r].shape[0] // 2
            sends.append(remote(srcs[r].at[pl.ds(pl.multiple_of(c * rows, 16), rows), :], 2 * x + y))
            arrivals.append(remote(_half_window(arena_ref, r, 2 * cx + cy, c), 2 * cx + cy))
    return sends, arrivals


def _sibling_copies(srcs, arena_ref, send_sems, recv_sems, regions):
    x, y, c = _coords()
    sends, arrivals = [], []

    def remote(win, r, k, src=None):
        return pltpu.make_async_remote_copy(
            src_ref=win if src is None else src, dst_ref=win, send_sem=send_sems.at[r, k],
            recv_sem=recv_sems.at[r, k], device_id=(x, y, 1 - c), device_id_type=MESH)

    for k, (cx, cy) in enumerate(_other_chips()):
        for r in regions:
            sends.append(remote(_half_window(arena_ref, r, 2 * cx + cy, c), r, k))
            arrivals.append(remote(_half_window(arena_ref, r, 2 * cx + cy, 1 - c), r, k))
    for r in regions:
        own = _chip_window(arena_ref, None, r, 2 * x + y)
        sends.append(remote(own, r, 3, src=srcs[r]))
        arrivals.append(remote(own, r, 3))
    return sends, arrivals


ICI_SEMS = pltpu.SemaphoreType.DMA((3 * N_REG,))
SIBLING_SEMS = pltpu.SemaphoreType.DMA((N_REG, 4))
ARENA_SHAPE = (ARENA_ROWS, ARENA_W)
ALL_REGIONS = tuple(range(N_REG))
IN_REGION = (3,)
REST_REGIONS = (0, 1, 2, 4, 5)


def _gather_layer(shards, name, regions=ALL_REGIONS):
    def body(*refs):
        srcs, arena_ref = refs[:N_REG], refs[N_REG]
        ici_send, ici_recv, sib_send, sib_recv = refs[N_REG + 1:]
        sends, arrivals = _ici_copies(srcs, arena_ref, ici_send, ici_recv, regions)
        passes, landings = _sibling_copies(srcs, arena_ref, sib_send, sib_recv, regions)
        for cp in sends + passes[len(arrivals):]:
            cp.start()
        for arrival, onward in zip(arrivals, passes):
            arrival.wait_recv()
            onward.start()
        for cp in landings:
            cp.wait_recv()
        for cp in sends + passes:
            cp.wait_send()

    return pl.pallas_call(
        body, name=name, out_shape=jax.ShapeDtypeStruct(ARENA_SHAPE, BF16),
        in_specs=[pl.BlockSpec(memory_space=pl.ANY)] * N_REG, out_specs=pl.BlockSpec(memory_space=pl.ANY),
        scratch_shapes=[ICI_SEMS, ICI_SEMS, SIBLING_SEMS, SIBLING_SEMS],
    )(*shards)


def _gather_start(shards, after, name, regions=ALL_REGIONS):
    def body(*refs):
        srcs, arena_ref = refs[:N_REG], refs[N_REG]
        send_sems, recv_sems = refs[N_REG + 2], refs[N_REG + 3]
        token = refs[-1]
        for cp in _ici_copies(srcs, arena_ref, send_sems, recv_sems, regions)[0]:
            cp.start()
        token[...] = jnp.zeros_like(token)

    hbm = lambda a: pltpu.with_memory_space_constraint(a, pltpu.HBM)
    outs = pl.pallas_call(
        body, name=name,
        out_shape=(ICI_SEMS, ICI_SEMS, *[pltpu.HBM(s.shape, s.dtype) for s in shards],
                   pltpu.HBM(ARENA_SHAPE, BF16), pltpu.HBM(after.shape, after.dtype),
                   jax.ShapeDtypeStruct((8, LANE), F32)),
        in_specs=[HBM_SPEC] * (N_REG + 2),
        out_specs=(SEM_SPEC, SEM_SPEC, *[HBM_SPEC] * (N_REG + 2), pl.BlockSpec(memory_space=pltpu.VMEM)),
        input_output_aliases={i: 2 + i for i in range(N_REG + 2)},
        compiler_params=pltpu.CompilerParams(has_side_effects=pltpu.SideEffectType.DATAFLOW_SIDE_EFFECTING),
    )(*[hbm(s) for s in shards], hbm(lax.empty(ARENA_SHAPE, BF16)), hbm(after))
    return outs[0], outs[1], outs[2:2 + N_REG], outs[2 + N_REG], outs[-1], outs[3 + N_REG]


def _gather_wait(send_sems, recv_sems, shards, arena, after, name, regions=ALL_REGIONS):
    def body(*refs):
        srcs, arena_ref = refs[:N_REG], refs[N_REG]
        sends, arrivals = _ici_copies(srcs, arena_ref, refs[N_REG + 1], refs[N_REG + 2], regions)
        for cp in sends:
            cp.wait_send()
        for cp in arrivals:
            cp.wait_recv()

    outs = pl.pallas_call(
        body, name=name,
        out_shape=(*[pltpu.HBM(s.shape, s.dtype) for s in shards], pltpu.HBM(ARENA_SHAPE, BF16)),
        in_specs=[HBM_SPEC] * (N_REG + 1) + [SEM_SPEC, SEM_SPEC, pl.BlockSpec(memory_space=pl.ANY)],
        out_specs=(HBM_SPEC,) * (N_REG + 1), input_output_aliases={i: i for i in range(N_REG + 1)},
        compiler_params=pltpu.CompilerParams(has_side_effects=pltpu.SideEffectType.DATAFLOW_SIDE_EFFECTING),
    )(*shards, arena, send_sems, recv_sems, after)
    return outs[:N_REG], outs[N_REG]


def _gather_finish(shards, arena, name, regions=ALL_REGIONS):
    def body(*refs):
        srcs, arena_ref = refs[:N_REG], refs[N_REG + 1]
        sends, arrivals = _sibling_copies(srcs, arena_ref, refs[N_REG + 2], refs[N_REG + 3], regions)
        for cp in sends:
            cp.start()
        for cp in arrivals:
            cp.wait_recv()
        for cp in sends:
            cp.wait_send()

    return pl.pallas_call(
        body, name=name, out_shape=jax.ShapeDtypeStruct(ARENA_SHAPE, BF16),
        in_specs=[pl.BlockSpec(memory_space=pl.ANY)] * (N_REG + 1), out_specs=pl.BlockSpec(memory_space=pl.ANY),
        scratch_shapes=[SIBLING_SEMS, SIBLING_SEMS], input_output_aliases={N_REG: 0},
    )(*shards, arena)


HALF_PIECE_OFF = tuple(o // 2 for o in PIECE_OFF)
HALF_PIECE_ROWS = PIECE_ROWS // 2
HALF_OUT_ROWS = OUT_ROWS // 2
SWAP_SEMS = pltpu.SemaphoreType.DMA((4 * N_REG,))
SCATTER_SEMS = pltpu.SemaphoreType.DMA((6,))


def _packed_shapes(slots, dtype):
    return (jax.ShapeDtypeStruct((slots, HALF_PIECE_ROWS, ARENA_W), dtype),
            jax.ShapeDtypeStruct((slots, HALF_OUT_ROWS, OUT_COLS), dtype))


def _swap_copies(g_ref, main_ref, outp_ref, send_sems, recv_sems):
    x, y, c = _coords()
    cps = []
    for j in range(4):
        for r in range(N_REG):
            if r < len(ROW_REGIONS):
                dst = main_ref.at[j, pl.ds(HALF_PIECE_OFF[r], ROW_REGIONS[r][1] // 2), :]
            else:
                dst = outp_ref.at[j]
            cps.append(pltpu.make_async_remote_copy(
                src_ref=_half_window(g_ref, r, j, 1 - c), dst_ref=dst, send_sem=send_sems.at[j * N_REG + r],
                recv_sem=recv_sems.at[j * N_REG + r], device_id=(x, y, 1 - c), device_id_type=MESH))
    return cps


def _swap_start(ga, name):
    def body(g_ref, main_ref, outp_ref, send_sems, recv_sems, *rest):
        for cp in _swap_copies(g_ref, main_ref, outp_ref, send_sems, recv_sems):
            cp.start()

    hbm = lambda a: pltpu.with_memory_space_constraint(a, pltpu.HBM)
    bufs = [ga] + [lax.empty(s.shape, s.dtype) for s in _packed_shapes(4, ga.dtype)]
    outs = pl.pallas_call(
        body, name=name, out_shape=(SWAP_SEMS, SWAP_SEMS, *[pltpu.HBM(b.shape, b.dtype) for b in bufs]),
        in_specs=[HBM_SPEC] * 3, out_specs=(SEM_SPEC, SEM_SPEC, *[HBM_SPEC] * 3),
        input_output_aliases={i: 2 + i for i in range(3)},
        compiler_params=pltpu.CompilerParams(has_side_effects=pltpu.SideEffectType.DATAFLOW_SIDE_EFFECTING),
    )(*[hbm(b) for b in bufs])
    return outs[0], outs[1], outs[2:]


def _swap_wait(send_sems, recv_sems, bufs, after, name):
    def body(g_ref, main_ref, outp_ref, send_sems, recv_sems, *rest):
        for cp in _swap_copies(g_ref, main_ref, outp_ref, send_sems, recv_sems):
            cp.wait_send()
            cp.wait_recv()

    return pl.pallas_call(
        body, name=name, out_shape=tuple(pltpu.HBM(b.shape, b.dtype) for b in bufs),
        in_specs=[HBM_SPEC] * 3 + [SEM_SPEC, SEM_SPEC, pl.BlockSpec(memory_space=pl.ANY)],
        out_specs=(HBM_SPEC,) * 3, input_output_aliases={i: i for i in range(3)},
        compiler_params=pltpu.CompilerParams(has_side_effects=pltpu.SideEffectType.DATAFLOW_SIDE_EFFECTING),
    )(*bufs, send_sems, recv_sems, after)


def _own_halves(ga, cc):
    mains = [jnp.concatenate([lax.dynamic_slice(ga, (off + j * rows + cc * (rows // 2), 0), (rows // 2, ARENA_W))
                              for off, rows in ROW_REGIONS]) for j in range(4)]
    outs = [lax.dynamic_slice(ga, (R_OUT + cc * HALF_OUT_ROWS, j * OUT_COLS), (HALF_OUT_ROWS, OUT_COLS))
            for j in range(4)]
    return jnp.stack(mains), jnp.stack(outs)


def _scatter_copies(main_ref, outp_ref, rmain_ref, routp_ref, send_sems, recv_sems):
    _, _, c = _coords()
    cps = []
    for k, (cx, cy) in enumerate(_other_chips()):
        for i, (src, dst) in enumerate(((main_ref, rmain_ref), (outp_ref, routp_ref))):
            cps.append(pltpu.make_async_remote_copy(
                src_ref=src.at[2 * cx + cy], dst_ref=dst.at[k], send_sem=send_sems.at[2 * k + i],
                recv_sem=recv_sems.at[2 * k + i], device_id=(cx, cy, c), device_id_type=MESH))
    return cps


def _scatter_start(main, outp, name):
    def body(main_ref, outp_ref, rmain_ref, routp_ref, send_sems, recv_sems, *rest):
        for cp in _scatter_copies(main_ref, outp_ref, rmain_ref, routp_ref, send_sems, recv_sems):
            cp.start()
        rest[-1][...] = jnp.zeros_like(rest[-1])

    hbm = lambda a: pltpu.with_memory_space_constraint(a, pltpu.HBM)
    land = [lax.empty(s.shape, s.dtype) for s in _packed_shapes(3, main.dtype)]
    bufs = [main, outp, *land]
    outs = pl.pallas_call(
        body, name=name,
        out_shape=(SCATTER_SEMS, SCATTER_SEMS, *[pltpu.HBM(b.shape, b.dtype) for b in bufs],
                   jax.ShapeDtypeStruct((8, LANE), F32)),
        in_specs=[HBM_SPEC] * 4, out_specs=(SEM_SPEC, SEM_SPEC, *[HBM_SPEC] * 4, pl.BlockSpec(memory_space=pltpu.VMEM)),
        input_output_aliases={i: 2 + i for i in range(4)},
        compiler_params=pltpu.CompilerParams(has_side_effects=pltpu.SideEffectType.DATAFLOW_SIDE_EFFECTING),
    )(*[hbm(b) for b in bufs])
    return outs[0], outs[1], outs[2:6], outs[6]


def _scatter_wait(send_sems, recv_sems, bufs, after, name):
    def body(main_ref, outp_ref, rmain_ref, routp_ref, send_sems, recv_sems, *rest):
        for cp in _scatter_copies(main_ref, outp_ref, rmain_ref, routp_ref, send_sems, recv_sems):
            cp.wait_send()
            cp.wait_recv()

    return pl.pallas_call(
        body, name=name, out_shape=tuple(pltpu.HBM(b.shape, b.dtype) for b in bufs),
        in_specs=[HBM_SPEC] * 4 + [SEM_SPEC, SEM_SPEC, pl.BlockSpec(memory_space=pl.ANY)],
        out_specs=(HBM_SPEC,) * 4, input_output_aliases={i: i for i in range(4)},
        compiler_params=pltpu.CompilerParams(has_side_effects=pltpu.SideEffectType.DATAFLOW_SIDE_EFFECTING),
    )(*bufs, send_sems, recv_sems, after)


def _swap_many(arrs, name):
    n = len(arrs)

    def body(*refs):
        x, y, c = _coords()
        send_sems, recv_sems = refs[2 * n], refs[2 * n + 1]
        cps = [pltpu.make_async_remote_copy(
            src_ref=refs[i], dst_ref=refs[n + i], send_sem=send_sems.at[i], recv_sem=recv_sems.at[i],
            device_id=(x, y, 1 - c), device_id_type=MESH) for i in range(n)]
        for cp in cps:
            cp.start()
        for cp in cps:
            cp.wait()

    return pl.pallas_call(
        body, name=name, out_shape=tuple(jax.ShapeDtypeStruct(a.shape, a.dtype) for a in arrs),
        in_specs=[pl.BlockSpec(memory_space=pl.ANY)] * n, out_specs=(pl.BlockSpec(memory_space=pl.ANY),) * n,
        scratch_shapes=[pltpu.SemaphoreType.DMA((n,)), pltpu.SemaphoreType.DMA((n,))],
    )(*arrs)


def _join_halves(mine, theirs, cc):
    return jnp.where(cc == 0, jnp.concatenate([mine, theirs]), jnp.concatenate([theirs, mine]))


def _reduced_layer(red, sib, cc):
    parts = []
    for (_, rows), off in zip(ROW_REGIONS, HALF_PIECE_OFF):
        parts.append(_join_halves(red[0][off:off + rows // 2], sib[0][off:off + rows // 2], cc))
    return jnp.concatenate(parts), _join_halves(red[1], sib[1], cc)


OUT_NAMES = ("w_a_out", "w_b_out", "w_c_out", "w_d_out")


def _arena_shards(w):
    t = lambda a: a.astype(BF16).transpose(0, 2, 1)
    return (w["w_ffn_down"].astype(BF16), t(w["w_ffn_gate"]), t(w["w_ffn_up"]), t(w["w_in"]), w["w_o"].astype(BF16),
            jnp.concatenate([w[n].astype(BF16) for n in OUT_NAMES], axis=1))


def _shard_grads(main, outp):
    t = lambda r: main[PIECE_OFF[r]:PIECE_OFF[r] + ROW_REGIONS[r][1]]
    g = dict(w_ffn_down=t(0), w_ffn_gate=t(1).T, w_ffn_up=t(2).T, w_in=t(3).T, w_o=t(4))
    for i, n in enumerate(OUT_NAMES):
        g[n] = outp[i * BW:(i + 1) * BW]
    return g


def _gather_taps(p, name):
    mine = jnp.concatenate([p[n] for n in CONV_NAMES], axis=1).reshape(DEPTH * N_TAPS, LANE)
    rows = -(-mine.shape[0] // 8) * 8
    mine = jnp.concatenate([mine, jnp.zeros((rows - mine.shape[0], LANE), F32)])
    g = _allgather8(jnp.stack([mine, mine]), name, pltpu.VMEM)[0::2, :DEPTH * N_TAPS]
    full = g.reshape(4, DEPTH, N_TAPS, LANE).transpose(1, 2, 0, 3).reshape(DEPTH, N_TAPS, BW)
    return dict(conv_a_w=full[:, :CONV_A], conv_b_w=full[:, CONV_A:CONV_A + CONV_B], conv_d_w=full[:, CONV_A + CONV_B:])


def _flat_pack(arrs):
    flat = jnp.concatenate([a.reshape(-1).astype(F32) for a in arrs])
    rows = -(-flat.shape[0] // (8 * LANE)) * 8
    return jnp.concatenate([flat, jnp.zeros((rows * LANE - flat.shape[0],), F32)]).reshape(rows, LANE)


def _flat_unpack(packed, shapes):
    flat, out, off = packed.reshape(-1), [], 0
    for s in shapes:
        cnt = int(np.prod(s))
        out.append(flat[off:off + cnt].reshape(s))
        off += cnt
    return out


def _blockdiag_chunks(w):
    w4 = w.reshape(4, 2, 64, 64)
    z = jnp.zeros((4, 2, 64, 2, 64), F32)
    z = z.at[:, 0, :, 0, :].set(w4[:, 0]).at[:, 1, :, 1, :].set(w4[:, 1])
    return z.reshape(4, LANE, LANE)


def _blockdiag_extract(d):
    d5 = d.reshape(4, 2, 64, 2, 64)
    return jnp.stack([d5[:, 0, :, 0, :], d5[:, 1, :, 1, :]], axis=1).reshape(8, 64, 64)


SLOPES = np.asarray([2.0 ** (-8.0 * (i + 1) / N_Q) for i in range(N_Q)], np.float32)


def _layer_consts(p, fw, l):
    row = lambda a: a[l].reshape(1, -1)
    return dict(
        g1=row(p["norm1_g"]), g2=row(p["norm2_g"]), wA=fw["conv_a_w"][l], bA=row(p["conv_a_b"]),
        wx=_blockdiag_chunks(p["lru_wx"][l]), bx=row(p["lru_bx"]), wa=_blockdiag_chunks(p["lru_wa"][l]),
        ba=row(p["lru_ba"]), lam=row(p["lru_lambda"]), wB=fw["conv_b_w"][l],
        ss=jnp.stack([p["sinks"][l], jnp.asarray(SLOPES)]), wD=fw["conv_d_w"][l], bD=row(p["conv_d_b"]),
        lg=row(p["ln_d_g"]), lb=row(p["ln_d_b"]))


def _layer_fwd(x, c, fw, l, rest_of_weights=None):
    t = f"l{l}_"
    xn = _rms_fwd(x, c["g1"], t + "rms1")
    wt = lambda off, rows: Win(fw["arena"][l], None, off, rows)
    proj = _mm(xn, Win(fw["arena_in"][l], None, R_IN, IN_W), "nt", t + "proj")
    ya, lru_h = _a_fwd(proj, c["wA"], c["bA"], c["wx"], c["bx"], c["wa"], c["ba"], c["lam"], t + "a_fwd")
    yb = _b_fwd(proj, c["wB"], t + "b_fwd")
    q3 = _heads(proj[:, OFF_Q:OFF_K], N_Q)
    k3 = _heads(proj[:, OFF_K:OFF_V], N_KV)
    v3 = _heads(proj[:, OFF_V:OFF_V + N_KV * HEAD_DIM], N_KV)
    yc = _unheads(_attn_fwd(q3, k3, v3, c["ss"], t + "attn_fwd"))
    cd = _d_conv_fwd(proj, c["wD"], c["bD"], t + "d_conv_fwd")
    yd = _ln_silu_fwd(cd, c["lg"], c["lb"], t + "d_ln_fwd")
    ys = (ya, yb, yc, yd)
    if fw["arena"][l] is None:
        fw["arena"][l] = rest_of_weights(yd)
    big_y = tuple(_mm(y, wt(R_OUT + i * BW, BW), "nn", t + f"out{i}") for i, y in enumerate(ys))
    merged = _merge_fwd(proj, big_y, t + "merge_fwd")
    hres = _mm(merged, wt(R_O, D_MODEL), "nn", t + "wo", add=x)
    hn = _rms_fwd(hres, c["g2"], t + "rms2")
    gg = _mm(hn, wt(R_GATE, D_FF), "nt", t + "ffn_gate")
    uu = _mm(hn, wt(R_UP, D_FF), "nt", t + "ffn_up")
    act = _swiglu_fwd(gg, uu, t + "swiglu_fwd")
    xout = _mm(act, wt(R_DOWN, D_FF), "nn", t + "ffn_down", add=hres)
    saved = dict(x=x, xn=xn, proj=proj, ys=ys, q3=q3, k3=k3, v3=v3, cd=cd, big_y=big_y, merged=merged, hres=hres,
                 hn=hn, gg=gg, uu=uu, act=act, lru_h=lru_h)
    return xout, saved


def _layer_bwd(dxout, s, c, fw, l, ga, weight_grads_done=None):
    t = f"l{l}_"
    gs = {}
    wt = lambda off, rows: Win(fw["arena"][l], None, off, rows)
    gt = lambda off, rows: Win(ga, None, off, rows)
    dact = _mm(dxout, wt(R_DOWN, D_FF), "nt", t + "d_act")
    ga = _mm(s["act"], dxout, "tn", t + "dw_down", out=gt(R_DOWN, D_FF))
    dgg, duu = _swiglu_bwd(s["gg"], s["uu"], dact, t + "swiglu_bwd")
    ga = _mm(dgg, s["hn"], "tn", t + "dw_gate", out=gt(R_GATE, D_FF))
    ga = _mm(duu, s["hn"], "tn", t + "dw_up", out=gt(R_UP, D_FF))
    dhn = _mm(dgg, wt(R_GATE, D_FF), "nn", t + "d_hn_g")
    dhn = _mm(duu, wt(R_UP, D_FF), "nn", t + "d_hn_u", add=dhn)
    dhres, dhres_bf, gs["norm2_g"] = _rms_bwd(s["hres"], c["g2"], dhn, dxout, t + "rms2_bwd")
    dmerged = _mm(dhres_bf, wt(R_O, D_MODEL), "nt", t + "d_merged")
    ga = _mm(s["merged"], dhres_bf, "tn", t + "dw_o", out=gt(R_O, D_MODEL))
    dbig_y, dgl = _merge_bwd(s["proj"], s["big_y"], dmerged, t + "merge_bwd")
    dys = []
    for i in range(4):
        ga = _mm(s["ys"][i], dbig_y[i], "tn", t + f"dw_out{i}", out=gt(R_OUT + i * BW, BW))
        dys.append(_mm(dbig_y[i], wt(R_OUT + i * BW, BW), "nt", t + f"d_y{i}"))
    proj = s["proj"]
    (dax, dag, gs["conv_a_w"], gs["conv_a_b"], dwx, gs["lru_bx"], dwa, gs["lru_ba"], gs["lru_lambda"]) = _a_bwd(
        proj, s["lru_h"], dys[0], c["wA"], c["bA"], c["wx"], c["bx"], c["wa"], c["ba"], c["lam"], t + "a_bwd")
    gs["lru_wx"] = _blockdiag_extract(dwx)
    gs["lru_wa"] = _blockdiag_extract(dwa)
    dbv, dbc, dbb, gs["conv_b_w"] = _b_bwd(proj, dys[1], c["wB"], t + "b_bwd")
    dq3, dk3, dv3, dsink = _attn_bwd(s["q3"], s["k3"], s["v3"], _heads(dys[2], N_Q), c["ss"], t + "attn_bwd")
    gs["sinks"] = dsink[:, 0, 0]
    dcd, gs["ln_d_g"], gs["ln_d_b"] = _ln_silu_bwd(s["cd"], c["lg"], c["lb"], dys[3], t + "d_ln_bwd")
    dd1, dd2, gs["conv_d_w"], gs["conv_d_b"] = _d_conv_bwd(proj, dcd, c["wD"], t + "d_conv_bwd")
    dproj = jnp.concatenate(
        [dax, dag, dbv, dbc, dbb, _unheads(dq3), _unheads(dk3).astype(BF16), _unheads(dv3).astype(BF16), dd1, dd2,
         *dgl], axis=1)
    ga = _mm(dproj, s["xn"], "tn", t + "dw_in", out=gt(R_IN, IN_W))
    token = weight_grads_done(ga) if weight_grads_done is not None else None
    dxn = _mm(dproj, Win(fw["arena_in"][l], None, R_IN, IN_W), "nn", t + "d_xn", after=token)
    dx, _, gs["norm1_g"] = _rms_bwd(s["x"], c["g1"], dxn, dhres, t + "rms1_bwd")
    return dx, ga, gs


def kernel(x, norm1_g, w_in, conv_a_w, conv_a_b, lru_wx, lru_bx, lru_wa, lru_ba, lru_lambda, w_a_out, conv_b_w, w_b_out, sinks, w_c_out, conv_d_w, conv_d_b, ln_d_g, ln_d_b, w_d_out, w_o, norm2_g, w_ffn_gate, w_ffn_up, w_ffn_down, final_g, loss_target, m_norm1_g, m_w_in, m_conv_a_w, m_conv_a_b, m_lru_wx, m_lru_bx, m_lru_wa, m_lru_ba, m_lru_lambda, m_w_a_out, m_conv_b_w, m_w_b_out, m_sinks, m_w_c_out, m_conv_d_w, m_conv_d_b, m_ln_d_g, m_ln_d_b, m_w_d_out, m_w_o, m_norm2_g, m_w_ffn_gate, m_w_ffn_up, m_w_ffn_down, m_final_g, v_norm1_g, v_w_in, v_conv_a_w, v_conv_a_b, v_lru_wx, v_lru_bx, v_lru_wa, v_lru_ba, v_lru_lambda, v_w_a_out, v_conv_b_w, v_w_b_out, v_sinks, v_w_c_out, v_conv_d_w, v_conv_d_b, v_ln_d_g, v_ln_d_b, v_w_d_out, v_w_o, v_norm2_g, v_w_ffn_gate, v_w_ffn_up, v_w_ffn_down, v_final_g):
    given = dict(locals())
    p = {n: given[n] for n in NAMES}
    mom = {n: given["m_" + n] for n in NAMES}
    var = {n: given["v_" + n] for n in NAMES}
    cx, cy, cc = _coords()
    chip = 2 * cx + cy

    shards = _arena_shards(p)
    fw = _gather_taps(p, "gather_taps")
    shards0, shards1 = [s[0] for s in shards], [s[1] for s in shards]
    flight0 = _gather_start(shards0, _gather_layer(shards0, "gather_l0_in", IN_REGION), "gather_l0_rest_start",
                            REST_REGIONS)
    fw["arena_in"] = [flight0[5], None]
    fw["arena"] = [None, None]
    consts = [_layer_consts(p, fw, l) for l in range(DEPTH)]
    consts[0]["g1"] = consts[0]["g1"] + flight0[4][0:1, 0:1]
    flight1 = []

    def rest_of_layer0(after):
        sh, landing = _gather_wait(*flight0[:4], after, "gather_l0_rest_wait", REST_REGIONS)
        arena = _gather_finish(sh, landing, "gather_l0_rest_finish", REST_REGIONS)
        flight1.extend(_gather_start(shards1, arena, "gather_l1_start"))
        return flight1[5]

    h = x[0]
    saved = []
    for l in range(DEPTH):
        if l == 1:
            sh, landing = _gather_wait(*flight1[:4], h, "gather_l1_wait")
            fw["arena"][1] = fw["arena_in"][1] = _gather_finish(sh, landing, "gather_l1_finish")
        h, s = _layer_fwd(h, consts[l], fw, l, rest_of_layer0)
        saved.append(s)
    loss_vec, dh, g_final = _loss_head(h, final_g.reshape(1, -1), loss_target[0], "loss_head")
    loss = lax.psum(loss_vec[0, 0], ("x", "y", "c"))

    zero = jnp.zeros((1,), jnp.int32)
    chip_sel = chip.reshape(1).astype(jnp.int32)

    def chip_sums(ga, t):
        ss, rs, flying = _swap_start(ga, t + "grads_swap_start")
        own = _own_halves(flying[0], cc)
        got = _swap_wait(ss, rs, flying, own[0], t + "grads_swap_wait")[1:]
        return [_sum_own_plus(o.reshape((1, -1, o.shape[-1])), zero, r.reshape((1, -1, r.shape[-1])),
                              t + f"grads_sum_chip{i}", BF16).reshape(o.shape) for i, (o, r) in enumerate(zip(own, got))]

    def all_sums(sums, got, t):
        return [_sum_own_plus(s, chip_sel, r, t + f"grads_sum_all{i}", F32) for i, (s, r) in enumerate(zip(sums, got))]

    gss = [None] * DEPTH
    dh, ga1, gss[1] = _layer_bwd(dh, saved[1], consts[1], fw, 1, lax.empty(ARENA_SHAPE, BF16))
    send_sems, recv_sems, bufs, token = _scatter_start(*chip_sums(ga1, "l1_"), "l1_grads_scatter_start")
    scatter0 = []

    def start_layer0_scatter(ga0):
        scatter0.extend(_scatter_start(*chip_sums(ga0, "l0_"), "l0_grads_scatter_start"))
        return scatter0[3]

    dh, _, gss[0] = _layer_bwd(dh + token[0:1, 0:1], saved[0], consts[0], fw, 0, lax.empty(ARENA_SHAPE, BF16),
                               start_layer0_scatter)
    grad_x = dh[None]

    flat = lambda a: a.reshape(-1, a.shape[-1])

    def finish_layer(bufs_l, l, carried, t):
        red = all_sums(bufs_l[:2], bufs_l[2:], t)
        g_l = _shard_grads(*_reduced_layer(red, _swap_many(red, t + "grads_swap_reduced"), cc))
        return {n: _adamw_layer(flat(p[n]), flat(mom[n]), flat(var[n]), g_l[n], l, carried and carried[n],
                                t + "adamw_" + n) for n in BIG}

    big = finish_layer(_scatter_wait(send_sems, recv_sems, bufs, scatter0[3], "l1_grads_scatter_wait"), 1, None, "l1_")
    done1 = big["w_in"][4][0:1, 0:1]

    small_full = {n: (g_final.reshape(-1) if n == "final_g" else
                      jnp.stack([gss[l][n].reshape(gss[l][n].shape[-2:] if n.startswith("conv") and n.endswith("_w")
                                                   else p[n].shape[1:]) for l in range(DEPTH)]))
                  for n in SMALL}
    part = _flat_pack([small_full[n] for n in SMALL]) + done1
    rows = part.shape[0]
    gathered = _allgather8(jnp.stack([part, part]), "gather_small_grads", pltpu.VMEM)
    small_packed = _sum_leading(gathered, "small_grads_sum")
    small_sum = _flat_unpack(small_packed, [small_full[n].shape for n in SMALL])

    big = finish_layer(_scatter_wait(*scatter0[:3], small_packed, "l0_grads_scatter_wait"), 0, big, "l0_")
    g, delta, new_m, new_v = ({n: big[n][i].reshape(p[n].shape) for n in BIG} for i in range(4))
    for n, a in zip(SMALL, small_sum):
        g[n] = lax.dynamic_slice_in_dim(a, chip * LANE, LANE, axis=2) if n in CONV_NAMES else a

    shapes = [p[n].shape for n in SMALL]
    d, nm, nv = _adamw(_flat_pack([p[n] for n in SMALL]), _flat_pack([g[n] for n in SMALL]),
                       _flat_pack([mom[n] for n in SMALL]), _flat_pack([var[n] for n in SMALL]), "adamw_small")
    for n, a, b, cval in zip(SMALL, _flat_unpack(d, shapes), _flat_unpack(nm, shapes), _flat_unpack(nv, shapes)):
        delta[n], new_m[n], new_v[n] = a, b, cval

    return (loss, grad_x, *[g[n] for n in NAMES], *[delta[n] for n in NAMES], *[new_m[n] for n in NAMES],
            *[new_v[n] for n in NAMES])
```
